```python
import math
import jax, jax.numpy as jnp
from jax import lax
import numpy as np

D_MODEL = 2048
BATCH = 8
SEQ = 2048
DEPTH = 2

N_MIXERS = 2
N_LAYERS_A = (DEPTH + 1) // 2
N_LAYERS_B = DEPTH // 2
NORM_EPS = 1e-6
PLE_DIM = 256
D_FF = -(-8 * D_MODEL // (3 * 256)) * 256
LRU_WIDTH = D_MODEL
LRU_HEADS = 8
LRU_BLOCK = LRU_WIDTH // LRU_HEADS
CONV_WIDTH = 4
LRU_C = 8.0
SSD_EXPAND = 2
SSD_INNER = SSD_EXPAND * D_MODEL
SSD_HEAD_DIM = 64
SSD_HEADS = SSD_INNER // SSD_HEAD_DIM
SSD_GROUPS = 8
SSD_HEADS_PER_GROUP = SSD_HEADS // SSD_GROUPS
SSD_STATE = 128
SSD_CHUNK = 128
SSD_CONV_DIM = SSD_INNER + 2 * SSD_GROUPS * SSD_STATE
SSD_IN_DIM = SSD_INNER + SSD_CONV_DIM + SSD_HEADS
SSD_NORM_GROUP = SSD_INNER // SSD_GROUPS

kernel_name = "hybrid_rglru_ssd_swiglu_ple"


def rmsnorm(x, g):
    xf = x.astype(jnp.float32)
    y = xf * lax.rsqrt(jnp.mean(xf * xf, axis=-1, keepdims=True) + NORM_EPS)
    return (y * g.astype(jnp.float32)).astype(x.dtype)


def causal_dwconv(x, w, b):
    W = w.shape[0]
    S = x.shape[1]
    xp = jnp.pad(x, ((0, 0), (W - 1, 0), (0, 0)))
    y = b + xp[:, 0:S] * w[0]
    for k in range(1, W):
        y = y + xp[:, k:k + S] * w[k]
    return y


def _lin_combine(c1, c2):
    a1, b1 = c1
    a2, b2 = c2
    return (a1 * a2, a2 * b1 + b2)


def rglru_mixer(u, w_in, conv_w, conv_b, w_gate_r, b_gate_r, w_gate_i, b_gate_i, lam, w_out):
    bsz, S, _ = u.shape
    xy = u @ w_in
    xr, yg = jnp.split(xy, 2, axis=-1)
    yg = jax.nn.gelu(yg, approximate=True)
    xr = causal_dwconv(xr, conv_w, conv_b)
    xb = xr.reshape(bsz, S, LRU_HEADS, LRU_BLOCK)
    r = jax.nn.sigmoid(jnp.einsum('bshi,hij->bshj', xb, w_gate_r) + b_gate_r).reshape(bsz, S, LRU_WIDTH)
    i = jax.nn.sigmoid(jnp.einsum('bshi,hij->bshj', xb, w_gate_i) + b_gate_i).reshape(bsz, S, LRU_WIDTH)
    log_a = -LRU_C * r.astype(jnp.float32) * jax.nn.softplus(-lam.astype(jnp.float32))
    a = jnp.exp(log_a)
    mult = jnp.sqrt(-jnp.expm1(2.0 * log_a))
    bterm = mult * (i * xr).astype(jnp.float32)
    _, hs = lax.associative_scan(_lin_combine, (a, bterm), axis=1)
    y = hs.astype(u.dtype) * yg
    return y @ w_out


def segsum(x):
    T = x.shape[-1]
    cs = jnp.cumsum(x, axis=-1)
    diff = cs[..., :, None] - cs[..., None, :]
    mask = jnp.tril(jnp.ones((T, T), dtype=bool))
    return jnp.where(mask, diff, -jnp.inf)


def ssd_scan(x, dt, A, Bm, Cm):
    b, S, H, P = x.shape
    L = SSD_CHUNK
    nc = S // L
    G, E, N = SSD_GROUPS, SSD_HEADS_PER_GROUP, SSD_STATE
    X = (x * dt[..., None]).reshape(b, nc, L, G, E, P)
    Adt = (A * dt).reshape(b, nc, L, G, E).transpose(0, 3, 4, 1, 2)
    Bc = Bm.reshape(b, nc, L, G, N)
    Cc = Cm.reshape(b, nc, L, G, N)
    A_cs = jnp.cumsum(Adt, axis=-1)
    Lmat = jnp.exp(segsum(Adt))
    CB = jnp.einsum('bclgn,bcsgn->bgcls', Cc, Bc)
    scores = CB[:, :, None] * Lmat
    y_diag = jnp.einsum('bgecls,bcsgep->bclgep', scores, X)
    decay_states = jnp.exp(A_cs[..., -1:] - A_cs)
    states = jnp.einsum('bclgn,bgecl,bclgep->bcgepn', Bc, decay_states, X)
    states = jnp.concatenate([jnp.zeros_like(states[:, :1]), states], axis=1)
    chunk_tot = jnp.pad(A_cs[..., -1], ((0, 0), (0, 0), (0, 0), (1, 0)))
    decay_chunk = jnp.exp(segsum(chunk_tot))
    states_in = jnp.einsum('bgezc,bcgepn->bzgepn', decay_chunk, states)[:, :-1]
    y_off = jnp.einsum('bclgn,bcgepn,bgecl->bclgep', Cc, states_in, jnp.exp(A_cs))
    return (y_diag + y_off).reshape(b, S, H, P)


def ssd_mixer(u, w_in, conv_w, conv_b, dt_bias, a_log, d_skip, norm_g, w_out):
    b, S, _ = u.shape
    G, N = SSD_GROUPS, SSD_STATE
    zxbcdt = u @ w_in
    z = zxbcdt[..., :SSD_INNER]
    xbc = zxbcdt[..., SSD_INNER:SSD_INNER + SSD_CONV_DIM]
    dt = zxbcdt[..., SSD_INNER + SSD_CONV_DIM:]
    xbc = jax.nn.silu(causal_dwconv(xbc, conv_w, conv_b))
    xs = xbc[..., :SSD_INNER].reshape(b, S, SSD_HEADS, SSD_HEAD_DIM).astype(jnp.float32)
    Bm = xbc[..., SSD_INNER:SSD_INNER + G * N].reshape(b, S, G, N).astype(jnp.float32)
    Cm = xbc[..., SSD_INNER + G * N:].reshape(b, S, G, N).astype(jnp.float32)
    dt = jax.nn.softplus(dt.astype(jnp.float32) + dt_bias.astype(jnp.float32))
    A = -jnp.exp(a_log.astype(jnp.float32))
    y = ssd_scan(xs, dt, A, Bm, Cm)
    y = y + d_skip.astype(jnp.float32)[:, None] * xs
    y = y.reshape(b, S, SSD_INNER) * jax.nn.silu(z.astype(jnp.float32))
    yg = y.reshape(b, S, SSD_GROUPS, SSD_NORM_GROUP)
    yg = yg * lax.rsqrt(jnp.mean(yg * yg, axis=-1, keepdims=True) + NORM_EPS)
    y = (yg.reshape(b, S, SSD_INNER) * norm_g.astype(jnp.float32)).astype(u.dtype)
    return y @ w_out


def swiglu(u, w_gate, w_up, w_down):
    return (jax.nn.silu(u @ w_gate) * (u @ w_up)) @ w_down


def _fwd_setup_inputs(seed: int = 0) -> dict:
    key = jax.random.key(seed)
    ks = jax.random.split(key, 32)
    f32 = jnp.float32
    nrm = lambda k, shape, scale: jax.random.normal(k, shape, f32) * scale
    gain = lambda k, shape: 1.0 + 0.02 * jax.random.normal(k, shape, f32)
    x = jax.random.normal(ks[0], (BATCH, SEQ, D_MODEL), f32)
    p = jax.random.normal(ks[1], (DEPTH, BATCH, SEQ, PLE_DIM), f32)
    norm_mix_g = gain(ks[2], (DEPTH, D_MODEL))
    norm_ffn_g = gain(ks[3], (DEPTH, D_MODEL))
    norm_ple_g = gain(ks[4], (DEPTH, D_MODEL))
    final_norm_g = gain(ks[5], (D_MODEL,))
    a_w_in = nrm(ks[6], (N_LAYERS_A, D_MODEL, 2 * LRU_WIDTH), D_MODEL ** -0.5)
    a_conv_w = nrm(ks[7], (N_LAYERS_A, CONV_WIDTH, LRU_WIDTH), CONV_WIDTH ** -0.5)
    a_conv_b = nrm(ks[8], (N_LAYERS_A, LRU_WIDTH), 0.01)
    a_w_gate_r = nrm(ks[9], (N_LAYERS_A, LRU_HEADS, LRU_BLOCK, LRU_BLOCK), LRU_BLOCK ** -0.5)
    a_b_gate_r = nrm(ks[10], (N_LAYERS_A, LRU_HEADS, LRU_BLOCK), 0.01)
    a_w_gate_i = nrm(ks[11], (N_LAYERS_A, LRU_HEADS, LRU_BLOCK, LRU_BLOCK), LRU_BLOCK ** -0.5)
    a_b_gate_i = nrm(ks[12], (N_LAYERS_A, LRU_HEADS, LRU_BLOCK), 0.01)
    u_a = jax.random.uniform(ks[13], (N_LAYERS_A, LRU_WIDTH), f32, 0.9, 0.999)
    s_a = u_a ** (1.0 / LRU_C)
    a_lambda = jnp.log(s_a) - jnp.log1p(-s_a)
    a_w_out = nrm(ks[14], (N_LAYERS_A, LRU_WIDTH, D_MODEL), LRU_WIDTH ** -0.5)
    b_w_in = nrm(ks[15], (N_LAYERS_B, D_MODEL, SSD_IN_DIM), D_MODEL ** -0.5)
    b_conv_w = nrm(ks[16], (N_LAYERS_B, CONV_WIDTH, SSD_CONV_DIM), CONV_WIDTH ** -0.5)
    b_conv_b = nrm(ks[17], (N_LAYERS_B, SSD_CONV_DIM), 0.01)
    dt0 = jnp.exp(jax.random.uniform(ks[18], (N_LAYERS_B, SSD_HEADS), f32, math.log(1e-3), math.log(1e-1)))
    b_dt_bias = dt0 + jnp.log(-jnp.expm1(-dt0))
    b_a_log = jnp.log(jax.random.uniform(ks[19], (N_LAYERS_B, SSD_HEADS), f32, 1.0, 16.0))
    b_d_skip = 1.0 + 0.1 * jax.random.normal(ks[20], (N_LAYERS_B, SSD_HEADS), f32)
    b_norm_g = gain(ks[21], (N_LAYERS_B, SSD_INNER))
    b_w_out = nrm(ks[22], (N_LAYERS_B, SSD_INNER, D_MODEL), SSD_INNER ** -0.5)
    ffn_w_gate = nrm(ks[23], (DEPTH, D_MODEL, D_FF), D_MODEL ** -0.5)
    ffn_w_up = nrm(ks[24], (DEPTH, D_MODEL, D_FF), D_MODEL ** -0.5)
    ffn_w_down = nrm(ks[25], (DEPTH, D_FF, D_MODEL), D_FF ** -0.5)
    ple_w_proj = nrm(ks[26], (DEPTH, PLE_DIM, D_MODEL), PLE_DIM ** -0.5)
    ple_w_gate = nrm(ks[27], (DEPTH, D_MODEL, D_MODEL), D_MODEL ** -0.5)
    return {"x": x, "p": p, "norm_mix_g": norm_mix_g, "norm_ffn_g": norm_ffn_g,
            "norm_ple_g": norm_ple_g, "final_norm_g": final_norm_g,
            "a_w_in": a_w_in, "a_conv_w": a_conv_w, "a_conv_b": a_conv_b,
            "a_w_gate_r": a_w_gate_r, "a_b_gate_r": a_b_gate_r,
            "a_w_gate_i": a_w_gate_i, "a_b_gate_i": a_b_gate_i,
            "a_lambda": a_lambda, "a_w_out": a_w_out,
            "b_w_in": b_w_in, "b_conv_w": b_conv_w, "b_conv_b": b_conv_b,
            "b_dt_bias": b_dt_bias, "b_a_log": b_a_log, "b_d_skip": b_d_skip,
            "b_norm_g": b_norm_g, "b_w_out": b_w_out,
            "ffn_w_gate": ffn_w_gate, "ffn_w_up": ffn_w_up, "ffn_w_down": ffn_w_down,
            "ple_w_proj": ple_w_proj, "ple_w_gate": ple_w_gate}


def _fwd_reference(x, p, norm_mix_g, norm_ffn_g, norm_ple_g, final_norm_g,
              a_w_in, a_conv_w, a_conv_b, a_w_gate_r, a_b_gate_r, a_w_gate_i, a_b_gate_i,
              a_lambda, a_w_out,
              b_w_in, b_conv_w, b_conv_b, b_dt_bias, b_a_log, b_d_skip, b_norm_g, b_w_out,
              ffn_w_gate, ffn_w_up, ffn_w_down, ple_w_proj, ple_w_gate):
    h = x
    for i in range(DEPTH):
        u = rmsnorm(h, norm_mix_g[i])
        j = i // N_MIXERS
        if i % N_MIXERS == 0:
            m = rglru_mixer(u, a_w_in[j], a_conv_w[j], a_conv_b[j], a_w_gate_r[j], a_b_gate_r[j],
                            a_w_gate_i[j], a_b_gate_i[j], a_lambda[j], a_w_out[j])
        else:
            m = ssd_mixer(u, b_w_in[j], b_conv_w[j], b_conv_b[j], b_dt_bias[j], b_a_log[j],
                          b_d_skip[j], b_norm_g[j], b_w_out[j])
        h = h + m
        h = h + swiglu(rmsnorm(h, norm_ffn_g[i]), ffn_w_gate[i], ffn_w_up[i], ffn_w_down[i])
        gate = jax.nn.sigmoid(rmsnorm(h, norm_ple_g[i]) @ ple_w_gate[i])
        h = h + gate * (p[i].astype(h.dtype) @ ple_w_proj[i])
    return rmsnorm(h, final_norm_g)


import jax as _jax
import jax.numpy as _jnp

TWIN_FORMAT = 'train_step'
FWD_PARAMS = ['x', 'p', 'norm_mix_g', 'norm_ffn_g', 'norm_ple_g', 'final_norm_g', 'a_w_in', 'a_conv_w', 'a_conv_b', 'a_w_gate_r', 'a_b_gate_r', 'a_w_gate_i', 'a_b_gate_i', 'a_lambda', 'a_w_out', 'b_w_in', 'b_conv_w', 'b_conv_b', 'b_dt_bias', 'b_a_log', 'b_d_skip', 'b_norm_g', 'b_w_out', 'ffn_w_gate', 'ffn_w_up', 'ffn_w_down', 'ple_w_proj', 'ple_w_gate']
TWIN_WEIGHTS = ['norm_mix_g', 'norm_ffn_g', 'norm_ple_g', 'final_norm_g', 'a_w_in', 'a_conv_w', 'a_conv_b', 'a_w_gate_r', 'a_b_gate_r', 'a_w_gate_i', 'a_b_gate_i', 'a_lambda', 'a_w_out', 'b_w_in', 'b_conv_w', 'b_conv_b', 'b_dt_bias', 'b_a_log', 'b_d_skip', 'b_norm_g', 'b_w_out', 'ffn_w_gate', 'ffn_w_up', 'ffn_w_down', 'ple_w_proj', 'ple_w_gate']
TWIN_DIFF_INPUT = 'x'
TWIN_INPUTS = ['x', 'p', 'norm_mix_g', 'norm_ffn_g', 'norm_ple_g', 'final_norm_g', 'a_w_in', 'a_conv_w', 'a_conv_b', 'a_w_gate_r', 'a_b_gate_r', 'a_w_gate_i', 'a_b_gate_i', 'a_lambda', 'a_w_out', 'b_w_in', 'b_conv_w', 'b_conv_b', 'b_dt_bias', 'b_a_log', 'b_d_skip', 'b_norm_g', 'b_w_out', 'ffn_w_gate', 'ffn_w_up', 'ffn_w_down', 'ple_w_proj', 'ple_w_gate', 'loss_target', 'm_norm_mix_g', 'm_norm_ffn_g', 'm_norm_ple_g', 'm_final_norm_g', 'm_a_w_in', 'm_a_conv_w', 'm_a_conv_b', 'm_a_w_gate_r', 'm_a_b_gate_r', 'm_a_w_gate_i', 'm_a_b_gate_i', 'm_a_lambda', 'm_a_w_out', 'm_b_w_in', 'm_b_conv_w', 'm_b_conv_b', 'm_b_dt_bias', 'm_b_a_log', 'm_b_d_skip', 'm_b_norm_g', 'm_b_w_out', 'm_ffn_w_gate', 'm_ffn_w_up', 'm_ffn_w_down', 'm_ple_w_proj', 'm_ple_w_gate', 'v_norm_mix_g', 'v_norm_ffn_g', 'v_norm_ple_g', 'v_final_norm_g', 'v_a_w_in', 'v_a_conv_w', 'v_a_conv_b', 'v_a_w_gate_r', 'v_a_b_gate_r', 'v_a_w_gate_i', 'v_a_b_gate_i', 'v_a_lambda', 'v_a_w_out', 'v_b_w_in', 'v_b_conv_w', 'v_b_conv_b', 'v_b_dt_bias', 'v_b_a_log', 'v_b_d_skip', 'v_b_norm_g', 'v_b_w_out', 'v_ffn_w_gate', 'v_ffn_w_up', 'v_ffn_w_down', 'v_ple_w_proj', 'v_ple_w_gate']
TWIN_OUTPUTS = ['loss', 'grad_x', 'grad_norm_mix_g', 'grad_norm_ffn_g', 'grad_norm_ple_g', 'grad_final_norm_g', 'grad_a_w_in', 'grad_a_conv_w', 'grad_a_conv_b', 'grad_a_w_gate_r', 'grad_a_b_gate_r', 'grad_a_w_gate_i', 'grad_a_b_gate_i', 'grad_a_lambda', 'grad_a_w_out', 'grad_b_w_in', 'grad_b_conv_w', 'grad_b_conv_b', 'grad_b_dt_bias', 'grad_b_a_log', 'grad_b_d_skip', 'grad_b_norm_g', 'grad_b_w_out', 'grad_ffn_w_gate', 'grad_ffn_w_up', 'grad_ffn_w_down', 'grad_ple_w_proj', 'grad_ple_w_gate', 'delta_norm_mix_g', 'delta_norm_ffn_g', 'delta_norm_ple_g', 'delta_final_norm_g', 'delta_a_w_in', 'delta_a_conv_w', 'delta_a_conv_b', 'delta_a_w_gate_r', 'delta_a_b_gate_r', 'delta_a_w_gate_i', 'delta_a_b_gate_i', 'delta_a_lambda', 'delta_a_w_out', 'delta_b_w_in', 'delta_b_conv_w', 'delta_b_conv_b', 'delta_b_dt_bias', 'delta_b_a_log', 'delta_b_d_skip', 'delta_b_norm_g', 'delta_b_w_out', 'delta_ffn_w_gate', 'delta_ffn_w_up', 'delta_ffn_w_down', 'delta_ple_w_proj', 'delta_ple_w_gate', 'new_m_norm_mix_g', 'new_m_norm_ffn_g', 'new_m_norm_ple_g', 'new_m_final_norm_g', 'new_m_a_w_in', 'new_m_a_conv_w', 'new_m_a_conv_b', 'new_m_a_w_gate_r', 'new_m_a_b_gate_r', 'new_m_a_w_gate_i', 'new_m_a_b_gate_i', 'new_m_a_lambda', 'new_m_a_w_out', 'new_m_b_w_in', 'new_m_b_conv_w', 'new_m_b_conv_b', 'new_m_b_dt_bias', 'new_m_b_a_log', 'new_m_b_d_skip', 'new_m_b_norm_g', 'new_m_b_w_out', 'new_m_ffn_w_gate', 'new_m_ffn_w_up', 'new_m_ffn_w_down', 'new_m_ple_w_proj', 'new_m_ple_w_gate', 'new_v_norm_mix_g', 'new_v_norm_ffn_g', 'new_v_norm_ple_g', 'new_v_final_norm_g', 'new_v_a_w_in', 'new_v_a_conv_w', 'new_v_a_conv_b', 'new_v_a_w_gate_r', 'new_v_a_b_gate_r', 'new_v_a_w_gate_i', 'new_v_a_b_gate_i', 'new_v_a_lambda', 'new_v_a_w_out', 'new_v_b_w_in', 'new_v_b_conv_w', 'new_v_b_conv_b', 'new_v_b_dt_bias', 'new_v_b_a_log', 'new_v_b_d_skip', 'new_v_b_norm_g', 'new_v_b_w_out', 'new_v_ffn_w_gate', 'new_v_ffn_w_up', 'new_v_ffn_w_down', 'new_v_ple_w_proj', 'new_v_ple_w_gate']
TWIN_LEAF_KINDS = {'loss': 'loss', 'grad_x': 'grad_x', 'grad_norm_mix_g': 'grad_w', 'grad_norm_ffn_g': 'grad_w', 'grad_norm_ple_g': 'grad_w', 'grad_final_norm_g': 'grad_w', 'grad_a_w_in': 'grad_w', 'grad_a_conv_w': 'grad_w', 'grad_a_conv_b': 'grad_w', 'grad_a_w_gate_r': 'grad_w', 'grad_a_b_gate_r': 'grad_w', 'grad_a_w_gate_i': 'grad_w', 'grad_a_b_gate_i': 'grad_w', 'grad_a_lambda': 'grad_w', 'grad_a_w_out': 'grad_w', 'grad_b_w_in': 'grad_w', 'grad_b_conv_w': 'grad_w', 'grad_b_conv_b': 'grad_w', 'grad_b_dt_bias': 'grad_w', 'grad_b_a_log': 'grad_w', 'grad_b_d_skip': 'grad_w', 'grad_b_norm_g': 'grad_w', 'grad_b_w_out': 'grad_w', 'grad_ffn_w_gate': 'grad_w', 'grad_ffn_w_up': 'grad_w', 'grad_ffn_w_down': 'grad_w', 'grad_ple_w_proj': 'grad_w', 'grad_ple_w_gate': 'grad_w', 'delta_norm_mix_g': 'delta_w', 'delta_norm_ffn_g': 'delta_w', 'delta_norm_ple_g': 'delta_w', 'delta_final_norm_g': 'delta_w', 'delta_a_w_in': 'delta_w', 'delta_a_conv_w': 'delta_w', 'delta_a_conv_b': 'delta_w', 'delta_a_w_gate_r': 'delta_w', 'delta_a_b_gate_r': 'delta_w', 'delta_a_w_gate_i': 'delta_w', 'delta_a_b_gate_i': 'delta_w', 'delta_a_lambda': 'delta_w', 'delta_a_w_out': 'delta_w', 'delta_b_w_in': 'delta_w', 'delta_b_conv_w': 'delta_w', 'delta_b_conv_b': 'delta_w', 'delta_b_dt_bias': 'delta_w', 'delta_b_a_log': 'delta_w', 'delta_b_d_skip': 'delta_w', 'delta_b_norm_g': 'delta_w', 'delta_b_w_out': 'delta_w', 'delta_ffn_w_gate': 'delta_w', 'delta_ffn_w_up': 'delta_w', 'delta_ffn_w_down': 'delta_w', 'delta_ple_w_proj': 'delta_w', 'delta_ple_w_gate': 'delta_w', 'new_m_norm_mix_g': 'new_m', 'new_m_norm_ffn_g': 'new_m', 'new_m_norm_ple_g': 'new_m', 'new_m_final_norm_g': 'new_m', 'new_m_a_w_in': 'new_m', 'new_m_a_conv_w': 'new_m', 'new_m_a_conv_b': 'new_m', 'new_m_a_w_gate_r': 'new_m', 'new_m_a_b_gate_r': 'new_m', 'new_m_a_w_gate_i': 'new_m', 'new_m_a_b_gate_i': 'new_m', 'new_m_a_lambda': 'new_m', 'new_m_a_w_out': 'new_m', 'new_m_b_w_in': 'new_m', 'new_m_b_conv_w': 'new_m', 'new_m_b_conv_b': 'new_m', 'new_m_b_dt_bias': 'new_m', 'new_m_b_a_log': 'new_m', 'new_m_b_d_skip': 'new_m', 'new_m_b_norm_g': 'new_m', 'new_m_b_w_out': 'new_m', 'new_m_ffn_w_gate': 'new_m', 'new_m_ffn_w_up': 'new_m', 'new_m_ffn_w_down': 'new_m', 'new_m_ple_w_proj': 'new_m', 'new_m_ple_w_gate': 'new_m', 'new_v_norm_mix_g': 'new_v', 'new_v_norm_ffn_g': 'new_v', 'new_v_norm_ple_g': 'new_v', 'new_v_final_norm_g': 'new_v', 'new_v_a_w_in': 'new_v', 'new_v_a_conv_w': 'new_v', 'new_v_a_conv_b': 'new_v', 'new_v_a_w_gate_r': 'new_v', 'new_v_a_b_gate_r': 'new_v', 'new_v_a_w_gate_i': 'new_v', 'new_v_a_b_gate_i': 'new_v', 'new_v_a_lambda': 'new_v', 'new_v_a_w_out': 'new_v', 'new_v_b_w_in': 'new_v', 'new_v_b_conv_w': 'new_v', 'new_v_b_conv_b': 'new_v', 'new_v_b_dt_bias': 'new_v', 'new_v_b_a_log': 'new_v', 'new_v_b_d_skip': 'new_v', 'new_v_b_norm_g': 'new_v', 'new_v_b_w_out': 'new_v', 'new_v_ffn_w_gate': 'new_v', 'new_v_ffn_w_up': 'new_v', 'new_v_ffn_w_down': 'new_v', 'new_v_ple_w_proj': 'new_v', 'new_v_ple_w_gate': 'new_v'}


def _forward(args):
    return _fwd_reference(*[args[k] for k in FWD_PARAMS])


def _output_shape():
    out = _jax.eval_shape(lambda: _forward(_fwd_setup_inputs(0)))
    return out.shape, out.dtype

N_MICROBATCH = 1
ADAM_LR = 0.001
ADAM_B1 = 0.9
ADAM_B2 = 0.999
ADAM_EPS = 1e-08
ADAM_WD = 0.01
ADAM_STEP = 10
PER_EXAMPLE_BATCH_AXIS = {'x': 0, 'p': 1, 'loss_target': 0}
SHARED_INPUTS = []
_WEIGHT_DTYPES = {'norm_mix_g': _jnp.float32, 'norm_ffn_g': _jnp.float32, 'norm_ple_g': _jnp.float32, 'final_norm_g': _jnp.float32, 'a_w_in': _jnp.float32, 'a_conv_w': _jnp.float32, 'a_conv_b': _jnp.float32, 'a_w_gate_r': _jnp.float32, 'a_b_gate_r': _jnp.float32, 'a_w_gate_i': _jnp.float32, 'a_b_gate_i': _jnp.float32, 'a_lambda': _jnp.float32, 'a_w_out': _jnp.float32, 'b_w_in': _jnp.float32, 'b_conv_w': _jnp.float32, 'b_conv_b': _jnp.float32, 'b_dt_bias': _jnp.float32, 'b_a_log': _jnp.float32, 'b_d_skip': _jnp.float32, 'b_norm_g': _jnp.float32, 'b_w_out': _jnp.float32, 'ffn_w_gate': _jnp.float32, 'ffn_w_up': _jnp.float32, 'ffn_w_down': _jnp.float32, 'ple_w_proj': _jnp.float32, 'ple_w_gate': _jnp.float32}
MOMENT_SCALE = {'norm_mix_g': 5.086759e-02, 'norm_ffn_g': 4.227743e-02, 'norm_ple_g': 1.012361e-02, 'final_norm_g': 7.997809e+00, 'a_w_in': 3.066690e-02, 'a_conv_w': 3.146336e-02, 'a_conv_b': 3.483471e-01, 'a_w_gate_r': 8.391792e-03, 'a_b_gate_r': 7.047870e-03, 'a_w_gate_i': 1.488531e-02, 'a_b_gate_i': 1.100961e-02, 'a_lambda': 1.489730e-02, 'a_w_out': 3.002913e-02, 'b_w_in': 2.607159e-02, 'b_conv_w': 2.404678e-02, 'b_conv_b': 3.082657e-02, 'b_dt_bias': 8.358889e-02, 'b_a_log': 1.018510e-01, 'b_d_skip': 1.433897e-01, 'b_norm_g': 2.759773e-02, 'b_w_out': 3.830610e-02, 'ffn_w_gate': 1.793915e-02, 'ffn_w_up': 1.737804e-02, 'ffn_w_down': 2.878723e-02, 'ple_w_proj': 2.578956e-02, 'ple_w_gate': 1.011093e-02}


def _to_microbatches(a, axis):
    t = _jnp.moveaxis(a, axis, 0)
    t = t.reshape((N_MICROBATCH, t.shape[0] // N_MICROBATCH) + t.shape[1:])
    return _jnp.moveaxis(t, 1, axis + 1)


def setup_inputs(seed: int = 0) -> dict:
    inp = _fwd_setup_inputs(seed)
    key = _jax.random.fold_in(_jax.random.key(seed), 7919)
    shape, _ = _output_shape()
    out = dict(inp)
    out["loss_target"] = _jax.random.normal(_jax.random.fold_in(key, 0), shape, _jnp.float32)
    for i, name in enumerate(TWIN_WEIGHTS):
        w = inp[name].astype(_jnp.float32)
        if MOMENT_SCALE is None:
            s = _jnp.sqrt(_jnp.mean(_jnp.square(w)) + 1e-30)
        else:
            s = MOMENT_SCALE[name]
        km, kv = _jax.random.split(_jax.random.fold_in(key, i + 1))
        out[name] = w
        out["m_" + name] = s * _jax.random.normal(km, w.shape, _jnp.float32)
        out["v_" + name] = (s * s) * _jax.random.uniform(kv, w.shape, _jnp.float32, 0.5, 1.5)
    if N_MICROBATCH > 1:
        for name, axis in PER_EXAMPLE_BATCH_AXIS.items():
            out[name] = _to_microbatches(out[name], axis)
    return {'x': out['x'], 'p': out['p'], 'norm_mix_g': out['norm_mix_g'], 'norm_ffn_g': out['norm_ffn_g'], 'norm_ple_g': out['norm_ple_g'], 'final_norm_g': out['final_norm_g'], 'a_w_in': out['a_w_in'], 'a_conv_w': out['a_conv_w'], 'a_conv_b': out['a_conv_b'], 'a_w_gate_r': out['a_w_gate_r'], 'a_b_gate_r': out['a_b_gate_r'], 'a_w_gate_i': out['a_w_gate_i'], 'a_b_gate_i': out['a_b_gate_i'], 'a_lambda': out['a_lambda'], 'a_w_out': out['a_w_out'], 'b_w_in': out['b_w_in'], 'b_conv_w': out['b_conv_w'], 'b_conv_b': out['b_conv_b'], 'b_dt_bias': out['b_dt_bias'], 'b_a_log': out['b_a_log'], 'b_d_skip': out['b_d_skip'], 'b_norm_g': out['b_norm_g'], 'b_w_out': out['b_w_out'], 'ffn_w_gate': out['ffn_w_gate'], 'ffn_w_up': out['ffn_w_up'], 'ffn_w_down': out['ffn_w_down'], 'ple_w_proj': out['ple_w_proj'], 'ple_w_gate': out['ple_w_gate'], 'loss_target': out['loss_target'], 'm_norm_mix_g': out['m_norm_mix_g'], 'm_norm_ffn_g': out['m_norm_ffn_g'], 'm_norm_ple_g': out['m_norm_ple_g'], 'm_final_norm_g': out['m_final_norm_g'], 'm_a_w_in': out['m_a_w_in'], 'm_a_conv_w': out['m_a_conv_w'], 'm_a_conv_b': out['m_a_conv_b'], 'm_a_w_gate_r': out['m_a_w_gate_r'], 'm_a_b_gate_r': out['m_a_b_gate_r'], 'm_a_w_gate_i': out['m_a_w_gate_i'], 'm_a_b_gate_i': out['m_a_b_gate_i'], 'm_a_lambda': out['m_a_lambda'], 'm_a_w_out': out['m_a_w_out'], 'm_b_w_in': out['m_b_w_in'], 'm_b_conv_w': out['m_b_conv_w'], 'm_b_conv_b': out['m_b_conv_b'], 'm_b_dt_bias': out['m_b_dt_bias'], 'm_b_a_log': out['m_b_a_log'], 'm_b_d_skip': out['m_b_d_skip'], 'm_b_norm_g': out['m_b_norm_g'], 'm_b_w_out': out['m_b_w_out'], 'm_ffn_w_gate': out['m_ffn_w_gate'], 'm_ffn_w_up': out['m_ffn_w_up'], 'm_ffn_w_down': out['m_ffn_w_down'], 'm_ple_w_proj': out['m_ple_w_proj'], 'm_ple_w_gate': out['m_ple_w_gate'], 'v_norm_mix_g': out['v_norm_mix_g'], 'v_norm_ffn_g': out['v_norm_ffn_g'], 'v_norm_ple_g': out['v_norm_ple_g'], 'v_final_norm_g': out['v_final_norm_g'], 'v_a_w_in': out['v_a_w_in'], 'v_a_conv_w': out['v_a_conv_w'], 'v_a_conv_b': out['v_a_conv_b'], 'v_a_w_gate_r': out['v_a_w_gate_r'], 'v_a_b_gate_r': out['v_a_b_gate_r'], 'v_a_w_gate_i': out['v_a_w_gate_i'], 'v_a_b_gate_i': out['v_a_b_gate_i'], 'v_a_lambda': out['v_a_lambda'], 'v_a_w_out': out['v_a_w_out'], 'v_b_w_in': out['v_b_w_in'], 'v_b_conv_w': out['v_b_conv_w'], 'v_b_conv_b': out['v_b_conv_b'], 'v_b_dt_bias': out['v_b_dt_bias'], 'v_b_a_log': out['v_b_a_log'], 'v_b_d_skip': out['v_b_d_skip'], 'v_b_norm_g': out['v_b_norm_g'], 'v_b_w_out': out['v_b_w_out'], 'v_ffn_w_gate': out['v_ffn_w_gate'], 'v_ffn_w_up': out['v_ffn_w_up'], 'v_ffn_w_down': out['v_ffn_w_down'], 'v_ple_w_proj': out['v_ple_w_proj'], 'v_ple_w_gate': out['v_ple_w_gate']}


def _loss(weights, diff, rest, loss_target):
    with _jax.named_scope("forward"):
        args = {**rest, TWIN_DIFF_INPUT: diff, **{k: w.astype(_WEIGHT_DTYPES[k]) for k, w in weights.items()}}
        y = _forward(args)
    with _jax.named_scope("loss_head"):
        err = _jnp.square(y.astype(_jnp.float32) - loss_target)
        return 0.5 * _jnp.sum(_jnp.mean(err, axis=-1)) if err.ndim else 0.5 * err


def _adamw(w, g, m, v):
    m = ADAM_B1 * m + (1.0 - ADAM_B1) * g
    v = ADAM_B2 * v + (1.0 - ADAM_B2) * _jnp.square(g)
    m_hat = m / (1.0 - ADAM_B1 ** ADAM_STEP)
    v_hat = v / (1.0 - ADAM_B2 ** ADAM_STEP)
    delta = -ADAM_LR * (m_hat / (_jnp.sqrt(v_hat) + ADAM_EPS) + ADAM_WD * w)
    return delta, m, v


def reference(x, p, norm_mix_g, norm_ffn_g, norm_ple_g, final_norm_g, a_w_in, a_conv_w, a_conv_b, a_w_gate_r, a_b_gate_r, a_w_gate_i, a_b_gate_i, a_lambda, a_w_out, b_w_in, b_conv_w, b_conv_b, b_dt_bias, b_a_log, b_d_skip, b_norm_g, b_w_out, ffn_w_gate, ffn_w_up, ffn_w_down, ple_w_proj, ple_w_gate, loss_target, m_norm_mix_g, m_norm_ffn_g, m_norm_ple_g, m_final_norm_g, m_a_w_in, m_a_conv_w, m_a_conv_b, m_a_w_gate_r, m_a_b_gate_r, m_a_w_gate_i, m_a_b_gate_i, m_a_lambda, m_a_w_out, m_b_w_in, m_b_conv_w, m_b_conv_b, m_b_dt_bias, m_b_a_log, m_b_d_skip, m_b_norm_g, m_b_w_out, m_ffn_w_gate, m_ffn_w_up, m_ffn_w_down, m_ple_w_proj, m_ple_w_gate, v_norm_mix_g, v_norm_ffn_g, v_norm_ple_g, v_final_norm_g, v_a_w_in, v_a_conv_w, v_a_conv_b, v_a_w_gate_r, v_a_b_gate_r, v_a_w_gate_i, v_a_b_gate_i, v_a_lambda, v_a_w_out, v_b_w_in, v_b_conv_w, v_b_conv_b, v_b_dt_bias, v_b_a_log, v_b_d_skip, v_b_norm_g, v_b_w_out, v_ffn_w_gate, v_ffn_w_up, v_ffn_w_down, v_ple_w_proj, v_ple_w_gate):
    given = dict(x=x, p=p, norm_mix_g=norm_mix_g, norm_ffn_g=norm_ffn_g, norm_ple_g=norm_ple_g, final_norm_g=final_norm_g, a_w_in=a_w_in, a_conv_w=a_conv_w, a_conv_b=a_conv_b, a_w_gate_r=a_w_gate_r, a_b_gate_r=a_b_gate_r, a_w_gate_i=a_w_gate_i, a_b_gate_i=a_b_gate_i, a_lambda=a_lambda, a_w_out=a_w_out, b_w_in=b_w_in, b_conv_w=b_conv_w, b_conv_b=b_conv_b, b_dt_bias=b_dt_bias, b_a_log=b_a_log, b_d_skip=b_d_skip, b_norm_g=b_norm_g, b_w_out=b_w_out, ffn_w_gate=ffn_w_gate, ffn_w_up=ffn_w_up, ffn_w_down=ffn_w_down, ple_w_proj=ple_w_proj, ple_w_gate=ple_w_gate, loss_target=loss_target, m_norm_mix_g=m_norm_mix_g, m_norm_ffn_g=m_norm_ffn_g, m_norm_ple_g=m_norm_ple_g, m_final_norm_g=m_final_norm_g, m_a_w_in=m_a_w_in, m_a_conv_w=m_a_conv_w, m_a_conv_b=m_a_conv_b, m_a_w_gate_r=m_a_w_gate_r, m_a_b_gate_r=m_a_b_gate_r, m_a_w_gate_i=m_a_w_gate_i, m_a_b_gate_i=m_a_b_gate_i, m_a_lambda=m_a_lambda, m_a_w_out=m_a_w_out, m_b_w_in=m_b_w_in, m_b_conv_w=m_b_conv_w, m_b_conv_b=m_b_conv_b, m_b_dt_bias=m_b_dt_bias, m_b_a_log=m_b_a_log, m_b_d_skip=m_b_d_skip, m_b_norm_g=m_b_norm_g, m_b_w_out=m_b_w_out, m_ffn_w_gate=m_ffn_w_gate, m_ffn_w_up=m_ffn_w_up, m_ffn_w_down=m_ffn_w_down, m_ple_w_proj=m_ple_w_proj, m_ple_w_gate=m_ple_w_gate, v_norm_mix_g=v_norm_mix_g, v_norm_ffn_g=v_norm_ffn_g, v_norm_ple_g=v_norm_ple_g, v_final_norm_g=v_final_norm_g, v_a_w_in=v_a_w_in, v_a_conv_w=v_a_conv_w, v_a_conv_b=v_a_conv_b, v_a_w_gate_r=v_a_w_gate_r, v_a_b_gate_r=v_a_b_gate_r, v_a_w_gate_i=v_a_w_gate_i, v_a_b_gate_i=v_a_b_gate_i, v_a_lambda=v_a_lambda, v_a_w_out=v_a_w_out, v_b_w_in=v_b_w_in, v_b_conv_w=v_b_conv_w, v_b_conv_b=v_b_conv_b, v_b_dt_bias=v_b_dt_bias, v_b_a_log=v_b_a_log, v_b_d_skip=v_b_d_skip, v_b_norm_g=v_b_norm_g, v_b_w_out=v_b_w_out, v_ffn_w_gate=v_ffn_w_gate, v_ffn_w_up=v_ffn_w_up, v_ffn_w_down=v_ffn_w_down, v_ple_w_proj=v_ple_w_proj, v_ple_w_gate=v_ple_w_gate)
    weights = {n: given[n] for n in TWIN_WEIGHTS}
    shared = {n: given[n] for n in SHARED_INPUTS}
    per_example = {n: given[n] for n in ['x', 'p']}
    grad_fn = _jax.value_and_grad(_loss, argnums=(0, 1))

    def one_microbatch(ex, loss_target):
        ex = dict(ex)
        diff = ex.pop(TWIN_DIFF_INPUT)
        return grad_fn(weights, diff, {**shared, **ex}, loss_target)

    if N_MICROBATCH == 1:
        loss, (grad_w, grad_x) = one_microbatch(per_example, given["loss_target"])
    else:
        def body(carry, xs):
            loss_sum, grad_sum = carry
            l_k, (gw_k, gx_k) = one_microbatch(xs[0], xs[1])
            with _jax.named_scope("update"):
                return (loss_sum + l_k, _jax.tree.map(_jnp.add, grad_sum, gw_k)), gx_k

        init = (_jnp.zeros((), _jnp.float32), _jax.tree.map(_jnp.zeros_like, weights))
        (loss, grad_w), grad_x = _jax.lax.scan(body, init, (per_example, given["loss_target"]))
    with _jax.named_scope("update"):
        delta_w, new_m, new_v = {}, {}, {}
        for n in TWIN_WEIGHTS:
            delta_w[n], new_m[n], new_v[n] = _adamw(weights[n], grad_w[n], given["m_" + n], given["v_" + n])
    return (loss, grad_x, *[grad_w[n] for n in TWIN_WEIGHTS], *[delta_w[n] for n in TWIN_WEIGHTS],
            *[new_m[n] for n in TWIN_WEIGHTS], *[new_v[n] for n in TWIN_WEIGHTS])
```

```python
import functools

import jax
import jax.numpy as jnp
from jax import lax
from jax.experimental import pallas as pl
from jax.experimental.pallas import tpu as pltpu

F32 = jnp.float32
BF16 = jnp.bfloat16
MESH = pl.DeviceIdType.MESH
HIGHEST = lax.Precision.HIGHEST

NORM_EPS = 1e-6
LRU_C = 8.0
CONV_WIDTH = 4
SSD_HEAD_DIM = 64
SSD_STATE = 128
SSD_CHUNK = 128
SSD_HEADS_PER_GROUP = 8
LANE = 128
SUBLANE = 8
N_CHIPS = 4
N_DEV = 8
VMEM_LIMIT = 48 * 1024 * 1024

ADAM_LR = 0.001
ADAM_B1 = 0.9
ADAM_B2 = 0.999
ADAM_EPS = 1e-08
ADAM_WD = 0.01
ADAM_STEP = 10


def _tile(n, cap, mult=LANE):
    best = None
    for t in range(mult, min(n, cap) + 1, mult):
        if n % t == 0:
            best = t
    return best if best is not None else n


def _params(*sem):
    return pltpu.CompilerParams(dimension_semantics=sem, vmem_limit_bytes=VMEM_LIMIT)


def _mm(a, b, *, ta=False, tb=False, add=None, out_dtype=F32, name):
    if ta:
        kd, m = a.shape
    else:
        m, kd = a.shape
    if tb:
        n, kb = b.shape
    else:
        kb, n = b.shape
    assert kd == kb, (a.shape, b.shape, ta, tb)
    tm = _tile(m, 1024)
    tn = _tile(n, 1024)
    tk = _tile(kd, 512)
    nk = kd // tk
    dims = (((0 if ta else 1,), (1 if tb else 0,)), ((), ()))

    def body(*refs):
        if add is None:
            a_ref, b_ref, o_ref, acc_ref = refs
        else:
            a_ref, b_ref, add_ref, o_ref, acc_ref = refs
        k = pl.program_id(2)

        @pl.when(k == 0)
        def _():
            acc_ref[...] = jnp.zeros_like(acc_ref)

        acc_ref[...] += lax.dot_general(a_ref[...].astype(BF16), b_ref[...].astype(BF16), dims,
                                        preferred_element_type=F32)

        @pl.when(k == nk - 1)
        def _():
            r = acc_ref[...]
            if add is not None:
                r = r + add_ref[...]
            o_ref[...] = r.astype(out_dtype)

    a_spec = pl.BlockSpec((tk, tm), lambda i, j, k: (k, i)) if ta else pl.BlockSpec((tm, tk), lambda i, j, k: (i, k))
    b_spec = pl.BlockSpec((tn, tk), lambda i, j, k: (j, k)) if tb else pl.BlockSpec((tk, tn), lambda i, j, k: (k, j))
    o_spec = pl.BlockSpec((tm, tn), lambda i, j, k: (i, j))
    in_specs = [a_spec, b_spec] + ([o_spec] if add is not None else [])
    args = (a, b) + ((add,) if add is not None else ())
    return pl.pallas_call(
        body, name=name, grid=(m // tm, n // tn, nk), in_specs=in_specs, out_specs=o_spec,
        out_shape=jax.ShapeDtypeStruct((m, n), out_dtype), scratch_shapes=[pltpu.VMEM((tm, tn), F32)],
        compiler_params=_params("parallel", "parallel", "arbitrary"))(*args)


def _rowwise(name, body, ins, outs, nrows, ts, ncol=1):
    ts = min(ts, nrows)
    nrow = nrows // ts
    hb = ts // SUBLANE
    nb8 = nrows // SUBLANE
    in_specs, args = [], []
    for kind, arr, cb in ins:
        if kind == "row":
            spec = pl.BlockSpec((ts, cb), lambda j, i: (i, j))
        elif kind == "prev":
            spec = pl.BlockSpec((SUBLANE, cb), lambda j, i: (jnp.maximum(i * hb - 1, 0), j))
        elif kind == "next":
            spec = pl.BlockSpec((SUBLANE, cb), lambda j, i: (jnp.minimum((i + 1) * hb, nb8 - 1), j))
        else:
            spec = pl.BlockSpec((arr.shape[0], cb), lambda j, i: (0, j))
        in_specs.append(spec)
        args.append(arr)
    out_specs, out_shape = [], []
    for kind, rows, ctot, cb, dt in outs:
        if kind == "row":
            out_shape.append(jax.ShapeDtypeStruct((nrows, ctot), dt))
            out_specs.append(pl.BlockSpec((ts, cb), lambda j, i: (i, j)))
        else:
            out_shape.append(jax.ShapeDtypeStruct((rows, ctot), dt))
            out_specs.append(pl.BlockSpec((rows, cb), lambda j, i: (0, j)))

    def kern(*refs):
        body(pl.program_id(1), nrow, *refs)

    return pl.pallas_call(kern, name=name, grid=(ncol, nrow), in_specs=in_specs, out_specs=out_specs,
                          out_shape=out_shape, compiler_params=_params("parallel", "arbitrary"))(*args)


def _colsum(x):
    return jnp.sum(x, axis=0, keepdims=True)


def _acc(i, ref, val):
    @pl.when(i == 0)
    def _():
        ref[...] = val

    @pl.when(i > 0)
    def _():
        ref[...] += val


def _shift_down(x, halo, k):
    xx = jnp.concatenate([halo, x], axis=0)
    return pltpu.roll(xx, k, axis=0)[SUBLANE:, :]


def _shift_up(x, halo, k):
    xx = jnp.concatenate([x, halo], axis=0)
    n = xx.shape[0]
    return pltpu.roll(xx, n - k, axis=0)[: x.shape[0], :]


def _sigmoid(x):
    return 1.0 / (1.0 + jnp.exp(-x))


def _silu(x):
    return x * _sigmoid(x)


def _dsilu(x):
    s = _sigmoid(x)
    return s * (1.0 + x * (1.0 - s))


_GELU_K = 0.7978845608028654
_GELU_C = 0.044715


def _gelu(x):
    return 0.5 * x * (1.0 + jnp.tanh(_GELU_K * (x + _GELU_C * x * x * x)))


def _dgelu(x):
    t = jnp.tanh(_GELU_K * (x + _GELU_C * x * x * x))
    return 0.5 * (1.0 + t) + 0.5 * x * (1.0 - t * t) * _GELU_K * (1.0 + 3.0 * _GELU_C * x * x)


def _softplus(x):
    return jnp.maximum(x, 0.0) + jnp.log1p(jnp.exp(-jnp.abs(x)))


def _neg_expm1(x):
    poly = -x * (1.0 + x * (0.5 + x * (1.0 / 6.0 + x * (1.0 / 24.0 + x * (1.0 / 120.0)))))
    return jnp.where(x > -0.05, poly, 1.0 - jnp.exp(x))


def _rmsnorm_fwd(h, g, name):
    s, d = h.shape

    def body(i, n, h_ref, g_ref, o_ref):
        x = h_ref[...]
        r = lax.rsqrt(jnp.mean(x * x, axis=-1, keepdims=True) + NORM_EPS)
        o_ref[...] = (x * r * g_ref[...]).astype(BF16)

    return _rowwise(name, body, [("row", h, d), ("vec", g, d)], [("row", None, d, d, BF16)], s, 256)[0]


def _rmsnorm_bwd(dn, h, g, dres, name):
    s, d = h.shape

    def body(i, n, dn_ref, h_ref, g_ref, dres_ref, dh_ref, dg_ref):
        x = h_ref[...]
        dy = dn_ref[...].astype(F32)
        r = lax.rsqrt(jnp.mean(x * x, axis=-1, keepdims=True) + NORM_EPS)
        xh = x * r
        _acc(i, dg_ref, _colsum(dy * xh))
        dxh = dy * g_ref[...]
        dh_ref[...] = dres_ref[...] + r * (dxh - xh * jnp.mean(dxh * xh, axis=-1, keepdims=True))

    return _rowwise(name, body, [("row", dn, d), ("row", h, d), ("vec", g, d), ("row", dres, d)],
                    [("row", None, d, d, F32), ("acc", 1, d, d, F32)], s, 256)


def _final_loss_bwd(h, g, tgt):
    s, d = h.shape

    def body(i, n, h_ref, g_ref, t_ref, dh_ref, dg_ref, loss_ref):
        x = h_ref[...]
        gg = g_ref[...]
        r = lax.rsqrt(jnp.mean(x * x, axis=-1, keepdims=True) + NORM_EPS)
        xh = x * r
        err = xh * gg - t_ref[...]
        part = 0.5 * jnp.sum(jnp.mean(err * err, axis=-1, keepdims=True), axis=0, keepdims=True)
        _acc(i, loss_ref, jnp.broadcast_to(part, (1, LANE)))
        dy = err * (1.0 / d)
        _acc(i, dg_ref, _colsum(dy * xh))
        dxh = dy * gg
        dh_ref[...] = r * (dxh - xh * jnp.mean(dxh * xh, axis=-1, keepdims=True))

    return _rowwise("final_loss_bwd", body, [("row", h, d), ("vec", g, d), ("row", tgt, d)],
                    [("row", None, d, d, F32), ("acc", 1, d, d, F32), ("acc", 1, LANE, LANE, F32)], s, 256)


def _conv_rows(x, halo, w, b):
    y = b + w[3:4, :] * x
    for k in range(CONV_WIDTH - 1):
        y = y + w[k:k + 1, :] * _shift_down(x, halo, CONV_WIDTH - 1 - k)
    return y


def _conv_fwd(x, w, b, silu, name):
    s, c = x.shape
    cb = _tile(c, 512)

    def body(i, n, x_ref, p_ref, w_ref, b_ref, o_ref):
        halo = jnp.where(i == 0, 0.0, p_ref[...])
        y = _conv_rows(x_ref[...], halo, w_ref[...], b_ref[...])
        o_ref[...] = _silu(y) if silu else y

    return _rowwise(name, body, [("row", x, cb), ("prev", x, cb), ("vec", w, cb), ("vec", b, cb)],
                    [("row", None, c, cb, F32)], s, 512, ncol=c // cb)[0]


def _silu_conv_bwd_pre(dy, x, w, b, name):
    s, c = x.shape
    cb = _tile(c, 512)

    def body(i, n, dy_ref, x_ref, p_ref, w_ref, b_ref, o_ref):
        halo = jnp.where(i == 0, 0.0, p_ref[...])
        y = _conv_rows(x_ref[...], halo, w_ref[...], b_ref[...])
        o_ref[...] = dy_ref[...] * _dsilu(y)

    return _rowwise(name, body, [("row", dy, cb), ("row", x, cb), ("prev", x, cb), ("vec", w, cb), ("vec", b, cb)],
                    [("row", None, c, cb, F32)], s, 512, ncol=c // cb)[0]


def _conv_bwd(dy, x, w, name):
    s, c = x.shape
    cb = _tile(c, 512)

    def body(i, n, dy_ref, nx_ref, x_ref, p_ref, w_ref, dx_ref, dw_ref, db_ref):
        d = dy_ref[...]
        xx = x_ref[...]
        wv = w_ref[...]
        nxt = jnp.where(i == n - 1, 0.0, nx_ref[...])
        prv = jnp.where(i == 0, 0.0, p_ref[...])
        dx = wv[3:4, :] * d
        parts = []
        for k in range(CONV_WIDTH - 1):
            sh = CONV_WIDTH - 1 - k
            dx = dx + wv[k:k + 1, :] * _shift_up(d, nxt, sh)
            parts.append(_colsum(d * _shift_down(xx, prv, sh)))
        parts.append(_colsum(d * xx))
        dx_ref[...] = dx.astype(BF16)
        _acc(i, dw_ref, jnp.concatenate(parts, axis=0))
        _acc(i, db_ref, _colsum(d))

    return _rowwise(name, body, [("row", dy, cb), ("next", dy, cb), ("row", x, cb), ("prev", x, cb), ("vec", w, cb)],
                    [("row", None, c, cb, BF16), ("acc", CONV_WIDTH, c, cb, F32), ("acc", 1, c, cb, F32)],
                    s, 512, ncol=c // cb)


def _lru_gate_math(xr, r_pre, i_pre, lam):
    r = _sigmoid(r_pre)
    ig = _sigmoid(i_pre)
    sp = _softplus(-lam)
    log_a = -LRU_C * r * sp
    a = jnp.exp(log_a)
    mult = jnp.sqrt(_neg_expm1(2.0 * log_a))
    return r, ig, sp, a, mult


def _lru_gates_fwd(xr, wr, wi, br, bi, lam):
    s, d = xr.shape
    nh, bw, _ = wr.shape
    ts = min(512, s)

    def body(x_ref, wr_ref, wi_ref, br_ref, bi_ref, lam_ref, a_ref, b_ref):
        x = x_ref[...]
        xb = x.astype(BF16)
        r_pre = jnp.dot(xb, wr_ref[0], preferred_element_type=F32) + br_ref[...]
        i_pre = jnp.dot(xb, wi_ref[0], preferred_element_type=F32) + bi_ref[...]
        _, ig, _, a, mult = _lru_gate_math(x, r_pre, i_pre, lam_ref[...])
        a_ref[...] = a
        b_ref[...] = mult * (ig * x)

    row = pl.BlockSpec((ts, bw), lambda h, i: (i, h))
    wsp = pl.BlockSpec((1, bw, bw), lambda h, i: (h, 0, 0))
    vec = pl.BlockSpec((1, bw), lambda h, i: (0, h))
    return pl.pallas_call(
        body, name="lru_gates_fwd", grid=(nh, s // ts), in_specs=[row, wsp, wsp, vec, vec, vec], out_specs=[row, row],
        out_shape=[jax.ShapeDtypeStruct((s, d), F32)] * 2, compiler_params=_params("parallel", "arbitrary"),
    )(xr, wr, wi, br, bi, lam)


def _lru_gates_bwd(xr, g, hs, wr, wi, br, bi, lam):
    s, d = xr.shape
    nh, bw, _ = wr.shape
    ts = min(512, s)
    hb = ts // SUBLANE
    tn_dims = (((0,), (0,)), ((), ()))
    nt_dims = (((1,), (1,)), ((), ()))

    def body(x_ref, g_ref, hs_ref, hp_ref, wr_ref, wi_ref, br_ref, bi_ref, lam_ref,
             dx_ref, dwr_ref, dwi_ref, dbr_ref, dbi_ref, dlam_ref):
        i = pl.program_id(1)
        x = x_ref[...]
        xb = x.astype(BF16)
        gg = g_ref[...]
        lam_v = lam_ref[...]
        r_pre = jnp.dot(xb, wr_ref[0], preferred_element_type=F32) + br_ref[...]
        i_pre = jnp.dot(xb, wi_ref[0], preferred_element_type=F32) + bi_ref[...]
        r, ig, sp, a, mult = _lru_gate_math(x, r_pre, i_pre, lam_v)
        h_prev = _shift_down(hs_ref[...], jnp.where(i == 0, 0.0, hp_ref[...]), 1)
        da = gg * h_prev
        dmult = gg * ig * x
        dlog_a = da * a - dmult * (a * a) / mult
        d_r = dlog_a * (-LRU_C * sp)
        dr_pre = d_r * r * (1.0 - r)
        di_pre = (gg * mult * x) * ig * (1.0 - ig)
        drb = dr_pre.astype(BF16)
        dib = di_pre.astype(BF16)
        dx_ref[...] = (gg * mult * ig
                       + lax.dot_general(drb, wr_ref[0], nt_dims, preferred_element_type=F32)
                       + lax.dot_general(dib, wi_ref[0], nt_dims, preferred_element_type=F32))
        dwr = lax.dot_general(xb, drb, tn_dims, preferred_element_type=F32)[None]
        dwi = lax.dot_general(xb, dib, tn_dims, preferred_element_type=F32)[None]
        dlam = _colsum(dlog_a * (-LRU_C * r)) * (-_sigmoid(-lam_v))
        _acc(i, dwr_ref, dwr)
        _acc(i, dwi_ref, dwi)
        _acc(i, dbr_ref, _colsum(dr_pre))
        _acc(i, dbi_ref, _colsum(di_pre))
        _acc(i, dlam_ref, dlam)

    row = pl.BlockSpec((ts, bw), lambda h, i: (i, h))
    prev = pl.BlockSpec((SUBLANE, bw), lambda h, i: (jnp.maximum(i * hb - 1, 0), h))
    wsp = pl.BlockSpec((1, bw, bw), lambda h, i: (h, 0, 0))
    vec = pl.BlockSpec((1, bw), lambda h, i: (0, h))
    return pl.pallas_call(
        body, name="lru_gates_bwd", grid=(nh, s // ts),
        in_specs=[row, row, row, prev, wsp, wsp, vec, vec, vec], out_specs=[row, wsp, wsp, vec, vec, vec],
        out_shape=[jax.ShapeDtypeStruct((s, d), F32), jax.ShapeDtypeStruct((nh, bw, bw), F32),
                   jax.ShapeDtypeStruct((nh, bw, bw), F32)] + [jax.ShapeDtypeStruct((1, d), F32)] * 3,
        compiler_params=_params("parallel", "arbitrary"),
    )(xr, g, hs, hs, wr, wi, br, bi, lam)


def _scan(a, b, reverse, name):
    s, c = a.shape
    cb = _tile(c, 512)
    nt = s // SUBLANE

    def body(a_ref, b_ref, o_ref):
        row = lax.broadcasted_iota(jnp.int32, (SUBLANE, cb), 0)

        def fwd_step(t, carry):
            r0 = pl.multiple_of(t * SUBLANE, SUBLANE)
            aa = a_ref[pl.ds(r0, SUBLANE), :]
            bb = b_ref[pl.ds(r0, SUBLANE), :]
            for sh in (1, 2, 4):
                a_s = jnp.where(row >= sh, pltpu.roll(aa, sh, axis=0), 1.0)
                b_s = jnp.where(row >= sh, pltpu.roll(bb, sh, axis=0), 0.0)
                bb = aa * b_s + bb
                aa = aa * a_s
            h = bb + aa * carry
            o_ref[pl.ds(r0, SUBLANE), :] = h
            return h[SUBLANE - 1:SUBLANE, :]

        def rev_step(k, carry):
            r0 = pl.multiple_of((nt - 1 - k) * SUBLANE, SUBLANE)
            aa = a_ref[pl.ds(r0, SUBLANE), :]
            dd = b_ref[pl.ds(r0, SUBLANE), :]
            cc = aa * dd
            for sh in (1, 2, 4):
                a_s = jnp.where(row < SUBLANE - sh, pltpu.roll(aa, SUBLANE - sh, axis=0), 1.0)
                c_s = jnp.where(row < SUBLANE - sh, pltpu.roll(cc, SUBLANE - sh, axis=0), 0.0)
                cc = cc + aa * c_s
                aa = aa * a_s
            big = cc + aa * carry
            nxt = jnp.where(row < SUBLANE - 1, pltpu.roll(big, SUBLANE - 1, axis=0), carry)
            o_ref[pl.ds(r0, SUBLANE), :] = dd + nxt
            return big[0:1, :]

        lax.fori_loop(0, nt, rev_step if reverse else fwd_step, jnp.zeros((1, cb), F32))

    spec = pl.BlockSpec((s, cb), lambda j: (0, j))
    return pl.pallas_call(body, name=name, grid=(c // cb,), in_specs=[spec, spec], out_specs=spec,
                          out_shape=jax.ShapeDtypeStruct((s, c), F32), compiler_params=_params("parallel"))(a, b)


def _lru_out_fwd(hs, yg):
    s, d = hs.shape
    cb = _tile(d, 1024)

    def body(i, n, h_ref, y_ref, o_ref):
        o_ref[...] = (h_ref[...] * _gelu(y_ref[...])).astype(BF16)

    return _rowwise("lru_out_fwd", body, [("row", hs, cb), ("row", yg, cb)], [("row", None, d, cb, BF16)],
                    s, 512, ncol=d // cb)[0]


def _lru_out_bwd(dy, hs, yg):
    s, d = hs.shape
    cb = _tile(d, 1024)

    def body(i, n, dy_ref, h_ref, y_ref, dh_ref, dyg_ref):
        dyv = dy_ref[...]
        y = y_ref[...]
        dh_ref[...] = dyv * _gelu(y)
        dyg_ref[...] = (dyv * h_ref[...] * _dgelu(y)).astype(BF16)

    return _rowwise("lru_out_bwd", body, [("row", dy, cb), ("row", hs, cb), ("row", yg, cb)],
                    [("row", None, d, cb, F32), ("row", None, d, cb, BF16)], s, 512, ncol=d // cb)


def _swiglu_act(gt, up):
    s, f = gt.shape
    cb = _tile(f, 1024)

    def body(i, n, g_ref, u_ref, o_ref):
        o_ref[...] = (_silu(g_ref[...]) * u_ref[...]).astype(BF16)

    return _rowwise("swiglu_act", body, [("row", gt, cb), ("row", up, cb)], [("row", None, f, cb, BF16)],
                    s, 512, ncol=f // cb)[0]


def _swiglu_bwd(dact, gt, up):
    s, f = gt.shape
    cb = _tile(f, 1024)

    def body(i, n, d_ref, g_ref, u_ref, dg_ref, du_ref):
        d = d_ref[...]
        g = g_ref[...]
        dg_ref[...] = (d * u_ref[...] * _dsilu(g)).astype(BF16)
        du_ref[...] = (d * _silu(g)).astype(BF16)

    return _rowwise("swiglu_bwd", body, [("row", dact, cb), ("row", gt, cb), ("row", up, cb)],
                    [("row", None, f, cb, BF16), ("row", None, f, cb, BF16)], s, 512, ncol=f // cb)


def _ple_fwd(h, gp, pp):
    s, d = h.shape
    cb = _tile(d, 1024)

    def body(i, n, h_ref, g_ref, p_ref, o_ref):
        o_ref[...] = h_ref[...] + _sigmoid(g_ref[...]) * p_ref[...]

    return _rowwise("ple_fwd", body, [("row", h, cb), ("row", gp, cb), ("row", pp, cb)], [("row", None, d, cb, F32)],
                    s, 512, ncol=d // cb)[0]


def _ple_bwd(dh, gp, pp):
    s, d = dh.shape
    cb = _tile(d, 1024)

    def body(i, n, d_ref, g_ref, p_ref, dg_ref, dp_ref):
        dv = d_ref[...]
        sg = _sigmoid(g_ref[...])
        dg_ref[...] = (dv * p_ref[...] * sg * (1.0 - sg)).astype(BF16)
        dp_ref[...] = (dv * sg).astype(BF16)

    return _rowwise("ple_bwd", body, [("row", dh, cb), ("row", gp, cb), ("row", pp, cb)],
                    [("row", None, d, cb, BF16), ("row", None, d, cb, BF16)], s, 512, ncol=d // cb)


def _dt_fwd(dt_pre, bias, n_heads):
    s = dt_pre.shape[0]

    def body(i, n, d_ref, b_ref, o_ref):
        lane = lax.broadcasted_iota(jnp.int32, d_ref.shape, 1)
        o_ref[...] = jnp.where(lane < n_heads, _softplus(d_ref[...] + b_ref[...]), 0.0)

    return _rowwise("ssd_dt_fwd", body, [("row", dt_pre, LANE), ("vec", bias, LANE)], [("row", None, LANE, LANE, F32)],
                    s, 512)[0]


def _dt_bwd(ddt, dt_pre, bias, n_heads):
    s = dt_pre.shape[0]

    def body(i, n, g_ref, d_ref, b_ref, o_ref, db_ref):
        lane = lax.broadcasted_iota(jnp.int32, d_ref.shape, 1)
        v = jnp.where(lane < n_heads, g_ref[...] * _sigmoid(d_ref[...] + b_ref[...]), 0.0)
        o_ref[...] = v.astype(BF16)
        _acc(i, db_ref, _colsum(v))

    return _rowwise("ssd_dt_bwd", body, [("row", ddt, LANE), ("row", dt_pre, LANE), ("vec", bias, LANE)],
                    [("row", None, LANE, LANE, BF16), ("acc", 1, LANE, LANE, F32)], s, 512)


def _ssd_chunk_terms(dt, alog, g, gw):
    ln = dt.shape[0]
    a_neg = -jnp.exp(alog)
    adt = dt * a_neg
    row = lax.broadcasted_iota(jnp.int32, (ln, ln), 0)
    col = lax.broadcasted_iota(jnp.int32, (ln, ln), 1)
    tril = row >= col
    cs = jnp.dot(tril.astype(F32), adt, preferred_element_type=F32, precision=HIGHEST)
    er = lax.broadcasted_iota(jnp.int32, (LANE, gw), 0)
    el = lax.broadcasted_iota(jnp.int32, (LANE, gw), 1)
    eg = (er == g * SSD_HEADS_PER_GROUP + el // SSD_HEAD_DIM).astype(F32)
    cs_e = jnp.dot(cs, eg, preferred_element_type=F32, precision=HIGHEST)
    dt_e = jnp.dot(dt, eg, preferred_element_type=F32, precision=HIGHEST)
    return a_neg, adt, cs, tril, eg, cs_e, dt_e


def _ssd_head_scores(cs, cst_ref, cb_mat, tril, g, e):
    h = g * SSD_HEADS_PER_GROUP + e
    sel = (lax.broadcasted_iota(jnp.int32, (LANE, LANE), 0) == h).astype(F32)
    ccol = jnp.dot(cs, sel, preferred_element_type=F32, precision=HIGHEST)
    crow = cst_ref[pl.ds(h, 1), :]
    lm = jnp.where(tril, jnp.exp(jnp.minimum(ccol - crow, 0.0)), 0.0)
    return (cb_mat * lm).astype(BF16), lm


_NT = (((1,), (1,)), ((), ()))
_TN = (((0,), (0,)), ((), ()))


def _ssd_fwd(xbc, dt, alog, inner, n_groups):
    s = xbc.shape[0]
    ln = SSD_CHUNK
    nc = s // ln
    gw = SSD_HEADS_PER_GROUP * SSD_HEAD_DIM
    npair = gw // LANE
    boff = inner // LANE

    def body(xs_ref, b_ref, c_ref, dt_ref, alog_ref, y_ref, sin_ref, st_ref, cst_ref):
        c = pl.program_id(0)
        g = pl.program_id(1)

        @pl.when(c == 0)
        def _():
            st_ref[g] = jnp.zeros((SSD_STATE, gw), F32)

        dtv = dt_ref[...]
        _, _, cs, tril, _, cs_e, dt_e = _ssd_chunk_terms(dtv, alog_ref[...], g, gw)
        cst_ref[...] = cs.T
        x = xs_ref[...] * dt_e
        tot_e = cs_e[ln - 1:ln, :]
        dec = jnp.exp(tot_e - cs_e)
        ecs = jnp.exp(cs_e)
        xb = x.astype(BF16)
        xd = (x * dec).astype(BF16)
        bg = b_ref[...].astype(BF16)
        cg = c_ref[...].astype(BF16)
        cb_mat = lax.dot_general(cg, bg, _NT, preferred_element_type=F32)
        sg = st_ref[g]
        sin_ref[0] = sg
        lo = lax.broadcasted_iota(jnp.int32, (ln, LANE), 1) < SSD_HEAD_DIM
        scores = [_ssd_head_scores(cs, cst_ref, cb_mat, tril, g, e)[0] for e in range(SSD_HEADS_PER_GROUP)]
        ys, news = [], []
        for pr in range(npair):
            cols = slice(LANE * pr, LANE * (pr + 1))
            xp = xb[:, cols]
            zero = jnp.zeros_like(xp)
            acc = jnp.dot(scores[2 * pr], jnp.where(lo, xp, zero), preferred_element_type=F32)
            acc = acc + jnp.dot(scores[2 * pr + 1], jnp.where(lo, zero, xp), preferred_element_type=F32)
            sp = sg[:, cols]
            yoff = jnp.dot(cg, sp.astype(BF16), preferred_element_type=F32) * ecs[:, cols]
            ys.append(acc + yoff)
            news.append(jnp.exp(tot_e[:, cols]) * sp
                        + lax.dot_general(bg, xd[:, cols], _TN, preferred_element_type=F32))
        y_ref[...] = jnp.concatenate(ys, axis=1)
        st_ref[g] = jnp.concatenate(news, axis=1)

    in_specs = [pl.BlockSpec((ln, gw), lambda c, g: (c, g)),
                pl.BlockSpec((ln, SSD_STATE), lambda c, g: (c, boff + g)),
                pl.BlockSpec((ln, SSD_STATE), lambda c, g: (c, boff + n_groups + g)),
                pl.BlockSpec((ln, LANE), lambda c, g: (c, 0)),
                pl.BlockSpec((1, LANE), lambda c, g: (0, 0))]
    out_specs = [pl.BlockSpec((ln, gw), lambda c, g: (c, g)),
                 pl.BlockSpec((1, SSD_STATE, gw), lambda c, g: (c, 0, g))]
    return pl.pallas_call(
        body, name="ssd_fwd", grid=(nc, n_groups), in_specs=in_specs, out_specs=out_specs,
        out_shape=[jax.ShapeDtypeStruct((s, inner), F32), jax.ShapeDtypeStruct((nc, SSD_STATE, inner), F32)],
        scratch_shapes=[pltpu.VMEM((n_groups, SSD_STATE, gw), F32), pltpu.VMEM((LANE, ln), F32)],
        compiler_params=_params("arbitrary", "arbitrary"),
    )(xbc, xbc, xbc, dt, alog)


def _ssd_bwd(xbc, dt, alog, dy, y, sin, dskip_e, inner, n_groups):
    s = xbc.shape[0]
    ln = SSD_CHUNK
    nc = s // ln
    gw = SSD_HEADS_PER_GROUP * SSD_HEAD_DIM
    npair = gw // LANE
    boff = inner // LANE

    def body(xs_ref, b_ref, c_ref, dt_ref, alog_ref, dy_ref, y_ref, sin_ref, sout_ref, dsk_ref,
             dxs_ref, db_ref, dc_ref, ddt_ref, dalog_ref, ds_ref, cst_ref):
        step = pl.program_id(0)
        g = pl.program_id(1)

        @pl.when(step == 0)
        def _():
            ds_ref[g] = jnp.zeros((SSD_STATE, gw), F32)

        dtv = dt_ref[...]
        a_neg, _, cs, tril, eg, cs_e, dt_e = _ssd_chunk_terms(dtv, alog_ref[...], g, gw)
        cst_ref[...] = cs.T
        xs = xs_ref[...]
        x = xs * dt_e
        tot_e = cs_e[ln - 1:ln, :]
        dec = jnp.exp(tot_e - cs_e)
        ecs = jnp.exp(cs_e)
        xb = x.astype(BF16)
        xd = (x * dec).astype(BF16)
        bg = b_ref[...].astype(BF16)
        cg = c_ref[...].astype(BF16)
        cb_mat = lax.dot_general(cg, bg, _NT, preferred_element_type=F32)
        dyv = dy_ref[...]
        dyb = dyv.astype(BF16)
        dye = (ecs * dyv).astype(BF16)
        s_in = sin_ref[0]
        d_s = ds_ref[g]
        dtot_lane = _colsum(sout_ref[0] * d_s)
        lo = lax.broadcasted_iota(jnp.int32, (ln, LANE), 1) < SSD_HEAD_DIM
        heads = [_ssd_head_scores(cs, cst_ref, cb_mat, tril, g, e) for e in range(SSD_HEADS_PER_GROUP)]
        dcb = jnp.zeros((ln, ln), F32)
        dbg = jnp.zeros((ln, SSD_STATE), F32)
        dcg = jnp.zeros((ln, SSD_STATE), F32)
        dxs_parts, nds = [], []
        for pr in range(npair):
            cols = slice(LANE * pr, LANE * (pr + 1))
            xp = xb[:, cols]
            dyp = dyb[:, cols]
            zero = jnp.zeros_like(dyp)
            dxp = jnp.zeros((ln, LANE), F32)
            for e, dym in ((2 * pr, jnp.where(lo, dyp, zero)), (2 * pr + 1, jnp.where(lo, zero, dyp))):
                sc, lm = heads[e]
                dsc = lax.dot_general(dym, xp, _NT, preferred_element_type=F32)
                dcb = dcb + dsc * lm
                dxp = dxp + lax.dot_general(sc, dym, _TN, preferred_element_type=F32)
            dsp = d_s[:, cols]
            dspb = dsp.astype(BF16)
            dxp = dxp + dec[:, cols] * jnp.dot(bg, dspb, preferred_element_type=F32)
            dcg = dcg + lax.dot_general(dye[:, cols], s_in[:, cols].astype(BF16), _NT, preferred_element_type=F32)
            dbg = dbg + lax.dot_general(xd[:, cols], dspb, _NT, preferred_element_type=F32)
            nds.append(jnp.exp(tot_e[:, cols]) * dsp
                       + lax.dot_general(cg, dye[:, cols], _TN, preferred_element_type=F32))
            dxs_parts.append(dxp)
        ds_ref[g] = jnp.concatenate(nds, axis=1)
        dx = jnp.concatenate(dxs_parts, axis=1)
        dcbb = dcb.astype(BF16)
        dc_ref[...] = dcg + jnp.dot(dcbb, bg, preferred_element_type=F32)
        db_ref[...] = dbg + lax.dot_general(dcbb, cg, _TN, preferred_element_type=F32)
        dxs_ref[...] = dx * dt_e + dyv * dsk_ref[...]
        w1 = y_ref[...] * dyb.astype(F32) - xb.astype(F32) * dx
        dcs = lax.dot_general(w1, eg, _NT, preferred_element_type=F32, precision=HIGHEST)
        dtot = lax.dot_general(jnp.broadcast_to(dtot_lane, (SUBLANE, gw)), eg, _NT,
                               preferred_element_type=F32, precision=HIGHEST)[0:1, :]
        rowl = lax.broadcasted_iota(jnp.int32, (ln, LANE), 0)
        dcs = jnp.where(rowl == ln - 1, dcs + dtot, dcs)
        row = lax.broadcasted_iota(jnp.int32, (ln, ln), 0)
        col = lax.broadcasted_iota(jnp.int32, (ln, ln), 1)
        dadt = jnp.dot((row <= col).astype(F32), dcs, preferred_element_type=F32, precision=HIGHEST)
        ddt = a_neg * dadt + lax.dot_general(dx * xs, eg, _NT, preferred_element_type=F32, precision=HIGHEST)
        dal = _colsum(dadt * dtv) * a_neg

        @pl.when(g == 0)
        def _():
            ddt_ref[...] = ddt

        @pl.when(g > 0)
        def _():
            ddt_ref[...] += ddt

        @pl.when((step == 0) & (g == 0))
        def _():
            dalog_ref[...] = dal

        @pl.when((step > 0) | (g > 0))
        def _():
            dalog_ref[...] += dal

    def rc(step):
        return nc - 1 - step

    in_specs = [pl.BlockSpec((ln, gw), lambda t, g: (rc(t), g)),
                pl.BlockSpec((ln, SSD_STATE), lambda t, g: (rc(t), boff + g)),
                pl.BlockSpec((ln, SSD_STATE), lambda t, g: (rc(t), boff + n_groups + g)),
                pl.BlockSpec((ln, LANE), lambda t, g: (rc(t), 0)),
                pl.BlockSpec((1, LANE), lambda t, g: (0, 0)),
                pl.BlockSpec((ln, gw), lambda t, g: (rc(t), g)),
                pl.BlockSpec((ln, gw), lambda t, g: (rc(t), g)),
                pl.BlockSpec((1, SSD_STATE, gw), lambda t, g: (rc(t), 0, g)),
                pl.BlockSpec((1, SSD_STATE, gw), lambda t, g: (jnp.minimum(rc(t) + 1, nc - 1), 0, g)),
                pl.BlockSpec((1, gw), lambda t, g: (0, g))]
    out_specs = [pl.BlockSpec((ln, gw), lambda t, g: (rc(t), g)),
                 pl.BlockSpec((ln, SSD_STATE), lambda t, g: (rc(t), g)),
                 pl.BlockSpec((ln, SSD_STATE), lambda t, g: (rc(t), g)),
                 pl.BlockSpec((ln, LANE), lambda t, g: (rc(t), 0)),
                 pl.BlockSpec((1, LANE), lambda t, g: (0, 0))]
    gn = n_groups * SSD_STATE
    return pl.pallas_call(
        body, name="ssd_bwd", grid=(nc, n_groups), in_specs=in_specs, out_specs=out_specs,
        out_shape=[jax.ShapeDtypeStruct((s, inner), F32), jax.ShapeDtypeStruct((s, gn), F32),
                   jax.ShapeDtypeStruct((s, gn), F32), jax.ShapeDtypeStruct((s, LANE), F32),
                   jax.ShapeDtypeStruct((1, LANE), F32)],
        scratch_shapes=[pltpu.VMEM((n_groups, SSD_STATE, gw), F32), pltpu.VMEM((LANE, ln), F32)],
        compiler_params=_params("arbitrary", "arbitrary"),
    )(xbc, xbc, xbc, dt, alog, dy, y, sin, sin, dskip_e)


def _ssd_gate_norm_fwd(ysc, xbc, z, dskip_e, norm_g, n_groups):
    s, inner = ysc.shape
    gsz = inner // n_groups

    def body(i, n, y_ref, x_ref, z_ref, d_ref, g_ref, o_ref):
        y2 = (y_ref[...] + d_ref[...] * x_ref[...]) * _silu(z_ref[...])
        gg = g_ref[...]
        outs = []
        for k in range(n_groups):
            cols = slice(k * gsz, (k + 1) * gsz)
            v = y2[:, cols]
            r = lax.rsqrt(jnp.mean(v * v, axis=-1, keepdims=True) + NORM_EPS)
            outs.append(v * r * gg[:, cols])
        o_ref[...] = jnp.concatenate(outs, axis=1).astype(BF16)

    return _rowwise("ssd_gate_norm_fwd", body,
                    [("row", ysc, inner), ("row", xbc, inner), ("row", z, inner), ("vec", dskip_e, inner),
                     ("vec", norm_g, inner)], [("row", None, inner, inner, BF16)], s, 128)[0]


def _ssd_gate_norm_bwd(dyn, ysc, xbc, z, dskip_e, norm_g, n_groups):
    s, inner = ysc.shape
    gsz = inner // n_groups

    def body(i, n, dn_ref, y_ref, x_ref, z_ref, d_ref, g_ref, dy_ref, dz_ref, dg_ref, dd_ref):
        xs = x_ref[...]
        zz = z_ref[...]
        y = y_ref[...] + d_ref[...] * xs
        sz = _silu(zz)
        y2 = y * sz
        dn = dn_ref[...]
        gg = g_ref[...]
        dy2s, dgs = [], []
        for k in range(n_groups):
            cols = slice(k * gsz, (k + 1) * gsz)
            v = y2[:, cols]
            d = dn[:, cols]
            r = lax.rsqrt(jnp.mean(v * v, axis=-1, keepdims=True) + NORM_EPS)
            vh = v * r
            dgs.append(_colsum(d * vh))
            dvh = d * gg[:, cols]
            dy2s.append(r * (dvh - vh * jnp.mean(dvh * vh, axis=-1, keepdims=True)))
        dy2 = jnp.concatenate(dy2s, axis=1)
        dy = dy2 * sz
        dy_ref[...] = dy
        dz_ref[...] = (dy2 * y * _dsilu(zz)).astype(BF16)
        _acc(i, dg_ref, jnp.concatenate(dgs, axis=1))
        _acc(i, dd_ref, _colsum(dy * xs))

    return _rowwise("ssd_gate_norm_bwd", body,
                    [("row", dyn, inner), ("row", ysc, inner), ("row", xbc, inner), ("row", z, inner),
                     ("vec", dskip_e, inner), ("vec", norm_g, inner)],
                    [("row", None, inner, inner, F32), ("row", None, inner, inner, BF16),
                     ("acc", 1, inner, inner, F32), ("acc", 1, inner, inner, F32)], s, 128)


def _adamw(w, g, m, v, name):
    rows, c = w.shape
    bc1 = 1.0 - ADAM_B1 ** ADAM_STEP
    bc2 = 1.0 - ADAM_B2 ** ADAM_STEP

    def body(i, n, w_ref, g_ref, m_ref, v_ref, d_ref, mo_ref, vo_ref):
        gg = g_ref[...]
        mn = ADAM_B1 * m_ref[...] + (1.0 - ADAM_B1) * gg
        vn = ADAM_B2 * v_ref[...] + (1.0 - ADAM_B2) * (gg * gg)
        d_ref[...] = -ADAM_LR * ((mn / bc1) / (jnp.sqrt(vn / bc2) + ADAM_EPS) + ADAM_WD * w_ref[...])
        mo_ref[...] = mn
        vo_ref[...] = vn

    ts = 128 if rows % 128 == 0 else rows
    return _rowwise(name, body, [("row", w, c), ("row", g, c), ("row", m, c), ("row", v, c)],
                    [("row", None, c, c, F32)] * 3, rows, ts)


_ANY = pl.BlockSpec(memory_space=pl.ANY)


def _place():
    x, y, c = lax.axis_index("x"), lax.axis_index("y"), lax.axis_index("c")
    chips = [(1 - x, y), (x, 1 - y), (1 - x, 1 - y)]
    return x, y, c, chips


def _rcopy(src, dst, ssem, rsem, dev):
    return pltpu.make_async_remote_copy(src_ref=src, dst_ref=dst, send_sem=ssem, recv_sem=rsem, device_id=dev,
                                        device_id_type=MESH)


def _gather_chips(entries, name):
    n = len(entries)
    half = [e.shape[0] // 2 for e in entries]

    def body(*refs):
        ins, outs = refs[:n], refs[n:2 * n]
        send_a, recv_a, send_b, recv_b, loc = refs[2 * n:]
        x, y, c, chips = _place()
        q = 2 * x + y
        sib = (x, y, 1 - c)
        mine = [pltpu.make_async_copy(ins[e], outs[e].at[q], loc.at[e]) for e in range(n)]
        for cp in mine:
            cp.start()
        first = []
        for e in range(n):
            rows = pl.ds(c * half[e], half[e])
            for j, (cx, cy) in enumerate(chips):
                first.append(_rcopy(ins[e].at[rows], outs[e].at[q, rows], send_a.at[3 * e + j], recv_a.at[3 * e + j],
                                    (cx, cy, c)))
        for cp in first:
            cp.start()
        passed = []
        for e in range(n):
            rows = pl.ds(c * half[e], half[e])
            for j, (cx, cy) in enumerate(chips):
                blk = outs[e].at[2 * cx + cy, rows]
                _rcopy(blk, blk, send_a.at[3 * e + j], recv_a.at[3 * e + j], (cx, cy, c)).wait_recv()
                fw = _rcopy(blk, blk, send_b.at[3 * e + j], recv_b.at[3 * e + j], sib)
                fw.start()
                passed.append(fw)
        for e in range(n):
            rows = pl.ds((1 - c) * half[e], half[e])
            for j, (cx, cy) in enumerate(chips):
                blk = outs[e].at[2 * cx + cy, rows]
                _rcopy(blk, blk, send_b.at[3 * e + j], recv_b.at[3 * e + j], sib).wait_recv()
        for cp in first + passed:
            cp.wait_send()
        for cp in mine:
            cp.wait()

    return pl.pallas_call(
        body, name=name, in_specs=[_ANY] * n, out_specs=[_ANY] * n,
        out_shape=[jax.ShapeDtypeStruct((N_CHIPS,) + e.shape, e.dtype) for e in entries],
        scratch_shapes=[pltpu.SemaphoreType.DMA((3 * n,))] * 4 + [pltpu.SemaphoreType.DMA((n,))],
    )(*entries)


def _swap_halves(entries, name):
    n = len(entries)
    half = [e.shape[1] // 2 for e in entries]

    def body(*refs):
        ins, outs = refs[:n], refs[n:2 * n]
        send, recv = refs[2 * n:]
        x, y, c, _ = _place()
        cps = [_rcopy(ins[e].at[:, pl.ds((1 - c) * half[e], half[e]), :], outs[e], send.at[e], recv.at[e],
                      (x, y, 1 - c)) for e in range(n)]
        for cp in cps:
            cp.start()
        for cp in cps:
            cp.wait()

    return pl.pallas_call(
        body, name=name, in_specs=[_ANY] * n, out_specs=[_ANY] * n,
        out_shape=[jax.ShapeDtypeStruct((N_CHIPS, h, e.shape[2]), e.dtype) for e, h in zip(entries, half)],
        scratch_shapes=[pltpu.SemaphoreType.DMA((n,))] * 2,
    )(*entries)


def _scatter_chips(entries, name):
    n = len(entries)

    def body(*refs):
        ins, outs = refs[:n], refs[n:2 * n]
        send, recv, loc = refs[2 * n:]
        x, y, c, chips = _place()
        q = 2 * x + y
        mine = [pltpu.make_async_copy(ins[e].at[q], outs[e].at[q], loc.at[e]) for e in range(n)]
        for cp in mine:
            cp.start()
        cps = []
        for e in range(n):
            for j, (cx, cy) in enumerate(chips):
                cps.append(_rcopy(ins[e].at[2 * cx + cy], outs[e].at[q], send.at[3 * e + j], recv.at[3 * e + j],
                                  (cx, cy, c)))
        for cp in cps:
            cp.start()
        for e in range(n):
            for j, (cx, cy) in enumerate(chips):
                blk = outs[e].at[2 * cx + cy]
                _rcopy(blk, blk, send.at[3 * e + j], recv.at[3 * e + j], (cx, cy, c)).wait_recv()
        for cp in cps:
            cp.wait_send()
        for cp in mine:
            cp.wait()

    return pl.pallas_call(
        body, name=name, in_specs=[_ANY] * n, out_specs=[_ANY] * n,
        out_shape=[jax.ShapeDtypeStruct(e.shape, e.dtype) for e in entries],
        scratch_shapes=[pltpu.SemaphoreType.DMA((3 * n,))] * 2 + [pltpu.SemaphoreType.DMA((n,))],
    )(*entries)


def _join_halves(entries, slots, out_shapes, name):
    n = len(entries)
    no = len(out_shapes)

    def body(*refs):
        ins, outs = refs[:n], refs[n:n + no]
        send, recv, loc = refs[n + no:]
        x, y, c, _ = _place()
        cps, mine = [], []
        for e in range(n):
            o, layer = slots[e]
            r2 = entries[e].shape[0]
            dst = outs[o].at[layer, pl.ds(c * r2, r2)]
            mine.append(pltpu.make_async_copy(ins[e], dst, loc.at[e]))
            cps.append(_rcopy(ins[e], dst, send.at[e], recv.at[e], (x, y, 1 - c)))
        for cp in mine + cps:
            cp.start()
        for e in range(n):
            o, layer = slots[e]
            r2 = entries[e].shape[0]
            blk = outs[o].at[layer, pl.ds((1 - c) * r2, r2)]
            _rcopy(blk, blk, send.at[e], recv.at[e], (x, y, 1 - c)).wait_recv()
        for cp in cps:
            cp.wait_send()
        for cp in mine:
            cp.wait()

    return pl.pallas_call(
        body, name=name, in_specs=[_ANY] * n, out_specs=[_ANY] * no,
        out_shape=[jax.ShapeDtypeStruct(sh, F32) for sh in out_shapes],
        scratch_shapes=[pltpu.SemaphoreType.DMA((n,))] * 3,
    )(*entries)


def _gather_all(v, name):
    def body(v_ref, o_ref, send, recv, loc):
        x, y, c, _ = _place()
        me = 4 * x + 2 * y + c
        mine = pltpu.make_async_copy(v_ref, o_ref.at[me], loc)
        mine.start()
        peers = []
        for k in range(1, N_DEV):
            px = 1 - x if k & 4 else x
            py = 1 - y if k & 2 else y
            pc = 1 - c if k & 1 else c
            peers.append((px, py, pc))
        cps = [_rcopy(v_ref, o_ref.at[me], send.at[k], recv.at[k], peers[k]) for k in range(N_DEV - 1)]
        for cp in cps:
            cp.start()
        for k, (px, py, pc) in enumerate(peers):
            blk = o_ref.at[4 * px + 2 * py + pc]
            _rcopy(blk, blk, send.at[k], recv.at[k], (px, py, pc)).wait_recv()
        for cp in cps:
            cp.wait_send()
        mine.wait()

    return pl.pallas_call(
        body, name=name, in_specs=[_ANY], out_specs=_ANY, out_shape=jax.ShapeDtypeStruct((N_DEV,) + v.shape, v.dtype),
        scratch_shapes=[pltpu.SemaphoreType.DMA((N_DEV - 1,))] * 2 + [pltpu.SemaphoreType.DMA],
    )(v)


def _add_own_half(gst, rx, c_idx, name):
    _, r, cc = gst.shape
    r2 = r // 2
    tr = _tile(r2, 256, 16)
    g4 = gst.reshape(N_CHIPS, 2, r2, cc)

    def body(c_ref, g_ref, r_ref, o_ref):
        o_ref[...] = (g_ref[0].astype(F32) + r_ref[...].astype(F32)).astype(BF16)

    grid_spec = pltpu.PrefetchScalarGridSpec(
        num_scalar_prefetch=1, grid=(N_CHIPS, r2 // tr),
        in_specs=[pl.BlockSpec((1, 1, tr, cc), lambda k, i, c_ref: (k, c_ref[0], i, 0)),
                  pl.BlockSpec((1, tr, cc), lambda k, i, c_ref: (k, i, 0))],
        out_specs=pl.BlockSpec((1, tr, cc), lambda k, i, c_ref: (k, i, 0)))
    return pl.pallas_call(body, name=name, grid_spec=grid_spec, out_shape=jax.ShapeDtypeStruct((N_CHIPS, r2, cc), BF16),
                          compiler_params=_params("parallel", "parallel"))(c_idx, g4, rx)


def _sum_slots(st, name):
    k, r, cc = st.shape
    tr = _tile(r, 256, 8)

    def body(s_ref, o_ref):
        acc = s_ref[0].astype(F32)
        for j in range(1, k):
            acc = acc + s_ref[j].astype(F32)
        o_ref[...] = acc

    return pl.pallas_call(body, name=name, grid=(r // tr,), in_specs=[pl.BlockSpec((k, tr, cc), lambda i: (0, i, 0))],
                          out_specs=pl.BlockSpec((tr, cc), lambda i: (i, 0)),
                          out_shape=jax.ShapeDtypeStruct((r, cc), F32), compiler_params=_params("parallel"))(st)


def _pack(arrs, rows_mult=2 * SUBLANE):
    flat = jnp.concatenate([a.reshape(-1).astype(F32) for a in arrs])
    quantum = rows_mult * LANE
    padded = -(-flat.shape[0] // quantum) * quantum
    return jnp.pad(flat, (0, padded - flat.shape[0])).reshape(-1, LANE)


def _unpack(buf, shapes):
    flat = buf.reshape(-1)
    out, off = [], 0
    for sh in shapes:
        size = 1
        for d in sh:
            size *= d
        out.append(flat[off:off + size].reshape(sh))
        off += size
    return out


def kernel(x, p, norm_mix_g, norm_ffn_g, norm_ple_g, final_norm_g, a_w_in, a_conv_w, a_conv_b, a_w_gate_r, a_b_gate_r, a_w_gate_i, a_b_gate_i, a_lambda, a_w_out, b_w_in, b_conv_w, b_conv_b, b_dt_bias, b_a_log, b_d_skip, b_norm_g, b_w_out, ffn_w_gate, ffn_w_up, ffn_w_down, ple_w_proj, ple_w_gate, loss_target, m_norm_mix_g, m_norm_ffn_g, m_norm_ple_g, m_final_norm_g, m_a_w_in, m_a_conv_w, m_a_conv_b, m_a_w_gate_r, m_a_b_gate_r, m_a_w_gate_i, m_a_b_gate_i, m_a_lambda, m_a_w_out, m_b_w_in, m_b_conv_w, m_b_conv_b, m_b_dt_bias, m_b_a_log, m_b_d_skip, m_b_norm_g, m_b_w_out, m_ffn_w_gate, m_ffn_w_up, m_ffn_w_down, m_ple_w_proj, m_ple_w_gate, v_norm_mix_g, v_norm_ffn_g, v_norm_ple_g, v_final_norm_g, v_a_w_in, v_a_conv_w, v_a_conv_b, v_a_w_gate_r, v_a_b_gate_r, v_a_w_gate_i, v_a_b_gate_i, v_a_lambda, v_a_w_out, v_b_w_in, v_b_conv_w, v_b_conv_b, v_b_dt_bias, v_b_a_log, v_b_d_skip, v_b_norm_g, v_b_w_out, v_ffn_w_gate, v_ffn_w_up, v_ffn_w_down, v_ple_w_proj, v_ple_w_gate):
    names = ["norm_mix_g", "norm_ffn_g", "norm_ple_g", "final_norm_g", "a_w_in", "a_conv_w", "a_conv_b", "a_w_gate_r",
             "a_b_gate_r", "a_w_gate_i", "a_b_gate_i", "a_lambda", "a_w_out", "b_w_in", "b_conv_w", "b_conv_b",
             "b_dt_bias", "b_a_log", "b_d_skip", "b_norm_g", "b_w_out", "ffn_w_gate", "ffn_w_up", "ffn_w_down",
             "ple_w_proj", "ple_w_gate"]
    w_in = dict(zip(names, [norm_mix_g, norm_ffn_g, norm_ple_g, final_norm_g, a_w_in, a_conv_w, a_conv_b, a_w_gate_r,
                            a_b_gate_r, a_w_gate_i, a_b_gate_i, a_lambda, a_w_out, b_w_in, b_conv_w, b_conv_b,
                            b_dt_bias, b_a_log, b_d_skip, b_norm_g, b_w_out, ffn_w_gate, ffn_w_up, ffn_w_down,
                            ple_w_proj, ple_w_gate]))
    m_in = dict(zip(names, [m_norm_mix_g, m_norm_ffn_g, m_norm_ple_g, m_final_norm_g, m_a_w_in, m_a_conv_w,
                            m_a_conv_b, m_a_w_gate_r, m_a_b_gate_r, m_a_w_gate_i, m_a_b_gate_i, m_a_lambda,
                            m_a_w_out, m_b_w_in, m_b_conv_w, m_b_conv_b, m_b_dt_bias, m_b_a_log, m_b_d_skip,
                            m_b_norm_g, m_b_w_out, m_ffn_w_gate, m_ffn_w_up, m_ffn_w_down, m_ple_w_proj,
                            m_ple_w_gate]))
    v_in = dict(zip(names, [v_norm_mix_g, v_norm_ffn_g, v_norm_ple_g, v_final_norm_g, v_a_w_in, v_a_conv_w,
                            v_a_conv_b, v_a_w_gate_r, v_a_b_gate_r, v_a_w_gate_i, v_a_b_gate_i, v_a_lambda,
                            v_a_w_out, v_b_w_in, v_b_conv_w, v_b_conv_b, v_b_dt_bias, v_b_a_log, v_b_d_skip,
                            v_b_norm_g, v_b_w_out, v_ffn_w_gate, v_ffn_w_up, v_ffn_w_down, v_ple_w_proj,
                            v_ple_w_gate]))

    s, d = x.shape[1], x.shape[2]
    depth = norm_mix_g.shape[0]
    assert depth == 2
    q_idx = 2 * lax.axis_index("x") + lax.axis_index("y")
    c_idx = lax.axis_index("c").astype(jnp.int32).reshape(1)

    inner = b_w_out.shape[1] * N_CHIPS
    n_heads = inner // SSD_HEAD_DIM
    n_groups = n_heads // SSD_HEADS_PER_GROUP
    gn = n_groups * SSD_STATE
    xbcw = inner + 2 * gn
    assert b_conv_w.shape[2] * N_CHIPS == xbcw and n_heads <= LANE

    big = [("a_w_in", "col"), ("a_w_gate_r", "gate"), ("a_w_gate_i", "gate"), ("a_w_out", "row"),
           ("ffn_w_gate", "col"), ("ffn_w_up", "col"), ("ffn_w_down", "row"), ("ple_w_proj", "col"),
           ("ple_w_gate", "row"), ("b_w_in", "col"), ("b_w_out", "row")]
    kind_of = dict(big)

    def shard2d(name, arr):
        if kind_of[name] == "gate":
            return [arr[l].reshape(-1, arr.shape[-1]) for l in range(arr.shape[0])]
        return [arr[l] for l in range(arr.shape[0])]

    small_sharded = ["a_conv_w", "a_b_gate_r", "a_b_gate_i", "b_conv_w", "b_conv_b", "b_norm_g"]
    small_pack = _pack([w_in[nm] for nm in small_sharded], rows_mult=16)

    entries, entry_keys = [], []
    for nm, _ in big:
        for l, sh in enumerate(shard2d(nm, w_in[nm])):
            entries.append(sh.astype(BF16))
            entry_keys.append((nm, l))
    entries.append(small_pack)
    gathered = _gather_chips(entries, "gather_weights")
    small_st = gathered[-1]
    wst = dict(zip(entry_keys, gathered[:-1]))

    def whole(nm, l):
        st = wst[(nm, l)]
        kind = kind_of[nm]
        if kind == "row":
            return st.reshape(-1, st.shape[-1])
        if kind == "col":
            return jnp.concatenate([st[k] for k in range(N_CHIPS)], axis=1)
        heads = w_in[nm].shape[1]
        return st.reshape(N_CHIPS, heads, -1, st.shape[-1]).transpose(1, 0, 2, 3).reshape(heads, -1, st.shape[-1])

    small_parts = [_unpack(small_st[k], [w_in[nm].shape for nm in small_sharded]) for k in range(N_CHIPS)]
    small_full = {nm: jnp.concatenate([small_parts[k][i] for k in range(N_CHIPS)], axis=-1)
                  for i, nm in enumerate(small_sharded)}
    a_cw = small_full["a_conv_w"][0]
    a_br = small_full["a_b_gate_r"][0].reshape(1, -1)
    a_bi = small_full["a_b_gate_i"][0].reshape(1, -1)
    b_cw = small_full["b_conv_w"][0]
    b_cb = small_full["b_conv_b"]
    b_ng = small_full["b_norm_g"]

    def pad_lanes(v):
        return jnp.pad(v, ((0, 0), (0, LANE - v.shape[1])))

    dt_bias = pad_lanes(b_dt_bias)
    a_log = pad_lanes(b_a_log)
    dskip_e = jnp.repeat(b_d_skip, SSD_HEAD_DIM, axis=1)

    w_a_in = whole("a_w_in", 0)
    w_ax, w_ay = w_a_in[:, :d], w_a_in[:, d:]
    w_ar, w_ai, w_ao = whole("a_w_gate_r", 0), whole("a_w_gate_i", 0), whole("a_w_out", 0)
    w_b_in = whole("b_w_in", 0)
    w_bz, w_bx = w_b_in[:, :inner], w_b_in[:, inner:inner + xbcw]
    w_bd = pad_lanes(w_b_in[:, inner + xbcw:])
    w_bo = whole("b_w_out", 0)
    w_fg = [whole("ffn_w_gate", l) for l in range(depth)]
    w_fu = [whole("ffn_w_up", l) for l in range(depth)]
    w_fd = [whole("ffn_w_down", l) for l in range(depth)]
    w_pp = [whole("ple_w_proj", l) for l in range(depth)]
    w_pg = [whole("ple_w_gate", l) for l in range(depth)]

    grads = {}

    h0 = x[0]
    g_mix = [norm_mix_g[l:l + 1] for l in range(depth)]
    g_ffn = [norm_ffn_g[l:l + 1] for l in range(depth)]
    g_ple = [norm_ple_g[l:l + 1] for l in range(depth)]
    g_fin = final_norm_g.reshape(1, -1)

    u0 = _rmsnorm_fwd(h0, g_mix[0], "norm_mix0")
    xr_pre = _mm(u0, w_ax, name="lru_in_x")
    yg = _mm(u0, w_ay, name="lru_in_y")
    xr = _conv_fwd(xr_pre, a_cw, a_conv_b, False, "lru_conv")
    lru_a, lru_b = _lru_gates_fwd(xr, w_ar, w_ai, a_br, a_bi, a_lambda)
    hs = _scan(lru_a, lru_b, False, "lru_scan")
    y_lru = _lru_out_fwd(hs, yg)
    h_mix = [_mm(y_lru, w_ao, add=h0, name="lru_out"), None]

    def ffn_ple_fwd(h_in, l):
        n_f = _rmsnorm_fwd(h_in, g_ffn[l], f"norm_ffn{l}")
        gt = _mm(n_f, w_fg[l], name=f"ffn_gate{l}")
        up = _mm(n_f, w_fu[l], name=f"ffn_up{l}")
        act = _swiglu_act(gt, up)
        h_f = _mm(act, w_fd[l], add=h_in, name=f"ffn_down{l}")
        n_p = _rmsnorm_fwd(h_f, g_ple[l], f"norm_ple{l}")
        gp = _mm(n_p, w_pg[l], name=f"ple_gate{l}")
        pp = _mm(p[l, 0], w_pp[l], name=f"ple_proj{l}")
        h_out = _ple_fwd(h_f, gp, pp)
        return h_out, dict(h_in=h_in, n_f=n_f, gt=gt, up=up, act=act, h_f=h_f, n_p=n_p, gp=gp, pp=pp)

    h_l0, sv0 = ffn_ple_fwd(h_mix[0], 0)

    u1 = _rmsnorm_fwd(h_l0, g_mix[1], "norm_mix1")
    z = _mm(u1, w_bz, name="ssd_in_z")
    xbc_pre = _mm(u1, w_bx, name="ssd_in_xbc")
    dt_pre = _mm(u1, w_bd, name="ssd_in_dt")
    xbc = _conv_fwd(xbc_pre, b_cw, b_cb, True, "ssd_conv")
    dt = _dt_fwd(dt_pre, dt_bias, n_heads)
    ysc, s_in = _ssd_fwd(xbc, dt, a_log, inner, n_groups)
    yn = _ssd_gate_norm_fwd(ysc, xbc, z, dskip_e, b_ng, n_groups)
    h_mix[1] = _mm(yn, w_bo, add=h_l0, name="ssd_out")
    h_l1, sv1 = ffn_ple_fwd(h_mix[1], 1)

    dh, dg_fin, loss_row = _final_loss_bwd(h_l1, g_fin, loss_target[0])

    d_norm_ffn, d_norm_ple, d_norm_mix = [None] * depth, [None] * depth, [None] * depth
    for nm in ("ffn_w_gate", "ffn_w_up", "ffn_w_down", "ple_w_proj", "ple_w_gate"):
        grads[nm] = [None] * depth

    def ffn_ple_bwd(dh_out, sv, l):
        dgp, dpp = _ple_bwd(dh_out, sv["gp"], sv["pp"])
        grads["ple_w_gate"][l] = _mm(sv["n_p"], dgp, ta=True, out_dtype=BF16, name=f"ple_gate_dw{l}")
        grads["ple_w_proj"][l] = _mm(p[l, 0], dpp, ta=True, out_dtype=BF16, name=f"ple_proj_dw{l}")
        dn = _mm(dgp, w_pg[l], tb=True, name=f"ple_gate_dx{l}")
        dh_f, d_norm_ple[l] = _rmsnorm_bwd(dn, sv["h_f"], g_ple[l], dh_out, f"norm_ple_bwd{l}")
        grads["ffn_w_down"][l] = _mm(sv["act"], dh_f, ta=True, out_dtype=BF16, name=f"ffn_down_dw{l}")
        dact = _mm(dh_f, w_fd[l], tb=True, name=f"ffn_down_dx{l}")
        dgt, dup = _swiglu_bwd(dact, sv["gt"], sv["up"])
        grads["ffn_w_gate"][l] = _mm(sv["n_f"], dgt, ta=True, out_dtype=BF16, name=f"ffn_gate_dw{l}")
        grads["ffn_w_up"][l] = _mm(sv["n_f"], dup, ta=True, out_dtype=BF16, name=f"ffn_up_dw{l}")
        dn = _mm(dgt, w_fg[l], tb=True, name=f"ffn_gate_dx{l}")
        dn = _mm(dup, w_fu[l], tb=True, add=dn, name=f"ffn_up_dx{l}")
        dh_in, d_norm_ffn[l] = _rmsnorm_bwd(dn, sv["h_in"], g_ffn[l], dh_f, f"norm_ffn_bwd{l}")
        return dh_in

    dh = ffn_ple_bwd(dh, sv1, 1)

    grads["b_w_out"] = [_mm(yn, dh, ta=True, out_dtype=BF16, name="ssd_out_dw")]
    dyn = _mm(dh, w_bo, tb=True, name="ssd_out_dx")
    dy_ssd, dz, d_b_norm_g, dd_lane = _ssd_gate_norm_bwd(dyn, ysc, xbc, z, dskip_e, b_ng, n_groups)
    dxs, d_bm, d_cm, ddt, d_a_log = _ssd_bwd(xbc, dt, a_log, dy_ssd, ysc, s_in, dskip_e, inner, n_groups)
    dxbc = jnp.concatenate([dxs, d_bm, d_cm], axis=1)
    dconv = _silu_conv_bwd_pre(dxbc, xbc_pre, b_cw, b_cb, "ssd_conv_bwd_pre")
    dxbc_pre, d_b_conv_w, d_b_conv_b = _conv_bwd(dconv, xbc_pre, b_cw, "ssd_conv_bwd")
    ddt_pre, d_dt_bias = _dt_bwd(ddt, dt_pre, dt_bias, n_heads)
    gw_bz = _mm(u1, dz, ta=True, out_dtype=BF16, name="ssd_in_z_dw")
    gw_bx = _mm(u1, dxbc_pre, ta=True, out_dtype=BF16, name="ssd_in_xbc_dw")
    gw_bd = _mm(u1, ddt_pre, ta=True, out_dtype=BF16, name="ssd_in_dt_dw")
    grads["b_w_in"] = [jnp.concatenate([gw_bz, gw_bx, gw_bd[:, :n_heads]], axis=1)]
    du = _mm(dz, w_bz, tb=True, name="ssd_in_z_dx")
    du = _mm(dxbc_pre, w_bx, tb=True, add=du, name="ssd_in_xbc_dx")
    du = _mm(ddt_pre, w_bd, tb=True, add=du, name="ssd_in_dt_dx")
    dh, d_norm_mix[1] = _rmsnorm_bwd(du, h_l0, g_mix[1], dh, "norm_mix_bwd1")

    dh = ffn_ple_bwd(dh, sv0, 0)

    grads["a_w_out"] = [_mm(y_lru, dh, ta=True, out_dtype=BF16, name="lru_out_dw")]
    dy_lru = _mm(dh, w_ao, tb=True, name="lru_out_dx")
    dhs, dyg = _lru_out_bwd(dy_lru, hs, yg)
    g_scan = _scan(lru_a, dhs, True, "lru_scan_bwd")
    dxr, d_wr, d_wi, d_br, d_bi, d_lam = _lru_gates_bwd(xr, g_scan, hs, w_ar, w_ai, a_br, a_bi, a_lambda)
    dxr_pre, d_a_conv_w, d_a_conv_b = _conv_bwd(dxr, xr_pre, a_cw, "lru_conv_bwd")
    gw_ax = _mm(u0, dxr_pre, ta=True, out_dtype=BF16, name="lru_in_x_dw")
    gw_ay = _mm(u0, dyg, ta=True, out_dtype=BF16, name="lru_in_y_dw")
    grads["a_w_in"] = [jnp.concatenate([gw_ax, gw_ay], axis=1)]
    grads["a_w_gate_r"] = [d_wr.astype(BF16)]
    grads["a_w_gate_i"] = [d_wi.astype(BF16)]
    du = _mm(dxr_pre, w_ax, tb=True, name="lru_in_x_dx")
    du = _mm(dyg, w_ay, tb=True, add=du, name="lru_in_y_dx")
    grad_x, d_norm_mix[0] = _rmsnorm_bwd(du, h0, g_mix[0], dh, "norm_mix_bwd0")

    def stacked(nm, gfull):
        kind = kind_of[nm]
        if kind == "row":
            return gfull.reshape(N_CHIPS, -1, gfull.shape[-1])
        if kind == "col":
            n_loc = gfull.shape[1] // N_CHIPS
            return jnp.stack([gfull[:, k * n_loc:(k + 1) * n_loc] for k in range(N_CHIPS)])
        heads, bw, _ = gfull.shape
        return gfull.reshape(heads, N_CHIPS, bw // N_CHIPS, bw).transpose(1, 0, 2, 3).reshape(N_CHIPS, -1, bw)

    gst = [stacked(nm, grads[nm][l]) for nm, l in entry_keys]
    from_sib = _swap_halves(gst, "reduce_swap_halves")
    chip_sum = [_add_own_half(g, r, c_idx, f"reduce_add_{nm}{l}") for g, r, (nm, l) in zip(gst, from_sib, entry_keys)]
    from_chips = _scatter_chips(chip_sum, "reduce_scatter_chips")
    reduced = [_sum_slots(t, f"reduce_sum_{nm}{l}") for t, (nm, l) in zip(from_chips, entry_keys)]
    out_index = {nm: i for i, (nm, _) in enumerate(big)}
    slots = [(out_index[nm], l) for nm, l in entry_keys]
    out_shapes = []
    for nm, _ in big:
        sh = shard2d(nm, w_in[nm])
        out_shapes.append((len(sh),) + sh[0].shape)
    g_big = dict(zip([nm for nm, _ in big], _join_halves(reduced, slots, out_shapes, "reduce_join_halves")))

    small_full_grads = {
        "norm_mix_g": jnp.concatenate(d_norm_mix, axis=0), "norm_ffn_g": jnp.concatenate(d_norm_ffn, axis=0),
        "norm_ple_g": jnp.concatenate(d_norm_ple, axis=0), "final_norm_g": dg_fin[0],
        "a_conv_w": d_a_conv_w[None], "a_conv_b": d_a_conv_b,
        "a_b_gate_r": d_br.reshape(a_b_gate_r.shape[0], a_b_gate_r.shape[1], -1),
        "a_b_gate_i": d_bi.reshape(a_b_gate_i.shape[0], a_b_gate_i.shape[1], -1),
        "a_lambda": d_lam, "b_conv_w": d_b_conv_w[None], "b_conv_b": d_b_conv_b,
        "b_dt_bias": d_dt_bias[:, :n_heads], "b_a_log": d_a_log[:, :n_heads],
        "b_d_skip": dd_lane.reshape(1, n_heads, SSD_HEAD_DIM).sum(axis=-1), "b_norm_g": d_b_norm_g,
    }
    small_names = list(small_full_grads)
    small_shapes = [small_full_grads[nm].shape for nm in small_names]
    packed = _pack([loss_row] + [small_full_grads[nm] for nm in small_names])
    total = _sum_slots(_gather_all(packed, "gather_small_grads"), "sum_small_grads")
    parts = _unpack(total, [(1, LANE)] + small_shapes)
    loss = parts[0][0, 0]
    g_small = {}
    for nm, gfull in zip(small_names, parts[1:]):
        if nm in small_sharded:
            n_loc = w_in[nm].shape[-1]
            gfull = lax.dynamic_slice_in_dim(gfull, q_idx * n_loc, n_loc, axis=gfull.ndim - 1)
        g_small[nm] = gfull

    grad_out, delta_out, m_out, v_out = {}, {}, {}, {}
    for nm, _ in big:
        shape = w_in[nm].shape
        gfull = g_big[nm]
        cols = gfull.shape[-1]
        dl, mn, vn = _adamw(w_in[nm].reshape(-1, cols), gfull.reshape(-1, cols), m_in[nm].reshape(-1, cols),
                            v_in[nm].reshape(-1, cols), f"adamw_{nm}")
        grad_out[nm], delta_out[nm] = gfull.reshape(shape), dl.reshape(shape)
        m_out[nm], v_out[nm] = mn.reshape(shape), vn.reshape(shape)
    sm_shapes = [w_in[nm].shape for nm in small_names]
    dl, mn, vn = _adamw(_pack([w_in[nm] for nm in small_names]), _pack([g_small[nm] for nm in small_names]),
                        _pack([m_in[nm] for nm in small_names]), _pack([v_in[nm] for nm in small_names]),
                        "adamw_small")
    for nm, a, b_, c_ in zip(small_names, _unpack(dl, sm_shapes), _unpack(mn, sm_shapes), _unpack(vn, sm_shapes)):
        grad_out[nm] = g_small[nm].reshape(w_in[nm].shape)
        delta_out[nm], m_out[nm], v_out[nm] = a, b_, c_

    return (loss, grad_x[None], *[grad_out[nm] for nm in names], *[delta_out[nm] for nm in names],
            *[m_out[nm] for nm in names], *[v_out[nm] for nm in names])
```

```python
import functools

import jax
import jax.numpy as jnp
from jax import lax
from jax.experimental import pallas as pl
from jax.experimental.pallas import tpu as pltpu

F32 = jnp.float32
BF16 = jnp.bfloat16
MESH = pl.DeviceIdType.MESH
HIGHEST = lax.Precision.HIGHEST

NORM_EPS = 1e-6
LRU_C = 8.0
CONV_WIDTH = 4
SSD_HEAD_DIM = 64
SSD_STATE = 128
SSD_CHUNK = 128
SSD_HEADS_PER_GROUP = 8
LANE = 128
SUBLANE = 8
N_CHIPS = 4
N_DEV = 8
VMEM_LIMIT = 48 * 1024 * 1024

ADAM_LR = 0.001
ADAM_B1 = 0.9
ADAM_B2 = 0.999
ADAM_EPS = 1e-08
ADAM_WD = 0.01
ADAM_STEP = 10


def _tile(n, cap, mult=LANE):
    best = None
    for t in range(mult, min(n, cap) + 1, mult):
        if n % t == 0:
            best = t
    return best if best is not None else n


def _params(*sem):
    return pltpu.CompilerParams(dimension_semantics=sem, vmem_limit_bytes=VMEM_LIMIT)


def _mm(a, b, *, ta=False, tb=False, add=None, out_dtype=F32, stacked_b=False, stacked_out=False, name):
    if ta:
        kd, m = a.shape
    else:
        m, kd = a.shape
    n_loc = None
    if stacked_b:
        _, kb, n_loc = b.shape
        if tb:
            n, kb = kb, N_CHIPS * n_loc
        else:
            n = N_CHIPS * n_loc
    elif tb:
        n, kb = b.shape
    else:
        kb, n = b.shape
    assert kd == kb, (a.shape, b.shape, ta, tb)
    tm = _tile(m, 1024)
    tn = _tile(n, 1024)
    tk = _tile(kd, 512)
    if stacked_b and tb:
        tk = _tile(n_loc, 1408)
    elif stacked_b:
        tn = _tile(n_loc, 1408)
    if stacked_out:
        n_loc = n // N_CHIPS
        tn = _tile(n_loc, 1408)
    nk = kd // tk
    dims = (((0 if ta else 1,), (1 if tb else 0,)), ((), ()))

    def body(*refs):
        if add is None:
            a_ref, b_ref, o_ref, acc_ref = refs
        else:
            a_ref, b_ref, add_ref, o_ref, acc_ref = refs
        k = pl.program_id(2)

        @pl.when(k == 0)
        def _():
            acc_ref[...] = jnp.zeros_like(acc_ref)

        bv = b_ref[0] if stacked_b else b_ref[...]
        acc_ref[...] += lax.dot_general(a_ref[...].astype(BF16), bv.astype(BF16), dims, preferred_element_type=F32)

        @pl.when(k == nk - 1)
        def _():
            r = acc_ref[...]
            if add is not None:
                r = r + add_ref[...]
            if stacked_out:
                o_ref[0] = r.astype(out_dtype)
            else:
                o_ref[...] = r.astype(out_dtype)

    a_spec = pl.BlockSpec((tk, tm), lambda i, j, k: (k, i)) if ta else pl.BlockSpec((tm, tk), lambda i, j, k: (i, k))
    if stacked_b and tb:
        per = n_loc // tk
        b_spec = pl.BlockSpec((1, tn, tk), lambda i, j, k: (k // per, j, k % per))
    elif stacked_b:
        per = n_loc // tn
        b_spec = pl.BlockSpec((1, tk, tn), lambda i, j, k: (j // per, k, j % per))
    elif tb:
        b_spec = pl.BlockSpec((tn, tk), lambda i, j, k: (j, k))
    else:
        b_spec = pl.BlockSpec((tk, tn), lambda i, j, k: (k, j))
    o_spec = pl.BlockSpec((tm, tn), lambda i, j, k: (i, j))
    in_specs = [a_spec, b_spec] + ([o_spec] if add is not None else [])
    args = (a, b) + ((add,) if add is not None else ())
    if stacked_out:
        per_o = n_loc // tn
        out_spec = pl.BlockSpec((1, tm, tn), lambda i, j, k: (j // per_o, i, j % per_o))
        out_shape = jax.ShapeDtypeStruct((N_CHIPS, m, n_loc), out_dtype)
    else:
        out_spec, out_shape = o_spec, jax.ShapeDtypeStruct((m, n), out_dtype)
    return pl.pallas_call(
        body, name=name, grid=(m // tm, n // tn, nk), in_specs=in_specs, out_specs=out_spec, out_shape=out_shape,
        scratch_shapes=[pltpu.VMEM((tm, tn), F32)],
        compiler_params=_params("parallel", "parallel", "arbitrary"))(*args)


def _rowwise(name, body, ins, outs, nrows, ts, ncol=1):
    ts = min(ts, nrows)
    nrow = nrows // ts
    hb = ts // SUBLANE
    nb8 = nrows // SUBLANE
    in_specs, args = [], []
    for kind, arr, cb in ins:
        if kind == "row":
            spec = pl.BlockSpec((ts, cb), lambda j, i: (i, j))
        elif kind == "prev":
            spec = pl.BlockSpec((SUBLANE, cb), lambda j, i: (jnp.maximum(i * hb - 1, 0), j))
        elif kind == "next":
            spec = pl.BlockSpec((SUBLANE, cb), lambda j, i: (jnp.minimum((i + 1) * hb, nb8 - 1), j))
        else:
            spec = pl.BlockSpec((arr.shape[0], cb), lambda j, i: (0, j))
        in_specs.append(spec)
        args.append(arr)
    out_specs, out_shape = [], []
    for kind, rows, ctot, cb, dt in outs:
        if kind == "row":
            out_shape.append(jax.ShapeDtypeStruct((nrows, ctot), dt))
            out_specs.append(pl.BlockSpec((ts, cb), lambda j, i: (i, j)))
        else:
            out_shape.append(jax.ShapeDtypeStruct((rows, ctot), dt))
            out_specs.append(pl.BlockSpec((rows, cb), lambda j, i: (0, j)))

    def kern(*refs):
        body(pl.program_id(1), nrow, *refs)

    return pl.pallas_call(kern, name=name, grid=(ncol, nrow), in_specs=in_specs, out_specs=out_specs,
                          out_shape=out_shape, compiler_params=_params("parallel", "arbitrary"))(*args)


def _colsum(x):
    return jnp.sum(x, axis=0, keepdims=True)


def _acc(i, ref, val):
    @pl.when(i == 0)
    def _():
        ref[...] = val

    @pl.when(i > 0)
    def _():
        ref[...] += val


def _shift_down(x, halo, k):
    xx = jnp.concatenate([halo, x], axis=0)
    return pltpu.roll(xx, k, axis=0)[SUBLANE:, :]


def _shift_up(x, halo, k):
    xx = jnp.concatenate([x, halo], axis=0)
    n = xx.shape[0]
    return pltpu.roll(xx, n - k, axis=0)[: x.shape[0], :]


def _sigmoid(x):
    return 1.0 / (1.0 + jnp.exp(-x))


def _silu(x):
    return x * _sigmoid(x)


def _dsilu(x):
    s = _sigmoid(x)
    return s * (1.0 + x * (1.0 - s))


_GELU_K = 0.7978845608028654
_GELU_C = 0.044715


def _gelu(x):
    return 0.5 * x * (1.0 + jnp.tanh(_GELU_K * (x + _GELU_C * x * x * x)))


def _dgelu(x):
    t = jnp.tanh(_GELU_K * (x + _GELU_C * x * x * x))
    return 0.5 * (1.0 + t) + 0.5 * x * (1.0 - t * t) * _GELU_K * (1.0 + 3.0 * _GELU_C * x * x)


def _softplus(x):
    return jnp.maximum(x, 0.0) + jnp.log1p(jnp.exp(-jnp.abs(x)))


def _neg_expm1(x):
    poly = -x * (1.0 + x * (0.5 + x * (1.0 / 6.0 + x * (1.0 / 24.0 + x * (1.0 / 120.0)))))
    return jnp.where(x > -0.05, poly, 1.0 - jnp.exp(x))


def _rmsnorm_fwd(h, g, name):
    s, d = h.shape

    def body(i, n, h_ref, g_ref, o_ref):
        x = h_ref[...]
        r = lax.rsqrt(jnp.mean(x * x, axis=-1, keepdims=True) + NORM_EPS)
        o_ref[...] = (x * r * g_ref[...]).astype(BF16)

    return _rowwise(name, body, [("row", h, d), ("vec", g, d)], [("row", None, d, d, BF16)], s, 256)[0]


def _rmsnorm_bwd(dn, h, g, dres, name):
    s, d = h.shape

    def body(i, n, dn_ref, h_ref, g_ref, dres_ref, dh_ref, dg_ref):
        x = h_ref[...]
        dy = dn_ref[...].astype(F32)
        r = lax.rsqrt(jnp.mean(x * x, axis=-1, keepdims=True) + NORM_EPS)
        xh = x * r
        _acc(i, dg_ref, _colsum(dy * xh))
        dxh = dy * g_ref[...]
        dh_ref[...] = dres_ref[...] + r * (dxh - xh * jnp.mean(dxh * xh, axis=-1, keepdims=True))

    return _rowwise(name, body, [("row", dn, d), ("row", h, d), ("vec", g, d), ("row", dres, d)],
                    [("row", None, d, d, F32), ("acc", 1, d, d, F32)], s, 256)


def _final_loss_bwd(h, g, tgt):
    s, d = h.shape

    def body(i, n, h_ref, g_ref, t_ref, dh_ref, dg_ref, loss_ref):
        x = h_ref[...]
        gg = g_ref[...]
        r = lax.rsqrt(jnp.mean(x * x, axis=-1, keepdims=True) + NORM_EPS)
        xh = x * r
        err = xh * gg - t_ref[...]
        part = 0.5 * jnp.sum(jnp.mean(err * err, axis=-1, keepdims=True), axis=0, keepdims=True)
        _acc(i, loss_ref, jnp.broadcast_to(part, (1, LANE)))
        dy = err * (1.0 / d)
        _acc(i, dg_ref, _colsum(dy * xh))
        dxh = dy * gg
        dh_ref[...] = r * (dxh - xh * jnp.mean(dxh * xh, axis=-1, keepdims=True))

    return _rowwise("final_loss_bwd", body, [("row", h, d), ("vec", g, d), ("row", tgt, d)],
                    [("row", None, d, d, F32), ("acc", 1, d, d, F32), ("acc", 1, LANE, LANE, F32)], s, 256)


def _conv_rows(x, halo, w, b):
    y = b + w[3:4, :] * x
    for k in range(CONV_WIDTH - 1):
        y = y + w[k:k + 1, :] * _shift_down(x, halo, CONV_WIDTH - 1 - k)
    return y


def _conv_fwd(x, w, b, silu, name):
    s, c = x.shape
    cb = _tile(c, 512)

    def body(i, n, x_ref, p_ref, w_ref, b_ref, o_ref):
        halo = jnp.where(i == 0, 0.0, p_ref[...])
        y = _conv_rows(x_ref[...], halo, w_ref[...], b_ref[...])
        o_ref[...] = _silu(y) if silu else y

    return _rowwise(name, body, [("row", x, cb), ("prev", x, cb), ("vec", w, cb), ("vec", b, cb)],
                    [("row", None, c, cb, F32)], s, 512, ncol=c // cb)[0]


def _silu_conv_bwd_pre(dy, x, w, b, name):
    s, c = x.shape
    cb = _tile(c, 512)

    def body(i, n, dy_ref, x_ref, p_ref, w_ref, b_ref, o_ref):
        halo = jnp.where(i == 0, 0.0, p_ref[...])
        y = _conv_rows(x_ref[...], halo, w_ref[...], b_ref[...])
        o_ref[...] = dy_ref[...] * _dsilu(y)

    return _rowwise(name, body, [("row", dy, cb), ("row", x, cb), ("prev", x, cb), ("vec", w, cb), ("vec", b, cb)],
                    [("row", None, c, cb, F32)], s, 512, ncol=c // cb)[0]


def _conv_bwd(dy, x, w, name):
    s, c = x.shape
    cb = _tile(c, 512)

    def body(i, n, dy_ref, nx_ref, x_ref, p_ref, w_ref, dx_ref, dw_ref, db_ref):
        d = dy_ref[...]
        xx = x_ref[...]
        wv = w_ref[...]
        nxt = jnp.where(i == n - 1, 0.0, nx_ref[...])
        prv = jnp.where(i == 0, 0.0, p_ref[...])
        dx = wv[3:4, :] * d
        parts = []
        for k in range(CONV_WIDTH - 1):
            sh = CONV_WIDTH - 1 - k
            dx = dx + wv[k:k + 1, :] * _shift_up(d, nxt, sh)
            parts.append(_colsum(d * _shift_down(xx, prv, sh)))
        parts.append(_colsum(d * xx))
        dx_ref[...] = dx.astype(BF16)
        _acc(i, dw_ref, jnp.concatenate(parts, axis=0))
        _acc(i, db_ref, _colsum(d))

    return _rowwise(name, body, [("row", dy, cb), ("next", dy, cb), ("row", x, cb), ("prev", x, cb), ("vec", w, cb)],
                    [("row", None, c, cb, BF16), ("acc", CONV_WIDTH, c, cb, F32), ("acc", 1, c, cb, F32)],
                    s, 512, ncol=c // cb)


def _lru_gate_math(xr, r_pre, i_pre, lam):
    r = _sigmoid(r_pre)
    ig = _sigmoid(i_pre)
    sp = _softplus(-lam)
    log_a = -LRU_C * r * sp
    a = jnp.exp(log_a)
    mult = jnp.sqrt(_neg_expm1(2.0 * log_a))
    return r, ig, sp, a, mult


def _lru_gates_fwd(xr, wr, wi, br, bi, lam):
    s, d = xr.shape
    nh, bw, _ = wr.shape
    ts = min(512, s)

    def body(x_ref, wr_ref, wi_ref, br_ref, bi_ref, lam_ref, a_ref, b_ref):
        x = x_ref[...]
        xb = x.astype(BF16)
        r_pre = jnp.dot(xb, wr_ref[0], preferred_element_type=F32) + br_ref[...]
        i_pre = jnp.dot(xb, wi_ref[0], preferred_element_type=F32) + bi_ref[...]
        _, ig, _, a, mult = _lru_gate_math(x, r_pre, i_pre, lam_ref[...])
        a_ref[...] = a
        b_ref[...] = mult * (ig * x)

    row = pl.BlockSpec((ts, bw), lambda h, i: (i, h))
    wsp = pl.BlockSpec((1, bw, bw), lambda h, i: (h, 0, 0))
    vec = pl.BlockSpec((1, bw), lambda h, i: (0, h))
    return pl.pallas_call(
        body, name="lru_gates_fwd", grid=(nh, s // ts), in_specs=[row, wsp, wsp, vec, vec, vec], out_specs=[row, row],
        out_shape=[jax.ShapeDtypeStruct((s, d), F32)] * 2, compiler_params=_params("parallel", "arbitrary"),
    )(xr, wr, wi, br, bi, lam)


def _lru_gates_bwd(xr, g, hs, wr, wi, br, bi, lam):
    s, d = xr.shape
    nh, bw, _ = wr.shape
    ts = min(512, s)
    hb = ts // SUBLANE
    tn_dims = (((0,), (0,)), ((), ()))
    nt_dims = (((1,), (1,)), ((), ()))

    def body(x_ref, g_ref, hs_ref, hp_ref, wr_ref, wi_ref, br_ref, bi_ref, lam_ref,
             dx_ref, dwr_ref, dwi_ref, dbr_ref, dbi_ref, dlam_ref):
        i = pl.program_id(1)
        x = x_ref[...]
        xb = x.astype(BF16)
        gg = g_ref[...]
        lam_v = lam_ref[...]
        r_pre = jnp.dot(xb, wr_ref[0], preferred_element_type=F32) + br_ref[...]
        i_pre = jnp.dot(xb, wi_ref[0], preferred_element_type=F32) + bi_ref[...]
        r, ig, sp, a, mult = _lru_gate_math(x, r_pre, i_pre, lam_v)
        h_prev = _shift_down(hs_ref[...], jnp.where(i == 0, 0.0, hp_ref[...]), 1)
        da = gg * h_prev
        dmult = gg * ig * x
        dlog_a = da * a - dmult * (a * a) / mult
        d_r = dlog_a * (-LRU_C * sp)
        dr_pre = d_r * r * (1.0 - r)
        di_pre = (gg * mult * x) * ig * (1.0 - ig)
        drb = dr_pre.astype(BF16)
        dib = di_pre.astype(BF16)
        dx_ref[...] = (gg * mult * ig
                       + lax.dot_general(drb, wr_ref[0], nt_dims, preferred_element_type=F32)
                       + lax.dot_general(dib, wi_ref[0], nt_dims, preferred_element_type=F32))
        dwr = lax.dot_general(xb, drb, tn_dims, preferred_element_type=F32)[None]
        dwi = lax.dot_general(xb, dib, tn_dims, preferred_element_type=F32)[None]
        dlam = _colsum(dlog_a * (-LRU_C * r)) * (-_sigmoid(-lam_v))
        _acc(i, dwr_ref, dwr)
        _acc(i, dwi_ref, dwi)
        _acc(i, dbr_ref, _colsum(dr_pre))
        _acc(i, dbi_ref, _colsum(di_pre))
        _acc(i, dlam_ref, dlam)

    row = pl.BlockSpec((ts, bw), lambda h, i: (i, h))
    prev = pl.BlockSpec((SUBLANE, bw), lambda h, i: (jnp.maximum(i * hb - 1, 0), h))
    wsp = pl.BlockSpec((1, bw, bw), lambda h, i: (h, 0, 0))
    vec = pl.BlockSpec((1, bw), lambda h, i: (0, h))
    return pl.pallas_call(
        body, name="lru_gates_bwd", grid=(nh, s // ts),
        in_specs=[row, row, row, prev, wsp, wsp, vec, vec, vec], out_specs=[row, wsp, wsp, vec, vec, vec],
        out_shape=[jax.ShapeDtypeStruct((s, d), F32), jax.ShapeDtypeStruct((nh, bw, bw), F32),
                   jax.ShapeDtypeStruct((nh, bw, bw), F32)] + [jax.ShapeDtypeStruct((1, d), F32)] * 3,
        compiler_params=_params("parallel", "arbitrary"),
    )(xr, g, hs, hs, wr, wi, br, bi, lam)


def _scan(a, b, reverse, name):
    s, c = a.shape
    cb = _tile(c, 512)
    nt = s // SUBLANE

    def body(a_ref, b_ref, o_ref):
        row = lax.broadcasted_iota(jnp.int32, (SUBLANE, cb), 0)

        def fwd_step(t, carry):
            r0 = pl.multiple_of(t * SUBLANE, SUBLANE)
            aa = a_ref[pl.ds(r0, SUBLANE), :]
            bb = b_ref[pl.ds(r0, SUBLANE), :]
            for sh in (1, 2, 4):
                a_s = jnp.where(row >= sh, pltpu.roll(aa, sh, axis=0), 1.0)
                b_s = jnp.where(row >= sh, pltpu.roll(bb, sh, axis=0), 0.0)
                bb = aa * b_s + bb
                aa = aa * a_s
            h = bb + aa * carry
            o_ref[pl.ds(r0, SUBLANE), :] = h
            return h[SUBLANE - 1:SUBLANE, :]

        def rev_step(k, carry):
            r0 = pl.multiple_of((nt - 1 - k) * SUBLANE, SUBLANE)
            aa = a_ref[pl.ds(r0, SUBLANE), :]
            dd = b_ref[pl.ds(r0, SUBLANE), :]
            cc = aa * dd
            for sh in (1, 2, 4):
                a_s = jnp.where(row < SUBLANE - sh, pltpu.roll(aa, SUBLANE - sh, axis=0), 1.0)
                c_s = jnp.where(row < SUBLANE - sh, pltpu.roll(cc, SUBLANE - sh, axis=0), 0.0)
                cc = cc + aa * c_s
                aa = aa * a_s
            big = cc + aa * carry
            nxt = jnp.where(row < SUBLANE - 1, pltpu.roll(big, SUBLANE - 1, axis=0), carry)
            o_ref[pl.ds(r0, SUBLANE), :] = dd + nxt
            return big[0:1, :]

        lax.fori_loop(0, nt, rev_step if reverse else fwd_step, jnp.zeros((1, cb), F32))

    spec = pl.BlockSpec((s, cb), lambda j: (0, j))
    return pl.pallas_call(body, name=name, grid=(c // cb,), in_specs=[spec, spec], out_specs=spec,
                          out_shape=jax.ShapeDtypeStruct((s, c), F32), compiler_params=_params("parallel"))(a, b)


def _lru_out_fwd(hs, yg):
    s, d = hs.shape
    cb = _tile(d, 1024)

    def body(i, n, h_ref, y_ref, o_ref):
        o_ref[...] = (h_ref[...] * _gelu(y_ref[...])).astype(BF16)

    return _rowwise("lru_out_fwd", body, [("row", hs, cb), ("row", yg, cb)], [("row", None, d, cb, BF16)],
                    s, 512, ncol=d // cb)[0]


def _lru_out_bwd(dy, hs, yg):
    s, d = hs.shape
    cb = _tile(d, 1024)

    def body(i, n, dy_ref, h_ref, y_ref, dh_ref, dyg_ref):
        dyv = dy_ref[...]
        y = y_ref[...]
        dh_ref[...] = dyv * _gelu(y)
        dyg_ref[...] = (dyv * h_ref[...] * _dgelu(y)).astype(BF16)

    return _rowwise("lru_out_bwd", body, [("row", dy, cb), ("row", hs, cb), ("row", yg, cb)],
                    [("row", None, d, cb, F32), ("row", None, d, cb, BF16)], s, 512, ncol=d // cb)


def _swiglu_act(gt, up):
    s, f = gt.shape
    cb = _tile(f, 1024)

    def body(i, n, g_ref, u_ref, o_ref):
        o_ref[...] = (_silu(g_ref[...]) * u_ref[...]).astype(BF16)

    return _rowwise("swiglu_act", body, [("row", gt, cb), ("row", up, cb)], [("row", None, f, cb, BF16)],
                    s, 512, ncol=f // cb)[0]


def _swiglu_bwd(dact, gt, up):
    s, f = gt.shape
    cb = _tile(f, 1024)

    def body(i, n, d_ref, g_ref, u_ref, dg_ref, du_ref):
        d = d_ref[...]
        g = g_ref[...]
        dg_ref[...] = (d * u_ref[...] * _dsilu(g)).astype(BF16)
        du_ref[...] = (d * _silu(g)).astype(BF16)

    return _rowwise("swiglu_bwd", body, [("row", dact, cb), ("row", gt, cb), ("row", up, cb)],
                    [("row", None, f, cb, BF16), ("row", None, f, cb, BF16)], s, 512, ncol=f // cb)


def _ple_fwd(h, gp, pp):
    s, d = h.shape
    cb = _tile(d, 1024)

    def body(i, n, h_ref, g_ref, p_ref, o_ref):
        o_ref[...] = h_ref[...] + _sigmoid(g_ref[...]) * p_ref[...]

    return _rowwise("ple_fwd", body, [("row", h, cb), ("row", gp, cb), ("row", pp, cb)], [("row", None, d, cb, F32)],
                    s, 512, ncol=d // cb)[0]


def _ple_bwd(dh, gp, pp):
    s, d = dh.shape
    cb = _tile(d, 1024)

    def body(i, n, d_ref, g_ref, p_ref, dg_ref, dp_ref):
        dv = d_ref[...]
        sg = _sigmoid(g_ref[...])
        dg_ref[...] = (dv * p_ref[...] * sg * (1.0 - sg)).astype(BF16)
        dp_ref[...] = (dv * sg).astype(BF16)

    return _rowwise("ple_bwd", body, [("row", dh, cb), ("row", gp, cb), ("row", pp, cb)],
                    [("row", None, d, cb, BF16), ("row", None, d, cb, BF16)], s, 512, ncol=d // cb)


def _dt_fwd(dt_pre, bias, n_heads):
    s = dt_pre.shape[0]

    def body(i, n, d_ref, b_ref, o_ref):
        lane = lax.broadcasted_iota(jnp.int32, d_ref.shape, 1)
        o_ref[...] = jnp.where(lane < n_heads, _softplus(d_ref[...] + b_ref[...]), 0.0)

    return _rowwise("ssd_dt_fwd", body, [("row", dt_pre, LANE), ("vec", bias, LANE)], [("row", None, LANE, LANE, F32)],
                    s, 512)[0]


def _dt_bwd(ddt, dt_pre, bias, n_heads):
    s = dt_pre.shape[0]

    def body(i, n, g_ref, d_ref, b_ref, o_ref, db_ref):
        lane = lax.broadcasted_iota(jnp.int32, d_ref.shape, 1)
        v = jnp.where(lane < n_heads, g_ref[...] * _sigmoid(d_ref[...] + b_ref[...]), 0.0)
        o_ref[...] = v.astype(BF16)
        _acc(i, db_ref, _colsum(v))

    return _rowwise("ssd_dt_bwd", body, [("row", ddt, LANE), ("row", dt_pre, LANE), ("vec", bias, LANE)],
                    [("row", None, LANE, LANE, BF16), ("acc", 1, LANE, LANE, F32)], s, 512)


def _ssd_chunk_terms(dt, alog, g, gw):
    ln = dt.shape[0]
    a_neg = -jnp.exp(alog)
    adt = dt * a_neg
    row = lax.broadcasted_iota(jnp.int32, (ln, ln), 0)
    col = lax.broadcasted_iota(jnp.int32, (ln, ln), 1)
    tril = row >= col
    cs = jnp.dot(tril.astype(F32), adt, preferred_element_type=F32, precision=HIGHEST)
    er = lax.broadcasted_iota(jnp.int32, (LANE, gw), 0)
    el = lax.broadcasted_iota(jnp.int32, (LANE, gw), 1)
    eg = (er == g * SSD_HEADS_PER_GROUP + el // SSD_HEAD_DIM).astype(F32)
    cs_e = jnp.dot(cs, eg, preferred_element_type=F32, precision=HIGHEST)
    dt_e = jnp.dot(dt, eg, preferred_element_type=F32, precision=HIGHEST)
    return a_neg, adt, cs, tril, eg, cs_e, dt_e


def _ssd_head_scores(cs, cst_ref, cb_mat, tril, g, e):
    h = g * SSD_HEADS_PER_GROUP + e
    sel = (lax.broadcasted_iota(jnp.int32, (LANE, LANE), 0) == h).astype(F32)
    ccol = jnp.dot(cs, sel, preferred_element_type=F32, precision=HIGHEST)
    crow = cst_ref[pl.ds(h, 1), :]
    lm = jnp.where(tril, jnp.exp(jnp.minimum(ccol - crow, 0.0)), 0.0)
    return (cb_mat * lm).astype(BF16), lm


_NT = (((1,), (1,)), ((), ()))
_TN = (((0,), (0,)), ((), ()))


def _ssd_fwd(xbc, dt, alog, inner, n_groups):
    s = xbc.shape[0]
    ln = SSD_CHUNK
    nc = s // ln
    gw = SSD_HEADS_PER_GROUP * SSD_HEAD_DIM
    npair = gw // LANE
    boff = inner // LANE

    def body(xs_ref, b_ref, c_ref, dt_ref, alog_ref, y_ref, sin_ref, st_ref, cst_ref):
        c = pl.program_id(0)
        g = pl.program_id(1)

        @pl.when(c == 0)
        def _():
            st_ref[g] = jnp.zeros((SSD_STATE, gw), F32)

        dtv = dt_ref[...]
        _, _, cs, tril, _, cs_e, dt_e = _ssd_chunk_terms(dtv, alog_ref[...], g, gw)
        cst_ref[...] = cs.T
        x = xs_ref[...] * dt_e
        tot_e = cs_e[ln - 1:ln, :]
        dec = jnp.exp(tot_e - cs_e)
        ecs = jnp.exp(cs_e)
        xb = x.astype(BF16)
        xd = (x * dec).astype(BF16)
        bg = b_ref[...].astype(BF16)
        cg = c_ref[...].astype(BF16)
        cb_mat = lax.dot_general(cg, bg, _NT, preferred_element_type=F32)
        sg = st_ref[g]
        sin_ref[0] = sg
        lo = lax.broadcasted_iota(jnp.int32, (ln, LANE), 1) < SSD_HEAD_DIM
        scores = [_ssd_head_scores(cs, cst_ref, cb_mat, tril, g, e)[0] for e in range(SSD_HEADS_PER_GROUP)]
        ys, news = [], []
        for pr in range(npair):
            cols = slice(LANE * pr, LANE * (pr + 1))
            xp = xb[:, cols]
            zero = jnp.zeros_like(xp)
            acc = jnp.dot(scores[2 * pr], jnp.where(lo, xp, zero), preferred_element_type=F32)
            acc = acc + jnp.dot(scores[2 * pr + 1], jnp.where(lo, zero, xp), preferred_element_type=F32)
            sp = sg[:, cols]
            yoff = jnp.dot(cg, sp.astype(BF16), preferred_element_type=F32) * ecs[:, cols]
            ys.append(acc + yoff)
            news.append(jnp.exp(tot_e[:, cols]) * sp
                        + lax.dot_general(bg, xd[:, cols], _TN, preferred_element_type=F32))
        y_ref[...] = jnp.concatenate(ys, axis=1)
        st_ref[g] = jnp.concatenate(news, axis=1)

    in_specs = [pl.BlockSpec((ln, gw), lambda c, g: (c, g)),
                pl.BlockSpec((ln, SSD_STATE), lambda c, g: (c, boff + g)),
                pl.BlockSpec((ln, SSD_STATE), lambda c, g: (c, boff + n_groups + g)),
                pl.BlockSpec((ln, LANE), lambda c, g: (c, 0)),
                pl.BlockSpec((1, LANE), lambda c, g: (0, 0))]
    out_specs = [pl.BlockSpec((ln, gw), lambda c, g: (c, g)),
                 pl.BlockSpec((1, SSD_STATE, gw), lambda c, g: (c, 0, g))]
    return pl.pallas_call(
        body, name="ssd_fwd", grid=(nc, n_groups), in_specs=in_specs, out_specs=out_specs,
        out_shape=[jax.ShapeDtypeStruct((s, inner), F32), jax.ShapeDtypeStruct((nc, SSD_STATE, inner), F32)],
        scratch_shapes=[pltpu.VMEM((n_groups, SSD_STATE, gw), F32), pltpu.VMEM((LANE, ln), F32)],
        compiler_params=_params("arbitrary", "arbitrary"),
    )(xbc, xbc, xbc, dt, alog)


def _ssd_bwd(xbc, dt, alog, dy, y, sin, dskip_e, inner, n_groups):
    s = xbc.shape[0]
    ln = SSD_CHUNK
    nc = s // ln
    gw = SSD_HEADS_PER_GROUP * SSD_HEAD_DIM
    npair = gw // LANE
    boff = inner // LANE

    def body(xs_ref, b_ref, c_ref, dt_ref, alog_ref, dy_ref, y_ref, sin_ref, sout_ref, dsk_ref,
             dxs_ref, db_ref, dc_ref, ddt_ref, dalog_ref, ds_ref, cst_ref):
        step = pl.program_id(0)
        g = pl.program_id(1)

        @pl.when(step == 0)
        def _():
            ds_ref[g] = jnp.zeros((SSD_STATE, gw), F32)

        dtv = dt_ref[...]
        a_neg, _, cs, tril, eg, cs_e, dt_e = _ssd_chunk_terms(dtv, alog_ref[...], g, gw)
        cst_ref[...] = cs.T
        xs = xs_ref[...]
        x = xs * dt_e
        tot_e = cs_e[ln - 1:ln, :]
        dec = jnp.exp(tot_e - cs_e)
        ecs = jnp.exp(cs_e)
        xb = x.astype(BF16)
        xd = (x * dec).astype(BF16)
        bg = b_ref[...].astype(BF16)
        cg = c_ref[...].astype(BF16)
        cb_mat = lax.dot_general(cg, bg, _NT, preferred_element_type=F32)
        dyv = dy_ref[...]
        dyb = dyv.astype(BF16)
        dye = (ecs * dyv).astype(BF16)
        s_in = sin_ref[0]
        d_s = ds_ref[g]
        dtot_lane = _colsum(sout_ref[0] * d_s)
        lo = lax.broadcasted_iota(jnp.int32, (ln, LANE), 1) < SSD_HEAD_DIM
        heads = [_ssd_head_scores(cs, cst_ref, cb_mat, tril, g, e) for e in range(SSD_HEADS_PER_GROUP)]
        dcb = jnp.zeros((ln, ln), F32)
        dbg = jnp.zeros((ln, SSD_STATE), F32)
        dcg = jnp.zeros((ln, SSD_STATE), F32)
        dxs_parts, nds = [], []
        for pr in range(npair):
            cols = slice(LANE * pr, LANE * (pr + 1))
            xp = xb[:, cols]
            dyp = dyb[:, cols]
            zero = jnp.zeros_like(dyp)
            dxp = jnp.zeros((ln, LANE), F32)
            for e, dym in ((2 * pr, jnp.where(lo, dyp, zero)), (2 * pr + 1, jnp.where(lo, zero, dyp))):
                sc, lm = heads[e]
                dsc = lax.dot_general(dym, xp, _NT, preferred_element_type=F32)
                dcb = dcb + dsc * lm
                dxp = dxp + lax.dot_general(sc, dym, _TN, preferred_element_type=F32)
            dsp = d_s[:, cols]
            dspb = dsp.astype(BF16)
            dxp = dxp + dec[:, cols] * jnp.dot(bg, dspb, preferred_element_type=F32)
            dcg = dcg + lax.dot_general(dye[:, cols], s_in[:, cols].astype(BF16), _NT, preferred_element_type=F32)
            dbg = dbg + lax.dot_general(xd[:, cols], dspb, _NT, preferred_element_type=F32)
            nds.append(jnp.exp(tot_e[:, cols]) * dsp
                       + lax.dot_general(cg, dye[:, cols], _TN, preferred_element_type=F32))
            dxs_parts.append(dxp)
        ds_ref[g] = jnp.concatenate(nds, axis=1)
        dx = jnp.concatenate(dxs_parts, axis=1)
        dcbb = dcb.astype(BF16)
        dc_ref[...] = dcg + jnp.dot(dcbb, bg, preferred_element_type=F32)
        db_ref[...] = dbg + lax.dot_general(dcbb, cg, _TN, preferred_element_type=F32)
        dxs_ref[...] = dx * dt_e + dyv * dsk_ref[...]
        w1 = y_ref[...] * dyb.astype(F32) - xb.astype(F32) * dx
        dcs = lax.dot_general(w1, eg, _NT, preferred_element_type=F32, precision=HIGHEST)
        dtot = lax.dot_general(jnp.broadcast_to(dtot_lane, (SUBLANE, gw)), eg, _NT,
                               preferred_element_type=F32, precision=HIGHEST)[0:1, :]
        rowl = lax.broadcasted_iota(jnp.int32, (ln, LANE), 0)
        dcs = jnp.where(rowl == ln - 1, dcs + dtot, dcs)
        row = lax.broadcasted_iota(jnp.int32, (ln, ln), 0)
        col = lax.broadcasted_iota(jnp.int32, (ln, ln), 1)
        dadt = jnp.dot((row <= col).astype(F32), dcs, preferred_element_type=F32, precision=HIGHEST)
        ddt = a_neg * dadt + lax.dot_general(dx * xs, eg, _NT, preferred_element_type=F32, precision=HIGHEST)
        dal = _colsum(dadt * dtv) * a_neg

        @pl.when(g == 0)
        def _():
            ddt_ref[...] = ddt

        @pl.when(g > 0)
        def _():
            ddt_ref[...] += ddt

        @pl.when((step == 0) & (g == 0))
        def _():
            dalog_ref[...] = dal

        @pl.when((step > 0) | (g > 0))
        def _():
            dalog_ref[...] += dal

    def rc(step):
        return nc - 1 - step

    in_specs = [pl.BlockSpec((ln, gw), lambda t, g: (rc(t), g)),
                pl.BlockSpec((ln, SSD_STATE), lambda t, g: (rc(t), boff + g)),
                pl.BlockSpec((ln, SSD_STATE), lambda t, g: (rc(t), boff + n_groups + g)),
                pl.BlockSpec((ln, LANE), lambda t, g: (rc(t), 0)),
                pl.BlockSpec((1, LANE), lambda t, g: (0, 0)),
                pl.BlockSpec((ln, gw), lambda t, g: (rc(t), g)),
                pl.BlockSpec((ln, gw), lambda t, g: (rc(t), g)),
                pl.BlockSpec((1, SSD_STATE, gw), lambda t, g: (rc(t), 0, g)),
                pl.BlockSpec((1, SSD_STATE, gw), lambda t, g: (jnp.minimum(rc(t) + 1, nc - 1), 0, g)),
                pl.BlockSpec((1, gw), lambda t, g: (0, g))]
    out_specs = [pl.BlockSpec((ln, gw), lambda t, g: (rc(t), g)),
                 pl.BlockSpec((ln, SSD_STATE), lambda t, g: (rc(t), g)),
                 pl.BlockSpec((ln, SSD_STATE), lambda t, g: (rc(t), g)),
                 pl.BlockSpec((ln, LANE), lambda t, g: (rc(t), 0)),
                 pl.BlockSpec((1, LANE), lambda t, g: (0, 0))]
    gn = n_groups * SSD_STATE
    return pl.pallas_call(
        body, name="ssd_bwd", grid=(nc, n_groups), in_specs=in_specs, out_specs=out_specs,
        out_shape=[jax.ShapeDtypeStruct((s, inner), F32), jax.ShapeDtypeStruct((s, gn), F32),
                   jax.ShapeDtypeStruct((s, gn), F32), jax.ShapeDtypeStruct((s, LANE), F32),
                   jax.ShapeDtypeStruct((1, LANE), F32)],
        scratch_shapes=[pltpu.VMEM((n_groups, SSD_STATE, gw), F32), pltpu.VMEM((LANE, ln), F32)],
        compiler_params=_params("arbitrary", "arbitrary"),
    )(xbc, xbc, xbc, dt, alog, dy, y, sin, sin, dskip_e)


def _ssd_gate_norm_fwd(ysc, xbc, z, dskip_e, norm_g, n_groups):
    s, inner = ysc.shape
    gsz = inner // n_groups

    def body(i, n, y_ref, x_ref, z_ref, d_ref, g_ref, o_ref):
        y2 = (y_ref[...] + d_ref[...] * x_ref[...]) * _silu(z_ref[...])
        gg = g_ref[...]
        outs = []
        for k in range(n_groups):
            cols = slice(k * gsz, (k + 1) * gsz)
            v = y2[:, cols]
            r = lax.rsqrt(jnp.mean(v * v, axis=-1, keepdims=True) + NORM_EPS)
            outs.append(v * r * gg[:, cols])
        o_ref[...] = jnp.concatenate(outs, axis=1).astype(BF16)

    return _rowwise("ssd_gate_norm_fwd", body,
                    [("row", ysc, inner), ("row", xbc, inner), ("row", z, inner), ("vec", dskip_e, inner),
                     ("vec", norm_g, inner)], [("row", None, inner, inner, BF16)], s, 128)[0]


def _ssd_gate_norm_bwd(dyn, ysc, xbc, z, dskip_e, norm_g, n_groups):
    s, inner = ysc.shape
    gsz = inner // n_groups

    def body(i, n, dn_ref, y_ref, x_ref, z_ref, d_ref, g_ref, dy_ref, dz_ref, dg_ref, dd_ref):
        xs = x_ref[...]
        zz = z_ref[...]
        y = y_ref[...] + d_ref[...] * xs
        sz = _silu(zz)
        y2 = y * sz
        dn = dn_ref[...]
        gg = g_ref[...]
        dy2s, dgs = [], []
        for k in range(n_groups):
            cols = slice(k * gsz, (k + 1) * gsz)
            v = y2[:, cols]
            d = dn[:, cols]
            r = lax.rsqrt(jnp.mean(v * v, axis=-1, keepdims=True) + NORM_EPS)
            vh = v * r
            dgs.append(_colsum(d * vh))
            dvh = d * gg[:, cols]
            dy2s.append(r * (dvh - vh * jnp.mean(dvh * vh, axis=-1, keepdims=True)))
        dy2 = jnp.concatenate(dy2s, axis=1)
        dy = dy2 * sz
        dy_ref[...] = dy
        dz_ref[...] = (dy2 * y * _dsilu(zz)).astype(BF16)
        _acc(i, dg_ref, jnp.concatenate(dgs, axis=1))
        _acc(i, dd_ref, _colsum(dy * xs))

    return _rowwise("ssd_gate_norm_bwd", body,
                    [("row", dyn, inner), ("row", ysc, inner), ("row", xbc, inner), ("row", z, inner),
                     ("vec", dskip_e, inner), ("vec", norm_g, inner)],
                    [("row", None, inner, inner, F32), ("row", None, inner, inner, BF16),
                     ("acc", 1, inner, inner, F32), ("acc", 1, inner, inner, F32)], s, 128)


def _adamw(w, g, m, v, name):
    rows, c = w.shape
    bc1 = 1.0 - ADAM_B1 ** ADAM_STEP
    bc2 = 1.0 - ADAM_B2 ** ADAM_STEP

    def body(i, n, w_ref, g_ref, m_ref, v_ref, d_ref, mo_ref, vo_ref):
        gg = g_ref[...]
        mn = ADAM_B1 * m_ref[...] + (1.0 - ADAM_B1) * gg
        vn = ADAM_B2 * v_ref[...] + (1.0 - ADAM_B2) * (gg * gg)
        d_ref[...] = -ADAM_LR * ((mn / bc1) / (jnp.sqrt(vn / bc2) + ADAM_EPS) + ADAM_WD * w_ref[...])
        mo_ref[...] = mn
        vo_ref[...] = vn

    ts = 128 if rows % 128 == 0 else rows
    return _rowwise(name, body, [("row", w, c), ("row", g, c), ("row", m, c), ("row", v, c)],
                    [("row", None, c, c, F32)] * 3, rows, ts)


_ANY = pl.BlockSpec(memory_space=pl.ANY)


def _place():
    x, y, c = lax.axis_index("x"), lax.axis_index("y"), lax.axis_index("c")
    chips = [(1 - x, y), (x, 1 - y), (1 - x, 1 - y)]
    return x, y, c, chips


def _rcopy(src, dst, ssem, rsem, dev):
    return pltpu.make_async_remote_copy(src_ref=src, dst_ref=dst, send_sem=ssem, recv_sem=rsem, device_id=dev,
                                        device_id_type=MESH)


def _place_shard(shard, q_idx, dtype, name):
    r, cc = shard.shape
    tr = _tile(r, 256, 16)

    def body(q_ref, s_ref, o_ref):
        o_ref[0] = s_ref[...].astype(dtype)

    grid_spec = pltpu.PrefetchScalarGridSpec(
        num_scalar_prefetch=1, grid=(r // tr,),
        in_specs=[pl.BlockSpec((tr, cc), lambda i, q_ref: (i, 0))],
        out_specs=pl.BlockSpec((1, tr, cc), lambda i, q_ref: (q_ref[0], i, 0)))
    return pl.pallas_call(body, name=name, grid_spec=grid_spec, out_shape=jax.ShapeDtypeStruct((N_CHIPS, r, cc), dtype),
                          compiler_params=_params("parallel"))(q_idx, shard)


def _gather_chips(bufs, name):
    n = len(bufs)
    half = [e.shape[1] // 2 for e in bufs]

    def body(*refs):
        outs = refs[n:2 * n]
        send_a, recv_a, send_b, recv_b = refs[2 * n:]
        x, y, c, chips = _place()
        q = 2 * x + y
        sib = (x, y, 1 - c)
        first = []
        for e in range(n):
            blk = outs[e].at[q, pl.ds(c * half[e], half[e])]
            for j, (cx, cy) in enumerate(chips):
                first.append(_rcopy(blk, blk, send_a.at[3 * e + j], recv_a.at[3 * e + j], (cx, cy, c)))
        for cp in first:
            cp.start()
        passed = []
        for e in range(n):
            rows = pl.ds(c * half[e], half[e])
            for j, (cx, cy) in enumerate(chips):
                blk = outs[e].at[2 * cx + cy, rows]
                _rcopy(blk, blk, send_a.at[3 * e + j], recv_a.at[3 * e + j], (cx, cy, c)).wait_recv()
                fw = _rcopy(blk, blk, send_b.at[3 * e + j], recv_b.at[3 * e + j], sib)
                fw.start()
                passed.append(fw)
        for e in range(n):
            rows = pl.ds((1 - c) * half[e], half[e])
            for j, (cx, cy) in enumerate(chips):
                blk = outs[e].at[2 * cx + cy, rows]
                _rcopy(blk, blk, send_b.at[3 * e + j], recv_b.at[3 * e + j], sib).wait_recv()
        for cp in first + passed:
            cp.wait_send()

    return pl.pallas_call(
        body, name=name, in_specs=[_ANY] * n, out_specs=[_ANY] * n,
        out_shape=[jax.ShapeDtypeStruct(e.shape, e.dtype) for e in bufs],
        input_output_aliases={e: e for e in range(n)},
        scratch_shapes=[pltpu.SemaphoreType.DMA((3 * n,))] * 4,
    )(*bufs)


def _swap_halves(entries, name):
    n = len(entries)
    half = [e.shape[1] // 2 for e in entries]

    def body(*refs):
        ins, outs = refs[:n], refs[n:2 * n]
        send, recv = refs[2 * n:]
        x, y, c, _ = _place()
        cps = [_rcopy(ins[e].at[:, pl.ds((1 - c) * half[e], half[e]), :], outs[e], send.at[e], recv.at[e],
                      (x, y, 1 - c)) for e in range(n)]
        for cp in cps:
            cp.start()
        for cp in cps:
            cp.wait()

    return pl.pallas_call(
        body, name=name, in_specs=[_ANY] * n, out_specs=[_ANY] * n,
        out_shape=[jax.ShapeDtypeStruct((N_CHIPS, h, e.shape[2]), e.dtype) for e, h in zip(entries, half)],
        scratch_shapes=[pltpu.SemaphoreType.DMA((n,))] * 2,
    )(*entries)


def _scatter_chips(entries, name):
    n = len(entries)

    def body(*refs):
        ins, outs = refs[:n], refs[n:2 * n]
        send, recv = refs[2 * n:]
        x, y, c, chips = _place()
        q = 2 * x + y
        cps = []
        for e in range(n):
            for j, (cx, cy) in enumerate(chips):
                cps.append(_rcopy(ins[e].at[2 * cx + cy], outs[e].at[q], send.at[3 * e + j], recv.at[3 * e + j],
                                  (cx, cy, c)))
        for cp in cps:
            cp.start()
        for e in range(n):
            for j, (cx, cy) in enumerate(chips):
                blk = outs[e].at[2 * cx + cy]
                _rcopy(blk, blk, send.at[3 * e + j], recv.at[3 * e + j], (cx, cy, c)).wait_recv()
        for cp in cps:
            cp.wait_send()

    return pl.pallas_call(
        body, name=name, in_specs=[_ANY] * n, out_specs=[_ANY] * n,
        out_shape=[jax.ShapeDtypeStruct(e.shape, e.dtype) for e in entries],
        scratch_shapes=[pltpu.SemaphoreType.DMA((3 * n,))] * 2,
    )(*entries)


def _join_halves(bufs, name):
    n = len(bufs)
    pairs = [(o, layer) for o in range(n) for layer in range(bufs[o].shape[0])]
    npair = len(pairs)

    def body(*refs):
        outs = refs[n:2 * n]
        send, recv = refs[2 * n:]
        x, y, c, _ = _place()
        cps = []
        for k, (o, layer) in enumerate(pairs):
            r2 = bufs[o].shape[1] // 2
            blk = outs[o].at[layer, pl.ds(c * r2, r2)]
            cps.append(_rcopy(blk, blk, send.at[k], recv.at[k], (x, y, 1 - c)))
        for cp in cps:
            cp.start()
        for k, (o, layer) in enumerate(pairs):
            r2 = bufs[o].shape[1] // 2
            blk = outs[o].at[layer, pl.ds((1 - c) * r2, r2)]
            _rcopy(blk, blk, send.at[k], recv.at[k], (x, y, 1 - c)).wait_recv()
        for cp in cps:
            cp.wait_send()

    return pl.pallas_call(
        body, name=name, in_specs=[_ANY] * n, out_specs=[_ANY] * n,
        out_shape=[jax.ShapeDtypeStruct(b.shape, b.dtype) for b in bufs],
        input_output_aliases={e: e for e in range(n)},
        scratch_shapes=[pltpu.SemaphoreType.DMA((npair,))] * 2,
    )(*bufs)


def _gather_all(v, name):
    def body(v_ref, o_ref, send, recv, loc):
        x, y, c, _ = _place()
        me = 4 * x + 2 * y + c
        mine = pltpu.make_async_copy(v_ref, o_ref.at[me], loc)
        mine.start()
        peers = []
        for k in range(1, N_DEV):
            px = 1 - x if k & 4 else x
            py = 1 - y if k & 2 else y
            pc = 1 - c if k & 1 else c
            peers.append((px, py, pc))
        cps = [_rcopy(v_ref, o_ref.at[me], send.at[k], recv.at[k], peers[k]) for k in range(N_DEV - 1)]
        for cp in cps:
            cp.start()
        for k, (px, py, pc) in enumerate(peers):
            blk = o_ref.at[4 * px + 2 * py + pc]
            _rcopy(blk, blk, send.at[k], recv.at[k], (px, py, pc)).wait_recv()
        for cp in cps:
            cp.wait_send()
        mine.wait()

    return pl.pallas_call(
        body, name=name, in_specs=[_ANY], out_specs=_ANY, out_shape=jax.ShapeDtypeStruct((N_DEV,) + v.shape, v.dtype),
        scratch_shapes=[pltpu.SemaphoreType.DMA((N_DEV - 1,))] * 2 + [pltpu.SemaphoreType.DMA],
    )(v)


def _add_own_half(gst, rx, c_idx, name):
    _, r, cc = gst.shape
    r2 = r // 2
    tr = _tile(r2, 256, 16)
    g4 = gst.reshape(N_CHIPS, 2, r2, cc)

    def body(c_ref, g_ref, r_ref, o_ref):
        o_ref[...] = (g_ref[0].astype(F32) + r_ref[...].astype(F32)).astype(BF16)

    grid_spec = pltpu.PrefetchScalarGridSpec(
        num_scalar_prefetch=1, grid=(N_CHIPS, r2 // tr),
        in_specs=[pl.BlockSpec((1, 1, tr, cc), lambda k, i, c_ref: (k, c_ref[0], i, 0)),
                  pl.BlockSpec((1, tr, cc), lambda k, i, c_ref: (k, i, 0))],
        out_specs=pl.BlockSpec((1, tr, cc), lambda k, i, c_ref: (k, i, 0)))
    return pl.pallas_call(body, name=name, grid_spec=grid_spec, out_shape=jax.ShapeDtypeStruct((N_CHIPS, r2, cc), BF16),
                          compiler_params=_params("parallel", "parallel"))(c_idx, g4, rx)


def _sum_into(buf, rx, own, layer, qc, out_shape, name):
    _, r2, cc = rx.shape
    tr = _tile(r2, 256, 16)
    nb = r2 // tr

    def body(qc_ref, *refs):
        rx_ref, own_ref, o_ref = refs[-3:]
        q = qc_ref[0]
        acc = None
        for k in range(N_CHIPS):
            v = jnp.where(q == k, own_ref[0], rx_ref[k]).astype(F32)
            acc = v if acc is None else acc + v
        o_ref[0] = acc

    in_specs = [pl.BlockSpec((N_CHIPS, tr, cc), lambda i, qc_ref: (0, i, 0)),
                pl.BlockSpec((1, tr, cc), lambda i, qc_ref: (qc_ref[0], i, 0))]
    args = (rx, own)
    aliases = {}
    if buf is not None:
        in_specs = [_ANY] + in_specs
        args = (buf,) + args
        aliases = {1: 0}
    grid_spec = pltpu.PrefetchScalarGridSpec(
        num_scalar_prefetch=1, grid=(nb,), in_specs=in_specs,
        out_specs=pl.BlockSpec((1, tr, cc), lambda i, qc_ref: (layer, qc_ref[1] * nb + i, 0)))
    return pl.pallas_call(body, name=name, grid_spec=grid_spec, out_shape=jax.ShapeDtypeStruct(out_shape, F32),
                          input_output_aliases=aliases, compiler_params=_params("parallel"))(qc, *args)


def _sum_slots(st, name):
    k, r, cc = st.shape
    tr = _tile(r, 256, 8)

    def body(s_ref, o_ref):
        acc = s_ref[0].astype(F32)
        for j in range(1, k):
            acc = acc + s_ref[j].astype(F32)
        o_ref[...] = acc

    return pl.pallas_call(body, name=name, grid=(r // tr,), in_specs=[pl.BlockSpec((k, tr, cc), lambda i: (0, i, 0))],
                          out_specs=pl.BlockSpec((tr, cc), lambda i: (i, 0)),
                          out_shape=jax.ShapeDtypeStruct((r, cc), F32), compiler_params=_params("parallel"))(st)


def _pack(arrs, rows_mult=2 * SUBLANE):
    flat = jnp.concatenate([a.reshape(-1).astype(F32) for a in arrs])
    quantum = rows_mult * LANE
    padded = -(-flat.shape[0] // quantum) * quantum
    return jnp.pad(flat, (0, padded - flat.shape[0])).reshape(-1, LANE)


def _unpack(buf, shapes):
    flat = buf.reshape(-1)
    out, off = [], 0
    for sh in shapes:
        size = 1
        for d in sh:
            size *= d
        out.append(flat[off:off + size].reshape(sh))
        off += size
    return out


def kernel(x, p, norm_mix_g, norm_ffn_g, norm_ple_g, final_norm_g, a_w_in, a_conv_w, a_conv_b, a_w_gate_r, a_b_gate_r, a_w_gate_i, a_b_gate_i, a_lambda, a_w_out, b_w_in, b_conv_w, b_conv_b, b_dt_bias, b_a_log, b_d_skip, b_norm_g, b_w_out, ffn_w_gate, ffn_w_up, ffn_w_down, ple_w_proj, ple_w_gate, loss_target, m_norm_mix_g, m_norm_ffn_g, m_norm_ple_g, m_final_norm_g, m_a_w_in, m_a_conv_w, m_a_conv_b, m_a_w_gate_r, m_a_b_gate_r, m_a_w_gate_i, m_a_b_gate_i, m_a_lambda, m_a_w_out, m_b_w_in, m_b_conv_w, m_b_conv_b, m_b_dt_bias, m_b_a_log, m_b_d_skip, m_b_norm_g, m_b_w_out, m_ffn_w_gate, m_ffn_w_up, m_ffn_w_down, m_ple_w_proj, m_ple_w_gate, v_norm_mix_g, v_norm_ffn_g, v_norm_ple_g, v_final_norm_g, v_a_w_in, v_a_conv_w, v_a_conv_b, v_a_w_gate_r, v_a_b_gate_r, v_a_w_gate_i, v_a_b_gate_i, v_a_lambda, v_a_w_out, v_b_w_in, v_b_conv_w, v_b_conv_b, v_b_dt_bias, v_b_a_log, v_b_d_skip, v_b_norm_g, v_b_w_out, v_ffn_w_gate, v_ffn_w_up, v_ffn_w_down, v_ple_w_proj, v_ple_w_gate):
    names = ["norm_mix_g", "norm_ffn_g", "norm_ple_g", "final_norm_g", "a_w_in", "a_conv_w", "a_conv_b", "a_w_gate_r",
             "a_b_gate_r", "a_w_gate_i", "a_b_gate_i", "a_lambda", "a_w_out", "b_w_in", "b_conv_w", "b_conv_b",
             "b_dt_bias", "b_a_log", "b_d_skip", "b_norm_g", "b_w_out", "ffn_w_gate", "ffn_w_up", "ffn_w_down",
             "ple_w_proj", "ple_w_gate"]
    w_in = dict(zip(names, [norm_mix_g, norm_ffn_g, norm_ple_g, final_norm_g, a_w_in, a_conv_w, a_conv_b, a_w_gate_r,
                            a_b_gate_r, a_w_gate_i, a_b_gate_i, a_lambda, a_w_out, b_w_in, b_conv_w, b_conv_b,
                            b_dt_bias, b_a_log, b_d_skip, b_norm_g, b_w_out, ffn_w_gate, ffn_w_up, ffn_w_down,
                            ple_w_proj, ple_w_gate]))
    m_in = dict(zip(names, [m_norm_mix_g, m_norm_ffn_g, m_norm_ple_g, m_final_norm_g, m_a_w_in, m_a_conv_w,
                            m_a_conv_b, m_a_w_gate_r, m_a_b_gate_r, m_a_w_gate_i, m_a_b_gate_i, m_a_lambda,
                            m_a_w_out, m_b_w_in, m_b_conv_w, m_b_conv_b, m_b_dt_bias, m_b_a_log, m_b_d_skip,
                            m_b_norm_g, m_b_w_out, m_ffn_w_gate, m_ffn_w_up, m_ffn_w_down, m_ple_w_proj,
                            m_ple_w_gate]))
    v_in = dict(zip(names, [v_norm_mix_g, v_norm_ffn_g, v_norm_ple_g, v_final_norm_g, v_a_w_in, v_a_conv_w,
                            v_a_conv_b, v_a_w_gate_r, v_a_b_gate_r, v_a_w_gate_i, v_a_b_gate_i, v_a_lambda,
                            v_a_w_out, v_b_w_in, v_b_conv_w, v_b_conv_b, v_b_dt_bias, v_b_a_log, v_b_d_skip,
                            v_b_norm_g, v_b_w_out, v_ffn_w_gate, v_ffn_w_up, v_ffn_w_down, v_ple_w_proj,
                            v_ple_w_gate]))

    s, d = x.shape[1], x.shape[2]
    depth = norm_mix_g.shape[0]
    assert depth == 2
    q_idx = 2 * lax.axis_index("x") + lax.axis_index("y")
    c_idx = lax.axis_index("c").astype(jnp.int32).reshape(1)

    inner = b_w_out.shape[1] * N_CHIPS
    n_heads = inner // SSD_HEAD_DIM
    n_groups = n_heads // SSD_HEADS_PER_GROUP
    gn = n_groups * SSD_STATE
    xbcw = inner + 2 * gn
    assert b_conv_w.shape[2] * N_CHIPS == xbcw and n_heads <= LANE

    big = [("a_w_in", "col"), ("a_w_gate_r", "gate"), ("a_w_gate_i", "gate"), ("a_w_out", "row"),
           ("ffn_w_gate", "col"), ("ffn_w_up", "col"), ("ffn_w_down", "row"), ("ple_w_proj", "col"),
           ("ple_w_gate", "row"), ("b_w_in", "col"), ("b_w_out", "row")]
    kind_of = dict(big)

    def shard2d(name, arr):
        if kind_of[name] == "gate":
            return [arr[l].reshape(-1, arr.shape[-1]) for l in range(arr.shape[0])]
        return [arr[l] for l in range(arr.shape[0])]

    small_sharded = ["a_conv_w", "a_b_gate_r", "a_b_gate_i", "b_conv_w", "b_conv_b", "b_norm_g"]
    small_pack = _pack([w_in[nm] for nm in small_sharded], rows_mult=16)

    q_vec = q_idx.astype(jnp.int32).reshape(1)
    qc_vec = jnp.stack([q_idx, lax.axis_index("c")]).astype(jnp.int32)
    bufs, entry_keys = [], []
    for nm, _ in big:
        for l, sh in enumerate(shard2d(nm, w_in[nm])):
            bufs.append(_place_shard(sh, q_vec, BF16, f"place_{nm}{l}"))
            entry_keys.append((nm, l))
    bufs.append(_place_shard(small_pack, q_vec, F32, "place_small"))
    gathered = _gather_chips(bufs, "gather_weights")
    small_st = gathered[-1]
    wst = dict(zip(entry_keys, gathered[:-1]))
    kept_stacked = ("ffn_w_gate", "ffn_w_up", "ple_w_proj")

    def whole(nm, l):
        st = wst[(nm, l)]
        kind = kind_of[nm]
        if nm in kept_stacked:
            return st
        if kind == "row":
            return st.reshape(-1, st.shape[-1])
        if kind == "col":
            return jnp.concatenate([st[k] for k in range(N_CHIPS)], axis=1)
        heads = w_in[nm].shape[1]
        return st.reshape(N_CHIPS, heads, -1, st.shape[-1]).transpose(1, 0, 2, 3).reshape(heads, -1, st.shape[-1])

    small_parts = [_unpack(small_st[k], [w_in[nm].shape for nm in small_sharded]) for k in range(N_CHIPS)]
    small_full = {nm: jnp.concatenate([small_parts[k][i] for k in range(N_CHIPS)], axis=-1)
                  for i, nm in enumerate(small_sharded)}
    a_cw = small_full["a_conv_w"][0]
    a_br = small_full["a_b_gate_r"][0].reshape(1, -1)
    a_bi = small_full["a_b_gate_i"][0].reshape(1, -1)
    b_cw = small_full["b_conv_w"][0]
    b_cb = small_full["b_conv_b"]
    b_ng = small_full["b_norm_g"]

    def pad_lanes(v):
        return jnp.pad(v, ((0, 0), (0, LANE - v.shape[1])))

    dt_bias = pad_lanes(b_dt_bias)
    a_log = pad_lanes(b_a_log)
    dskip_e = jnp.repeat(b_d_skip, SSD_HEAD_DIM, axis=1)

    w_a_in = whole("a_w_in", 0)
    w_ax, w_ay = w_a_in[:, :d], w_a_in[:, d:]
    w_ar, w_ai, w_ao = whole("a_w_gate_r", 0), whole("a_w_gate_i", 0), whole("a_w_out", 0)
    w_b_in = whole("b_w_in", 0)
    w_bz, w_bx = w_b_in[:, :inner], w_b_in[:, inner:inner + xbcw]
    w_bd = pad_lanes(w_b_in[:, inner + xbcw:])
    w_bo = whole("b_w_out", 0)
    w_fg = [whole("ffn_w_gate", l) for l in range(depth)]
    w_fu = [whole("ffn_w_up", l) for l in range(depth)]
    w_fd = [whole("ffn_w_down", l) for l in range(depth)]
    w_pp = [whole("ple_w_proj", l) for l in range(depth)]
    w_pg = [whole("ple_w_gate", l) for l in range(depth)]

    grads = {}

    h0 = x[0]
    g_mix = [norm_mix_g[l:l + 1] for l in range(depth)]
    g_ffn = [norm_ffn_g[l:l + 1] for l in range(depth)]
    g_ple = [norm_ple_g[l:l + 1] for l in range(depth)]
    g_fin = final_norm_g.reshape(1, -1)

    u0 = _rmsnorm_fwd(h0, g_mix[0], "norm_mix0")
    xr_pre = _mm(u0, w_ax, name="lru_in_x")
    yg = _mm(u0, w_ay, name="lru_in_y")
    xr = _conv_fwd(xr_pre, a_cw, a_conv_b, False, "lru_conv")
    lru_a, lru_b = _lru_gates_fwd(xr, w_ar, w_ai, a_br, a_bi, a_lambda)
    hs = _scan(lru_a, lru_b, False, "lru_scan")
    y_lru = _lru_out_fwd(hs, yg)
    h_mix = [_mm(y_lru, w_ao, add=h0, name="lru_out"), None]

    def ffn_ple_fwd(h_in, l):
        n_f = _rmsnorm_fwd(h_in, g_ffn[l], f"norm_ffn{l}")
        gt = _mm(n_f, w_fg[l], stacked_b=True, name=f"ffn_gate{l}")
        up = _mm(n_f, w_fu[l], stacked_b=True, name=f"ffn_up{l}")
        act = _swiglu_act(gt, up)
        h_f = _mm(act, w_fd[l], add=h_in, name=f"ffn_down{l}")
        n_p = _rmsnorm_fwd(h_f, g_ple[l], f"norm_ple{l}")
        gp = _mm(n_p, w_pg[l], name=f"ple_gate{l}")
        pp = _mm(p[l, 0], w_pp[l], stacked_b=True, name=f"ple_proj{l}")
        h_out = _ple_fwd(h_f, gp, pp)
        return h_out, dict(h_in=h_in, n_f=n_f, gt=gt, up=up, act=act, h_f=h_f, n_p=n_p, gp=gp, pp=pp)

    h_l0, sv0 = ffn_ple_fwd(h_mix[0], 0)

    u1 = _rmsnorm_fwd(h_l0, g_mix[1], "norm_mix1")
    z = _mm(u1, w_bz, name="ssd_in_z")
    xbc_pre = _mm(u1, w_bx, name="ssd_in_xbc")
    dt_pre = _mm(u1, w_bd, name="ssd_in_dt")
    xbc = _conv_fwd(xbc_pre, b_cw, b_cb, True, "ssd_conv")
    dt = _dt_fwd(dt_pre, dt_bias, n_heads)
    ysc, s_in = _ssd_fwd(xbc, dt, a_log, inner, n_groups)
    yn = _ssd_gate_norm_fwd(ysc, xbc, z, dskip_e, b_ng, n_groups)
    h_mix[1] = _mm(yn, w_bo, add=h_l0, name="ssd_out")
    h_l1, sv1 = ffn_ple_fwd(h_mix[1], 1)

    dh, dg_fin, loss_row = _final_loss_bwd(h_l1, g_fin, loss_target[0])

    d_norm_ffn, d_norm_ple, d_norm_mix = [None] * depth, [None] * depth, [None] * depth
    for nm in ("ffn_w_gate", "ffn_w_up", "ffn_w_down", "ple_w_proj", "ple_w_gate"):
        grads[nm] = [None] * depth

    def ffn_ple_bwd(dh_out, sv, l):
        dgp, dpp = _ple_bwd(dh_out, sv["gp"], sv["pp"])
        grads["ple_w_gate"][l] = _mm(sv["n_p"], dgp, ta=True, out_dtype=BF16, name=f"ple_gate_dw{l}")
        grads["ple_w_proj"][l] = _mm(p[l, 0], dpp, ta=True, out_dtype=BF16, stacked_out=True,
                                     name=f"ple_proj_dw{l}")
        dn = _mm(dgp, w_pg[l], tb=True, name=f"ple_gate_dx{l}")
        dh_f, d_norm_ple[l] = _rmsnorm_bwd(dn, sv["h_f"], g_ple[l], dh_out, f"norm_ple_bwd{l}")
        grads["ffn_w_down"][l] = _mm(sv["act"], dh_f, ta=True, out_dtype=BF16, name=f"ffn_down_dw{l}")
        dact = _mm(dh_f, w_fd[l], tb=True, name=f"ffn_down_dx{l}")
        dgt, dup = _swiglu_bwd(dact, sv["gt"], sv["up"])
        grads["ffn_w_gate"][l] = _mm(sv["n_f"], dgt, ta=True, out_dtype=BF16, stacked_out=True,
                                     name=f"ffn_gate_dw{l}")
        grads["ffn_w_up"][l] = _mm(sv["n_f"], dup, ta=True, out_dtype=BF16, stacked_out=True, name=f"ffn_up_dw{l}")
        dn = _mm(dgt, w_fg[l], tb=True, stacked_b=True, name=f"ffn_gate_dx{l}")
        dn = _mm(dup, w_fu[l], tb=True, stacked_b=True, add=dn, name=f"ffn_up_dx{l}")
        dh_in, d_norm_ffn[l] = _rmsnorm_bwd(dn, sv["h_in"], g_ffn[l], dh_f, f"norm_ffn_bwd{l}")
        return dh_in

    dh = ffn_ple_bwd(dh, sv1, 1)

    grads["b_w_out"] = [_mm(yn, dh, ta=True, out_dtype=BF16, name="ssd_out_dw")]
    dyn = _mm(dh, w_bo, tb=True, name="ssd_out_dx")
    dy_ssd, dz, d_b_norm_g, dd_lane = _ssd_gate_norm_bwd(dyn, ysc, xbc, z, dskip_e, b_ng, n_groups)
    dxs, d_bm, d_cm, ddt, d_a_log = _ssd_bwd(xbc, dt, a_log, dy_ssd, ysc, s_in, dskip_e, inner, n_groups)
    dxbc = jnp.concatenate([dxs, d_bm, d_cm], axis=1)
    dconv = _silu_conv_bwd_pre(dxbc, xbc_pre, b_cw, b_cb, "ssd_conv_bwd_pre")
    dxbc_pre, d_b_conv_w, d_b_conv_b = _conv_bwd(dconv, xbc_pre, b_cw, "ssd_conv_bwd")
    ddt_pre, d_dt_bias = _dt_bwd(ddt, dt_pre, dt_bias, n_heads)
    gw_bz = _mm(u1, dz, ta=True, out_dtype=BF16, name="ssd_in_z_dw")
    gw_bx = _mm(u1, dxbc_pre, ta=True, out_dtype=BF16, name="ssd_in_xbc_dw")
    gw_bd = _mm(u1, ddt_pre, ta=True, out_dtype=BF16, name="ssd_in_dt_dw")
    grads["b_w_in"] = [jnp.concatenate([gw_bz, gw_bx, gw_bd[:, :n_heads]], axis=1)]
    du = _mm(dz, w_bz, tb=True, name="ssd_in_z_dx")
    du = _mm(dxbc_pre, w_bx, tb=True, add=du, name="ssd_in_xbc_dx")
    du = _mm(ddt_pre, w_bd, tb=True, add=du, name="ssd_in_dt_dx")
    dh, d_norm_mix[1] = _rmsnorm_bwd(du, h_l0, g_mix[1], dh, "norm_mix_bwd1")

    dh = ffn_ple_bwd(dh, sv0, 0)

    grads["a_w_out"] = [_mm(y_lru, dh, ta=True, out_dtype=BF16, name="lru_out_dw")]
    dy_lru = _mm(dh, w_ao, tb=True, name="lru_out_dx")
    dhs, dyg = _lru_out_bwd(dy_lru, hs, yg)
    g_scan = _scan(lru_a, dhs, True, "lru_scan_bwd")
    dxr, d_wr, d_wi, d_br, d_bi, d_lam = _lru_gates_bwd(xr, g_scan, hs, w_ar, w_ai, a_br, a_bi, a_lambda)
    dxr_pre, d_a_conv_w, d_a_conv_b = _conv_bwd(dxr, xr_pre, a_cw, "lru_conv_bwd")
    gw_ax = _mm(u0, dxr_pre, ta=True, out_dtype=BF16, name="lru_in_x_dw")
    gw_ay = _mm(u0, dyg, ta=True, out_dtype=BF16, name="lru_in_y_dw")
    grads["a_w_in"] = [jnp.concatenate([gw_ax, gw_ay], axis=1)]
    grads["a_w_gate_r"] = [d_wr.astype(BF16)]
    grads["a_w_gate_i"] = [d_wi.astype(BF16)]
    du = _mm(dxr_pre, w_ax, tb=True, name="lru_in_x_dx")
    du = _mm(dyg, w_ay, tb=True, add=du, name="lru_in_y_dx")
    grad_x, d_norm_mix[0] = _rmsnorm_bwd(du, h0, g_mix[0], dh, "norm_mix_bwd0")

    def stacked(nm, gfull):
        kind = kind_of[nm]
        if nm in kept_stacked:
            return gfull
        if kind == "row":
            return gfull.reshape(N_CHIPS, -1, gfull.shape[-1])
        if kind == "col":
            n_loc = gfull.shape[1] // N_CHIPS
            return jnp.stack([gfull[:, k * n_loc:(k + 1) * n_loc] for k in range(N_CHIPS)])
        heads, bw, _ = gfull.shape
        return gfull.reshape(heads, N_CHIPS, bw // N_CHIPS, bw).transpose(1, 0, 2, 3).reshape(N_CHIPS, -1, bw)

    gst = [stacked(nm, grads[nm][l]) for nm, l in entry_keys]
    from_sib = _swap_halves(gst, "reduce_swap_halves")
    chip_sum = [_add_own_half(g, r, c_idx, f"reduce_add_{nm}{l}") for g, r, (nm, l) in zip(gst, from_sib, entry_keys)]
    from_chips = _scatter_chips(chip_sum, "reduce_scatter_chips")
    g_half = {}
    for rx, own, (nm, l) in zip(from_chips, chip_sum, entry_keys):
        sh = shard2d(nm, w_in[nm])
        g_half[nm] = _sum_into(g_half.get(nm), rx, own, l, qc_vec, (len(sh),) + sh[0].shape, f"reduce_sum_{nm}{l}")
    g_big = dict(zip([nm for nm, _ in big], _join_halves([g_half[nm] for nm, _ in big], "reduce_join_halves")))

    small_full_grads = {
        "norm_mix_g": jnp.concatenate(d_norm_mix, axis=0), "norm_ffn_g": jnp.concatenate(d_norm_ffn, axis=0),
        "norm_ple_g": jnp.concatenate(d_norm_ple, axis=0), "final_norm_g": dg_fin[0],
        "a_conv_w": d_a_conv_w[None], "a_conv_b": d_a_conv_b,
        "a_b_gate_r": d_br.reshape(a_b_gate_r.shape[0], a_b_gate_r.shape[1], -1),
        "a_b_gate_i": d_bi.reshape(a_b_gate_i.shape[0], a_b_gate_i.shape[1], -1),
        "a_lambda": d_lam, "b_conv_w": d_b_conv_w[None], "b_conv_b": d_b_conv_b,
        "b_dt_bias": d_dt_bias[:, :n_heads], "b_a_log": d_a_log[:, :n_heads],
        "b_d_skip": dd_lane.reshape(1, n_heads, SSD_HEAD_DIM).sum(axis=-1), "b_norm_g": d_b_norm_g,
    }
    small_names = list(small_full_grads)
    small_shapes = [small_full_grads[nm].shape for nm in small_names]
    packed = _pack([loss_row] + [small_full_grads[nm] for nm in small_names])
    total = _sum_slots(_gather_all(packed, "gather_small_grads"), "sum_small_grads")
    parts = _unpack(total, [(1, LANE)] + small_shapes)
    loss = parts[0][0, 0]
    g_small = {}
    for nm, gfull in zip(small_names, parts[1:]):
        if nm in small_sharded:
            n_loc = w_in[nm].shape[-1]
            gfull = lax.dynamic_slice_in_dim(gfull, q_idx * n_loc, n_loc, axis=gfull.ndim - 1)
        g_small[nm] = gfull

    grad_out, delta_out, m_out, v_out = {}, {}, {}, {}
    for nm, _ in big:
        shape = w_in[nm].shape
        gfull = g_big[nm]
        cols = gfull.shape[-1]
        dl, mn, vn = _adamw(w_in[nm].reshape(-1, cols), gfull.reshape(-1, cols), m_in[nm].reshape(-1, cols),
                            v_in[nm].reshape(-1, cols), f"adamw_{nm}")
        grad_out[nm], delta_out[nm] = gfull.reshape(shape), dl.reshape(shape)
        m_out[nm], v_out[nm] = mn.reshape(shape), vn.reshape(shape)
    sm_shapes = [w_in[nm].shape for nm in small_names]
    dl, mn, vn = _adamw(_pack([w_in[nm] for nm in small_names]), _pack([g_small[nm] for nm in small_names]),
                        _pack([m_in[nm] for nm in small_names]), _pack([v_in[nm] for nm in small_names]),
                        "adamw_small")
    for nm, a, b_, c_ in zip(small_names, _unpack(dl, sm_shapes), _unpack(mn, sm_shapes), _unpack(vn, sm_shapes)):
        grad_out[nm] = g_small[nm].reshape(w_in[nm].shape)
        delta_out[nm], m_out[nm], v_out[nm] = a, b_, c_

    return (loss, grad_x[None], *[grad_out[nm] for nm in names], *[delta_out[nm] for nm in names],
            *[m_out[nm] for nm in names], *[v_out[nm] for nm in names])
```

```python
import functools

import jax
import jax.numpy as jnp
from jax import lax
from jax.experimental import pallas as pl
from jax.experimental.pallas import tpu as pltpu

F32 = jnp.float32
BF16 = jnp.bfloat16
MESH = pl.DeviceIdType.MESH
HIGHEST = lax.Precision.HIGHEST

NORM_EPS = 1e-6
LRU_C = 8.0
CONV_WIDTH = 4
SSD_HEAD_DIM = 64
SSD_STATE = 128
SSD_CHUNK = 128
SSD_HEADS_PER_GROUP = 8
LANE = 128
SUBLANE = 8
N_CHIPS = 4
N_DEV = 8
VMEM_LIMIT = 48 * 1024 * 1024

ADAM_LR = 0.001
ADAM_B1 = 0.9
ADAM_B2 = 0.999
ADAM_EPS = 1e-08
ADAM_WD = 0.01
ADAM_STEP = 10


def _tile(n, cap, mult=LANE):
    best = None
    for t in range(mult, min(n, cap) + 1, mult):
        if n % t == 0:
            best = t
    return best if best is not None else n


def _params(*sem):
    return pltpu.CompilerParams(dimension_semantics=sem, vmem_limit_bytes=VMEM_LIMIT)


def _mm(a, b, *, ta=False, tb=False, add=None, out_dtype=F32, stacked_b=False, stacked_out=False, name):
    if ta:
        kd, m = a.shape
    else:
        m, kd = a.shape
    n_loc = None
    if stacked_b:
        _, kb, n_loc = b.shape
        if tb:
            n, kb = kb, N_CHIPS * n_loc
        else:
            n = N_CHIPS * n_loc
    elif tb:
        n, kb = b.shape
    else:
        kb, n = b.shape
    assert kd == kb, (a.shape, b.shape, ta, tb)
    tm = _tile(m, 1408)
    tn = _tile(n, 1408)
    tk = _tile(kd, 2048)
    if stacked_b and tb:
        tk = _tile(n_loc, 1408)
    elif stacked_b:
        tn = _tile(n_loc, 1408)
    if stacked_out:
        n_loc = n // N_CHIPS
        tn = _tile(n_loc, 1408)
    nk = kd // tk
    dims = (((0 if ta else 1,), (1 if tb else 0,)), ((), ()))

    def body(*refs):
        a_ref, b_ref = refs[:2]
        add_ref = refs[2] if add is not None else None
        o_ref = refs[3] if add is not None else refs[2]
        bv = b_ref[0] if stacked_b else b_ref[...]
        part = lax.dot_general(a_ref[...].astype(BF16), bv.astype(BF16), dims, preferred_element_type=F32)

        def finish(r):
            if add is not None:
                r = r + add_ref[...]
            if stacked_out:
                o_ref[0] = r.astype(out_dtype)
            else:
                o_ref[...] = r.astype(out_dtype)

        if nk == 1:
            finish(part)
        else:
            acc_ref = refs[-1]
            k = pl.program_id(2)

            @pl.when(k == 0)
            def _():
                acc_ref[...] = part

            @pl.when((k > 0) & (k < nk - 1))
            def _():
                acc_ref[...] += part

            @pl.when(k == nk - 1)
            def _():
                finish(acc_ref[...] + part)

    a_spec = pl.BlockSpec((tk, tm), lambda i, j, k: (k, i)) if ta else pl.BlockSpec((tm, tk), lambda i, j, k: (i, k))
    if stacked_b and tb:
        per = n_loc // tk
        b_spec = pl.BlockSpec((1, tn, tk), lambda i, j, k: (k // per, j, k % per))
    elif stacked_b:
        per = n_loc // tn
        b_spec = pl.BlockSpec((1, tk, tn), lambda i, j, k: (j // per, k, j % per))
    elif tb:
        b_spec = pl.BlockSpec((tn, tk), lambda i, j, k: (j, k))
    else:
        b_spec = pl.BlockSpec((tk, tn), lambda i, j, k: (k, j))
    o_spec = pl.BlockSpec((tm, tn), lambda i, j, k: (i, j))
    in_specs = [a_spec, b_spec] + ([o_spec] if add is not None else [])
    args = (a, b) + ((add,) if add is not None else ())
    if stacked_out:
        per_o = n_loc // tn
        out_spec = pl.BlockSpec((1, tm, tn), lambda i, j, k: (j // per_o, i, j % per_o))
        out_shape = jax.ShapeDtypeStruct((N_CHIPS, m, n_loc), out_dtype)
    else:
        out_spec, out_shape = o_spec, jax.ShapeDtypeStruct((m, n), out_dtype)
    return pl.pallas_call(
        body, name=name, grid=(m // tm, n // tn, nk), in_specs=in_specs, out_specs=out_spec, out_shape=out_shape,
        scratch_shapes=[pltpu.VMEM((tm, tn), F32)] if nk > 1 else [],
        compiler_params=_params("parallel", "parallel", "arbitrary"))(*args)


def _rowwise(name, body, ins, outs, nrows, ts, ncol=1):
    ts = min(ts, nrows)
    nrow = nrows // ts
    hb = ts // SUBLANE
    nb8 = nrows // SUBLANE
    in_specs, args = [], []
    for kind, arr, cb in ins:
        if kind == "row":
            spec = pl.BlockSpec((ts, cb), lambda j, i: (i, j))
        elif kind == "prev":
            spec = pl.BlockSpec((SUBLANE, cb), lambda j, i: (jnp.maximum(i * hb - 1, 0), j))
        elif kind == "next":
            spec = pl.BlockSpec((SUBLANE, cb), lambda j, i: (jnp.minimum((i + 1) * hb, nb8 - 1), j))
        else:
            spec = pl.BlockSpec((arr.shape[0], cb), lambda j, i: (0, j))
        in_specs.append(spec)
        args.append(arr)
    out_specs, out_shape = [], []
    for kind, rows, ctot, cb, dt in outs:
        if kind == "row":
            out_shape.append(jax.ShapeDtypeStruct((nrows, ctot), dt))
            out_specs.append(pl.BlockSpec((ts, cb), lambda j, i: (i, j)))
        else:
            out_shape.append(jax.ShapeDtypeStruct((rows, ctot), dt))
            out_specs.append(pl.BlockSpec((rows, cb), lambda j, i: (0, j)))

    def kern(*refs):
        body(pl.program_id(1), nrow, *refs)

    return pl.pallas_call(kern, name=name, grid=(ncol, nrow), in_specs=in_specs, out_specs=out_specs,
                          out_shape=out_shape, compiler_params=_params("parallel", "arbitrary"))(*args)


def _colsum(x):
    return jnp.sum(x, axis=0, keepdims=True)


def _acc(i, ref, val):
    @pl.when(i == 0)
    def _():
        ref[...] = val

    @pl.when(i > 0)
    def _():
        ref[...] += val


def _shift_down(x, halo, k):
    xx = jnp.concatenate([halo, x], axis=0)
    return pltpu.roll(xx, k, axis=0)[SUBLANE:, :]


def _shift_up(x, halo, k):
    xx = jnp.concatenate([x, halo], axis=0)
    n = xx.shape[0]
    return pltpu.roll(xx, n - k, axis=0)[: x.shape[0], :]


def _sigmoid(x):
    return 1.0 / (1.0 + jnp.exp(-x))


def _silu(x):
    return x * _sigmoid(x)


def _dsilu(x):
    s = _sigmoid(x)
    return s * (1.0 + x * (1.0 - s))


_GELU_K = 0.7978845608028654
_GELU_C = 0.044715


def _gelu(x):
    return 0.5 * x * (1.0 + jnp.tanh(_GELU_K * (x + _GELU_C * x * x * x)))


def _dgelu(x):
    t = jnp.tanh(_GELU_K * (x + _GELU_C * x * x * x))
    return 0.5 * (1.0 + t) + 0.5 * x * (1.0 - t * t) * _GELU_K * (1.0 + 3.0 * _GELU_C * x * x)


def _softplus(x):
    return jnp.maximum(x, 0.0) + jnp.log1p(jnp.exp(-jnp.abs(x)))


def _neg_expm1(x):
    poly = -x * (1.0 + x * (0.5 + x * (1.0 / 6.0 + x * (1.0 / 24.0 + x * (1.0 / 120.0)))))
    return jnp.where(x > -0.05, poly, 1.0 - jnp.exp(x))


def _rmsnorm_fwd(h, g, name):
    s, d = h.shape

    def body(i, n, h_ref, g_ref, o_ref):
        x = h_ref[...]
        r = lax.rsqrt(jnp.mean(x * x, axis=-1, keepdims=True) + NORM_EPS)
        o_ref[...] = (x * r * g_ref[...]).astype(BF16)

    return _rowwise(name, body, [("row", h, d), ("vec", g, d)], [("row", None, d, d, BF16)], s, 256)[0]


def _rmsnorm_bwd(dn, h, g, dres, name):
    s, d = h.shape

    def body(i, n, dn_ref, h_ref, g_ref, dres_ref, dh_ref, dg_ref):
        x = h_ref[...]
        dy = dn_ref[...].astype(F32)
        r = lax.rsqrt(jnp.mean(x * x, axis=-1, keepdims=True) + NORM_EPS)
        xh = x * r
        _acc(i, dg_ref, _colsum(dy * xh))
        dxh = dy * g_ref[...]
        dh_ref[...] = dres_ref[...] + r * (dxh - xh * jnp.mean(dxh * xh, axis=-1, keepdims=True))

    return _rowwise(name, body, [("row", dn, d), ("row", h, d), ("vec", g, d), ("row", dres, d)],
                    [("row", None, d, d, F32), ("acc", 1, d, d, F32)], s, 256)


def _final_loss_bwd(h, g, tgt):
    s, d = h.shape

    def body(i, n, h_ref, g_ref, t_ref, dh_ref, dg_ref, loss_ref):
        x = h_ref[...]
        gg = g_ref[...]
        r = lax.rsqrt(jnp.mean(x * x, axis=-1, keepdims=True) + NORM_EPS)
        xh = x * r
        err = xh * gg - t_ref[...]
        part = 0.5 * jnp.sum(jnp.mean(err * err, axis=-1, keepdims=True), axis=0, keepdims=True)
        _acc(i, loss_ref, jnp.broadcast_to(part, (1, LANE)))
        dy = err * (1.0 / d)
        _acc(i, dg_ref, _colsum(dy * xh))
        dxh = dy * gg
        dh_ref[...] = r * (dxh - xh * jnp.mean(dxh * xh, axis=-1, keepdims=True))

    return _rowwise("final_loss_bwd", body, [("row", h, d), ("vec", g, d), ("row", tgt, d)],
                    [("row", None, d, d, F32), ("acc", 1, d, d, F32), ("acc", 1, LANE, LANE, F32)], s, 256)


def _conv_rows(x, halo, w, b):
    y = b + w[3:4, :] * x
    for k in range(CONV_WIDTH - 1):
        y = y + w[k:k + 1, :] * _shift_down(x, halo, CONV_WIDTH - 1 - k)
    return y


def _conv_fwd(x, w, b, silu, name):
    s, c = x.shape
    cb = _tile(c, 512)

    def body(i, n, x_ref, p_ref, w_ref, b_ref, o_ref):
        halo = jnp.where(i == 0, 0.0, p_ref[...])
        y = _conv_rows(x_ref[...], halo, w_ref[...], b_ref[...])
        o_ref[...] = _silu(y) if silu else y

    return _rowwise(name, body, [("row", x, cb), ("prev", x, cb), ("vec", w, cb), ("vec", b, cb)],
                    [("row", None, c, cb, F32)], s, 512, ncol=c // cb)[0]


def _silu_conv_bwd_pre(dy, x, w, b, name):
    s, c = x.shape
    cb = _tile(c, 512)

    def body(i, n, dy_ref, x_ref, p_ref, w_ref, b_ref, o_ref):
        halo = jnp.where(i == 0, 0.0, p_ref[...])
        y = _conv_rows(x_ref[...], halo, w_ref[...], b_ref[...])
        o_ref[...] = dy_ref[...] * _dsilu(y)

    return _rowwise(name, body, [("row", dy, cb), ("row", x, cb), ("prev", x, cb), ("vec", w, cb), ("vec", b, cb)],
                    [("row", None, c, cb, F32)], s, 512, ncol=c // cb)[0]


def _conv_bwd(dy, x, w, name):
    s, c = x.shape
    cb = _tile(c, 512)

    def body(i, n, dy_ref, nx_ref, x_ref, p_ref, w_ref, dx_ref, dw_ref, db_ref):
        d = dy_ref[...]
        xx = x_ref[...]
        wv = w_ref[...]
        nxt = jnp.where(i == n - 1, 0.0, nx_ref[...])
        prv = jnp.where(i == 0, 0.0, p_ref[...])
        dx = wv[3:4, :] * d
        parts = []
        for k in range(CONV_WIDTH - 1):
            sh = CONV_WIDTH - 1 - k
            dx = dx + wv[k:k + 1, :] * _shift_up(d, nxt, sh)
            parts.append(_colsum(d * _shift_down(xx, prv, sh)))
        parts.append(_colsum(d * xx))
        dx_ref[...] = dx.astype(BF16)
        _acc(i, dw_ref, jnp.concatenate(parts, axis=0))
        _acc(i, db_ref, _colsum(d))

    return _rowwise(name, body, [("row", dy, cb), ("next", dy, cb), ("row", x, cb), ("prev", x, cb), ("vec", w, cb)],
                    [("row", None, c, cb, BF16), ("acc", CONV_WIDTH, c, cb, F32), ("acc", 1, c, cb, F32)],
                    s, 512, ncol=c // cb)


def _lru_gate_math(xr, r_pre, i_pre, lam):
    r = _sigmoid(r_pre)
    ig = _sigmoid(i_pre)
    sp = _softplus(-lam)
    log_a = -LRU_C * r * sp
    a = jnp.exp(log_a)
    mult = jnp.sqrt(_neg_expm1(2.0 * log_a))
    return r, ig, sp, a, mult


def _lru_gates_fwd(xr, wr, wi, br, bi, lam):
    s, d = xr.shape
    nh, bw, _ = wr.shape
    ts = min(512, s)

    def body(x_ref, wr_ref, wi_ref, br_ref, bi_ref, lam_ref, a_ref, b_ref):
        x = x_ref[...]
        xb = x.astype(BF16)
        r_pre = jnp.dot(xb, wr_ref[0], preferred_element_type=F32) + br_ref[...]
        i_pre = jnp.dot(xb, wi_ref[0], preferred_element_type=F32) + bi_ref[...]
        _, ig, _, a, mult = _lru_gate_math(x, r_pre, i_pre, lam_ref[...])
        a_ref[...] = a
        b_ref[...] = mult * (ig * x)

    row = pl.BlockSpec((ts, bw), lambda h, i: (i, h))
    wsp = pl.BlockSpec((1, bw, bw), lambda h, i: (h, 0, 0))
    vec = pl.BlockSpec((1, bw), lambda h, i: (0, h))
    return pl.pallas_call(
        body, name="lru_gates_fwd", grid=(nh, s // ts), in_specs=[row, wsp, wsp, vec, vec, vec], out_specs=[row, row],
        out_shape=[jax.ShapeDtypeStruct((s, d), F32)] * 2, compiler_params=_params("parallel", "arbitrary"),
    )(xr, wr, wi, br, bi, lam)


def _lru_gates_bwd(xr, g, hs, wr, wi, br, bi, lam):
    s, d = xr.shape
    nh, bw, _ = wr.shape
    ts = min(512, s)
    hb = ts // SUBLANE
    tn_dims = (((0,), (0,)), ((), ()))
    nt_dims = (((1,), (1,)), ((), ()))

    def body(x_ref, g_ref, hs_ref, hp_ref, wr_ref, wi_ref, br_ref, bi_ref, lam_ref,
             dx_ref, dwr_ref, dwi_ref, dbr_ref, dbi_ref, dlam_ref):
        i = pl.program_id(1)
        x = x_ref[...]
        xb = x.astype(BF16)
        gg = g_ref[...]
        lam_v = lam_ref[...]
        r_pre = jnp.dot(xb, wr_ref[0], preferred_element_type=F32) + br_ref[...]
        i_pre = jnp.dot(xb, wi_ref[0], preferred_element_type=F32) + bi_ref[...]
        r, ig, sp, a, mult = _lru_gate_math(x, r_pre, i_pre, lam_v)
        h_prev = _shift_down(hs_ref[...], jnp.where(i == 0, 0.0, hp_ref[...]), 1)
        da = gg * h_prev
        dmult = gg * ig * x
        dlog_a = da * a - dmult * (a * a) / mult
        d_r = dlog_a * (-LRU_C * sp)
        dr_pre = d_r * r * (1.0 - r)
        di_pre = (gg * mult * x) * ig * (1.0 - ig)
        drb = dr_pre.astype(BF16)
        dib = di_pre.astype(BF16)
        dx_ref[...] = (gg * mult * ig
                       + lax.dot_general(drb, wr_ref[0], nt_dims, preferred_element_type=F32)
                       + lax.dot_general(dib, wi_ref[0], nt_dims, preferred_element_type=F32))
        dwr = lax.dot_general(xb, drb, tn_dims, preferred_element_type=F32)[None]
        dwi = lax.dot_general(xb, dib, tn_dims, preferred_element_type=F32)[None]
        dlam = _colsum(dlog_a * (-LRU_C * r)) * (-_sigmoid(-lam_v))
        _acc(i, dwr_ref, dwr)
        _acc(i, dwi_ref, dwi)
        _acc(i, dbr_ref, _colsum(dr_pre))
        _acc(i, dbi_ref, _colsum(di_pre))
        _acc(i, dlam_ref, dlam)

    row = pl.BlockSpec((ts, bw), lambda h, i: (i, h))
    prev = pl.BlockSpec((SUBLANE, bw), lambda h, i: (jnp.maximum(i * hb - 1, 0), h))
    wsp = pl.BlockSpec((1, bw, bw), lambda h, i: (h, 0, 0))
    vec = pl.BlockSpec((1, bw), lambda h, i: (0, h))
    return pl.pallas_call(
        body, name="lru_gates_bwd", grid=(nh, s // ts),
        in_specs=[row, row, row, prev, wsp, wsp, vec, vec, vec], out_specs=[row, wsp, wsp, vec, vec, vec],
        out_shape=[jax.ShapeDtypeStruct((s, d), F32), jax.ShapeDtypeStruct((nh, bw, bw), F32),
                   jax.ShapeDtypeStruct((nh, bw, bw), F32)] + [jax.ShapeDtypeStruct((1, d), F32)] * 3,
        compiler_params=_params("parallel", "arbitrary"),
    )(xr, g, hs, hs, wr, wi, br, bi, lam)


def _scan(a, b, reverse, name):
    s, c = a.shape
    cb = _tile(c, 512)
    nt = s // SUBLANE

    def body(a_ref, b_ref, o_ref):
        row = lax.broadcasted_iota(jnp.int32, (SUBLANE, cb), 0)

        def fwd_step(t, carry):
            r0 = pl.multiple_of(t * SUBLANE, SUBLANE)
            aa = a_ref[pl.ds(r0, SUBLANE), :]
            bb = b_ref[pl.ds(r0, SUBLANE), :]
            for sh in (1, 2, 4):
                a_s = jnp.where(row >= sh, pltpu.roll(aa, sh, axis=0), 1.0)
                b_s = jnp.where(row >= sh, pltpu.roll(bb, sh, axis=0), 0.0)
                bb = aa * b_s + bb
                aa = aa * a_s
            h = bb + aa * carry
            o_ref[pl.ds(r0, SUBLANE), :] = h
            return h[SUBLANE - 1:SUBLANE, :]

        def rev_step(k, carry):
            r0 = pl.multiple_of((nt - 1 - k) * SUBLANE, SUBLANE)
            aa = a_ref[pl.ds(r0, SUBLANE), :]
            dd = b_ref[pl.ds(r0, SUBLANE), :]
            cc = aa * dd
            for sh in (1, 2, 4):
                a_s = jnp.where(row < SUBLANE - sh, pltpu.roll(aa, SUBLANE - sh, axis=0), 1.0)
                c_s = jnp.where(row < SUBLANE - sh, pltpu.roll(cc, SUBLANE - sh, axis=0), 0.0)
                cc = cc + aa * c_s
                aa = aa * a_s
            big = cc + aa * carry
            nxt = jnp.where(row < SUBLANE - 1, pltpu.roll(big, SUBLANE - 1, axis=0), carry)
            o_ref[pl.ds(r0, SUBLANE), :] = dd + nxt
            return big[0:1, :]

        lax.fori_loop(0, nt, rev_step if reverse else fwd_step, jnp.zeros((1, cb), F32))

    spec = pl.BlockSpec((s, cb), lambda j: (0, j))
    return pl.pallas_call(body, name=name, grid=(c // cb,), in_specs=[spec, spec], out_specs=spec,
                          out_shape=jax.ShapeDtypeStruct((s, c), F32), compiler_params=_params("parallel"))(a, b)


def _lru_out_fwd(hs, yg):
    s, d = hs.shape
    cb = _tile(d, 1024)

    def body(i, n, h_ref, y_ref, o_ref):
        o_ref[...] = (h_ref[...] * _gelu(y_ref[...])).astype(BF16)

    return _rowwise("lru_out_fwd", body, [("row", hs, cb), ("row", yg, cb)], [("row", None, d, cb, BF16)],
                    s, 512, ncol=d // cb)[0]


def _lru_out_bwd(dy, hs, yg):
    s, d = hs.shape
    cb = _tile(d, 1024)

    def body(i, n, dy_ref, h_ref, y_ref, dh_ref, dyg_ref):
        dyv = dy_ref[...]
        y = y_ref[...]
        dh_ref[...] = dyv * _gelu(y)
        dyg_ref[...] = (dyv * h_ref[...] * _dgelu(y)).astype(BF16)

    return _rowwise("lru_out_bwd", body, [("row", dy, cb), ("row", hs, cb), ("row", yg, cb)],
                    [("row", None, d, cb, F32), ("row", None, d, cb, BF16)], s, 512, ncol=d // cb)


def _swiglu_act(gt, up):
    s, f = gt.shape
    cb = _tile(f, 1024)

    def body(i, n, g_ref, u_ref, o_ref):
        o_ref[...] = (_silu(g_ref[...]) * u_ref[...]).astype(BF16)

    return _rowwise("swiglu_act", body, [("row", gt, cb), ("row", up, cb)], [("row", None, f, cb, BF16)],
                    s, 512, ncol=f // cb)[0]


def _swiglu_bwd(dact, gt, up):
    s, f = gt.shape
    cb = _tile(f, 1024)

    def body(i, n, d_ref, g_ref, u_ref, dg_ref, du_ref):
        d = d_ref[...]
        g = g_ref[...]
        dg_ref[...] = (d * u_ref[...] * _dsilu(g)).astype(BF16)
        du_ref[...] = (d * _silu(g)).astype(BF16)

    return _rowwise("swiglu_bwd", body, [("row", dact, cb), ("row", gt, cb), ("row", up, cb)],
                    [("row", None, f, cb, BF16), ("row", None, f, cb, BF16)], s, 512, ncol=f // cb)


def _ple_fwd(h, gp, pp):
    s, d = h.shape
    cb = _tile(d, 1024)

    def body(i, n, h_ref, g_ref, p_ref, o_ref):
        o_ref[...] = h_ref[...] + _sigmoid(g_ref[...]) * p_ref[...]

    return _rowwise("ple_fwd", body, [("row", h, cb), ("row", gp, cb), ("row", pp, cb)], [("row", None, d, cb, F32)],
                    s, 512, ncol=d // cb)[0]


def _ple_bwd(dh, gp, pp):
    s, d = dh.shape
    cb = _tile(d, 1024)

    def body(i, n, d_ref, g_ref, p_ref, dg_ref, dp_ref):
        dv = d_ref[...]
        sg = _sigmoid(g_ref[...])
        dg_ref[...] = (dv * p_ref[...] * sg * (1.0 - sg)).astype(BF16)
        dp_ref[...] = (dv * sg).astype(BF16)

    return _rowwise("ple_bwd", body, [("row", dh, cb), ("row", gp, cb), ("row", pp, cb)],
                    [("row", None, d, cb, BF16), ("row", None, d, cb, BF16)], s, 512, ncol=d // cb)


def _dt_fwd(dt_pre, bias, n_heads):
    s = dt_pre.shape[0]

    def body(i, n, d_ref, b_ref, o_ref):
        lane = lax.broadcasted_iota(jnp.int32, d_ref.shape, 1)
        o_ref[...] = jnp.where(lane < n_heads, _softplus(d_ref[...] + b_ref[...]), 0.0)

    return _rowwise("ssd_dt_fwd", body, [("row", dt_pre, LANE), ("vec", bias, LANE)], [("row", None, LANE, LANE, F32)],
                    s, 512)[0]


def _dt_bwd(ddt, dt_pre, bias, n_heads):
    s = dt_pre.shape[0]

    def body(i, n, g_ref, d_ref, b_ref, o_ref, db_ref):
        lane = lax.broadcasted_iota(jnp.int32, d_ref.shape, 1)
        v = jnp.where(lane < n_heads, g_ref[...] * _sigmoid(d_ref[...] + b_ref[...]), 0.0)
        o_ref[...] = v.astype(BF16)
        _acc(i, db_ref, _colsum(v))

    return _rowwise("ssd_dt_bwd", body, [("row", ddt, LANE), ("row", dt_pre, LANE), ("vec", bias, LANE)],
                    [("row", None, LANE, LANE, BF16), ("acc", 1, LANE, LANE, F32)], s, 512)


def _ssd_chunk_terms(dt, alog, g, gw):
    ln = dt.shape[0]
    a_neg = -jnp.exp(alog)
    adt = dt * a_neg
    row = lax.broadcasted_iota(jnp.int32, (ln, ln), 0)
    col = lax.broadcasted_iota(jnp.int32, (ln, ln), 1)
    tril = row >= col
    cs = jnp.dot(tril.astype(F32), adt, preferred_element_type=F32, precision=HIGHEST)
    er = lax.broadcasted_iota(jnp.int32, (LANE, gw), 0)
    el = lax.broadcasted_iota(jnp.int32, (LANE, gw), 1)
    eg = (er == g * SSD_HEADS_PER_GROUP + el // SSD_HEAD_DIM).astype(F32)
    cs_e = jnp.dot(cs, eg, preferred_element_type=F32, precision=HIGHEST)
    dt_e = jnp.dot(dt, eg, preferred_element_type=F32, precision=HIGHEST)
    return a_neg, adt, cs, tril, eg, cs_e, dt_e


def _ssd_head_scores(cs, cst_ref, cb_mat, tril, g, e):
    h = g * SSD_HEADS_PER_GROUP + e
    sel = (lax.broadcasted_iota(jnp.int32, (LANE, LANE), 0) == h).astype(F32)
    ccol = jnp.dot(cs, sel, preferred_element_type=F32, precision=HIGHEST)
    crow = cst_ref[pl.ds(h, 1), :]
    lm = jnp.where(tril, jnp.exp(jnp.minimum(ccol - crow, 0.0)), 0.0)
    return (cb_mat * lm).astype(BF16), lm


_NT = (((1,), (1,)), ((), ()))
_TN = (((0,), (0,)), ((), ()))


def _ssd_fwd(xbc, dt, alog, inner, n_groups):
    s = xbc.shape[0]
    ln = SSD_CHUNK
    nc = s // ln
    gw = SSD_HEADS_PER_GROUP * SSD_HEAD_DIM
    npair = gw // LANE
    boff = inner // LANE

    def body(xs_ref, b_ref, c_ref, dt_ref, alog_ref, y_ref, sin_ref, st_ref, cst_ref):
        c = pl.program_id(0)
        g = pl.program_id(1)

        @pl.when(c == 0)
        def _():
            st_ref[g] = jnp.zeros((SSD_STATE, gw), F32)

        dtv = dt_ref[...]
        _, _, cs, tril, _, cs_e, dt_e = _ssd_chunk_terms(dtv, alog_ref[...], g, gw)
        cst_ref[...] = cs.T
        x = xs_ref[...] * dt_e
        tot_e = cs_e[ln - 1:ln, :]
        dec = jnp.exp(tot_e - cs_e)
        ecs = jnp.exp(cs_e)
        xb = x.astype(BF16)
        xd = (x * dec).astype(BF16)
        bg = b_ref[...].astype(BF16)
        cg = c_ref[...].astype(BF16)
        cb_mat = lax.dot_general(cg, bg, _NT, preferred_element_type=F32)
        sg = st_ref[g]
        sin_ref[0] = sg
        lo = lax.broadcasted_iota(jnp.int32, (ln, LANE), 1) < SSD_HEAD_DIM
        scores = [_ssd_head_scores(cs, cst_ref, cb_mat, tril, g, e)[0] for e in range(SSD_HEADS_PER_GROUP)]
        ys, news = [], []
        for pr in range(npair):
            cols = slice(LANE * pr, LANE * (pr + 1))
            xp = xb[:, cols]
            zero = jnp.zeros_like(xp)
            acc = jnp.dot(scores[2 * pr], jnp.where(lo, xp, zero), preferred_element_type=F32)
            acc = acc + jnp.dot(scores[2 * pr + 1], jnp.where(lo, zero, xp), preferred_element_type=F32)
            sp = sg[:, cols]
            yoff = jnp.dot(cg, sp.astype(BF16), preferred_element_type=F32) * ecs[:, cols]
            ys.append(acc + yoff)
            news.append(jnp.exp(tot_e[:, cols]) * sp
                        + lax.dot_general(bg, xd[:, cols], _TN, preferred_element_type=F32))
        y_ref[...] = jnp.concatenate(ys, axis=1)
        st_ref[g] = jnp.concatenate(news, axis=1)

    in_specs = [pl.BlockSpec((ln, gw), lambda c, g: (c, g)),
                pl.BlockSpec((ln, SSD_STATE), lambda c, g: (c, boff + g)),
                pl.BlockSpec((ln, SSD_STATE), lambda c, g: (c, boff + n_groups + g)),
                pl.BlockSpec((ln, LANE), lambda c, g: (c, 0)),
                pl.BlockSpec((1, LANE), lambda c, g: (0, 0))]
    out_specs = [pl.BlockSpec((ln, gw), lambda c, g: (c, g)),
                 pl.BlockSpec((1, SSD_STATE, gw), lambda c, g: (c, 0, g))]
    return pl.pallas_call(
        body, name="ssd_fwd", grid=(nc, n_groups), in_specs=in_specs, out_specs=out_specs,
        out_shape=[jax.ShapeDtypeStruct((s, inner), F32), jax.ShapeDtypeStruct((nc, SSD_STATE, inner), F32)],
        scratch_shapes=[pltpu.VMEM((n_groups, SSD_STATE, gw), F32), pltpu.VMEM((LANE, ln), F32)],
        compiler_params=_params("arbitrary", "arbitrary"),
    )(xbc, xbc, xbc, dt, alog)


def _ssd_bwd(xbc, dt, alog, dy, y, sin, dskip_e, inner, n_groups):
    s = xbc.shape[0]
    ln = SSD_CHUNK
    nc = s // ln
    gw = SSD_HEADS_PER_GROUP * SSD_HEAD_DIM
    npair = gw // LANE
    boff = inner // LANE

    def body(xs_ref, b_ref, c_ref, dt_ref, alog_ref, dy_ref, y_ref, sin_ref, sout_ref, dsk_ref,
             dxs_ref, db_ref, dc_ref, ddt_ref, dalog_ref, ds_ref, cst_ref):
        step = pl.program_id(0)
        g = pl.program_id(1)

        @pl.when(step == 0)
        def _():
            ds_ref[g] = jnp.zeros((SSD_STATE, gw), F32)

        dtv = dt_ref[...]
        a_neg, _, cs, tril, eg, cs_e, dt_e = _ssd_chunk_terms(dtv, alog_ref[...], g, gw)
        cst_ref[...] = cs.T
        xs = xs_ref[...]
        x = xs * dt_e
        tot_e = cs_e[ln - 1:ln, :]
        dec = jnp.exp(tot_e - cs_e)
        ecs = jnp.exp(cs_e)
        xb = x.astype(BF16)
        xd = (x * dec).astype(BF16)
        bg = b_ref[...].astype(BF16)
        cg = c_ref[...].astype(BF16)
        cb_mat = lax.dot_general(cg, bg, _NT, preferred_element_type=F32)
        dyv = dy_ref[...]
        dyb = dyv.astype(BF16)
        dye = (ecs * dyv).astype(BF16)
        s_in = sin_ref[0]
        d_s = ds_ref[g]
        dtot_lane = _colsum(sout_ref[0] * d_s)
        lo = lax.broadcasted_iota(jnp.int32, (ln, LANE), 1) < SSD_HEAD_DIM
        heads = [_ssd_head_scores(cs, cst_ref, cb_mat, tril, g, e) for e in range(SSD_HEADS_PER_GROUP)]
        dcb = jnp.zeros((ln, ln), F32)
        dbg = jnp.zeros((ln, SSD_STATE), F32)
        dcg = jnp.zeros((ln, SSD_STATE), F32)
        dxs_parts, nds = [], []
        for pr in range(npair):
            cols = slice(LANE * pr, LANE * (pr + 1))
            xp = xb[:, cols]
            dyp = dyb[:, cols]
            zero = jnp.zeros_like(dyp)
            dxp = jnp.zeros((ln, LANE), F32)
            for e, dym in ((2 * pr, jnp.where(lo, dyp, zero)), (2 * pr + 1, jnp.where(lo, zero, dyp))):
                sc, lm = heads[e]
                dsc = lax.dot_general(dym, xp, _NT, preferred_element_type=F32)
                dcb = dcb + dsc * lm
                dxp = dxp + lax.dot_general(sc, dym, _TN, preferred_element_type=F32)
            dsp = d_s[:, cols]
            dspb = dsp.astype(BF16)
            dxp = dxp + dec[:, cols] * jnp.dot(bg, dspb, preferred_element_type=F32)
            dcg = dcg + lax.dot_general(dye[:, cols], s_in[:, cols].astype(BF16), _NT, preferred_element_type=F32)
            dbg = dbg + lax.dot_general(xd[:, cols], dspb, _NT, preferred_element_type=F32)
            nds.append(jnp.exp(tot_e[:, cols]) * dsp
                       + lax.dot_general(cg, dye[:, cols], _TN, preferred_element_type=F32))
            dxs_parts.append(dxp)
        ds_ref[g] = jnp.concatenate(nds, axis=1)
        dx = jnp.concatenate(dxs_parts, axis=1)
        dcbb = dcb.astype(BF16)
        dc_ref[...] = dcg + jnp.dot(dcbb, bg, preferred_element_type=F32)
        db_ref[...] = dbg + lax.dot_general(dcbb, cg, _TN, preferred_element_type=F32)
        dxs_ref[...] = dx * dt_e + dyv * dsk_ref[...]
        w1 = y_ref[...] * dyb.astype(F32) - xb.astype(F32) * dx
        dcs = lax.dot_general(w1, eg, _NT, preferred_element_type=F32, precision=HIGHEST)
        dtot = lax.dot_general(jnp.broadcast_to(dtot_lane, (SUBLANE, gw)), eg, _NT,
                               preferred_element_type=F32, precision=HIGHEST)[0:1, :]
        rowl = lax.broadcasted_iota(jnp.int32, (ln, LANE), 0)
        dcs = jnp.where(rowl == ln - 1, dcs + dtot, dcs)
        row = lax.broadcasted_iota(jnp.int32, (ln, ln), 0)
        col = lax.broadcasted_iota(jnp.int32, (ln, ln), 1)
        dadt = jnp.dot((row <= col).astype(F32), dcs, preferred_element_type=F32, precision=HIGHEST)
        ddt = a_neg * dadt + lax.dot_general(dx * xs, eg, _NT, preferred_element_type=F32, precision=HIGHEST)
        dal = _colsum(dadt * dtv) * a_neg

        @pl.when(g == 0)
        def _():
            ddt_ref[...] = ddt

        @pl.when(g > 0)
        def _():
            ddt_ref[...] += ddt

        @pl.when((step == 0) & (g == 0))
        def _():
            dalog_ref[...] = dal

        @pl.when((step > 0) | (g > 0))
        def _():
            dalog_ref[...] += dal

    def rc(step):
        return nc - 1 - step

    in_specs = [pl.BlockSpec((ln, gw), lambda t, g: (rc(t), g)),
                pl.BlockSpec((ln, SSD_STATE), lambda t, g: (rc(t), boff + g)),
                pl.BlockSpec((ln, SSD_STATE), lambda t, g: (rc(t), boff + n_groups + g)),
                pl.BlockSpec((ln, LANE), lambda t, g: (rc(t), 0)),
                pl.BlockSpec((1, LANE), lambda t, g: (0, 0)),
                pl.BlockSpec((ln, gw), lambda t, g: (rc(t), g)),
                pl.BlockSpec((ln, gw), lambda t, g: (rc(t), g)),
                pl.BlockSpec((1, SSD_STATE, gw), lambda t, g: (rc(t), 0, g)),
                pl.BlockSpec((1, SSD_STATE, gw), lambda t, g: (jnp.minimum(rc(t) + 1, nc - 1), 0, g)),
                pl.BlockSpec((1, gw), lambda t, g: (0, g))]
    out_specs = [pl.BlockSpec((ln, gw), lambda t, g: (rc(t), g)),
                 pl.BlockSpec((ln, SSD_STATE), lambda t, g: (rc(t), g)),
                 pl.BlockSpec((ln, SSD_STATE), lambda t, g: (rc(t), g)),
                 pl.BlockSpec((ln, LANE), lambda t, g: (rc(t), 0)),
                 pl.BlockSpec((1, LANE), lambda t, g: (0, 0))]
    gn = n_groups * SSD_STATE
    return pl.pallas_call(
        body, name="ssd_bwd", grid=(nc, n_groups), in_specs=in_specs, out_specs=out_specs,
        out_shape=[jax.ShapeDtypeStruct((s, inner), F32), jax.ShapeDtypeStruct((s, gn), F32),
                   jax.ShapeDtypeStruct((s, gn), F32), jax.ShapeDtypeStruct((s, LANE), F32),
                   jax.ShapeDtypeStruct((1, LANE), F32)],
        scratch_shapes=[pltpu.VMEM((n_groups, SSD_STATE, gw), F32), pltpu.VMEM((LANE, ln), F32)],
        compiler_params=_params("arbitrary", "arbitrary"),
    )(xbc, xbc, xbc, dt, alog, dy, y, sin, sin, dskip_e)


def _ssd_gate_norm_fwd(ysc, xbc, z, dskip_e, norm_g, n_groups):
    s, inner = ysc.shape
    gsz = inner // n_groups

    def body(i, n, y_ref, x_ref, z_ref, d_ref, g_ref, o_ref):
        y2 = (y_ref[...] + d_ref[...] * x_ref[...]) * _silu(z_ref[...])
        gg = g_ref[...]
        outs = []
        for k in range(n_groups):
            cols = slice(k * gsz, (k + 1) * gsz)
            v = y2[:, cols]
            r = lax.rsqrt(jnp.mean(v * v, axis=-1, keepdims=True) + NORM_EPS)
            outs.append(v * r * gg[:, cols])
        o_ref[...] = jnp.concatenate(outs, axis=1).astype(BF16)

    return _rowwise("ssd_gate_norm_fwd", body,
                    [("row", ysc, inner), ("row", xbc, inner), ("row", z, inner), ("vec", dskip_e, inner),
                     ("vec", norm_g, inner)], [("row", None, inner, inner, BF16)], s, 128)[0]


def _ssd_gate_norm_bwd(dyn, ysc, xbc, z, dskip_e, norm_g, n_groups):
    s, inner = ysc.shape
    gsz = inner // n_groups

    def body(i, n, dn_ref, y_ref, x_ref, z_ref, d_ref, g_ref, dy_ref, dz_ref, dg_ref, dd_ref):
        xs = x_ref[...]
        zz = z_ref[...]
        y = y_ref[...] + d_ref[...] * xs
        sz = _silu(zz)
        y2 = y * sz
        dn = dn_ref[...]
        gg = g_ref[...]
        dy2s, dgs = [], []
        for k in range(n_groups):
            cols = slice(k * gsz, (k + 1) * gsz)
            v = y2[:, cols]
            d = dn[:, cols]
            r = lax.rsqrt(jnp.mean(v * v, axis=-1, keepdims=True) + NORM_EPS)
            vh = v * r
            dgs.append(_colsum(d * vh))
            dvh = d * gg[:, cols]
            dy2s.append(r * (dvh - vh * jnp.mean(dvh * vh, axis=-1, keepdims=True)))
        dy2 = jnp.concatenate(dy2s, axis=1)
        dy = dy2 * sz
        dy_ref[...] = dy
        dz_ref[...] = (dy2 * y * _dsilu(zz)).astype(BF16)
        _acc(i, dg_ref, jnp.concatenate(dgs, axis=1))
        _acc(i, dd_ref, _colsum(dy * xs))

    return _rowwise("ssd_gate_norm_bwd", body,
                    [("row", dyn, inner), ("row", ysc, inner), ("row", xbc, inner), ("row", z, inner),
                     ("vec", dskip_e, inner), ("vec", norm_g, inner)],
                    [("row", None, inner, inner, F32), ("row", None, inner, inner, BF16),
                     ("acc", 1, inner, inner, F32), ("acc", 1, inner, inner, F32)], s, 128)


def _adamw(w, g, m, v, name):
    rows, c = w.shape
    bc1 = 1.0 - ADAM_B1 ** ADAM_STEP
    bc2 = 1.0 - ADAM_B2 ** ADAM_STEP

    def body(i, n, w_ref, g_ref, m_ref, v_ref, d_ref, mo_ref, vo_ref):
        gg = g_ref[...]
        mn = ADAM_B1 * m_ref[...] + (1.0 - ADAM_B1) * gg
        vn = ADAM_B2 * v_ref[...] + (1.0 - ADAM_B2) * (gg * gg)
        d_ref[...] = -ADAM_LR * ((mn / bc1) / (jnp.sqrt(vn / bc2) + ADAM_EPS) + ADAM_WD * w_ref[...])
        mo_ref[...] = mn
        vo_ref[...] = vn

    ts = 128 if rows % 128 == 0 else rows
    return _rowwise(name, body, [("row", w, c), ("row", g, c), ("row", m, c), ("row", v, c)],
                    [("row", None, c, c, F32)] * 3, rows, ts)


_ANY = pl.BlockSpec(memory_space=pl.ANY)


def _place():
    x, y, c = lax.axis_index("x"), lax.axis_index("y"), lax.axis_index("c")
    chips = [(1 - x, y), (x, 1 - y), (1 - x, 1 - y)]
    return x, y, c, chips


def _rcopy(src, dst, ssem, rsem, dev):
    return pltpu.make_async_remote_copy(src_ref=src, dst_ref=dst, send_sem=ssem, recv_sem=rsem, device_id=dev,
                                        device_id_type=MESH)


def _place_shard(shard, q_idx, dtype, name):
    r, cc = shard.shape
    tr = _tile(r, 256, 16)

    def body(q_ref, s_ref, o_ref):
        o_ref[0] = s_ref[...].astype(dtype)

    grid_spec = pltpu.PrefetchScalarGridSpec(
        num_scalar_prefetch=1, grid=(r // tr,),
        in_specs=[pl.BlockSpec((tr, cc), lambda i, q_ref: (i, 0))],
        out_specs=pl.BlockSpec((1, tr, cc), lambda i, q_ref: (q_ref[0], i, 0)))
    return pl.pallas_call(body, name=name, grid_spec=grid_spec, out_shape=jax.ShapeDtypeStruct((N_CHIPS, r, cc), dtype),
                          compiler_params=_params("parallel"))(q_idx, shard)


_HBM = pl.BlockSpec(memory_space=pltpu.HBM)
_SEM = pl.BlockSpec(memory_space=pltpu.SEMAPHORE)
_EFFECT = pltpu.SideEffectType.DATAFLOW_SIDE_EFFECTING


def _in_hbm(arrs):
    return [pltpu.with_memory_space_constraint(a, pltpu.HBM) for a in arrs]


def _gather_start(bufs, name):
    n = len(bufs)
    half = [e.shape[1] // 2 for e in bufs]

    def body(*refs):
        ins, send, recv, token = refs[:n], refs[n], refs[n + 1], refs[2 * n + 2]
        x, y, c, chips = _place()
        q = 2 * x + y
        for e in range(n):
            blk = ins[e].at[q, pl.ds(c * half[e], half[e])]
            for j, (cx, cy) in enumerate(chips):
                _rcopy(blk, blk, send.at[3 * e + j], recv.at[3 * e + j], (cx, cy, c)).start()
        token[...] = jnp.zeros_like(token)

    out = pl.pallas_call(
        body, name=name,
        out_shape=(pltpu.SemaphoreType.DMA((3 * n,)), pltpu.SemaphoreType.DMA((3 * n,)),
                   *[pltpu.HBM(b.shape, b.dtype) for b in bufs], jax.ShapeDtypeStruct((SUBLANE, LANE), F32)),
        in_specs=[_HBM] * n, out_specs=(_SEM, _SEM, *[_HBM] * n, pl.BlockSpec(memory_space=pltpu.VMEM)),
        input_output_aliases={e: 2 + e for e in range(n)},
        compiler_params=pltpu.CompilerParams(has_side_effects=_EFFECT))(*_in_hbm(bufs))
    return out[0], out[1], list(out[2:2 + n]), out[2 + n]


def _gather_wait(send, recv, bufs, after, name):
    n = len(bufs)
    half = [e.shape[1] // 2 for e in bufs]

    def body(*refs):
        ins, send_ref, recv_ref = refs[:n], refs[n], refs[n + 1]
        x, y, c, chips = _place()
        q = 2 * x + y
        for e in range(n):
            rows = pl.ds(c * half[e], half[e])
            for j, (cx, cy) in enumerate(chips):
                cp = _rcopy(ins[e].at[q, rows], ins[e].at[2 * cx + cy, rows], send_ref.at[3 * e + j],
                            recv_ref.at[3 * e + j], (cx, cy, c))
                cp.wait_send()
                cp.wait_recv()

    return list(pl.pallas_call(
        body, name=name, out_shape=tuple(pltpu.HBM(b.shape, b.dtype) for b in bufs),
        in_specs=[_HBM] * n + [_SEM, _SEM, _ANY], out_specs=[_HBM] * n,
        input_output_aliases={e: e for e in range(n)},
        compiler_params=pltpu.CompilerParams(has_side_effects=_EFFECT))(*bufs, send, recv, after))


def _forward_sibling(bufs, name):
    n = len(bufs)
    half = [e.shape[1] // 2 for e in bufs]

    def body(*refs):
        outs = refs[n:2 * n]
        send, recv = refs[2 * n:]
        x, y, c, chips = _place()
        sib = (x, y, 1 - c)
        cps = []
        for e in range(n):
            for j, (cx, cy) in enumerate(chips):
                blk = outs[e].at[2 * cx + cy, pl.ds(c * half[e], half[e])]
                cps.append(_rcopy(blk, blk, send.at[3 * e + j], recv.at[3 * e + j], sib))
        for cp in cps:
            cp.start()
        for e in range(n):
            for j, (cx, cy) in enumerate(chips):
                blk = outs[e].at[2 * cx + cy, pl.ds((1 - c) * half[e], half[e])]
                _rcopy(blk, blk, send.at[3 * e + j], recv.at[3 * e + j], sib).wait_recv()
        for cp in cps:
            cp.wait_send()

    return list(pl.pallas_call(
        body, name=name, in_specs=[_ANY] * n, out_specs=[_ANY] * n,
        out_shape=[jax.ShapeDtypeStruct(e.shape, e.dtype) for e in bufs],
        input_output_aliases={e: e for e in range(n)},
        scratch_shapes=[pltpu.SemaphoreType.DMA((3 * n,))] * 2,
    )(*bufs))


def _scatter_start(entries, name):
    n = len(entries)
    lands = [lax.empty(e.shape, e.dtype) for e in entries]

    def body(*refs):
        ins, land, send, recv, token = refs[:n], refs[n:2 * n], refs[2 * n], refs[2 * n + 1], refs[4 * n + 2]
        x, y, c, chips = _place()
        q = 2 * x + y
        for e in range(n):
            for j, (cx, cy) in enumerate(chips):
                _rcopy(ins[e].at[2 * cx + cy], land[e].at[q], send.at[3 * e + j], recv.at[3 * e + j],
                       (cx, cy, c)).start()
        token[...] = jnp.zeros_like(token)

    out = pl.pallas_call(
        body, name=name,
        out_shape=(pltpu.SemaphoreType.DMA((3 * n,)), pltpu.SemaphoreType.DMA((3 * n,)),
                   *[pltpu.HBM(b.shape, b.dtype) for b in entries + lands],
                   jax.ShapeDtypeStruct((SUBLANE, LANE), F32)),
        in_specs=[_HBM] * (2 * n),
        out_specs=(_SEM, _SEM, *[_HBM] * (2 * n), pl.BlockSpec(memory_space=pltpu.VMEM)),
        input_output_aliases={e: 2 + e for e in range(2 * n)},
        compiler_params=pltpu.CompilerParams(has_side_effects=_EFFECT))(*_in_hbm(entries + lands))
    return out[0], out[1], list(out[2:2 + n]), list(out[2 + n:2 + 2 * n]), out[2 + 2 * n]


def _scatter_wait(send, recv, entries, lands, after, name):
    n = len(entries)

    def body(*refs):
        ins, land, send_ref, recv_ref = refs[:n], refs[n:2 * n], refs[2 * n], refs[2 * n + 1]
        x, y, c, chips = _place()
        for e in range(n):
            for j, (cx, cy) in enumerate(chips):
                k = 2 * cx + cy
                cp = _rcopy(ins[e].at[k], land[e].at[k], send_ref.at[3 * e + j], recv_ref.at[3 * e + j],
                            (cx, cy, c))
                cp.wait_send()
                cp.wait_recv()

    out = pl.pallas_call(
        body, name=name, out_shape=tuple(pltpu.HBM(b.shape, b.dtype) for b in entries + lands),
        in_specs=[_HBM] * (2 * n) + [_SEM, _SEM, _ANY], out_specs=[_HBM] * (2 * n),
        input_output_aliases={e: e for e in range(2 * n)},
        compiler_params=pltpu.CompilerParams(has_side_effects=_EFFECT))(*entries, *lands, send, recv, after)
    return list(out[:n]), list(out[n:])


def _swap_halves(entries, name):
    n = len(entries)
    half = [e.shape[1] // 2 for e in entries]

    def body(*refs):
        ins, outs = refs[:n], refs[n:2 * n]
        send, recv = refs[2 * n:]
        x, y, c, _ = _place()
        cps = [_rcopy(ins[e].at[:, pl.ds((1 - c) * half[e], half[e]), :], outs[e], send.at[e], recv.at[e],
                      (x, y, 1 - c)) for e in range(n)]
        for cp in cps:
            cp.start()
        for cp in cps:
            cp.wait()

    return pl.pallas_call(
        body, name=name, in_specs=[_ANY] * n, out_specs=[_ANY] * n,
        out_shape=[jax.ShapeDtypeStruct((N_CHIPS, h, e.shape[2]), e.dtype) for e, h in zip(entries, half)],
        scratch_shapes=[pltpu.SemaphoreType.DMA((n,))] * 2,
    )(*entries)


def _join_halves(bufs, name):
    n = len(bufs)
    pairs = [(o, layer) for o in range(n) for layer in range(bufs[o].shape[0])]
    npair = len(pairs)

    def body(*refs):
        outs = refs[n:2 * n]
        send, recv = refs[2 * n:]
        x, y, c, _ = _place()
        cps = []
        for k, (o, layer) in enumerate(pairs):
            r2 = bufs[o].shape[1] // 2
            blk = outs[o].at[layer, pl.ds(c * r2, r2)]
            cps.append(_rcopy(blk, blk, send.at[k], recv.at[k], (x, y, 1 - c)))
        for cp in cps:
            cp.start()
        for k, (o, layer) in enumerate(pairs):
            r2 = bufs[o].shape[1] // 2
            blk = outs[o].at[layer, pl.ds((1 - c) * r2, r2)]
            _rcopy(blk, blk, send.at[k], recv.at[k], (x, y, 1 - c)).wait_recv()
        for cp in cps:
            cp.wait_send()

    return pl.pallas_call(
        body, name=name, in_specs=[_ANY] * n, out_specs=[_ANY] * n,
        out_shape=[jax.ShapeDtypeStruct(b.shape, b.dtype) for b in bufs],
        input_output_aliases={e: e for e in range(n)},
        scratch_shapes=[pltpu.SemaphoreType.DMA((npair,))] * 2,
    )(*bufs)


def _gather_all(v, name):
    def body(v_ref, o_ref, send, recv, loc):
        x, y, c, _ = _place()
        me = 4 * x + 2 * y + c
        mine = pltpu.make_async_copy(v_ref, o_ref.at[me], loc)
        mine.start()
        peers = []
        for k in range(1, N_DEV):
            px = 1 - x if k & 4 else x
            py = 1 - y if k & 2 else y
            pc = 1 - c if k & 1 else c
            peers.append((px, py, pc))
        cps = [_rcopy(v_ref, o_ref.at[me], send.at[k], recv.at[k], peers[k]) for k in range(N_DEV - 1)]
        for cp in cps:
            cp.start()
        for k, (px, py, pc) in enumerate(peers):
            blk = o_ref.at[4 * px + 2 * py + pc]
            _rcopy(blk, blk, send.at[k], recv.at[k], (px, py, pc)).wait_recv()
        for cp in cps:
            cp.wait_send()
        mine.wait()

    return pl.pallas_call(
        body, name=name, in_specs=[_ANY], out_specs=_ANY, out_shape=jax.ShapeDtypeStruct((N_DEV,) + v.shape, v.dtype),
        scratch_shapes=[pltpu.SemaphoreType.DMA((N_DEV - 1,))] * 2 + [pltpu.SemaphoreType.DMA],
    )(v)


def _add_own_half(gst, rx, c_idx, name):
    _, r, cc = gst.shape
    r2 = r // 2
    tr = _tile(r2, 256, 16)
    g4 = gst.reshape(N_CHIPS, 2, r2, cc)

    def body(c_ref, g_ref, r_ref, o_ref):
        o_ref[...] = (g_ref[0].astype(F32) + r_ref[...].astype(F32)).astype(BF16)

    grid_spec = pltpu.PrefetchScalarGridSpec(
        num_scalar_prefetch=1, grid=(N_CHIPS, r2 // tr),
        in_specs=[pl.BlockSpec((1, 1, tr, cc), lambda k, i, c_ref: (k, c_ref[0], i, 0)),
                  pl.BlockSpec((1, tr, cc), lambda k, i, c_ref: (k, i, 0))],
        out_specs=pl.BlockSpec((1, tr, cc), lambda k, i, c_ref: (k, i, 0)))
    return pl.pallas_call(body, name=name, grid_spec=grid_spec, out_shape=jax.ShapeDtypeStruct((N_CHIPS, r2, cc), BF16),
                          compiler_params=_params("parallel", "parallel"))(c_idx, g4, rx)


def _sum_into(buf, rx, own, layer, qc, out_shape, name):
    _, r2, cc = rx.shape
    tr = _tile(r2, 256, 16)
    nb = r2 // tr

    def body(qc_ref, *refs):
        rx_ref, own_ref, o_ref = refs[-3:]
        q = qc_ref[0]
        acc = None
        for k in range(N_CHIPS):
            v = jnp.where(q == k, own_ref[0], rx_ref[k]).astype(F32)
            acc = v if acc is None else acc + v
        o_ref[0] = acc

    in_specs = [pl.BlockSpec((N_CHIPS, tr, cc), lambda i, qc_ref: (0, i, 0)),
                pl.BlockSpec((1, tr, cc), lambda i, qc_ref: (qc_ref[0], i, 0))]
    args = (rx, own)
    aliases = {}
    if buf is not None:
        in_specs = [_ANY] + in_specs
        args = (buf,) + args
        aliases = {1: 0}
    grid_spec = pltpu.PrefetchScalarGridSpec(
        num_scalar_prefetch=1, grid=(nb,), in_specs=in_specs,
        out_specs=pl.BlockSpec((1, tr, cc), lambda i, qc_ref: (layer, qc_ref[1] * nb + i, 0)))
    return pl.pallas_call(body, name=name, grid_spec=grid_spec, out_shape=jax.ShapeDtypeStruct(out_shape, F32),
                          input_output_aliases=aliases, compiler_params=_params("parallel"))(qc, *args)


def _sum_slots(st, name):
    k, r, cc = st.shape
    tr = _tile(r, 256, 8)

    def body(s_ref, o_ref):
        acc = s_ref[0].astype(F32)
        for j in range(1, k):
            acc = acc + s_ref[j].astype(F32)
        o_ref[...] = acc

    return pl.pallas_call(body, name=name, grid=(r // tr,), in_specs=[pl.BlockSpec((k, tr, cc), lambda i: (0, i, 0))],
                          out_specs=pl.BlockSpec((tr, cc), lambda i: (i, 0)),
                          out_shape=jax.ShapeDtypeStruct((r, cc), F32), compiler_params=_params("parallel"))(st)


def _pack(arrs, rows_mult=2 * SUBLANE):
    flat = jnp.concatenate([a.reshape(-1).astype(F32) for a in arrs])
    quantum = rows_mult * LANE
    padded = -(-flat.shape[0] // quantum) * quantum
    return jnp.pad(flat, (0, padded - flat.shape[0])).reshape(-1, LANE)


def _unpack(buf, shapes):
    flat = buf.reshape(-1)
    out, off = [], 0
    for sh in shapes:
        size = 1
        for d in sh:
            size *= d
        out.append(flat[off:off + size].reshape(sh))
        off += size
    return out


def kernel(x, p, norm_mix_g, norm_ffn_g, norm_ple_g, final_norm_g, a_w_in, a_conv_w, a_conv_b, a_w_gate_r, a_b_gate_r, a_w_gate_i, a_b_gate_i, a_lambda, a_w_out, b_w_in, b_conv_w, b_conv_b, b_dt_bias, b_a_log, b_d_skip, b_norm_g, b_w_out, ffn_w_gate, ffn_w_up, ffn_w_down, ple_w_proj, ple_w_gate, loss_target, m_norm_mix_g, m_norm_ffn_g, m_norm_ple_g, m_final_norm_g, m_a_w_in, m_a_conv_w, m_a_conv_b, m_a_w_gate_r, m_a_b_gate_r, m_a_w_gate_i, m_a_b_gate_i, m_a_lambda, m_a_w_out, m_b_w_in, m_b_conv_w, m_b_conv_b, m_b_dt_bias, m_b_a_log, m_b_d_skip, m_b_norm_g, m_b_w_out, m_ffn_w_gate, m_ffn_w_up, m_ffn_w_down, m_ple_w_proj, m_ple_w_gate, v_norm_mix_g, v_norm_ffn_g, v_norm_ple_g, v_final_norm_g, v_a_w_in, v_a_conv_w, v_a_conv_b, v_a_w_gate_r, v_a_b_gate_r, v_a_w_gate_i, v_a_b_gate_i, v_a_lambda, v_a_w_out, v_b_w_in, v_b_conv_w, v_b_conv_b, v_b_dt_bias, v_b_a_log, v_b_d_skip, v_b_norm_g, v_b_w_out, v_ffn_w_gate, v_ffn_w_up, v_ffn_w_down, v_ple_w_proj, v_ple_w_gate):
    names = ["norm_mix_g", "norm_ffn_g", "norm_ple_g", "final_norm_g", "a_w_in", "a_conv_w", "a_conv_b", "a_w_gate_r",
             "a_b_gate_r", "a_w_gate_i", "a_b_gate_i", "a_lambda", "a_w_out", "b_w_in", "b_conv_w", "b_conv_b",
             "b_dt_bias", "b_a_log", "b_d_skip", "b_norm_g", "b_w_out", "ffn_w_gate", "ffn_w_up", "ffn_w_down",
             "ple_w_proj", "ple_w_gate"]
    w_in = dict(zip(names, [norm_mix_g, norm_ffn_g, norm_ple_g, final_norm_g, a_w_in, a_conv_w, a_conv_b, a_w_gate_r,
                            a_b_gate_r, a_w_gate_i, a_b_gate_i, a_lambda, a_w_out, b_w_in, b_conv_w, b_conv_b,
                            b_dt_bias, b_a_log, b_d_skip, b_norm_g, b_w_out, ffn_w_gate, ffn_w_up, ffn_w_down,
                            ple_w_proj, ple_w_gate]))
    m_in = dict(zip(names, [m_norm_mix_g, m_norm_ffn_g, m_norm_ple_g, m_final_norm_g, m_a_w_in, m_a_conv_w,
                            m_a_conv_b, m_a_w_gate_r, m_a_b_gate_r, m_a_w_gate_i, m_a_b_gate_i, m_a_lambda,
                            m_a_w_out, m_b_w_in, m_b_conv_w, m_b_conv_b, m_b_dt_bias, m_b_a_log, m_b_d_skip,
                            m_b_norm_g, m_b_w_out, m_ffn_w_gate, m_ffn_w_up, m_ffn_w_down, m_ple_w_proj,
                            m_ple_w_gate]))
    v_in = dict(zip(names, [v_norm_mix_g, v_norm_ffn_g, v_norm_ple_g, v_final_norm_g, v_a_w_in, v_a_conv_w,
                            v_a_conv_b, v_a_w_gate_r, v_a_b_gate_r, v_a_w_gate_i, v_a_b_gate_i, v_a_lambda,
                            v_a_w_out, v_b_w_in, v_b_conv_w, v_b_conv_b, v_b_dt_bias, v_b_a_log, v_b_d_skip,
                            v_b_norm_g, v_b_w_out, v_ffn_w_gate, v_ffn_w_up, v_ffn_w_down, v_ple_w_proj,
                            v_ple_w_gate]))

    s, d = x.shape[1], x.shape[2]
    depth = norm_mix_g.shape[0]
    assert depth == 2
    q_idx = 2 * lax.axis_index("x") + lax.axis_index("y")
    c_idx = lax.axis_index("c").astype(jnp.int32).reshape(1)

    inner = b_w_out.shape[1] * N_CHIPS
    n_heads = inner // SSD_HEAD_DIM
    n_groups = n_heads // SSD_HEADS_PER_GROUP
    gn = n_groups * SSD_STATE
    xbcw = inner + 2 * gn
    assert b_conv_w.shape[2] * N_CHIPS == xbcw and n_heads <= LANE

    big = [("a_w_in", "col"), ("a_w_gate_r", "gate"), ("a_w_gate_i", "gate"), ("a_w_out", "row"),
           ("ffn_w_gate", "col"), ("ffn_w_up", "col"), ("ffn_w_down", "row"), ("ple_w_proj", "col"),
           ("ple_w_gate", "row"), ("b_w_in", "col"), ("b_w_out", "row")]
    kind_of = dict(big)

    def shard2d(name, arr):
        if kind_of[name] == "gate":
            return [arr[l].reshape(-1, arr.shape[-1]) for l in range(arr.shape[0])]
        return [arr[l] for l in range(arr.shape[0])]

    small_sharded = ["a_conv_w", "a_b_gate_r", "a_b_gate_i", "b_conv_w", "b_conv_b", "b_norm_g"]
    small_pack = _pack([w_in[nm] for nm in small_sharded], rows_mult=16)

    q_vec = q_idx.astype(jnp.int32).reshape(1)
    qc_vec = jnp.stack([q_idx, lax.axis_index("c")]).astype(jnp.int32)
    placed, entry_keys = {}, []
    for nm, _ in big:
        for l, sh in enumerate(shard2d(nm, w_in[nm])):
            placed[(nm, l)] = _place_shard(sh, q_vec, BF16, f"place_{nm}{l}")
            entry_keys.append((nm, l))
    placed[("small", 0)] = _place_shard(small_pack, q_vec, F32, "place_small")
    gather_groups = [
        [("a_w_in", 0), ("a_w_gate_r", 0), ("a_w_gate_i", 0), ("a_w_out", 0), ("small", 0)],
        [("ffn_w_gate", 0), ("ffn_w_up", 0), ("ffn_w_down", 0), ("ple_w_proj", 0), ("ple_w_gate", 0)],
        [("b_w_in", 0), ("b_w_out", 0)],
        [("ffn_w_gate", 1), ("ffn_w_up", 1), ("ffn_w_down", 1), ("ple_w_proj", 1), ("ple_w_gate", 1)],
    ]
    gather_started = [_gather_start([placed[k] for k in grp], f"gather_start{gi}")
                      for gi, grp in enumerate(gather_groups)]
    all_started = sum(st[3][0:1, 0:1] for st in gather_started)
    wst = {}

    def gather_finish(gi, after):
        send, recv, thru, _ = gather_started[gi]
        landed = _gather_wait(send, recv, thru, after, f"gather_wait{gi}")
        for k, arr in zip(gather_groups[gi], _forward_sibling(landed, f"gather_forward{gi}")):
            wst[k] = arr

    gather_finish(0, all_started)
    small_st = wst[("small", 0)]
    kept_stacked = ("ffn_w_gate", "ffn_w_up", "ple_w_proj")

    def whole(nm, l):
        st = wst[(nm, l)]
        kind = kind_of[nm]
        if nm in kept_stacked:
            return st
        if kind == "row":
            return st.reshape(-1, st.shape[-1])
        if kind == "col":
            return jnp.concatenate([st[k] for k in range(N_CHIPS)], axis=1)
        heads = w_in[nm].shape[1]
        return st.reshape(N_CHIPS, heads, -1, st.shape[-1]).transpose(1, 0, 2, 3).reshape(heads, -1, st.shape[-1])

    small_parts = [_unpack(small_st[k], [w_in[nm].shape for nm in small_sharded]) for k in range(N_CHIPS)]
    small_full = {nm: jnp.concatenate([small_parts[k][i] for k in range(N_CHIPS)], axis=-1)
                  for i, nm in enumerate(small_sharded)}
    a_cw = small_full["a_conv_w"][0]
    a_br = small_full["a_b_gate_r"][0].reshape(1, -1)
    a_bi = small_full["a_b_gate_i"][0].reshape(1, -1)
    b_cw = small_full["b_conv_w"][0]
    b_cb = small_full["b_conv_b"]
    b_ng = small_full["b_norm_g"]

    def pad_lanes(v):
        return jnp.pad(v, ((0, 0), (0, LANE - v.shape[1])))

    dt_bias = pad_lanes(b_dt_bias)
    a_log = pad_lanes(b_a_log)
    dskip_e = jnp.repeat(b_d_skip, SSD_HEAD_DIM, axis=1)

    w_a_in = whole("a_w_in", 0)
    w_ax, w_ay = w_a_in[:, :d], w_a_in[:, d:]
    w_ar, w_ai, w_ao = whole("a_w_gate_r", 0), whole("a_w_gate_i", 0), whole("a_w_out", 0)
    w_fg, w_fu, w_fd, w_pp, w_pg = ([None] * depth for _ in range(5))

    def take_ffn_ple(l):
        w_fg[l], w_fu[l], w_fd[l] = whole("ffn_w_gate", l), whole("ffn_w_up", l), whole("ffn_w_down", l)
        w_pp[l], w_pg[l] = whole("ple_w_proj", l), whole("ple_w_gate", l)

    grads = {}

    h0 = x[0]
    g_mix = [norm_mix_g[l:l + 1] for l in range(depth)]
    g_ffn = [norm_ffn_g[l:l + 1] for l in range(depth)]
    g_ple = [norm_ple_g[l:l + 1] for l in range(depth)]
    g_fin = final_norm_g.reshape(1, -1)

    u0 = _rmsnorm_fwd(h0, g_mix[0], "norm_mix0")
    xr_pre = _mm(u0, w_ax, name="lru_in_x")
    yg = _mm(u0, w_ay, name="lru_in_y")
    xr = _conv_fwd(xr_pre, a_cw, a_conv_b, False, "lru_conv")
    lru_a, lru_b = _lru_gates_fwd(xr, w_ar, w_ai, a_br, a_bi, a_lambda)
    hs = _scan(lru_a, lru_b, False, "lru_scan")
    y_lru = _lru_out_fwd(hs, yg)
    h_mix = [_mm(y_lru, w_ao, add=h0, name="lru_out"), None]

    def ffn_ple_fwd(h_in, l):
        n_f = _rmsnorm_fwd(h_in, g_ffn[l], f"norm_ffn{l}")
        gt = _mm(n_f, w_fg[l], stacked_b=True, name=f"ffn_gate{l}")
        up = _mm(n_f, w_fu[l], stacked_b=True, name=f"ffn_up{l}")
        act = _swiglu_act(gt, up)
        h_f = _mm(act, w_fd[l], add=h_in, name=f"ffn_down{l}")
        n_p = _rmsnorm_fwd(h_f, g_ple[l], f"norm_ple{l}")
        gp = _mm(n_p, w_pg[l], name=f"ple_gate{l}")
        pp = _mm(p[l, 0], w_pp[l], stacked_b=True, name=f"ple_proj{l}")
        h_out = _ple_fwd(h_f, gp, pp)
        return h_out, dict(h_in=h_in, n_f=n_f, gt=gt, up=up, act=act, h_f=h_f, n_p=n_p, gp=gp, pp=pp)

    gather_finish(1, h_mix[0])
    take_ffn_ple(0)
    h_l0, sv0 = ffn_ple_fwd(h_mix[0], 0)

    gather_finish(2, h_l0)
    w_b_in = whole("b_w_in", 0)
    w_bz, w_bx = w_b_in[:, :inner], w_b_in[:, inner:inner + xbcw]
    w_bd = pad_lanes(w_b_in[:, inner + xbcw:])
    w_bo = whole("b_w_out", 0)
    u1 = _rmsnorm_fwd(h_l0, g_mix[1], "norm_mix1")
    z = _mm(u1, w_bz, name="ssd_in_z")
    xbc_pre = _mm(u1, w_bx, name="ssd_in_xbc")
    dt_pre = _mm(u1, w_bd, name="ssd_in_dt")
    xbc = _conv_fwd(xbc_pre, b_cw, b_cb, True, "ssd_conv")
    dt = _dt_fwd(dt_pre, dt_bias, n_heads)
    ysc, s_in = _ssd_fwd(xbc, dt, a_log, inner, n_groups)
    yn = _ssd_gate_norm_fwd(ysc, xbc, z, dskip_e, b_ng, n_groups)
    h_mix[1] = _mm(yn, w_bo, add=h_l0, name="ssd_out")
    gather_finish(3, h_mix[1])
    take_ffn_ple(1)
    h_l1, sv1 = ffn_ple_fwd(h_mix[1], 1)

    dh, dg_fin, loss_row = _final_loss_bwd(h_l1, g_fin, loss_target[0])

    d_norm_ffn, d_norm_ple, d_norm_mix = [None] * depth, [None] * depth, [None] * depth
    for nm in ("ffn_w_gate", "ffn_w_up", "ffn_w_down", "ple_w_proj", "ple_w_gate"):
        grads[nm] = [None] * depth

    def stacked(nm, gfull):
        kind = kind_of[nm]
        if nm in kept_stacked:
            return gfull
        if kind == "row":
            return gfull.reshape(N_CHIPS, -1, gfull.shape[-1])
        if kind == "col":
            n_loc = gfull.shape[1] // N_CHIPS
            return jnp.stack([gfull[:, k * n_loc:(k + 1) * n_loc] for k in range(N_CHIPS)])
        heads, bw, _ = gfull.shape
        return gfull.reshape(heads, N_CHIPS, bw // N_CHIPS, bw).transpose(1, 0, 2, 3).reshape(N_CHIPS, -1, bw)

    reduce_started = []

    def reduce_start(keys, tag):
        gst = [stacked(nm, grads[nm][l]) for nm, l in keys]
        from_sib = _swap_halves(gst, f"reduce_swap_{tag}")
        chip_sum = [_add_own_half(g, r, c_idx, f"reduce_add_{nm}{l}") for g, r, (nm, l) in zip(gst, from_sib, keys)]
        send, recv, ents, lands, token = _scatter_start(chip_sum, f"reduce_scatter_start_{tag}")
        reduce_started.append((keys, tag, send, recv, ents, lands))
        return token[0:1, 0:1]

    def ffn_ple_keys(l):
        return [("ple_w_gate", l), ("ple_w_proj", l), ("ffn_w_down", l), ("ffn_w_gate", l), ("ffn_w_up", l)]

    def ffn_ple_bwd(dh_out, sv, l, g_ple_l):
        dgp, dpp = _ple_bwd(dh_out, sv["gp"], sv["pp"])
        grads["ple_w_gate"][l] = _mm(sv["n_p"], dgp, ta=True, out_dtype=BF16, name=f"ple_gate_dw{l}")
        grads["ple_w_proj"][l] = _mm(p[l, 0], dpp, ta=True, out_dtype=BF16, stacked_out=True,
                                     name=f"ple_proj_dw{l}")
        dn = _mm(dgp, w_pg[l], tb=True, name=f"ple_gate_dx{l}")
        dh_f, d_norm_ple[l] = _rmsnorm_bwd(dn, sv["h_f"], g_ple_l, dh_out, f"norm_ple_bwd{l}")
        grads["ffn_w_down"][l] = _mm(sv["act"], dh_f, ta=True, out_dtype=BF16, name=f"ffn_down_dw{l}")
        dact = _mm(dh_f, w_fd[l], tb=True, name=f"ffn_down_dx{l}")
        dgt, dup = _swiglu_bwd(dact, sv["gt"], sv["up"])
        grads["ffn_w_gate"][l] = _mm(sv["n_f"], dgt, ta=True, out_dtype=BF16, stacked_out=True,
                                     name=f"ffn_gate_dw{l}")
        grads["ffn_w_up"][l] = _mm(sv["n_f"], dup, ta=True, out_dtype=BF16, stacked_out=True, name=f"ffn_up_dw{l}")
        dn = _mm(dgt, w_fg[l], tb=True, stacked_b=True, name=f"ffn_gate_dx{l}")
        dn = _mm(dup, w_fu[l], tb=True, stacked_b=True, add=dn, name=f"ffn_up_dx{l}")
        dh_in, d_norm_ffn[l] = _rmsnorm_bwd(dn, sv["h_in"], g_ffn[l], dh_f, f"norm_ffn_bwd{l}")
        return dh_in

    dh = ffn_ple_bwd(dh, sv1, 1, g_ple[1])
    tok = reduce_start(ffn_ple_keys(1), "l1")

    grads["b_w_out"] = [_mm(yn, dh, ta=True, out_dtype=BF16, name="ssd_out_dw")]
    dyn = _mm(dh, w_bo, tb=True, name="ssd_out_dx")
    dy_ssd, dz, d_b_norm_g, dd_lane = _ssd_gate_norm_bwd(dyn, ysc, xbc, z, dskip_e, b_ng + tok, n_groups)
    dxs, d_bm, d_cm, ddt, d_a_log = _ssd_bwd(xbc, dt, a_log, dy_ssd, ysc, s_in, dskip_e, inner, n_groups)
    dxbc = jnp.concatenate([dxs, d_bm, d_cm], axis=1)
    dconv = _silu_conv_bwd_pre(dxbc, xbc_pre, b_cw, b_cb, "ssd_conv_bwd_pre")
    dxbc_pre, d_b_conv_w, d_b_conv_b = _conv_bwd(dconv, xbc_pre, b_cw, "ssd_conv_bwd")
    ddt_pre, d_dt_bias = _dt_bwd(ddt, dt_pre, dt_bias, n_heads)
    gw_bz = _mm(u1, dz, ta=True, out_dtype=BF16, name="ssd_in_z_dw")
    gw_bx = _mm(u1, dxbc_pre, ta=True, out_dtype=BF16, name="ssd_in_xbc_dw")
    gw_bd = _mm(u1, ddt_pre, ta=True, out_dtype=BF16, name="ssd_in_dt_dw")
    grads["b_w_in"] = [jnp.concatenate([gw_bz, gw_bx, gw_bd[:, :n_heads]], axis=1)]
    du = _mm(dz, w_bz, tb=True, name="ssd_in_z_dx")
    du = _mm(dxbc_pre, w_bx, tb=True, add=du, name="ssd_in_xbc_dx")
    du = _mm(ddt_pre, w_bd, tb=True, add=du, name="ssd_in_dt_dx")
    dh, d_norm_mix[1] = _rmsnorm_bwd(du, h_l0, g_mix[1], dh, "norm_mix_bwd1")
    tok = reduce_start([("b_w_out", 0), ("b_w_in", 0)], "ssd")

    dh = ffn_ple_bwd(dh, sv0, 0, g_ple[0] + tok)
    tok = reduce_start(ffn_ple_keys(0), "l0")

    grads["a_w_out"] = [_mm(y_lru, dh, ta=True, out_dtype=BF16, name="lru_out_dw")]
    dy_lru = _mm(dh, w_ao, tb=True, name="lru_out_dx")
    dhs, dyg = _lru_out_bwd(dy_lru, hs, yg)
    g_scan = _scan(lru_a, dhs, True, "lru_scan_bwd")
    dxr, d_wr, d_wi, d_br, d_bi, d_lam = _lru_gates_bwd(xr, g_scan, hs, w_ar, w_ai, a_br, a_bi, a_lambda + tok)
    dxr_pre, d_a_conv_w, d_a_conv_b = _conv_bwd(dxr, xr_pre, a_cw, "lru_conv_bwd")
    gw_ax = _mm(u0, dxr_pre, ta=True, out_dtype=BF16, name="lru_in_x_dw")
    gw_ay = _mm(u0, dyg, ta=True, out_dtype=BF16, name="lru_in_y_dw")
    grads["a_w_in"] = [jnp.concatenate([gw_ax, gw_ay], axis=1)]
    grads["a_w_gate_r"] = [d_wr.astype(BF16)]
    grads["a_w_gate_i"] = [d_wi.astype(BF16)]
    du = _mm(dxr_pre, w_ax, tb=True, name="lru_in_x_dx")
    du = _mm(dyg, w_ay, tb=True, add=du, name="lru_in_y_dx")
    grad_x, d_norm_mix[0] = _rmsnorm_bwd(du, h0, g_mix[0], dh, "norm_mix_bwd0")

    reduce_start([("a_w_out", 0), ("a_w_in", 0), ("a_w_gate_r", 0), ("a_w_gate_i", 0)], "lru")
    g_half = {}
    for keys, tag, send, recv, ents, lands in reduce_started:
        ents, lands = _scatter_wait(send, recv, ents, lands, grad_x, f"reduce_scatter_wait_{tag}")
        for rx, own, (nm, l) in zip(lands, ents, keys):
            sh = shard2d(nm, w_in[nm])
            g_half[nm] = _sum_into(g_half.get(nm), rx, own, l, qc_vec, (len(sh),) + sh[0].shape,
                                   f"reduce_sum_{nm}{l}")
    g_big = dict(zip([nm for nm, _ in big], _join_halves([g_half[nm] for nm, _ in big], "reduce_join_halves")))

    small_full_grads = {
        "norm_mix_g": jnp.concatenate(d_norm_mix, axis=0), "norm_ffn_g": jnp.concatenate(d_norm_ffn, axis=0),
        "norm_ple_g": jnp.concatenate(d_norm_ple, axis=0), "final_norm_g": dg_fin[0],
        "a_conv_w": d_a_conv_w[None], "a_conv_b": d_a_conv_b,
        "a_b_gate_r": d_br.reshape(a_b_gate_r.shape[0], a_b_gate_r.shape[1], -1),
        "a_b_gate_i": d_bi.reshape(a_b_gate_i.shape[0], a_b_gate_i.shape[1], -1),
        "a_lambda": d_lam, "b_conv_w": d_b_conv_w[None], "b_conv_b": d_b_conv_b,
        "b_dt_bias": d_dt_bias[:, :n_heads], "b_a_log": d_a_log[:, :n_heads],
        "b_d_skip": dd_lane.reshape(1, n_heads, SSD_HEAD_DIM).sum(axis=-1), "b_norm_g": d_b_norm_g,
    }
    small_names = list(small_full_grads)
    small_shapes = [small_full_grads[nm].shape for nm in small_names]
    packed = _pack([loss_row] + [small_full_grads[nm] for nm in small_names])
    total = _sum_slots(_gather_all(packed, "gather_small_grads"), "sum_small_grads")
    parts = _unpack(total, [(1, LANE)] + small_shapes)
    loss = parts[0][0, 0]
    g_small = {}
    for nm, gfull in zip(small_names, parts[1:]):
        if nm in small_sharded:
            n_loc = w_in[nm].shape[-1]
            gfull = lax.dynamic_slice_in_dim(gfull, q_idx * n_loc, n_loc, axis=gfull.ndim - 1)
        g_small[nm] = gfull

    grad_out, delta_out, m_out, v_out = {}, {}, {}, {}
    for nm, _ in big:
        shape = w_in[nm].shape
        gfull = g_big[nm]
        cols = gfull.shape[-1]
        dl, mn, vn = _adamw(w_in[nm].reshape(-1, cols), gfull.reshape(-1, cols), m_in[nm].reshape(-1, cols),
                            v_in[nm].reshape(-1, cols), f"adamw_{nm}")
        grad_out[nm], delta_out[nm] = gfull.reshape(shape), dl.reshape(shape)
        m_out[nm], v_out[nm] = mn.reshape(shape), vn.reshape(shape)
    sm_shapes = [w_in[nm].shape for nm in small_names]
    dl, mn, vn = _adamw(_pack([w_in[nm] for nm in small_names]), _pack([g_small[nm] for nm in small_names]),
                        _pack([m_in[nm] for nm in small_names]), _pack([v_in[nm] for nm in small_names]),
                        "adamw_small")
    for nm, a, b_, c_ in zip(small_names, _unpack(dl, sm_shapes), _unpack(mn, sm_shapes), _unpack(vn, sm_shapes)):
        grad_out[nm] = g_small[nm].reshape(w_in[nm].shape)
        delta_out[nm], m_out[nm], v_out[nm] = a, b_, c_

    return (loss, grad_x[None], *[grad_out[nm] for nm in names], *[delta_out[nm] for nm in names],
            *[m_out[nm] for nm in names], *[v_out[nm] for nm in names])
```

```python
import functools

import jax
import jax.numpy as jnp
from jax import lax
from jax.experimental import pallas as pl
from jax.experimental.pallas import tpu as pltpu

F32 = jnp.float32
BF16 = jnp.bfloat16
MESH = pl.DeviceIdType.MESH
HIGHEST = lax.Precision.HIGHEST

NORM_EPS = 1e-6
LRU_C = 8.0
CONV_WIDTH = 4
SSD_HEAD_DIM = 64
SSD_STATE = 128
SSD_CHUNK = 128
SSD_HEADS_PER_GROUP = 8
LANE = 128
SUBLANE = 8
N_CHIPS = 4
N_DEV = 8
VMEM_LIMIT = 48 * 1024 * 1024

ADAM_LR = 0.001
ADAM_B1 = 0.9
ADAM_B2 = 0.999
ADAM_EPS = 1e-08
ADAM_WD = 0.01
ADAM_STEP = 10


def _tile(n, cap, mult=LANE):
    best = None
    for t in range(mult, min(n, cap) + 1, mult):
        if n % t == 0:
            best = t
    return best if best is not None else n


def _params(*sem):
    return pltpu.CompilerParams(dimension_semantics=sem, vmem_limit_bytes=VMEM_LIMIT)


def _mm(a, b, *, ta=False, tb=False, add=None, out_dtype=F32, stacked_b=False, stacked_out=0, name):
    if ta:
        kd, m = a.shape
    else:
        m, kd = a.shape
    n_loc = None
    if stacked_b:
        slots, kb, n_loc = b.shape
        if tb:
            n, kb = kb, slots * n_loc
        else:
            n = slots * n_loc
    elif tb:
        n, kb = b.shape
    else:
        kb, n = b.shape
    assert kd == kb, (a.shape, b.shape, ta, tb)
    tm = _tile(m, 1408)
    tn = _tile(n, 1408)
    tk = _tile(kd, 2048)
    if stacked_b and tb:
        tk = _tile(n_loc, 1408)
    elif stacked_b:
        tn = _tile(n_loc, 1408)
    if stacked_out:
        n_loc = n // stacked_out
        tn = _tile(n_loc, 1408)
    nk = kd // tk
    dims = (((0 if ta else 1,), (1 if tb else 0,)), ((), ()))

    def body(*refs):
        a_ref, b_ref = refs[:2]
        add_ref = refs[2] if add is not None else None
        o_ref = refs[3] if add is not None else refs[2]
        bv = b_ref[0] if stacked_b else b_ref[...]
        part = lax.dot_general(a_ref[...].astype(BF16), bv.astype(BF16), dims, preferred_element_type=F32)

        def finish(r):
            if add is not None:
                r = r + add_ref[...]
            if stacked_out:
                o_ref[0] = r.astype(out_dtype)
            else:
                o_ref[...] = r.astype(out_dtype)

        if nk == 1:
            finish(part)
        else:
            acc_ref = refs[-1]
            k = pl.program_id(2)

            @pl.when(k == 0)
            def _():
                acc_ref[...] = part

            @pl.when((k > 0) & (k < nk - 1))
            def _():
                acc_ref[...] += part

            @pl.when(k == nk - 1)
            def _():
                finish(acc_ref[...] + part)

    a_spec = pl.BlockSpec((tk, tm), lambda i, j, k: (k, i)) if ta else pl.BlockSpec((tm, tk), lambda i, j, k: (i, k))
    if stacked_b and tb:
        per = n_loc // tk
        b_spec = pl.BlockSpec((1, tn, tk), lambda i, j, k: (k // per, j, k % per))
    elif stacked_b:
        per = n_loc // tn
        b_spec = pl.BlockSpec((1, tk, tn), lambda i, j, k: (j // per, k, j % per))
    elif tb:
        b_spec = pl.BlockSpec((tn, tk), lambda i, j, k: (j, k))
    else:
        b_spec = pl.BlockSpec((tk, tn), lambda i, j, k: (k, j))
    o_spec = pl.BlockSpec((tm, tn), lambda i, j, k: (i, j))
    in_specs = [a_spec, b_spec] + ([o_spec] if add is not None else [])
    args = (a, b) + ((add,) if add is not None else ())
    if stacked_out:
        per_o = n_loc // tn
        out_spec = pl.BlockSpec((1, tm, tn), lambda i, j, k: (j // per_o, i, j % per_o))
        out_shape = jax.ShapeDtypeStruct((stacked_out, m, n_loc), out_dtype)
    else:
        out_spec, out_shape = o_spec, jax.ShapeDtypeStruct((m, n), out_dtype)
    return pl.pallas_call(
        body, name=name, grid=(m // tm, n // tn, nk), in_specs=in_specs, out_specs=out_spec, out_shape=out_shape,
        scratch_shapes=[pltpu.VMEM((tm, tn), F32)] if nk > 1 else [],
        compiler_params=_params("parallel", "parallel", "arbitrary"))(*args)


def _rowwise(name, body, ins, outs, nrows, ts, ncol=1):
    ts = min(ts, nrows)
    nrow = nrows // ts
    hb = ts // SUBLANE
    nb8 = nrows // SUBLANE
    in_specs, args = [], []
    for kind, arr, cb in ins:
        if kind == "row":
            spec = pl.BlockSpec((ts, cb), lambda j, i: (i, j))
        elif kind == "prev":
            spec = pl.BlockSpec((SUBLANE, cb), lambda j, i: (jnp.maximum(i * hb - 1, 0), j))
        elif kind == "next":
            spec = pl.BlockSpec((SUBLANE, cb), lambda j, i: (jnp.minimum((i + 1) * hb, nb8 - 1), j))
        else:
            spec = pl.BlockSpec((arr.shape[0], cb), lambda j, i: (0, j))
        in_specs.append(spec)
        args.append(arr)
    out_specs, out_shape = [], []
    for kind, rows, ctot, cb, dt in outs:
        if kind == "row":
            out_shape.append(jax.ShapeDtypeStruct((nrows, ctot), dt))
            out_specs.append(pl.BlockSpec((ts, cb), lambda j, i: (i, j)))
        else:
            out_shape.append(jax.ShapeDtypeStruct((rows, ctot), dt))
            out_specs.append(pl.BlockSpec((rows, cb), lambda j, i: (0, j)))

    def kern(*refs):
        body(pl.program_id(1), nrow, *refs)

    return pl.pallas_call(kern, name=name, grid=(ncol, nrow), in_specs=in_specs, out_specs=out_specs,
                          out_shape=out_shape, compiler_params=_params("parallel", "arbitrary"))(*args)


def _colsum(x):
    return jnp.sum(x, axis=0, keepdims=True)


def _acc(i, ref, val):
    @pl.when(i == 0)
    def _():
        ref[...] = val

    @pl.when(i > 0)
    def _():
        ref[...] += val


def _shift_down(x, halo, k):
    xx = jnp.concatenate([halo, x], axis=0)
    return pltpu.roll(xx, k, axis=0)[SUBLANE:, :]


def _shift_up(x, halo, k):
    xx = jnp.concatenate([x, halo], axis=0)
    n = xx.shape[0]
    return pltpu.roll(xx, n - k, axis=0)[: x.shape[0], :]


def _sigmoid(x):
    return 1.0 / (1.0 + jnp.exp(-x))


def _silu(x):
    return x * _sigmoid(x)


def _dsilu(x):
    s = _sigmoid(x)
    return s * (1.0 + x * (1.0 - s))


_GELU_K = 0.7978845608028654
_GELU_C = 0.044715


def _gelu(x):
    return 0.5 * x * (1.0 + jnp.tanh(_GELU_K * (x + _GELU_C * x * x * x)))


def _dgelu(x):
    t = jnp.tanh(_GELU_K * (x + _GELU_C * x * x * x))
    return 0.5 * (1.0 + t) + 0.5 * x * (1.0 - t * t) * _GELU_K * (1.0 + 3.0 * _GELU_C * x * x)


def _softplus(x):
    return jnp.maximum(x, 0.0) + jnp.log1p(jnp.exp(-jnp.abs(x)))


def _neg_expm1(x):
    poly = -x * (1.0 + x * (0.5 + x * (1.0 / 6.0 + x * (1.0 / 24.0 + x * (1.0 / 120.0)))))
    return jnp.where(x > -0.05, poly, 1.0 - jnp.exp(x))


def _rmsnorm_fwd(h, g, name):
    s, d = h.shape

    def body(i, n, h_ref, g_ref, o_ref):
        x = h_ref[...]
        r = lax.rsqrt(jnp.mean(x * x, axis=-1, keepdims=True) + NORM_EPS)
        o_ref[...] = (x * r * g_ref[...]).astype(BF16)

    return _rowwise(name, body, [("row", h, d), ("vec", g, d)], [("row", None, d, d, BF16)], s, 256)[0]


def _rmsnorm_bwd(dn, h, g, dres, name):
    s, d = h.shape

    def body(i, n, dn_ref, h_ref, g_ref, dres_ref, dh_ref, dg_ref):
        x = h_ref[...]
        dy = dn_ref[...].astype(F32)
        r = lax.rsqrt(jnp.mean(x * x, axis=-1, keepdims=True) + NORM_EPS)
        xh = x * r
        _acc(i, dg_ref, _colsum(dy * xh))
        dxh = dy * g_ref[...]
        dh_ref[...] = dres_ref[...] + r * (dxh - xh * jnp.mean(dxh * xh, axis=-1, keepdims=True))

    return _rowwise(name, body, [("row", dn, d), ("row", h, d), ("vec", g, d), ("row", dres, d)],
                    [("row", None, d, d, F32), ("acc", 1, d, d, F32)], s, 256)


def _final_loss_bwd(h, g, tgt):
    s, d = h.shape

    def body(i, n, h_ref, g_ref, t_ref, dh_ref, dg_ref, loss_ref):
        x = h_ref[...]
        gg = g_ref[...]
        r = lax.rsqrt(jnp.mean(x * x, axis=-1, keepdims=True) + NORM_EPS)
        xh = x * r
        err = xh * gg - t_ref[...]
        part = 0.5 * jnp.sum(jnp.mean(err * err, axis=-1, keepdims=True), axis=0, keepdims=True)
        _acc(i, loss_ref, jnp.broadcast_to(part, (1, LANE)))
        dy = err * (1.0 / d)
        _acc(i, dg_ref, _colsum(dy * xh))
        dxh = dy * gg
        dh_ref[...] = r * (dxh - xh * jnp.mean(dxh * xh, axis=-1, keepdims=True))

    return _rowwise("final_loss_bwd", body, [("row", h, d), ("vec", g, d), ("row", tgt, d)],
                    [("row", None, d, d, F32), ("acc", 1, d, d, F32), ("acc", 1, LANE, LANE, F32)], s, 256)


def _conv_rows(x, halo, w, b):
    y = b + w[3:4, :] * x
    for k in range(CONV_WIDTH - 1):
        y = y + w[k:k + 1, :] * _shift_down(x, halo, CONV_WIDTH - 1 - k)
    return y


def _conv_fwd(x, w, b, silu, name):
    s, c = x.shape
    cb = _tile(c, 512)

    def body(i, n, x_ref, p_ref, w_ref, b_ref, o_ref):
        halo = jnp.where(i == 0, 0.0, p_ref[...])
        y = _conv_rows(x_ref[...], halo, w_ref[...], b_ref[...])
        o_ref[...] = _silu(y) if silu else y

    return _rowwise(name, body, [("row", x, cb), ("prev", x, cb), ("vec", w, cb), ("vec", b, cb)],
                    [("row", None, c, cb, F32)], s, 512, ncol=c // cb)[0]


def _silu_conv_bwd_pre(dy, x, w, b, name):
    s, c = x.shape
    cb = _tile(c, 512)

    def body(i, n, dy_ref, x_ref, p_ref, w_ref, b_ref, o_ref):
        halo = jnp.where(i == 0, 0.0, p_ref[...])
        y = _conv_rows(x_ref[...], halo, w_ref[...], b_ref[...])
        o_ref[...] = dy_ref[...] * _dsilu(y)

    return _rowwise(name, body, [("row", dy, cb), ("row", x, cb), ("prev", x, cb), ("vec", w, cb), ("vec", b, cb)],
                    [("row", None, c, cb, F32)], s, 512, ncol=c // cb)[0]


def _conv_bwd(dy, x, w, name):
    s, c = x.shape
    cb = _tile(c, 512)

    def body(i, n, dy_ref, nx_ref, x_ref, p_ref, w_ref, dx_ref, dw_ref, db_ref):
        d = dy_ref[...]
        xx = x_ref[...]
        wv = w_ref[...]
        nxt = jnp.where(i == n - 1, 0.0, nx_ref[...])
        prv = jnp.where(i == 0, 0.0, p_ref[...])
        dx = wv[3:4, :] * d
        parts = []
        for k in range(CONV_WIDTH - 1):
            sh = CONV_WIDTH - 1 - k
            dx = dx + wv[k:k + 1, :] * _shift_up(d, nxt, sh)
            parts.append(_colsum(d * _shift_down(xx, prv, sh)))
        parts.append(_colsum(d * xx))
        dx_ref[...] = dx.astype(BF16)
        _acc(i, dw_ref, jnp.concatenate(parts, axis=0))
        _acc(i, db_ref, _colsum(d))

    return _rowwise(name, body, [("row", dy, cb), ("next", dy, cb), ("row", x, cb), ("prev", x, cb), ("vec", w, cb)],
                    [("row", None, c, cb, BF16), ("acc", CONV_WIDTH, c, cb, F32), ("acc", 1, c, cb, F32)],
                    s, 512, ncol=c // cb)


def _lru_gate_math(xr, r_pre, i_pre, lam):
    r = _sigmoid(r_pre)
    ig = _sigmoid(i_pre)
    sp = _softplus(-lam)
    log_a = -LRU_C * r * sp
    a = jnp.exp(log_a)
    mult = jnp.sqrt(_neg_expm1(2.0 * log_a))
    return r, ig, sp, a, mult


def _lru_gates_fwd(xr, wr, wi, br, bi, lam):
    s, d = xr.shape
    nh, bw, _ = wr.shape
    ts = min(512, s)

    def body(x_ref, wr_ref, wi_ref, br_ref, bi_ref, lam_ref, a_ref, b_ref):
        x = x_ref[...]
        xb = x.astype(BF16)
        r_pre = jnp.dot(xb, wr_ref[0], preferred_element_type=F32) + br_ref[...]
        i_pre = jnp.dot(xb, wi_ref[0], preferred_element_type=F32) + bi_ref[...]
        _, ig, _, a, mult = _lru_gate_math(x, r_pre, i_pre, lam_ref[...])
        a_ref[...] = a
        b_ref[...] = mult * (ig * x)

    row = pl.BlockSpec((ts, bw), lambda h, i: (i, h))
    wsp = pl.BlockSpec((1, bw, bw), lambda h, i: (h, 0, 0))
    vec = pl.BlockSpec((1, bw), lambda h, i: (0, h))
    return pl.pallas_call(
        body, name="lru_gates_fwd", grid=(nh, s // ts), in_specs=[row, wsp, wsp, vec, vec, vec], out_specs=[row, row],
        out_shape=[jax.ShapeDtypeStruct((s, d), F32)] * 2, compiler_params=_params("parallel", "arbitrary"),
    )(xr, wr, wi, br, bi, lam)


def _lru_gates_bwd(xr, g, hs, wr, wi, br, bi, lam):
    s, d = xr.shape
    nh, bw, _ = wr.shape
    ts = min(512, s)
    hb = ts // SUBLANE
    tn_dims = (((0,), (0,)), ((), ()))
    nt_dims = (((1,), (1,)), ((), ()))

    def body(x_ref, g_ref, hs_ref, hp_ref, wr_ref, wi_ref, br_ref, bi_ref, lam_ref,
             dx_ref, dwr_ref, dwi_ref, dbr_ref, dbi_ref, dlam_ref):
        i = pl.program_id(1)
        x = x_ref[...]
        xb = x.astype(BF16)
        gg = g_ref[...]
        lam_v = lam_ref[...]
        r_pre = jnp.dot(xb, wr_ref[0], preferred_element_type=F32) + br_ref[...]
        i_pre = jnp.dot(xb, wi_ref[0], preferred_element_type=F32) + bi_ref[...]
        r, ig, sp, a, mult = _lru_gate_math(x, r_pre, i_pre, lam_v)
        h_prev = _shift_down(hs_ref[...], jnp.where(i == 0, 0.0, hp_ref[...]), 1)
        da = gg * h_prev
        dmult = gg * ig * x
        dlog_a = da * a - dmult * (a * a) / mult
        d_r = dlog_a * (-LRU_C * sp)
        dr_pre = d_r * r * (1.0 - r)
        di_pre = (gg * mult * x) * ig * (1.0 - ig)
        drb = dr_pre.astype(BF16)
        dib = di_pre.astype(BF16)
        dx_ref[...] = (gg * mult * ig
                       + lax.dot_general(drb, wr_ref[0], nt_dims, preferred_element_type=F32)
                       + lax.dot_general(dib, wi_ref[0], nt_dims, preferred_element_type=F32))
        dwr = lax.dot_general(xb, drb, tn_dims, preferred_element_type=F32)[None]
        dwi = lax.dot_general(xb, dib, tn_dims, preferred_element_type=F32)[None]
        dlam = _colsum(dlog_a * (-LRU_C * r)) * (-_sigmoid(-lam_v))
        _acc(i, dwr_ref, dwr)
        _acc(i, dwi_ref, dwi)
        _acc(i, dbr_ref, _colsum(dr_pre))
        _acc(i, dbi_ref, _colsum(di_pre))
        _acc(i, dlam_ref, dlam)

    row = pl.BlockSpec((ts, bw), lambda h, i: (i, h))
    prev = pl.BlockSpec((SUBLANE, bw), lambda h, i: (jnp.maximum(i * hb - 1, 0), h))
    wsp = pl.BlockSpec((1, bw, bw), lambda h, i: (h, 0, 0))
    vec = pl.BlockSpec((1, bw), lambda h, i: (0, h))
    return pl.pallas_call(
        body, name="lru_gates_bwd", grid=(nh, s // ts),
        in_specs=[row, row, row, prev, wsp, wsp, vec, vec, vec], out_specs=[row, wsp, wsp, vec, vec, vec],
        out_shape=[jax.ShapeDtypeStruct((s, d), F32), jax.ShapeDtypeStruct((nh, bw, bw), F32),
                   jax.ShapeDtypeStruct((nh, bw, bw), F32)] + [jax.ShapeDtypeStruct((1, d), F32)] * 3,
        compiler_params=_params("parallel", "arbitrary"),
    )(xr, g, hs, hs, wr, wi, br, bi, lam)


def _scan(a, b, reverse, name):
    s, c = a.shape
    cb = _tile(c, 512)
    nt = s // SUBLANE

    def body(a_ref, b_ref, o_ref):
        row = lax.broadcasted_iota(jnp.int32, (SUBLANE, cb), 0)

        def fwd_step(t, carry):
            r0 = pl.multiple_of(t * SUBLANE, SUBLANE)
            aa = a_ref[pl.ds(r0, SUBLANE), :]
            bb = b_ref[pl.ds(r0, SUBLANE), :]
            for sh in (1, 2, 4):
                a_s = jnp.where(row >= sh, pltpu.roll(aa, sh, axis=0), 1.0)
                b_s = jnp.where(row >= sh, pltpu.roll(bb, sh, axis=0), 0.0)
                bb = aa * b_s + bb
                aa = aa * a_s
            h = bb + aa * carry
            o_ref[pl.ds(r0, SUBLANE), :] = h
            return h[SUBLANE - 1:SUBLANE, :]

        def rev_step(k, carry):
            r0 = pl.multiple_of((nt - 1 - k) * SUBLANE, SUBLANE)
            aa = a_ref[pl.ds(r0, SUBLANE), :]
            dd = b_ref[pl.ds(r0, SUBLANE), :]
            cc = aa * dd
            for sh in (1, 2, 4):
                a_s = jnp.where(row < SUBLANE - sh, pltpu.roll(aa, SUBLANE - sh, axis=0), 1.0)
                c_s = jnp.where(row < SUBLANE - sh, pltpu.roll(cc, SUBLANE - sh, axis=0), 0.0)
                cc = cc + aa * c_s
                aa = aa * a_s
            big = cc + aa * carry
            nxt = jnp.where(row < SUBLANE - 1, pltpu.roll(big, SUBLANE - 1, axis=0), carry)
            o_ref[pl.ds(r0, SUBLANE), :] = dd + nxt
            return big[0:1, :]

        lax.fori_loop(0, nt, rev_step if reverse else fwd_step, jnp.zeros((1, cb), F32))

    spec = pl.BlockSpec((s, cb), lambda j: (0, j))
    return pl.pallas_call(body, name=name, grid=(c // cb,), in_specs=[spec, spec], out_specs=spec,
                          out_shape=jax.ShapeDtypeStruct((s, c), F32), compiler_params=_params("parallel"))(a, b)


def _lru_out_fwd(hs, yg):
    s, d = hs.shape
    cb = _tile(d, 1024)

    def body(i, n, h_ref, y_ref, o_ref):
        o_ref[...] = (h_ref[...] * _gelu(y_ref[...])).astype(BF16)

    return _rowwise("lru_out_fwd", body, [("row", hs, cb), ("row", yg, cb)], [("row", None, d, cb, BF16)],
                    s, 512, ncol=d // cb)[0]


def _lru_out_bwd(dy, hs, yg):
    s, d = hs.shape
    cb = _tile(d, 1024)

    def body(i, n, dy_ref, h_ref, y_ref, dh_ref, dyg_ref):
        dyv = dy_ref[...]
        y = y_ref[...]
        dh_ref[...] = dyv * _gelu(y)
        dyg_ref[...] = (dyv * h_ref[...] * _dgelu(y)).astype(BF16)

    return _rowwise("lru_out_bwd", body, [("row", dy, cb), ("row", hs, cb), ("row", yg, cb)],
                    [("row", None, d, cb, F32), ("row", None, d, cb, BF16)], s, 512, ncol=d // cb)


def _swiglu_act(gt, up):
    s, f = gt.shape
    cb = _tile(f, 1024)

    def body(i, n, g_ref, u_ref, o_ref):
        o_ref[...] = (_silu(g_ref[...]) * u_ref[...]).astype(BF16)

    return _rowwise("swiglu_act", body, [("row", gt, cb), ("row", up, cb)], [("row", None, f, cb, BF16)],
                    s, 512, ncol=f // cb)[0]


def _swiglu_bwd(dact, gt, up):
    s, f = gt.shape
    cb = _tile(f, 1024)

    def body(i, n, d_ref, g_ref, u_ref, dg_ref, du_ref):
        d = d_ref[...]
        g = g_ref[...]
        dg_ref[...] = (d * u_ref[...] * _dsilu(g)).astype(BF16)
        du_ref[...] = (d * _silu(g)).astype(BF16)

    return _rowwise("swiglu_bwd", body, [("row", dact, cb), ("row", gt, cb), ("row", up, cb)],
                    [("row", None, f, cb, BF16), ("row", None, f, cb, BF16)], s, 512, ncol=f // cb)


def _ple_fwd(h, gp, pp):
    s, d = h.shape
    cb = _tile(d, 1024)

    def body(i, n, h_ref, g_ref, p_ref, o_ref):
        o_ref[...] = h_ref[...] + _sigmoid(g_ref[...]) * p_ref[...]

    return _rowwise("ple_fwd", body, [("row", h, cb), ("row", gp, cb), ("row", pp, cb)], [("row", None, d, cb, F32)],
                    s, 512, ncol=d // cb)[0]


def _ple_bwd(dh, gp, pp):
    s, d = dh.shape
    cb = _tile(d, 1024)

    def body(i, n, d_ref, g_ref, p_ref, dg_ref, dp_ref):
        dv = d_ref[...]
        sg = _sigmoid(g_ref[...])
        dg_ref[...] = (dv * p_ref[...] * sg * (1.0 - sg)).astype(BF16)
        dp_ref[...] = (dv * sg).astype(BF16)

    return _rowwise("ple_bwd", body, [("row", dh, cb), ("row", gp, cb), ("row", pp, cb)],
                    [("row", None, d, cb, BF16), ("row", None, d, cb, BF16)], s, 512, ncol=d // cb)


def _group_matrix(n_groups):
    r = lax.broadcasted_iota(jnp.int32, (LANE, n_groups * LANE), 0)
    c = lax.broadcasted_iota(jnp.int32, (LANE, n_groups * LANE), 1)
    return ((c % LANE < SSD_HEADS_PER_GROUP) & (r == (c // LANE) * SSD_HEADS_PER_GROUP + c % LANE)).astype(F32)


def _dt_fwd(dt_pre, bias, n_heads, n_groups):
    s = dt_pre.shape[0]
    gl = n_groups * LANE

    def body(i, n, d_ref, b_ref, o_ref):
        lane = lax.broadcasted_iota(jnp.int32, d_ref.shape, 1)
        v = jnp.where(lane < n_heads, _softplus(d_ref[...] + b_ref[...]), 0.0)
        o_ref[...] = jnp.dot(v, _group_matrix(n_groups), preferred_element_type=F32, precision=HIGHEST)

    return _rowwise("ssd_dt_fwd", body, [("row", dt_pre, LANE), ("vec", bias, LANE)], [("row", None, gl, gl, F32)],
                    s, 512)[0]


def _dt_bwd(ddt_g, dt_pre, bias, n_heads, n_groups):
    s = dt_pre.shape[0]
    gl = n_groups * LANE

    def body(i, n, g_ref, d_ref, b_ref, o_ref, db_ref):
        lane = lax.broadcasted_iota(jnp.int32, d_ref.shape, 1)
        ddt = lax.dot_general(g_ref[...], _group_matrix(n_groups), _NT, preferred_element_type=F32, precision=HIGHEST)
        v = jnp.where(lane < n_heads, ddt * _sigmoid(d_ref[...] + b_ref[...]), 0.0)
        o_ref[...] = v.astype(BF16)
        _acc(i, db_ref, _colsum(v))

    return _rowwise("ssd_dt_bwd", body, [("row", ddt_g, gl), ("row", dt_pre, LANE), ("vec", bias, LANE)],
                    [("row", None, LANE, LANE, BF16), ("acc", 1, LANE, LANE, F32)], s, 512)


def _ssd_chunk_terms(dt, alog):
    ln = dt.shape[0]
    a_neg = -jnp.exp(alog)
    row = lax.broadcasted_iota(jnp.int32, (ln, ln), 0)
    col = lax.broadcasted_iota(jnp.int32, (ln, ln), 1)
    tril = row >= col
    cs = jnp.dot(tril.astype(F32), dt * a_neg, preferred_element_type=F32, precision=HIGHEST)
    return a_neg, cs, tril


def _head_lanes(v):
    return [jnp.broadcast_to(v[:, e:e + 1], v.shape) for e in range(SSD_HEADS_PER_GROUP)]


def _ssd_head_scores(bc_cs, cst, cb_mat, tril, e):
    lm = jnp.where(tril, jnp.exp(jnp.minimum(bc_cs[e] - cst[e:e + 1, :], 0.0)), 0.0)
    return (cb_mat * lm).astype(BF16), lm


_NT = (((1,), (1,)), ((), ()))
_TN = (((0,), (0,)), ((), ()))


def _ssd_fwd(xbc, dt, alog, inner, n_groups):
    s = xbc.shape[0]
    ln = SSD_CHUNK
    nc = s // ln
    gw = SSD_HEADS_PER_GROUP * SSD_HEAD_DIM
    npair = gw // LANE
    boff = inner // LANE

    def body(xs_ref, b_ref, c_ref, dt_ref, alog_ref, y_ref, sin_ref, st_ref):
        c = pl.program_id(1)

        @pl.when(c == 0)
        def _():
            st_ref[...] = jnp.zeros_like(st_ref)

        dtv = dt_ref[...]
        _, cs, tril = _ssd_chunk_terms(dtv, alog_ref[...])
        cst = cs.T
        bc_cs, bc_dt = _head_lanes(cs), _head_lanes(dtv)
        xs = xs_ref[...]
        bg = b_ref[...].astype(BF16)
        cg = c_ref[...].astype(BF16)
        cb_mat = lax.dot_general(cg, bg, _NT, preferred_element_type=F32)
        sg = st_ref[...]
        sin_ref[0] = sg
        lo = lax.broadcasted_iota(jnp.int32, (ln, LANE), 1) < SSD_HEAD_DIM
        ys, news = [], []
        for pr in range(npair):
            cols = slice(LANE * pr, LANE * (pr + 1))
            cs_p = jnp.where(lo, bc_cs[2 * pr], bc_cs[2 * pr + 1])
            x = xs[:, cols] * jnp.where(lo, bc_dt[2 * pr], bc_dt[2 * pr + 1])
            tot_p = cs_p[ln - 1:ln, :]
            xp = x.astype(BF16)
            xd = (x * jnp.exp(tot_p - cs_p)).astype(BF16)
            zero = jnp.zeros_like(xp)
            sc0 = _ssd_head_scores(bc_cs, cst, cb_mat, tril, 2 * pr)[0]
            sc1 = _ssd_head_scores(bc_cs, cst, cb_mat, tril, 2 * pr + 1)[0]
            acc = jnp.dot(sc0, jnp.where(lo, xp, zero), preferred_element_type=F32)
            acc = acc + jnp.dot(sc1, jnp.where(lo, zero, xp), preferred_element_type=F32)
            sp = sg[:, cols]
            yoff = jnp.dot(cg, sp.astype(BF16), preferred_element_type=F32) * jnp.exp(cs_p)
            ys.append(acc + yoff)
            news.append(jnp.exp(tot_p) * sp + lax.dot_general(bg, xd, _TN, preferred_element_type=F32))
        y_ref[...] = jnp.concatenate(ys, axis=1)
        st_ref[...] = jnp.concatenate(news, axis=1)

    in_specs = [pl.BlockSpec((ln, gw), lambda g, c: (c, g)),
                pl.BlockSpec((ln, SSD_STATE), lambda g, c: (c, boff + g)),
                pl.BlockSpec((ln, SSD_STATE), lambda g, c: (c, boff + n_groups + g)),
                pl.BlockSpec((ln, LANE), lambda g, c: (c, g)),
                pl.BlockSpec((1, LANE), lambda g, c: (0, g))]
    out_specs = [pl.BlockSpec((ln, gw), lambda g, c: (c, g)),
                 pl.BlockSpec((1, SSD_STATE, gw), lambda g, c: (c, 0, g))]
    return pl.pallas_call(
        body, name="ssd_fwd", grid=(n_groups, nc), in_specs=in_specs, out_specs=out_specs,
        out_shape=[jax.ShapeDtypeStruct((s, inner), F32), jax.ShapeDtypeStruct((nc, SSD_STATE, inner), F32)],
        scratch_shapes=[pltpu.VMEM((SSD_STATE, gw), F32)],
        compiler_params=_params("parallel", "arbitrary"),
    )(xbc, xbc, xbc, dt, alog)


def _ssd_bwd(xbc, dt, alog, dy, y, sin, dskip_e, inner, n_groups):
    s = xbc.shape[0]
    ln = SSD_CHUNK
    nc = s // ln
    gw = SSD_HEADS_PER_GROUP * SSD_HEAD_DIM
    npair = gw // LANE
    boff = inner // LANE

    def body(xs_ref, b_ref, c_ref, dt_ref, alog_ref, dy_ref, y_ref, sin_ref, sout_ref, dsk_ref,
             dxs_ref, db_ref, dc_ref, ddt_ref, dalog_ref, ds_ref):
        step = pl.program_id(1)

        @pl.when(step == 0)
        def _():
            ds_ref[...] = jnp.zeros_like(ds_ref)

        dtv = dt_ref[...]
        a_neg, cs, tril = _ssd_chunk_terms(dtv, alog_ref[...])
        cst = cs.T
        bc_cs, bc_dt = _head_lanes(cs), _head_lanes(dtv)
        xs = xs_ref[...]
        bg = b_ref[...].astype(BF16)
        cg = c_ref[...].astype(BF16)
        cb_mat = lax.dot_general(cg, bg, _NT, preferred_element_type=F32)
        dyv = dy_ref[...]
        yv = y_ref[...]
        dskv = dsk_ref[...]
        s_in = sin_ref[0]
        s_out = sout_ref[0]
        d_s = ds_ref[...]
        lane = lax.broadcasted_iota(jnp.int32, (ln, LANE), 1)
        rowl = lax.broadcasted_iota(jnp.int32, (ln, LANE), 0)
        lo = lane < SSD_HEAD_DIM
        dcb = jnp.zeros((ln, ln), F32)
        dbg = jnp.zeros((ln, SSD_STATE), F32)
        dcg = jnp.zeros((ln, SSD_STATE), F32)
        dcs = jnp.zeros((ln, LANE), F32)
        ddt_x = jnp.zeros((ln, LANE), F32)
        dxs_parts, nds = [], []

        def head_sums(v, pr, into):
            s0 = jnp.sum(jnp.where(lo, v, 0.0), axis=1, keepdims=True)
            s1 = jnp.sum(jnp.where(lo, 0.0, v), axis=1, keepdims=True)
            return into + jnp.where(lane == 2 * pr, s0, 0.0) + jnp.where(lane == 2 * pr + 1, s1, 0.0)

        for pr in range(npair):
            cols = slice(LANE * pr, LANE * (pr + 1))
            cs_p = jnp.where(lo, bc_cs[2 * pr], bc_cs[2 * pr + 1])
            dt_p = jnp.where(lo, bc_dt[2 * pr], bc_dt[2 * pr + 1])
            xs_p = xs[:, cols]
            x = xs_p * dt_p
            tot_p = cs_p[ln - 1:ln, :]
            dec = jnp.exp(tot_p - cs_p)
            xp = x.astype(BF16)
            xd = (x * dec).astype(BF16)
            dy_p = dyv[:, cols]
            dyp = dy_p.astype(BF16)
            dye = (jnp.exp(cs_p) * dy_p).astype(BF16)
            zero = jnp.zeros_like(dyp)
            dxp = jnp.zeros((ln, LANE), F32)
            for e, dym in ((2 * pr, jnp.where(lo, dyp, zero)), (2 * pr + 1, jnp.where(lo, zero, dyp))):
                sc, lm = _ssd_head_scores(bc_cs, cst, cb_mat, tril, e)
                dsc = lax.dot_general(dym, xp, _NT, preferred_element_type=F32)
                dcb = dcb + dsc * lm
                dxp = dxp + lax.dot_general(sc, dym, _TN, preferred_element_type=F32)
            dsp = d_s[:, cols]
            dspb = dsp.astype(BF16)
            dxp = dxp + dec * jnp.dot(bg, dspb, preferred_element_type=F32)
            dcg = dcg + lax.dot_general(dye, s_in[:, cols].astype(BF16), _NT, preferred_element_type=F32)
            dbg = dbg + lax.dot_general(xd, dspb, _NT, preferred_element_type=F32)
            nds.append(jnp.exp(tot_p) * dsp + lax.dot_general(cg, dye, _TN, preferred_element_type=F32))
            dxs_parts.append(dxp * dt_p + dy_p * dskv[:, cols])
            dcs = head_sums(yv[:, cols] * dyp.astype(F32) - xp.astype(F32) * dxp, pr, dcs)
            tot_row = jnp.broadcast_to(_colsum(s_out[:, cols] * dsp), (ln, LANE))
            dcs = head_sums(jnp.where(rowl == ln - 1, tot_row, 0.0), pr, dcs)
            ddt_x = head_sums(dxp * xs_p, pr, ddt_x)
        ds_ref[...] = jnp.concatenate(nds, axis=1)
        dxs_ref[...] = jnp.concatenate(dxs_parts, axis=1)
        dcbb = dcb.astype(BF16)
        dc_ref[...] = dcg + jnp.dot(dcbb, bg, preferred_element_type=F32)
        db_ref[...] = dbg + lax.dot_general(dcbb, cg, _TN, preferred_element_type=F32)
        row = lax.broadcasted_iota(jnp.int32, (ln, ln), 0)
        col = lax.broadcasted_iota(jnp.int32, (ln, ln), 1)
        dadt = jnp.dot((row <= col).astype(F32), dcs, preferred_element_type=F32, precision=HIGHEST)
        ddt_ref[...] = a_neg * dadt + ddt_x
        _acc(step, dalog_ref, _colsum(dadt * dtv) * a_neg)

    def rc(step):
        return nc - 1 - step

    in_specs = [pl.BlockSpec((ln, gw), lambda g, t: (rc(t), g)),
                pl.BlockSpec((ln, SSD_STATE), lambda g, t: (rc(t), boff + g)),
                pl.BlockSpec((ln, SSD_STATE), lambda g, t: (rc(t), boff + n_groups + g)),
                pl.BlockSpec((ln, LANE), lambda g, t: (rc(t), g)),
                pl.BlockSpec((1, LANE), lambda g, t: (0, g)),
                pl.BlockSpec((ln, gw), lambda g, t: (rc(t), g)),
                pl.BlockSpec((ln, gw), lambda g, t: (rc(t), g)),
                pl.BlockSpec((1, SSD_STATE, gw), lambda g, t: (rc(t), 0, g)),
                pl.BlockSpec((1, SSD_STATE, gw), lambda g, t: (jnp.minimum(rc(t) + 1, nc - 1), 0, g)),
                pl.BlockSpec((1, gw), lambda g, t: (0, g))]
    out_specs = [pl.BlockSpec((ln, gw), lambda g, t: (rc(t), g)),
                 pl.BlockSpec((ln, SSD_STATE), lambda g, t: (rc(t), g)),
                 pl.BlockSpec((ln, SSD_STATE), lambda g, t: (rc(t), g)),
                 pl.BlockSpec((ln, LANE), lambda g, t: (rc(t), g)),
                 pl.BlockSpec((1, LANE), lambda g, t: (0, g))]
    gn = n_groups * SSD_STATE
    return pl.pallas_call(
        body, name="ssd_bwd", grid=(n_groups, nc), in_specs=in_specs, out_specs=out_specs,
        out_shape=[jax.ShapeDtypeStruct((s, inner + 2 * gn), F32), jax.ShapeDtypeStruct((s, gn), F32),
                   jax.ShapeDtypeStruct((s, gn), F32), jax.ShapeDtypeStruct((s, n_groups * LANE), F32),
                   jax.ShapeDtypeStruct((1, n_groups * LANE), F32)],
        scratch_shapes=[pltpu.VMEM((SSD_STATE, gw), F32)],
        compiler_params=_params("parallel", "arbitrary"),
    )(xbc, xbc, xbc, dt, alog, dy, y, sin, sin, dskip_e)


def _put_cols(buf, part, block, name):
    s, w = part.shape
    ts = min(512, s)

    def body(b_ref, p_ref, o_ref):
        o_ref[...] = p_ref[...]

    return pl.pallas_call(
        body, name=name, grid=(s // ts,), in_specs=[_ANY, pl.BlockSpec((ts, w), lambda i: (i, 0))],
        out_specs=pl.BlockSpec((ts, w), lambda i: (i, block)), out_shape=jax.ShapeDtypeStruct(buf.shape, buf.dtype),
        input_output_aliases={0: 0}, compiler_params=_params("parallel"))(buf, part)


def _ssd_gate_norm_fwd(ysc, xbc, z, dskip_e, norm_g, n_groups):
    s, inner = ysc.shape
    gsz = inner // n_groups

    def body(i, n, y_ref, x_ref, z_ref, d_ref, g_ref, o_ref):
        y2 = (y_ref[...] + d_ref[...] * x_ref[...]) * _silu(z_ref[...])
        gg = g_ref[...]
        outs = []
        for k in range(n_groups):
            cols = slice(k * gsz, (k + 1) * gsz)
            v = y2[:, cols]
            r = lax.rsqrt(jnp.mean(v * v, axis=-1, keepdims=True) + NORM_EPS)
            outs.append(v * r * gg[:, cols])
        o_ref[...] = jnp.concatenate(outs, axis=1).astype(BF16)

    return _rowwise("ssd_gate_norm_fwd", body,
                    [("row", ysc, inner), ("row", xbc, inner), ("row", z, inner), ("vec", dskip_e, inner),
                     ("vec", norm_g, inner)], [("row", None, inner, inner, BF16)], s, 128)[0]


def _ssd_gate_norm_bwd(dyn, ysc, xbc, z, dskip_e, norm_g, n_groups):
    s, inner = ysc.shape
    gsz = inner // n_groups

    def body(i, n, dn_ref, y_ref, x_ref, z_ref, d_ref, g_ref, dy_ref, dz_ref, dg_ref, dd_ref):
        xs = x_ref[...]
        zz = z_ref[...]
        y = y_ref[...] + d_ref[...] * xs
        sz = _silu(zz)
        y2 = y * sz
        dn = dn_ref[...]
        gg = g_ref[...]
        dy2s, dgs = [], []
        for k in range(n_groups):
            cols = slice(k * gsz, (k + 1) * gsz)
            v = y2[:, cols]
            d = dn[:, cols]
            r = lax.rsqrt(jnp.mean(v * v, axis=-1, keepdims=True) + NORM_EPS)
            vh = v * r
            dgs.append(_colsum(d * vh))
            dvh = d * gg[:, cols]
            dy2s.append(r * (dvh - vh * jnp.mean(dvh * vh, axis=-1, keepdims=True)))
        dy2 = jnp.concatenate(dy2s, axis=1)
        dy = dy2 * sz
        dy_ref[...] = dy
        dz_ref[...] = (dy2 * y * _dsilu(zz)).astype(BF16)
        _acc(i, dg_ref, jnp.concatenate(dgs, axis=1))
        _acc(i, dd_ref, _colsum(dy * xs))

    return _rowwise("ssd_gate_norm_bwd", body,
                    [("row", dyn, inner), ("row", ysc, inner), ("row", xbc, inner), ("row", z, inner),
                     ("vec", dskip_e, inner), ("vec", norm_g, inner)],
                    [("row", None, inner, inner, F32), ("row", None, inner, inner, BF16),
                     ("acc", 1, inner, inner, F32), ("acc", 1, inner, inner, F32)], s, 128)


def _adamw(w, g, m, v, name, emit_g=False):
    rows, c = w.shape
    bc1 = 1.0 - ADAM_B1 ** ADAM_STEP
    bc2 = 1.0 - ADAM_B2 ** ADAM_STEP

    def body(i, n, w_ref, g_ref, m_ref, v_ref, d_ref, mo_ref, vo_ref, *go_ref):
        gg = g_ref[...]
        if emit_g:
            go_ref[0][...] = gg
        mn = ADAM_B1 * m_ref[...] + (1.0 - ADAM_B1) * gg
        vn = ADAM_B2 * v_ref[...] + (1.0 - ADAM_B2) * (gg * gg)
        d_ref[...] = -ADAM_LR * ((mn / bc1) / (jnp.sqrt(vn / bc2) + ADAM_EPS) + ADAM_WD * w_ref[...])
        mo_ref[...] = mn
        vo_ref[...] = vn

    ts = 128 if rows % 128 == 0 else rows
    return _rowwise(name, body, [("row", w, c), ("row", g, c), ("row", m, c), ("row", v, c)],
                    [("row", None, c, c, F32)] * (4 if emit_g else 3), rows, ts)


_ANY = pl.BlockSpec(memory_space=pl.ANY)


def _place():
    x, y, c = lax.axis_index("x"), lax.axis_index("y"), lax.axis_index("c")
    chips = [(1 - x, y), (x, 1 - y), (1 - x, 1 - y)]
    return x, y, c, chips


def _rcopy(src, dst, ssem, rsem, dev):
    return pltpu.make_async_remote_copy(src_ref=src, dst_ref=dst, send_sem=ssem, recv_sem=rsem, device_id=dev,
                                        device_id_type=MESH)


def _place_shard(shard, q_idx, dtype, name):
    r, cc = shard.shape
    tr = _tile(r, 256, 16)

    def body(q_ref, s_ref, o_ref):
        o_ref[0] = s_ref[...].astype(dtype)

    grid_spec = pltpu.PrefetchScalarGridSpec(
        num_scalar_prefetch=1, grid=(r // tr,),
        in_specs=[pl.BlockSpec((tr, cc), lambda i, q_ref: (i, 0))],
        out_specs=pl.BlockSpec((1, tr, cc), lambda i, q_ref: (q_ref[0], i, 0)))
    return pl.pallas_call(body, name=name, grid_spec=grid_spec, out_shape=jax.ShapeDtypeStruct((N_CHIPS, r, cc), dtype),
                          compiler_params=_params("parallel"))(q_idx, shard)


_HBM = pl.BlockSpec(memory_space=pltpu.HBM)
_SEM = pl.BlockSpec(memory_space=pltpu.SEMAPHORE)
_EFFECT = pltpu.SideEffectType.DATAFLOW_SIDE_EFFECTING


def _in_hbm(arrs):
    return [pltpu.with_memory_space_constraint(a, pltpu.HBM) for a in arrs]


def _gather_start(bufs, name):
    n = len(bufs)
    half = [e.shape[1] // 2 for e in bufs]

    def body(*refs):
        ins, send, recv, token = refs[:n], refs[n], refs[n + 1], refs[2 * n + 2]
        x, y, c, chips = _place()
        q = 2 * x + y
        for e in range(n):
            blk = ins[e].at[q, pl.ds(c * half[e], half[e])]
            for j, (cx, cy) in enumerate(chips):
                _rcopy(blk, blk, send.at[3 * e + j], recv.at[3 * e + j], (cx, cy, c)).start()
        token[...] = jnp.zeros_like(token)

    out = pl.pallas_call(
        body, name=name,
        out_shape=(pltpu.SemaphoreType.DMA((3 * n,)), pltpu.SemaphoreType.DMA((3 * n,)),
                   *[pltpu.HBM(b.shape, b.dtype) for b in bufs], jax.ShapeDtypeStruct((SUBLANE, LANE), F32)),
        in_specs=[_HBM] * n, out_specs=(_SEM, _SEM, *[_HBM] * n, pl.BlockSpec(memory_space=pltpu.VMEM)),
        input_output_aliases={e: 2 + e for e in range(n)},
        compiler_params=pltpu.CompilerParams(has_side_effects=_EFFECT))(*_in_hbm(bufs))
    return out[0], out[1], list(out[2:2 + n]), out[2 + n]


def _gather_wait(send, recv, bufs, after, name):
    n = len(bufs)
    half = [e.shape[1] // 2 for e in bufs]

    def body(*refs):
        ins, send_ref, recv_ref = refs[:n], refs[n], refs[n + 1]
        x, y, c, chips = _place()
        q = 2 * x + y
        for e in range(n):
            rows = pl.ds(c * half[e], half[e])
            for j, (cx, cy) in enumerate(chips):
                cp = _rcopy(ins[e].at[q, rows], ins[e].at[2 * cx + cy, rows], send_ref.at[3 * e + j],
                            recv_ref.at[3 * e + j], (cx, cy, c))
                cp.wait_send()
                cp.wait_recv()

    return list(pl.pallas_call(
        body, name=name, out_shape=tuple(pltpu.HBM(b.shape, b.dtype) for b in bufs),
        in_specs=[_HBM] * n + [_SEM, _SEM, _ANY], out_specs=[_HBM] * n,
        input_output_aliases={e: e for e in range(n)},
        compiler_params=pltpu.CompilerParams(has_side_effects=_EFFECT))(*bufs, send, recv, after))


def _forward_sibling(bufs, name):
    n = len(bufs)
    half = [e.shape[1] // 2 for e in bufs]

    def body(*refs):
        outs = refs[n:2 * n]
        send, recv = refs[2 * n:]
        x, y, c, chips = _place()
        sib = (x, y, 1 - c)
        cps = []
        for e in range(n):
            for j, (cx, cy) in enumerate(chips):
                blk = outs[e].at[2 * cx + cy, pl.ds(c * half[e], half[e])]
                cps.append(_rcopy(blk, blk, send.at[3 * e + j], recv.at[3 * e + j], sib))
        for cp in cps:
            cp.start()
        for e in range(n):
            for j, (cx, cy) in enumerate(chips):
                blk = outs[e].at[2 * cx + cy, pl.ds((1 - c) * half[e], half[e])]
                _rcopy(blk, blk, send.at[3 * e + j], recv.at[3 * e + j], sib).wait_recv()
        for cp in cps:
            cp.wait_send()

    return list(pl.pallas_call(
        body, name=name, in_specs=[_ANY] * n, out_specs=[_ANY] * n,
        out_shape=[jax.ShapeDtypeStruct(e.shape, e.dtype) for e in bufs],
        input_output_aliases={e: e for e in range(n)},
        scratch_shapes=[pltpu.SemaphoreType.DMA((3 * n,))] * 2,
    )(*bufs))


def _scatter_start(entries, name):
    n = len(entries)
    lands = [lax.empty(e.shape, e.dtype) for e in entries]

    def body(*refs):
        ins, land, send, recv, token = refs[:n], refs[n:2 * n], refs[2 * n], refs[2 * n + 1], refs[4 * n + 2]
        x, y, c, chips = _place()
        q = 2 * x + y
        for e in range(n):
            for j, (cx, cy) in enumerate(chips):
                _rcopy(ins[e].at[2 * cx + cy], land[e].at[q], send.at[3 * e + j], recv.at[3 * e + j],
                       (cx, cy, c)).start()
        token[...] = jnp.zeros_like(token)

    out = pl.pallas_call(
        body, name=name,
        out_shape=(pltpu.SemaphoreType.DMA((3 * n,)), pltpu.SemaphoreType.DMA((3 * n,)),
                   *[pltpu.HBM(b.shape, b.dtype) for b in entries + lands],
                   jax.ShapeDtypeStruct((SUBLANE, LANE), F32)),
        in_specs=[_HBM] * (2 * n),
        out_specs=(_SEM, _SEM, *[_HBM] * (2 * n), pl.BlockSpec(memory_space=pltpu.VMEM)),
        input_output_aliases={e: 2 + e for e in range(2 * n)},
        compiler_params=pltpu.CompilerParams(has_side_effects=_EFFECT))(*_in_hbm(entries + lands))
    return out[0], out[1], list(out[2:2 + n]), list(out[2 + n:2 + 2 * n]), out[2 + 2 * n]


def _scatter_wait(send, recv, entries, lands, after, name):
    n = len(entries)

    def body(*refs):
        ins, land, send_ref, recv_ref = refs[:n], refs[n:2 * n], refs[2 * n], refs[2 * n + 1]
        x, y, c, chips = _place()
        for e in range(n):
            for j, (cx, cy) in enumerate(chips):
                k = 2 * cx + cy
                cp = _rcopy(ins[e].at[k], land[e].at[k], send_ref.at[3 * e + j], recv_ref.at[3 * e + j],
                            (cx, cy, c))
                cp.wait_send()
                cp.wait_recv()

    out = pl.pallas_call(
        body, name=name, out_shape=tuple(pltpu.HBM(b.shape, b.dtype) for b in entries + lands),
        in_specs=[_HBM] * (2 * n) + [_SEM, _SEM, _ANY], out_specs=[_HBM] * (2 * n),
        input_output_aliases={e: e for e in range(2 * n)},
        compiler_params=pltpu.CompilerParams(has_side_effects=_EFFECT))(*entries, *lands, send, recv, after)
    return list(out[:n]), list(out[n:])


def _swap_halves(entries, name):
    n = len(entries)
    half = [e.shape[1] // 2 for e in entries]

    def body(*refs):
        ins, outs = refs[:n], refs[n:2 * n]
        send, recv = refs[2 * n:]
        x, y, c, _ = _place()
        cps = [_rcopy(ins[e].at[:, pl.ds((1 - c) * half[e], half[e]), :], outs[e], send.at[e], recv.at[e],
                      (x, y, 1 - c)) for e in range(n)]
        for cp in cps:
            cp.start()
        for cp in cps:
            cp.wait()

    return pl.pallas_call(
        body, name=name, in_specs=[_ANY] * n, out_specs=[_ANY] * n,
        out_shape=[jax.ShapeDtypeStruct((N_CHIPS, h, e.shape[2]), e.dtype) for e, h in zip(entries, half)],
        scratch_shapes=[pltpu.SemaphoreType.DMA((n,))] * 2,
    )(*entries)


def _join_halves(bufs, name):
    n = len(bufs)
    pairs = [(o, layer) for o in range(n) for layer in range(bufs[o].shape[0])]
    npair = len(pairs)

    def body(*refs):
        outs = refs[n:2 * n]
        send, recv = refs[2 * n:]
        x, y, c, _ = _place()
        cps = []
        for k, (o, layer) in enumerate(pairs):
            r2 = bufs[o].shape[1] // 2
            blk = outs[o].at[layer, pl.ds(c * r2, r2)]
            cps.append(_rcopy(blk, blk, send.at[k], recv.at[k], (x, y, 1 - c)))
        for cp in cps:
            cp.start()
        for k, (o, layer) in enumerate(pairs):
            r2 = bufs[o].shape[1] // 2
            blk = outs[o].at[layer, pl.ds((1 - c) * r2, r2)]
            _rcopy(blk, blk, send.at[k], recv.at[k], (x, y, 1 - c)).wait_recv()
        for cp in cps:
            cp.wait_send()

    return pl.pallas_call(
        body, name=name, in_specs=[_ANY] * n, out_specs=[_ANY] * n,
        out_shape=[jax.ShapeDtypeStruct(b.shape, b.dtype) for b in bufs],
        input_output_aliases={e: e for e in range(n)},
        scratch_shapes=[pltpu.SemaphoreType.DMA((npair,))] * 2,
    )(*bufs)


def _gather_all(v, name):
    def body(v_ref, o_ref, send, recv, loc):
        x, y, c, _ = _place()
        me = 4 * x + 2 * y + c
        mine = pltpu.make_async_copy(v_ref, o_ref.at[me], loc)
        mine.start()
        peers = []
        for k in range(1, N_DEV):
            px = 1 - x if k & 4 else x
            py = 1 - y if k & 2 else y
            pc = 1 - c if k & 1 else c
            peers.append((px, py, pc))
        cps = [_rcopy(v_ref, o_ref.at[me], send.at[k], recv.at[k], peers[k]) for k in range(N_DEV - 1)]
        for cp in cps:
            cp.start()
        for k, (px, py, pc) in enumerate(peers):
            blk = o_ref.at[4 * px + 2 * py + pc]
            _rcopy(blk, blk, send.at[k], recv.at[k], (px, py, pc)).wait_recv()
        for cp in cps:
            cp.wait_send()
        mine.wait()

    return pl.pallas_call(
        body, name=name, in_specs=[_ANY], out_specs=_ANY, out_shape=jax.ShapeDtypeStruct((N_DEV,) + v.shape, v.dtype),
        scratch_shapes=[pltpu.SemaphoreType.DMA((N_DEV - 1,))] * 2 + [pltpu.SemaphoreType.DMA],
    )(v)


def _add_own_half(gst, rx, c_idx, name):
    _, r, cc = gst.shape
    r2 = r // 2
    tr = _tile(r2, 256, 16)
    g4 = gst.reshape(N_CHIPS, 2, r2, cc)

    def body(c_ref, g_ref, r_ref, o_ref):
        o_ref[...] = (g_ref[0].astype(F32) + r_ref[...].astype(F32)).astype(BF16)

    grid_spec = pltpu.PrefetchScalarGridSpec(
        num_scalar_prefetch=1, grid=(N_CHIPS, r2 // tr),
        in_specs=[pl.BlockSpec((1, 1, tr, cc), lambda k, i, c_ref: (k, c_ref[0], i, 0)),
                  pl.BlockSpec((1, tr, cc), lambda k, i, c_ref: (k, i, 0))],
        out_specs=pl.BlockSpec((1, tr, cc), lambda k, i, c_ref: (k, i, 0)))
    return pl.pallas_call(body, name=name, grid_spec=grid_spec, out_shape=jax.ShapeDtypeStruct((N_CHIPS, r2, cc), BF16),
                          compiler_params=_params("parallel", "parallel"))(c_idx, g4, rx)


def _sum_into(buf, rx, own, layer, qc, out_shape, name):
    _, r2, cc = rx.shape
    tr = _tile(r2, 256, 16)
    nb = r2 // tr

    def body(qc_ref, *refs):
        rx_ref, own_ref, o_ref = refs[-3:]
        q = qc_ref[0]
        acc = None
        for k in range(N_CHIPS):
            v = jnp.where(q == k, own_ref[0], rx_ref[k]).astype(F32)
            acc = v if acc is None else acc + v
        o_ref[0] = acc

    in_specs = [pl.BlockSpec((N_CHIPS, tr, cc), lambda i, qc_ref: (0, i, 0)),
                pl.BlockSpec((1, tr, cc), lambda i, qc_ref: (qc_ref[0], i, 0))]
    args = (rx, own)
    aliases = {}
    if buf is not None:
        in_specs = [_ANY] + in_specs
        args = (buf,) + args
        aliases = {1: 0}
    grid_spec = pltpu.PrefetchScalarGridSpec(
        num_scalar_prefetch=1, grid=(nb,), in_specs=in_specs,
        out_specs=pl.BlockSpec((1, tr, cc), lambda i, qc_ref: (layer, qc_ref[1] * nb + i, 0)))
    return pl.pallas_call(body, name=name, grid_spec=grid_spec, out_shape=jax.ShapeDtypeStruct(out_shape, F32),
                          input_output_aliases=aliases, compiler_params=_params("parallel"))(qc, *args)


def _sum_slots(st, name):
    k, r, cc = st.shape
    tr = _tile(r, 256, 8)

    def body(s_ref, o_ref):
        acc = s_ref[0].astype(F32)
        for j in range(1, k):
            acc = acc + s_ref[j].astype(F32)
        o_ref[...] = acc

    return pl.pallas_call(body, name=name, grid=(r // tr,), in_specs=[pl.BlockSpec((k, tr, cc), lambda i: (0, i, 0))],
                          out_specs=pl.BlockSpec((tr, cc), lambda i: (i, 0)),
                          out_shape=jax.ShapeDtypeStruct((r, cc), F32), compiler_params=_params("parallel"))(st)


def _pack(arrs, rows_mult=2 * SUBLANE):
    flat = jnp.concatenate([a.reshape(-1).astype(F32) for a in arrs])
    quantum = rows_mult * LANE
    padded = -(-flat.shape[0] // quantum) * quantum
    return jnp.pad(flat, (0, padded - flat.shape[0])).reshape(-1, LANE)


def _unpack(buf, shapes):
    flat = buf.reshape(-1)
    out, off = [], 0
    for sh in shapes:
        size = 1
        for d in sh:
            size *= d
        out.append(flat[off:off + size].reshape(sh))
        off += size
    return out


def kernel(x, p, norm_mix_g, norm_ffn_g, norm_ple_g, final_norm_g, a_w_in, a_conv_w, a_conv_b, a_w_gate_r, a_b_gate_r, a_w_gate_i, a_b_gate_i, a_lambda, a_w_out, b_w_in, b_conv_w, b_conv_b, b_dt_bias, b_a_log, b_d_skip, b_norm_g, b_w_out, ffn_w_gate, ffn_w_up, ffn_w_down, ple_w_proj, ple_w_gate, loss_target, m_norm_mix_g, m_norm_ffn_g, m_norm_ple_g, m_final_norm_g, m_a_w_in, m_a_conv_w, m_a_conv_b, m_a_w_gate_r, m_a_b_gate_r, m_a_w_gate_i, m_a_b_gate_i, m_a_lambda, m_a_w_out, m_b_w_in, m_b_conv_w, m_b_conv_b, m_b_dt_bias, m_b_a_log, m_b_d_skip, m_b_norm_g, m_b_w_out, m_ffn_w_gate, m_ffn_w_up, m_ffn_w_down, m_ple_w_proj, m_ple_w_gate, v_norm_mix_g, v_norm_ffn_g, v_norm_ple_g, v_final_norm_g, v_a_w_in, v_a_conv_w, v_a_conv_b, v_a_w_gate_r, v_a_b_gate_r, v_a_w_gate_i, v_a_b_gate_i, v_a_lambda, v_a_w_out, v_b_w_in, v_b_conv_w, v_b_conv_b, v_b_dt_bias, v_b_a_log, v_b_d_skip, v_b_norm_g, v_b_w_out, v_ffn_w_gate, v_ffn_w_up, v_ffn_w_down, v_ple_w_proj, v_ple_w_gate):
    names = ["norm_mix_g", "norm_ffn_g", "norm_ple_g", "final_norm_g", "a_w_in", "a_conv_w", "a_conv_b", "a_w_gate_r",
             "a_b_gate_r", "a_w_gate_i", "a_b_gate_i", "a_lambda", "a_w_out", "b_w_in", "b_conv_w", "b_conv_b",
             "b_dt_bias", "b_a_log", "b_d_skip", "b_norm_g", "b_w_out", "ffn_w_gate", "ffn_w_up", "ffn_w_down",
             "ple_w_proj", "ple_w_gate"]
    w_in = dict(zip(names, [norm_mix_g, norm_ffn_g, norm_ple_g, final_norm_g, a_w_in, a_conv_w, a_conv_b, a_w_gate_r,
                            a_b_gate_r, a_w_gate_i, a_b_gate_i, a_lambda, a_w_out, b_w_in, b_conv_w, b_conv_b,
                            b_dt_bias, b_a_log, b_d_skip, b_norm_g, b_w_out, ffn_w_gate, ffn_w_up, ffn_w_down,
                            ple_w_proj, ple_w_gate]))
    m_in = dict(zip(names, [m_norm_mix_g, m_norm_ffn_g, m_norm_ple_g, m_final_norm_g, m_a_w_in, m_a_conv_w,
                            m_a_conv_b, m_a_w_gate_r, m_a_b_gate_r, m_a_w_gate_i, m_a_b_gate_i, m_a_lambda,
                            m_a_w_out, m_b_w_in, m_b_conv_w, m_b_conv_b, m_b_dt_bias, m_b_a_log, m_b_d_skip,
                            m_b_norm_g, m_b_w_out, m_ffn_w_gate, m_ffn_w_up, m_ffn_w_down, m_ple_w_proj,
                            m_ple_w_gate]))
    v_in = dict(zip(names, [v_norm_mix_g, v_norm_ffn_g, v_norm_ple_g, v_final_norm_g, v_a_w_in, v_a_conv_w,
                            v_a_conv_b, v_a_w_gate_r, v_a_b_gate_r, v_a_w_gate_i, v_a_b_gate_i, v_a_lambda,
                            v_a_w_out, v_b_w_in, v_b_conv_w, v_b_conv_b, v_b_dt_bias, v_b_a_log, v_b_d_skip,
                            v_b_norm_g, v_b_w_out, v_ffn_w_gate, v_ffn_w_up, v_ffn_w_down, v_ple_w_proj,
                            v_ple_w_gate]))

    s, d = x.shape[1], x.shape[2]
    depth = norm_mix_g.shape[0]
    assert depth == 2
    q_idx = 2 * lax.axis_index("x") + lax.axis_index("y")
    c_idx = lax.axis_index("c").astype(jnp.int32).reshape(1)

    inner = b_w_out.shape[1] * N_CHIPS
    n_heads = inner // SSD_HEAD_DIM
    n_groups = n_heads // SSD_HEADS_PER_GROUP
    gn = n_groups * SSD_STATE
    xbcw = inner + 2 * gn
    assert b_conv_w.shape[2] * N_CHIPS == xbcw and n_heads <= LANE

    big = [("a_w_in", "col"), ("a_w_gate_r", "gate"), ("a_w_gate_i", "gate"), ("a_w_out", "row"),
           ("ffn_w_gate", "col"), ("ffn_w_up", "col"), ("ffn_w_down", "row"), ("ple_w_proj", "col"),
           ("ple_w_gate", "row"), ("b_w_in", "col"), ("b_w_out", "row")]
    kind_of = dict(big)

    def shard2d(name, arr):
        if kind_of[name] == "gate":
            return [arr[l].reshape(-1, arr.shape[-1]) for l in range(arr.shape[0])]
        return [arr[l] for l in range(arr.shape[0])]

    small_sharded = ["a_conv_w", "a_b_gate_r", "a_b_gate_i", "b_conv_w", "b_conv_b", "b_norm_g"]
    small_pack = _pack([w_in[nm] for nm in small_sharded], rows_mult=16)

    q_vec = q_idx.astype(jnp.int32).reshape(1)
    qc_vec = jnp.stack([q_idx, lax.axis_index("c")]).astype(jnp.int32)
    placed, entry_keys = {}, []
    for nm, _ in big:
        for l, sh in enumerate(shard2d(nm, w_in[nm])):
            placed[(nm, l)] = _place_shard(sh, q_vec, BF16, f"place_{nm}{l}")
            entry_keys.append((nm, l))
    placed[("small", 0)] = _place_shard(small_pack, q_vec, F32, "place_small")
    gather_groups = [
        [("a_w_in", 0), ("a_w_gate_r", 0), ("a_w_gate_i", 0), ("a_w_out", 0), ("small", 0)],
        [("ffn_w_gate", 0), ("ffn_w_up", 0), ("ffn_w_down", 0), ("ple_w_proj", 0), ("ple_w_gate", 0)],
        [("b_w_in", 0), ("b_w_out", 0)],
        [("ffn_w_gate", 1), ("ffn_w_up", 1), ("ffn_w_down", 1), ("ple_w_proj", 1), ("ple_w_gate", 1)],
    ]
    gather_started = [_gather_start([placed[k] for k in grp], f"gather_start{gi}")
                      for gi, grp in enumerate(gather_groups)]
    all_started = sum(st[3][0:1, 0:1] for st in gather_started)
    wst = {}

    def gather_finish(gi, after):
        send, recv, thru, _ = gather_started[gi]
        landed = _gather_wait(send, recv, thru, after, f"gather_wait{gi}")
        for k, arr in zip(gather_groups[gi], _forward_sibling(landed, f"gather_forward{gi}")):
            wst[k] = arr

    gather_finish(0, all_started)
    small_st = wst[("small", 0)]
    kept_stacked = ("a_w_in", "ffn_w_gate", "ffn_w_up", "ple_w_proj")

    def whole(nm, l):
        st = wst[(nm, l)]
        kind = kind_of[nm]
        if nm in kept_stacked:
            return st
        if kind == "row":
            return st.reshape(-1, st.shape[-1])
        if kind == "col":
            return jnp.concatenate([st[k] for k in range(N_CHIPS)], axis=1)
        heads = w_in[nm].shape[1]
        return st.reshape(N_CHIPS, heads, -1, st.shape[-1]).transpose(1, 0, 2, 3).reshape(heads, -1, st.shape[-1])

    small_parts = [_unpack(small_st[k], [w_in[nm].shape for nm in small_sharded]) for k in range(N_CHIPS)]
    small_full = {nm: jnp.concatenate([small_parts[k][i] for k in range(N_CHIPS)], axis=-1)
                  for i, nm in enumerate(small_sharded)}
    a_cw = small_full["a_conv_w"][0]
    a_br = small_full["a_b_gate_r"][0].reshape(1, -1)
    a_bi = small_full["a_b_gate_i"][0].reshape(1, -1)
    b_cw = small_full["b_conv_w"][0]
    b_cb = small_full["b_conv_b"]
    b_ng = small_full["b_norm_g"]

    def pad_lanes(v):
        return jnp.pad(v, ((0, 0), (0, LANE - v.shape[1])))

    dt_bias = pad_lanes(b_dt_bias)
    a_log = pad_lanes(b_a_log.reshape(n_groups, SSD_HEADS_PER_GROUP)).reshape(1, n_groups * LANE)
    dskip_e = jnp.repeat(b_d_skip, SSD_HEAD_DIM, axis=1)

    w_a_in = whole("a_w_in", 0)
    w_ax, w_ay = w_a_in[:N_CHIPS // 2], w_a_in[N_CHIPS // 2:]
    w_ar, w_ai, w_ao = whole("a_w_gate_r", 0), whole("a_w_gate_i", 0), whole("a_w_out", 0)
    w_fg, w_fu, w_fd, w_pp, w_pg = ([None] * depth for _ in range(5))

    def take_ffn_ple(l):
        w_fg[l], w_fu[l], w_fd[l] = whole("ffn_w_gate", l), whole("ffn_w_up", l), whole("ffn_w_down", l)
        w_pp[l], w_pg[l] = whole("ple_w_proj", l), whole("ple_w_gate", l)

    grads = {}

    h0 = x[0]
    g_mix = [norm_mix_g[l:l + 1] for l in range(depth)]
    g_ffn = [norm_ffn_g[l:l + 1] for l in range(depth)]
    g_ple = [norm_ple_g[l:l + 1] for l in range(depth)]
    g_fin = final_norm_g.reshape(1, -1)

    u0 = _rmsnorm_fwd(h0, g_mix[0], "norm_mix0")
    xr_pre = _mm(u0, w_ax, stacked_b=True, name="lru_in_x")
    yg = _mm(u0, w_ay, stacked_b=True, name="lru_in_y")
    xr = _conv_fwd(xr_pre, a_cw, a_conv_b, False, "lru_conv")
    lru_a, lru_b = _lru_gates_fwd(xr, w_ar, w_ai, a_br, a_bi, a_lambda)
    hs = _scan(lru_a, lru_b, False, "lru_scan")
    y_lru = _lru_out_fwd(hs, yg)
    h_mix = [_mm(y_lru, w_ao, add=h0, name="lru_out"), None]

    def ffn_ple_fwd(h_in, l):
        n_f = _rmsnorm_fwd(h_in, g_ffn[l], f"norm_ffn{l}")
        gt = _mm(n_f, w_fg[l], stacked_b=True, name=f"ffn_gate{l}")
        up = _mm(n_f, w_fu[l], stacked_b=True, name=f"ffn_up{l}")
        act = _swiglu_act(gt, up)
        h_f = _mm(act, w_fd[l], add=h_in, name=f"ffn_down{l}")
        n_p = _rmsnorm_fwd(h_f, g_ple[l], f"norm_ple{l}")
        gp = _mm(n_p, w_pg[l], name=f"ple_gate{l}")
        pp = _mm(p[l, 0], w_pp[l], stacked_b=True, name=f"ple_proj{l}")
        h_out = _ple_fwd(h_f, gp, pp)
        return h_out, dict(h_in=h_in, n_f=n_f, gt=gt, up=up, act=act, h_f=h_f, n_p=n_p, gp=gp, pp=pp)

    gather_finish(1, h_mix[0])
    take_ffn_ple(0)
    h_l0, sv0 = ffn_ple_fwd(h_mix[0], 0)

    gather_finish(2, h_l0)
    w_b_in = whole("b_w_in", 0)
    w_bz, w_bx = w_b_in[:, :inner], w_b_in[:, inner:inner + xbcw]
    w_bd = pad_lanes(w_b_in[:, inner + xbcw:])
    w_bo = whole("b_w_out", 0)
    u1 = _rmsnorm_fwd(h_l0, g_mix[1], "norm_mix1")
    z = _mm(u1, w_bz, name="ssd_in_z")
    xbc_pre = _mm(u1, w_bx, name="ssd_in_xbc")
    dt_pre = _mm(u1, w_bd, name="ssd_in_dt")
    xbc = _conv_fwd(xbc_pre, b_cw, b_cb, True, "ssd_conv")
    dt = _dt_fwd(dt_pre, dt_bias, n_heads, n_groups)
    ysc, s_in = _ssd_fwd(xbc, dt, a_log, inner, n_groups)
    yn = _ssd_gate_norm_fwd(ysc, xbc, z, dskip_e, b_ng, n_groups)
    h_mix[1] = _mm(yn, w_bo, add=h_l0, name="ssd_out")
    gather_finish(3, h_mix[1])
    take_ffn_ple(1)
    h_l1, sv1 = ffn_ple_fwd(h_mix[1], 1)

    dh, dg_fin, loss_row = _final_loss_bwd(h_l1, g_fin, loss_target[0])

    d_norm_ffn, d_norm_ple, d_norm_mix = [None] * depth, [None] * depth, [None] * depth
    for nm in ("ffn_w_gate", "ffn_w_up", "ffn_w_down", "ple_w_proj", "ple_w_gate"):
        grads[nm] = [None] * depth

    def stacked(nm, gfull):
        kind = kind_of[nm]
        if nm in kept_stacked:
            return gfull
        if kind == "row":
            return gfull.reshape(N_CHIPS, -1, gfull.shape[-1])
        if kind == "col":
            n_loc = gfull.shape[1] // N_CHIPS
            return jnp.stack([gfull[:, k * n_loc:(k + 1) * n_loc] for k in range(N_CHIPS)])
        heads, bw, _ = gfull.shape
        return gfull.reshape(heads, N_CHIPS, bw // N_CHIPS, bw).transpose(1, 0, 2, 3).reshape(N_CHIPS, -1, bw)

    reduce_started = []

    def reduce_start(keys, tag):
        gst = [stacked(nm, grads[nm][l]) for nm, l in keys]
        from_sib = _swap_halves(gst, f"reduce_swap_{tag}")
        chip_sum = [_add_own_half(g, r, c_idx, f"reduce_add_{nm}{l}") for g, r, (nm, l) in zip(gst, from_sib, keys)]
        send, recv, ents, lands, token = _scatter_start(chip_sum, f"reduce_scatter_start_{tag}")
        reduce_started.append((keys, tag, send, recv, ents, lands))
        return token[0:1, 0:1]

    def ffn_ple_keys(l):
        return [("ple_w_gate", l), ("ple_w_proj", l), ("ffn_w_down", l), ("ffn_w_gate", l), ("ffn_w_up", l)]

    def ffn_ple_bwd(dh_out, sv, l, g_ple_l):
        dgp, dpp = _ple_bwd(dh_out, sv["gp"], sv["pp"])
        grads["ple_w_gate"][l] = _mm(sv["n_p"], dgp, ta=True, out_dtype=BF16, name=f"ple_gate_dw{l}")
        grads["ple_w_proj"][l] = _mm(p[l, 0], dpp, ta=True, out_dtype=BF16, stacked_out=N_CHIPS,
                                     name=f"ple_proj_dw{l}")
        dn = _mm(dgp, w_pg[l], tb=True, name=f"ple_gate_dx{l}")
        dh_f, d_norm_ple[l] = _rmsnorm_bwd(dn, sv["h_f"], g_ple_l, dh_out, f"norm_ple_bwd{l}")
        grads["ffn_w_down"][l] = _mm(sv["act"], dh_f, ta=True, out_dtype=BF16, name=f"ffn_down_dw{l}")
        dact = _mm(dh_f, w_fd[l], tb=True, name=f"ffn_down_dx{l}")
        dgt, dup = _swiglu_bwd(dact, sv["gt"], sv["up"])
        grads["ffn_w_gate"][l] = _mm(sv["n_f"], dgt, ta=True, out_dtype=BF16, stacked_out=N_CHIPS,
                                     name=f"ffn_gate_dw{l}")
        grads["ffn_w_up"][l] = _mm(sv["n_f"], dup, ta=True, out_dtype=BF16, stacked_out=N_CHIPS, name=f"ffn_up_dw{l}")
        dn = _mm(dgt, w_fg[l], tb=True, stacked_b=True, name=f"ffn_gate_dx{l}")
        dn = _mm(dup, w_fu[l], tb=True, stacked_b=True, add=dn, name=f"ffn_up_dx{l}")
        dh_in, d_norm_ffn[l] = _rmsnorm_bwd(dn, sv["h_in"], g_ffn[l], dh_f, f"norm_ffn_bwd{l}")
        return dh_in

    dh = ffn_ple_bwd(dh, sv1, 1, g_ple[1])
    tok = reduce_start(ffn_ple_keys(1), "l1")

    grads["b_w_out"] = [_mm(yn, dh, ta=True, out_dtype=BF16, name="ssd_out_dw")]
    dyn = _mm(dh, w_bo, tb=True, name="ssd_out_dx")
    dy_ssd, dz, d_b_norm_g, dd_lane = _ssd_gate_norm_bwd(dyn, ysc, xbc, z, dskip_e, b_ng + tok, n_groups)
    dxs, d_bm, d_cm, ddt, d_a_log = _ssd_bwd(xbc, dt, a_log, dy_ssd, ysc, s_in, dskip_e, inner, n_groups)
    dxbc = _put_cols(dxs, d_bm, inner // gn, "ssd_put_db")
    dxbc = _put_cols(dxbc, d_cm, inner // gn + 1, "ssd_put_dc")
    dconv = _silu_conv_bwd_pre(dxbc, xbc_pre, b_cw, b_cb, "ssd_conv_bwd_pre")
    dxbc_pre, d_b_conv_w, d_b_conv_b = _conv_bwd(dconv, xbc_pre, b_cw, "ssd_conv_bwd")
    ddt_pre, d_dt_bias = _dt_bwd(ddt, dt_pre, dt_bias, n_heads, n_groups)
    d_a_log = d_a_log.reshape(n_groups, LANE)[:, :SSD_HEADS_PER_GROUP].reshape(1, n_heads)
    gw_bz = _mm(u1, dz, ta=True, out_dtype=BF16, name="ssd_in_z_dw")
    gw_bx = _mm(u1, dxbc_pre, ta=True, out_dtype=BF16, name="ssd_in_xbc_dw")
    gw_bd = _mm(u1, ddt_pre, ta=True, out_dtype=BF16, name="ssd_in_dt_dw")
    grads["b_w_in"] = [jnp.concatenate([gw_bz, gw_bx, gw_bd[:, :n_heads]], axis=1)]
    du = _mm(dz, w_bz, tb=True, name="ssd_in_z_dx")
    du = _mm(dxbc_pre, w_bx, tb=True, add=du, name="ssd_in_xbc_dx")
    du = _mm(ddt_pre, w_bd, tb=True, add=du, name="ssd_in_dt_dx")
    dh, d_norm_mix[1] = _rmsnorm_bwd(du, h_l0, g_mix[1], dh, "norm_mix_bwd1")
    tok = reduce_start([("b_w_out", 0), ("b_w_in", 0)], "ssd")

    dh = ffn_ple_bwd(dh, sv0, 0, g_ple[0] + tok)
    tok = reduce_start(ffn_ple_keys(0), "l0")

    grads["a_w_out"] = [_mm(y_lru, dh, ta=True, out_dtype=BF16, name="lru_out_dw")]
    dy_lru = _mm(dh, w_ao, tb=True, name="lru_out_dx")
    dhs, dyg = _lru_out_bwd(dy_lru, hs, yg)
    g_scan = _scan(lru_a, dhs, True, "lru_scan_bwd")
    dxr, d_wr, d_wi, d_br, d_bi, d_lam = _lru_gates_bwd(xr, g_scan, hs, w_ar, w_ai, a_br, a_bi, a_lambda + tok)
    dxr_pre, d_a_conv_w, d_a_conv_b = _conv_bwd(dxr, xr_pre, a_cw, "lru_conv_bwd")
    gw_ax = _mm(u0, dxr_pre, ta=True, out_dtype=BF16, stacked_out=N_CHIPS // 2, name="lru_in_x_dw")
    gw_ay = _mm(u0, dyg, ta=True, out_dtype=BF16, stacked_out=N_CHIPS // 2, name="lru_in_y_dw")
    grads["a_w_in"] = [jnp.concatenate([gw_ax, gw_ay], axis=0)]
    grads["a_w_gate_r"] = [d_wr.astype(BF16)]
    grads["a_w_gate_i"] = [d_wi.astype(BF16)]
    du = _mm(dxr_pre, w_ax, tb=True, stacked_b=True, name="lru_in_x_dx")
    du = _mm(dyg, w_ay, tb=True, stacked_b=True, add=du, name="lru_in_y_dx")
    grad_x, d_norm_mix[0] = _rmsnorm_bwd(du, h0, g_mix[0], dh, "norm_mix_bwd0")

    reduce_start([("a_w_out", 0), ("a_w_in", 0), ("a_w_gate_r", 0), ("a_w_gate_i", 0)], "lru")
    grad_out, delta_out, m_out, v_out = {}, {}, {}, {}

    def reduce_finish(groups, after, tag):
        g_half = {}
        for keys, gtag, send, recv, ents, lands in groups:
            ents, lands = _scatter_wait(send, recv, ents, lands, after, f"reduce_scatter_wait_{gtag}")
            for rx, own, (nm, l) in zip(lands, ents, keys):
                sh = shard2d(nm, w_in[nm])
                g_half[nm] = _sum_into(g_half.get(nm), rx, own, l, qc_vec, (len(sh),) + sh[0].shape,
                                       f"reduce_sum_{nm}{l}")
        nms = list(g_half)
        dl = None
        for nm, gfull in zip(nms, _join_halves([g_half[nm] for nm in nms], f"reduce_join_{tag}")):
            shape, cols = w_in[nm].shape, gfull.shape[-1]
            dl, mn, vn, gout = _adamw(w_in[nm].reshape(-1, cols), gfull.reshape(-1, cols),
                                      m_in[nm].reshape(-1, cols), v_in[nm].reshape(-1, cols), f"adamw_{nm}",
                                      emit_g=True)
            grad_out[nm], delta_out[nm] = gout.reshape(shape), dl.reshape(shape)
            m_out[nm], v_out[nm] = mn.reshape(shape), vn.reshape(shape)
        return dl

    done = reduce_finish(reduce_started[:-1], grad_x, "rest")
    reduce_finish(reduce_started[-1:], done, "lru")

    small_full_grads = {
        "norm_mix_g": jnp.concatenate(d_norm_mix, axis=0), "norm_ffn_g": jnp.concatenate(d_norm_ffn, axis=0),
        "norm_ple_g": jnp.concatenate(d_norm_ple, axis=0), "final_norm_g": dg_fin[0],
        "a_conv_w": d_a_conv_w[None], "a_conv_b": d_a_conv_b,
        "a_b_gate_r": d_br.reshape(a_b_gate_r.shape[0], a_b_gate_r.shape[1], -1),
        "a_b_gate_i": d_bi.reshape(a_b_gate_i.shape[0], a_b_gate_i.shape[1], -1),
        "a_lambda": d_lam, "b_conv_w": d_b_conv_w[None], "b_conv_b": d_b_conv_b,
        "b_dt_bias": d_dt_bias[:, :n_heads], "b_a_log": d_a_log,
        "b_d_skip": dd_lane.reshape(1, n_heads, SSD_HEAD_DIM).sum(axis=-1), "b_norm_g": d_b_norm_g,
    }
    small_names = list(small_full_grads)
    small_shapes = [small_full_grads[nm].shape for nm in small_names]
    packed = _pack([loss_row] + [small_full_grads[nm] for nm in small_names])
    total = _sum_slots(_gather_all(packed, "gather_small_grads"), "sum_small_grads")
    parts = _unpack(total, [(1, LANE)] + small_shapes)
    loss = parts[0][0, 0]
    g_small = {}
    for nm, gfull in zip(small_names, parts[1:]):
        if nm in small_sharded:
            n_loc = w_in[nm].shape[-1]
            gfull = lax.dynamic_slice_in_dim(gfull, q_idx * n_loc, n_loc, axis=gfull.ndim - 1)
        g_small[nm] = gfull

    sm_shapes = [w_in[nm].shape for nm in small_names]
    dl, mn, vn = _adamw(_pack([w_in[nm] for nm in small_names]), _pack([g_small[nm] for nm in small_names]),
                        _pack([m_in[nm] for nm in small_names]), _pack([v_in[nm] for nm in small_names]),
                        "adamw_small")
    for nm, a, b_, c_ in zip(small_names, _unpack(dl, sm_shapes), _unpack(mn, sm_shapes), _unpack(vn, sm_shapes)):
        grad_out[nm] = g_small[nm].reshape(w_in[nm].shape)
        delta_out[nm], m_out[nm], v_out[nm] = a, b_, c_

    return (loss, grad_x[None], *[grad_out[nm] for nm in names], *[delta_out[nm] for nm in names],
            *[m_out[nm] for nm in names], *[v_out[nm] for nm in names])
```

```python
import functools

import jax
import jax.numpy as jnp
from jax import lax
from jax.experimental import pallas as pl
from jax.experimental.pallas import tpu as pltpu

F32 = jnp.float32
BF16 = jnp.bfloat16
MESH = pl.DeviceIdType.MESH
HIGHEST = lax.Precision.HIGHEST

NORM_EPS = 1e-6
LRU_C = 8.0
CONV_WIDTH = 4
SSD_HEAD_DIM = 64
SSD_STATE = 128
SSD_CHUNK = 128
SSD_HEADS_PER_GROUP = 8
LANE = 128
SUBLANE = 8
N_CHIPS = 4
N_DEV = 8
VMEM_LIMIT = 48 * 1024 * 1024

ADAM_LR = 0.001
ADAM_B1 = 0.9
ADAM_B2 = 0.999
ADAM_EPS = 1e-08
ADAM_WD = 0.01
ADAM_STEP = 10


def _tile(n, cap, mult=LANE):
    best = None
    for t in range(mult, min(n, cap) + 1, mult):
        if n % t == 0:
            best = t
    return best if best is not None else n


def _params(*sem):
    return pltpu.CompilerParams(dimension_semantics=sem, vmem_limit_bytes=VMEM_LIMIT)


def _mm(a, b, *, ta=False, tb=False, add=None, out_dtype=F32, stacked_b=False, stacked_out=0, name):
    if ta:
        kd, m = a.shape
    else:
        m, kd = a.shape
    n_loc = None
    if stacked_b:
        slots, kb, n_loc = b.shape
        if tb:
            n, kb = kb, slots * n_loc
        else:
            n = slots * n_loc
    elif tb:
        n, kb = b.shape
    else:
        kb, n = b.shape
    assert kd == kb, (a.shape, b.shape, ta, tb)
    tm = _tile(m, 1408)
    tn = _tile(n, 1408)
    tk = _tile(kd, 2048)
    if stacked_b and tb:
        tk = _tile(n_loc, 1408)
    elif stacked_b:
        tn = _tile(n_loc, 1408)
    if stacked_out:
        n_loc = n // stacked_out
        tn = _tile(n_loc, 1408)
    nk = kd // tk
    dims = (((0 if ta else 1,), (1 if tb else 0,)), ((), ()))

    def body(*refs):
        a_ref, b_ref = refs[:2]
        add_ref = refs[2] if add is not None else None
        o_ref = refs[3] if add is not None else refs[2]
        bv = b_ref[0] if stacked_b else b_ref[...]
        part = lax.dot_general(a_ref[...].astype(BF16), bv.astype(BF16), dims, preferred_element_type=F32)

        def finish(r):
            if add is not None:
                r = r + add_ref[...]
            if stacked_out:
                o_ref[0] = r.astype(out_dtype)
            else:
                o_ref[...] = r.astype(out_dtype)

        if nk == 1:
            finish(part)
        else:
            acc_ref = refs[-1]
            k = pl.program_id(2)

            @pl.when(k == 0)
            def _():
                acc_ref[...] = part

            @pl.when((k > 0) & (k < nk - 1))
            def _():
                acc_ref[...] += part

            @pl.when(k == nk - 1)
            def _():
                finish(acc_ref[...] + part)

    a_spec = pl.BlockSpec((tk, tm), lambda i, j, k: (k, i)) if ta else pl.BlockSpec((tm, tk), lambda i, j, k: (i, k))
    if stacked_b and tb:
        per = n_loc // tk
        b_spec = pl.BlockSpec((1, tn, tk), lambda i, j, k: (k // per, j, k % per))
    elif stacked_b:
        per = n_loc // tn
        b_spec = pl.BlockSpec((1, tk, tn), lambda i, j, k: (j // per, k, j % per))
    elif tb:
        b_spec = pl.BlockSpec((tn, tk), lambda i, j, k: (j, k))
    else:
        b_spec = pl.BlockSpec((tk, tn), lambda i, j, k: (k, j))
    o_spec = pl.BlockSpec((tm, tn), lambda i, j, k: (i, j))
    in_specs = [a_spec, b_spec] + ([o_spec] if add is not None else [])
    args = (a, b) + ((add,) if add is not None else ())
    if stacked_out:
        per_o = n_loc // tn
        out_spec = pl.BlockSpec((1, tm, tn), lambda i, j, k: (j // per_o, i, j % per_o))
        out_shape = jax.ShapeDtypeStruct((stacked_out, m, n_loc), out_dtype)
    else:
        out_spec, out_shape = o_spec, jax.ShapeDtypeStruct((m, n), out_dtype)
    return pl.pallas_call(
        body, name=name, grid=(m // tm, n // tn, nk), in_specs=in_specs, out_specs=out_spec, out_shape=out_shape,
        scratch_shapes=[pltpu.VMEM((tm, tn), F32)] if nk > 1 else [],
        compiler_params=_params("parallel", "parallel", "arbitrary"))(*args)


def _rowwise(name, body, ins, outs, nrows, ts, ncol=1):
    ts = min(ts, nrows)
    nrow = nrows // ts
    hb = ts // SUBLANE
    nb8 = nrows // SUBLANE
    in_specs, args = [], []
    for kind, arr, cb in ins:
        if kind == "row":
            spec = pl.BlockSpec((ts, cb), lambda j, i: (i, j))
        elif kind == "prev":
            spec = pl.BlockSpec((SUBLANE, cb), lambda j, i: (jnp.maximum(i * hb - 1, 0), j))
        elif kind == "next":
            spec = pl.BlockSpec((SUBLANE, cb), lambda j, i: (jnp.minimum((i + 1) * hb, nb8 - 1), j))
        else:
            spec = pl.BlockSpec((arr.shape[0], cb), lambda j, i: (0, j))
        in_specs.append(spec)
        args.append(arr)
    out_specs, out_shape = [], []
    for kind, rows, ctot, cb, dt in outs:
        if kind == "row":
            out_shape.append(jax.ShapeDtypeStruct((nrows, ctot), dt))
            out_specs.append(pl.BlockSpec((ts, cb), lambda j, i: (i, j)))
        else:
            out_shape.append(jax.ShapeDtypeStruct((rows, ctot), dt))
            out_specs.append(pl.BlockSpec((rows, cb), lambda j, i: (0, j)))

    def kern(*refs):
        body(pl.program_id(1), nrow, *refs)

    return pl.pallas_call(kern, name=name, grid=(ncol, nrow), in_specs=in_specs, out_specs=out_specs,
                          out_shape=out_shape, compiler_params=_params("parallel", "arbitrary"))(*args)


def _colsum(x):
    return jnp.sum(x, axis=0, keepdims=True)


def _acc(i, ref, val):
    @pl.when(i == 0)
    def _():
        ref[...] = val

    @pl.when(i > 0)
    def _():
        ref[...] += val


def _shift_down(x, halo, k):
    xx = jnp.concatenate([halo, x], axis=0)
    return pltpu.roll(xx, k, axis=0)[SUBLANE:, :]


def _shift_up(x, halo, k):
    xx = jnp.concatenate([x, halo], axis=0)
    n = xx.shape[0]
    return pltpu.roll(xx, n - k, axis=0)[: x.shape[0], :]


def _sigmoid(x):
    return 1.0 / (1.0 + jnp.exp(-x))


def _silu(x):
    return x * _sigmoid(x)


def _dsilu(x):
    s = _sigmoid(x)
    return s * (1.0 + x * (1.0 - s))


_GELU_K = 0.7978845608028654
_GELU_C = 0.044715


def _gelu(x):
    return 0.5 * x * (1.0 + jnp.tanh(_GELU_K * (x + _GELU_C * x * x * x)))


def _dgelu(x):
    t = jnp.tanh(_GELU_K * (x + _GELU_C * x * x * x))
    return 0.5 * (1.0 + t) + 0.5 * x * (1.0 - t * t) * _GELU_K * (1.0 + 3.0 * _GELU_C * x * x)


def _softplus(x):
    return jnp.maximum(x, 0.0) + jnp.log1p(jnp.exp(-jnp.abs(x)))


def _neg_expm1(x):
    poly = -x * (1.0 + x * (0.5 + x * (1.0 / 6.0 + x * (1.0 / 24.0 + x * (1.0 / 120.0)))))
    return jnp.where(x > -0.05, poly, 1.0 - jnp.exp(x))


def _rmsnorm_fwd(h, g, name):
    s, d = h.shape

    def body(i, n, h_ref, g_ref, o_ref):
        x = h_ref[...]
        r = lax.rsqrt(jnp.mean(x * x, axis=-1, keepdims=True) + NORM_EPS)
        o_ref[...] = (x * r * g_ref[...]).astype(BF16)

    return _rowwise(name, body, [("row", h, d), ("vec", g, d)], [("row", None, d, d, BF16)], s, 256)[0]


def _rmsnorm_bwd(dn, h, g, dres, name):
    s, d = h.shape

    def body(i, n, dn_ref, h_ref, g_ref, dres_ref, dh_ref, dg_ref):
        x = h_ref[...]
        dy = dn_ref[...].astype(F32)
        r = lax.rsqrt(jnp.mean(x * x, axis=-1, keepdims=True) + NORM_EPS)
        xh = x * r
        _acc(i, dg_ref, _colsum(dy * xh))
        dxh = dy * g_ref[...]
        dh_ref[...] = dres_ref[...] + r * (dxh - xh * jnp.mean(dxh * xh, axis=-1, keepdims=True))

    return _rowwise(name, body, [("row", dn, d), ("row", h, d), ("vec", g, d), ("row", dres, d)],
                    [("row", None, d, d, F32), ("acc", 1, d, d, F32)], s, 256)


def _final_loss_bwd(h, g, tgt):
    s, d = h.shape

    def body(i, n, h_ref, g_ref, t_ref, dh_ref, dg_ref, loss_ref):
        x = h_ref[...]
        gg = g_ref[...]
        r = lax.rsqrt(jnp.mean(x * x, axis=-1, keepdims=True) + NORM_EPS)
        xh = x * r
        err = xh * gg - t_ref[...]
        part = 0.5 * jnp.sum(jnp.mean(err * err, axis=-1, keepdims=True), axis=0, keepdims=True)
        _acc(i, loss_ref, jnp.broadcast_to(part, (1, LANE)))
        dy = err * (1.0 / d)
        _acc(i, dg_ref, _colsum(dy * xh))
        dxh = dy * gg
        dh_ref[...] = r * (dxh - xh * jnp.mean(dxh * xh, axis=-1, keepdims=True))

    return _rowwise("final_loss_bwd", body, [("row", h, d), ("vec", g, d), ("row", tgt, d)],
                    [("row", None, d, d, F32), ("acc", 1, d, d, F32), ("acc", 1, LANE, LANE, F32)], s, 256)


def _conv_rows(x, halo, w, b):
    y = b + w[3:4, :] * x
    for k in range(CONV_WIDTH - 1):
        y = y + w[k:k + 1, :] * _shift_down(x, halo, CONV_WIDTH - 1 - k)
    return y


def _conv_fwd(x, w, b, silu, name):
    s, c = x.shape
    cb = _tile(c, 512)

    def body(i, n, x_ref, p_ref, w_ref, b_ref, o_ref):
        halo = jnp.where(i == 0, 0.0, p_ref[...])
        y = _conv_rows(x_ref[...], halo, w_ref[...], b_ref[...])
        o_ref[...] = _silu(y) if silu else y

    return _rowwise(name, body, [("row", x, cb), ("prev", x, cb), ("vec", w, cb), ("vec", b, cb)],
                    [("row", None, c, cb, F32)], s, 512, ncol=c // cb)[0]


def _silu_conv_bwd_pre(dy, x, w, b, name):
    s, c = x.shape
    cb = _tile(c, 512)

    def body(i, n, dy_ref, x_ref, p_ref, w_ref, b_ref, o_ref):
        halo = jnp.where(i == 0, 0.0, p_ref[...])
        y = _conv_rows(x_ref[...], halo, w_ref[...], b_ref[...])
        o_ref[...] = dy_ref[...] * _dsilu(y)

    return _rowwise(name, body, [("row", dy, cb), ("row", x, cb), ("prev", x, cb), ("vec", w, cb), ("vec", b, cb)],
                    [("row", None, c, cb, F32)], s, 512, ncol=c // cb)[0]


def _conv_bwd(dy, x, w, name):
    s, c = x.shape
    cb = _tile(c, 512)

    def body(i, n, dy_ref, nx_ref, x_ref, p_ref, w_ref, dx_ref, dw_ref, db_ref):
        d = dy_ref[...]
        xx = x_ref[...]
        wv = w_ref[...]
        nxt = jnp.where(i == n - 1, 0.0, nx_ref[...])
        prv = jnp.where(i == 0, 0.0, p_ref[...])
        dx = wv[3:4, :] * d
        parts = []
        for k in range(CONV_WIDTH - 1):
            sh = CONV_WIDTH - 1 - k
            dx = dx + wv[k:k + 1, :] * _shift_up(d, nxt, sh)
            parts.append(_colsum(d * _shift_down(xx, prv, sh)))
        parts.append(_colsum(d * xx))
        dx_ref[...] = dx.astype(BF16)
        _acc(i, dw_ref, jnp.concatenate(parts, axis=0))
        _acc(i, db_ref, _colsum(d))

    return _rowwise(name, body, [("row", dy, cb), ("next", dy, cb), ("row", x, cb), ("prev", x, cb), ("vec", w, cb)],
                    [("row", None, c, cb, BF16), ("acc", CONV_WIDTH, c, cb, F32), ("acc", 1, c, cb, F32)],
                    s, 512, ncol=c // cb)


def _lru_gate_math(xr, r_pre, i_pre, lam):
    r = _sigmoid(r_pre)
    ig = _sigmoid(i_pre)
    sp = _softplus(-lam)
    log_a = -LRU_C * r * sp
    a = jnp.exp(log_a)
    mult = jnp.sqrt(_neg_expm1(2.0 * log_a))
    return r, ig, sp, a, mult


def _lru_gates_fwd(xr, wr, wi, br, bi, lam):
    s, d = xr.shape
    nh, bw, _ = wr.shape
    ts = min(512, s)

    def body(x_ref, wr_ref, wi_ref, br_ref, bi_ref, lam_ref, a_ref, b_ref):
        x = x_ref[...]
        xb = x.astype(BF16)
        r_pre = jnp.dot(xb, wr_ref[0], preferred_element_type=F32) + br_ref[...]
        i_pre = jnp.dot(xb, wi_ref[0], preferred_element_type=F32) + bi_ref[...]
        _, ig, _, a, mult = _lru_gate_math(x, r_pre, i_pre, lam_ref[...])
        a_ref[...] = a
        b_ref[...] = mult * (ig * x)

    row = pl.BlockSpec((ts, bw), lambda h, i: (i, h))
    wsp = pl.BlockSpec((1, bw, bw), lambda h, i: (h, 0, 0))
    vec = pl.BlockSpec((1, bw), lambda h, i: (0, h))
    return pl.pallas_call(
        body, name="lru_gates_fwd", grid=(nh, s // ts), in_specs=[row, wsp, wsp, vec, vec, vec], out_specs=[row, row],
        out_shape=[jax.ShapeDtypeStruct((s, d), F32)] * 2, compiler_params=_params("parallel", "arbitrary"),
    )(xr, wr, wi, br, bi, lam)


def _lru_gates_bwd(xr, g, hs, wr, wi, br, bi, lam):
    s, d = xr.shape
    nh, bw, _ = wr.shape
    ts = min(512, s)
    hb = ts // SUBLANE
    tn_dims = (((0,), (0,)), ((), ()))
    nt_dims = (((1,), (1,)), ((), ()))

    def body(x_ref, g_ref, hs_ref, hp_ref, wr_ref, wi_ref, br_ref, bi_ref, lam_ref,
             dx_ref, dwr_ref, dwi_ref, dbr_ref, dbi_ref, dlam_ref):
        i = pl.program_id(1)
        x = x_ref[...]
        xb = x.astype(BF16)
        gg = g_ref[...]
        lam_v = lam_ref[...]
        r_pre = jnp.dot(xb, wr_ref[0], preferred_element_type=F32) + br_ref[...]
        i_pre = jnp.dot(xb, wi_ref[0], preferred_element_type=F32) + bi_ref[...]
        r, ig, sp, a, mult = _lru_gate_math(x, r_pre, i_pre, lam_v)
        h_prev = _shift_down(hs_ref[...], jnp.where(i == 0, 0.0, hp_ref[...]), 1)
        da = gg * h_prev
        dmult = gg * ig * x
        dlog_a = da * a - dmult * (a * a) / mult
        d_r = dlog_a * (-LRU_C * sp)
        dr_pre = d_r * r * (1.0 - r)
        di_pre = (gg * mult * x) * ig * (1.0 - ig)
        drb = dr_pre.astype(BF16)
        dib = di_pre.astype(BF16)
        dx_ref[...] = (gg * mult * ig
                       + lax.dot_general(drb, wr_ref[0], nt_dims, preferred_element_type=F32)
                       + lax.dot_general(dib, wi_ref[0], nt_dims, preferred_element_type=F32))
        dwr = lax.dot_general(xb, drb, tn_dims, preferred_element_type=F32)[None]
        dwi = lax.dot_general(xb, dib, tn_dims, preferred_element_type=F32)[None]
        dlam = _colsum(dlog_a * (-LRU_C * r)) * (-_sigmoid(-lam_v))
        _acc(i, dwr_ref, dwr)
        _acc(i, dwi_ref, dwi)
        _acc(i, dbr_ref, _colsum(dr_pre))
        _acc(i, dbi_ref, _colsum(di_pre))
        _acc(i, dlam_ref, dlam)

    row = pl.BlockSpec((ts, bw), lambda h, i: (i, h))
    prev = pl.BlockSpec((SUBLANE, bw), lambda h, i: (jnp.maximum(i * hb - 1, 0), h))
    wsp = pl.BlockSpec((1, bw, bw), lambda h, i: (h, 0, 0))
    vec = pl.BlockSpec((1, bw), lambda h, i: (0, h))
    return pl.pallas_call(
        body, name="lru_gates_bwd", grid=(nh, s // ts),
        in_specs=[row, row, row, prev, wsp, wsp, vec, vec, vec], out_specs=[row, wsp, wsp, vec, vec, vec],
        out_shape=[jax.ShapeDtypeStruct((s, d), F32), jax.ShapeDtypeStruct((nh, bw, bw), F32),
                   jax.ShapeDtypeStruct((nh, bw, bw), F32)] + [jax.ShapeDtypeStruct((1, d), F32)] * 3,
        compiler_params=_params("parallel", "arbitrary"),
    )(xr, g, hs, hs, wr, wi, br, bi, lam)


def _scan(a, b, reverse, name):
    s, c = a.shape
    cb = _tile(c, 512)
    nt = s // SUBLANE

    def body(a_ref, b_ref, o_ref):
        row = lax.broadcasted_iota(jnp.int32, (SUBLANE, cb), 0)

        def fwd_step(t, carry):
            r0 = pl.multiple_of(t * SUBLANE, SUBLANE)
            aa = a_ref[pl.ds(r0, SUBLANE), :]
            bb = b_ref[pl.ds(r0, SUBLANE), :]
            for sh in (1, 2, 4):
                a_s = jnp.where(row >= sh, pltpu.roll(aa, sh, axis=0), 1.0)
                b_s = jnp.where(row >= sh, pltpu.roll(bb, sh, axis=0), 0.0)
                bb = aa * b_s + bb
                aa = aa * a_s
            h = bb + aa * carry
            o_ref[pl.ds(r0, SUBLANE), :] = h
            return h[SUBLANE - 1:SUBLANE, :]

        def rev_step(k, carry):
            r0 = pl.multiple_of((nt - 1 - k) * SUBLANE, SUBLANE)
            aa = a_ref[pl.ds(r0, SUBLANE), :]
            dd = b_ref[pl.ds(r0, SUBLANE), :]
            cc = aa * dd
            for sh in (1, 2, 4):
                a_s = jnp.where(row < SUBLANE - sh, pltpu.roll(aa, SUBLANE - sh, axis=0), 1.0)
                c_s = jnp.where(row < SUBLANE - sh, pltpu.roll(cc, SUBLANE - sh, axis=0), 0.0)
                cc = cc + aa * c_s
                aa = aa * a_s
            big = cc + aa * carry
            nxt = jnp.where(row < SUBLANE - 1, pltpu.roll(big, SUBLANE - 1, axis=0), carry)
            o_ref[pl.ds(r0, SUBLANE), :] = dd + nxt
            return big[0:1, :]

        lax.fori_loop(0, nt, rev_step if reverse else fwd_step, jnp.zeros((1, cb), F32))

    spec = pl.BlockSpec((s, cb), lambda j: (0, j))
    return pl.pallas_call(body, name=name, grid=(c // cb,), in_specs=[spec, spec], out_specs=spec,
                          out_shape=jax.ShapeDtypeStruct((s, c), F32), compiler_params=_params("parallel"))(a, b)


def _lru_out_fwd(hs, yg):
    s, d = hs.shape
    cb = _tile(d, 1024)

    def body(i, n, h_ref, y_ref, o_ref):
        o_ref[...] = (h_ref[...] * _gelu(y_ref[...])).astype(BF16)

    return _rowwise("lru_out_fwd", body, [("row", hs, cb), ("row", yg, cb)], [("row", None, d, cb, BF16)],
                    s, 512, ncol=d // cb)[0]


def _lru_out_bwd(dy, hs, yg):
    s, d = hs.shape
    cb = _tile(d, 1024)

    def body(i, n, dy_ref, h_ref, y_ref, dh_ref, dyg_ref):
        dyv = dy_ref[...]
        y = y_ref[...]
        dh_ref[...] = dyv * _gelu(y)
        dyg_ref[...] = (dyv * h_ref[...] * _dgelu(y)).astype(BF16)

    return _rowwise("lru_out_bwd", body, [("row", dy, cb), ("row", hs, cb), ("row", yg, cb)],
                    [("row", None, d, cb, F32), ("row", None, d, cb, BF16)], s, 512, ncol=d // cb)


def _swiglu_act(gt, up):
    s, f = gt.shape
    cb = _tile(f, 1024)

    def body(i, n, g_ref, u_ref, o_ref):
        o_ref[...] = (_silu(g_ref[...]) * u_ref[...]).astype(BF16)

    return _rowwise("swiglu_act", body, [("row", gt, cb), ("row", up, cb)], [("row", None, f, cb, BF16)],
                    s, 512, ncol=f // cb)[0]


def _swiglu_bwd(dact, gt, up):
    s, f = gt.shape
    cb = _tile(f, 1024)

    def body(i, n, d_ref, g_ref, u_ref, dg_ref, du_ref):
        d = d_ref[...]
        g = g_ref[...]
        dg_ref[...] = (d * u_ref[...] * _dsilu(g)).astype(BF16)
        du_ref[...] = (d * _silu(g)).astype(BF16)

    return _rowwise("swiglu_bwd", body, [("row", dact, cb), ("row", gt, cb), ("row", up, cb)],
                    [("row", None, f, cb, BF16), ("row", None, f, cb, BF16)], s, 512, ncol=f // cb)


def _ple_fwd(h, gp, pp):
    s, d = h.shape
    cb = _tile(d, 1024)

    def body(i, n, h_ref, g_ref, p_ref, o_ref):
        o_ref[...] = h_ref[...] + _sigmoid(g_ref[...]) * p_ref[...]

    return _rowwise("ple_fwd", body, [("row", h, cb), ("row", gp, cb), ("row", pp, cb)], [("row", None, d, cb, F32)],
                    s, 512, ncol=d // cb)[0]


def _ple_bwd(dh, gp, pp):
    s, d = dh.shape
    cb = _tile(d, 1024)

    def body(i, n, d_ref, g_ref, p_ref, dg_ref, dp_ref):
        dv = d_ref[...]
        sg = _sigmoid(g_ref[...])
        dg_ref[...] = (dv * p_ref[...] * sg * (1.0 - sg)).astype(BF16)
        dp_ref[...] = (dv * sg).astype(BF16)

    return _rowwise("ple_bwd", body, [("row", dh, cb), ("row", gp, cb), ("row", pp, cb)],
                    [("row", None, d, cb, BF16), ("row", None, d, cb, BF16)], s, 512, ncol=d // cb)


def _group_matrix(n_groups):
    r = lax.broadcasted_iota(jnp.int32, (LANE, n_groups * LANE), 0)
    c = lax.broadcasted_iota(jnp.int32, (LANE, n_groups * LANE), 1)
    return ((c % LANE < SSD_HEADS_PER_GROUP) & (r == (c // LANE) * SSD_HEADS_PER_GROUP + c % LANE)).astype(F32)


def _dt_fwd(dt_pre, bias, n_heads, n_groups):
    s = dt_pre.shape[0]
    gl = n_groups * LANE

    def body(i, n, d_ref, b_ref, o_ref):
        lane = lax.broadcasted_iota(jnp.int32, d_ref.shape, 1)
        v = jnp.where(lane < n_heads, _softplus(d_ref[...] + b_ref[...]), 0.0)
        o_ref[...] = jnp.dot(v, _group_matrix(n_groups), preferred_element_type=F32, precision=HIGHEST)

    return _rowwise("ssd_dt_fwd", body, [("row", dt_pre, LANE), ("vec", bias, LANE)], [("row", None, gl, gl, F32)],
                    s, 512)[0]


def _dt_bwd(ddt_g, dt_pre, bias, n_heads, n_groups):
    s = dt_pre.shape[0]
    gl = n_groups * LANE

    def body(i, n, g_ref, d_ref, b_ref, o_ref, db_ref):
        lane = lax.broadcasted_iota(jnp.int32, d_ref.shape, 1)
        ddt = lax.dot_general(g_ref[...], _group_matrix(n_groups), _NT, preferred_element_type=F32, precision=HIGHEST)
        v = jnp.where(lane < n_heads, ddt * _sigmoid(d_ref[...] + b_ref[...]), 0.0)
        o_ref[...] = v.astype(BF16)
        _acc(i, db_ref, _colsum(v))

    return _rowwise("ssd_dt_bwd", body, [("row", ddt_g, gl), ("row", dt_pre, LANE), ("vec", bias, LANE)],
                    [("row", None, LANE, LANE, BF16), ("acc", 1, LANE, LANE, F32)], s, 512)


def _ssd_chunk_terms(dt, alog):
    ln = dt.shape[0]
    a_neg = -jnp.exp(alog)
    row = lax.broadcasted_iota(jnp.int32, (ln, ln), 0)
    col = lax.broadcasted_iota(jnp.int32, (ln, ln), 1)
    tril = row >= col
    cs = jnp.dot(tril.astype(F32), dt * a_neg, preferred_element_type=F32, precision=HIGHEST)
    return a_neg, cs, tril


def _head_lanes(v):
    return [jnp.broadcast_to(v[:, e:e + 1], v.shape) for e in range(SSD_HEADS_PER_GROUP)]


def _ssd_head_scores(bc_cs, cst, cb_mat, tril, e):
    lm = jnp.where(tril, jnp.exp(jnp.minimum(bc_cs[e] - cst[e:e + 1, :], 0.0)), 0.0)
    return (cb_mat * lm).astype(BF16), lm


_NT = (((1,), (1,)), ((), ()))
_TN = (((0,), (0,)), ((), ()))


def _ssd_fwd(xbc, dt, alog, inner, n_groups):
    s = xbc.shape[0]
    ln = SSD_CHUNK
    nc = s // ln
    gw = SSD_HEADS_PER_GROUP * SSD_HEAD_DIM
    npair = gw // LANE
    boff = inner // LANE

    def body(xs_ref, b_ref, c_ref, dt_ref, alog_ref, y_ref, sin_ref, st_ref):
        c = pl.program_id(1)

        @pl.when(c == 0)
        def _():
            st_ref[...] = jnp.zeros_like(st_ref)

        dtv = dt_ref[...]
        _, cs, tril = _ssd_chunk_terms(dtv, alog_ref[...])
        cst = cs.T
        bc_cs, bc_dt = _head_lanes(cs), _head_lanes(dtv)
        xs = xs_ref[...]
        bg = b_ref[...].astype(BF16)
        cg = c_ref[...].astype(BF16)
        cb_mat = lax.dot_general(cg, bg, _NT, preferred_element_type=F32)
        sg = st_ref[...]
        sin_ref[0] = sg
        lo = lax.broadcasted_iota(jnp.int32, (ln, LANE), 1) < SSD_HEAD_DIM
        ys, news = [], []
        for pr in range(npair):
            cols = slice(LANE * pr, LANE * (pr + 1))
            cs_p = jnp.where(lo, bc_cs[2 * pr], bc_cs[2 * pr + 1])
            x = xs[:, cols] * jnp.where(lo, bc_dt[2 * pr], bc_dt[2 * pr + 1])
            tot_p = cs_p[ln - 1:ln, :]
            xp = x.astype(BF16)
            xd = (x * jnp.exp(tot_p - cs_p)).astype(BF16)
            zero = jnp.zeros_like(xp)
            sc0 = _ssd_head_scores(bc_cs, cst, cb_mat, tril, 2 * pr)[0]
            sc1 = _ssd_head_scores(bc_cs, cst, cb_mat, tril, 2 * pr + 1)[0]
            acc = jnp.dot(sc0, jnp.where(lo, xp, zero), preferred_element_type=F32)
            acc = acc + jnp.dot(sc1, jnp.where(lo, zero, xp), preferred_element_type=F32)
            sp = sg[:, cols]
            yoff = jnp.dot(cg, sp.astype(BF16), preferred_element_type=F32) * jnp.exp(cs_p)
            ys.append(acc + yoff)
            news.append(jnp.exp(tot_p) * sp + lax.dot_general(bg, xd, _TN, preferred_element_type=F32))
        y_ref[...] = jnp.concatenate(ys, axis=1)
        st_ref[...] = jnp.concatenate(news, axis=1)

    in_specs = [pl.BlockSpec((ln, gw), lambda g, c: (c, g)),
                pl.BlockSpec((ln, SSD_STATE), lambda g, c: (c, boff + g)),
                pl.BlockSpec((ln, SSD_STATE), lambda g, c: (c, boff + n_groups + g)),
                pl.BlockSpec((ln, LANE), lambda g, c: (c, g)),
                pl.BlockSpec((1, LANE), lambda g, c: (0, g))]
    out_specs = [pl.BlockSpec((ln, gw), lambda g, c: (c, g)),
                 pl.BlockSpec((1, SSD_STATE, gw), lambda g, c: (c, 0, g))]
    return pl.pallas_call(
        body, name="ssd_fwd", grid=(n_groups, nc), in_specs=in_specs, out_specs=out_specs,
        out_shape=[jax.ShapeDtypeStruct((s, inner), F32), jax.ShapeDtypeStruct((nc, SSD_STATE, inner), F32)],
        scratch_shapes=[pltpu.VMEM((SSD_STATE, gw), F32)],
        compiler_params=_params("parallel", "arbitrary"),
    )(xbc, xbc, xbc, dt, alog)


def _ssd_bwd(xbc, dt, alog, dy, y, sin, dskip_e, inner, n_groups):
    s = xbc.shape[0]
    ln = SSD_CHUNK
    nc = s // ln
    gw = SSD_HEADS_PER_GROUP * SSD_HEAD_DIM
    npair = gw // LANE
    boff = inner // LANE

    def body(xs_ref, b_ref, c_ref, dt_ref, alog_ref, dy_ref, y_ref, sin_ref, sout_ref, dsk_ref,
             dxs_ref, db_ref, dc_ref, ddt_ref, dalog_ref, ds_ref):
        step = pl.program_id(1)

        @pl.when(step == 0)
        def _():
            ds_ref[...] = jnp.zeros_like(ds_ref)

        dtv = dt_ref[...]
        a_neg, cs, tril = _ssd_chunk_terms(dtv, alog_ref[...])
        cst = cs.T
        bc_cs, bc_dt = _head_lanes(cs), _head_lanes(dtv)
        xs = xs_ref[...]
        bg = b_ref[...].astype(BF16)
        cg = c_ref[...].astype(BF16)
        cb_mat = lax.dot_general(cg, bg, _NT, preferred_element_type=F32)
        dyv = dy_ref[...]
        yv = y_ref[...]
        dskv = dsk_ref[...]
        s_in = sin_ref[0]
        s_out = sout_ref[0]
        d_s = ds_ref[...]
        lane = lax.broadcasted_iota(jnp.int32, (ln, LANE), 1)
        rowl = lax.broadcasted_iota(jnp.int32, (ln, LANE), 0)
        lo = lane < SSD_HEAD_DIM
        dcb = jnp.zeros((ln, ln), F32)
        dbg = jnp.zeros((ln, SSD_STATE), F32)
        dcg = jnp.zeros((ln, SSD_STATE), F32)
        dcs = jnp.zeros((ln, LANE), F32)
        ddt_x = jnp.zeros((ln, LANE), F32)
        dxs_parts, nds = [], []

        def head_sums(v, pr, into):
            s0 = jnp.sum(jnp.where(lo, v, 0.0), axis=1, keepdims=True)
            s1 = jnp.sum(jnp.where(lo, 0.0, v), axis=1, keepdims=True)
            return into + jnp.where(lane == 2 * pr, s0, 0.0) + jnp.where(lane == 2 * pr + 1, s1, 0.0)

        for pr in range(npair):
            cols = slice(LANE * pr, LANE * (pr + 1))
            cs_p = jnp.where(lo, bc_cs[2 * pr], bc_cs[2 * pr + 1])
            dt_p = jnp.where(lo, bc_dt[2 * pr], bc_dt[2 * pr + 1])
            xs_p = xs[:, cols]
            x = xs_p * dt_p
            tot_p = cs_p[ln - 1:ln, :]
            dec = jnp.exp(tot_p - cs_p)
            xp = x.astype(BF16)
            xd = (x * dec).astype(BF16)
            dy_p = dyv[:, cols]
            dyp = dy_p.astype(BF16)
            dye = (jnp.exp(cs_p) * dy_p).astype(BF16)
            zero = jnp.zeros_like(dyp)
            dxp = jnp.zeros((ln, LANE), F32)
            for e, dym in ((2 * pr, jnp.where(lo, dyp, zero)), (2 * pr + 1, jnp.where(lo, zero, dyp))):
                sc, lm = _ssd_head_scores(bc_cs, cst, cb_mat, tril, e)
                dsc = lax.dot_general(dym, xp, _NT, preferred_element_type=F32)
                dcb = dcb + dsc * lm
                dxp = dxp + lax.dot_general(sc, dym, _TN, preferred_element_type=F32)
            dsp = d_s[:, cols]
            dspb = dsp.astype(BF16)
            dxp = dxp + dec * jnp.dot(bg, dspb, preferred_element_type=F32)
            dcg = dcg + lax.dot_general(dye, s_in[:, cols].astype(BF16), _NT, preferred_element_type=F32)
            dbg = dbg + lax.dot_general(xd, dspb, _NT, preferred_element_type=F32)
            nds.append(jnp.exp(tot_p) * dsp + lax.dot_general(cg, dye, _TN, preferred_element_type=F32))
            dxs_parts.append(dxp * dt_p + dy_p * dskv[:, cols])
            dcs = head_sums(yv[:, cols] * dyp.astype(F32) - xp.astype(F32) * dxp, pr, dcs)
            tot_row = jnp.broadcast_to(_colsum(s_out[:, cols] * dsp), (ln, LANE))
            dcs = head_sums(jnp.where(rowl == ln - 1, tot_row, 0.0), pr, dcs)
            ddt_x = head_sums(dxp * xs_p, pr, ddt_x)
        ds_ref[...] = jnp.concatenate(nds, axis=1)
        dxs_ref[...] = jnp.concatenate(dxs_parts, axis=1)
        dcbb = dcb.astype(BF16)
        dc_ref[...] = dcg + jnp.dot(dcbb, bg, preferred_element_type=F32)
        db_ref[...] = dbg + lax.dot_general(dcbb, cg, _TN, preferred_element_type=F32)
        row = lax.broadcasted_iota(jnp.int32, (ln, ln), 0)
        col = lax.broadcasted_iota(jnp.int32, (ln, ln), 1)
        dadt = jnp.dot((row <= col).astype(F32), dcs, preferred_element_type=F32, precision=HIGHEST)
        ddt_ref[...] = a_neg * dadt + ddt_x
        _acc(step, dalog_ref, _colsum(dadt * dtv) * a_neg)

    def rc(step):
        return nc - 1 - step

    in_specs = [pl.BlockSpec((ln, gw), lambda g, t: (rc(t), g)),
                pl.BlockSpec((ln, SSD_STATE), lambda g, t: (rc(t), boff + g)),
                pl.BlockSpec((ln, SSD_STATE), lambda g, t: (rc(t), boff + n_groups + g)),
                pl.BlockSpec((ln, LANE), lambda g, t: (rc(t), g)),
                pl.BlockSpec((1, LANE), lambda g, t: (0, g)),
                pl.BlockSpec((ln, gw), lambda g, t: (rc(t), g)),
                pl.BlockSpec((ln, gw), lambda g, t: (rc(t), g)),
                pl.BlockSpec((1, SSD_STATE, gw), lambda g, t: (rc(t), 0, g)),
                pl.BlockSpec((1, SSD_STATE, gw), lambda g, t: (jnp.minimum(rc(t) + 1, nc - 1), 0, g)),
                pl.BlockSpec((1, gw), lambda g, t: (0, g))]
    out_specs = [pl.BlockSpec((ln, gw), lambda g, t: (rc(t), g)),
                 pl.BlockSpec((ln, SSD_STATE), lambda g, t: (rc(t), g)),
                 pl.BlockSpec((ln, SSD_STATE), lambda g, t: (rc(t), g)),
                 pl.BlockSpec((ln, LANE), lambda g, t: (rc(t), g)),
                 pl.BlockSpec((1, LANE), lambda g, t: (0, g))]
    gn = n_groups * SSD_STATE
    return pl.pallas_call(
        body, name="ssd_bwd", grid=(n_groups, nc), in_specs=in_specs, out_specs=out_specs,
        out_shape=[jax.ShapeDtypeStruct((s, inner + 2 * gn), F32), jax.ShapeDtypeStruct((s, gn), F32),
                   jax.ShapeDtypeStruct((s, gn), F32), jax.ShapeDtypeStruct((s, n_groups * LANE), F32),
                   jax.ShapeDtypeStruct((1, n_groups * LANE), F32)],
        scratch_shapes=[pltpu.VMEM((SSD_STATE, gw), F32)],
        compiler_params=_params("parallel", "arbitrary"),
    )(xbc, xbc, xbc, dt, alog, dy, y, sin, sin, dskip_e)


def _put_cols(buf, part, block, name):
    s, w = part.shape
    ts = min(512, s)

    def body(b_ref, p_ref, o_ref):
        o_ref[...] = p_ref[...]

    return pl.pallas_call(
        body, name=name, grid=(s // ts,), in_specs=[_ANY, pl.BlockSpec((ts, w), lambda i: (i, 0))],
        out_specs=pl.BlockSpec((ts, w), lambda i: (i, block)), out_shape=jax.ShapeDtypeStruct(buf.shape, buf.dtype),
        input_output_aliases={0: 0}, compiler_params=_params("parallel"))(buf, part)


def _ssd_gate_norm_fwd(ysc, xbc, z, dskip_e, norm_g, n_groups):
    s, inner = ysc.shape
    gsz = inner // n_groups

    def body(i, n, y_ref, x_ref, z_ref, d_ref, g_ref, o_ref):
        y2 = (y_ref[...] + d_ref[...] * x_ref[...]) * _silu(z_ref[...])
        gg = g_ref[...]
        outs = []
        for k in range(n_groups):
            cols = slice(k * gsz, (k + 1) * gsz)
            v = y2[:, cols]
            r = lax.rsqrt(jnp.mean(v * v, axis=-1, keepdims=True) + NORM_EPS)
            outs.append(v * r * gg[:, cols])
        o_ref[...] = jnp.concatenate(outs, axis=1).astype(BF16)

    return _rowwise("ssd_gate_norm_fwd", body,
                    [("row", ysc, inner), ("row", xbc, inner), ("row", z, inner), ("vec", dskip_e, inner),
                     ("vec", norm_g, inner)], [("row", None, inner, inner, BF16)], s, 128)[0]


def _ssd_gate_norm_bwd(dyn, ysc, xbc, z, dskip_e, norm_g, n_groups):
    s, inner = ysc.shape
    gsz = inner // n_groups

    def body(i, n, dn_ref, y_ref, x_ref, z_ref, d_ref, g_ref, dy_ref, dz_ref, dg_ref, dd_ref):
        xs = x_ref[...]
        zz = z_ref[...]
        y = y_ref[...] + d_ref[...] * xs
        sz = _silu(zz)
        y2 = y * sz
        dn = dn_ref[...]
        gg = g_ref[...]
        dy2s, dgs = [], []
        for k in range(n_groups):
            cols = slice(k * gsz, (k + 1) * gsz)
            v = y2[:, cols]
            d = dn[:, cols]
            r = lax.rsqrt(jnp.mean(v * v, axis=-1, keepdims=True) + NORM_EPS)
            vh = v * r
            dgs.append(_colsum(d * vh))
            dvh = d * gg[:, cols]
            dy2s.append(r * (dvh - vh * jnp.mean(dvh * vh, axis=-1, keepdims=True)))
        dy2 = jnp.concatenate(dy2s, axis=1)
        dy = dy2 * sz
        dy_ref[...] = dy
        dz_ref[...] = (dy2 * y * _dsilu(zz)).astype(BF16)
        _acc(i, dg_ref, jnp.concatenate(dgs, axis=1))
        _acc(i, dd_ref, _colsum(dy * xs))

    return _rowwise("ssd_gate_norm_bwd", body,
                    [("row", dyn, inner), ("row", ysc, inner), ("row", xbc, inner), ("row", z, inner),
                     ("vec", dskip_e, inner), ("vec", norm_g, inner)],
                    [("row", None, inner, inner, F32), ("row", None, inner, inner, BF16),
                     ("acc", 1, inner, inner, F32), ("acc", 1, inner, inner, F32)], s, 128)


def _adamw(w, g, m, v, name, emit_g=False):
    rows, c = w.shape
    bc1 = 1.0 - ADAM_B1 ** ADAM_STEP
    bc2 = 1.0 - ADAM_B2 ** ADAM_STEP

    def body(i, n, w_ref, g_ref, m_ref, v_ref, d_ref, mo_ref, vo_ref, *go_ref):
        gg = g_ref[...]
        if emit_g:
            go_ref[0][...] = gg
        mn = ADAM_B1 * m_ref[...] + (1.0 - ADAM_B1) * gg
        vn = ADAM_B2 * v_ref[...] + (1.0 - ADAM_B2) * (gg * gg)
        d_ref[...] = -ADAM_LR * ((mn / bc1) / (jnp.sqrt(vn / bc2) + ADAM_EPS) + ADAM_WD * w_ref[...])
        mo_ref[...] = mn
        vo_ref[...] = vn

    ts = 128 if rows % 128 == 0 else rows
    return _rowwise(name, body, [("row", w, c), ("row", g, c), ("row", m, c), ("row", v, c)],
                    [("row", None, c, c, F32)] * (4 if emit_g else 3), rows, ts)


_ANY = pl.BlockSpec(memory_space=pl.ANY)


def _place():
    x, y, c = lax.axis_index("x"), lax.axis_index("y"), lax.axis_index("c")
    chips = [(1 - x, y), (x, 1 - y), (1 - x, 1 - y)]
    return x, y, c, chips


def _rcopy(src, dst, ssem, rsem, dev):
    return pltpu.make_async_remote_copy(src_ref=src, dst_ref=dst, send_sem=ssem, recv_sem=rsem, device_id=dev,
                                        device_id_type=MESH)


def _place_shard(shards, layer, q_idx, dtype, name, after=None):
    _, r, cc = shards.shape
    tr = _tile(r, 256, 16)

    def body(q_ref, s_ref, *refs):
        refs[-1][0] = s_ref[0].astype(dtype)

    in_specs = [pl.BlockSpec((1, tr, cc), lambda i, q_ref: (layer, i, 0))]
    args = (shards,)
    if after is not None:
        in_specs.append(_ANY)
        args += (after,)
    grid_spec = pltpu.PrefetchScalarGridSpec(
        num_scalar_prefetch=1, grid=(r // tr,), in_specs=in_specs,
        out_specs=pl.BlockSpec((1, tr, cc), lambda i, q_ref: (q_ref[0], i, 0)))
    return pl.pallas_call(body, name=name, grid_spec=grid_spec, out_shape=jax.ShapeDtypeStruct((N_CHIPS, r, cc), dtype),
                          compiler_params=_params("parallel"))(q_idx, *args)


_HBM = pl.BlockSpec(memory_space=pltpu.HBM)
_SEM = pl.BlockSpec(memory_space=pltpu.SEMAPHORE)
_EFFECT = pltpu.SideEffectType.DATAFLOW_SIDE_EFFECTING


def _in_hbm(arrs):
    return [pltpu.with_memory_space_constraint(a, pltpu.HBM) for a in arrs]


def _gather_start(bufs, name):
    n = len(bufs)
    half = [e.shape[1] // 2 for e in bufs]

    def body(*refs):
        ins, send, recv, token = refs[:n], refs[n], refs[n + 1], refs[2 * n + 2]
        x, y, c, chips = _place()
        q = 2 * x + y
        for e in range(n):
            blk = ins[e].at[q, pl.ds(c * half[e], half[e])]
            for j, (cx, cy) in enumerate(chips):
                _rcopy(blk, blk, send.at[3 * e + j], recv.at[3 * e + j], (cx, cy, c)).start()
        token[...] = jnp.zeros_like(token)

    out = pl.pallas_call(
        body, name=name,
        out_shape=(pltpu.SemaphoreType.DMA((3 * n,)), pltpu.SemaphoreType.DMA((3 * n,)),
                   *[pltpu.HBM(b.shape, b.dtype) for b in bufs], jax.ShapeDtypeStruct((SUBLANE, LANE), F32)),
        in_specs=[_HBM] * n, out_specs=(_SEM, _SEM, *[_HBM] * n, pl.BlockSpec(memory_space=pltpu.VMEM)),
        input_output_aliases={e: 2 + e for e in range(n)},
        compiler_params=pltpu.CompilerParams(has_side_effects=_EFFECT))(*_in_hbm(bufs))
    return out[0], out[1], list(out[2:2 + n]), out[2 + n]


def _gather_wait(send, recv, bufs, after, name):
    n = len(bufs)
    half = [e.shape[1] // 2 for e in bufs]

    def body(*refs):
        ins, send_ref, recv_ref = refs[:n], refs[n], refs[n + 1]
        x, y, c, chips = _place()
        q = 2 * x + y
        for e in range(n):
            rows = pl.ds(c * half[e], half[e])
            for j, (cx, cy) in enumerate(chips):
                cp = _rcopy(ins[e].at[q, rows], ins[e].at[2 * cx + cy, rows], send_ref.at[3 * e + j],
                            recv_ref.at[3 * e + j], (cx, cy, c))
                cp.wait_send()
                cp.wait_recv()

    return list(pl.pallas_call(
        body, name=name, out_shape=tuple(pltpu.HBM(b.shape, b.dtype) for b in bufs),
        in_specs=[_HBM] * n + [_SEM, _SEM, _ANY], out_specs=[_HBM] * n,
        input_output_aliases={e: e for e in range(n)},
        compiler_params=pltpu.CompilerParams(has_side_effects=_EFFECT))(*bufs, send, recv, after))


def _forward_sibling(bufs, name):
    n = len(bufs)
    half = [e.shape[1] // 2 for e in bufs]

    def body(*refs):
        outs = refs[n:2 * n]
        send, recv = refs[2 * n:]
        x, y, c, chips = _place()
        sib = (x, y, 1 - c)
        cps = []
        for e in range(n):
            for j, (cx, cy) in enumerate(chips):
                blk = outs[e].at[2 * cx + cy, pl.ds(c * half[e], half[e])]
                cps.append(_rcopy(blk, blk, send.at[3 * e + j], recv.at[3 * e + j], sib))
        for cp in cps:
            cp.start()
        for e in range(n):
            for j, (cx, cy) in enumerate(chips):
                blk = outs[e].at[2 * cx + cy, pl.ds((1 - c) * half[e], half[e])]
                _rcopy(blk, blk, send.at[3 * e + j], recv.at[3 * e + j], sib).wait_recv()
        for cp in cps:
            cp.wait_send()

    return list(pl.pallas_call(
        body, name=name, in_specs=[_ANY] * n, out_specs=[_ANY] * n,
        out_shape=[jax.ShapeDtypeStruct(e.shape, e.dtype) for e in bufs],
        input_output_aliases={e: e for e in range(n)},
        scratch_shapes=[pltpu.SemaphoreType.DMA((3 * n,))] * 2,
    )(*bufs))


def _scatter_start(entries, name):
    n = len(entries)
    lands = [lax.empty(e.shape, e.dtype) for e in entries]

    def body(*refs):
        ins, land, send, recv, token = refs[:n], refs[n:2 * n], refs[2 * n], refs[2 * n + 1], refs[4 * n + 2]
        x, y, c, chips = _place()
        q = 2 * x + y
        for e in range(n):
            for j, (cx, cy) in enumerate(chips):
                _rcopy(ins[e].at[2 * cx + cy], land[e].at[q], send.at[3 * e + j], recv.at[3 * e + j],
                       (cx, cy, c)).start()
        token[...] = jnp.zeros_like(token)

    out = pl.pallas_call(
        body, name=name,
        out_shape=(pltpu.SemaphoreType.DMA((3 * n,)), pltpu.SemaphoreType.DMA((3 * n,)),
                   *[pltpu.HBM(b.shape, b.dtype) for b in entries + lands],
                   jax.ShapeDtypeStruct((SUBLANE, LANE), F32)),
        in_specs=[_HBM] * (2 * n),
        out_specs=(_SEM, _SEM, *[_HBM] * (2 * n), pl.BlockSpec(memory_space=pltpu.VMEM)),
        input_output_aliases={e: 2 + e for e in range(2 * n)},
        compiler_params=pltpu.CompilerParams(has_side_effects=_EFFECT))(*_in_hbm(entries + lands))
    return out[0], out[1], list(out[2:2 + n]), list(out[2 + n:2 + 2 * n]), out[2 + 2 * n]


def _scatter_wait(send, recv, entries, lands, after, name):
    n = len(entries)

    def body(*refs):
        ins, land, send_ref, recv_ref = refs[:n], refs[n:2 * n], refs[2 * n], refs[2 * n + 1]
        x, y, c, chips = _place()
        for e in range(n):
            for j, (cx, cy) in enumerate(chips):
                k = 2 * cx + cy
                cp = _rcopy(ins[e].at[k], land[e].at[k], send_ref.at[3 * e + j], recv_ref.at[3 * e + j],
                            (cx, cy, c))
                cp.wait_send()
                cp.wait_recv()

    out = pl.pallas_call(
        body, name=name, out_shape=tuple(pltpu.HBM(b.shape, b.dtype) for b in entries + lands),
        in_specs=[_HBM] * (2 * n) + [_SEM, _SEM, _ANY], out_specs=[_HBM] * (2 * n),
        input_output_aliases={e: e for e in range(2 * n)},
        compiler_params=pltpu.CompilerParams(has_side_effects=_EFFECT))(*entries, *lands, send, recv, after)
    return list(out[:n]), list(out[n:])


def _swap_halves(entries, name):
    n = len(entries)
    half = [e.shape[1] // 2 for e in entries]

    def body(*refs):
        ins, outs = refs[:n], refs[n:2 * n]
        send, recv = refs[2 * n:]
        x, y, c, _ = _place()
        cps = [_rcopy(ins[e].at[:, pl.ds((1 - c) * half[e], half[e]), :], outs[e], send.at[e], recv.at[e],
                      (x, y, 1 - c)) for e in range(n)]
        for cp in cps:
            cp.start()
        for cp in cps:
            cp.wait()

    return pl.pallas_call(
        body, name=name, in_specs=[_ANY] * n, out_specs=[_ANY] * n,
        out_shape=[jax.ShapeDtypeStruct((N_CHIPS, h, e.shape[2]), e.dtype) for e, h in zip(entries, half)],
        scratch_shapes=[pltpu.SemaphoreType.DMA((n,))] * 2,
    )(*entries)


def _join_halves(bufs, name):
    n = len(bufs)
    pairs = [(o, layer) for o in range(n) for layer in range(bufs[o].shape[0])]
    npair = len(pairs)

    def body(*refs):
        outs = refs[n:2 * n]
        send, recv = refs[2 * n:]
        x, y, c, _ = _place()
        cps = []
        for k, (o, layer) in enumerate(pairs):
            r2 = bufs[o].shape[1] // 2
            blk = outs[o].at[layer, pl.ds(c * r2, r2)]
            cps.append(_rcopy(blk, blk, send.at[k], recv.at[k], (x, y, 1 - c)))
        for cp in cps:
            cp.start()
        for k, (o, layer) in enumerate(pairs):
            r2 = bufs[o].shape[1] // 2
            blk = outs[o].at[layer, pl.ds((1 - c) * r2, r2)]
            _rcopy(blk, blk, send.at[k], recv.at[k], (x, y, 1 - c)).wait_recv()
        for cp in cps:
            cp.wait_send()

    return pl.pallas_call(
        body, name=name, in_specs=[_ANY] * n, out_specs=[_ANY] * n,
        out_shape=[jax.ShapeDtypeStruct(b.shape, b.dtype) for b in bufs],
        input_output_aliases={e: e for e in range(n)},
        scratch_shapes=[pltpu.SemaphoreType.DMA((npair,))] * 2,
    )(*bufs)


def _gather_all(v, name):
    def body(v_ref, o_ref, send, recv, loc):
        x, y, c, _ = _place()
        me = 4 * x + 2 * y + c
        mine = pltpu.make_async_copy(v_ref, o_ref.at[me], loc)
        mine.start()
        peers = []
        for k in range(1, N_DEV):
            px = 1 - x if k & 4 else x
            py = 1 - y if k & 2 else y
            pc = 1 - c if k & 1 else c
            peers.append((px, py, pc))
        cps = [_rcopy(v_ref, o_ref.at[me], send.at[k], recv.at[k], peers[k]) for k in range(N_DEV - 1)]
        for cp in cps:
            cp.start()
        for k, (px, py, pc) in enumerate(peers):
            blk = o_ref.at[4 * px + 2 * py + pc]
            _rcopy(blk, blk, send.at[k], recv.at[k], (px, py, pc)).wait_recv()
        for cp in cps:
            cp.wait_send()
        mine.wait()

    return pl.pallas_call(
        body, name=name, in_specs=[_ANY], out_specs=_ANY, out_shape=jax.ShapeDtypeStruct((N_DEV,) + v.shape, v.dtype),
        scratch_shapes=[pltpu.SemaphoreType.DMA((N_DEV - 1,))] * 2 + [pltpu.SemaphoreType.DMA],
    )(v)


def _add_own_half(gst, rx, c_idx, name):
    _, r, cc = gst.shape
    r2 = r // 2
    tr = _tile(r2, 256, 16)
    g4 = gst.reshape(N_CHIPS, 2, r2, cc)

    def body(c_ref, g_ref, r_ref, o_ref):
        o_ref[...] = (g_ref[0].astype(F32) + r_ref[...].astype(F32)).astype(BF16)

    grid_spec = pltpu.PrefetchScalarGridSpec(
        num_scalar_prefetch=1, grid=(N_CHIPS, r2 // tr),
        in_specs=[pl.BlockSpec((1, 1, tr, cc), lambda k, i, c_ref: (k, c_ref[0], i, 0)),
                  pl.BlockSpec((1, tr, cc), lambda k, i, c_ref: (k, i, 0))],
        out_specs=pl.BlockSpec((1, tr, cc), lambda k, i, c_ref: (k, i, 0)))
    return pl.pallas_call(body, name=name, grid_spec=grid_spec, out_shape=jax.ShapeDtypeStruct((N_CHIPS, r2, cc), BF16),
                          compiler_params=_params("parallel", "parallel"))(c_idx, g4, rx)


def _sum_into(buf, rx, own, layer, qc, out_shape, name):
    _, r2, cc = rx.shape
    tr = _tile(r2, 256, 16)
    nb = r2 // tr

    def body(qc_ref, *refs):
        rx_ref, own_ref, o_ref = refs[-3:]
        q = qc_ref[0]
        acc = None
        for k in range(N_CHIPS):
            v = jnp.where(q == k, own_ref[0], rx_ref[k]).astype(F32)
            acc = v if acc is None else acc + v
        o_ref[0] = acc

    in_specs = [pl.BlockSpec((N_CHIPS, tr, cc), lambda i, qc_ref: (0, i, 0)),
                pl.BlockSpec((1, tr, cc), lambda i, qc_ref: (qc_ref[0], i, 0))]
    args = (rx, own)
    aliases = {}
    if buf is not None:
        in_specs = [_ANY] + in_specs
        args = (buf,) + args
        aliases = {1: 0}
    grid_spec = pltpu.PrefetchScalarGridSpec(
        num_scalar_prefetch=1, grid=(nb,), in_specs=in_specs,
        out_specs=pl.BlockSpec((1, tr, cc), lambda i, qc_ref: (layer, qc_ref[1] * nb + i, 0)))
    return pl.pallas_call(body, name=name, grid_spec=grid_spec, out_shape=jax.ShapeDtypeStruct(out_shape, F32),
                          input_output_aliases=aliases, compiler_params=_params("parallel"))(qc, *args)


def _sum_slots(st, name):
    k, r, cc = st.shape
    tr = _tile(r, 256, 8)

    def body(s_ref, o_ref):
        acc = s_ref[0].astype(F32)
        for j in range(1, k):
            acc = acc + s_ref[j].astype(F32)
        o_ref[...] = acc

    return pl.pallas_call(body, name=name, grid=(r // tr,), in_specs=[pl.BlockSpec((k, tr, cc), lambda i: (0, i, 0))],
                          out_specs=pl.BlockSpec((tr, cc), lambda i: (i, 0)),
                          out_shape=jax.ShapeDtypeStruct((r, cc), F32), compiler_params=_params("parallel"))(st)


def _pack(arrs, rows_mult=2 * SUBLANE):
    flat = jnp.concatenate([a.reshape(-1).astype(F32) for a in arrs])
    quantum = rows_mult * LANE
    padded = -(-flat.shape[0] // quantum) * quantum
    return jnp.pad(flat, (0, padded - flat.shape[0])).reshape(-1, LANE)


def _unpack(buf, shapes):
    flat = buf.reshape(-1)
    out, off = [], 0
    for sh in shapes:
        size = 1
        for d in sh:
            size *= d
        out.append(flat[off:off + size].reshape(sh))
        off += size
    return out


def kernel(x, p, norm_mix_g, norm_ffn_g, norm_ple_g, final_norm_g, a_w_in, a_conv_w, a_conv_b, a_w_gate_r, a_b_gate_r, a_w_gate_i, a_b_gate_i, a_lambda, a_w_out, b_w_in, b_conv_w, b_conv_b, b_dt_bias, b_a_log, b_d_skip, b_norm_g, b_w_out, ffn_w_gate, ffn_w_up, ffn_w_down, ple_w_proj, ple_w_gate, loss_target, m_norm_mix_g, m_norm_ffn_g, m_norm_ple_g, m_final_norm_g, m_a_w_in, m_a_conv_w, m_a_conv_b, m_a_w_gate_r, m_a_b_gate_r, m_a_w_gate_i, m_a_b_gate_i, m_a_lambda, m_a_w_out, m_b_w_in, m_b_conv_w, m_b_conv_b, m_b_dt_bias, m_b_a_log, m_b_d_skip, m_b_norm_g, m_b_w_out, m_ffn_w_gate, m_ffn_w_up, m_ffn_w_down, m_ple_w_proj, m_ple_w_gate, v_norm_mix_g, v_norm_ffn_g, v_norm_ple_g, v_final_norm_g, v_a_w_in, v_a_conv_w, v_a_conv_b, v_a_w_gate_r, v_a_b_gate_r, v_a_w_gate_i, v_a_b_gate_i, v_a_lambda, v_a_w_out, v_b_w_in, v_b_conv_w, v_b_conv_b, v_b_dt_bias, v_b_a_log, v_b_d_skip, v_b_norm_g, v_b_w_out, v_ffn_w_gate, v_ffn_w_up, v_ffn_w_down, v_ple_w_proj, v_ple_w_gate):
    names = ["norm_mix_g", "norm_ffn_g", "norm_ple_g", "final_norm_g", "a_w_in", "a_conv_w", "a_conv_b", "a_w_gate_r",
             "a_b_gate_r", "a_w_gate_i", "a_b_gate_i", "a_lambda", "a_w_out", "b_w_in", "b_conv_w", "b_conv_b",
             "b_dt_bias", "b_a_log", "b_d_skip", "b_norm_g", "b_w_out", "ffn_w_gate", "ffn_w_up", "ffn_w_down",
             "ple_w_proj", "ple_w_gate"]
    w_in = dict(zip(names, [norm_mix_g, norm_ffn_g, norm_ple_g, final_norm_g, a_w_in, a_conv_w, a_conv_b, a_w_gate_r,
                            a_b_gate_r, a_w_gate_i, a_b_gate_i, a_lambda, a_w_out, b_w_in, b_conv_w, b_conv_b,
                            b_dt_bias, b_a_log, b_d_skip, b_norm_g, b_w_out, ffn_w_gate, ffn_w_up, ffn_w_down,
                            ple_w_proj, ple_w_gate]))
    m_in = dict(zip(names, [m_norm_mix_g, m_norm_ffn_g, m_norm_ple_g, m_final_norm_g, m_a_w_in, m_a_conv_w,
                            m_a_conv_b, m_a_w_gate_r, m_a_b_gate_r, m_a_w_gate_i, m_a_b_gate_i, m_a_lambda,
                            m_a_w_out, m_b_w_in, m_b_conv_w, m_b_conv_b, m_b_dt_bias, m_b_a_log, m_b_d_skip,
                            m_b_norm_g, m_b_w_out, m_ffn_w_gate, m_ffn_w_up, m_ffn_w_down, m_ple_w_proj,
                            m_ple_w_gate]))
    v_in = dict(zip(names, [v_norm_mix_g, v_norm_ffn_g, v_norm_ple_g, v_final_norm_g, v_a_w_in, v_a_conv_w,
                            v_a_conv_b, v_a_w_gate_r, v_a_b_gate_r, v_a_w_gate_i, v_a_b_gate_i, v_a_lambda,
                            v_a_w_out, v_b_w_in, v_b_conv_w, v_b_conv_b, v_b_dt_bias, v_b_a_log, v_b_d_skip,
                            v_b_norm_g, v_b_w_out, v_ffn_w_gate, v_ffn_w_up, v_ffn_w_down, v_ple_w_proj,
                            v_ple_w_gate]))

    s, d = x.shape[1], x.shape[2]
    depth = norm_mix_g.shape[0]
    assert depth == 2
    q_idx = 2 * lax.axis_index("x") + lax.axis_index("y")
    c_idx = lax.axis_index("c").astype(jnp.int32).reshape(1)

    inner = b_w_out.shape[1] * N_CHIPS
    n_heads = inner // SSD_HEAD_DIM
    n_groups = n_heads // SSD_HEADS_PER_GROUP
    gn = n_groups * SSD_STATE
    xbcw = inner + 2 * gn
    assert b_conv_w.shape[2] * N_CHIPS == xbcw and n_heads <= LANE

    big = [("a_w_in", "col"), ("a_w_gate_r", "gate"), ("a_w_gate_i", "gate"), ("a_w_out", "row"),
           ("ffn_w_gate", "col"), ("ffn_w_up", "col"), ("ffn_w_down", "row"), ("ple_w_proj", "col"),
           ("ple_w_gate", "row"), ("b_w_in", "col"), ("b_w_out", "row")]
    kind_of = dict(big)

    def shard2d(name, arr):
        if kind_of[name] == "gate":
            return [arr[l].reshape(-1, arr.shape[-1]) for l in range(arr.shape[0])]
        return [arr[l] for l in range(arr.shape[0])]

    small_sharded = ["a_conv_w", "a_b_gate_r", "a_b_gate_i", "b_conv_w", "b_conv_b", "b_norm_g"]
    small_pack = _pack([w_in[nm] for nm in small_sharded], rows_mult=16)

    q_vec = q_idx.astype(jnp.int32).reshape(1)
    qc_vec = jnp.stack([q_idx, lax.axis_index("c")]).astype(jnp.int32)
    gather_groups = [
        [("a_w_in", 0), ("a_w_gate_r", 0), ("a_w_gate_i", 0), ("a_w_out", 0), ("small", 0)],
        [("ffn_w_gate", 0), ("ffn_w_up", 0)],
        [("ffn_w_down", 0), ("ple_w_proj", 0), ("ple_w_gate", 0)],
        [("b_w_in", 0), ("b_w_out", 0)],
        [("ffn_w_gate", 1), ("ffn_w_up", 1), ("ffn_w_down", 1), ("ple_w_proj", 1), ("ple_w_gate", 1)],
    ]
    gather_started = [None] * len(gather_groups)

    def gather_begin(gi, after=None):
        bufs = []
        for nm, l in gather_groups[gi]:
            if nm == "small":
                bufs.append(_place_shard(small_pack[None], 0, q_vec, F32, "place_small", after))
            else:
                arr = w_in[nm]
                arr = arr.reshape(arr.shape[0], -1, arr.shape[-1]) if kind_of[nm] == "gate" else arr
                bufs.append(_place_shard(arr, l, q_vec, BF16, f"place_{nm}{l}", after))
        gather_started[gi] = _gather_start(bufs, f"gather_start{gi}")
        return gather_started[gi][3][0:1, 0:1]

    wst = {}

    def gather_finish(gi, after):
        send, recv, thru, _ = gather_started[gi]
        landed = _gather_wait(send, recv, thru, after, f"gather_wait{gi}")
        for k, arr in zip(gather_groups[gi], _forward_sibling(landed, f"gather_forward{gi}")):
            wst[k] = arr

    gather_finish(0, gather_begin(0) + gather_begin(1) + gather_begin(2))
    small_st = wst[("small", 0)]
    kept_stacked = ("a_w_in", "ffn_w_gate", "ffn_w_up", "ple_w_proj")

    def whole(nm, l):
        st = wst[(nm, l)]
        kind = kind_of[nm]
        if nm in kept_stacked:
            return st
        if kind == "row":
            return st.reshape(-1, st.shape[-1])
        if kind == "col":
            return jnp.concatenate([st[k] for k in range(N_CHIPS)], axis=1)
        heads = w_in[nm].shape[1]
        return st.reshape(N_CHIPS, heads, -1, st.shape[-1]).transpose(1, 0, 2, 3).reshape(heads, -1, st.shape[-1])

    small_parts = [_unpack(small_st[k], [w_in[nm].shape for nm in small_sharded]) for k in range(N_CHIPS)]
    small_full = {nm: jnp.concatenate([small_parts[k][i] for k in range(N_CHIPS)], axis=-1)
                  for i, nm in enumerate(small_sharded)}
    a_cw = small_full["a_conv_w"][0]
    a_br = small_full["a_b_gate_r"][0].reshape(1, -1)
    a_bi = small_full["a_b_gate_i"][0].reshape(1, -1)
    b_cw = small_full["b_conv_w"][0]
    b_cb = small_full["b_conv_b"]
    b_ng = small_full["b_norm_g"]

    def pad_lanes(v):
        return jnp.pad(v, ((0, 0), (0, LANE - v.shape[1])))

    dt_bias = pad_lanes(b_dt_bias)
    a_log = pad_lanes(b_a_log.reshape(n_groups, SSD_HEADS_PER_GROUP)).reshape(1, n_groups * LANE)
    dskip_e = jnp.repeat(b_d_skip, SSD_HEAD_DIM, axis=1)

    w_a_in = whole("a_w_in", 0)
    w_ax, w_ay = w_a_in[:N_CHIPS // 2], w_a_in[N_CHIPS // 2:]
    w_ar, w_ai, w_ao = whole("a_w_gate_r", 0), whole("a_w_gate_i", 0), whole("a_w_out", 0)
    w_fg, w_fu, w_fd, w_pp, w_pg = ([None] * depth for _ in range(5))

    def take_ffn_in(l):
        w_fg[l], w_fu[l] = whole("ffn_w_gate", l), whole("ffn_w_up", l)

    def take_ffn_out_ple(l):
        w_fd[l], w_pp[l], w_pg[l] = whole("ffn_w_down", l), whole("ple_w_proj", l), whole("ple_w_gate", l)

    grads = {}

    h0 = x[0]
    g_mix = [norm_mix_g[l:l + 1] for l in range(depth)]
    g_ffn = [norm_ffn_g[l:l + 1] for l in range(depth)]
    g_ple = [norm_ple_g[l:l + 1] for l in range(depth)]
    g_fin = final_norm_g.reshape(1, -1)

    u0 = _rmsnorm_fwd(h0, g_mix[0], "norm_mix0")
    xr_pre = _mm(u0, w_ax, stacked_b=True, name="lru_in_x")
    yg = _mm(u0, w_ay, stacked_b=True, name="lru_in_y")
    xr = _conv_fwd(xr_pre, a_cw, a_conv_b, False, "lru_conv")
    lru_a, lru_b = _lru_gates_fwd(xr, w_ar, w_ai, a_br, a_bi, a_lambda)
    hs = _scan(lru_a, lru_b, False, "lru_scan")
    y_lru = _lru_out_fwd(hs, yg)
    h_mix = [_mm(y_lru, w_ao, add=h0, name="lru_out"), None]

    def ffn_ple_fwd(h_in, l, before_down=None):
        n_f = _rmsnorm_fwd(h_in, g_ffn[l], f"norm_ffn{l}")
        gt = _mm(n_f, w_fg[l], stacked_b=True, name=f"ffn_gate{l}")
        up = _mm(n_f, w_fu[l], stacked_b=True, name=f"ffn_up{l}")
        act = _swiglu_act(gt, up)
        if before_down is not None:
            before_down(act)
        h_f = _mm(act, w_fd[l], add=h_in, name=f"ffn_down{l}")
        n_p = _rmsnorm_fwd(h_f, g_ple[l], f"norm_ple{l}")
        gp = _mm(n_p, w_pg[l], name=f"ple_gate{l}")
        pp = _mm(p[l, 0], w_pp[l], stacked_b=True, name=f"ple_proj{l}")
        h_out = _ple_fwd(h_f, gp, pp)
        return h_out, dict(h_in=h_in, n_f=n_f, gt=gt, up=up, act=act, h_f=h_f, n_p=n_p, gp=gp, pp=pp)

    gather_finish(1, gather_begin(3, h_mix[0]) + gather_begin(4, h_mix[0]))
    take_ffn_in(0)

    def finish_ffn_out_ple0(act):
        gather_finish(2, act)
        take_ffn_out_ple(0)

    h_l0, sv0 = ffn_ple_fwd(h_mix[0], 0, finish_ffn_out_ple0)

    gather_finish(3, h_l0)
    w_b_in = whole("b_w_in", 0)
    w_bz, w_bx = w_b_in[:, :inner], w_b_in[:, inner:inner + xbcw]
    w_bd = pad_lanes(w_b_in[:, inner + xbcw:])
    w_bo = whole("b_w_out", 0)
    u1 = _rmsnorm_fwd(h_l0, g_mix[1], "norm_mix1")
    z = _mm(u1, w_bz, name="ssd_in_z")
    xbc_pre = _mm(u1, w_bx, name="ssd_in_xbc")
    dt_pre = _mm(u1, w_bd, name="ssd_in_dt")
    xbc = _conv_fwd(xbc_pre, b_cw, b_cb, True, "ssd_conv")
    dt = _dt_fwd(dt_pre, dt_bias, n_heads, n_groups)
    ysc, s_in = _ssd_fwd(xbc, dt, a_log, inner, n_groups)
    yn = _ssd_gate_norm_fwd(ysc, xbc, z, dskip_e, b_ng, n_groups)
    h_mix[1] = _mm(yn, w_bo, add=h_l0, name="ssd_out")
    gather_finish(4, h_mix[1])
    take_ffn_in(1)
    take_ffn_out_ple(1)
    h_l1, sv1 = ffn_ple_fwd(h_mix[1], 1)

    dh, dg_fin, loss_row = _final_loss_bwd(h_l1, g_fin, loss_target[0])

    d_norm_ffn, d_norm_ple, d_norm_mix = [None] * depth, [None] * depth, [None] * depth
    for nm in ("ffn_w_gate", "ffn_w_up", "ffn_w_down", "ple_w_proj", "ple_w_gate"):
        grads[nm] = [None] * depth

    def stacked(nm, gfull):
        kind = kind_of[nm]
        if nm in kept_stacked:
            return gfull
        if kind == "row":
            return gfull.reshape(N_CHIPS, -1, gfull.shape[-1])
        if kind == "col":
            n_loc = gfull.shape[1] // N_CHIPS
            return jnp.stack([gfull[:, k * n_loc:(k + 1) * n_loc] for k in range(N_CHIPS)])
        heads, bw, _ = gfull.shape
        return gfull.reshape(heads, N_CHIPS, bw // N_CHIPS, bw).transpose(1, 0, 2, 3).reshape(N_CHIPS, -1, bw)

    reduce_started = []

    def reduce_start(keys, tag):
        gst = [stacked(nm, grads[nm][l]) for nm, l in keys]
        from_sib = _swap_halves(gst, f"reduce_swap_{tag}")
        chip_sum = [_add_own_half(g, r, c_idx, f"reduce_add_{nm}{l}") for g, r, (nm, l) in zip(gst, from_sib, keys)]
        send, recv, ents, lands, token = _scatter_start(chip_sum, f"reduce_scatter_start_{tag}")
        reduce_started.append((keys, tag, send, recv, ents, lands))
        return token[0:1, 0:1]

    def ffn_ple_keys(l):
        return [("ple_w_gate", l), ("ple_w_proj", l), ("ffn_w_down", l), ("ffn_w_gate", l), ("ffn_w_up", l)]

    def ffn_ple_bwd(dh_out, sv, l, g_ple_l):
        dgp, dpp = _ple_bwd(dh_out, sv["gp"], sv["pp"])
        grads["ple_w_gate"][l] = _mm(sv["n_p"], dgp, ta=True, out_dtype=BF16, name=f"ple_gate_dw{l}")
        grads["ple_w_proj"][l] = _mm(p[l, 0], dpp, ta=True, out_dtype=BF16, stacked_out=N_CHIPS,
                                     name=f"ple_proj_dw{l}")
        dn = _mm(dgp, w_pg[l], tb=True, name=f"ple_gate_dx{l}")
        dh_f, d_norm_ple[l] = _rmsnorm_bwd(dn, sv["h_f"], g_ple_l, dh_out, f"norm_ple_bwd{l}")
        grads["ffn_w_down"][l] = _mm(sv["act"], dh_f, ta=True, out_dtype=BF16, name=f"ffn_down_dw{l}")
        dact = _mm(dh_f, w_fd[l], tb=True, name=f"ffn_down_dx{l}")
        dgt, dup = _swiglu_bwd(dact, sv["gt"], sv["up"])
        grads["ffn_w_gate"][l] = _mm(sv["n_f"], dgt, ta=True, out_dtype=BF16, stacked_out=N_CHIPS,
                                     name=f"ffn_gate_dw{l}")
        grads["ffn_w_up"][l] = _mm(sv["n_f"], dup, ta=True, out_dtype=BF16, stacked_out=N_CHIPS, name=f"ffn_up_dw{l}")
        dn = _mm(dgt, w_fg[l], tb=True, stacked_b=True, name=f"ffn_gate_dx{l}")
        dn = _mm(dup, w_fu[l], tb=True, stacked_b=True, add=dn, name=f"ffn_up_dx{l}")
        dh_in, d_norm_ffn[l] = _rmsnorm_bwd(dn, sv["h_in"], g_ffn[l], dh_f, f"norm_ffn_bwd{l}")
        return dh_in

    dh = ffn_ple_bwd(dh, sv1, 1, g_ple[1])
    tok = reduce_start(ffn_ple_keys(1), "l1")

    grads["b_w_out"] = [_mm(yn, dh, ta=True, out_dtype=BF16, name="ssd_out_dw")]
    dyn = _mm(dh, w_bo, tb=True, name="ssd_out_dx")
    dy_ssd, dz, d_b_norm_g, dd_lane = _ssd_gate_norm_bwd(dyn, ysc, xbc, z, dskip_e, b_ng + tok, n_groups)
    dxs, d_bm, d_cm, ddt, d_a_log = _ssd_bwd(xbc, dt, a_log, dy_ssd, ysc, s_in, dskip_e, inner, n_groups)
    dxbc = _put_cols(dxs, d_bm, inner // gn, "ssd_put_db")
    dxbc = _put_cols(dxbc, d_cm, inner // gn + 1, "ssd_put_dc")
    dconv = _silu_conv_bwd_pre(dxbc, xbc_pre, b_cw, b_cb, "ssd_conv_bwd_pre")
    dxbc_pre, d_b_conv_w, d_b_conv_b = _conv_bwd(dconv, xbc_pre, b_cw, "ssd_conv_bwd")
    ddt_pre, d_dt_bias = _dt_bwd(ddt, dt_pre, dt_bias, n_heads, n_groups)
    d_a_log = d_a_log.reshape(n_groups, LANE)[:, :SSD_HEADS_PER_GROUP].reshape(1, n_heads)
    gw_bz = _mm(u1, dz, ta=True, out_dtype=BF16, name="ssd_in_z_dw")
    gw_bx = _mm(u1, dxbc_pre, ta=True, out_dtype=BF16, name="ssd_in_xbc_dw")
    gw_bd = _mm(u1, ddt_pre, ta=True, out_dtype=BF16, name="ssd_in_dt_dw")
    grads["b_w_in"] = [jnp.concatenate([gw_bz, gw_bx, gw_bd[:, :n_heads]], axis=1)]
    du = _mm(dz, w_bz, tb=True, name="ssd_in_z_dx")
    du = _mm(dxbc_pre, w_bx, tb=True, add=du, name="ssd_in_xbc_dx")
    du = _mm(ddt_pre, w_bd, tb=True, add=du, name="ssd_in_dt_dx")
    dh, d_norm_mix[1] = _rmsnorm_bwd(du, h_l0, g_mix[1], dh, "norm_mix_bwd1")
    tok = reduce_start([("b_w_out", 0), ("b_w_in", 0)], "ssd")

    dh = ffn_ple_bwd(dh, sv0, 0, g_ple[0] + tok)
    tok = reduce_start(ffn_ple_keys(0), "l0")

    grads["a_w_out"] = [_mm(y_lru, dh, ta=True, out_dtype=BF16, name="lru_out_dw")]
    dy_lru = _mm(dh, w_ao, tb=True, name="lru_out_dx")
    dhs, dyg = _lru_out_bwd(dy_lru, hs, yg)
    g_scan = _scan(lru_a, dhs, True, "lru_scan_bwd")
    dxr, d_wr, d_wi, d_br, d_bi, d_lam = _lru_gates_bwd(xr, g_scan, hs, w_ar, w_ai, a_br, a_bi, a_lambda + tok)
    dxr_pre, d_a_conv_w, d_a_conv_b = _conv_bwd(dxr, xr_pre, a_cw, "lru_conv_bwd")
    gw_ax = _mm(u0, dxr_pre, ta=True, out_dtype=BF16, stacked_out=N_CHIPS // 2, name="lru_in_x_dw")
    gw_ay = _mm(u0, dyg, ta=True, out_dtype=BF16, stacked_out=N_CHIPS // 2, name="lru_in_y_dw")
    grads["a_w_in"] = [jnp.concatenate([gw_ax, gw_ay], axis=0)]
    grads["a_w_gate_r"] = [d_wr.astype(BF16)]
    grads["a_w_gate_i"] = [d_wi.astype(BF16)]
    du = _mm(dxr_pre, w_ax, tb=True, stacked_b=True, name="lru_in_x_dx")
    du = _mm(dyg, w_ay, tb=True, stacked_b=True, add=du, name="lru_in_y_dx")
    grad_x, d_norm_mix[0] = _rmsnorm_bwd(du, h0, g_mix[0], dh, "norm_mix_bwd0")

    tok = reduce_start([("a_w_out", 0), ("a_w_in", 0), ("a_w_gate_r", 0), ("a_w_gate_i", 0)], "lru")
    grad_out, delta_out, m_out, v_out = {}, {}, {}, {}

    def reduce_finish(groups, after, tag):
        g_half = {}
        for keys, gtag, send, recv, ents, lands in groups:
            ents, lands = _scatter_wait(send, recv, ents, lands, after, f"reduce_scatter_wait_{gtag}")
            for rx, own, (nm, l) in zip(lands, ents, keys):
                sh = shard2d(nm, w_in[nm])
                g_half[nm] = _sum_into(g_half.get(nm), rx, own, l, qc_vec, (len(sh),) + sh[0].shape,
                                       f"reduce_sum_{nm}{l}")
        nms = list(g_half)
        seen = jnp.zeros((1, 1), F32)
        for nm, gfull in zip(nms, _join_halves([g_half[nm] for nm in nms], f"reduce_join_{tag}")):
            shape, cols = w_in[nm].shape, gfull.shape[-1]
            dl, mn, vn, gout = _adamw(w_in[nm].reshape(-1, cols), gfull.reshape(-1, cols),
                                      m_in[nm].reshape(-1, cols), v_in[nm].reshape(-1, cols), f"adamw_{nm}",
                                      emit_g=True)
            grad_out[nm], delta_out[nm] = gout.reshape(shape), dl.reshape(shape)
            m_out[nm], v_out[nm] = mn.reshape(shape), vn.reshape(shape)
            seen = seen + dl[0:1, 0:1]
        return seen

    done = reduce_finish(reduce_started[:-1], grad_x[0:1, 0:1] + tok, "rest")
    reduce_finish(reduce_started[-1:], done, "lru")

    small_full_grads = {
        "norm_mix_g": jnp.concatenate(d_norm_mix, axis=0), "norm_ffn_g": jnp.concatenate(d_norm_ffn, axis=0),
        "norm_ple_g": jnp.concatenate(d_norm_ple, axis=0), "final_norm_g": dg_fin[0],
        "a_conv_w": d_a_conv_w[None], "a_conv_b": d_a_conv_b,
        "a_b_gate_r": d_br.reshape(a_b_gate_r.shape[0], a_b_gate_r.shape[1], -1),
        "a_b_gate_i": d_bi.reshape(a_b_gate_i.shape[0], a_b_gate_i.shape[1], -1),
        "a_lambda": d_lam, "b_conv_w": d_b_conv_w[None], "b_conv_b": d_b_conv_b,
        "b_dt_bias": d_dt_bias[:, :n_heads], "b_a_log": d_a_log,
        "b_d_skip": dd_lane.reshape(1, n_heads, SSD_HEAD_DIM).sum(axis=-1), "b_norm_g": d_b_norm_g,
    }
    small_names = list(small_full_grads)
    small_shapes = [small_full_grads[nm].shape for nm in small_names]
    packed = _pack([loss_row] + [small_full_grads[nm] for nm in small_names])
    total = _sum_slots(_gather_all(packed, "gather_small_grads"), "sum_small_grads")
    parts = _unpack(total, [(1, LANE)] + small_shapes)
    loss = parts[0][0, 0]
    g_small = {}
    for nm, gfull in zip(small_names, parts[1:]):
        if nm in small_sharded:
            n_loc = w_in[nm].shape[-1]
            gfull = lax.dynamic_slice_in_dim(gfull, q_idx * n_loc, n_loc, axis=gfull.ndim - 1)
        g_small[nm] = gfull

    sm_shapes = [w_in[nm].shape for nm in small_names]
    dl, mn, vn = _adamw(_pack([w_in[nm] for nm in small_names]), _pack([g_small[nm] for nm in small_names]),
                        _pack([m_in[nm] for nm in small_names]), _pack([v_in[nm] for nm in small_names]),
                        "adamw_small")
    for nm, a, b_, c_ in zip(small_names, _unpack(dl, sm_shapes), _unpack(mn, sm_shapes), _unpack(vn, sm_shapes)):
        grad_out[nm] = g_small[nm].reshape(w_in[nm].shape)
        delta_out[nm], m_out[nm], v_out[nm] = a, b_, c_

    return (loss, grad_x[None], *[grad_out[nm] for nm in names], *[delta_out[nm] for nm in names],
            *[m_out[nm] for nm in names], *[v_out[nm] for nm in names])
```

```python
import functools

import jax
import jax.numpy as jnp
from jax import lax
from jax.experimental import pallas as pl
from jax.experimental.pallas import tpu as pltpu

F32 = jnp.float32
BF16 = jnp.bfloat16
MESH = pl.DeviceIdType.MESH
HIGHEST = lax.Precision.HIGHEST

NORM_EPS = 1e-6
LRU_C = 8.0
CONV_WIDTH = 4
SSD_HEAD_DIM = 64
SSD_STATE = 128
SSD_CHUNK = 128
SSD_HEADS_PER_GROUP = 8
LANE = 128
SUBLANE = 8
N_CHIPS = 4
N_DEV = 8
VMEM_LIMIT = 48 * 1024 * 1024

ADAM_LR = 0.001
ADAM_B1 = 0.9
ADAM_B2 = 0.999
ADAM_EPS = 1e-08
ADAM_WD = 0.01
ADAM_STEP = 10


def _tile(n, cap, mult=LANE):
    best = None
    for t in range(mult, min(n, cap) + 1, mult):
        if n % t == 0:
            best = t
    return best if best is not None else n


def _params(*sem):
    return pltpu.CompilerParams(dimension_semantics=sem, vmem_limit_bytes=VMEM_LIMIT)


def _mm(a, b, *, ta=False, tb=False, add=None, out_dtype=F32, stacked_b=False, stacked_out=0, name):
    if ta:
        kd, m = a.shape
    else:
        m, kd = a.shape
    n_loc = None
    if stacked_b:
        slots, kb, n_loc = b.shape
        if tb:
            n, kb = kb, slots * n_loc
        else:
            n = slots * n_loc
    elif tb:
        n, kb = b.shape
    else:
        kb, n = b.shape
    assert kd == kb, (a.shape, b.shape, ta, tb)
    tm = _tile(m, 1408)
    tn = _tile(n, 1408)
    tk = _tile(kd, 2048)
    if stacked_b and tb:
        tk = _tile(n_loc, 1408)
    elif stacked_b:
        tn = _tile(n_loc, 1408)
    if stacked_out:
        n_loc = n // stacked_out
        tn = _tile(n_loc, 1408)
    nk = kd // tk
    dims = (((0 if ta else 1,), (1 if tb else 0,)), ((), ()))

    def body(*refs):
        a_ref, b_ref = refs[:2]
        add_ref = refs[2] if add is not None else None
        o_ref = refs[3] if add is not None else refs[2]
        bv = b_ref[0] if stacked_b else b_ref[...]
        part = lax.dot_general(a_ref[...].astype(BF16), bv.astype(BF16), dims, preferred_element_type=F32)

        def finish(r):
            if add is not None:
                r = r + add_ref[...]
            if stacked_out:
                o_ref[0] = r.astype(out_dtype)
            else:
                o_ref[...] = r.astype(out_dtype)

        if nk == 1:
            finish(part)
        else:
            acc_ref = refs[-1]
            k = pl.program_id(2)

            @pl.when(k == 0)
            def _():
                acc_ref[...] = part

            @pl.when((k > 0) & (k < nk - 1))
            def _():
                acc_ref[...] += part

            @pl.when(k == nk - 1)
            def _():
                finish(acc_ref[...] + part)

    a_spec = pl.BlockSpec((tk, tm), lambda i, j, k: (k, i)) if ta else pl.BlockSpec((tm, tk), lambda i, j, k: (i, k))
    if stacked_b and tb:
        per = n_loc // tk
        b_spec = pl.BlockSpec((1, tn, tk), lambda i, j, k: (k // per, j, k % per))
    elif stacked_b:
        per = n_loc // tn
        b_spec = pl.BlockSpec((1, tk, tn), lambda i, j, k: (j // per, k, j % per))
    elif tb:
        b_spec = pl.BlockSpec((tn, tk), lambda i, j, k: (j, k))
    else:
        b_spec = pl.BlockSpec((tk, tn), lambda i, j, k: (k, j))
    o_spec = pl.BlockSpec((tm, tn), lambda i, j, k: (i, j))
    in_specs = [a_spec, b_spec] + ([o_spec] if add is not None else [])
    args = (a, b) + ((add,) if add is not None else ())
    if stacked_out:
        per_o = n_loc // tn
        out_spec = pl.BlockSpec((1, tm, tn), lambda i, j, k: (j // per_o, i, j % per_o))
        out_shape = jax.ShapeDtypeStruct((stacked_out, m, n_loc), out_dtype)
    else:
        out_spec, out_shape = o_spec, jax.ShapeDtypeStruct((m, n), out_dtype)
    return pl.pallas_call(
        body, name=name, grid=(m // tm, n // tn, nk), in_specs=in_specs, out_specs=out_spec, out_shape=out_shape,
        scratch_shapes=[pltpu.VMEM((tm, tn), F32)] if nk > 1 else [],
        compiler_params=_params("parallel", "parallel", "arbitrary"))(*args)


def _rowwise(name, body, ins, outs, nrows, ts, ncol=1):
    ts = min(ts, nrows)
    nrow = nrows // ts
    hb = ts // SUBLANE
    nb8 = nrows // SUBLANE
    in_specs, args = [], []
    for kind, arr, cb in ins:
        if kind == "row":
            spec = pl.BlockSpec((ts, cb), lambda j, i: (i, j))
        elif kind == "prev":
            spec = pl.BlockSpec((SUBLANE, cb), lambda j, i: (jnp.maximum(i * hb - 1, 0), j))
        elif kind == "next":
            spec = pl.BlockSpec((SUBLANE, cb), lambda j, i: (jnp.minimum((i + 1) * hb, nb8 - 1), j))
        else:
            spec = pl.BlockSpec((arr.shape[0], cb), lambda j, i: (0, j))
        in_specs.append(spec)
        args.append(arr)
    out_specs, out_shape = [], []
    for kind, rows, ctot, cb, dt in outs:
        if kind == "row":
            out_shape.append(jax.ShapeDtypeStruct((nrows, ctot), dt))
            out_specs.append(pl.BlockSpec((ts, cb), lambda j, i: (i, j)))
        else:
            out_shape.append(jax.ShapeDtypeStruct((rows, ctot), dt))
            out_specs.append(pl.BlockSpec((rows, cb), lambda j, i: (0, j)))

    def kern(*refs):
        body(pl.program_id(1), nrow, *refs)

    return pl.pallas_call(kern, name=name, grid=(ncol, nrow), in_specs=in_specs, out_specs=out_specs,
                          out_shape=out_shape, compiler_params=_params("parallel", "arbitrary"))(*args)


def _colsum(x):
    return jnp.sum(x, axis=0, keepdims=True)


def _acc(i, ref, val):
    @pl.when(i == 0)
    def _():
        ref[...] = val

    @pl.when(i > 0)
    def _():
        ref[...] += val


def _shift_down(x, halo, k):
    xx = jnp.concatenate([halo, x], axis=0)
    return pltpu.roll(xx, k, axis=0)[SUBLANE:, :]


def _shift_up(x, halo, k):
    xx = jnp.concatenate([x, halo], axis=0)
    n = xx.shape[0]
    return pltpu.roll(xx, n - k, axis=0)[: x.shape[0], :]


def _sigmoid(x):
    return 1.0 / (1.0 + jnp.exp(-x))


def _silu(x):
    return x * _sigmoid(x)


def _dsilu(x):
    s = _sigmoid(x)
    return s * (1.0 + x * (1.0 - s))


_GELU_K = 0.7978845608028654
_GELU_C = 0.044715


def _gelu(x):
    return 0.5 * x * (1.0 + jnp.tanh(_GELU_K * (x + _GELU_C * x * x * x)))


def _dgelu(x):
    t = jnp.tanh(_GELU_K * (x + _GELU_C * x * x * x))
    return 0.5 * (1.0 + t) + 0.5 * x * (1.0 - t * t) * _GELU_K * (1.0 + 3.0 * _GELU_C * x * x)


def _softplus(x):
    return jnp.maximum(x, 0.0) + jnp.log1p(jnp.exp(-jnp.abs(x)))


def _neg_expm1(x):
    poly = -x * (1.0 + x * (0.5 + x * (1.0 / 6.0 + x * (1.0 / 24.0 + x * (1.0 / 120.0)))))
    return jnp.where(x > -0.05, poly, 1.0 - jnp.exp(x))


def _rmsnorm_fwd(h, g, name):
    s, d = h.shape

    def body(i, n, h_ref, g_ref, o_ref):
        x = h_ref[...]
        r = lax.rsqrt(jnp.mean(x * x, axis=-1, keepdims=True) + NORM_EPS)
        o_ref[...] = (x * r * g_ref[...]).astype(BF16)

    return _rowwise(name, body, [("row", h, d), ("vec", g, d)], [("row", None, d, d, BF16)], s, 256)[0]


def _rmsnorm_bwd(dn, h, g, dres, name):
    s, d = h.shape

    def body(i, n, dn_ref, h_ref, g_ref, dres_ref, dh_ref, dg_ref):
        x = h_ref[...]
        dy = dn_ref[...].astype(F32)
        r = lax.rsqrt(jnp.mean(x * x, axis=-1, keepdims=True) + NORM_EPS)
        xh = x * r
        _acc(i, dg_ref, _colsum(dy * xh))
        dxh = dy * g_ref[...]
        dh_ref[...] = dres_ref[...] + r * (dxh - xh * jnp.mean(dxh * xh, axis=-1, keepdims=True))

    return _rowwise(name, body, [("row", dn, d), ("row", h, d), ("vec", g, d), ("row", dres, d)],
                    [("row", None, d, d, F32), ("acc", 1, d, d, F32)], s, 256)


def _final_loss_bwd(h, g, tgt):
    s, d = h.shape

    def body(i, n, h_ref, g_ref, t_ref, dh_ref, dg_ref, loss_ref):
        x = h_ref[...]
        gg = g_ref[...]
        r = lax.rsqrt(jnp.mean(x * x, axis=-1, keepdims=True) + NORM_EPS)
        xh = x * r
        err = xh * gg - t_ref[...]
        part = 0.5 * jnp.sum(jnp.mean(err * err, axis=-1, keepdims=True), axis=0, keepdims=True)
        _acc(i, loss_ref, jnp.broadcast_to(part, (1, LANE)))
        dy = err * (1.0 / d)
        _acc(i, dg_ref, _colsum(dy * xh))
        dxh = dy * gg
        dh_ref[...] = r * (dxh - xh * jnp.mean(dxh * xh, axis=-1, keepdims=True))

    return _rowwise("final_loss_bwd", body, [("row", h, d), ("vec", g, d), ("row", tgt, d)],
                    [("row", None, d, d, F32), ("acc", 1, d, d, F32), ("acc", 1, LANE, LANE, F32)], s, 256)


def _conv_rows(x, halo, w, b):
    y = b + w[3:4, :] * x
    for k in range(CONV_WIDTH - 1):
        y = y + w[k:k + 1, :] * _shift_down(x, halo, CONV_WIDTH - 1 - k)
    return y


def _conv_fwd(x, w, b, silu, name):
    s, c = x.shape
    cb = _tile(c, 512)

    def body(i, n, x_ref, p_ref, w_ref, b_ref, o_ref):
        halo = jnp.where(i == 0, 0.0, p_ref[...])
        y = _conv_rows(x_ref[...], halo, w_ref[...], b_ref[...])
        o_ref[...] = _silu(y) if silu else y

    return _rowwise(name, body, [("row", x, cb), ("prev", x, cb), ("vec", w, cb), ("vec", b, cb)],
                    [("row", None, c, cb, F32)], s, 512, ncol=c // cb)[0]


def _silu_conv_bwd_pre(dy, x, w, b, name):
    s, c = x.shape
    cb = _tile(c, 512)

    def body(i, n, dy_ref, x_ref, p_ref, w_ref, b_ref, o_ref):
        halo = jnp.where(i == 0, 0.0, p_ref[...])
        y = _conv_rows(x_ref[...], halo, w_ref[...], b_ref[...])
        o_ref[...] = dy_ref[...] * _dsilu(y)

    return _rowwise(name, body, [("row", dy, cb), ("row", x, cb), ("prev", x, cb), ("vec", w, cb), ("vec", b, cb)],
                    [("row", None, c, cb, F32)], s, 512, ncol=c // cb)[0]


def _conv_bwd(dy, x, w, name):
    s, c = x.shape
    cb = _tile(c, 512)

    def body(i, n, dy_ref, nx_ref, x_ref, p_ref, w_ref, dx_ref, dw_ref, db_ref):
        d = dy_ref[...]
        xx = x_ref[...]
        wv = w_ref[...]
        nxt = jnp.where(i == n - 1, 0.0, nx_ref[...])
        prv = jnp.where(i == 0, 0.0, p_ref[...])
        dx = wv[3:4, :] * d
        parts = []
        for k in range(CONV_WIDTH - 1):
            sh = CONV_WIDTH - 1 - k
            dx = dx + wv[k:k + 1, :] * _shift_up(d, nxt, sh)
            parts.append(_colsum(d * _shift_down(xx, prv, sh)))
        parts.append(_colsum(d * xx))
        dx_ref[...] = dx.astype(BF16)
        _acc(i, dw_ref, jnp.concatenate(parts, axis=0))
        _acc(i, db_ref, _colsum(d))

    return _rowwise(name, body, [("row", dy, cb), ("next", dy, cb), ("row", x, cb), ("prev", x, cb), ("vec", w, cb)],
                    [("row", None, c, cb, BF16), ("acc", CONV_WIDTH, c, cb, F32), ("acc", 1, c, cb, F32)],
                    s, 512, ncol=c // cb)


def _lru_gate_math(xr, r_pre, i_pre, lam):
    r = _sigmoid(r_pre)
    ig = _sigmoid(i_pre)
    sp = _softplus(-lam)
    log_a = -LRU_C * r * sp
    a = jnp.exp(log_a)
    mult = jnp.sqrt(_neg_expm1(2.0 * log_a))
    return r, ig, sp, a, mult


def _lru_gates_fwd(xr, wr, wi, br, bi, lam):
    s, d = xr.shape
    nh, bw, _ = wr.shape
    ts = min(512, s)

    def body(x_ref, wr_ref, wi_ref, br_ref, bi_ref, lam_ref, a_ref, b_ref):
        x = x_ref[...]
        xb = x.astype(BF16)
        r_pre = jnp.dot(xb, wr_ref[0], preferred_element_type=F32) + br_ref[...]
        i_pre = jnp.dot(xb, wi_ref[0], preferred_element_type=F32) + bi_ref[...]
        _, ig, _, a, mult = _lru_gate_math(x, r_pre, i_pre, lam_ref[...])
        a_ref[...] = a
        b_ref[...] = mult * (ig * x)

    row = pl.BlockSpec((ts, bw), lambda h, i: (i, h))
    wsp = pl.BlockSpec((1, bw, bw), lambda h, i: (h, 0, 0))
    vec = pl.BlockSpec((1, bw), lambda h, i: (0, h))
    return pl.pallas_call(
        body, name="lru_gates_fwd", grid=(nh, s // ts), in_specs=[row, wsp, wsp, vec, vec, vec], out_specs=[row, row],
        out_shape=[jax.ShapeDtypeStruct((s, d), F32)] * 2, compiler_params=_params("parallel", "arbitrary"),
    )(xr, wr, wi, br, bi, lam)


def _lru_gates_bwd(xr, g, hs, wr, wi, br, bi, lam):
    s, d = xr.shape
    nh, bw, _ = wr.shape
    ts = min(512, s)
    hb = ts // SUBLANE
    tn_dims = (((0,), (0,)), ((), ()))
    nt_dims = (((1,), (1,)), ((), ()))

    def body(x_ref, g_ref, hs_ref, hp_ref, wr_ref, wi_ref, br_ref, bi_ref, lam_ref,
             dx_ref, dwr_ref, dwi_ref, dbr_ref, dbi_ref, dlam_ref):
        i = pl.program_id(1)
        x = x_ref[...]
        xb = x.astype(BF16)
        gg = g_ref[...]
        lam_v = lam_ref[...]
        r_pre = jnp.dot(xb, wr_ref[0], preferred_element_type=F32) + br_ref[...]
        i_pre = jnp.dot(xb, wi_ref[0], preferred_element_type=F32) + bi_ref[...]
        r, ig, sp, a, mult = _lru_gate_math(x, r_pre, i_pre, lam_v)
        h_prev = _shift_down(hs_ref[...], jnp.where(i == 0, 0.0, hp_ref[...]), 1)
        da = gg * h_prev
        dmult = gg * ig * x
        dlog_a = da * a - dmult * (a * a) / mult
        d_r = dlog_a * (-LRU_C * sp)
        dr_pre = d_r * r * (1.0 - r)
        di_pre = (gg * mult * x) * ig * (1.0 - ig)
        drb = dr_pre.astype(BF16)
        dib = di_pre.astype(BF16)
        dx_ref[...] = (gg * mult * ig
                       + lax.dot_general(drb, wr_ref[0], nt_dims, preferred_element_type=F32)
                       + lax.dot_general(dib, wi_ref[0], nt_dims, preferred_element_type=F32))
        dwr = lax.dot_general(xb, drb, tn_dims, preferred_element_type=F32)[None]
        dwi = lax.dot_general(xb, dib, tn_dims, preferred_element_type=F32)[None]
        dlam = _colsum(dlog_a * (-LRU_C * r)) * (-_sigmoid(-lam_v))
        _acc(i, dwr_ref, dwr)
        _acc(i, dwi_ref, dwi)
        _acc(i, dbr_ref, _colsum(dr_pre))
        _acc(i, dbi_ref, _colsum(di_pre))
        _acc(i, dlam_ref, dlam)

    row = pl.BlockSpec((ts, bw), lambda h, i: (i, h))
    prev = pl.BlockSpec((SUBLANE, bw), lambda h, i: (jnp.maximum(i * hb - 1, 0), h))
    wsp = pl.BlockSpec((1, bw, bw), lambda h, i: (h, 0, 0))
    vec = pl.BlockSpec((1, bw), lambda h, i: (0, h))
    return pl.pallas_call(
        body, name="lru_gates_bwd", grid=(nh, s // ts),
        in_specs=[row, row, row, prev, wsp, wsp, vec, vec, vec], out_specs=[row, wsp, wsp, vec, vec, vec],
        out_shape=[jax.ShapeDtypeStruct((s, d), F32), jax.ShapeDtypeStruct((nh, bw, bw), F32),
                   jax.ShapeDtypeStruct((nh, bw, bw), F32)] + [jax.ShapeDtypeStruct((1, d), F32)] * 3,
        compiler_params=_params("parallel", "arbitrary"),
    )(xr, g, hs, hs, wr, wi, br, bi, lam)


def _scan(a, b, reverse, name):
    s, c = a.shape
    cb = _tile(c, 512)
    nt = s // SUBLANE

    def body(a_ref, b_ref, o_ref):
        row = lax.broadcasted_iota(jnp.int32, (SUBLANE, cb), 0)

        def fwd_step(t, carry):
            r0 = pl.multiple_of(t * SUBLANE, SUBLANE)
            aa = a_ref[pl.ds(r0, SUBLANE), :]
            bb = b_ref[pl.ds(r0, SUBLANE), :]
            for sh in (1, 2, 4):
                a_s = jnp.where(row >= sh, pltpu.roll(aa, sh, axis=0), 1.0)
                b_s = jnp.where(row >= sh, pltpu.roll(bb, sh, axis=0), 0.0)
                bb = aa * b_s + bb
                aa = aa * a_s
            h = bb + aa * carry
            o_ref[pl.ds(r0, SUBLANE), :] = h
            return h[SUBLANE - 1:SUBLANE, :]

        def rev_step(k, carry):
            r0 = pl.multiple_of((nt - 1 - k) * SUBLANE, SUBLANE)
            aa = a_ref[pl.ds(r0, SUBLANE), :]
            dd = b_ref[pl.ds(r0, SUBLANE), :]
            cc = aa * dd
            for sh in (1, 2, 4):
                a_s = jnp.where(row < SUBLANE - sh, pltpu.roll(aa, SUBLANE - sh, axis=0), 1.0)
                c_s = jnp.where(row < SUBLANE - sh, pltpu.roll(cc, SUBLANE - sh, axis=0), 0.0)
                cc = cc + aa * c_s
                aa = aa * a_s
            big = cc + aa * carry
            nxt = jnp.where(row < SUBLANE - 1, pltpu.roll(big, SUBLANE - 1, axis=0), carry)
            o_ref[pl.ds(r0, SUBLANE), :] = dd + nxt
            return big[0:1, :]

        lax.fori_loop(0, nt, rev_step if reverse else fwd_step, jnp.zeros((1, cb), F32))

    spec = pl.BlockSpec((s, cb), lambda j: (0, j))
    return pl.pallas_call(body, name=name, grid=(c // cb,), in_specs=[spec, spec], out_specs=spec,
                          out_shape=jax.ShapeDtypeStruct((s, c), F32), compiler_params=_params("parallel"))(a, b)


def _lru_out_fwd(hs, yg):
    s, d = hs.shape
    cb = _tile(d, 1024)

    def body(i, n, h_ref, y_ref, o_ref):
        o_ref[...] = (h_ref[...] * _gelu(y_ref[...])).astype(BF16)

    return _rowwise("lru_out_fwd", body, [("row", hs, cb), ("row", yg, cb)], [("row", None, d, cb, BF16)],
                    s, 512, ncol=d // cb)[0]


def _lru_out_bwd(dy, hs, yg):
    s, d = hs.shape
    cb = _tile(d, 1024)

    def body(i, n, dy_ref, h_ref, y_ref, dh_ref, dyg_ref):
        dyv = dy_ref[...]
        y = y_ref[...]
        dh_ref[...] = dyv * _gelu(y)
        dyg_ref[...] = (dyv * h_ref[...] * _dgelu(y)).astype(BF16)

    return _rowwise("lru_out_bwd", body, [("row", dy, cb), ("row", hs, cb), ("row", yg, cb)],
                    [("row", None, d, cb, F32), ("row", None, d, cb, BF16)], s, 512, ncol=d // cb)


def _swiglu_act(gt, up):
    s, f = gt.shape
    cb = _tile(f, 1024)

    def body(i, n, g_ref, u_ref, o_ref):
        o_ref[...] = (_silu(g_ref[...].astype(F32)) * u_ref[...].astype(F32)).astype(BF16)

    return _rowwise("swiglu_act", body, [("row", gt, cb), ("row", up, cb)], [("row", None, f, cb, BF16)],
                    s, 512, ncol=f // cb)[0]


def _swiglu_bwd(dact, gt, up):
    s, f = gt.shape
    cb = _tile(f, 1024)

    def body(i, n, d_ref, g_ref, u_ref, dg_ref, du_ref):
        d = d_ref[...]
        g = g_ref[...].astype(F32)
        dg_ref[...] = (d * u_ref[...].astype(F32) * _dsilu(g)).astype(BF16)
        du_ref[...] = (d * _silu(g)).astype(BF16)

    return _rowwise("swiglu_bwd", body, [("row", dact, cb), ("row", gt, cb), ("row", up, cb)],
                    [("row", None, f, cb, BF16), ("row", None, f, cb, BF16)], s, 512, ncol=f // cb)


def _ple_fwd(h, gp, pp):
    s, d = h.shape
    cb = _tile(d, 1024)

    def body(i, n, h_ref, g_ref, p_ref, o_ref):
        o_ref[...] = h_ref[...] + _sigmoid(g_ref[...]) * p_ref[...]

    return _rowwise("ple_fwd", body, [("row", h, cb), ("row", gp, cb), ("row", pp, cb)], [("row", None, d, cb, F32)],
                    s, 512, ncol=d // cb)[0]


def _ple_bwd(dh, gp, pp):
    s, d = dh.shape
    cb = _tile(d, 1024)

    def body(i, n, d_ref, g_ref, p_ref, dg_ref, dp_ref):
        dv = d_ref[...]
        sg = _sigmoid(g_ref[...])
        dg_ref[...] = (dv * p_ref[...] * sg * (1.0 - sg)).astype(BF16)
        dp_ref[...] = (dv * sg).astype(BF16)

    return _rowwise("ple_bwd", body, [("row", dh, cb), ("row", gp, cb), ("row", pp, cb)],
                    [("row", None, d, cb, BF16), ("row", None, d, cb, BF16)], s, 512, ncol=d // cb)


def _group_matrix(n_groups):
    r = lax.broadcasted_iota(jnp.int32, (LANE, n_groups * LANE), 0)
    c = lax.broadcasted_iota(jnp.int32, (LANE, n_groups * LANE), 1)
    return ((c % LANE < SSD_HEADS_PER_GROUP) & (r == (c // LANE) * SSD_HEADS_PER_GROUP + c % LANE)).astype(F32)


def _dt_fwd(dt_pre, bias, n_heads, n_groups):
    s = dt_pre.shape[0]
    gl = n_groups * LANE

    def body(i, n, d_ref, b_ref, o_ref):
        lane = lax.broadcasted_iota(jnp.int32, d_ref.shape, 1)
        v = jnp.where(lane < n_heads, _softplus(d_ref[...] + b_ref[...]), 0.0)
        o_ref[...] = jnp.dot(v, _group_matrix(n_groups), preferred_element_type=F32, precision=HIGHEST)

    return _rowwise("ssd_dt_fwd", body, [("row", dt_pre, LANE), ("vec", bias, LANE)], [("row", None, gl, gl, F32)],
                    s, 512)[0]


def _dt_bwd(ddt_g, dt_pre, bias, n_heads, n_groups):
    s = dt_pre.shape[0]
    gl = n_groups * LANE

    def body(i, n, g_ref, d_ref, b_ref, o_ref, db_ref):
        lane = lax.broadcasted_iota(jnp.int32, d_ref.shape, 1)
        ddt = lax.dot_general(g_ref[...], _group_matrix(n_groups), _NT, preferred_element_type=F32, precision=HIGHEST)
        v = jnp.where(lane < n_heads, ddt * _sigmoid(d_ref[...] + b_ref[...]), 0.0)
        o_ref[...] = v.astype(BF16)
        _acc(i, db_ref, _colsum(v))

    return _rowwise("ssd_dt_bwd", body, [("row", ddt_g, gl), ("row", dt_pre, LANE), ("vec", bias, LANE)],
                    [("row", None, LANE, LANE, BF16), ("acc", 1, LANE, LANE, F32)], s, 512)


def _ssd_chunk_terms(dt, alog):
    ln = dt.shape[0]
    a_neg = -jnp.exp(alog)
    row = lax.broadcasted_iota(jnp.int32, (ln, ln), 0)
    col = lax.broadcasted_iota(jnp.int32, (ln, ln), 1)
    tril = row >= col
    cs = jnp.dot(tril.astype(F32), dt * a_neg, preferred_element_type=F32, precision=HIGHEST)
    return a_neg, cs, tril


def _head_lanes(v):
    return [jnp.broadcast_to(v[:, e:e + 1], v.shape) for e in range(SSD_HEADS_PER_GROUP)]


def _ssd_head_scores(bc_cs, cst, cb_mat, tril, e):
    lm = jnp.where(tril, jnp.exp(jnp.minimum(bc_cs[e] - cst[e:e + 1, :], 0.0)), 0.0)
    return (cb_mat * lm).astype(BF16), lm


_NT = (((1,), (1,)), ((), ()))
_TN = (((0,), (0,)), ((), ()))


def _ssd_fwd(xbc, dt, alog, inner, n_groups):
    s = xbc.shape[0]
    ln = SSD_CHUNK
    nc = s // ln
    gw = SSD_HEADS_PER_GROUP * SSD_HEAD_DIM
    npair = gw // LANE
    boff = inner // LANE

    def body(xs_ref, b_ref, c_ref, dt_ref, alog_ref, y_ref, sin_ref, st_ref):
        c = pl.program_id(1)

        @pl.when(c == 0)
        def _():
            st_ref[...] = jnp.zeros_like(st_ref)

        dtv = dt_ref[...]
        _, cs, tril = _ssd_chunk_terms(dtv, alog_ref[...])
        cst = cs.T
        bc_cs, bc_dt = _head_lanes(cs), _head_lanes(dtv)
        xs = xs_ref[...]
        bg = b_ref[...].astype(BF16)
        cg = c_ref[...].astype(BF16)
        cb_mat = lax.dot_general(cg, bg, _NT, preferred_element_type=F32)
        sg = st_ref[...]
        sin_ref[0] = sg
        lo = lax.broadcasted_iota(jnp.int32, (ln, LANE), 1) < SSD_HEAD_DIM
        ys, news = [], []
        for pr in range(npair):
            cols = slice(LANE * pr, LANE * (pr + 1))
            cs_p = jnp.where(lo, bc_cs[2 * pr], bc_cs[2 * pr + 1])
            x = xs[:, cols] * jnp.where(lo, bc_dt[2 * pr], bc_dt[2 * pr + 1])
            tot_p = cs_p[ln - 1:ln, :]
            xp = x.astype(BF16)
            xd = (x * jnp.exp(tot_p - cs_p)).astype(BF16)
            zero = jnp.zeros_like(xp)
            sc0 = _ssd_head_scores(bc_cs, cst, cb_mat, tril, 2 * pr)[0]
            sc1 = _ssd_head_scores(bc_cs, cst, cb_mat, tril, 2 * pr + 1)[0]
            acc = jnp.dot(sc0, jnp.where(lo, xp, zero), preferred_element_type=F32)
            acc = acc + jnp.dot(sc1, jnp.where(lo, zero, xp), preferred_element_type=F32)
            sp = sg[:, cols]
            yoff = jnp.dot(cg, sp.astype(BF16), preferred_element_type=F32) * jnp.exp(cs_p)
            ys.append(acc + yoff)
            news.append(jnp.exp(tot_p) * sp + lax.dot_general(bg, xd, _TN, preferred_element_type=F32))
        y_ref[...] = jnp.concatenate(ys, axis=1)
        st_ref[...] = jnp.concatenate(news, axis=1)

    in_specs = [pl.BlockSpec((ln, gw), lambda g, c: (c, g)),
                pl.BlockSpec((ln, SSD_STATE), lambda g, c: (c, boff + g)),
                pl.BlockSpec((ln, SSD_STATE), lambda g, c: (c, boff + n_groups + g)),
                pl.BlockSpec((ln, LANE), lambda g, c: (c, g)),
                pl.BlockSpec((1, LANE), lambda g, c: (0, g))]
    out_specs = [pl.BlockSpec((ln, gw), lambda g, c: (c, g)),
                 pl.BlockSpec((1, SSD_STATE, gw), lambda g, c: (c, 0, g))]
    return pl.pallas_call(
        body, name="ssd_fwd", grid=(n_groups, nc), in_specs=in_specs, out_specs=out_specs,
        out_shape=[jax.ShapeDtypeStruct((s, inner), F32), jax.ShapeDtypeStruct((nc, SSD_STATE, inner), F32)],
        scratch_shapes=[pltpu.VMEM((SSD_STATE, gw), F32)],
        compiler_params=_params("parallel", "arbitrary"),
    )(xbc, xbc, xbc, dt, alog)


def _ssd_bwd(xbc, dt, alog, dy, y, sin, dskip_e, inner, n_groups):
    s = xbc.shape[0]
    ln = SSD_CHUNK
    nc = s // ln
    gw = SSD_HEADS_PER_GROUP * SSD_HEAD_DIM
    npair = gw // LANE
    boff = inner // LANE

    def body(xs_ref, b_ref, c_ref, dt_ref, alog_ref, dy_ref, y_ref, sin_ref, sout_ref, dsk_ref,
             dxs_ref, db_ref, dc_ref, ddt_ref, dalog_ref, ds_ref):
        step = pl.program_id(1)

        @pl.when(step == 0)
        def _():
            ds_ref[...] = jnp.zeros_like(ds_ref)

        dtv = dt_ref[...]
        a_neg, cs, tril = _ssd_chunk_terms(dtv, alog_ref[...])
        cst = cs.T
        bc_cs, bc_dt = _head_lanes(cs), _head_lanes(dtv)
        xs = xs_ref[...]
        bg = b_ref[...].astype(BF16)
        cg = c_ref[...].astype(BF16)
        cb_mat = lax.dot_general(cg, bg, _NT, preferred_element_type=F32)
        dyv = dy_ref[...]
        yv = y_ref[...]
        dskv = dsk_ref[...]
        s_in = sin_ref[0]
        s_out = sout_ref[0]
        d_s = ds_ref[...]
        lane = lax.broadcasted_iota(jnp.int32, (ln, LANE), 1)
        rowl = lax.broadcasted_iota(jnp.int32, (ln, LANE), 0)
        lo = lane < SSD_HEAD_DIM
        dcb = jnp.zeros((ln, ln), F32)
        dbg = jnp.zeros((ln, SSD_STATE), F32)
        dcg = jnp.zeros((ln, SSD_STATE), F32)
        dcs = jnp.zeros((ln, LANE), F32)
        ddt_x = jnp.zeros((ln, LANE), F32)
        dxs_parts, nds = [], []

        def head_sums(v, pr, into):
            s0 = jnp.sum(jnp.where(lo, v, 0.0), axis=1, keepdims=True)
            s1 = jnp.sum(jnp.where(lo, 0.0, v), axis=1, keepdims=True)
            return into + jnp.where(lane == 2 * pr, s0, 0.0) + jnp.where(lane == 2 * pr + 1, s1, 0.0)

        for pr in range(npair):
            cols = slice(LANE * pr, LANE * (pr + 1))
            cs_p = jnp.where(lo, bc_cs[2 * pr], bc_cs[2 * pr + 1])
            dt_p = jnp.where(lo, bc_dt[2 * pr], bc_dt[2 * pr + 1])
            xs_p = xs[:, cols]
            x = xs_p * dt_p
            tot_p = cs_p[ln - 1:ln, :]
            dec = jnp.exp(tot_p - cs_p)
            xp = x.astype(BF16)
            xd = (x * dec).astype(BF16)
            dy_p = dyv[:, cols]
            dyp = dy_p.astype(BF16)
            dye = (jnp.exp(cs_p) * dy_p).astype(BF16)
            zero = jnp.zeros_like(dyp)
            dxp = jnp.zeros((ln, LANE), F32)
            for e, dym in ((2 * pr, jnp.where(lo, dyp, zero)), (2 * pr + 1, jnp.where(lo, zero, dyp))):
                sc, lm = _ssd_head_scores(bc_cs, cst, cb_mat, tril, e)
                dsc = lax.dot_general(dym, xp, _NT, preferred_element_type=F32)
                dcb = dcb + dsc * lm
                dxp = dxp + lax.dot_general(sc, dym, _TN, preferred_element_type=F32)
            dsp = d_s[:, cols]
            dspb = dsp.astype(BF16)
            dxp = dxp + dec * jnp.dot(bg, dspb, preferred_element_type=F32)
            dcg = dcg + lax.dot_general(dye, s_in[:, cols].astype(BF16), _NT, preferred_element_type=F32)
            dbg = dbg + lax.dot_general(xd, dspb, _NT, preferred_element_type=F32)
            nds.append(jnp.exp(tot_p) * dsp + lax.dot_general(cg, dye, _TN, preferred_element_type=F32))
            dxs_parts.append(dxp * dt_p + dy_p * dskv[:, cols])
            dcs = head_sums(yv[:, cols] * dyp.astype(F32) - xp.astype(F32) * dxp, pr, dcs)
            tot_row = jnp.broadcast_to(_colsum(s_out[:, cols] * dsp), (ln, LANE))
            dcs = head_sums(jnp.where(rowl == ln - 1, tot_row, 0.0), pr, dcs)
            ddt_x = head_sums(dxp * xs_p, pr, ddt_x)
        ds_ref[...] = jnp.concatenate(nds, axis=1)
        dxs_ref[...] = jnp.concatenate(dxs_parts, axis=1)
        dcbb = dcb.astype(BF16)
        dc_ref[...] = dcg + jnp.dot(dcbb, bg, preferred_element_type=F32)
        db_ref[...] = dbg + lax.dot_general(dcbb, cg, _TN, preferred_element_type=F32)
        row = lax.broadcasted_iota(jnp.int32, (ln, ln), 0)
        col = lax.broadcasted_iota(jnp.int32, (ln, ln), 1)
        dadt = jnp.dot((row <= col).astype(F32), dcs, preferred_element_type=F32, precision=HIGHEST)
        ddt_ref[...] = a_neg * dadt + ddt_x
        _acc(step, dalog_ref, _colsum(dadt * dtv) * a_neg)

    def rc(step):
        return nc - 1 - step

    in_specs = [pl.BlockSpec((ln, gw), lambda g, t: (rc(t), g)),
                pl.BlockSpec((ln, SSD_STATE), lambda g, t: (rc(t), boff + g)),
                pl.BlockSpec((ln, SSD_STATE), lambda g, t: (rc(t), boff + n_groups + g)),
                pl.BlockSpec((ln, LANE), lambda g, t: (rc(t), g)),
                pl.BlockSpec((1, LANE), lambda g, t: (0, g)),
                pl.BlockSpec((ln, gw), lambda g, t: (rc(t), g)),
                pl.BlockSpec((ln, gw), lambda g, t: (rc(t), g)),
                pl.BlockSpec((1, SSD_STATE, gw), lambda g, t: (rc(t), 0, g)),
                pl.BlockSpec((1, SSD_STATE, gw), lambda g, t: (jnp.minimum(rc(t) + 1, nc - 1), 0, g)),
                pl.BlockSpec((1, gw), lambda g, t: (0, g))]
    out_specs = [pl.BlockSpec((ln, gw), lambda g, t: (rc(t), g)),
                 pl.BlockSpec((ln, SSD_STATE), lambda g, t: (rc(t), g)),
                 pl.BlockSpec((ln, SSD_STATE), lambda g, t: (rc(t), g)),
                 pl.BlockSpec((ln, LANE), lambda g, t: (rc(t), g)),
                 pl.BlockSpec((1, LANE), lambda g, t: (0, g))]
    gn = n_groups * SSD_STATE
    return pl.pallas_call(
        body, name="ssd_bwd", grid=(n_groups, nc), in_specs=in_specs, out_specs=out_specs,
        out_shape=[jax.ShapeDtypeStruct((s, inner + 2 * gn), F32), jax.ShapeDtypeStruct((s, gn), F32),
                   jax.ShapeDtypeStruct((s, gn), F32), jax.ShapeDtypeStruct((s, n_groups * LANE), F32),
                   jax.ShapeDtypeStruct((1, n_groups * LANE), F32)],
        scratch_shapes=[pltpu.VMEM((SSD_STATE, gw), F32)],
        compiler_params=_params("parallel", "arbitrary"),
    )(xbc, xbc, xbc, dt, alog, dy, y, sin, sin, dskip_e)


def _put_cols(buf, part, block, name):
    s, w = part.shape
    ts = min(512, s)

    def body(b_ref, p_ref, o_ref):
        o_ref[...] = p_ref[...]

    return pl.pallas_call(
        body, name=name, grid=(s // ts,), in_specs=[_ANY, pl.BlockSpec((ts, w), lambda i: (i, 0))],
        out_specs=pl.BlockSpec((ts, w), lambda i: (i, block)), out_shape=jax.ShapeDtypeStruct(buf.shape, buf.dtype),
        input_output_aliases={0: 0}, compiler_params=_params("parallel"))(buf, part)


def _ssd_gate_norm_fwd(ysc, xbc, z, dskip_e, norm_g, n_groups):
    s, inner = ysc.shape
    gsz = inner // n_groups

    def body(i, n, y_ref, x_ref, z_ref, d_ref, g_ref, o_ref):
        y2 = (y_ref[...] + d_ref[...] * x_ref[...]) * _silu(z_ref[...])
        gg = g_ref[...]
        outs = []
        for k in range(n_groups):
            cols = slice(k * gsz, (k + 1) * gsz)
            v = y2[:, cols]
            r = lax.rsqrt(jnp.mean(v * v, axis=-1, keepdims=True) + NORM_EPS)
            outs.append(v * r * gg[:, cols])
        o_ref[...] = jnp.concatenate(outs, axis=1).astype(BF16)

    return _rowwise("ssd_gate_norm_fwd", body,
                    [("row", ysc, inner), ("row", xbc, inner), ("row", z, inner), ("vec", dskip_e, inner),
                     ("vec", norm_g, inner)], [("row", None, inner, inner, BF16)], s, 128)[0]


def _ssd_gate_norm_bwd(dyn, ysc, xbc, z, dskip_e, norm_g, n_groups):
    s, inner = ysc.shape
    gsz = inner // n_groups

    def body(i, n, dn_ref, y_ref, x_ref, z_ref, d_ref, g_ref, dy_ref, dz_ref, dg_ref, dd_ref):
        xs = x_ref[...]
        zz = z_ref[...]
        y = y_ref[...] + d_ref[...] * xs
        sz = _silu(zz)
        y2 = y * sz
        dn = dn_ref[...]
        gg = g_ref[...]
        dy2s, dgs = [], []
        for k in range(n_groups):
            cols = slice(k * gsz, (k + 1) * gsz)
            v = y2[:, cols]
            d = dn[:, cols]
            r = lax.rsqrt(jnp.mean(v * v, axis=-1, keepdims=True) + NORM_EPS)
            vh = v * r
            dgs.append(_colsum(d * vh))
            dvh = d * gg[:, cols]
            dy2s.append(r * (dvh - vh * jnp.mean(dvh * vh, axis=-1, keepdims=True)))
        dy2 = jnp.concatenate(dy2s, axis=1)
        dy = dy2 * sz
        dy_ref[...] = dy
        dz_ref[...] = (dy2 * y * _dsilu(zz)).astype(BF16)
        _acc(i, dg_ref, jnp.concatenate(dgs, axis=1))
        _acc(i, dd_ref, _colsum(dy * xs))

    return _rowwise("ssd_gate_norm_bwd", body,
                    [("row", dyn, inner), ("row", ysc, inner), ("row", xbc, inner), ("row", z, inner),
                     ("vec", dskip_e, inner), ("vec", norm_g, inner)],
                    [("row", None, inner, inner, F32), ("row", None, inner, inner, BF16),
                     ("acc", 1, inner, inner, F32), ("acc", 1, inner, inner, F32)], s, 128)


def _adamw(w, g, m, v, name, emit_g=False):
    rows, c = w.shape
    bc1 = 1.0 - ADAM_B1 ** ADAM_STEP
    bc2 = 1.0 - ADAM_B2 ** ADAM_STEP

    def body(i, n, w_ref, g_ref, m_ref, v_ref, d_ref, mo_ref, vo_ref, *go_ref):
        gg = g_ref[...]
        if emit_g:
            go_ref[0][...] = gg
        mn = ADAM_B1 * m_ref[...] + (1.0 - ADAM_B1) * gg
        vn = ADAM_B2 * v_ref[...] + (1.0 - ADAM_B2) * (gg * gg)
        d_ref[...] = -ADAM_LR * ((mn / bc1) / (jnp.sqrt(vn / bc2) + ADAM_EPS) + ADAM_WD * w_ref[...])
        mo_ref[...] = mn
        vo_ref[...] = vn

    ts = 128 if rows % 128 == 0 else rows
    return _rowwise(name, body, [("row", w, c), ("row", g, c), ("row", m, c), ("row", v, c)],
                    [("row", None, c, c, F32)] * (4 if emit_g else 3), rows, ts)


_ANY = pl.BlockSpec(memory_space=pl.ANY)


def _place():
    x, y, c = lax.axis_index("x"), lax.axis_index("y"), lax.axis_index("c")
    chips = [(1 - x, y), (x, 1 - y), (1 - x, 1 - y)]
    return x, y, c, chips


def _rcopy(src, dst, ssem, rsem, dev):
    return pltpu.make_async_remote_copy(src_ref=src, dst_ref=dst, send_sem=ssem, recv_sem=rsem, device_id=dev,
                                        device_id_type=MESH)


def _place_shard(shards, layer, q_idx, dtype, name):
    _, r, cc = shards.shape
    tr = _tile(r, 256, 16)

    def body(q_ref, s_ref, o_ref):
        o_ref[0] = s_ref[0].astype(dtype)

    grid_spec = pltpu.PrefetchScalarGridSpec(
        num_scalar_prefetch=1, grid=(r // tr,),
        in_specs=[pl.BlockSpec((1, tr, cc), lambda i, q_ref: (layer, i, 0))],
        out_specs=pl.BlockSpec((1, tr, cc), lambda i, q_ref: (q_ref[0], i, 0)))
    return pl.pallas_call(body, name=name, grid_spec=grid_spec, out_shape=jax.ShapeDtypeStruct((N_CHIPS, r, cc), dtype),
                          compiler_params=_params("parallel"))(q_idx, shards)


_HBM = pl.BlockSpec(memory_space=pltpu.HBM)
_SEM = pl.BlockSpec(memory_space=pltpu.SEMAPHORE)
_EFFECT = pltpu.SideEffectType.DATAFLOW_SIDE_EFFECTING


def _in_hbm(arrs):
    return [pltpu.with_memory_space_constraint(a, pltpu.HBM) for a in arrs]


def _gather_start(bufs, name):
    n = len(bufs)
    half = [e.shape[1] // 2 for e in bufs]

    def body(*refs):
        ins, send, recv, token = refs[:n], refs[n], refs[n + 1], refs[2 * n + 2]
        x, y, c, chips = _place()
        q = 2 * x + y
        for e in range(n):
            blk = ins[e].at[q, pl.ds(c * half[e], half[e])]
            for j, (cx, cy) in enumerate(chips):
                _rcopy(blk, blk, send.at[3 * e + j], recv.at[3 * e + j], (cx, cy, c)).start()
        token[...] = jnp.zeros_like(token)

    out = pl.pallas_call(
        body, name=name,
        out_shape=(pltpu.SemaphoreType.DMA((3 * n,)), pltpu.SemaphoreType.DMA((3 * n,)),
                   *[pltpu.HBM(b.shape, b.dtype) for b in bufs], jax.ShapeDtypeStruct((SUBLANE, LANE), F32)),
        in_specs=[_HBM] * n, out_specs=(_SEM, _SEM, *[_HBM] * n, pl.BlockSpec(memory_space=pltpu.VMEM)),
        input_output_aliases={e: 2 + e for e in range(n)},
        compiler_params=pltpu.CompilerParams(has_side_effects=_EFFECT))(*_in_hbm(bufs))
    return out[0], out[1], list(out[2:2 + n]), out[2 + n]


def _gather_wait(send, recv, bufs, after, name):
    n = len(bufs)
    half = [e.shape[1] // 2 for e in bufs]

    def body(*refs):
        ins, send_ref, recv_ref = refs[:n], refs[n], refs[n + 1]
        x, y, c, chips = _place()
        q = 2 * x + y
        for e in range(n):
            rows = pl.ds(c * half[e], half[e])
            for j, (cx, cy) in enumerate(chips):
                cp = _rcopy(ins[e].at[q, rows], ins[e].at[2 * cx + cy, rows], send_ref.at[3 * e + j],
                            recv_ref.at[3 * e + j], (cx, cy, c))
                cp.wait_send()
                cp.wait_recv()

    return list(pl.pallas_call(
        body, name=name, out_shape=tuple(pltpu.HBM(b.shape, b.dtype) for b in bufs),
        in_specs=[_HBM] * n + [_SEM, _SEM, _ANY], out_specs=[_HBM] * n,
        input_output_aliases={e: e for e in range(n)},
        compiler_params=pltpu.CompilerParams(has_side_effects=_EFFECT))(*bufs, send, recv, after))


def _forward_sibling(bufs, name):
    n = len(bufs)
    half = [e.shape[1] // 2 for e in bufs]

    def body(*refs):
        outs = refs[n:2 * n]
        send, recv = refs[2 * n:]
        x, y, c, chips = _place()
        sib = (x, y, 1 - c)
        cps = []
        for e in range(n):
            for j, (cx, cy) in enumerate(chips):
                blk = outs[e].at[2 * cx + cy, pl.ds(c * half[e], half[e])]
                cps.append(_rcopy(blk, blk, send.at[3 * e + j], recv.at[3 * e + j], sib))
        for cp in cps:
            cp.start()
        for e in range(n):
            for j, (cx, cy) in enumerate(chips):
                blk = outs[e].at[2 * cx + cy, pl.ds((1 - c) * half[e], half[e])]
                _rcopy(blk, blk, send.at[3 * e + j], recv.at[3 * e + j], sib).wait_recv()
        for cp in cps:
            cp.wait_send()

    return list(pl.pallas_call(
        body, name=name, in_specs=[_ANY] * n, out_specs=[_ANY] * n,
        out_shape=[jax.ShapeDtypeStruct(e.shape, e.dtype) for e in bufs],
        input_output_aliases={e: e for e in range(n)},
        scratch_shapes=[pltpu.SemaphoreType.DMA((3 * n,))] * 2,
    )(*bufs))


def _scatter_start(entries, name):
    n = len(entries)
    lands = [lax.empty(e.shape, e.dtype) for e in entries]

    def body(*refs):
        ins, land, send, recv, token = refs[:n], refs[n:2 * n], refs[2 * n], refs[2 * n + 1], refs[4 * n + 2]
        x, y, c, chips = _place()
        q = 2 * x + y
        for e in range(n):
            for j, (cx, cy) in enumerate(chips):
                _rcopy(ins[e].at[2 * cx + cy], land[e].at[q], send.at[3 * e + j], recv.at[3 * e + j],
                       (cx, cy, c)).start()
        token[...] = jnp.zeros_like(token)

    out = pl.pallas_call(
        body, name=name,
        out_shape=(pltpu.SemaphoreType.DMA((3 * n,)), pltpu.SemaphoreType.DMA((3 * n,)),
                   *[pltpu.HBM(b.shape, b.dtype) for b in entries + lands],
                   jax.ShapeDtypeStruct((SUBLANE, LANE), F32)),
        in_specs=[_HBM] * (2 * n),
        out_specs=(_SEM, _SEM, *[_HBM] * (2 * n), pl.BlockSpec(memory_space=pltpu.VMEM)),
        input_output_aliases={e: 2 + e for e in range(2 * n)},
        compiler_params=pltpu.CompilerParams(has_side_effects=_EFFECT))(*_in_hbm(entries + lands))
    return out[0], out[1], list(out[2:2 + n]), list(out[2 + n:2 + 2 * n]), out[2 + 2 * n]


def _scatter_wait(send, recv, entries, lands, after, name):
    n = len(entries)

    def body(*refs):
        ins, land, send_ref, recv_ref = refs[:n], refs[n:2 * n], refs[2 * n], refs[2 * n + 1]
        x, y, c, chips = _place()
        for e in range(n):
            for j, (cx, cy) in enumerate(chips):
                k = 2 * cx + cy
                cp = _rcopy(ins[e].at[k], land[e].at[k], send_ref.at[3 * e + j], recv_ref.at[3 * e + j],
                            (cx, cy, c))
                cp.wait_send()
                cp.wait_recv()

    out = pl.pallas_call(
        body, name=name, out_shape=tuple(pltpu.HBM(b.shape, b.dtype) for b in entries + lands),
        in_specs=[_HBM] * (2 * n) + [_SEM, _SEM, _ANY], out_specs=[_HBM] * (2 * n),
        input_output_aliases={e: e for e in range(2 * n)},
        compiler_params=pltpu.CompilerParams(has_side_effects=_EFFECT))(*entries, *lands, send, recv, after)
    return list(out[:n]), list(out[n:])


def _swap_halves(entries, name):
    n = len(entries)
    half = [e.shape[1] // 2 for e in entries]

    def body(*refs):
        ins, outs = refs[:n], refs[n:2 * n]
        send, recv = refs[2 * n:]
        x, y, c, _ = _place()
        cps = [_rcopy(ins[e].at[:, pl.ds((1 - c) * half[e], half[e]), :], outs[e], send.at[e], recv.at[e],
                      (x, y, 1 - c)) for e in range(n)]
        for cp in cps:
            cp.start()
        for cp in cps:
            cp.wait()

    return pl.pallas_call(
        body, name=name, in_specs=[_ANY] * n, out_specs=[_ANY] * n,
        out_shape=[jax.ShapeDtypeStruct((N_CHIPS, h, e.shape[2]), e.dtype) for e, h in zip(entries, half)],
        scratch_shapes=[pltpu.SemaphoreType.DMA((n,))] * 2,
    )(*entries)


def _join_halves(bufs, name):
    n = len(bufs)
    pairs = [(o, layer) for o in range(n) for layer in range(bufs[o].shape[0])]
    npair = len(pairs)

    def body(*refs):
        outs = refs[n:2 * n]
        send, recv = refs[2 * n:]
        x, y, c, _ = _place()
        cps = []
        for k, (o, layer) in enumerate(pairs):
            r2 = bufs[o].shape[1] // 2
            blk = outs[o].at[layer, pl.ds(c * r2, r2)]
            cps.append(_rcopy(blk, blk, send.at[k], recv.at[k], (x, y, 1 - c)))
        for cp in cps:
            cp.start()
        for k, (o, layer) in enumerate(pairs):
            r2 = bufs[o].shape[1] // 2
            blk = outs[o].at[layer, pl.ds((1 - c) * r2, r2)]
            _rcopy(blk, blk, send.at[k], recv.at[k], (x, y, 1 - c)).wait_recv()
        for cp in cps:
            cp.wait_send()

    return pl.pallas_call(
        body, name=name, in_specs=[_ANY] * n, out_specs=[_ANY] * n,
        out_shape=[jax.ShapeDtypeStruct(b.shape, b.dtype) for b in bufs],
        input_output_aliases={e: e for e in range(n)},
        scratch_shapes=[pltpu.SemaphoreType.DMA((npair,))] * 2,
    )(*bufs)


def _gather_all(v, name):
    def body(v_ref, o_ref, send, recv, loc):
        x, y, c, _ = _place()
        me = 4 * x + 2 * y + c
        mine = pltpu.make_async_copy(v_ref, o_ref.at[me], loc)
        mine.start()
        peers = []
        for k in range(1, N_DEV):
            px = 1 - x if k & 4 else x
            py = 1 - y if k & 2 else y
            pc = 1 - c if k & 1 else c
            peers.append((px, py, pc))
        cps = [_rcopy(v_ref, o_ref.at[me], send.at[k], recv.at[k], peers[k]) for k in range(N_DEV - 1)]
        for cp in cps:
            cp.start()
        for k, (px, py, pc) in enumerate(peers):
            blk = o_ref.at[4 * px + 2 * py + pc]
            _rcopy(blk, blk, send.at[k], recv.at[k], (px, py, pc)).wait_recv()
        for cp in cps:
            cp.wait_send()
        mine.wait()

    return pl.pallas_call(
        body, name=name, in_specs=[_ANY], out_specs=_ANY, out_shape=jax.ShapeDtypeStruct((N_DEV,) + v.shape, v.dtype),
        scratch_shapes=[pltpu.SemaphoreType.DMA((N_DEV - 1,))] * 2 + [pltpu.SemaphoreType.DMA],
    )(v)


def _add_own_half(gst, rx, c_idx, name):
    _, r, cc = gst.shape
    r2 = r // 2
    tr = _tile(r2, 256, 16)
    g4 = gst.reshape(N_CHIPS, 2, r2, cc)

    def body(c_ref, g_ref, r_ref, o_ref):
        o_ref[...] = (g_ref[0].astype(F32) + r_ref[...].astype(F32)).astype(BF16)

    grid_spec = pltpu.PrefetchScalarGridSpec(
        num_scalar_prefetch=1, grid=(N_CHIPS, r2 // tr),
        in_specs=[pl.BlockSpec((1, 1, tr, cc), lambda k, i, c_ref: (k, c_ref[0], i, 0)),
                  pl.BlockSpec((1, tr, cc), lambda k, i, c_ref: (k, i, 0))],
        out_specs=pl.BlockSpec((1, tr, cc), lambda k, i, c_ref: (k, i, 0)))
    return pl.pallas_call(body, name=name, grid_spec=grid_spec, out_shape=jax.ShapeDtypeStruct((N_CHIPS, r2, cc), BF16),
                          compiler_params=_params("parallel", "parallel"))(c_idx, g4, rx)


def _sum_into(buf, rx, own, layer, qc, out_shape, name):
    _, r2, cc = rx.shape
    tr = _tile(r2, 256, 16)
    nb = r2 // tr

    def body(qc_ref, *refs):
        rx_ref, own_ref, o_ref = refs[-3:]
        q = qc_ref[0]
        acc = None
        for k in range(N_CHIPS):
            v = jnp.where(q == k, own_ref[0], rx_ref[k]).astype(F32)
            acc = v if acc is None else acc + v
        o_ref[0] = acc

    in_specs = [pl.BlockSpec((N_CHIPS, tr, cc), lambda i, qc_ref: (0, i, 0)),
                pl.BlockSpec((1, tr, cc), lambda i, qc_ref: (qc_ref[0], i, 0))]
    args = (rx, own)
    aliases = {}
    if buf is not None:
        in_specs = [_ANY] + in_specs
        args = (buf,) + args
        aliases = {1: 0}
    grid_spec = pltpu.PrefetchScalarGridSpec(
        num_scalar_prefetch=1, grid=(nb,), in_specs=in_specs,
        out_specs=pl.BlockSpec((1, tr, cc), lambda i, qc_ref: (layer, qc_ref[1] * nb + i, 0)))
    return pl.pallas_call(body, name=name, grid_spec=grid_spec, out_shape=jax.ShapeDtypeStruct(out_shape, F32),
                          input_output_aliases=aliases, compiler_params=_params("parallel"))(qc, *args)


def _sum_slots(st, name):
    k, r, cc = st.shape
    tr = _tile(r, 256, 8)

    def body(s_ref, o_ref):
        acc = s_ref[0].astype(F32)
        for j in range(1, k):
            acc = acc + s_ref[j].astype(F32)
        o_ref[...] = acc

    return pl.pallas_call(body, name=name, grid=(r // tr,), in_specs=[pl.BlockSpec((k, tr, cc), lambda i: (0, i, 0))],
                          out_specs=pl.BlockSpec((tr, cc), lambda i: (i, 0)),
                          out_shape=jax.ShapeDtypeStruct((r, cc), F32), compiler_params=_params("parallel"))(st)


def _pack(arrs, rows_mult=2 * SUBLANE):
    flat = jnp.concatenate([a.reshape(-1).astype(F32) for a in arrs])
    quantum = rows_mult * LANE
    padded = -(-flat.shape[0] // quantum) * quantum
    return jnp.pad(flat, (0, padded - flat.shape[0])).reshape(-1, LANE)


def _unpack(buf, shapes):
    flat = buf.reshape(-1)
    out, off = [], 0
    for sh in shapes:
        size = 1
        for d in sh:
            size *= d
        out.append(flat[off:off + size].reshape(sh))
        off += size
    return out


def kernel(x, p, norm_mix_g, norm_ffn_g, norm_ple_g, final_norm_g, a_w_in, a_conv_w, a_conv_b, a_w_gate_r, a_b_gate_r, a_w_gate_i, a_b_gate_i, a_lambda, a_w_out, b_w_in, b_conv_w, b_conv_b, b_dt_bias, b_a_log, b_d_skip, b_norm_g, b_w_out, ffn_w_gate, ffn_w_up, ffn_w_down, ple_w_proj, ple_w_gate, loss_target, m_norm_mix_g, m_norm_ffn_g, m_norm_ple_g, m_final_norm_g, m_a_w_in, m_a_conv_w, m_a_conv_b, m_a_w_gate_r, m_a_b_gate_r, m_a_w_gate_i, m_a_b_gate_i, m_a_lambda, m_a_w_out, m_b_w_in, m_b_conv_w, m_b_conv_b, m_b_dt_bias, m_b_a_log, m_b_d_skip, m_b_norm_g, m_b_w_out, m_ffn_w_gate, m_ffn_w_up, m_ffn_w_down, m_ple_w_proj, m_ple_w_gate, v_norm_mix_g, v_norm_ffn_g, v_norm_ple_g, v_final_norm_g, v_a_w_in, v_a_conv_w, v_a_conv_b, v_a_w_gate_r, v_a_b_gate_r, v_a_w_gate_i, v_a_b_gate_i, v_a_lambda, v_a_w_out, v_b_w_in, v_b_conv_w, v_b_conv_b, v_b_dt_bias, v_b_a_log, v_b_d_skip, v_b_norm_g, v_b_w_out, v_ffn_w_gate, v_ffn_w_up, v_ffn_w_down, v_ple_w_proj, v_ple_w_gate):
    names = ["norm_mix_g", "norm_ffn_g", "norm_ple_g", "final_norm_g", "a_w_in", "a_conv_w", "a_conv_b", "a_w_gate_r",
             "a_b_gate_r", "a_w_gate_i", "a_b_gate_i", "a_lambda", "a_w_out", "b_w_in", "b_conv_w", "b_conv_b",
             "b_dt_bias", "b_a_log", "b_d_skip", "b_norm_g", "b_w_out", "ffn_w_gate", "ffn_w_up", "ffn_w_down",
             "ple_w_proj", "ple_w_gate"]
    w_in = dict(zip(names, [norm_mix_g, norm_ffn_g, norm_ple_g, final_norm_g, a_w_in, a_conv_w, a_conv_b, a_w_gate_r,
                            a_b_gate_r, a_w_gate_i, a_b_gate_i, a_lambda, a_w_out, b_w_in, b_conv_w, b_conv_b,
                            b_dt_bias, b_a_log, b_d_skip, b_norm_g, b_w_out, ffn_w_gate, ffn_w_up, ffn_w_down,
                            ple_w_proj, ple_w_gate]))
    m_in = dict(zip(names, [m_norm_mix_g, m_norm_ffn_g, m_norm_ple_g, m_final_norm_g, m_a_w_in, m_a_conv_w,
                            m_a_conv_b, m_a_w_gate_r, m_a_b_gate_r, m_a_w_gate_i, m_a_b_gate_i, m_a_lambda,
                            m_a_w_out, m_b_w_in, m_b_conv_w, m_b_conv_b, m_b_dt_bias, m_b_a_log, m_b_d_skip,
                            m_b_norm_g, m_b_w_out, m_ffn_w_gate, m_ffn_w_up, m_ffn_w_down, m_ple_w_proj,
                            m_ple_w_gate]))
    v_in = dict(zip(names, [v_norm_mix_g, v_norm_ffn_g, v_norm_ple_g, v_final_norm_g, v_a_w_in, v_a_conv_w,
                            v_a_conv_b, v_a_w_gate_r, v_a_b_gate_r, v_a_w_gate_i, v_a_b_gate_i, v_a_lambda,
                            v_a_w_out, v_b_w_in, v_b_conv_w, v_b_conv_b, v_b_dt_bias, v_b_a_log, v_b_d_skip,
                            v_b_norm_g, v_b_w_out, v_ffn_w_gate, v_ffn_w_up, v_ffn_w_down, v_ple_w_proj,
                            v_ple_w_gate]))

    s, d = x.shape[1], x.shape[2]
    depth = norm_mix_g.shape[0]
    assert depth == 2
    q_idx = 2 * lax.axis_index("x") + lax.axis_index("y")
    c_idx = lax.axis_index("c").astype(jnp.int32).reshape(1)

    inner = b_w_out.shape[1] * N_CHIPS
    n_heads = inner // SSD_HEAD_DIM
    n_groups = n_heads // SSD_HEADS_PER_GROUP
    gn = n_groups * SSD_STATE
    xbcw = inner + 2 * gn
    assert b_conv_w.shape[2] * N_CHIPS == xbcw and n_heads <= LANE

    big = [("a_w_in", "col"), ("a_w_gate_r", "gate"), ("a_w_gate_i", "gate"), ("a_w_out", "row"),
           ("ffn_w_gate", "col"), ("ffn_w_up", "col"), ("ffn_w_down", "row"), ("ple_w_proj", "col"),
           ("ple_w_gate", "row"), ("b_w_in", "col"), ("b_w_out", "row")]
    kind_of = dict(big)

    def shard2d(name, arr):
        if kind_of[name] == "gate":
            return [arr[l].reshape(-1, arr.shape[-1]) for l in range(arr.shape[0])]
        return [arr[l] for l in range(arr.shape[0])]

    small_sharded = ["a_conv_w", "a_b_gate_r", "a_b_gate_i", "b_conv_w", "b_conv_b", "b_norm_g"]
    small_pack = _pack([w_in[nm] for nm in small_sharded], rows_mult=16)

    q_vec = q_idx.astype(jnp.int32).reshape(1)
    qc_vec = jnp.stack([q_idx, lax.axis_index("c")]).astype(jnp.int32)
    gather_groups = [
        [("a_w_in", 0), ("a_w_gate_r", 0), ("a_w_gate_i", 0), ("a_w_out", 0), ("small", 0)],
        [("ffn_w_gate", 0), ("ffn_w_up", 0)],
        [("ffn_w_down", 0), ("ple_w_proj", 0), ("ple_w_gate", 0)],
        [("b_w_in", 0), ("b_w_out", 0)],
        [("ffn_w_gate", 1), ("ffn_w_up", 1), ("ffn_w_down", 1), ("ple_w_proj", 1), ("ple_w_gate", 1)],
    ]
    gather_started = [None] * len(gather_groups)

    def gather_begin(gi):
        bufs = []
        for nm, l in gather_groups[gi]:
            if nm == "small":
                bufs.append(_place_shard(small_pack[None], 0, q_vec, F32, "place_small"))
            else:
                arr = w_in[nm]
                arr = arr.reshape(arr.shape[0], -1, arr.shape[-1]) if kind_of[nm] == "gate" else arr
                bufs.append(_place_shard(arr, l, q_vec, BF16, f"place_{nm}{l}"))
        gather_started[gi] = _gather_start(bufs, f"gather_start{gi}")
        return gather_started[gi][3][0:1, 0:1]

    wst = {}

    def gather_finish(gi, after):
        send, recv, thru, _ = gather_started[gi]
        landed = _gather_wait(send, recv, thru, after, f"gather_wait{gi}")
        for k, arr in zip(gather_groups[gi], _forward_sibling(landed, f"gather_forward{gi}")):
            wst[k] = arr

    gather_finish(0, sum(gather_begin(gi) for gi in range(len(gather_groups))))
    small_st = wst[("small", 0)]
    kept_stacked = ("a_w_in", "ffn_w_gate", "ffn_w_up", "ple_w_proj")

    def whole(nm, l):
        st = wst[(nm, l)]
        kind = kind_of[nm]
        if nm in kept_stacked:
            return st
        if kind == "row":
            return st.reshape(-1, st.shape[-1])
        if kind == "col":
            return jnp.concatenate([st[k] for k in range(N_CHIPS)], axis=1)
        heads = w_in[nm].shape[1]
        return st.reshape(N_CHIPS, heads, -1, st.shape[-1]).transpose(1, 0, 2, 3).reshape(heads, -1, st.shape[-1])

    small_parts = [_unpack(small_st[k], [w_in[nm].shape for nm in small_sharded]) for k in range(N_CHIPS)]
    small_full = {nm: jnp.concatenate([small_parts[k][i] for k in range(N_CHIPS)], axis=-1)
                  for i, nm in enumerate(small_sharded)}
    a_cw = small_full["a_conv_w"][0]
    a_br = small_full["a_b_gate_r"][0].reshape(1, -1)
    a_bi = small_full["a_b_gate_i"][0].reshape(1, -1)
    b_cw = small_full["b_conv_w"][0]
    b_cb = small_full["b_conv_b"]
    b_ng = small_full["b_norm_g"]

    def pad_lanes(v):
        return jnp.pad(v, ((0, 0), (0, LANE - v.shape[1])))

    dt_bias = pad_lanes(b_dt_bias)
    a_log = pad_lanes(b_a_log.reshape(n_groups, SSD_HEADS_PER_GROUP)).reshape(1, n_groups * LANE)
    dskip_e = jnp.repeat(b_d_skip, SSD_HEAD_DIM, axis=1)

    w_a_in = whole("a_w_in", 0)
    w_ax, w_ay = w_a_in[:N_CHIPS // 2], w_a_in[N_CHIPS // 2:]
    w_ar, w_ai, w_ao = whole("a_w_gate_r", 0), whole("a_w_gate_i", 0), whole("a_w_out", 0)
    w_fg, w_fu, w_fd, w_pp, w_pg = ([None] * depth for _ in range(5))

    def take_ffn_in(l):
        w_fg[l], w_fu[l] = whole("ffn_w_gate", l), whole("ffn_w_up", l)

    def take_ffn_out_ple(l):
        w_fd[l], w_pp[l], w_pg[l] = whole("ffn_w_down", l), whole("ple_w_proj", l), whole("ple_w_gate", l)

    grads = {}

    h0 = x[0]
    g_mix = [norm_mix_g[l:l + 1] for l in range(depth)]
    g_ffn = [norm_ffn_g[l:l + 1] for l in range(depth)]
    g_ple = [norm_ple_g[l:l + 1] for l in range(depth)]
    g_fin = final_norm_g.reshape(1, -1)

    u0 = _rmsnorm_fwd(h0, g_mix[0], "norm_mix0")
    xr_pre = _mm(u0, w_ax, stacked_b=True, name="lru_in_x")
    yg = _mm(u0, w_ay, stacked_b=True, name="lru_in_y")
    xr = _conv_fwd(xr_pre, a_cw, a_conv_b, False, "lru_conv")
    lru_a, lru_b = _lru_gates_fwd(xr, w_ar, w_ai, a_br, a_bi, a_lambda)
    hs = _scan(lru_a, lru_b, False, "lru_scan")
    y_lru = _lru_out_fwd(hs, yg)
    h_mix = [_mm(y_lru, w_ao, add=h0, name="lru_out"), None]

    def ffn_ple_fwd(h_in, l, before_down=None):
        n_f = _rmsnorm_fwd(h_in, g_ffn[l], f"norm_ffn{l}")
        gt = _mm(n_f, w_fg[l], stacked_b=True, out_dtype=BF16, name=f"ffn_gate{l}")
        up = _mm(n_f, w_fu[l], stacked_b=True, out_dtype=BF16, name=f"ffn_up{l}")
        act = _swiglu_act(gt, up)
        if before_down is not None:
            before_down(act)
        h_f = _mm(act, w_fd[l], add=h_in, name=f"ffn_down{l}")
        n_p = _rmsnorm_fwd(h_f, g_ple[l], f"norm_ple{l}")
        gp = _mm(n_p, w_pg[l], name=f"ple_gate{l}")
        pp = _mm(p[l, 0], w_pp[l], stacked_b=True, name=f"ple_proj{l}")
        h_out = _ple_fwd(h_f, gp, pp)
        return h_out, dict(h_in=h_in, n_f=n_f, gt=gt, up=up, act=act, h_f=h_f, n_p=n_p, gp=gp, pp=pp)

    gather_finish(1, h_mix[0])
    take_ffn_in(0)

    def finish_ffn_out_ple0(act):
        gather_finish(2, act)
        take_ffn_out_ple(0)

    h_l0, sv0 = ffn_ple_fwd(h_mix[0], 0, finish_ffn_out_ple0)

    gather_finish(3, h_l0)
    w_b_in = whole("b_w_in", 0)
    w_bz, w_bx = w_b_in[:, :inner], w_b_in[:, inner:inner + xbcw]
    w_bd = pad_lanes(w_b_in[:, inner + xbcw:])
    w_bo = whole("b_w_out", 0)
    u1 = _rmsnorm_fwd(h_l0, g_mix[1], "norm_mix1")
    z = _mm(u1, w_bz, name="ssd_in_z")
    xbc_pre = _mm(u1, w_bx, name="ssd_in_xbc")
    dt_pre = _mm(u1, w_bd, name="ssd_in_dt")
    xbc = _conv_fwd(xbc_pre, b_cw, b_cb, True, "ssd_conv")
    dt = _dt_fwd(dt_pre, dt_bias, n_heads, n_groups)
    ysc, s_in = _ssd_fwd(xbc, dt, a_log, inner, n_groups)
    yn = _ssd_gate_norm_fwd(ysc, xbc, z, dskip_e, b_ng, n_groups)
    h_mix[1] = _mm(yn, w_bo, add=h_l0, name="ssd_out")
    gather_finish(4, h_mix[1])
    take_ffn_in(1)
    take_ffn_out_ple(1)
    h_l1, sv1 = ffn_ple_fwd(h_mix[1], 1)

    dh, dg_fin, loss_row = _final_loss_bwd(h_l1, g_fin, loss_target[0])

    d_norm_ffn, d_norm_ple, d_norm_mix = [None] * depth, [None] * depth, [None] * depth
    for nm in ("ffn_w_gate", "ffn_w_up", "ffn_w_down", "ple_w_proj", "ple_w_gate"):
        grads[nm] = [None] * depth

    def stacked(nm, gfull):
        kind = kind_of[nm]
        if nm in kept_stacked:
            return gfull
        if kind == "row":
            return gfull.reshape(N_CHIPS, -1, gfull.shape[-1])
        if kind == "col":
            n_loc = gfull.shape[1] // N_CHIPS
            return jnp.stack([gfull[:, k * n_loc:(k + 1) * n_loc] for k in range(N_CHIPS)])
        heads, bw, _ = gfull.shape
        return gfull.reshape(heads, N_CHIPS, bw // N_CHIPS, bw).transpose(1, 0, 2, 3).reshape(N_CHIPS, -1, bw)

    reduce_started = []

    def reduce_start(keys, tag):
        gst = [stacked(nm, grads[nm][l]) for nm, l in keys]
        from_sib = _swap_halves(gst, f"reduce_swap_{tag}")
        chip_sum = [_add_own_half(g, r, c_idx, f"reduce_add_{nm}{l}") for g, r, (nm, l) in zip(gst, from_sib, keys)]
        send, recv, ents, lands, token = _scatter_start(chip_sum, f"reduce_scatter_start_{tag}")
        reduce_started.append((keys, tag, send, recv, ents, lands))
        return token[0:1, 0:1]

    def ffn_ple_keys(l):
        return [("ple_w_gate", l), ("ple_w_proj", l), ("ffn_w_down", l), ("ffn_w_gate", l), ("ffn_w_up", l)]

    def ffn_ple_bwd(dh_out, sv, l, g_ple_l):
        dgp, dpp = _ple_bwd(dh_out, sv["gp"], sv["pp"])
        grads["ple_w_gate"][l] = _mm(sv["n_p"], dgp, ta=True, out_dtype=BF16, name=f"ple_gate_dw{l}")
        grads["ple_w_proj"][l] = _mm(p[l, 0], dpp, ta=True, out_dtype=BF16, stacked_out=N_CHIPS,
                                     name=f"ple_proj_dw{l}")
        dn = _mm(dgp, w_pg[l], tb=True, name=f"ple_gate_dx{l}")
        dh_f, d_norm_ple[l] = _rmsnorm_bwd(dn, sv["h_f"], g_ple_l, dh_out, f"norm_ple_bwd{l}")
        grads["ffn_w_down"][l] = _mm(sv["act"], dh_f, ta=True, out_dtype=BF16, name=f"ffn_down_dw{l}")
        dact = _mm(dh_f, w_fd[l], tb=True, name=f"ffn_down_dx{l}")
        dgt, dup = _swiglu_bwd(dact, sv["gt"], sv["up"])
        grads["ffn_w_gate"][l] = _mm(sv["n_f"], dgt, ta=True, out_dtype=BF16, stacked_out=N_CHIPS,
                                     name=f"ffn_gate_dw{l}")
        grads["ffn_w_up"][l] = _mm(sv["n_f"], dup, ta=True, out_dtype=BF16, stacked_out=N_CHIPS, name=f"ffn_up_dw{l}")
        dn = _mm(dgt, w_fg[l], tb=True, stacked_b=True, name=f"ffn_gate_dx{l}")
        dn = _mm(dup, w_fu[l], tb=True, stacked_b=True, add=dn, name=f"ffn_up_dx{l}")
        dh_in, d_norm_ffn[l] = _rmsnorm_bwd(dn, sv["h_in"], g_ffn[l], dh_f, f"norm_ffn_bwd{l}")
        return dh_in

    dh = ffn_ple_bwd(dh, sv1, 1, g_ple[1])
    tok = reduce_start(ffn_ple_keys(1), "l1")

    grads["b_w_out"] = [_mm(yn, dh, ta=True, out_dtype=BF16, name="ssd_out_dw")]
    dyn = _mm(dh, w_bo, tb=True, name="ssd_out_dx")
    dy_ssd, dz, d_b_norm_g, dd_lane = _ssd_gate_norm_bwd(dyn, ysc, xbc, z, dskip_e, b_ng + tok, n_groups)
    dxs, d_bm, d_cm, ddt, d_a_log = _ssd_bwd(xbc, dt, a_log, dy_ssd, ysc, s_in, dskip_e, inner, n_groups)
    dxbc = _put_cols(dxs, d_bm, inner // gn, "ssd_put_db")
    dxbc = _put_cols(dxbc, d_cm, inner // gn + 1, "ssd_put_dc")
    dconv = _silu_conv_bwd_pre(dxbc, xbc_pre, b_cw, b_cb, "ssd_conv_bwd_pre")
    dxbc_pre, d_b_conv_w, d_b_conv_b = _conv_bwd(dconv, xbc_pre, b_cw, "ssd_conv_bwd")
    ddt_pre, d_dt_bias = _dt_bwd(ddt, dt_pre, dt_bias, n_heads, n_groups)
    d_a_log = d_a_log.reshape(n_groups, LANE)[:, :SSD_HEADS_PER_GROUP].reshape(1, n_heads)
    gw_bz = _mm(u1, dz, ta=True, out_dtype=BF16, name="ssd_in_z_dw")
    gw_bx = _mm(u1, dxbc_pre, ta=True, out_dtype=BF16, name="ssd_in_xbc_dw")
    gw_bd = _mm(u1, ddt_pre, ta=True, out_dtype=BF16, name="ssd_in_dt_dw")
    grads["b_w_in"] = [jnp.concatenate([gw_bz, gw_bx, gw_bd[:, :n_heads]], axis=1)]
    du = _mm(dz, w_bz, tb=True, name="ssd_in_z_dx")
    du = _mm(dxbc_pre, w_bx, tb=True, add=du, name="ssd_in_xbc_dx")
    du = _mm(ddt_pre, w_bd, tb=True, add=du, name="ssd_in_dt_dx")
    dh, d_norm_mix[1] = _rmsnorm_bwd(du, h_l0, g_mix[1], dh, "norm_mix_bwd1")
    tok = reduce_start([("b_w_out", 0), ("b_w_in", 0)], "ssd")

    dh = ffn_ple_bwd(dh, sv0, 0, g_ple[0] + tok)
    tok = reduce_start(ffn_ple_keys(0), "l0")

    grads["a_w_out"] = [_mm(y_lru, dh, ta=True, out_dtype=BF16, name="lru_out_dw")]
    dy_lru = _mm(dh, w_ao, tb=True, name="lru_out_dx")
    dhs, dyg = _lru_out_bwd(dy_lru, hs, yg)
    g_scan = _scan(lru_a, dhs, True, "lru_scan_bwd")
    dxr, d_wr, d_wi, d_br, d_bi, d_lam = _lru_gates_bwd(xr, g_scan, hs, w_ar, w_ai, a_br, a_bi, a_lambda + tok)
    dxr_pre, d_a_conv_w, d_a_conv_b = _conv_bwd(dxr, xr_pre, a_cw, "lru_conv_bwd")
    gw_ax = _mm(u0, dxr_pre, ta=True, out_dtype=BF16, stacked_out=N_CHIPS // 2, name="lru_in_x_dw")
    gw_ay = _mm(u0, dyg, ta=True, out_dtype=BF16, stacked_out=N_CHIPS // 2, name="lru_in_y_dw")
    grads["a_w_in"] = [jnp.concatenate([gw_ax, gw_ay], axis=0)]
    grads["a_w_gate_r"] = [d_wr.astype(BF16)]
    grads["a_w_gate_i"] = [d_wi.astype(BF16)]
    du = _mm(dxr_pre, w_ax, tb=True, stacked_b=True, name="lru_in_x_dx")
    du = _mm(dyg, w_ay, tb=True, stacked_b=True, add=du, name="lru_in_y_dx")
    grad_x, d_norm_mix[0] = _rmsnorm_bwd(du, h0, g_mix[0], dh, "norm_mix_bwd0")

    tok = reduce_start([("a_w_out", 0), ("a_w_in", 0), ("a_w_gate_r", 0), ("a_w_gate_i", 0)], "lru")
    grad_out, delta_out, m_out, v_out = {}, {}, {}, {}

    def reduce_finish(groups, after, tag):
        g_half = {}
        for keys, gtag, send, recv, ents, lands in groups:
            ents, lands = _scatter_wait(send, recv, ents, lands, after, f"reduce_scatter_wait_{gtag}")
            for rx, own, (nm, l) in zip(lands, ents, keys):
                sh = shard2d(nm, w_in[nm])
                g_half[nm] = _sum_into(g_half.get(nm), rx, own, l, qc_vec, (len(sh),) + sh[0].shape,
                                       f"reduce_sum_{nm}{l}")
        nms = list(g_half)
        seen = jnp.zeros((1, 1), F32)
        for nm, gfull in zip(nms, _join_halves([g_half[nm] for nm in nms], f"reduce_join_{tag}")):
            shape, cols = w_in[nm].shape, gfull.shape[-1]
            dl, mn, vn, gout = _adamw(w_in[nm].reshape(-1, cols), gfull.reshape(-1, cols),
                                      m_in[nm].reshape(-1, cols), v_in[nm].reshape(-1, cols), f"adamw_{nm}",
                                      emit_g=True)
            grad_out[nm], delta_out[nm] = gout.reshape(shape), dl.reshape(shape)
            m_out[nm], v_out[nm] = mn.reshape(shape), vn.reshape(shape)
            seen = seen + dl[0:1, 0:1]
        return seen

    done = reduce_finish(reduce_started[:-1], grad_x[0:1, 0:1] + tok, "rest")
    reduce_finish(reduce_started[-1:], done, "lru")

    small_full_grads = {
        "norm_mix_g": jnp.concatenate(d_norm_mix, axis=0), "norm_ffn_g": jnp.concatenate(d_norm_ffn, axis=0),
        "norm_ple_g": jnp.concatenate(d_norm_ple, axis=0), "final_norm_g": dg_fin[0],
        "a_conv_w": d_a_conv_w[None], "a_conv_b": d_a_conv_b,
        "a_b_gate_r": d_br.reshape(a_b_gate_r.shape[0], a_b_gate_r.shape[1], -1),
        "a_b_gate_i": d_bi.reshape(a_b_gate_i.shape[0], a_b_gate_i.shape[1], -1),
        "a_lambda": d_lam, "b_conv_w": d_b_conv_w[None], "b_conv_b": d_b_conv_b,
        "b_dt_bias": d_dt_bias[:, :n_heads], "b_a_log": d_a_log,
        "b_d_skip": dd_lane.reshape(1, n_heads, SSD_HEAD_DIM).sum(axis=-1), "b_norm_g": d_b_norm_g,
    }
    small_names = list(small_full_grads)
    small_shapes = [small_full_grads[nm].shape for nm in small_names]
    packed = _pack([loss_row] + [small_full_grads[nm] for nm in small_names])
    total = _sum_slots(_gather_all(packed, "gather_small_grads"), "sum_small_grads")
    parts = _unpack(total, [(1, LANE)] + small_shapes)
    loss = parts[0][0, 0]
    g_small = {}
    for nm, gfull in zip(small_names, parts[1:]):
        if nm in small_sharded:
            n_loc = w_in[nm].shape[-1]
            gfull = lax.dynamic_slice_in_dim(gfull, q_idx * n_loc, n_loc, axis=gfull.ndim - 1)
        g_small[nm] = gfull

    sm_shapes = [w_in[nm].shape for nm in small_names]
    dl, mn, vn = _adamw(_pack([w_in[nm] for nm in small_names]), _pack([g_small[nm] for nm in small_names]),
                        _pack([m_in[nm] for nm in small_names]), _pack([v_in[nm] for nm in small_names]),
                        "adamw_small")
    for nm, a, b_, c_ in zip(small_names, _unpack(dl, sm_shapes), _unpack(mn, sm_shapes), _unpack(vn, sm_shapes)):
        grad_out[nm] = g_small[nm].reshape(w_in[nm].shape)
        delta_out[nm], m_out[nm], v_out[nm] = a, b_, c_

    return (loss, grad_x[None], *[grad_out[nm] for nm in names], *[delta_out[nm] for nm in names],
            *[m_out[nm] for nm in names], *[v_out[nm] for nm in names])
```

```python
import functools

import jax
import jax.numpy as jnp
from jax import lax
from jax.experimental import pallas as pl
from jax.experimental.pallas import tpu as pltpu

F32 = jnp.float32
BF16 = jnp.bfloat16
MESH = pl.DeviceIdType.MESH
HIGHEST = lax.Precision.HIGHEST

NORM_EPS = 1e-6
LRU_C = 8.0
CONV_WIDTH = 4
SSD_HEAD_DIM = 64
SSD_STATE = 128
SSD_CHUNK = 128
SSD_HEADS_PER_GROUP = 8
LANE = 128
SUBLANE = 8
N_CHIPS = 4
N_DEV = 8
VMEM_LIMIT = 48 * 1024 * 1024

ADAM_LR = 0.001
ADAM_B1 = 0.9
ADAM_B2 = 0.999
ADAM_EPS = 1e-08
ADAM_WD = 0.01
ADAM_STEP = 10


def _tile(n, cap, mult=LANE):
    best = None
    for t in range(mult, min(n, cap) + 1, mult):
        if n % t == 0:
            best = t
    return best if best is not None else n


def _params(*sem):
    return pltpu.CompilerParams(dimension_semantics=sem, vmem_limit_bytes=VMEM_LIMIT)


def _mm(a, b, *, ta=False, tb=False, add=None, out_dtype=F32, stacked_b=False, stacked_out=0, name):
    if ta:
        kd, m = a.shape
    else:
        m, kd = a.shape
    n_loc = None
    if stacked_b:
        slots, kb, n_loc = b.shape
        if tb:
            n, kb = kb, slots * n_loc
        else:
            n = slots * n_loc
    elif tb:
        n, kb = b.shape
    else:
        kb, n = b.shape
    assert kd == kb, (a.shape, b.shape, ta, tb)
    tm = _tile(m, 1408)
    tn = _tile(n, 1408)
    tk = _tile(kd, 2048)
    if stacked_b and tb:
        tk = _tile(n_loc, 1408)
    elif stacked_b:
        tn = _tile(n_loc, 1408)
    if stacked_out:
        n_loc = n // stacked_out
        tn = _tile(n_loc, 1408)
    nk = kd // tk
    dims = (((0 if ta else 1,), (1 if tb else 0,)), ((), ()))

    def body(*refs):
        a_ref, b_ref = refs[:2]
        add_ref = refs[2] if add is not None else None
        o_ref = refs[3] if add is not None else refs[2]
        bv = b_ref[0] if stacked_b else b_ref[...]
        part = lax.dot_general(a_ref[...].astype(BF16), bv.astype(BF16), dims, preferred_element_type=F32)

        def finish(r):
            if add is not None:
                r = r + add_ref[...]
            if stacked_out:
                o_ref[0] = r.astype(out_dtype)
            else:
                o_ref[...] = r.astype(out_dtype)

        if nk == 1:
            finish(part)
        else:
            acc_ref = refs[-1]
            k = pl.program_id(2)

            @pl.when(k == 0)
            def _():
                acc_ref[...] = part

            @pl.when((k > 0) & (k < nk - 1))
            def _():
                acc_ref[...] += part

            @pl.when(k == nk - 1)
            def _():
                finish(acc_ref[...] + part)

    a_spec = pl.BlockSpec((tk, tm), lambda i, j, k: (k, i)) if ta else pl.BlockSpec((tm, tk), lambda i, j, k: (i, k))
    if stacked_b and tb:
        per = n_loc // tk
        b_spec = pl.BlockSpec((1, tn, tk), lambda i, j, k: (k // per, j, k % per))
    elif stacked_b:
        per = n_loc // tn
        b_spec = pl.BlockSpec((1, tk, tn), lambda i, j, k: (j // per, k, j % per))
    elif tb:
        b_spec = pl.BlockSpec((tn, tk), lambda i, j, k: (j, k))
    else:
        b_spec = pl.BlockSpec((tk, tn), lambda i, j, k: (k, j))
    o_spec = pl.BlockSpec((tm, tn), lambda i, j, k: (i, j))
    in_specs = [a_spec, b_spec] + ([o_spec] if add is not None else [])
    args = (a, b) + ((add,) if add is not None else ())
    if stacked_out:
        per_o = n_loc // tn
        out_spec = pl.BlockSpec((1, tm, tn), lambda i, j, k: (j // per_o, i, j % per_o))
        out_shape = jax.ShapeDtypeStruct((stacked_out, m, n_loc), out_dtype)
    else:
        out_spec, out_shape = o_spec, jax.ShapeDtypeStruct((m, n), out_dtype)
    return pl.pallas_call(
        body, name=name, grid=(m // tm, n // tn, nk), in_specs=in_specs, out_specs=out_spec, out_shape=out_shape,
        scratch_shapes=[pltpu.VMEM((tm, tn), F32)] if nk > 1 else [],
        compiler_params=_params("parallel", "parallel", "arbitrary"))(*args)


def _rowwise(name, body, ins, outs, nrows, ts, ncol=1):
    ts = min(ts, nrows)
    nrow = nrows // ts
    hb = ts // SUBLANE
    nb8 = nrows // SUBLANE
    in_specs, args = [], []
    for kind, arr, cb in ins:
        if kind == "row":
            spec = pl.BlockSpec((ts, cb), lambda j, i: (i, j))
        elif kind == "prev":
            spec = pl.BlockSpec((SUBLANE, cb), lambda j, i: (jnp.maximum(i * hb - 1, 0), j))
        elif kind == "next":
            spec = pl.BlockSpec((SUBLANE, cb), lambda j, i: (jnp.minimum((i + 1) * hb, nb8 - 1), j))
        else:
            spec = pl.BlockSpec((arr.shape[0], cb), lambda j, i: (0, j))
        in_specs.append(spec)
        args.append(arr)
    out_specs, out_shape = [], []
    for kind, rows, ctot, cb, dt in outs:
        if kind == "row":
            out_shape.append(jax.ShapeDtypeStruct((nrows, ctot), dt))
            out_specs.append(pl.BlockSpec((ts, cb), lambda j, i: (i, j)))
        else:
            out_shape.append(jax.ShapeDtypeStruct((rows, ctot), dt))
            out_specs.append(pl.BlockSpec((rows, cb), lambda j, i: (0, j)))

    def kern(*refs):
        body(pl.program_id(1), nrow, *refs)

    return pl.pallas_call(kern, name=name, grid=(ncol, nrow), in_specs=in_specs, out_specs=out_specs,
                          out_shape=out_shape, compiler_params=_params("parallel", "arbitrary"))(*args)


def _colsum(x):
    return jnp.sum(x, axis=0, keepdims=True)


def _acc(i, ref, val):
    @pl.when(i == 0)
    def _():
        ref[...] = val

    @pl.when(i > 0)
    def _():
        ref[...] += val


def _shift_down(x, halo, k):
    xx = jnp.concatenate([halo, x], axis=0)
    return pltpu.roll(xx, k, axis=0)[SUBLANE:, :]


def _shift_up(x, halo, k):
    xx = jnp.concatenate([x, halo], axis=0)
    n = xx.shape[0]
    return pltpu.roll(xx, n - k, axis=0)[: x.shape[0], :]


def _sigmoid(x):
    return 1.0 / (1.0 + jnp.exp(-x))


def _silu(x):
    return x * _sigmoid(x)


def _dsilu(x):
    s = _sigmoid(x)
    return s * (1.0 + x * (1.0 - s))


_GELU_K = 0.7978845608028654
_GELU_C = 0.044715


def _gelu(x):
    return 0.5 * x * (1.0 + jnp.tanh(_GELU_K * (x + _GELU_C * x * x * x)))


def _dgelu(x):
    t = jnp.tanh(_GELU_K * (x + _GELU_C * x * x * x))
    return 0.5 * (1.0 + t) + 0.5 * x * (1.0 - t * t) * _GELU_K * (1.0 + 3.0 * _GELU_C * x * x)


def _softplus(x):
    return jnp.maximum(x, 0.0) + jnp.log1p(jnp.exp(-jnp.abs(x)))


def _neg_expm1(x):
    poly = -x * (1.0 + x * (0.5 + x * (1.0 / 6.0 + x * (1.0 / 24.0 + x * (1.0 / 120.0)))))
    return jnp.where(x > -0.05, poly, 1.0 - jnp.exp(x))


def _rmsnorm_fwd(h, g, name):
    s, d = h.shape

    def body(i, n, h_ref, g_ref, o_ref):
        x = h_ref[...]
        r = lax.rsqrt(jnp.mean(x * x, axis=-1, keepdims=True) + NORM_EPS)
        o_ref[...] = (x * r * g_ref[...]).astype(BF16)

    return _rowwise(name, body, [("row", h, d), ("vec", g, d)], [("row", None, d, d, BF16)], s, 256)[0]


def _rmsnorm_bwd(dn, h, g, dres, name):
    s, d = h.shape

    def body(i, n, dn_ref, h_ref, g_ref, dres_ref, dh_ref, dg_ref):
        x = h_ref[...]
        dy = dn_ref[...].astype(F32)
        r = lax.rsqrt(jnp.mean(x * x, axis=-1, keepdims=True) + NORM_EPS)
        xh = x * r
        _acc(i, dg_ref, _colsum(dy * xh))
        dxh = dy * g_ref[...]
        dh_ref[...] = dres_ref[...] + r * (dxh - xh * jnp.mean(dxh * xh, axis=-1, keepdims=True))

    return _rowwise(name, body, [("row", dn, d), ("row", h, d), ("vec", g, d), ("row", dres, d)],
                    [("row", None, d, d, F32), ("acc", 1, d, d, F32)], s, 256)


def _final_loss_bwd(h, g, tgt):
    s, d = h.shape

    def body(i, n, h_ref, g_ref, t_ref, dh_ref, dg_ref, loss_ref):
        x = h_ref[...]
        gg = g_ref[...]
        r = lax.rsqrt(jnp.mean(x * x, axis=-1, keepdims=True) + NORM_EPS)
        xh = x * r
        err = xh * gg - t_ref[...]
        part = 0.5 * jnp.sum(jnp.mean(err * err, axis=-1, keepdims=True), axis=0, keepdims=True)
        _acc(i, loss_ref, jnp.broadcast_to(part, (1, LANE)))
        dy = err * (1.0 / d)
        _acc(i, dg_ref, _colsum(dy * xh))
        dxh = dy * gg
        dh_ref[...] = r * (dxh - xh * jnp.mean(dxh * xh, axis=-1, keepdims=True))

    return _rowwise("final_loss_bwd", body, [("row", h, d), ("vec", g, d), ("row", tgt, d)],
                    [("row", None, d, d, F32), ("acc", 1, d, d, F32), ("acc", 1, LANE, LANE, F32)], s, 256)


def _conv_rows(x, halo, w, b):
    y = b + w[3:4, :] * x
    for k in range(CONV_WIDTH - 1):
        y = y + w[k:k + 1, :] * _shift_down(x, halo, CONV_WIDTH - 1 - k)
    return y


def _conv_fwd(x, w, b, silu, name):
    s, c = x.shape
    cb = _tile(c, 512)

    def body(i, n, x_ref, p_ref, w_ref, b_ref, o_ref):
        halo = jnp.where(i == 0, 0.0, p_ref[...])
        y = _conv_rows(x_ref[...], halo, w_ref[...], b_ref[...])
        o_ref[...] = _silu(y) if silu else y

    return _rowwise(name, body, [("row", x, cb), ("prev", x, cb), ("vec", w, cb), ("vec", b, cb)],
                    [("row", None, c, cb, F32)], s, 512, ncol=c // cb)[0]


def _silu_conv_bwd_pre(dy, x, w, b, name):
    s, c = x.shape
    cb = _tile(c, 512)

    def body(i, n, dy_ref, x_ref, p_ref, w_ref, b_ref, o_ref):
        halo = jnp.where(i == 0, 0.0, p_ref[...])
        y = _conv_rows(x_ref[...], halo, w_ref[...], b_ref[...])
        o_ref[...] = dy_ref[...] * _dsilu(y)

    return _rowwise(name, body, [("row", dy, cb), ("row", x, cb), ("prev", x, cb), ("vec", w, cb), ("vec", b, cb)],
                    [("row", None, c, cb, F32)], s, 512, ncol=c // cb)[0]


def _conv_bwd(dy, x, w, name):
    s, c = x.shape
    cb = _tile(c, 512)

    def body(i, n, dy_ref, nx_ref, x_ref, p_ref, w_ref, dx_ref, dw_ref, db_ref):
        d = dy_ref[...]
        xx = x_ref[...]
        wv = w_ref[...]
        nxt = jnp.where(i == n - 1, 0.0, nx_ref[...])
        prv = jnp.where(i == 0, 0.0, p_ref[...])
        dx = wv[3:4, :] * d
        parts = []
        for k in range(CONV_WIDTH - 1):
            sh = CONV_WIDTH - 1 - k
            dx = dx + wv[k:k + 1, :] * _shift_up(d, nxt, sh)
            parts.append(_colsum(d * _shift_down(xx, prv, sh)))
        parts.append(_colsum(d * xx))
        dx_ref[...] = dx.astype(BF16)
        _acc(i, dw_ref, jnp.concatenate(parts, axis=0))
        _acc(i, db_ref, _colsum(d))

    return _rowwise(name, body, [("row", dy, cb), ("next", dy, cb), ("row", x, cb), ("prev", x, cb), ("vec", w, cb)],
                    [("row", None, c, cb, BF16), ("acc", CONV_WIDTH, c, cb, F32), ("acc", 1, c, cb, F32)],
                    s, 512, ncol=c // cb)


def _lru_gate_math(xr, r_pre, i_pre, lam):
    r = _sigmoid(r_pre)
    ig = _sigmoid(i_pre)
    sp = _softplus(-lam)
    log_a = -LRU_C * r * sp
    a = jnp.exp(log_a)
    mult = jnp.sqrt(_neg_expm1(2.0 * log_a))
    return r, ig, sp, a, mult


def _lru_gates_fwd(xr, wr, wi, br, bi, lam):
    s, d = xr.shape
    nh, bw, _ = wr.shape
    ts = min(512, s)

    def body(x_ref, wr_ref, wi_ref, br_ref, bi_ref, lam_ref, a_ref, b_ref):
        x = x_ref[...]
        xb = x.astype(BF16)
        r_pre = jnp.dot(xb, wr_ref[0], preferred_element_type=F32) + br_ref[...]
        i_pre = jnp.dot(xb, wi_ref[0], preferred_element_type=F32) + bi_ref[...]
        _, ig, _, a, mult = _lru_gate_math(x, r_pre, i_pre, lam_ref[...])
        a_ref[...] = a
        b_ref[...] = mult * (ig * x)

    row = pl.BlockSpec((ts, bw), lambda h, i: (i, h))
    wsp = pl.BlockSpec((1, bw, bw), lambda h, i: (h, 0, 0))
    vec = pl.BlockSpec((1, bw), lambda h, i: (0, h))
    return pl.pallas_call(
        body, name="lru_gates_fwd", grid=(nh, s // ts), in_specs=[row, wsp, wsp, vec, vec, vec], out_specs=[row, row],
        out_shape=[jax.ShapeDtypeStruct((s, d), F32)] * 2, compiler_params=_params("parallel", "arbitrary"),
    )(xr, wr, wi, br, bi, lam)


def _lru_gates_bwd(xr, g, hs, wr, wi, br, bi, lam):
    s, d = xr.shape
    nh, bw, _ = wr.shape
    ts = min(512, s)
    hb = ts // SUBLANE
    tn_dims = (((0,), (0,)), ((), ()))
    nt_dims = (((1,), (1,)), ((), ()))

    def body(x_ref, g_ref, hs_ref, hp_ref, wr_ref, wi_ref, br_ref, bi_ref, lam_ref,
             dx_ref, dwr_ref, dwi_ref, dbr_ref, dbi_ref, dlam_ref):
        i = pl.program_id(1)
        x = x_ref[...]
        xb = x.astype(BF16)
        gg = g_ref[...]
        lam_v = lam_ref[...]
        r_pre = jnp.dot(xb, wr_ref[0], preferred_element_type=F32) + br_ref[...]
        i_pre = jnp.dot(xb, wi_ref[0], preferred_element_type=F32) + bi_ref[...]
        r, ig, sp, a, mult = _lru_gate_math(x, r_pre, i_pre, lam_v)
        h_prev = _shift_down(hs_ref[...], jnp.where(i == 0, 0.0, hp_ref[...]), 1)
        da = gg * h_prev
        dmult = gg * ig * x
        dlog_a = da * a - dmult * (a * a) / mult
        d_r = dlog_a * (-LRU_C * sp)
        dr_pre = d_r * r * (1.0 - r)
        di_pre = (gg * mult * x) * ig * (1.0 - ig)
        drb = dr_pre.astype(BF16)
        dib = di_pre.astype(BF16)
        dx_ref[...] = (gg * mult * ig
                       + lax.dot_general(drb, wr_ref[0], nt_dims, preferred_element_type=F32)
                       + lax.dot_general(dib, wi_ref[0], nt_dims, preferred_element_type=F32))
        dwr = lax.dot_general(xb, drb, tn_dims, preferred_element_type=F32)[None]
        dwi = lax.dot_general(xb, dib, tn_dims, preferred_element_type=F32)[None]
        dlam = _colsum(dlog_a * (-LRU_C * r)) * (-_sigmoid(-lam_v))
        _acc(i, dwr_ref, dwr)
        _acc(i, dwi_ref, dwi)
        _acc(i, dbr_ref, _colsum(dr_pre))
        _acc(i, dbi_ref, _colsum(di_pre))
        _acc(i, dlam_ref, dlam)

    row = pl.BlockSpec((ts, bw), lambda h, i: (i, h))
    prev = pl.BlockSpec((SUBLANE, bw), lambda h, i: (jnp.maximum(i * hb - 1, 0), h))
    wsp = pl.BlockSpec((1, bw, bw), lambda h, i: (h, 0, 0))
    vec = pl.BlockSpec((1, bw), lambda h, i: (0, h))
    return pl.pallas_call(
        body, name="lru_gates_bwd", grid=(nh, s // ts),
        in_specs=[row, row, row, prev, wsp, wsp, vec, vec, vec], out_specs=[row, wsp, wsp, vec, vec, vec],
        out_shape=[jax.ShapeDtypeStruct((s, d), F32), jax.ShapeDtypeStruct((nh, bw, bw), F32),
                   jax.ShapeDtypeStruct((nh, bw, bw), F32)] + [jax.ShapeDtypeStruct((1, d), F32)] * 3,
        compiler_params=_params("parallel", "arbitrary"),
    )(xr, g, hs, hs, wr, wi, br, bi, lam)


def _scan(a, b, reverse, name):
    s, c = a.shape
    cb = _tile(c, 512)
    nt = s // SUBLANE

    def body(a_ref, b_ref, o_ref):
        row = lax.broadcasted_iota(jnp.int32, (SUBLANE, cb), 0)

        def fwd_step(t, carry):
            r0 = pl.multiple_of(t * SUBLANE, SUBLANE)
            aa = a_ref[pl.ds(r0, SUBLANE), :]
            bb = b_ref[pl.ds(r0, SUBLANE), :]
            for sh in (1, 2, 4):
                a_s = jnp.where(row >= sh, pltpu.roll(aa, sh, axis=0), 1.0)
                b_s = jnp.where(row >= sh, pltpu.roll(bb, sh, axis=0), 0.0)
                bb = aa * b_s + bb
                aa = aa * a_s
            h = bb + aa * carry
            o_ref[pl.ds(r0, SUBLANE), :] = h
            return h[SUBLANE - 1:SUBLANE, :]

        def rev_step(k, carry):
            r0 = pl.multiple_of((nt - 1 - k) * SUBLANE, SUBLANE)
            aa = a_ref[pl.ds(r0, SUBLANE), :]
            dd = b_ref[pl.ds(r0, SUBLANE), :]
            cc = aa * dd
            for sh in (1, 2, 4):
                a_s = jnp.where(row < SUBLANE - sh, pltpu.roll(aa, SUBLANE - sh, axis=0), 1.0)
                c_s = jnp.where(row < SUBLANE - sh, pltpu.roll(cc, SUBLANE - sh, axis=0), 0.0)
                cc = cc + aa * c_s
                aa = aa * a_s
            big = cc + aa * carry
            nxt = jnp.where(row < SUBLANE - 1, pltpu.roll(big, SUBLANE - 1, axis=0), carry)
            o_ref[pl.ds(r0, SUBLANE), :] = dd + nxt
            return big[0:1, :]

        lax.fori_loop(0, nt, rev_step if reverse else fwd_step, jnp.zeros((1, cb), F32))

    spec = pl.BlockSpec((s, cb), lambda j: (0, j))
    return pl.pallas_call(body, name=name, grid=(c // cb,), in_specs=[spec, spec], out_specs=spec,
                          out_shape=jax.ShapeDtypeStruct((s, c), F32), compiler_params=_params("parallel"))(a, b)


def _lru_out_fwd(hs, yg):
    s, d = hs.shape
    cb = _tile(d, 1024)

    def body(i, n, h_ref, y_ref, o_ref):
        o_ref[...] = (h_ref[...] * _gelu(y_ref[...])).astype(BF16)

    return _rowwise("lru_out_fwd", body, [("row", hs, cb), ("row", yg, cb)], [("row", None, d, cb, BF16)],
                    s, 512, ncol=d // cb)[0]


def _lru_out_bwd(dy, hs, yg):
    s, d = hs.shape
    cb = _tile(d, 1024)

    def body(i, n, dy_ref, h_ref, y_ref, dh_ref, dyg_ref):
        dyv = dy_ref[...]
        y = y_ref[...]
        dh_ref[...] = dyv * _gelu(y)
        dyg_ref[...] = (dyv * h_ref[...] * _dgelu(y)).astype(BF16)

    return _rowwise("lru_out_bwd", body, [("row", dy, cb), ("row", hs, cb), ("row", yg, cb)],
                    [("row", None, d, cb, F32), ("row", None, d, cb, BF16)], s, 512, ncol=d // cb)


def _swiglu_act(gt, up):
    s, f = gt.shape
    cb = _tile(f, 1024)

    def body(i, n, g_ref, u_ref, o_ref):
        o_ref[...] = (_silu(g_ref[...].astype(F32)) * u_ref[...].astype(F32)).astype(BF16)

    return _rowwise("swiglu_act", body, [("row", gt, cb), ("row", up, cb)], [("row", None, f, cb, BF16)],
                    s, 512, ncol=f // cb)[0]


def _swiglu_bwd(dact, gt, up):
    s, f = gt.shape
    cb = _tile(f, 1024)

    def body(i, n, d_ref, g_ref, u_ref, dg_ref, du_ref):
        d = d_ref[...]
        g = g_ref[...].astype(F32)
        dg_ref[...] = (d * u_ref[...].astype(F32) * _dsilu(g)).astype(BF16)
        du_ref[...] = (d * _silu(g)).astype(BF16)

    return _rowwise("swiglu_bwd", body, [("row", dact, cb), ("row", gt, cb), ("row", up, cb)],
                    [("row", None, f, cb, BF16), ("row", None, f, cb, BF16)], s, 512, ncol=f // cb)


def _ple_fwd(h, gp, pp):
    s, d = h.shape
    cb = _tile(d, 1024)

    def body(i, n, h_ref, g_ref, p_ref, o_ref):
        o_ref[...] = h_ref[...] + _sigmoid(g_ref[...]) * p_ref[...]

    return _rowwise("ple_fwd", body, [("row", h, cb), ("row", gp, cb), ("row", pp, cb)], [("row", None, d, cb, F32)],
                    s, 512, ncol=d // cb)[0]


def _ple_bwd(dh, gp, pp):
    s, d = dh.shape
    cb = _tile(d, 1024)

    def body(i, n, d_ref, g_ref, p_ref, dg_ref, dp_ref):
        dv = d_ref[...]
        sg = _sigmoid(g_ref[...])
        dg_ref[...] = (dv * p_ref[...] * sg * (1.0 - sg)).astype(BF16)
        dp_ref[...] = (dv * sg).astype(BF16)

    return _rowwise("ple_bwd", body, [("row", dh, cb), ("row", gp, cb), ("row", pp, cb)],
                    [("row", None, d, cb, BF16), ("row", None, d, cb, BF16)], s, 512, ncol=d // cb)


def _group_matrix(n_groups):
    r = lax.broadcasted_iota(jnp.int32, (LANE, n_groups * LANE), 0)
    c = lax.broadcasted_iota(jnp.int32, (LANE, n_groups * LANE), 1)
    return ((c % LANE < SSD_HEADS_PER_GROUP) & (r == (c // LANE) * SSD_HEADS_PER_GROUP + c % LANE)).astype(F32)


def _dt_fwd(dt_pre, bias, n_heads, n_groups):
    s = dt_pre.shape[0]
    gl = n_groups * LANE

    def body(i, n, d_ref, b_ref, o_ref):
        lane = lax.broadcasted_iota(jnp.int32, d_ref.shape, 1)
        v = jnp.where(lane < n_heads, _softplus(d_ref[...] + b_ref[...]), 0.0)
        o_ref[...] = jnp.dot(v, _group_matrix(n_groups), preferred_element_type=F32, precision=HIGHEST)

    return _rowwise("ssd_dt_fwd", body, [("row", dt_pre, LANE), ("vec", bias, LANE)], [("row", None, gl, gl, F32)],
                    s, 512)[0]


def _dt_bwd(ddt_g, dt_pre, bias, n_heads, n_groups):
    s = dt_pre.shape[0]
    gl = n_groups * LANE

    def body(i, n, g_ref, d_ref, b_ref, o_ref, db_ref):
        lane = lax.broadcasted_iota(jnp.int32, d_ref.shape, 1)
        ddt = lax.dot_general(g_ref[...], _group_matrix(n_groups), _NT, preferred_element_type=F32, precision=HIGHEST)
        v = jnp.where(lane < n_heads, ddt * _sigmoid(d_ref[...] + b_ref[...]), 0.0)
        o_ref[...] = v.astype(BF16)
        _acc(i, db_ref, _colsum(v))

    return _rowwise("ssd_dt_bwd", body, [("row", ddt_g, gl), ("row", dt_pre, LANE), ("vec", bias, LANE)],
                    [("row", None, LANE, LANE, BF16), ("acc", 1, LANE, LANE, F32)], s, 512)


def _ssd_chunk_terms(dt, alog):
    ln = dt.shape[0]
    a_neg = -jnp.exp(alog)
    row = lax.broadcasted_iota(jnp.int32, (ln, ln), 0)
    col = lax.broadcasted_iota(jnp.int32, (ln, ln), 1)
    tril = row >= col
    cs = jnp.dot(tril.astype(F32), dt * a_neg, preferred_element_type=F32, precision=HIGHEST)
    return a_neg, cs, tril


def _head_lanes(v):
    return [jnp.broadcast_to(v[:, e:e + 1], v.shape) for e in range(SSD_HEADS_PER_GROUP)]


def _ssd_head_scores(bc_cs, cst, cb_mat, tril, e):
    lm = jnp.where(tril, jnp.exp(jnp.minimum(bc_cs[e] - cst[e:e + 1, :], 0.0)), 0.0)
    return (cb_mat * lm).astype(BF16), lm


_NT = (((1,), (1,)), ((), ()))
_TN = (((0,), (0,)), ((), ()))


def _ssd_fwd(xbc, dt, alog, inner, n_groups):
    s = xbc.shape[0]
    ln = SSD_CHUNK
    nc = s // ln
    gw = SSD_HEADS_PER_GROUP * SSD_HEAD_DIM
    npair = gw // LANE
    boff = inner // LANE

    def body(xs_ref, b_ref, c_ref, dt_ref, alog_ref, y_ref, sin_ref, st_ref):
        c = pl.program_id(1)

        @pl.when(c == 0)
        def _():
            st_ref[...] = jnp.zeros_like(st_ref)

        dtv = dt_ref[...]
        _, cs, tril = _ssd_chunk_terms(dtv, alog_ref[...])
        cst = cs.T
        bc_cs, bc_dt = _head_lanes(cs), _head_lanes(dtv)
        xs = xs_ref[...]
        bg = b_ref[...].astype(BF16)
        cg = c_ref[...].astype(BF16)
        cb_mat = lax.dot_general(cg, bg, _NT, preferred_element_type=F32)
        sg = st_ref[...]
        sin_ref[0] = sg
        lo = lax.broadcasted_iota(jnp.int32, (ln, LANE), 1) < SSD_HEAD_DIM
        ys, news = [], []
        for pr in range(npair):
            cols = slice(LANE * pr, LANE * (pr + 1))
            cs_p = jnp.where(lo, bc_cs[2 * pr], bc_cs[2 * pr + 1])
            x = xs[:, cols] * jnp.where(lo, bc_dt[2 * pr], bc_dt[2 * pr + 1])
            tot_p = cs_p[ln - 1:ln, :]
            xp = x.astype(BF16)
            xd = (x * jnp.exp(tot_p - cs_p)).astype(BF16)
            zero = jnp.zeros_like(xp)
            sc0 = _ssd_head_scores(bc_cs, cst, cb_mat, tril, 2 * pr)[0]
            sc1 = _ssd_head_scores(bc_cs, cst, cb_mat, tril, 2 * pr + 1)[0]
            acc = jnp.dot(sc0, jnp.where(lo, xp, zero), preferred_element_type=F32)
            acc = acc + jnp.dot(sc1, jnp.where(lo, zero, xp), preferred_element_type=F32)
            sp = sg[:, cols]
            yoff = jnp.dot(cg, sp.astype(BF16), preferred_element_type=F32) * jnp.exp(cs_p)
            ys.append(acc + yoff)
            news.append(jnp.exp(tot_p) * sp + lax.dot_general(bg, xd, _TN, preferred_element_type=F32))
        y_ref[...] = jnp.concatenate(ys, axis=1)
        st_ref[...] = jnp.concatenate(news, axis=1)

    in_specs = [pl.BlockSpec((ln, gw), lambda g, c: (c, g)),
                pl.BlockSpec((ln, SSD_STATE), lambda g, c: (c, boff + g)),
                pl.BlockSpec((ln, SSD_STATE), lambda g, c: (c, boff + n_groups + g)),
                pl.BlockSpec((ln, LANE), lambda g, c: (c, g)),
                pl.BlockSpec((1, LANE), lambda g, c: (0, g))]
    out_specs = [pl.BlockSpec((ln, gw), lambda g, c: (c, g)),
                 pl.BlockSpec((1, SSD_STATE, gw), lambda g, c: (c, 0, g))]
    return pl.pallas_call(
        body, name="ssd_fwd", grid=(n_groups, nc), in_specs=in_specs, out_specs=out_specs,
        out_shape=[jax.ShapeDtypeStruct((s, inner), F32), jax.ShapeDtypeStruct((nc, SSD_STATE, inner), F32)],
        scratch_shapes=[pltpu.VMEM((SSD_STATE, gw), F32)],
        compiler_params=_params("parallel", "arbitrary"),
    )(xbc, xbc, xbc, dt, alog)


def _ssd_bwd(xbc, dt, alog, dy, y, sin, dskip_e, inner, n_groups):
    s = xbc.shape[0]
    ln = SSD_CHUNK
    nc = s // ln
    gw = SSD_HEADS_PER_GROUP * SSD_HEAD_DIM
    npair = gw // LANE
    boff = inner // LANE

    def body(xs_ref, b_ref, c_ref, dt_ref, alog_ref, dy_ref, y_ref, sin_ref, sout_ref, dsk_ref,
             dxs_ref, db_ref, dc_ref, ddt_ref, dalog_ref, ds_ref):
        step = pl.program_id(1)

        @pl.when(step == 0)
        def _():
            ds_ref[...] = jnp.zeros_like(ds_ref)

        dtv = dt_ref[...]
        a_neg, cs, tril = _ssd_chunk_terms(dtv, alog_ref[...])
        cst = cs.T
        bc_cs, bc_dt = _head_lanes(cs), _head_lanes(dtv)
        xs = xs_ref[...]
        bg = b_ref[...].astype(BF16)
        cg = c_ref[...].astype(BF16)
        cb_mat = lax.dot_general(cg, bg, _NT, preferred_element_type=F32)
        dyv = dy_ref[...]
        yv = y_ref[...]
        dskv = dsk_ref[...]
        s_in = sin_ref[0]
        s_out = sout_ref[0]
        d_s = ds_ref[...]
        lane = lax.broadcasted_iota(jnp.int32, (ln, LANE), 1)
        rowl = lax.broadcasted_iota(jnp.int32, (ln, LANE), 0)
        lo = lane < SSD_HEAD_DIM
        dcb = jnp.zeros((ln, ln), F32)
        dbg = jnp.zeros((ln, SSD_STATE), F32)
        dcg = jnp.zeros((ln, SSD_STATE), F32)
        dcs = jnp.zeros((ln, LANE), F32)
        ddt_x = jnp.zeros((ln, LANE), F32)
        dxs_parts, nds = [], []

        def head_sums(v, pr, into):
            s0 = jnp.sum(jnp.where(lo, v, 0.0), axis=1, keepdims=True)
            s1 = jnp.sum(jnp.where(lo, 0.0, v), axis=1, keepdims=True)
            return into + jnp.where(lane == 2 * pr, s0, 0.0) + jnp.where(lane == 2 * pr + 1, s1, 0.0)

        for pr in range(npair):
            cols = slice(LANE * pr, LANE * (pr + 1))
            cs_p = jnp.where(lo, bc_cs[2 * pr], bc_cs[2 * pr + 1])
            dt_p = jnp.where(lo, bc_dt[2 * pr], bc_dt[2 * pr + 1])
            xs_p = xs[:, cols]
            x = xs_p * dt_p
            tot_p = cs_p[ln - 1:ln, :]
            dec = jnp.exp(tot_p - cs_p)
            xp = x.astype(BF16)
            xd = (x * dec).astype(BF16)
            dy_p = dyv[:, cols]
            dyp = dy_p.astype(BF16)
            dye = (jnp.exp(cs_p) * dy_p).astype(BF16)
            zero = jnp.zeros_like(dyp)
            dxp = jnp.zeros((ln, LANE), F32)
            for e, dym in ((2 * pr, jnp.where(lo, dyp, zero)), (2 * pr + 1, jnp.where(lo, zero, dyp))):
                sc, lm = _ssd_head_scores(bc_cs, cst, cb_mat, tril, e)
                dsc = lax.dot_general(dym, xp, _NT, preferred_element_type=F32)
                dcb = dcb + dsc * lm
                dxp = dxp + lax.dot_general(sc, dym, _TN, preferred_element_type=F32)
            dsp = d_s[:, cols]
            dspb = dsp.astype(BF16)
            dxp = dxp + dec * jnp.dot(bg, dspb, preferred_element_type=F32)
            dcg = dcg + lax.dot_general(dye, s_in[:, cols].astype(BF16), _NT, preferred_element_type=F32)
            dbg = dbg + lax.dot_general(xd, dspb, _NT, preferred_element_type=F32)
            nds.append(jnp.exp(tot_p) * dsp + lax.dot_general(cg, dye, _TN, preferred_element_type=F32))
            dxs_parts.append(dxp * dt_p + dy_p * dskv[:, cols])
            dcs = head_sums(yv[:, cols] * dyp.astype(F32) - xp.astype(F32) * dxp, pr, dcs)
            tot_row = jnp.broadcast_to(_colsum(s_out[:, cols] * dsp), (ln, LANE))
            dcs = head_sums(jnp.where(rowl == ln - 1, tot_row, 0.0), pr, dcs)
            ddt_x = head_sums(dxp * xs_p, pr, ddt_x)
        ds_ref[...] = jnp.concatenate(nds, axis=1)
        dxs_ref[...] = jnp.concatenate(dxs_parts, axis=1)
        dcbb = dcb.astype(BF16)
        dc_ref[...] = dcg + jnp.dot(dcbb, bg, preferred_element_type=F32)
        db_ref[...] = dbg + lax.dot_general(dcbb, cg, _TN, preferred_element_type=F32)
        row = lax.broadcasted_iota(jnp.int32, (ln, ln), 0)
        col = lax.broadcasted_iota(jnp.int32, (ln, ln), 1)
        dadt = jnp.dot((row <= col).astype(F32), dcs, preferred_element_type=F32, precision=HIGHEST)
        ddt_ref[...] = a_neg * dadt + ddt_x
        _acc(step, dalog_ref, _colsum(dadt * dtv) * a_neg)

    def rc(step):
        return nc - 1 - step

    in_specs = [pl.BlockSpec((ln, gw), lambda g, t: (rc(t), g)),
                pl.BlockSpec((ln, SSD_STATE), lambda g, t: (rc(t), boff + g)),
                pl.BlockSpec((ln, SSD_STATE), lambda g, t: (rc(t), boff + n_groups + g)),
                pl.BlockSpec((ln, LANE), lambda g, t: (rc(t), g)),
                pl.BlockSpec((1, LANE), lambda g, t: (0, g)),
                pl.BlockSpec((ln, gw), lambda g, t: (rc(t), g)),
                pl.BlockSpec((ln, gw), lambda g, t: (rc(t), g)),
                pl.BlockSpec((1, SSD_STATE, gw), lambda g, t: (rc(t), 0, g)),
                pl.BlockSpec((1, SSD_STATE, gw), lambda g, t: (jnp.minimum(rc(t) + 1, nc - 1), 0, g)),
                pl.BlockSpec((1, gw), lambda g, t: (0, g))]
    out_specs = [pl.BlockSpec((ln, gw), lambda g, t: (rc(t), g)),
                 pl.BlockSpec((ln, SSD_STATE), lambda g, t: (rc(t), g)),
                 pl.BlockSpec((ln, SSD_STATE), lambda g, t: (rc(t), g)),
                 pl.BlockSpec((ln, LANE), lambda g, t: (rc(t), g)),
                 pl.BlockSpec((1, LANE), lambda g, t: (0, g))]
    gn = n_groups * SSD_STATE
    return pl.pallas_call(
        body, name="ssd_bwd", grid=(n_groups, nc), in_specs=in_specs, out_specs=out_specs,
        out_shape=[jax.ShapeDtypeStruct((s, inner + 2 * gn), F32), jax.ShapeDtypeStruct((s, gn), F32),
                   jax.ShapeDtypeStruct((s, gn), F32), jax.ShapeDtypeStruct((s, n_groups * LANE), F32),
                   jax.ShapeDtypeStruct((1, n_groups * LANE), F32)],
        scratch_shapes=[pltpu.VMEM((SSD_STATE, gw), F32)],
        compiler_params=_params("parallel", "arbitrary"),
    )(xbc, xbc, xbc, dt, alog, dy, y, sin, sin, dskip_e)


def _put_cols(buf, part, block, name):
    s, w = part.shape
    ts = min(512, s)

    def body(b_ref, p_ref, o_ref):
        o_ref[...] = p_ref[...]

    return pl.pallas_call(
        body, name=name, grid=(s // ts,), in_specs=[_ANY, pl.BlockSpec((ts, w), lambda i: (i, 0))],
        out_specs=pl.BlockSpec((ts, w), lambda i: (i, block)), out_shape=jax.ShapeDtypeStruct(buf.shape, buf.dtype),
        input_output_aliases={0: 0}, compiler_params=_params("parallel"))(buf, part)


def _ssd_gate_norm_fwd(ysc, xbc, z, dskip_e, norm_g, n_groups):
    s, inner = ysc.shape
    gsz = inner // n_groups

    def body(i, n, y_ref, x_ref, z_ref, d_ref, g_ref, o_ref):
        y2 = (y_ref[...] + d_ref[...] * x_ref[...]) * _silu(z_ref[...])
        gg = g_ref[...]
        outs = []
        for k in range(n_groups):
            cols = slice(k * gsz, (k + 1) * gsz)
            v = y2[:, cols]
            r = lax.rsqrt(jnp.mean(v * v, axis=-1, keepdims=True) + NORM_EPS)
            outs.append(v * r * gg[:, cols])
        o_ref[...] = jnp.concatenate(outs, axis=1).astype(BF16)

    return _rowwise("ssd_gate_norm_fwd", body,
                    [("row", ysc, inner), ("row", xbc, inner), ("row", z, inner), ("vec", dskip_e, inner),
                     ("vec", norm_g, inner)], [("row", None, inner, inner, BF16)], s, 128)[0]


def _ssd_gate_norm_bwd(dyn, ysc, xbc, z, dskip_e, norm_g, n_groups):
    s, inner = ysc.shape
    gsz = inner // n_groups

    def body(i, n, dn_ref, y_ref, x_ref, z_ref, d_ref, g_ref, dy_ref, dz_ref, dg_ref, dd_ref):
        xs = x_ref[...]
        zz = z_ref[...]
        y = y_ref[...] + d_ref[...] * xs
        sz = _silu(zz)
        y2 = y * sz
        dn = dn_ref[...]
        gg = g_ref[...]
        dy2s, dgs = [], []
        for k in range(n_groups):
            cols = slice(k * gsz, (k + 1) * gsz)
            v = y2[:, cols]
            d = dn[:, cols]
            r = lax.rsqrt(jnp.mean(v * v, axis=-1, keepdims=True) + NORM_EPS)
            vh = v * r
            dgs.append(_colsum(d * vh))
            dvh = d * gg[:, cols]
            dy2s.append(r * (dvh - vh * jnp.mean(dvh * vh, axis=-1, keepdims=True)))
        dy2 = jnp.concatenate(dy2s, axis=1)
        dy = dy2 * sz
        dy_ref[...] = dy
        dz_ref[...] = (dy2 * y * _dsilu(zz)).astype(BF16)
        _acc(i, dg_ref, jnp.concatenate(dgs, axis=1))
        _acc(i, dd_ref, _colsum(dy * xs))

    return _rowwise("ssd_gate_norm_bwd", body,
                    [("row", dyn, inner), ("row", ysc, inner), ("row", xbc, inner), ("row", z, inner),
                     ("vec", dskip_e, inner), ("vec", norm_g, inner)],
                    [("row", None, inner, inner, F32), ("row", None, inner, inner, BF16),
                     ("acc", 1, inner, inner, F32), ("acc", 1, inner, inner, F32)], s, 128)


def _adamw(w, g, m, v, name, emit_g=False):
    rows, c = w.shape
    bc1 = 1.0 - ADAM_B1 ** ADAM_STEP
    bc2 = 1.0 - ADAM_B2 ** ADAM_STEP

    def body(i, n, w_ref, g_ref, m_ref, v_ref, d_ref, mo_ref, vo_ref, *go_ref):
        gg = g_ref[...]
        if emit_g:
            go_ref[0][...] = gg
        mn = ADAM_B1 * m_ref[...] + (1.0 - ADAM_B1) * gg
        vn = ADAM_B2 * v_ref[...] + (1.0 - ADAM_B2) * (gg * gg)
        d_ref[...] = -ADAM_LR * ((mn / bc1) / (jnp.sqrt(vn / bc2) + ADAM_EPS) + ADAM_WD * w_ref[...])
        mo_ref[...] = mn
        vo_ref[...] = vn

    ts = 128 if rows % 128 == 0 else rows
    return _rowwise(name, body, [("row", w, c), ("row", g, c), ("row", m, c), ("row", v, c)],
                    [("row", None, c, c, F32)] * (4 if emit_g else 3), rows, ts)


_ANY = pl.BlockSpec(memory_space=pl.ANY)


def _place():
    x, y, c = lax.axis_index("x"), lax.axis_index("y"), lax.axis_index("c")
    chips = [(1 - x, y), (x, 1 - y), (1 - x, 1 - y)]
    return x, y, c, chips


def _rcopy(src, dst, ssem, rsem, dev):
    return pltpu.make_async_remote_copy(src_ref=src, dst_ref=dst, send_sem=ssem, recv_sem=rsem, device_id=dev,
                                        device_id_type=MESH)


def _place_shard(shards, layer, q_idx, dtype, name):
    _, r, cc = shards.shape
    tr = _tile(r, 256, 16)

    def body(q_ref, s_ref, o_ref):
        o_ref[0] = s_ref[0].astype(dtype)

    grid_spec = pltpu.PrefetchScalarGridSpec(
        num_scalar_prefetch=1, grid=(r // tr,),
        in_specs=[pl.BlockSpec((1, tr, cc), lambda i, q_ref: (layer, i, 0))],
        out_specs=pl.BlockSpec((1, tr, cc), lambda i, q_ref: (q_ref[0], i, 0)))
    return pl.pallas_call(body, name=name, grid_spec=grid_spec, out_shape=jax.ShapeDtypeStruct((N_CHIPS, r, cc), dtype),
                          compiler_params=_params("parallel"))(q_idx, shards)


_HBM = pl.BlockSpec(memory_space=pltpu.HBM)
_SEM = pl.BlockSpec(memory_space=pltpu.SEMAPHORE)
_EFFECT = pltpu.SideEffectType.DATAFLOW_SIDE_EFFECTING


def _in_hbm(arrs):
    return [pltpu.with_memory_space_constraint(a, pltpu.HBM) for a in arrs]


def _gather_start(bufs, name):
    n = len(bufs)
    half = [e.shape[1] // 2 for e in bufs]

    def body(*refs):
        ins, send, recv, token = refs[:n], refs[n], refs[n + 1], refs[2 * n + 2]
        x, y, c, chips = _place()
        q = 2 * x + y
        for e in range(n):
            blk = ins[e].at[q, pl.ds(c * half[e], half[e])]
            for j, (cx, cy) in enumerate(chips):
                _rcopy(blk, blk, send.at[3 * e + j], recv.at[3 * e + j], (cx, cy, c)).start()
        token[...] = jnp.zeros_like(token)

    out = pl.pallas_call(
        body, name=name,
        out_shape=(pltpu.SemaphoreType.DMA((3 * n,)), pltpu.SemaphoreType.DMA((3 * n,)),
                   *[pltpu.HBM(b.shape, b.dtype) for b in bufs], jax.ShapeDtypeStruct((SUBLANE, LANE), F32)),
        in_specs=[_HBM] * n, out_specs=(_SEM, _SEM, *[_HBM] * n, pl.BlockSpec(memory_space=pltpu.VMEM)),
        input_output_aliases={e: 2 + e for e in range(n)},
        compiler_params=pltpu.CompilerParams(has_side_effects=_EFFECT))(*_in_hbm(bufs))
    return out[0], out[1], list(out[2:2 + n]), out[2 + n]


def _gather_wait(send, recv, bufs, after, name):
    n = len(bufs)
    half = [e.shape[1] // 2 for e in bufs]

    def body(*refs):
        ins, send_ref, recv_ref = refs[:n], refs[n], refs[n + 1]
        x, y, c, chips = _place()
        q = 2 * x + y
        for e in range(n):
            rows = pl.ds(c * half[e], half[e])
            for j, (cx, cy) in enumerate(chips):
                cp = _rcopy(ins[e].at[q, rows], ins[e].at[2 * cx + cy, rows], send_ref.at[3 * e + j],
                            recv_ref.at[3 * e + j], (cx, cy, c))
                cp.wait_send()
                cp.wait_recv()

    return list(pl.pallas_call(
        body, name=name, out_shape=tuple(pltpu.HBM(b.shape, b.dtype) for b in bufs),
        in_specs=[_HBM] * n + [_SEM, _SEM, _ANY], out_specs=[_HBM] * n,
        input_output_aliases={e: e for e in range(n)},
        compiler_params=pltpu.CompilerParams(has_side_effects=_EFFECT))(*bufs, send, recv, after))


def _forward_sibling(bufs, name):
    n = len(bufs)
    half = [e.shape[1] // 2 for e in bufs]

    def body(*refs):
        outs = refs[n:2 * n]
        send, recv = refs[2 * n:]
        x, y, c, chips = _place()
        sib = (x, y, 1 - c)
        cps = []
        for e in range(n):
            for j, (cx, cy) in enumerate(chips):
                blk = outs[e].at[2 * cx + cy, pl.ds(c * half[e], half[e])]
                cps.append(_rcopy(blk, blk, send.at[3 * e + j], recv.at[3 * e + j], sib))
        for cp in cps:
            cp.start()
        for e in range(n):
            for j, (cx, cy) in enumerate(chips):
                blk = outs[e].at[2 * cx + cy, pl.ds((1 - c) * half[e], half[e])]
                _rcopy(blk, blk, send.at[3 * e + j], recv.at[3 * e + j], sib).wait_recv()
        for cp in cps:
            cp.wait_send()

    return list(pl.pallas_call(
        body, name=name, in_specs=[_ANY] * n, out_specs=[_ANY] * n,
        out_shape=[jax.ShapeDtypeStruct(e.shape, e.dtype) for e in bufs],
        input_output_aliases={e: e for e in range(n)},
        scratch_shapes=[pltpu.SemaphoreType.DMA((3 * n,))] * 2,
    )(*bufs))


def _scatter_start(entries, name):
    n = len(entries)
    lands = [lax.empty(e.shape, e.dtype) for e in entries]

    def body(*refs):
        ins, land, send, recv, token = refs[:n], refs[n:2 * n], refs[2 * n], refs[2 * n + 1], refs[4 * n + 2]
        x, y, c, chips = _place()
        q = 2 * x + y
        for e in range(n):
            for j, (cx, cy) in enumerate(chips):
                _rcopy(ins[e].at[2 * cx + cy], land[e].at[q], send.at[3 * e + j], recv.at[3 * e + j],
                       (cx, cy, c)).start()
        token[...] = jnp.zeros_like(token)

    out = pl.pallas_call(
        body, name=name,
        out_shape=(pltpu.SemaphoreType.DMA((3 * n,)), pltpu.SemaphoreType.DMA((3 * n,)),
                   *[pltpu.HBM(b.shape, b.dtype) for b in entries + lands],
                   jax.ShapeDtypeStruct((SUBLANE, LANE), F32)),
        in_specs=[_HBM] * (2 * n),
        out_specs=(_SEM, _SEM, *[_HBM] * (2 * n), pl.BlockSpec(memory_space=pltpu.VMEM)),
        input_output_aliases={e: 2 + e for e in range(2 * n)},
        compiler_params=pltpu.CompilerParams(has_side_effects=_EFFECT))(*_in_hbm(entries + lands))
    return out[0], out[1], list(out[2:2 + n]), list(out[2 + n:2 + 2 * n]), out[2 + 2 * n]


def _scatter_wait(send, recv, entries, lands, after, name):
    n = len(entries)

    def body(*refs):
        ins, land, send_ref, recv_ref = refs[:n], refs[n:2 * n], refs[2 * n], refs[2 * n + 1]
        x, y, c, chips = _place()
        for e in range(n):
            for j, (cx, cy) in enumerate(chips):
                k = 2 * cx + cy
                cp = _rcopy(ins[e].at[k], land[e].at[k], send_ref.at[3 * e + j], recv_ref.at[3 * e + j],
                            (cx, cy, c))
                cp.wait_send()
                cp.wait_recv()

    out = pl.pallas_call(
        body, name=name, out_shape=tuple(pltpu.HBM(b.shape, b.dtype) for b in entries + lands),
        in_specs=[_HBM] * (2 * n) + [_SEM, _SEM, _ANY], out_specs=[_HBM] * (2 * n),
        input_output_aliases={e: e for e in range(2 * n)},
        compiler_params=pltpu.CompilerParams(has_side_effects=_EFFECT))(*entries, *lands, send, recv, after)
    return list(out[:n]), list(out[n:])


def _swap_start(entries, name):
    n = len(entries)
    half = [e.shape[1] // 2 for e in entries]
    lands = [lax.empty((N_CHIPS, h, e.shape[2]), e.dtype) for e, h in zip(entries, half)]

    def body(*refs):
        ins, land, send, recv, token = refs[:n], refs[n:2 * n], refs[2 * n], refs[2 * n + 1], refs[4 * n + 2]
        x, y, c, _ = _place()
        for e in range(n):
            _rcopy(ins[e].at[:, pl.ds((1 - c) * half[e], half[e]), :], land[e], send.at[e], recv.at[e],
                   (x, y, 1 - c)).start()
        token[...] = jnp.zeros_like(token)

    out = pl.pallas_call(
        body, name=name,
        out_shape=(pltpu.SemaphoreType.DMA((n,)), pltpu.SemaphoreType.DMA((n,)),
                   *[pltpu.HBM(b.shape, b.dtype) for b in entries + lands],
                   jax.ShapeDtypeStruct((SUBLANE, LANE), F32)),
        in_specs=[_HBM] * (2 * n),
        out_specs=(_SEM, _SEM, *[_HBM] * (2 * n), pl.BlockSpec(memory_space=pltpu.VMEM)),
        input_output_aliases={e: 2 + e for e in range(2 * n)},
        compiler_params=pltpu.CompilerParams(has_side_effects=_EFFECT))(*_in_hbm(entries + lands))
    return out[0], out[1], list(out[2:2 + n]), list(out[2 + n:2 + 2 * n]), out[2 + 2 * n]


def _swap_wait(send, recv, entries, lands, after, name):
    n = len(entries)
    half = [e.shape[1] // 2 for e in entries]

    def body(*refs):
        ins, land, send_ref, recv_ref = refs[:n], refs[n:2 * n], refs[2 * n], refs[2 * n + 1]
        x, y, c, _ = _place()
        for e in range(n):
            cp = _rcopy(ins[e].at[:, pl.ds((1 - c) * half[e], half[e]), :], land[e], send_ref.at[e], recv_ref.at[e],
                        (x, y, 1 - c))
            cp.wait_send()
            cp.wait_recv()

    out = pl.pallas_call(
        body, name=name, out_shape=tuple(pltpu.HBM(b.shape, b.dtype) for b in entries + lands),
        in_specs=[_HBM] * (2 * n) + [_SEM, _SEM, _ANY], out_specs=[_HBM] * (2 * n),
        input_output_aliases={e: e for e in range(2 * n)},
        compiler_params=pltpu.CompilerParams(has_side_effects=_EFFECT))(*entries, *lands, send, recv, after)
    return list(out[:n]), list(out[n:])


def _swap_halves(entries, name):
    n = len(entries)
    half = [e.shape[1] // 2 for e in entries]

    def body(*refs):
        ins, outs = refs[:n], refs[n:2 * n]
        send, recv = refs[2 * n:]
        x, y, c, _ = _place()
        cps = [_rcopy(ins[e].at[:, pl.ds((1 - c) * half[e], half[e]), :], outs[e], send.at[e], recv.at[e],
                      (x, y, 1 - c)) for e in range(n)]
        for cp in cps:
            cp.start()
        for cp in cps:
            cp.wait()

    return pl.pallas_call(
        body, name=name, in_specs=[_ANY] * n, out_specs=[_ANY] * n,
        out_shape=[jax.ShapeDtypeStruct((N_CHIPS, h, e.shape[2]), e.dtype) for e, h in zip(entries, half)],
        scratch_shapes=[pltpu.SemaphoreType.DMA((n,))] * 2,
    )(*entries)


def _join_halves(bufs, name):
    n = len(bufs)
    pairs = [(o, layer) for o in range(n) for layer in range(bufs[o].shape[0])]
    npair = len(pairs)

    def body(*refs):
        outs = refs[n:2 * n]
        send, recv = refs[2 * n:]
        x, y, c, _ = _place()
        cps = []
        for k, (o, layer) in enumerate(pairs):
            r2 = bufs[o].shape[1] // 2
            blk = outs[o].at[layer, pl.ds(c * r2, r2)]
            cps.append(_rcopy(blk, blk, send.at[k], recv.at[k], (x, y, 1 - c)))
        for cp in cps:
            cp.start()
        for k, (o, layer) in enumerate(pairs):
            r2 = bufs[o].shape[1] // 2
            blk = outs[o].at[layer, pl.ds((1 - c) * r2, r2)]
            _rcopy(blk, blk, send.at[k], recv.at[k], (x, y, 1 - c)).wait_recv()
        for cp in cps:
            cp.wait_send()

    return pl.pallas_call(
        body, name=name, in_specs=[_ANY] * n, out_specs=[_ANY] * n,
        out_shape=[jax.ShapeDtypeStruct(b.shape, b.dtype) for b in bufs],
        input_output_aliases={e: e for e in range(n)},
        scratch_shapes=[pltpu.SemaphoreType.DMA((npair,))] * 2,
    )(*bufs)


def _gather_all(v, name):
    def body(v_ref, o_ref, send, recv, loc):
        x, y, c, _ = _place()
        me = 4 * x + 2 * y + c
        mine = pltpu.make_async_copy(v_ref, o_ref.at[me], loc)
        mine.start()
        peers = []
        for k in range(1, N_DEV):
            px = 1 - x if k & 4 else x
            py = 1 - y if k & 2 else y
            pc = 1 - c if k & 1 else c
            peers.append((px, py, pc))
        cps = [_rcopy(v_ref, o_ref.at[me], send.at[k], recv.at[k], peers[k]) for k in range(N_DEV - 1)]
        for cp in cps:
            cp.start()
        for k, (px, py, pc) in enumerate(peers):
            blk = o_ref.at[4 * px + 2 * py + pc]
            _rcopy(blk, blk, send.at[k], recv.at[k], (px, py, pc)).wait_recv()
        for cp in cps:
            cp.wait_send()
        mine.wait()

    return pl.pallas_call(
        body, name=name, in_specs=[_ANY], out_specs=_ANY, out_shape=jax.ShapeDtypeStruct((N_DEV,) + v.shape, v.dtype),
        scratch_shapes=[pltpu.SemaphoreType.DMA((N_DEV - 1,))] * 2 + [pltpu.SemaphoreType.DMA],
    )(v)


def _add_own_half(gst, rx, c_idx, name):
    _, r, cc = gst.shape
    r2 = r // 2
    tr = _tile(r2, 256, 16)
    g4 = gst.reshape(N_CHIPS, 2, r2, cc)

    def body(c_ref, g_ref, r_ref, o_ref):
        o_ref[...] = (g_ref[0].astype(F32) + r_ref[...].astype(F32)).astype(BF16)

    grid_spec = pltpu.PrefetchScalarGridSpec(
        num_scalar_prefetch=1, grid=(N_CHIPS, r2 // tr),
        in_specs=[pl.BlockSpec((1, 1, tr, cc), lambda k, i, c_ref: (k, c_ref[0], i, 0)),
                  pl.BlockSpec((1, tr, cc), lambda k, i, c_ref: (k, i, 0))],
        out_specs=pl.BlockSpec((1, tr, cc), lambda k, i, c_ref: (k, i, 0)))
    return pl.pallas_call(body, name=name, grid_spec=grid_spec, out_shape=jax.ShapeDtypeStruct((N_CHIPS, r2, cc), BF16),
                          compiler_params=_params("parallel", "parallel"))(c_idx, g4, rx)


def _sum_into(buf, rx, own, layer, qc, out_shape, name):
    _, r2, cc = rx.shape
    tr = _tile(r2, 256, 16)
    nb = r2 // tr

    def body(qc_ref, *refs):
        rx_ref, own_ref, o_ref = refs[-3:]
        q = qc_ref[0]
        acc = None
        for k in range(N_CHIPS):
            v = jnp.where(q == k, own_ref[0], rx_ref[k]).astype(F32)
            acc = v if acc is None else acc + v
        o_ref[0] = acc

    in_specs = [pl.BlockSpec((N_CHIPS, tr, cc), lambda i, qc_ref: (0, i, 0)),
                pl.BlockSpec((1, tr, cc), lambda i, qc_ref: (qc_ref[0], i, 0))]
    args = (rx, own)
    aliases = {}
    if buf is not None:
        in_specs = [_ANY] + in_specs
        args = (buf,) + args
        aliases = {1: 0}
    grid_spec = pltpu.PrefetchScalarGridSpec(
        num_scalar_prefetch=1, grid=(nb,), in_specs=in_specs,
        out_specs=pl.BlockSpec((1, tr, cc), lambda i, qc_ref: (layer, qc_ref[1] * nb + i, 0)))
    return pl.pallas_call(body, name=name, grid_spec=grid_spec, out_shape=jax.ShapeDtypeStruct(out_shape, F32),
                          input_output_aliases=aliases, compiler_params=_params("parallel"))(qc, *args)


def _sum_slots(st, name):
    k, r, cc = st.shape
    tr = _tile(r, 256, 8)

    def body(s_ref, o_ref):
        acc = s_ref[0].astype(F32)
        for j in range(1, k):
            acc = acc + s_ref[j].astype(F32)
        o_ref[...] = acc

    return pl.pallas_call(body, name=name, grid=(r // tr,), in_specs=[pl.BlockSpec((k, tr, cc), lambda i: (0, i, 0))],
                          out_specs=pl.BlockSpec((tr, cc), lambda i: (i, 0)),
                          out_shape=jax.ShapeDtypeStruct((r, cc), F32), compiler_params=_params("parallel"))(st)


def _pack(arrs, rows_mult=2 * SUBLANE):
    flat = jnp.concatenate([a.reshape(-1).astype(F32) for a in arrs])
    quantum = rows_mult * LANE
    padded = -(-flat.shape[0] // quantum) * quantum
    return jnp.pad(flat, (0, padded - flat.shape[0])).reshape(-1, LANE)


def _unpack(buf, shapes):
    flat = buf.reshape(-1)
    out, off = [], 0
    for sh in shapes:
        size = 1
        for d in sh:
            size *= d
        out.append(flat[off:off + size].reshape(sh))
        off += size
    return out


def kernel(x, p, norm_mix_g, norm_ffn_g, norm_ple_g, final_norm_g, a_w_in, a_conv_w, a_conv_b, a_w_gate_r, a_b_gate_r, a_w_gate_i, a_b_gate_i, a_lambda, a_w_out, b_w_in, b_conv_w, b_conv_b, b_dt_bias, b_a_log, b_d_skip, b_norm_g, b_w_out, ffn_w_gate, ffn_w_up, ffn_w_down, ple_w_proj, ple_w_gate, loss_target, m_norm_mix_g, m_norm_ffn_g, m_norm_ple_g, m_final_norm_g, m_a_w_in, m_a_conv_w, m_a_conv_b, m_a_w_gate_r, m_a_b_gate_r, m_a_w_gate_i, m_a_b_gate_i, m_a_lambda, m_a_w_out, m_b_w_in, m_b_conv_w, m_b_conv_b, m_b_dt_bias, m_b_a_log, m_b_d_skip, m_b_norm_g, m_b_w_out, m_ffn_w_gate, m_ffn_w_up, m_ffn_w_down, m_ple_w_proj, m_ple_w_gate, v_norm_mix_g, v_norm_ffn_g, v_norm_ple_g, v_final_norm_g, v_a_w_in, v_a_conv_w, v_a_conv_b, v_a_w_gate_r, v_a_b_gate_r, v_a_w_gate_i, v_a_b_gate_i, v_a_lambda, v_a_w_out, v_b_w_in, v_b_conv_w, v_b_conv_b, v_b_dt_bias, v_b_a_log, v_b_d_skip, v_b_norm_g, v_b_w_out, v_ffn_w_gate, v_ffn_w_up, v_ffn_w_down, v_ple_w_proj, v_ple_w_gate):
    names = ["norm_mix_g", "norm_ffn_g", "norm_ple_g", "final_norm_g", "a_w_in", "a_conv_w", "a_conv_b", "a_w_gate_r",
             "a_b_gate_r", "a_w_gate_i", "a_b_gate_i", "a_lambda", "a_w_out", "b_w_in", "b_conv_w", "b_conv_b",
             "b_dt_bias", "b_a_log", "b_d_skip", "b_norm_g", "b_w_out", "ffn_w_gate", "ffn_w_up", "ffn_w_down",
             "ple_w_proj", "ple_w_gate"]
    w_in = dict(zip(names, [norm_mix_g, norm_ffn_g, norm_ple_g, final_norm_g, a_w_in, a_conv_w, a_conv_b, a_w_gate_r,
                            a_b_gate_r, a_w_gate_i, a_b_gate_i, a_lambda, a_w_out, b_w_in, b_conv_w, b_conv_b,
                            b_dt_bias, b_a_log, b_d_skip, b_norm_g, b_w_out, ffn_w_gate, ffn_w_up, ffn_w_down,
                            ple_w_proj, ple_w_gate]))
    m_in = dict(zip(names, [m_norm_mix_g, m_norm_ffn_g, m_norm_ple_g, m_final_norm_g, m_a_w_in, m_a_conv_w,
                            m_a_conv_b, m_a_w_gate_r, m_a_b_gate_r, m_a_w_gate_i, m_a_b_gate_i, m_a_lambda,
                            m_a_w_out, m_b_w_in, m_b_conv_w, m_b_conv_b, m_b_dt_bias, m_b_a_log, m_b_d_skip,
                            m_b_norm_g, m_b_w_out, m_ffn_w_gate, m_ffn_w_up, m_ffn_w_down, m_ple_w_proj,
                            m_ple_w_gate]))
    v_in = dict(zip(names, [v_norm_mix_g, v_norm_ffn_g, v_norm_ple_g, v_final_norm_g, v_a_w_in, v_a_conv_w,
                            v_a_conv_b, v_a_w_gate_r, v_a_b_gate_r, v_a_w_gate_i, v_a_b_gate_i, v_a_lambda,
                            v_a_w_out, v_b_w_in, v_b_conv_w, v_b_conv_b, v_b_dt_bias, v_b_a_log, v_b_d_skip,
                            v_b_norm_g, v_b_w_out, v_ffn_w_gate, v_ffn_w_up, v_ffn_w_down, v_ple_w_proj,
                            v_ple_w_gate]))

    s, d = x.shape[1], x.shape[2]
    depth = norm_mix_g.shape[0]
    assert depth == 2
    q_idx = 2 * lax.axis_index("x") + lax.axis_index("y")
    c_idx = lax.axis_index("c").astype(jnp.int32).reshape(1)

    inner = b_w_out.shape[1] * N_CHIPS
    n_heads = inner // SSD_HEAD_DIM
    n_groups = n_heads // SSD_HEADS_PER_GROUP
    gn = n_groups * SSD_STATE
    xbcw = inner + 2 * gn
    assert b_conv_w.shape[2] * N_CHIPS == xbcw and n_heads <= LANE

    big = [("a_w_in", "col"), ("a_w_gate_r", "gate"), ("a_w_gate_i", "gate"), ("a_w_out", "row"),
           ("ffn_w_gate", "col"), ("ffn_w_up", "col"), ("ffn_w_down", "row"), ("ple_w_proj", "col"),
           ("ple_w_gate", "row"), ("b_w_in", "col"), ("b_w_out", "row")]
    kind_of = dict(big)

    def shard2d(name, arr):
        if kind_of[name] == "gate":
            return [arr[l].reshape(-1, arr.shape[-1]) for l in range(arr.shape[0])]
        return [arr[l] for l in range(arr.shape[0])]

    small_sharded = ["a_conv_w", "a_b_gate_r", "a_b_gate_i", "b_conv_w", "b_conv_b", "b_norm_g"]
    small_pack = _pack([w_in[nm] for nm in small_sharded], rows_mult=16)

    q_vec = q_idx.astype(jnp.int32).reshape(1)
    qc_vec = jnp.stack([q_idx, lax.axis_index("c")]).astype(jnp.int32)
    gather_groups = [
        [("a_w_in", 0), ("a_w_gate_r", 0), ("a_w_gate_i", 0), ("a_w_out", 0), ("small", 0)],
        [("ffn_w_gate", 0), ("ffn_w_up", 0)],
        [("ffn_w_down", 0), ("ple_w_proj", 0), ("ple_w_gate", 0)],
        [("b_w_in", 0), ("b_w_out", 0)],
        [("ffn_w_gate", 1), ("ffn_w_up", 1), ("ffn_w_down", 1), ("ple_w_proj", 1), ("ple_w_gate", 1)],
    ]
    gather_started = [None] * len(gather_groups)

    def gather_begin(gi):
        bufs = []
        for nm, l in gather_groups[gi]:
            if nm == "small":
                bufs.append(_place_shard(small_pack[None], 0, q_vec, F32, "place_small"))
            else:
                arr = w_in[nm]
                arr = arr.reshape(arr.shape[0], -1, arr.shape[-1]) if kind_of[nm] == "gate" else arr
                bufs.append(_place_shard(arr, l, q_vec, BF16, f"place_{nm}{l}"))
        gather_started[gi] = _gather_start(bufs, f"gather_start{gi}")
        return gather_started[gi][3][0:1, 0:1]

    wst = {}

    def gather_finish(gi, after):
        send, recv, thru, _ = gather_started[gi]
        landed = _gather_wait(send, recv, thru, after, f"gather_wait{gi}")
        for k, arr in zip(gather_groups[gi], _forward_sibling(landed, f"gather_forward{gi}")):
            wst[k] = arr

    gather_finish(0, sum(gather_begin(gi) for gi in range(len(gather_groups))))
    small_st = wst[("small", 0)]
    kept_stacked = ("a_w_in", "ffn_w_gate", "ffn_w_up", "ple_w_proj")

    def whole(nm, l):
        st = wst[(nm, l)]
        kind = kind_of[nm]
        if nm in kept_stacked:
            return st
        if kind == "row":
            return st.reshape(-1, st.shape[-1])
        if kind == "col":
            return jnp.concatenate([st[k] for k in range(N_CHIPS)], axis=1)
        heads = w_in[nm].shape[1]
        return st.reshape(N_CHIPS, heads, -1, st.shape[-1]).transpose(1, 0, 2, 3).reshape(heads, -1, st.shape[-1])

    small_parts = [_unpack(small_st[k], [w_in[nm].shape for nm in small_sharded]) for k in range(N_CHIPS)]
    small_full = {nm: jnp.concatenate([small_parts[k][i] for k in range(N_CHIPS)], axis=-1)
                  for i, nm in enumerate(small_sharded)}
    a_cw = small_full["a_conv_w"][0]
    a_br = small_full["a_b_gate_r"][0].reshape(1, -1)
    a_bi = small_full["a_b_gate_i"][0].reshape(1, -1)
    b_cw = small_full["b_conv_w"][0]
    b_cb = small_full["b_conv_b"]
    b_ng = small_full["b_norm_g"]

    def pad_lanes(v):
        return jnp.pad(v, ((0, 0), (0, LANE - v.shape[1])))

    dt_bias = pad_lanes(b_dt_bias)
    a_log = pad_lanes(b_a_log.reshape(n_groups, SSD_HEADS_PER_GROUP)).reshape(1, n_groups * LANE)
    dskip_e = jnp.repeat(b_d_skip, SSD_HEAD_DIM, axis=1)

    w_a_in = whole("a_w_in", 0)
    w_ax, w_ay = w_a_in[:N_CHIPS // 2], w_a_in[N_CHIPS // 2:]
    w_ar, w_ai, w_ao = whole("a_w_gate_r", 0), whole("a_w_gate_i", 0), whole("a_w_out", 0)
    w_fg, w_fu, w_fd, w_pp, w_pg = ([None] * depth for _ in range(5))

    def take_ffn_in(l):
        w_fg[l], w_fu[l] = whole("ffn_w_gate", l), whole("ffn_w_up", l)

    def take_ffn_out_ple(l):
        w_fd[l], w_pp[l], w_pg[l] = whole("ffn_w_down", l), whole("ple_w_proj", l), whole("ple_w_gate", l)

    grads = {}

    h0 = x[0]
    g_mix = [norm_mix_g[l:l + 1] for l in range(depth)]
    g_ffn = [norm_ffn_g[l:l + 1] for l in range(depth)]
    g_ple = [norm_ple_g[l:l + 1] for l in range(depth)]
    g_fin = final_norm_g.reshape(1, -1)

    u0 = _rmsnorm_fwd(h0, g_mix[0], "norm_mix0")
    xr_pre = _mm(u0, w_ax, stacked_b=True, name="lru_in_x")
    yg = _mm(u0, w_ay, stacked_b=True, name="lru_in_y")
    xr = _conv_fwd(xr_pre, a_cw, a_conv_b, False, "lru_conv")
    lru_a, lru_b = _lru_gates_fwd(xr, w_ar, w_ai, a_br, a_bi, a_lambda)
    hs = _scan(lru_a, lru_b, False, "lru_scan")
    y_lru = _lru_out_fwd(hs, yg)
    h_mix = [_mm(y_lru, w_ao, add=h0, name="lru_out"), None]

    def ffn_ple_fwd(h_in, l, before_down=None):
        n_f = _rmsnorm_fwd(h_in, g_ffn[l], f"norm_ffn{l}")
        gt = _mm(n_f, w_fg[l], stacked_b=True, out_dtype=BF16, name=f"ffn_gate{l}")
        up = _mm(n_f, w_fu[l], stacked_b=True, out_dtype=BF16, name=f"ffn_up{l}")
        act = _swiglu_act(gt, up)
        if before_down is not None:
            before_down(act)
        h_f = _mm(act, w_fd[l], add=h_in, name=f"ffn_down{l}")
        n_p = _rmsnorm_fwd(h_f, g_ple[l], f"norm_ple{l}")
        gp = _mm(n_p, w_pg[l], name=f"ple_gate{l}")
        pp = _mm(p[l, 0], w_pp[l], stacked_b=True, name=f"ple_proj{l}")
        h_out = _ple_fwd(h_f, gp, pp)
        return h_out, dict(h_in=h_in, n_f=n_f, gt=gt, up=up, act=act, h_f=h_f, n_p=n_p, gp=gp, pp=pp)

    gather_finish(1, h_mix[0])
    take_ffn_in(0)

    def finish_ffn_out_ple0(act):
        gather_finish(2, act)
        take_ffn_out_ple(0)

    h_l0, sv0 = ffn_ple_fwd(h_mix[0], 0, finish_ffn_out_ple0)

    gather_finish(3, h_l0)
    w_b_in = whole("b_w_in", 0)
    w_bz, w_bx = w_b_in[:, :inner], w_b_in[:, inner:inner + xbcw]
    w_bd = pad_lanes(w_b_in[:, inner + xbcw:])
    w_bo = whole("b_w_out", 0)
    u1 = _rmsnorm_fwd(h_l0, g_mix[1], "norm_mix1")
    z = _mm(u1, w_bz, name="ssd_in_z")
    xbc_pre = _mm(u1, w_bx, name="ssd_in_xbc")
    dt_pre = _mm(u1, w_bd, name="ssd_in_dt")
    xbc = _conv_fwd(xbc_pre, b_cw, b_cb, True, "ssd_conv")
    dt = _dt_fwd(dt_pre, dt_bias, n_heads, n_groups)
    ysc, s_in = _ssd_fwd(xbc, dt, a_log, inner, n_groups)
    yn = _ssd_gate_norm_fwd(ysc, xbc, z, dskip_e, b_ng, n_groups)
    h_mix[1] = _mm(yn, w_bo, add=h_l0, name="ssd_out")
    gather_finish(4, h_mix[1])
    take_ffn_in(1)
    take_ffn_out_ple(1)
    h_l1, sv1 = ffn_ple_fwd(h_mix[1], 1)

    dh, dg_fin, loss_row = _final_loss_bwd(h_l1, g_fin, loss_target[0])

    d_norm_ffn, d_norm_ple, d_norm_mix = [None] * depth, [None] * depth, [None] * depth
    for nm in ("ffn_w_gate", "ffn_w_up", "ffn_w_down", "ple_w_proj", "ple_w_gate"):
        grads[nm] = [None] * depth

    def stacked(nm, gfull):
        kind = kind_of[nm]
        if nm in kept_stacked:
            return gfull
        if kind == "row":
            return gfull.reshape(N_CHIPS, -1, gfull.shape[-1])
        if kind == "col":
            n_loc = gfull.shape[1] // N_CHIPS
            return jnp.stack([gfull[:, k * n_loc:(k + 1) * n_loc] for k in range(N_CHIPS)])
        heads, bw, _ = gfull.shape
        return gfull.reshape(heads, N_CHIPS, bw // N_CHIPS, bw).transpose(1, 0, 2, 3).reshape(N_CHIPS, -1, bw)

    reduce_started = []

    def reduce_start(keys, tag):
        gst = [stacked(nm, grads[nm][l]) for nm, l in keys]
        from_sib = _swap_halves(gst, f"reduce_swap_{tag}")
        return reduce_exchange(keys, tag, gst, from_sib)

    def reduce_exchange(keys, tag, gst, from_sib):
        chip_sum = [_add_own_half(g, r, c_idx, f"reduce_add_{nm}{l}") for g, r, (nm, l) in zip(gst, from_sib, keys)]
        send, recv, ents, lands, token = _scatter_start(chip_sum, f"reduce_scatter_start_{tag}")
        reduce_started.append((keys, tag, send, recv, ents, lands))
        return token[0:1, 0:1]

    def reduce_swap_begin(keys, tag):
        gst = [stacked(nm, grads[nm][l]) for nm, l in keys]
        send, recv, ents, lands, token = _swap_start(gst, f"reduce_swap_start_{tag}")
        return (keys, tag, send, recv, ents, lands), token[0:1, 0:1]

    def reduce_swap_end(state, after):
        keys, tag, send, recv, ents, lands = state
        gst, from_sib = _swap_wait(send, recv, ents, lands, after, f"reduce_swap_wait_{tag}")
        return reduce_exchange(keys, tag, gst, from_sib)

    def ffn_ple_keys(l):
        return [("ple_w_gate", l), ("ple_w_proj", l), ("ffn_w_down", l), ("ffn_w_gate", l), ("ffn_w_up", l)]

    def ffn_ple_bwd(dh_out, sv, l, g_ple_l, after_dact=None):
        dgp, dpp = _ple_bwd(dh_out, sv["gp"], sv["pp"])
        grads["ple_w_gate"][l] = _mm(sv["n_p"], dgp, ta=True, out_dtype=BF16, name=f"ple_gate_dw{l}")
        grads["ple_w_proj"][l] = _mm(p[l, 0], dpp, ta=True, out_dtype=BF16, stacked_out=N_CHIPS,
                                     name=f"ple_proj_dw{l}")
        dn = _mm(dgp, w_pg[l], tb=True, name=f"ple_gate_dx{l}")
        dh_f, d_norm_ple[l] = _rmsnorm_bwd(dn, sv["h_f"], g_ple_l, dh_out, f"norm_ple_bwd{l}")
        grads["ffn_w_down"][l] = _mm(sv["act"], dh_f, ta=True, out_dtype=BF16, name=f"ffn_down_dw{l}")
        dact = _mm(dh_f, w_fd[l], tb=True, name=f"ffn_down_dx{l}")
        g_ffn_l = g_ffn[l] if after_dact is None else g_ffn[l] + after_dact(dact)
        dgt, dup = _swiglu_bwd(dact, sv["gt"], sv["up"])
        grads["ffn_w_gate"][l] = _mm(sv["n_f"], dgt, ta=True, out_dtype=BF16, stacked_out=N_CHIPS,
                                     name=f"ffn_gate_dw{l}")
        grads["ffn_w_up"][l] = _mm(sv["n_f"], dup, ta=True, out_dtype=BF16, stacked_out=N_CHIPS, name=f"ffn_up_dw{l}")
        dn = _mm(dgt, w_fg[l], tb=True, stacked_b=True, name=f"ffn_gate_dx{l}")
        dn = _mm(dup, w_fu[l], tb=True, stacked_b=True, add=dn, name=f"ffn_up_dx{l}")
        dh_in, d_norm_ffn[l] = _rmsnorm_bwd(dn, sv["h_in"], g_ffn_l, dh_f, f"norm_ffn_bwd{l}")
        return dh_in

    dh = ffn_ple_bwd(dh, sv1, 1, g_ple[1])
    swapping, tok = reduce_swap_begin(ffn_ple_keys(1), "l1")

    grads["b_w_out"] = [_mm(yn, dh, ta=True, out_dtype=BF16, name="ssd_out_dw")]
    dyn = _mm(dh, w_bo, tb=True, name="ssd_out_dx")
    dy_ssd, dz, d_b_norm_g, dd_lane = _ssd_gate_norm_bwd(dyn, ysc, xbc, z, dskip_e, b_ng + tok, n_groups)
    tok = reduce_swap_end(swapping, dy_ssd)
    dxs, d_bm, d_cm, ddt, d_a_log = _ssd_bwd(xbc, dt, a_log, dy_ssd, ysc, s_in, dskip_e + tok, inner, n_groups)
    dxbc = _put_cols(dxs, d_bm, inner // gn, "ssd_put_db")
    dxbc = _put_cols(dxbc, d_cm, inner // gn + 1, "ssd_put_dc")
    dconv = _silu_conv_bwd_pre(dxbc, xbc_pre, b_cw, b_cb, "ssd_conv_bwd_pre")
    dxbc_pre, d_b_conv_w, d_b_conv_b = _conv_bwd(dconv, xbc_pre, b_cw, "ssd_conv_bwd")
    ddt_pre, d_dt_bias = _dt_bwd(ddt, dt_pre, dt_bias, n_heads, n_groups)
    d_a_log = d_a_log.reshape(n_groups, LANE)[:, :SSD_HEADS_PER_GROUP].reshape(1, n_heads)
    gw_bz = _mm(u1, dz, ta=True, out_dtype=BF16, name="ssd_in_z_dw")
    gw_bx = _mm(u1, dxbc_pre, ta=True, out_dtype=BF16, name="ssd_in_xbc_dw")
    gw_bd = _mm(u1, ddt_pre, ta=True, out_dtype=BF16, name="ssd_in_dt_dw")
    grads["b_w_in"] = [jnp.concatenate([gw_bz, gw_bx, gw_bd[:, :n_heads]], axis=1)]
    du = _mm(dz, w_bz, tb=True, name="ssd_in_z_dx")
    du = _mm(dxbc_pre, w_bx, tb=True, add=du, name="ssd_in_xbc_dx")
    du = _mm(ddt_pre, w_bd, tb=True, add=du, name="ssd_in_dt_dx")
    dh, d_norm_mix[1] = _rmsnorm_bwd(du, h_l0, g_mix[1], dh, "norm_mix_bwd1")
    swapping, tok = reduce_swap_begin([("b_w_out", 0), ("b_w_in", 0)], "ssd")

    dh = ffn_ple_bwd(dh, sv0, 0, g_ple[0] + tok, lambda dact: reduce_swap_end(swapping, dact))
    swapping_l0, tok = reduce_swap_begin(ffn_ple_keys(0), "l0")

    grads["a_w_out"] = [_mm(y_lru, dh, ta=True, out_dtype=BF16, name="lru_out_dw")]
    dy_lru = _mm(dh, w_ao, tb=True, name="lru_out_dx")
    dhs, dyg = _lru_out_bwd(dy_lru, hs, yg)
    g_scan = _scan(lru_a, dhs, True, "lru_scan_bwd")
    dxr, d_wr, d_wi, d_br, d_bi, d_lam = _lru_gates_bwd(xr, g_scan, hs, w_ar, w_ai, a_br, a_bi, a_lambda + tok)
    tok = reduce_swap_end(swapping_l0, dxr)
    dxr_pre, d_a_conv_w, d_a_conv_b = _conv_bwd(dxr, xr_pre, a_cw, "lru_conv_bwd")
    gw_ax = _mm(u0, dxr_pre, ta=True, out_dtype=BF16, stacked_out=N_CHIPS // 2, name="lru_in_x_dw")
    gw_ay = _mm(u0, dyg, ta=True, out_dtype=BF16, stacked_out=N_CHIPS // 2, name="lru_in_y_dw")
    grads["a_w_in"] = [jnp.concatenate([gw_ax, gw_ay], axis=0)]
    grads["a_w_gate_r"] = [d_wr.astype(BF16)]
    grads["a_w_gate_i"] = [d_wi.astype(BF16)]
    du = _mm(dxr_pre, w_ax, tb=True, stacked_b=True, name="lru_in_x_dx")
    du = _mm(dyg, w_ay, tb=True, stacked_b=True, add=du, name="lru_in_y_dx")
    grad_x, d_norm_mix[0] = _rmsnorm_bwd(du, h0, g_mix[0] + tok, dh, "norm_mix_bwd0")

    tok = reduce_start([("a_w_out", 0), ("a_w_in", 0), ("a_w_gate_r", 0), ("a_w_gate_i", 0)], "lru")
    grad_out, delta_out, m_out, v_out = {}, {}, {}, {}

    def reduce_finish(groups, after, tag):
        g_half = {}
        for keys, gtag, send, recv, ents, lands in groups:
            ents, lands = _scatter_wait(send, recv, ents, lands, after, f"reduce_scatter_wait_{gtag}")
            for rx, own, (nm, l) in zip(lands, ents, keys):
                sh = shard2d(nm, w_in[nm])
                g_half[nm] = _sum_into(g_half.get(nm), rx, own, l, qc_vec, (len(sh),) + sh[0].shape,
                                       f"reduce_sum_{nm}{l}")
        nms = list(g_half)
        seen = jnp.zeros((1, 1), F32)
        for nm, gfull in zip(nms, _join_halves([g_half[nm] for nm in nms], f"reduce_join_{tag}")):
            shape, cols = w_in[nm].shape, gfull.shape[-1]
            dl, mn, vn, gout = _adamw(w_in[nm].reshape(-1, cols), gfull.reshape(-1, cols),
                                      m_in[nm].reshape(-1, cols), v_in[nm].reshape(-1, cols), f"adamw_{nm}",
                                      emit_g=True)
            grad_out[nm], delta_out[nm] = gout.reshape(shape), dl.reshape(shape)
            m_out[nm], v_out[nm] = mn.reshape(shape), vn.reshape(shape)
            seen = seen + dl[0:1, 0:1]
        return seen

    done = reduce_finish(reduce_started[:-1], grad_x[0:1, 0:1] + tok, "rest")
    reduce_finish(reduce_started[-1:], done, "lru")

    small_full_grads = {
        "norm_mix_g": jnp.concatenate(d_norm_mix, axis=0), "norm_ffn_g": jnp.concatenate(d_norm_ffn, axis=0),
        "norm_ple_g": jnp.concatenate(d_norm_ple, axis=0), "final_norm_g": dg_fin[0],
        "a_conv_w": d_a_conv_w[None], "a_conv_b": d_a_conv_b,
        "a_b_gate_r": d_br.reshape(a_b_gate_r.shape[0], a_b_gate_r.shape[1], -1),
        "a_b_gate_i": d_bi.reshape(a_b_gate_i.shape[0], a_b_gate_i.shape[1], -1),
        "a_lambda": d_lam, "b_conv_w": d_b_conv_w[None], "b_conv_b": d_b_conv_b,
        "b_dt_bias": d_dt_bias[:, :n_heads], "b_a_log": d_a_log,
        "b_d_skip": dd_lane.reshape(1, n_heads, SSD_HEAD_DIM).sum(axis=-1), "b_norm_g": d_b_norm_g,
    }
    small_names = list(small_full_grads)
    small_shapes = [small_full_grads[nm].shape for nm in small_names]
    packed = _pack([loss_row] + [small_full_grads[nm] for nm in small_names])
    total = _sum_slots(_gather_all(packed, "gather_small_grads"), "sum_small_grads")
    parts = _unpack(total, [(1, LANE)] + small_shapes)
    loss = parts[0][0, 0]
    g_small = {}
    for nm, gfull in zip(small_names, parts[1:]):
        if nm in small_sharded:
            n_loc = w_in[nm].shape[-1]
            gfull = lax.dynamic_slice_in_dim(gfull, q_idx * n_loc, n_loc, axis=gfull.ndim - 1)
        g_small[nm] = gfull

    sm_shapes = [w_in[nm].shape for nm in small_names]
    dl, mn, vn = _adamw(_pack([w_in[nm] for nm in small_names]), _pack([g_small[nm] for nm in small_names]),
                        _pack([m_in[nm] for nm in small_names]), _pack([v_in[nm] for nm in small_names]),
                        "adamw_small")
    for nm, a, b_, c_ in zip(small_names, _unpack(dl, sm_shapes), _unpack(mn, sm_shapes), _unpack(vn, sm_shapes)):
        grad_out[nm] = g_small[nm].reshape(w_in[nm].shape)
        delta_out[nm], m_out[nm], v_out[nm] = a, b_, c_

    return (loss, grad_x[None], *[grad_out[nm] for nm in names], *[delta_out[nm] for nm in names],
            *[m_out[nm] for nm in names], *[v_out[nm] for nm in names])
```

```python
import functools

import jax
import jax.numpy as jnp
from jax import lax
from jax.experimental import pallas as pl
from jax.experimental.pallas import tpu as pltpu

F32 = jnp.float32
BF16 = jnp.bfloat16
MESH = pl.DeviceIdType.MESH
HIGHEST = lax.Precision.HIGHEST

NORM_EPS = 1e-6
LRU_C = 8.0
CONV_WIDTH = 4
SSD_HEAD_DIM = 64
SSD_STATE = 128
SSD_CHUNK = 128
SSD_HEADS_PER_GROUP = 8
LANE = 128
SUBLANE = 8
N_CHIPS = 4
N_DEV = 8
VMEM_LIMIT = 48 * 1024 * 1024

ADAM_LR = 0.001
ADAM_B1 = 0.9
ADAM_B2 = 0.999
ADAM_EPS = 1e-08
ADAM_WD = 0.01
ADAM_STEP = 10


def _tile(n, cap, mult=LANE):
    best = None
    for t in range(mult, min(n, cap) + 1, mult):
        if n % t == 0:
            best = t
    return best if best is not None else n


def _params(*sem):
    return pltpu.CompilerParams(dimension_semantics=sem, vmem_limit_bytes=VMEM_LIMIT)


def _mm(a, b, *, ta=False, tb=False, add=None, out_dtype=F32, stacked_b=False, stacked_out=0, name):
    if ta:
        kd, m = a.shape
    else:
        m, kd = a.shape
    n_loc = None
    if stacked_b:
        slots, kb, n_loc = b.shape
        if tb:
            n, kb = kb, slots * n_loc
        else:
            n = slots * n_loc
    elif tb:
        n, kb = b.shape
    else:
        kb, n = b.shape
    assert kd == kb, (a.shape, b.shape, ta, tb)
    tm = _tile(m, 1408)
    tn = _tile(n, 1408)
    tk = _tile(kd, 2048)
    if stacked_b and tb:
        tk = _tile(n_loc, 1408)
    elif stacked_b:
        tn = _tile(n_loc, 1408)
    if stacked_out:
        n_loc = n // stacked_out
        tn = _tile(n_loc, 1408)
    nk = kd // tk
    dims = (((0 if ta else 1,), (1 if tb else 0,)), ((), ()))

    def body(*refs):
        a_ref, b_ref = refs[:2]
        add_ref = refs[2] if add is not None else None
        o_ref = refs[3] if add is not None else refs[2]
        bv = b_ref[0] if stacked_b else b_ref[...]
        part = lax.dot_general(a_ref[...].astype(BF16), bv.astype(BF16), dims, preferred_element_type=F32)

        def finish(r):
            if add is not None:
                r = r + add_ref[...]
            if stacked_out:
                o_ref[0] = r.astype(out_dtype)
            else:
                o_ref[...] = r.astype(out_dtype)

        if nk == 1:
            finish(part)
        else:
            acc_ref = refs[-1]
            k = pl.program_id(2)

            @pl.when(k == 0)
            def _():
                acc_ref[...] = part

            @pl.when((k > 0) & (k < nk - 1))
            def _():
                acc_ref[...] += part

            @pl.when(k == nk - 1)
            def _():
                finish(acc_ref[...] + part)

    a_spec = pl.BlockSpec((tk, tm), lambda i, j, k: (k, i)) if ta else pl.BlockSpec((tm, tk), lambda i, j, k: (i, k))
    if stacked_b and tb:
        per = n_loc // tk
        b_spec = pl.BlockSpec((1, tn, tk), lambda i, j, k: (k // per, j, k % per))
    elif stacked_b:
        per = n_loc // tn
        b_spec = pl.BlockSpec((1, tk, tn), lambda i, j, k: (j // per, k, j % per))
    elif tb:
        b_spec = pl.BlockSpec((tn, tk), lambda i, j, k: (j, k))
    else:
        b_spec = pl.BlockSpec((tk, tn), lambda i, j, k: (k, j))
    o_spec = pl.BlockSpec((tm, tn), lambda i, j, k: (i, j))
    in_specs = [a_spec, b_spec] + ([o_spec] if add is not None else [])
    args = (a, b) + ((add,) if add is not None else ())
    if stacked_out:
        per_o = n_loc // tn
        out_spec = pl.BlockSpec((1, tm, tn), lambda i, j, k: (j // per_o, i, j % per_o))
        out_shape = jax.ShapeDtypeStruct((stacked_out, m, n_loc), out_dtype)
    else:
        out_spec, out_shape = o_spec, jax.ShapeDtypeStruct((m, n), out_dtype)
    return pl.pallas_call(
        body, name=name, grid=(m // tm, n // tn, nk), in_specs=in_specs, out_specs=out_spec, out_shape=out_shape,
        scratch_shapes=[pltpu.VMEM((tm, tn), F32)] if nk > 1 else [],
        compiler_params=_params("parallel", "parallel", "arbitrary"))(*args)


def _rowwise(name, body, ins, outs, nrows, ts, ncol=1):
    ts = min(ts, nrows)
    nrow = nrows // ts
    hb = ts // SUBLANE
    nb8 = nrows // SUBLANE
    in_specs, args = [], []
    for kind, arr, cb in ins:
        if kind == "row":
            spec = pl.BlockSpec((ts, cb), lambda j, i: (i, j))
        elif kind == "prev":
            spec = pl.BlockSpec((SUBLANE, cb), lambda j, i: (jnp.maximum(i * hb - 1, 0), j))
        elif kind == "next":
            spec = pl.BlockSpec((SUBLANE, cb), lambda j, i: (jnp.minimum((i + 1) * hb, nb8 - 1), j))
        else:
            spec = pl.BlockSpec((arr.shape[0], cb), lambda j, i: (0, j))
        in_specs.append(spec)
        args.append(arr)
    out_specs, out_shape = [], []
    for kind, rows, ctot, cb, dt in outs:
        if kind == "row":
            out_shape.append(jax.ShapeDtypeStruct((nrows, ctot), dt))
            out_specs.append(pl.BlockSpec((ts, cb), lambda j, i: (i, j)))
        else:
            out_shape.append(jax.ShapeDtypeStruct((rows, ctot), dt))
            out_specs.append(pl.BlockSpec((rows, cb), lambda j, i: (0, j)))

    def kern(*refs):
        body(pl.program_id(1), nrow, *refs)

    return pl.pallas_call(kern, name=name, grid=(ncol, nrow), in_specs=in_specs, out_specs=out_specs,
                          out_shape=out_shape, compiler_params=_params("parallel", "arbitrary"))(*args)


def _colsum(x):
    return jnp.sum(x, axis=0, keepdims=True)


def _acc(i, ref, val):
    @pl.when(i == 0)
    def _():
        ref[...] = val

    @pl.when(i > 0)
    def _():
        ref[...] += val


def _shift_down(x, halo, k):
    xx = jnp.concatenate([halo, x], axis=0)
    return pltpu.roll(xx, k, axis=0)[SUBLANE:, :]


def _shift_up(x, halo, k):
    xx = jnp.concatenate([x, halo], axis=0)
    n = xx.shape[0]
    return pltpu.roll(xx, n - k, axis=0)[: x.shape[0], :]


def _sigmoid(x):
    return 1.0 / (1.0 + jnp.exp(-x))


def _silu(x):
    return x * _sigmoid(x)


def _dsilu(x):
    s = _sigmoid(x)
    return s * (1.0 + x * (1.0 - s))


_GELU_K = 0.7978845608028654
_GELU_C = 0.044715


def _gelu(x):
    return 0.5 * x * (1.0 + jnp.tanh(_GELU_K * (x + _GELU_C * x * x * x)))


def _dgelu(x):
    t = jnp.tanh(_GELU_K * (x + _GELU_C * x * x * x))
    return 0.5 * (1.0 + t) + 0.5 * x * (1.0 - t * t) * _GELU_K * (1.0 + 3.0 * _GELU_C * x * x)


def _softplus(x):
    return jnp.maximum(x, 0.0) + jnp.log1p(jnp.exp(-jnp.abs(x)))


def _neg_expm1(x):
    poly = -x * (1.0 + x * (0.5 + x * (1.0 / 6.0 + x * (1.0 / 24.0 + x * (1.0 / 120.0)))))
    return jnp.where(x > -0.05, poly, 1.0 - jnp.exp(x))


def _rmsnorm_fwd(h, g, name):
    s, d = h.shape

    def body(i, n, h_ref, g_ref, o_ref):
        x = h_ref[...]
        r = lax.rsqrt(jnp.mean(x * x, axis=-1, keepdims=True) + NORM_EPS)
        o_ref[...] = (x * r * g_ref[...]).astype(BF16)

    return _rowwise(name, body, [("row", h, d), ("vec", g, d)], [("row", None, d, d, BF16)], s, 256)[0]


def _rmsnorm_bwd(dn, h, g, dres, name):
    s, d = h.shape

    def body(i, n, dn_ref, h_ref, g_ref, dres_ref, dh_ref, dg_ref):
        x = h_ref[...]
        dy = dn_ref[...].astype(F32)
        r = lax.rsqrt(jnp.mean(x * x, axis=-1, keepdims=True) + NORM_EPS)
        xh = x * r
        _acc(i, dg_ref, _colsum(dy * xh))
        dxh = dy * g_ref[...]
        dh_ref[...] = dres_ref[...] + r * (dxh - xh * jnp.mean(dxh * xh, axis=-1, keepdims=True))

    return _rowwise(name, body, [("row", dn, d), ("row", h, d), ("vec", g, d), ("row", dres, d)],
                    [("row", None, d, d, F32), ("acc", 1, d, d, F32)], s, 256)


def _final_loss_bwd(h, g, tgt):
    s, d = h.shape

    def body(i, n, h_ref, g_ref, t_ref, dh_ref, dg_ref, loss_ref):
        x = h_ref[...]
        gg = g_ref[...]
        r = lax.rsqrt(jnp.mean(x * x, axis=-1, keepdims=True) + NORM_EPS)
        xh = x * r
        err = xh * gg - t_ref[...]
        part = 0.5 * jnp.sum(jnp.mean(err * err, axis=-1, keepdims=True), axis=0, keepdims=True)
        _acc(i, loss_ref, jnp.broadcast_to(part, (1, LANE)))
        dy = err * (1.0 / d)
        _acc(i, dg_ref, _colsum(dy * xh))
        dxh = dy * gg
        dh_ref[...] = r * (dxh - xh * jnp.mean(dxh * xh, axis=-1, keepdims=True))

    return _rowwise("final_loss_bwd", body, [("row", h, d), ("vec", g, d), ("row", tgt, d)],
                    [("row", None, d, d, F32), ("acc", 1, d, d, F32), ("acc", 1, LANE, LANE, F32)], s, 256)


def _conv_rows(x, halo, w, b):
    y = b + w[3:4, :] * x
    for k in range(CONV_WIDTH - 1):
        y = y + w[k:k + 1, :] * _shift_down(x, halo, CONV_WIDTH - 1 - k)
    return y


def _conv_fwd(x, w, b, silu, name):
    s, c = x.shape
    cb = _tile(c, 512)

    def body(i, n, x_ref, p_ref, w_ref, b_ref, o_ref):
        halo = jnp.where(i == 0, 0.0, p_ref[...])
        y = _conv_rows(x_ref[...], halo, w_ref[...], b_ref[...])
        o_ref[...] = _silu(y) if silu else y

    return _rowwise(name, body, [("row", x, cb), ("prev", x, cb), ("vec", w, cb), ("vec", b, cb)],
                    [("row", None, c, cb, F32)], s, 512, ncol=c // cb)[0]


def _silu_conv_bwd_pre(dy, x, w, b, name):
    s, c = x.shape
    cb = _tile(c, 512)

    def body(i, n, dy_ref, x_ref, p_ref, w_ref, b_ref, o_ref):
        halo = jnp.where(i == 0, 0.0, p_ref[...])
        y = _conv_rows(x_ref[...], halo, w_ref[...], b_ref[...])
        o_ref[...] = dy_ref[...] * _dsilu(y)

    return _rowwise(name, body, [("row", dy, cb), ("row", x, cb), ("prev", x, cb), ("vec", w, cb), ("vec", b, cb)],
                    [("row", None, c, cb, F32)], s, 512, ncol=c // cb)[0]


def _conv_bwd(dy, x, w, name):
    s, c = x.shape
    cb = _tile(c, 512)

    def body(i, n, dy_ref, nx_ref, x_ref, p_ref, w_ref, dx_ref, dw_ref, db_ref):
        d = dy_ref[...]
        xx = x_ref[...]
        wv = w_ref[...]
        nxt = jnp.where(i == n - 1, 0.0, nx_ref[...])
        prv = jnp.where(i == 0, 0.0, p_ref[...])
        dx = wv[3:4, :] * d
        parts = []
        for k in range(CONV_WIDTH - 1):
            sh = CONV_WIDTH - 1 - k
            dx = dx + wv[k:k + 1, :] * _shift_up(d, nxt, sh)
            parts.append(_colsum(d * _shift_down(xx, prv, sh)))
        parts.append(_colsum(d * xx))
        dx_ref[...] = dx.astype(BF16)
        _acc(i, dw_ref, jnp.concatenate(parts, axis=0))
        _acc(i, db_ref, _colsum(d))

    return _rowwise(name, body, [("row", dy, cb), ("next", dy, cb), ("row", x, cb), ("prev", x, cb), ("vec", w, cb)],
                    [("row", None, c, cb, BF16), ("acc", CONV_WIDTH, c, cb, F32), ("acc", 1, c, cb, F32)],
                    s, 512, ncol=c // cb)


def _lru_gate_math(xr, r_pre, i_pre, lam):
    r = _sigmoid(r_pre)
    ig = _sigmoid(i_pre)
    sp = _softplus(-lam)
    log_a = -LRU_C * r * sp
    a = jnp.exp(log_a)
    mult = jnp.sqrt(_neg_expm1(2.0 * log_a))
    return r, ig, sp, a, mult


def _lru_gates_fwd(xr, wr, wi, br, bi, lam):
    s, d = xr.shape
    nh, bw, _ = wr.shape
    ts = min(512, s)

    def body(x_ref, wr_ref, wi_ref, br_ref, bi_ref, lam_ref, a_ref, b_ref):
        x = x_ref[...]
        xb = x.astype(BF16)
        r_pre = jnp.dot(xb, wr_ref[0], preferred_element_type=F32) + br_ref[...]
        i_pre = jnp.dot(xb, wi_ref[0], preferred_element_type=F32) + bi_ref[...]
        _, ig, _, a, mult = _lru_gate_math(x, r_pre, i_pre, lam_ref[...])
        a_ref[...] = a
        b_ref[...] = mult * (ig * x)

    row = pl.BlockSpec((ts, bw), lambda h, i: (i, h))
    wsp = pl.BlockSpec((1, bw, bw), lambda h, i: (h, 0, 0))
    vec = pl.BlockSpec((1, bw), lambda h, i: (0, h))
    return pl.pallas_call(
        body, name="lru_gates_fwd", grid=(nh, s // ts), in_specs=[row, wsp, wsp, vec, vec, vec], out_specs=[row, row],
        out_shape=[jax.ShapeDtypeStruct((s, d), F32)] * 2, compiler_params=_params("parallel", "arbitrary"),
    )(xr, wr, wi, br, bi, lam)


def _lru_gates_bwd(xr, g, hs, wr, wi, br, bi, lam):
    s, d = xr.shape
    nh, bw, _ = wr.shape
    ts = min(512, s)
    hb = ts // SUBLANE
    tn_dims = (((0,), (0,)), ((), ()))
    nt_dims = (((1,), (1,)), ((), ()))

    def body(x_ref, g_ref, hs_ref, hp_ref, wr_ref, wi_ref, br_ref, bi_ref, lam_ref,
             dx_ref, dwr_ref, dwi_ref, dbr_ref, dbi_ref, dlam_ref):
        i = pl.program_id(1)
        x = x_ref[...]
        xb = x.astype(BF16)
        gg = g_ref[...]
        lam_v = lam_ref[...]
        r_pre = jnp.dot(xb, wr_ref[0], preferred_element_type=F32) + br_ref[...]
        i_pre = jnp.dot(xb, wi_ref[0], preferred_element_type=F32) + bi_ref[...]
        r, ig, sp, a, mult = _lru_gate_math(x, r_pre, i_pre, lam_v)
        h_prev = _shift_down(hs_ref[...], jnp.where(i == 0, 0.0, hp_ref[...]), 1)
        da = gg * h_prev
        dmult = gg * ig * x
        dlog_a = da * a - dmult * (a * a) / mult
        d_r = dlog_a * (-LRU_C * sp)
        dr_pre = d_r * r * (1.0 - r)
        di_pre = (gg * mult * x) * ig * (1.0 - ig)
        drb = dr_pre.astype(BF16)
        dib = di_pre.astype(BF16)
        dx_ref[...] = (gg * mult * ig
                       + lax.dot_general(drb, wr_ref[0], nt_dims, preferred_element_type=F32)
                       + lax.dot_general(dib, wi_ref[0], nt_dims, preferred_element_type=F32))
        dwr = lax.dot_general(xb, drb, tn_dims, preferred_element_type=F32)[None]
        dwi = lax.dot_general(xb, dib, tn_dims, preferred_element_type=F32)[None]
        dlam = _colsum(dlog_a * (-LRU_C * r)) * (-_sigmoid(-lam_v))
        _acc(i, dwr_ref, dwr)
        _acc(i, dwi_ref, dwi)
        _acc(i, dbr_ref, _colsum(dr_pre))
        _acc(i, dbi_ref, _colsum(di_pre))
        _acc(i, dlam_ref, dlam)

    row = pl.BlockSpec((ts, bw), lambda h, i: (i, h))
    prev = pl.BlockSpec((SUBLANE, bw), lambda h, i: (jnp.maximum(i * hb - 1, 0), h))
    wsp = pl.BlockSpec((1, bw, bw), lambda h, i: (h, 0, 0))
    vec = pl.BlockSpec((1, bw), lambda h, i: (0, h))
    return pl.pallas_call(
        body, name="lru_gates_bwd", grid=(nh, s // ts),
        in_specs=[row, row, row, prev, wsp, wsp, vec, vec, vec], out_specs=[row, wsp, wsp, vec, vec, vec],
        out_shape=[jax.ShapeDtypeStruct((s, d), F32), jax.ShapeDtypeStruct((nh, bw, bw), F32),
                   jax.ShapeDtypeStruct((nh, bw, bw), F32)] + [jax.ShapeDtypeStruct((1, d), F32)] * 3,
        compiler_params=_params("parallel", "arbitrary"),
    )(xr, g, hs, hs, wr, wi, br, bi, lam)


def _scan(a, b, reverse, name):
    s, c = a.shape
    cb = _tile(c, 512)
    nt = s // SUBLANE

    def body(a_ref, b_ref, o_ref):
        row = lax.broadcasted_iota(jnp.int32, (SUBLANE, cb), 0)

        def fwd_step(t, carry):
            r0 = pl.multiple_of(t * SUBLANE, SUBLANE)
            aa = a_ref[pl.ds(r0, SUBLANE), :]
            bb = b_ref[pl.ds(r0, SUBLANE), :]
            for sh in (1, 2, 4):
                a_s = jnp.where(row >= sh, pltpu.roll(aa, sh, axis=0), 1.0)
                b_s = jnp.where(row >= sh, pltpu.roll(bb, sh, axis=0), 0.0)
                bb = aa * b_s + bb
                aa = aa * a_s
            h = bb + aa * carry
            o_ref[pl.ds(r0, SUBLANE), :] = h
            return h[SUBLANE - 1:SUBLANE, :]

        def rev_step(k, carry):
            r0 = pl.multiple_of((nt - 1 - k) * SUBLANE, SUBLANE)
            aa = a_ref[pl.ds(r0, SUBLANE), :]
            dd = b_ref[pl.ds(r0, SUBLANE), :]
            cc = aa * dd
            for sh in (1, 2, 4):
                a_s = jnp.where(row < SUBLANE - sh, pltpu.roll(aa, SUBLANE - sh, axis=0), 1.0)
                c_s = jnp.where(row < SUBLANE - sh, pltpu.roll(cc, SUBLANE - sh, axis=0), 0.0)
                cc = cc + aa * c_s
                aa = aa * a_s
            big = cc + aa * carry
            nxt = jnp.where(row < SUBLANE - 1, pltpu.roll(big, SUBLANE - 1, axis=0), carry)
            o_ref[pl.ds(r0, SUBLANE), :] = dd + nxt
            return big[0:1, :]

        lax.fori_loop(0, nt, rev_step if reverse else fwd_step, jnp.zeros((1, cb), F32))

    spec = pl.BlockSpec((s, cb), lambda j: (0, j))
    return pl.pallas_call(body, name=name, grid=(c // cb,), in_specs=[spec, spec], out_specs=spec,
                          out_shape=jax.ShapeDtypeStruct((s, c), F32), compiler_params=_params("parallel"))(a, b)


def _lru_out_fwd(hs, yg):
    s, d = hs.shape
    cb = _tile(d, 1024)

    def body(i, n, h_ref, y_ref, o_ref):
        o_ref[...] = (h_ref[...] * _gelu(y_ref[...])).astype(BF16)

    return _rowwise("lru_out_fwd", body, [("row", hs, cb), ("row", yg, cb)], [("row", None, d, cb, BF16)],
                    s, 512, ncol=d // cb)[0]


def _lru_out_bwd(dy, hs, yg):
    s, d = hs.shape
    cb = _tile(d, 1024)

    def body(i, n, dy_ref, h_ref, y_ref, dh_ref, dyg_ref):
        dyv = dy_ref[...]
        y = y_ref[...]
        dh_ref[...] = dyv * _gelu(y)
        dyg_ref[...] = (dyv * h_ref[...] * _dgelu(y)).astype(BF16)

    return _rowwise("lru_out_bwd", body, [("row", dy, cb), ("row", hs, cb), ("row", yg, cb)],
                    [("row", None, d, cb, F32), ("row", None, d, cb, BF16)], s, 512, ncol=d // cb)


def _swiglu_act(gt, up):
    s, f = gt.shape
    cb = _tile(f, 1024)

    def body(i, n, g_ref, u_ref, o_ref):
        o_ref[...] = (_silu(g_ref[...].astype(F32)) * u_ref[...].astype(F32)).astype(BF16)

    return _rowwise("swiglu_act", body, [("row", gt, cb), ("row", up, cb)], [("row", None, f, cb, BF16)],
                    s, 512, ncol=f // cb)[0]


def _swiglu_bwd(dact, gt, up):
    s, f = gt.shape
    cb = _tile(f, 1024)

    def body(i, n, d_ref, g_ref, u_ref, dg_ref, du_ref):
        d = d_ref[...]
        g = g_ref[...].astype(F32)
        dg_ref[...] = (d * u_ref[...].astype(F32) * _dsilu(g)).astype(BF16)
        du_ref[...] = (d * _silu(g)).astype(BF16)

    return _rowwise("swiglu_bwd", body, [("row", dact, cb), ("row", gt, cb), ("row", up, cb)],
                    [("row", None, f, cb, BF16), ("row", None, f, cb, BF16)], s, 512, ncol=f // cb)


def _ple_fwd(h, gp, pp):
    s, d = h.shape
    cb = _tile(d, 1024)

    def body(i, n, h_ref, g_ref, p_ref, o_ref):
        o_ref[...] = h_ref[...] + _sigmoid(g_ref[...]) * p_ref[...]

    return _rowwise("ple_fwd", body, [("row", h, cb), ("row", gp, cb), ("row", pp, cb)], [("row", None, d, cb, F32)],
                    s, 512, ncol=d // cb)[0]


def _ple_bwd(dh, gp, pp):
    s, d = dh.shape
    cb = _tile(d, 1024)

    def body(i, n, d_ref, g_ref, p_ref, dg_ref, dp_ref):
        dv = d_ref[...]
        sg = _sigmoid(g_ref[...])
        dg_ref[...] = (dv * p_ref[...] * sg * (1.0 - sg)).astype(BF16)
        dp_ref[...] = (dv * sg).astype(BF16)

    return _rowwise("ple_bwd", body, [("row", dh, cb), ("row", gp, cb), ("row", pp, cb)],
                    [("row", None, d, cb, BF16), ("row", None, d, cb, BF16)], s, 512, ncol=d // cb)


def _group_matrix(n_groups):
    r = lax.broadcasted_iota(jnp.int32, (LANE, n_groups * LANE), 0)
    c = lax.broadcasted_iota(jnp.int32, (LANE, n_groups * LANE), 1)
    return ((c % LANE < SSD_HEADS_PER_GROUP) & (r == (c // LANE) * SSD_HEADS_PER_GROUP + c % LANE)).astype(F32)


def _dt_fwd(dt_pre, bias, n_heads, n_groups):
    s = dt_pre.shape[0]
    gl = n_groups * LANE

    def body(i, n, d_ref, b_ref, o_ref):
        lane = lax.broadcasted_iota(jnp.int32, d_ref.shape, 1)
        v = jnp.where(lane < n_heads, _softplus(d_ref[...] + b_ref[...]), 0.0)
        o_ref[...] = jnp.dot(v, _group_matrix(n_groups), preferred_element_type=F32, precision=HIGHEST)

    return _rowwise("ssd_dt_fwd", body, [("row", dt_pre, LANE), ("vec", bias, LANE)], [("row", None, gl, gl, F32)],
                    s, 512)[0]


def _dt_bwd(ddt_g, dt_pre, bias, n_heads, n_groups):
    s = dt_pre.shape[0]
    gl = n_groups * LANE

    def body(i, n, g_ref, d_ref, b_ref, o_ref, db_ref):
        lane = lax.broadcasted_iota(jnp.int32, d_ref.shape, 1)
        ddt = lax.dot_general(g_ref[...], _group_matrix(n_groups), _NT, preferred_element_type=F32, precision=HIGHEST)
        v = jnp.where(lane < n_heads, ddt * _sigmoid(d_ref[...] + b_ref[...]), 0.0)
        o_ref[...] = v.astype(BF16)
        _acc(i, db_ref, _colsum(v))

    return _rowwise("ssd_dt_bwd", body, [("row", ddt_g, gl), ("row", dt_pre, LANE), ("vec", bias, LANE)],
                    [("row", None, LANE, LANE, BF16), ("acc", 1, LANE, LANE, F32)], s, 512)


def _ssd_chunk_terms(dt, alog):
    ln = dt.shape[0]
    a_neg = -jnp.exp(alog)
    row = lax.broadcasted_iota(jnp.int32, (ln, ln), 0)
    col = lax.broadcasted_iota(jnp.int32, (ln, ln), 1)
    tril = row >= col
    cs = jnp.dot(tril.astype(F32), dt * a_neg, preferred_element_type=F32, precision=HIGHEST)
    return a_neg, cs, tril


def _head_lanes(v):
    return [jnp.broadcast_to(v[:, e:e + 1], v.shape) for e in range(SSD_HEADS_PER_GROUP)]


def _ssd_head_scores(bc_cs, cst, cb_mat, tril, e):
    lm = jnp.where(tril, jnp.exp(jnp.minimum(bc_cs[e] - cst[e:e + 1, :], 0.0)), 0.0)
    return (cb_mat * lm).astype(BF16), lm


_NT = (((1,), (1,)), ((), ()))
_TN = (((0,), (0,)), ((), ()))


def _ssd_fwd(xbc, dt, alog, inner, n_groups):
    s = xbc.shape[0]
    ln = SSD_CHUNK
    nc = s // ln
    gw = SSD_HEADS_PER_GROUP * SSD_HEAD_DIM
    npair = gw // LANE
    boff = inner // LANE

    def body(xs_ref, b_ref, c_ref, dt_ref, alog_ref, y_ref, sin_ref, st_ref):
        c = pl.program_id(1)

        @pl.when(c == 0)
        def _():
            st_ref[...] = jnp.zeros_like(st_ref)

        dtv = dt_ref[...]
        _, cs, tril = _ssd_chunk_terms(dtv, alog_ref[...])
        cst = cs.T
        bc_cs, bc_dt = _head_lanes(cs), _head_lanes(dtv)
        xs = xs_ref[...]
        bg = b_ref[...].astype(BF16)
        cg = c_ref[...].astype(BF16)
        cb_mat = lax.dot_general(cg, bg, _NT, preferred_element_type=F32)
        sg = st_ref[...]
        sin_ref[0] = sg
        lo = lax.broadcasted_iota(jnp.int32, (ln, LANE), 1) < SSD_HEAD_DIM
        ys, news = [], []
        for pr in range(npair):
            cols = slice(LANE * pr, LANE * (pr + 1))
            cs_p = jnp.where(lo, bc_cs[2 * pr], bc_cs[2 * pr + 1])
            x = xs[:, cols] * jnp.where(lo, bc_dt[2 * pr], bc_dt[2 * pr + 1])
            tot_p = cs_p[ln - 1:ln, :]
            xp = x.astype(BF16)
            xd = (x * jnp.exp(tot_p - cs_p)).astype(BF16)
            zero = jnp.zeros_like(xp)
            sc0 = _ssd_head_scores(bc_cs, cst, cb_mat, tril, 2 * pr)[0]
            sc1 = _ssd_head_scores(bc_cs, cst, cb_mat, tril, 2 * pr + 1)[0]
            acc = jnp.dot(sc0, jnp.where(lo, xp, zero), preferred_element_type=F32)
            acc = acc + jnp.dot(sc1, jnp.where(lo, zero, xp), preferred_element_type=F32)
            sp = sg[:, cols]
            yoff = jnp.dot(cg, sp.astype(BF16), preferred_element_type=F32) * jnp.exp(cs_p)
            ys.append(acc + yoff)
            news.append(jnp.exp(tot_p) * sp + lax.dot_general(bg, xd, _TN, preferred_element_type=F32))
        y_ref[...] = jnp.concatenate(ys, axis=1)
        st_ref[...] = jnp.concatenate(news, axis=1)

    in_specs = [pl.BlockSpec((ln, gw), lambda g, c: (c, g)),
                pl.BlockSpec((ln, SSD_STATE), lambda g, c: (c, boff + g)),
                pl.BlockSpec((ln, SSD_STATE), lambda g, c: (c, boff + n_groups + g)),
                pl.BlockSpec((ln, LANE), lambda g, c: (c, g)),
                pl.BlockSpec((1, LANE), lambda g, c: (0, g))]
    out_specs = [pl.BlockSpec((ln, gw), lambda g, c: (c, g)),
                 pl.BlockSpec((1, SSD_STATE, gw), lambda g, c: (c, 0, g))]
    return pl.pallas_call(
        body, name="ssd_fwd", grid=(n_groups, nc), in_specs=in_specs, out_specs=out_specs,
        out_shape=[jax.ShapeDtypeStruct((s, inner), F32), jax.ShapeDtypeStruct((nc, SSD_STATE, inner), F32)],
        scratch_shapes=[pltpu.VMEM((SSD_STATE, gw), F32)],
        compiler_params=_params("parallel", "arbitrary"),
    )(xbc, xbc, xbc, dt, alog)


def _ssd_bwd(xbc, dt, alog, dy, y, sin, dskip_e, inner, n_groups):
    s = xbc.shape[0]
    ln = SSD_CHUNK
    nc = s // ln
    gw = SSD_HEADS_PER_GROUP * SSD_HEAD_DIM
    npair = gw // LANE
    boff = inner // LANE

    def body(xs_ref, b_ref, c_ref, dt_ref, alog_ref, dy_ref, y_ref, sin_ref, sout_ref, dsk_ref,
             dxs_ref, db_ref, dc_ref, ddt_ref, dalog_ref, ds_ref):
        step = pl.program_id(1)

        @pl.when(step == 0)
        def _():
            ds_ref[...] = jnp.zeros_like(ds_ref)

        dtv = dt_ref[...]
        a_neg, cs, tril = _ssd_chunk_terms(dtv, alog_ref[...])
        cst = cs.T
        bc_cs, bc_dt = _head_lanes(cs), _head_lanes(dtv)
        xs = xs_ref[...]
        bg = b_ref[...].astype(BF16)
        cg = c_ref[...].astype(BF16)
        cb_mat = lax.dot_general(cg, bg, _NT, preferred_element_type=F32)
        dyv = dy_ref[...]
        yv = y_ref[...]
        dskv = dsk_ref[...]
        s_in = sin_ref[0]
        s_out = sout_ref[0]
        d_s = ds_ref[...]
        lane = lax.broadcasted_iota(jnp.int32, (ln, LANE), 1)
        rowl = lax.broadcasted_iota(jnp.int32, (ln, LANE), 0)
        lo = lane < SSD_HEAD_DIM
        dcb = jnp.zeros((ln, ln), F32)
        dbg = jnp.zeros((ln, SSD_STATE), F32)
        dcg = jnp.zeros((ln, SSD_STATE), F32)
        dcs = jnp.zeros((ln, LANE), F32)
        ddt_x = jnp.zeros((ln, LANE), F32)
        dxs_parts, nds = [], []

        def head_sums(v, pr, into):
            s0 = jnp.sum(jnp.where(lo, v, 0.0), axis=1, keepdims=True)
            s1 = jnp.sum(jnp.where(lo, 0.0, v), axis=1, keepdims=True)
            return into + jnp.where(lane == 2 * pr, s0, 0.0) + jnp.where(lane == 2 * pr + 1, s1, 0.0)

        for pr in range(npair):
            cols = slice(LANE * pr, LANE * (pr + 1))
            cs_p = jnp.where(lo, bc_cs[2 * pr], bc_cs[2 * pr + 1])
            dt_p = jnp.where(lo, bc_dt[2 * pr], bc_dt[2 * pr + 1])
            xs_p = xs[:, cols]
            x = xs_p * dt_p
            tot_p = cs_p[ln - 1:ln, :]
            dec = jnp.exp(tot_p - cs_p)
            xp = x.astype(BF16)
            xd = (x * dec).astype(BF16)
            dy_p = dyv[:, cols]
            dyp = dy_p.astype(BF16)
            dye = (jnp.exp(cs_p) * dy_p).astype(BF16)
            zero = jnp.zeros_like(dyp)
            dxp = jnp.zeros((ln, LANE), F32)
            for e, dym in ((2 * pr, jnp.where(lo, dyp, zero)), (2 * pr + 1, jnp.where(lo, zero, dyp))):
                sc, lm = _ssd_head_scores(bc_cs, cst, cb_mat, tril, e)
                dsc = lax.dot_general(dym, xp, _NT, preferred_element_type=F32)
                dcb = dcb + dsc * lm
                dxp = dxp + lax.dot_general(sc, dym, _TN, preferred_element_type=F32)
            dsp = d_s[:, cols]
            dspb = dsp.astype(BF16)
            dxp = dxp + dec * jnp.dot(bg, dspb, preferred_element_type=F32)
            dcg = dcg + lax.dot_general(dye, s_in[:, cols].astype(BF16), _NT, preferred_element_type=F32)
            dbg = dbg + lax.dot_general(xd, dspb, _NT, preferred_element_type=F32)
            nds.append(jnp.exp(tot_p) * dsp + lax.dot_general(cg, dye, _TN, preferred_element_type=F32))
            dxs_parts.append(dxp * dt_p + dy_p * dskv[:, cols])
            dcs = head_sums(yv[:, cols] * dyp.astype(F32) - xp.astype(F32) * dxp, pr, dcs)
            tot_row = jnp.broadcast_to(_colsum(s_out[:, cols] * dsp), (ln, LANE))
            dcs = head_sums(jnp.where(rowl == ln - 1, tot_row, 0.0), pr, dcs)
            ddt_x = head_sums(dxp * xs_p, pr, ddt_x)
        ds_ref[...] = jnp.concatenate(nds, axis=1)
        dxs_ref[...] = jnp.concatenate(dxs_parts, axis=1)
        dcbb = dcb.astype(BF16)
        dc_ref[...] = dcg + jnp.dot(dcbb, bg, preferred_element_type=F32)
        db_ref[...] = dbg + lax.dot_general(dcbb, cg, _TN, preferred_element_type=F32)
        row = lax.broadcasted_iota(jnp.int32, (ln, ln), 0)
        col = lax.broadcasted_iota(jnp.int32, (ln, ln), 1)
        dadt = jnp.dot((row <= col).astype(F32), dcs, preferred_element_type=F32, precision=HIGHEST)
        ddt_ref[...] = a_neg * dadt + ddt_x
        _acc(step, dalog_ref, _colsum(dadt * dtv) * a_neg)

    def rc(step):
        return nc - 1 - step

    in_specs = [pl.BlockSpec((ln, gw), lambda g, t: (rc(t), g)),
                pl.BlockSpec((ln, SSD_STATE), lambda g, t: (rc(t), boff + g)),
                pl.BlockSpec((ln, SSD_STATE), lambda g, t: (rc(t), boff + n_groups + g)),
                pl.BlockSpec((ln, LANE), lambda g, t: (rc(t), g)),
                pl.BlockSpec((1, LANE), lambda g, t: (0, g)),
                pl.BlockSpec((ln, gw), lambda g, t: (rc(t), g)),
                pl.BlockSpec((ln, gw), lambda g, t: (rc(t), g)),
                pl.BlockSpec((1, SSD_STATE, gw), lambda g, t: (rc(t), 0, g)),
                pl.BlockSpec((1, SSD_STATE, gw), lambda g, t: (jnp.minimum(rc(t) + 1, nc - 1), 0, g)),
                pl.BlockSpec((1, gw), lambda g, t: (0, g))]
    out_specs = [pl.BlockSpec((ln, gw), lambda g, t: (rc(t), g)),
                 pl.BlockSpec((ln, SSD_STATE), lambda g, t: (rc(t), g)),
                 pl.BlockSpec((ln, SSD_STATE), lambda g, t: (rc(t), g)),
                 pl.BlockSpec((ln, LANE), lambda g, t: (rc(t), g)),
                 pl.BlockSpec((1, LANE), lambda g, t: (0, g))]
    gn = n_groups * SSD_STATE
    return pl.pallas_call(
        body, name="ssd_bwd", grid=(n_groups, nc), in_specs=in_specs, out_specs=out_specs,
        out_shape=[jax.ShapeDtypeStruct((s, inner + 2 * gn), F32), jax.ShapeDtypeStruct((s, gn), F32),
                   jax.ShapeDtypeStruct((s, gn), F32), jax.ShapeDtypeStruct((s, n_groups * LANE), F32),
                   jax.ShapeDtypeStruct((1, n_groups * LANE), F32)],
        scratch_shapes=[pltpu.VMEM((SSD_STATE, gw), F32)],
        compiler_params=_params("parallel", "arbitrary"),
    )(xbc, xbc, xbc, dt, alog, dy, y, sin, sin, dskip_e)


def _put_cols(buf, part, block, name):
    s, w = part.shape
    ts = min(512, s)

    def body(b_ref, p_ref, o_ref):
        o_ref[...] = p_ref[...]

    return pl.pallas_call(
        body, name=name, grid=(s // ts,), in_specs=[_ANY, pl.BlockSpec((ts, w), lambda i: (i, 0))],
        out_specs=pl.BlockSpec((ts, w), lambda i: (i, block)), out_shape=jax.ShapeDtypeStruct(buf.shape, buf.dtype),
        input_output_aliases={0: 0}, compiler_params=_params("parallel"))(buf, part)


def _ssd_gate_norm_fwd(ysc, xbc, z, dskip_e, norm_g, n_groups):
    s, inner = ysc.shape
    gsz = inner // n_groups

    def body(i, n, y_ref, x_ref, z_ref, d_ref, g_ref, o_ref):
        y2 = (y_ref[...] + d_ref[...] * x_ref[...]) * _silu(z_ref[...])
        gg = g_ref[...]
        outs = []
        for k in range(n_groups):
            cols = slice(k * gsz, (k + 1) * gsz)
            v = y2[:, cols]
            r = lax.rsqrt(jnp.mean(v * v, axis=-1, keepdims=True) + NORM_EPS)
            outs.append(v * r * gg[:, cols])
        o_ref[...] = jnp.concatenate(outs, axis=1).astype(BF16)

    return _rowwise("ssd_gate_norm_fwd", body,
                    [("row", ysc, inner), ("row", xbc, inner), ("row", z, inner), ("vec", dskip_e, inner),
                     ("vec", norm_g, inner)], [("row", None, inner, inner, BF16)], s, 128)[0]


def _ssd_gate_norm_bwd(dyn, ysc, xbc, z, dskip_e, norm_g, n_groups):
    s, inner = ysc.shape
    gsz = inner // n_groups

    def body(i, n, dn_ref, y_ref, x_ref, z_ref, d_ref, g_ref, dy_ref, dz_ref, dg_ref, dd_ref):
        xs = x_ref[...]
        zz = z_ref[...]
        y = y_ref[...] + d_ref[...] * xs
        sz = _silu(zz)
        y2 = y * sz
        dn = dn_ref[...]
        gg = g_ref[...]
        dy2s, dgs = [], []
        for k in range(n_groups):
            cols = slice(k * gsz, (k + 1) * gsz)
            v = y2[:, cols]
            d = dn[:, cols]
            r = lax.rsqrt(jnp.mean(v * v, axis=-1, keepdims=True) + NORM_EPS)
            vh = v * r
            dgs.append(_colsum(d * vh))
            dvh = d * gg[:, cols]
            dy2s.append(r * (dvh - vh * jnp.mean(dvh * vh, axis=-1, keepdims=True)))
        dy2 = jnp.concatenate(dy2s, axis=1)
        dy = dy2 * sz
        dy_ref[...] = dy
        dz_ref[...] = (dy2 * y * _dsilu(zz)).astype(BF16)
        _acc(i, dg_ref, jnp.concatenate(dgs, axis=1))
        _acc(i, dd_ref, _colsum(dy * xs))

    return _rowwise("ssd_gate_norm_bwd", body,
                    [("row", dyn, inner), ("row", ysc, inner), ("row", xbc, inner), ("row", z, inner),
                     ("vec", dskip_e, inner), ("vec", norm_g, inner)],
                    [("row", None, inner, inner, F32), ("row", None, inner, inner, BF16),
                     ("acc", 1, inner, inner, F32), ("acc", 1, inner, inner, F32)], s, 128)


def _adamw(w, g, m, v, name, emit_g=False):
    rows, c = w.shape
    bc1 = 1.0 - ADAM_B1 ** ADAM_STEP
    bc2 = 1.0 - ADAM_B2 ** ADAM_STEP

    def body(i, n, w_ref, g_ref, m_ref, v_ref, d_ref, mo_ref, vo_ref, *go_ref):
        gg = g_ref[...]
        if emit_g:
            go_ref[0][...] = gg
        mn = ADAM_B1 * m_ref[...] + (1.0 - ADAM_B1) * gg
        vn = ADAM_B2 * v_ref[...] + (1.0 - ADAM_B2) * (gg * gg)
        d_ref[...] = -ADAM_LR * ((mn / bc1) / (jnp.sqrt(vn / bc2) + ADAM_EPS) + ADAM_WD * w_ref[...])
        mo_ref[...] = mn
        vo_ref[...] = vn

    ts = 128 if rows % 128 == 0 else rows
    return _rowwise(name, body, [("row", w, c), ("row", g, c), ("row", m, c), ("row", v, c)],
                    [("row", None, c, c, F32)] * (4 if emit_g else 3), rows, ts)


_ANY = pl.BlockSpec(memory_space=pl.ANY)


def _place():
    x, y, c = lax.axis_index("x"), lax.axis_index("y"), lax.axis_index("c")
    chips = [(1 - x, y), (x, 1 - y), (1 - x, 1 - y)]
    return x, y, c, chips


def _rcopy(src, dst, ssem, rsem, dev):
    return pltpu.make_async_remote_copy(src_ref=src, dst_ref=dst, send_sem=ssem, recv_sem=rsem, device_id=dev,
                                        device_id_type=MESH)


def _place_shard(shards, layer, q_idx, dtype, name):
    _, r, cc = shards.shape
    tr = _tile(r, 256, 16)

    def body(q_ref, s_ref, o_ref):
        o_ref[0] = s_ref[0].astype(dtype)

    grid_spec = pltpu.PrefetchScalarGridSpec(
        num_scalar_prefetch=1, grid=(r // tr,),
        in_specs=[pl.BlockSpec((1, tr, cc), lambda i, q_ref: (layer, i, 0))],
        out_specs=pl.BlockSpec((1, tr, cc), lambda i, q_ref: (q_ref[0], i, 0)))
    return pl.pallas_call(body, name=name, grid_spec=grid_spec, out_shape=jax.ShapeDtypeStruct((N_CHIPS, r, cc), dtype),
                          compiler_params=_params("parallel"))(q_idx, shards)


_HBM = pl.BlockSpec(memory_space=pltpu.HBM)
_SEM = pl.BlockSpec(memory_space=pltpu.SEMAPHORE)
_EFFECT = pltpu.SideEffectType.DATAFLOW_SIDE_EFFECTING


def _in_hbm(arrs):
    return [pltpu.with_memory_space_constraint(a, pltpu.HBM) for a in arrs]


def _gather_start(bufs, name):
    n = len(bufs)
    half = [e.shape[1] // 2 for e in bufs]

    def body(*refs):
        ins, send, recv, token = refs[:n], refs[n], refs[n + 1], refs[2 * n + 2]
        x, y, c, chips = _place()
        q = 2 * x + y
        for e in range(n):
            blk = ins[e].at[q, pl.ds(c * half[e], half[e])]
            for j, (cx, cy) in enumerate(chips):
                _rcopy(blk, blk, send.at[3 * e + j], recv.at[3 * e + j], (cx, cy, c)).start()
        token[...] = jnp.zeros_like(token)

    out = pl.pallas_call(
        body, name=name,
        out_shape=(pltpu.SemaphoreType.DMA((3 * n,)), pltpu.SemaphoreType.DMA((3 * n,)),
                   *[pltpu.HBM(b.shape, b.dtype) for b in bufs], jax.ShapeDtypeStruct((SUBLANE, LANE), F32)),
        in_specs=[_HBM] * n, out_specs=(_SEM, _SEM, *[_HBM] * n, pl.BlockSpec(memory_space=pltpu.VMEM)),
        input_output_aliases={e: 2 + e for e in range(n)},
        compiler_params=pltpu.CompilerParams(has_side_effects=_EFFECT))(*_in_hbm(bufs))
    return out[0], out[1], list(out[2:2 + n]), out[2 + n]


def _gather_wait(send, recv, bufs, after, name):
    n = len(bufs)
    half = [e.shape[1] // 2 for e in bufs]

    def body(*refs):
        ins, send_ref, recv_ref = refs[:n], refs[n], refs[n + 1]
        x, y, c, chips = _place()
        q = 2 * x + y
        for e in range(n):
            rows = pl.ds(c * half[e], half[e])
            for j, (cx, cy) in enumerate(chips):
                cp = _rcopy(ins[e].at[q, rows], ins[e].at[2 * cx + cy, rows], send_ref.at[3 * e + j],
                            recv_ref.at[3 * e + j], (cx, cy, c))
                cp.wait_send()
                cp.wait_recv()

    return list(pl.pallas_call(
        body, name=name, out_shape=tuple(pltpu.HBM(b.shape, b.dtype) for b in bufs),
        in_specs=[_HBM] * n + [_SEM, _SEM, _ANY], out_specs=[_HBM] * n,
        input_output_aliases={e: e for e in range(n)},
        compiler_params=pltpu.CompilerParams(has_side_effects=_EFFECT))(*bufs, send, recv, after))


def _forward_sibling(bufs, name):
    n = len(bufs)
    half = [e.shape[1] // 2 for e in bufs]

    def body(*refs):
        outs = refs[n:2 * n]
        send, recv = refs[2 * n:]
        x, y, c, chips = _place()
        sib = (x, y, 1 - c)
        cps = []
        for e in range(n):
            for j, (cx, cy) in enumerate(chips):
                blk = outs[e].at[2 * cx + cy, pl.ds(c * half[e], half[e])]
                cps.append(_rcopy(blk, blk, send.at[3 * e + j], recv.at[3 * e + j], sib))
        for cp in cps:
            cp.start()
        for e in range(n):
            for j, (cx, cy) in enumerate(chips):
                blk = outs[e].at[2 * cx + cy, pl.ds((1 - c) * half[e], half[e])]
                _rcopy(blk, blk, send.at[3 * e + j], recv.at[3 * e + j], sib).wait_recv()
        for cp in cps:
            cp.wait_send()

    return list(pl.pallas_call(
        body, name=name, in_specs=[_ANY] * n, out_specs=[_ANY] * n,
        out_shape=[jax.ShapeDtypeStruct(e.shape, e.dtype) for e in bufs],
        input_output_aliases={e: e for e in range(n)},
        scratch_shapes=[pltpu.SemaphoreType.DMA((3 * n,))] * 2,
    )(*bufs))


def _scatter_start(entries, name):
    n = len(entries)
    lands = [lax.empty(e.shape, e.dtype) for e in entries]

    def body(*refs):
        ins, land, send, recv, token = refs[:n], refs[n:2 * n], refs[2 * n], refs[2 * n + 1], refs[4 * n + 2]
        x, y, c, chips = _place()
        q = 2 * x + y
        for e in range(n):
            for j, (cx, cy) in enumerate(chips):
                _rcopy(ins[e].at[2 * cx + cy], land[e].at[q], send.at[3 * e + j], recv.at[3 * e + j],
                       (cx, cy, c)).start()
        token[...] = jnp.zeros_like(token)

    out = pl.pallas_call(
        body, name=name,
        out_shape=(pltpu.SemaphoreType.DMA((3 * n,)), pltpu.SemaphoreType.DMA((3 * n,)),
                   *[pltpu.HBM(b.shape, b.dtype) for b in entries + lands],
                   jax.ShapeDtypeStruct((SUBLANE, LANE), F32)),
        in_specs=[_HBM] * (2 * n),
        out_specs=(_SEM, _SEM, *[_HBM] * (2 * n), pl.BlockSpec(memory_space=pltpu.VMEM)),
        input_output_aliases={e: 2 + e for e in range(2 * n)},
        compiler_params=pltpu.CompilerParams(has_side_effects=_EFFECT))(*_in_hbm(entries + lands))
    return out[0], out[1], list(out[2:2 + n]), list(out[2 + n:2 + 2 * n]), out[2 + 2 * n]


def _scatter_wait(send, recv, entries, lands, after, name):
    n = len(entries)

    def body(*refs):
        ins, land, send_ref, recv_ref = refs[:n], refs[n:2 * n], refs[2 * n], refs[2 * n + 1]
        x, y, c, chips = _place()
        for e in range(n):
            for j, (cx, cy) in enumerate(chips):
                k = 2 * cx + cy
                cp = _rcopy(ins[e].at[k], land[e].at[k], send_ref.at[3 * e + j], recv_ref.at[3 * e + j],
                            (cx, cy, c))
                cp.wait_send()
                cp.wait_recv()

    out = pl.pallas_call(
        body, name=name, out_shape=tuple(pltpu.HBM(b.shape, b.dtype) for b in entries + lands),
        in_specs=[_HBM] * (2 * n) + [_SEM, _SEM, _ANY], out_specs=[_HBM] * (2 * n),
        input_output_aliases={e: e for e in range(2 * n)},
        compiler_params=pltpu.CompilerParams(has_side_effects=_EFFECT))(*entries, *lands, send, recv, after)
    return list(out[:n]), list(out[n:])


def _swap_start(entries, name):
    n = len(entries)
    half = [e.shape[1] // 2 for e in entries]
    lands = [lax.empty((N_CHIPS, h, e.shape[2]), e.dtype) for e, h in zip(entries, half)]

    def body(*refs):
        ins, land, send, recv, token = refs[:n], refs[n:2 * n], refs[2 * n], refs[2 * n + 1], refs[4 * n + 2]
        x, y, c, _ = _place()
        for e in range(n):
            _rcopy(ins[e].at[:, pl.ds((1 - c) * half[e], half[e]), :], land[e], send.at[e], recv.at[e],
                   (x, y, 1 - c)).start()
        token[...] = jnp.zeros_like(token)

    out = pl.pallas_call(
        body, name=name,
        out_shape=(pltpu.SemaphoreType.DMA((n,)), pltpu.SemaphoreType.DMA((n,)),
                   *[pltpu.HBM(b.shape, b.dtype) for b in entries + lands],
                   jax.ShapeDtypeStruct((SUBLANE, LANE), F32)),
        in_specs=[_HBM] * (2 * n),
        out_specs=(_SEM, _SEM, *[_HBM] * (2 * n), pl.BlockSpec(memory_space=pltpu.VMEM)),
        input_output_aliases={e: 2 + e for e in range(2 * n)},
        compiler_params=pltpu.CompilerParams(has_side_effects=_EFFECT))(*_in_hbm(entries + lands))
    return out[0], out[1], list(out[2:2 + n]), list(out[2 + n:2 + 2 * n]), out[2 + 2 * n]


def _swap_wait(send, recv, entries, lands, after, name):
    n = len(entries)
    half = [e.shape[1] // 2 for e in entries]

    def body(*refs):
        ins, land, send_ref, recv_ref = refs[:n], refs[n:2 * n], refs[2 * n], refs[2 * n + 1]
        x, y, c, _ = _place()
        for e in range(n):
            cp = _rcopy(ins[e].at[:, pl.ds((1 - c) * half[e], half[e]), :], land[e], send_ref.at[e], recv_ref.at[e],
                        (x, y, 1 - c))
            cp.wait_send()
            cp.wait_recv()

    out = pl.pallas_call(
        body, name=name, out_shape=tuple(pltpu.HBM(b.shape, b.dtype) for b in entries + lands),
        in_specs=[_HBM] * (2 * n) + [_SEM, _SEM, _ANY], out_specs=[_HBM] * (2 * n),
        input_output_aliases={e: e for e in range(2 * n)},
        compiler_params=pltpu.CompilerParams(has_side_effects=_EFFECT))(*entries, *lands, send, recv, after)
    return list(out[:n]), list(out[n:])


def _swap_halves(entries, name):
    n = len(entries)
    half = [e.shape[1] // 2 for e in entries]

    def body(*refs):
        ins, outs = refs[:n], refs[n:2 * n]
        send, recv = refs[2 * n:]
        x, y, c, _ = _place()
        cps = [_rcopy(ins[e].at[:, pl.ds((1 - c) * half[e], half[e]), :], outs[e], send.at[e], recv.at[e],
                      (x, y, 1 - c)) for e in range(n)]
        for cp in cps:
            cp.start()
        for cp in cps:
            cp.wait()

    return pl.pallas_call(
        body, name=name, in_specs=[_ANY] * n, out_specs=[_ANY] * n,
        out_shape=[jax.ShapeDtypeStruct((N_CHIPS, h, e.shape[2]), e.dtype) for e, h in zip(entries, half)],
        scratch_shapes=[pltpu.SemaphoreType.DMA((n,))] * 2,
    )(*entries)


def _join_halves(bufs, name):
    n = len(bufs)
    pairs = [(o, layer) for o in range(n) for layer in range(bufs[o].shape[0])]
    npair = len(pairs)

    def body(*refs):
        outs = refs[n:2 * n]
        send, recv = refs[2 * n:]
        x, y, c, _ = _place()
        cps = []
        for k, (o, layer) in enumerate(pairs):
            r2 = bufs[o].shape[1] // 2
            blk = outs[o].at[layer, pl.ds(c * r2, r2)]
            cps.append(_rcopy(blk, blk, send.at[k], recv.at[k], (x, y, 1 - c)))
        for cp in cps:
            cp.start()
        for k, (o, layer) in enumerate(pairs):
            r2 = bufs[o].shape[1] // 2
            blk = outs[o].at[layer, pl.ds((1 - c) * r2, r2)]
            _rcopy(blk, blk, send.at[k], recv.at[k], (x, y, 1 - c)).wait_recv()
        for cp in cps:
            cp.wait_send()

    return pl.pallas_call(
        body, name=name, in_specs=[_ANY] * n, out_specs=[_ANY] * n,
        out_shape=[jax.ShapeDtypeStruct(b.shape, b.dtype) for b in bufs],
        input_output_aliases={e: e for e in range(n)},
        scratch_shapes=[pltpu.SemaphoreType.DMA((npair,))] * 2,
    )(*bufs)


def _gather_all(v, name):
    def body(v_ref, o_ref, send, recv, loc):
        x, y, c, _ = _place()
        me = 4 * x + 2 * y + c
        mine = pltpu.make_async_copy(v_ref, o_ref.at[me], loc)
        mine.start()
        peers = []
        for k in range(1, N_DEV):
            px = 1 - x if k & 4 else x
            py = 1 - y if k & 2 else y
            pc = 1 - c if k & 1 else c
            peers.append((px, py, pc))
        cps = [_rcopy(v_ref, o_ref.at[me], send.at[k], recv.at[k], peers[k]) for k in range(N_DEV - 1)]
        for cp in cps:
            cp.start()
        for k, (px, py, pc) in enumerate(peers):
            blk = o_ref.at[4 * px + 2 * py + pc]
            _rcopy(blk, blk, send.at[k], recv.at[k], (px, py, pc)).wait_recv()
        for cp in cps:
            cp.wait_send()
        mine.wait()

    return pl.pallas_call(
        body, name=name, in_specs=[_ANY], out_specs=_ANY, out_shape=jax.ShapeDtypeStruct((N_DEV,) + v.shape, v.dtype),
        scratch_shapes=[pltpu.SemaphoreType.DMA((N_DEV - 1,))] * 2 + [pltpu.SemaphoreType.DMA],
    )(v)


def _add_own_half(gst, rx, c_idx, name):
    _, r, cc = gst.shape
    r2 = r // 2
    tr = _tile(r2, 256, 16)
    g4 = gst.reshape(N_CHIPS, 2, r2, cc)

    def body(c_ref, g_ref, r_ref, o_ref):
        o_ref[...] = (g_ref[0].astype(F32) + r_ref[...].astype(F32)).astype(BF16)

    grid_spec = pltpu.PrefetchScalarGridSpec(
        num_scalar_prefetch=1, grid=(N_CHIPS, r2 // tr),
        in_specs=[pl.BlockSpec((1, 1, tr, cc), lambda k, i, c_ref: (k, c_ref[0], i, 0)),
                  pl.BlockSpec((1, tr, cc), lambda k, i, c_ref: (k, i, 0))],
        out_specs=pl.BlockSpec((1, tr, cc), lambda k, i, c_ref: (k, i, 0)))
    return pl.pallas_call(body, name=name, grid_spec=grid_spec, out_shape=jax.ShapeDtypeStruct((N_CHIPS, r2, cc), BF16),
                          compiler_params=_params("parallel", "parallel"))(c_idx, g4, rx)


def _sum_into(buf, rx, own, layer, qc, out_shape, name):
    _, r2, cc = rx.shape
    tr = _tile(r2, 256, 16)
    nb = r2 // tr

    def body(qc_ref, *refs):
        rx_ref, own_ref, o_ref = refs[-3:]
        q = qc_ref[0]
        acc = None
        for k in range(N_CHIPS):
            v = jnp.where(q == k, own_ref[0], rx_ref[k]).astype(F32)
            acc = v if acc is None else acc + v
        o_ref[0] = acc

    in_specs = [pl.BlockSpec((N_CHIPS, tr, cc), lambda i, qc_ref: (0, i, 0)),
                pl.BlockSpec((1, tr, cc), lambda i, qc_ref: (qc_ref[0], i, 0))]
    args = (rx, own)
    aliases = {}
    if buf is not None:
        in_specs = [_ANY] + in_specs
        args = (buf,) + args
        aliases = {1: 0}
    grid_spec = pltpu.PrefetchScalarGridSpec(
        num_scalar_prefetch=1, grid=(nb,), in_specs=in_specs,
        out_specs=pl.BlockSpec((1, tr, cc), lambda i, qc_ref: (layer, qc_ref[1] * nb + i, 0)))
    return pl.pallas_call(body, name=name, grid_spec=grid_spec, out_shape=jax.ShapeDtypeStruct(out_shape, F32),
                          input_output_aliases=aliases, compiler_params=_params("parallel"))(qc, *args)


def _sum_slots(st, name):
    k, r, cc = st.shape
    tr = _tile(r, 256, 8)

    def body(s_ref, o_ref):
        acc = s_ref[0].astype(F32)
        for j in range(1, k):
            acc = acc + s_ref[j].astype(F32)
        o_ref[...] = acc

    return pl.pallas_call(body, name=name, grid=(r // tr,), in_specs=[pl.BlockSpec((k, tr, cc), lambda i: (0, i, 0))],
                          out_specs=pl.BlockSpec((tr, cc), lambda i: (i, 0)),
                          out_shape=jax.ShapeDtypeStruct((r, cc), F32), compiler_params=_params("parallel"))(st)


def _pack(arrs, rows_mult=2 * SUBLANE):
    flat = jnp.concatenate([a.reshape(-1).astype(F32) for a in arrs])
    quantum = rows_mult * LANE
    padded = -(-flat.shape[0] // quantum) * quantum
    return jnp.pad(flat, (0, padded - flat.shape[0])).reshape(-1, LANE)


def _unpack(buf, shapes):
    flat = buf.reshape(-1)
    out, off = [], 0
    for sh in shapes:
        size = 1
        for d in sh:
            size *= d
        out.append(flat[off:off + size].reshape(sh))
        off += size
    return out


def kernel(x, p, norm_mix_g, norm_ffn_g, norm_ple_g, final_norm_g, a_w_in, a_conv_w, a_conv_b, a_w_gate_r, a_b_gate_r, a_w_gate_i, a_b_gate_i, a_lambda, a_w_out, b_w_in, b_conv_w, b_conv_b, b_dt_bias, b_a_log, b_d_skip, b_norm_g, b_w_out, ffn_w_gate, ffn_w_up, ffn_w_down, ple_w_proj, ple_w_gate, loss_target, m_norm_mix_g, m_norm_ffn_g, m_norm_ple_g, m_final_norm_g, m_a_w_in, m_a_conv_w, m_a_conv_b, m_a_w_gate_r, m_a_b_gate_r, m_a_w_gate_i, m_a_b_gate_i, m_a_lambda, m_a_w_out, m_b_w_in, m_b_conv_w, m_b_conv_b, m_b_dt_bias, m_b_a_log, m_b_d_skip, m_b_norm_g, m_b_w_out, m_ffn_w_gate, m_ffn_w_up, m_ffn_w_down, m_ple_w_proj, m_ple_w_gate, v_norm_mix_g, v_norm_ffn_g, v_norm_ple_g, v_final_norm_g, v_a_w_in, v_a_conv_w, v_a_conv_b, v_a_w_gate_r, v_a_b_gate_r, v_a_w_gate_i, v_a_b_gate_i, v_a_lambda, v_a_w_out, v_b_w_in, v_b_conv_w, v_b_conv_b, v_b_dt_bias, v_b_a_log, v_b_d_skip, v_b_norm_g, v_b_w_out, v_ffn_w_gate, v_ffn_w_up, v_ffn_w_down, v_ple_w_proj, v_ple_w_gate):
    names = ["norm_mix_g", "norm_ffn_g", "norm_ple_g", "final_norm_g", "a_w_in", "a_conv_w", "a_conv_b", "a_w_gate_r",
             "a_b_gate_r", "a_w_gate_i", "a_b_gate_i", "a_lambda", "a_w_out", "b_w_in", "b_conv_w", "b_conv_b",
             "b_dt_bias", "b_a_log", "b_d_skip", "b_norm_g", "b_w_out", "ffn_w_gate", "ffn_w_up", "ffn_w_down",
             "ple_w_proj", "ple_w_gate"]
    w_in = dict(zip(names, [norm_mix_g, norm_ffn_g, norm_ple_g, final_norm_g, a_w_in, a_conv_w, a_conv_b, a_w_gate_r,
                            a_b_gate_r, a_w_gate_i, a_b_gate_i, a_lambda, a_w_out, b_w_in, b_conv_w, b_conv_b,
                            b_dt_bias, b_a_log, b_d_skip, b_norm_g, b_w_out, ffn_w_gate, ffn_w_up, ffn_w_down,
                            ple_w_proj, ple_w_gate]))
    m_in = dict(zip(names, [m_norm_mix_g, m_norm_ffn_g, m_norm_ple_g, m_final_norm_g, m_a_w_in, m_a_conv_w,
                            m_a_conv_b, m_a_w_gate_r, m_a_b_gate_r, m_a_w_gate_i, m_a_b_gate_i, m_a_lambda,
                            m_a_w_out, m_b_w_in, m_b_conv_w, m_b_conv_b, m_b_dt_bias, m_b_a_log, m_b_d_skip,
                            m_b_norm_g, m_b_w_out, m_ffn_w_gate, m_ffn_w_up, m_ffn_w_down, m_ple_w_proj,
                            m_ple_w_gate]))
    v_in = dict(zip(names, [v_norm_mix_g, v_norm_ffn_g, v_norm_ple_g, v_final_norm_g, v_a_w_in, v_a_conv_w,
                            v_a_conv_b, v_a_w_gate_r, v_a_b_gate_r, v_a_w_gate_i, v_a_b_gate_i, v_a_lambda,
                            v_a_w_out, v_b_w_in, v_b_conv_w, v_b_conv_b, v_b_dt_bias, v_b_a_log, v_b_d_skip,
                            v_b_norm_g, v_b_w_out, v_ffn_w_gate, v_ffn_w_up, v_ffn_w_down, v_ple_w_proj,
                            v_ple_w_gate]))

    s, d = x.shape[1], x.shape[2]
    depth = norm_mix_g.shape[0]
    assert depth == 2
    q_idx = 2 * lax.axis_index("x") + lax.axis_index("y")
    c_idx = lax.axis_index("c").astype(jnp.int32).reshape(1)

    inner = b_w_out.shape[1] * N_CHIPS
    n_heads = inner // SSD_HEAD_DIM
    n_groups = n_heads // SSD_HEADS_PER_GROUP
    gn = n_groups * SSD_STATE
    xbcw = inner + 2 * gn
    assert b_conv_w.shape[2] * N_CHIPS == xbcw and n_heads <= LANE

    big = [("a_w_in", "col"), ("a_w_gate_r", "gate"), ("a_w_gate_i", "gate"), ("a_w_out", "row"),
           ("ffn_w_gate", "col"), ("ffn_w_up", "col"), ("ffn_w_down", "row"), ("ple_w_proj", "col"),
           ("ple_w_gate", "row"), ("b_w_in", "col"), ("b_w_out", "row")]
    kind_of = dict(big)

    def shard2d(name, arr):
        if kind_of[name] == "gate":
            return [arr[l].reshape(-1, arr.shape[-1]) for l in range(arr.shape[0])]
        return [arr[l] for l in range(arr.shape[0])]

    small_sharded = ["a_conv_w", "a_b_gate_r", "a_b_gate_i", "b_conv_w", "b_conv_b", "b_norm_g"]
    small_pack = _pack([w_in[nm] for nm in small_sharded], rows_mult=16)

    q_vec = q_idx.astype(jnp.int32).reshape(1)
    qc_vec = jnp.stack([q_idx, lax.axis_index("c")]).astype(jnp.int32)
    gather_groups = [
        [("a_w_in", 0), ("a_w_gate_r", 0), ("a_w_gate_i", 0), ("a_w_out", 0), ("small", 0)],
        [("ffn_w_gate", 0), ("ffn_w_up", 0)],
        [("ffn_w_down", 0), ("ple_w_proj", 0), ("ple_w_gate", 0)],
        [("b_w_in", 0), ("b_w_out", 0)],
        [("ffn_w_gate", 1), ("ffn_w_up", 1), ("ffn_w_down", 1), ("ple_w_proj", 1), ("ple_w_gate", 1)],
    ]
    gather_started = [None] * len(gather_groups)

    def gather_begin(gi):
        bufs = []
        for nm, l in gather_groups[gi]:
            if nm == "small":
                bufs.append(_place_shard(small_pack[None], 0, q_vec, F32, "place_small"))
            else:
                arr = w_in[nm]
                arr = arr.reshape(arr.shape[0], -1, arr.shape[-1]) if kind_of[nm] == "gate" else arr
                bufs.append(_place_shard(arr, l, q_vec, BF16, f"place_{nm}{l}"))
        gather_started[gi] = _gather_start(bufs, f"gather_start{gi}")
        return gather_started[gi][3][0:1, 0:1]

    wst = {}

    def gather_finish(gi, after):
        send, recv, thru, _ = gather_started[gi]
        landed = _gather_wait(send, recv, thru, after, f"gather_wait{gi}")
        for k, arr in zip(gather_groups[gi], _forward_sibling(landed, f"gather_forward{gi}")):
            wst[k] = arr

    gather_finish(0, sum(gather_begin(gi) for gi in range(len(gather_groups))))
    small_st = wst[("small", 0)]
    kept_stacked = ("a_w_in", "ffn_w_gate", "ffn_w_up", "ple_w_proj")

    def whole(nm, l):
        st = wst[(nm, l)]
        kind = kind_of[nm]
        if nm in kept_stacked:
            return st
        if kind == "row":
            return st.reshape(-1, st.shape[-1])
        if kind == "col":
            return jnp.concatenate([st[k] for k in range(N_CHIPS)], axis=1)
        heads = w_in[nm].shape[1]
        return st.reshape(N_CHIPS, heads, -1, st.shape[-1]).transpose(1, 0, 2, 3).reshape(heads, -1, st.shape[-1])

    small_parts = [_unpack(small_st[k], [w_in[nm].shape for nm in small_sharded]) for k in range(N_CHIPS)]
    small_full = {nm: jnp.concatenate([small_parts[k][i] for k in range(N_CHIPS)], axis=-1)
                  for i, nm in enumerate(small_sharded)}
    a_cw = small_full["a_conv_w"][0]
    a_br = small_full["a_b_gate_r"][0].reshape(1, -1)
    a_bi = small_full["a_b_gate_i"][0].reshape(1, -1)
    b_cw = small_full["b_conv_w"][0]
    b_cb = small_full["b_conv_b"]
    b_ng = small_full["b_norm_g"]

    def pad_lanes(v):
        return jnp.pad(v, ((0, 0), (0, LANE - v.shape[1])))

    dt_bias = pad_lanes(b_dt_bias)
    a_log = pad_lanes(b_a_log.reshape(n_groups, SSD_HEADS_PER_GROUP)).reshape(1, n_groups * LANE)
    dskip_e = jnp.repeat(b_d_skip, SSD_HEAD_DIM, axis=1)

    w_a_in = whole("a_w_in", 0)
    w_ax, w_ay = w_a_in[:N_CHIPS // 2], w_a_in[N_CHIPS // 2:]
    w_ar, w_ai, w_ao = whole("a_w_gate_r", 0), whole("a_w_gate_i", 0), whole("a_w_out", 0)
    w_fg, w_fu, w_fd, w_pp, w_pg = ([None] * depth for _ in range(5))

    def take_ffn_in(l):
        w_fg[l], w_fu[l] = whole("ffn_w_gate", l), whole("ffn_w_up", l)

    def take_ffn_out_ple(l):
        w_fd[l], w_pp[l], w_pg[l] = whole("ffn_w_down", l), whole("ple_w_proj", l), whole("ple_w_gate", l)

    grads = {}

    h0 = x[0]
    g_mix = [norm_mix_g[l:l + 1] for l in range(depth)]
    g_ffn = [norm_ffn_g[l:l + 1] for l in range(depth)]
    g_ple = [norm_ple_g[l:l + 1] for l in range(depth)]
    g_fin = final_norm_g.reshape(1, -1)

    u0 = _rmsnorm_fwd(h0, g_mix[0], "norm_mix0")
    xr_pre = _mm(u0, w_ax, stacked_b=True, name="lru_in_x")
    yg = _mm(u0, w_ay, stacked_b=True, name="lru_in_y")
    xr = _conv_fwd(xr_pre, a_cw, a_conv_b, False, "lru_conv")
    lru_a, lru_b = _lru_gates_fwd(xr, w_ar, w_ai, a_br, a_bi, a_lambda)
    hs = _scan(lru_a, lru_b, False, "lru_scan")
    y_lru = _lru_out_fwd(hs, yg)
    h_mix = [_mm(y_lru, w_ao, add=h0, name="lru_out"), None]

    def ffn_ple_fwd(h_in, l, before_down=None):
        n_f = _rmsnorm_fwd(h_in, g_ffn[l], f"norm_ffn{l}")
        gt = _mm(n_f, w_fg[l], stacked_b=True, out_dtype=BF16, name=f"ffn_gate{l}")
        up = _mm(n_f, w_fu[l], stacked_b=True, out_dtype=BF16, name=f"ffn_up{l}")
        act = _swiglu_act(gt, up)
        if before_down is not None:
            before_down(act)
        h_f = _mm(act, w_fd[l], add=h_in, name=f"ffn_down{l}")
        n_p = _rmsnorm_fwd(h_f, g_ple[l], f"norm_ple{l}")
        gp = _mm(n_p, w_pg[l], name=f"ple_gate{l}")
        pp = _mm(p[l, 0], w_pp[l], stacked_b=True, name=f"ple_proj{l}")
        h_out = _ple_fwd(h_f, gp, pp)
        return h_out, dict(h_in=h_in, n_f=n_f, gt=gt, up=up, act=act, h_f=h_f, n_p=n_p, gp=gp, pp=pp)

    gather_finish(1, h_mix[0])
    take_ffn_in(0)

    def finish_ffn_out_ple0(act):
        gather_finish(2, act)
        take_ffn_out_ple(0)

    h_l0, sv0 = ffn_ple_fwd(h_mix[0], 0, finish_ffn_out_ple0)

    gather_finish(3, h_l0)
    w_b_in = whole("b_w_in", 0)
    w_bz, w_bx = w_b_in[:, :inner], w_b_in[:, inner:inner + xbcw]
    w_bd = pad_lanes(w_b_in[:, inner + xbcw:])
    w_bo = whole("b_w_out", 0)
    u1 = _rmsnorm_fwd(h_l0, g_mix[1], "norm_mix1")
    z = _mm(u1, w_bz, name="ssd_in_z")
    xbc_pre = _mm(u1, w_bx, name="ssd_in_xbc")
    dt_pre = _mm(u1, w_bd, name="ssd_in_dt")
    xbc = _conv_fwd(xbc_pre, b_cw, b_cb, True, "ssd_conv")
    dt = _dt_fwd(dt_pre, dt_bias, n_heads, n_groups)
    ysc, s_in = _ssd_fwd(xbc, dt, a_log, inner, n_groups)
    yn = _ssd_gate_norm_fwd(ysc, xbc, z, dskip_e, b_ng, n_groups)
    h_mix[1] = _mm(yn, w_bo, add=h_l0, name="ssd_out")
    gather_finish(4, h_mix[1])
    take_ffn_in(1)
    take_ffn_out_ple(1)
    h_l1, sv1 = ffn_ple_fwd(h_mix[1], 1)

    dh, dg_fin, loss_row = _final_loss_bwd(h_l1, g_fin, loss_target[0])

    d_norm_ffn, d_norm_ple, d_norm_mix = [None] * depth, [None] * depth, [None] * depth
    for nm in ("ffn_w_gate", "ffn_w_up", "ffn_w_down", "ple_w_proj", "ple_w_gate"):
        grads[nm] = [None] * depth

    def stacked(nm, gfull):
        kind = kind_of[nm]
        if nm in kept_stacked:
            return gfull
        if kind == "row":
            return gfull.reshape(N_CHIPS, -1, gfull.shape[-1])
        if kind == "col":
            n_loc = gfull.shape[1] // N_CHIPS
            return jnp.stack([gfull[:, k * n_loc:(k + 1) * n_loc] for k in range(N_CHIPS)])
        heads, bw, _ = gfull.shape
        return gfull.reshape(heads, N_CHIPS, bw // N_CHIPS, bw).transpose(1, 0, 2, 3).reshape(N_CHIPS, -1, bw)

    reduce_started = []

    def reduce_start(keys, tag):
        gst = [stacked(nm, grads[nm][l]) for nm, l in keys]
        from_sib = _swap_halves(gst, f"reduce_swap_{tag}")
        return reduce_exchange(keys, tag, gst, from_sib)

    def reduce_exchange(keys, tag, gst, from_sib):
        chip_sum = [_add_own_half(g, r, c_idx, f"reduce_add_{nm}{l}") for g, r, (nm, l) in zip(gst, from_sib, keys)]
        send, recv, ents, lands, token = _scatter_start(chip_sum, f"reduce_scatter_start_{tag}")
        reduce_started.append((keys, tag, send, recv, ents, lands))
        return token[0:1, 0:1]

    def reduce_swap_begin(keys, tag):
        gst = [stacked(nm, grads[nm][l]) for nm, l in keys]
        send, recv, ents, lands, token = _swap_start(gst, f"reduce_swap_start_{tag}")
        return (keys, tag, send, recv, ents, lands), token[0:1, 0:1]

    def reduce_swap_end(state, after):
        keys, tag, send, recv, ents, lands = state
        gst, from_sib = _swap_wait(send, recv, ents, lands, after, f"reduce_swap_wait_{tag}")
        return reduce_exchange(keys, tag, gst, from_sib)

    def ffn_ple_keys(l):
        return [("ple_w_gate", l), ("ple_w_proj", l), ("ffn_w_down", l), ("ffn_w_gate", l), ("ffn_w_up", l)]

    def ffn_ple_bwd(dh_out, sv, l, g_ple_l, after_dact=None):
        dgp, dpp = _ple_bwd(dh_out, sv["gp"], sv["pp"])
        grads["ple_w_gate"][l] = _mm(sv["n_p"], dgp, ta=True, out_dtype=BF16, name=f"ple_gate_dw{l}")
        grads["ple_w_proj"][l] = _mm(p[l, 0], dpp, ta=True, out_dtype=BF16, stacked_out=N_CHIPS,
                                     name=f"ple_proj_dw{l}")
        dn = _mm(dgp, w_pg[l], tb=True, name=f"ple_gate_dx{l}")
        dh_f, d_norm_ple[l] = _rmsnorm_bwd(dn, sv["h_f"], g_ple_l, dh_out, f"norm_ple_bwd{l}")
        grads["ffn_w_down"][l] = _mm(sv["act"], dh_f, ta=True, out_dtype=BF16, name=f"ffn_down_dw{l}")
        dact = _mm(dh_f, w_fd[l], tb=True, name=f"ffn_down_dx{l}")
        g_ffn_l = g_ffn[l] if after_dact is None else g_ffn[l] + after_dact(dact)
        dgt, dup = _swiglu_bwd(dact, sv["gt"], sv["up"])
        grads["ffn_w_gate"][l] = _mm(sv["n_f"], dgt, ta=True, out_dtype=BF16, stacked_out=N_CHIPS,
                                     name=f"ffn_gate_dw{l}")
        grads["ffn_w_up"][l] = _mm(sv["n_f"], dup, ta=True, out_dtype=BF16, stacked_out=N_CHIPS, name=f"ffn_up_dw{l}")
        dn = _mm(dgt, w_fg[l], tb=True, stacked_b=True, name=f"ffn_gate_dx{l}")
        dn = _mm(dup, w_fu[l], tb=True, stacked_b=True, add=dn, name=f"ffn_up_dx{l}")
        dh_in, d_norm_ffn[l] = _rmsnorm_bwd(dn, sv["h_in"], g_ffn_l, dh_f, f"norm_ffn_bwd{l}")
        return dh_in

    dh = ffn_ple_bwd(dh, sv1, 1, g_ple[1])
    swapping, tok = reduce_swap_begin(ffn_ple_keys(1), "l1")

    grads["b_w_out"] = [_mm(yn, dh, ta=True, out_dtype=BF16, name="ssd_out_dw")]
    dyn = _mm(dh, w_bo, tb=True, name="ssd_out_dx")
    dy_ssd, dz, d_b_norm_g, dd_lane = _ssd_gate_norm_bwd(dyn, ysc, xbc, z, dskip_e, b_ng + tok, n_groups)
    tok = reduce_swap_end(swapping, dy_ssd)
    dxs, d_bm, d_cm, ddt, d_a_log = _ssd_bwd(xbc, dt, a_log, dy_ssd, ysc, s_in, dskip_e + tok, inner, n_groups)
    dxbc = _put_cols(dxs, d_bm, inner // gn, "ssd_put_db")
    dxbc = _put_cols(dxbc, d_cm, inner // gn + 1, "ssd_put_dc")
    dconv = _silu_conv_bwd_pre(dxbc, xbc_pre, b_cw, b_cb, "ssd_conv_bwd_pre")
    dxbc_pre, d_b_conv_w, d_b_conv_b = _conv_bwd(dconv, xbc_pre, b_cw, "ssd_conv_bwd")
    ddt_pre, d_dt_bias = _dt_bwd(ddt, dt_pre, dt_bias, n_heads, n_groups)
    d_a_log = d_a_log.reshape(n_groups, LANE)[:, :SSD_HEADS_PER_GROUP].reshape(1, n_heads)
    gw_bz = _mm(u1, dz, ta=True, out_dtype=BF16, name="ssd_in_z_dw")
    gw_bx = _mm(u1, dxbc_pre, ta=True, out_dtype=BF16, name="ssd_in_xbc_dw")
    gw_bd = _mm(u1, ddt_pre, ta=True, out_dtype=BF16, name="ssd_in_dt_dw")
    grads["b_w_in"] = [jnp.concatenate([gw_bz, gw_bx, gw_bd[:, :n_heads]], axis=1)]
    du = _mm(dz, w_bz, tb=True, name="ssd_in_z_dx")
    du = _mm(dxbc_pre, w_bx, tb=True, add=du, name="ssd_in_xbc_dx")
    du = _mm(ddt_pre, w_bd, tb=True, add=du, name="ssd_in_dt_dx")
    dh, d_norm_mix[1] = _rmsnorm_bwd(du, h_l0, g_mix[1], dh, "norm_mix_bwd1")
    swapping, tok = reduce_swap_begin([("b_w_out", 0), ("b_w_in", 0)], "ssd")

    dh = ffn_ple_bwd(dh, sv0, 0, g_ple[0] + tok, lambda dact: reduce_swap_end(swapping, dact))
    tok = reduce_start(ffn_ple_keys(0), "l0")

    grads["a_w_out"] = [_mm(y_lru, dh, ta=True, out_dtype=BF16, name="lru_out_dw")]
    dy_lru = _mm(dh, w_ao, tb=True, name="lru_out_dx")
    dhs, dyg = _lru_out_bwd(dy_lru, hs, yg)
    g_scan = _scan(lru_a, dhs, True, "lru_scan_bwd")
    dxr, d_wr, d_wi, d_br, d_bi, d_lam = _lru_gates_bwd(xr, g_scan, hs, w_ar, w_ai, a_br, a_bi, a_lambda + tok)
    dxr_pre, d_a_conv_w, d_a_conv_b = _conv_bwd(dxr, xr_pre, a_cw, "lru_conv_bwd")
    gw_ax = _mm(u0, dxr_pre, ta=True, out_dtype=BF16, stacked_out=N_CHIPS // 2, name="lru_in_x_dw")
    gw_ay = _mm(u0, dyg, ta=True, out_dtype=BF16, stacked_out=N_CHIPS // 2, name="lru_in_y_dw")
    grads["a_w_in"] = [jnp.concatenate([gw_ax, gw_ay], axis=0)]
    grads["a_w_gate_r"] = [d_wr.astype(BF16)]
    grads["a_w_gate_i"] = [d_wi.astype(BF16)]
    du = _mm(dxr_pre, w_ax, tb=True, stacked_b=True, name="lru_in_x_dx")
    du = _mm(dyg, w_ay, tb=True, stacked_b=True, add=du, name="lru_in_y_dx")
    grad_x, d_norm_mix[0] = _rmsnorm_bwd(du, h0, g_mix[0], dh, "norm_mix_bwd0")

    tok = reduce_start([("a_w_out", 0), ("a_w_in", 0), ("a_w_gate_r", 0), ("a_w_gate_i", 0)], "lru")
    grad_out, delta_out, m_out, v_out = {}, {}, {}, {}

    def reduce_finish(groups, after, tag):
        g_half = {}
        for keys, gtag, send, recv, ents, lands in groups:
            ents, lands = _scatter_wait(send, recv, ents, lands, after, f"reduce_scatter_wait_{gtag}")
            for rx, own, (nm, l) in zip(lands, ents, keys):
                sh = shard2d(nm, w_in[nm])
                g_half[nm] = _sum_into(g_half.get(nm), rx, own, l, qc_vec, (len(sh),) + sh[0].shape,
                                       f"reduce_sum_{nm}{l}")
        nms = list(g_half)
        seen = jnp.zeros((1, 1), F32)
        for nm, gfull in zip(nms, _join_halves([g_half[nm] for nm in nms], f"reduce_join_{tag}")):
            shape, cols = w_in[nm].shape, gfull.shape[-1]
            dl, mn, vn, gout = _adamw(w_in[nm].reshape(-1, cols), gfull.reshape(-1, cols),
                                      m_in[nm].reshape(-1, cols), v_in[nm].reshape(-1, cols), f"adamw_{nm}",
                                      emit_g=True)
            grad_out[nm], delta_out[nm] = gout.reshape(shape), dl.reshape(shape)
            m_out[nm], v_out[nm] = mn.reshape(shape), vn.reshape(shape)
            seen = seen + dl[0:1, 0:1]
        return seen

    by_tag = {grp[1]: grp for grp in reduce_started}
    done = reduce_finish([by_tag["ssd"]], grad_x[0:1, 0:1] + tok, "ssd")
    done = reduce_finish([by_tag["l1"], by_tag["l0"]], done, "ffn_ple")
    reduce_finish([by_tag["lru"]], done, "lru")

    small_full_grads = {
        "norm_mix_g": jnp.concatenate(d_norm_mix, axis=0), "norm_ffn_g": jnp.concatenate(d_norm_ffn, axis=0),
        "norm_ple_g": jnp.concatenate(d_norm_ple, axis=0), "final_norm_g": dg_fin[0],
        "a_conv_w": d_a_conv_w[None], "a_conv_b": d_a_conv_b,
        "a_b_gate_r": d_br.reshape(a_b_gate_r.shape[0], a_b_gate_r.shape[1], -1),
        "a_b_gate_i": d_bi.reshape(a_b_gate_i.shape[0], a_b_gate_i.shape[1], -1),
        "a_lambda": d_lam, "b_conv_w": d_b_conv_w[None], "b_conv_b": d_b_conv_b,
        "b_dt_bias": d_dt_bias[:, :n_heads], "b_a_log": d_a_log,
        "b_d_skip": dd_lane.reshape(1, n_heads, SSD_HEAD_DIM).sum(axis=-1), "b_norm_g": d_b_norm_g,
    }
    small_names = list(small_full_grads)
    small_shapes = [small_full_grads[nm].shape for nm in small_names]
    packed = _pack([loss_row] + [small_full_grads[nm] for nm in small_names])
    total = _sum_slots(_gather_all(packed, "gather_small_grads"), "sum_small_grads")
    parts = _unpack(total, [(1, LANE)] + small_shapes)
    loss = parts[0][0, 0]
    g_small = {}
    for nm, gfull in zip(small_names, parts[1:]):
        if nm in small_sharded:
            n_loc = w_in[nm].shape[-1]
            gfull = lax.dynamic_slice_in_dim(gfull, q_idx * n_loc, n_loc, axis=gfull.ndim - 1)
        g_small[nm] = gfull

    sm_shapes = [w_in[nm].shape for nm in small_names]
    dl, mn, vn = _adamw(_pack([w_in[nm] for nm in small_names]), _pack([g_small[nm] for nm in small_names]),
                        _pack([m_in[nm] for nm in small_names]), _pack([v_in[nm] for nm in small_names]),
                        "adamw_small")
    for nm, a, b_, c_ in zip(small_names, _unpack(dl, sm_shapes), _unpack(mn, sm_shapes), _unpack(vn, sm_shapes)):
        grad_out[nm] = g_small[nm].reshape(w_in[nm].shape)
        delta_out[nm], m_out[nm], v_out[nm] = a, b_, c_

    return (loss, grad_x[None], *[grad_out[nm] for nm in names], *[delta_out[nm] for nm in names],
            *[m_out[nm] for nm in names], *[v_out[nm] for nm in names])
```

```python
import functools

import jax
import jax.numpy as jnp
from jax import lax
from jax.experimental import pallas as pl
from jax.experimental.pallas import tpu as pltpu

F32 = jnp.float32
BF16 = jnp.bfloat16
MESH = pl.DeviceIdType.MESH
HIGHEST = lax.Precision.HIGHEST

NORM_EPS = 1e-6
LRU_C = 8.0
CONV_WIDTH = 4
SSD_HEAD_DIM = 64
SSD_STATE = 128
SSD_CHUNK = 128
SSD_HEADS_PER_GROUP = 8
LANE = 128
SUBLANE = 8
N_CHIPS = 4
N_DEV = 8
VMEM_LIMIT = 48 * 1024 * 1024

ADAM_LR = 0.001
ADAM_B1 = 0.9
ADAM_B2 = 0.999
ADAM_EPS = 1e-08
ADAM_WD = 0.01
ADAM_STEP = 10


def _tile(n, cap, mult=LANE):
    best = None
    for t in range(mult, min(n, cap) + 1, mult):
        if n % t == 0:
            best = t
    return best if best is not None else n


def _params(*sem):
    return pltpu.CompilerParams(dimension_semantics=sem, vmem_limit_bytes=VMEM_LIMIT)


def _mm(a, b, *, ta=False, tb=False, add=None, out_dtype=F32, stacked_b=False, stacked_out=0, name):
    if ta:
        kd, m = a.shape
    else:
        m, kd = a.shape
    n_loc = None
    if stacked_b:
        slots, kb, n_loc = b.shape
        if tb:
            n, kb = kb, slots * n_loc
        else:
            n = slots * n_loc
    elif tb:
        n, kb = b.shape
    else:
        kb, n = b.shape
    assert kd == kb, (a.shape, b.shape, ta, tb)
    tm = _tile(m, 1408)
    tn = _tile(n, 1408)
    tk = _tile(kd, 2048)
    if stacked_b and tb:
        tk = _tile(n_loc, 1408)
    elif stacked_b:
        tn = _tile(n_loc, 1408)
    if stacked_out:
        n_loc = n // stacked_out
        tn = _tile(n_loc, 1408)
    nk = kd // tk
    dims = (((0 if ta else 1,), (1 if tb else 0,)), ((), ()))

    def body(*refs):
        a_ref, b_ref = refs[:2]
        add_ref = refs[2] if add is not None else None
        o_ref = refs[3] if add is not None else refs[2]
        bv = b_ref[0] if stacked_b else b_ref[...]
        part = lax.dot_general(a_ref[...].astype(BF16), bv.astype(BF16), dims, preferred_element_type=F32)

        def finish(r):
            if add is not None:
                r = r + add_ref[...]
            if stacked_out:
                o_ref[0] = r.astype(out_dtype)
            else:
                o_ref[...] = r.astype(out_dtype)

        if nk == 1:
            finish(part)
        else:
            acc_ref = refs[-1]
            k = pl.program_id(2)

            @pl.when(k == 0)
            def _():
                acc_ref[...] = part

            @pl.when((k > 0) & (k < nk - 1))
            def _():
                acc_ref[...] += part

            @pl.when(k == nk - 1)
            def _():
                finish(acc_ref[...] + part)

    a_spec = pl.BlockSpec((tk, tm), lambda i, j, k: (k, i)) if ta else pl.BlockSpec((tm, tk), lambda i, j, k: (i, k))
    if stacked_b and tb:
        per = n_loc // tk
        b_spec = pl.BlockSpec((1, tn, tk), lambda i, j, k: (k // per, j, k % per))
    elif stacked_b:
        per = n_loc // tn
        b_spec = pl.BlockSpec((1, tk, tn), lambda i, j, k: (j // per, k, j % per))
    elif tb:
        b_spec = pl.BlockSpec((tn, tk), lambda i, j, k: (j, k))
    else:
        b_spec = pl.BlockSpec((tk, tn), lambda i, j, k: (k, j))
    o_spec = pl.BlockSpec((tm, tn), lambda i, j, k: (i, j))
    in_specs = [a_spec, b_spec] + ([o_spec] if add is not None else [])
    args = (a, b) + ((add,) if add is not None else ())
    if stacked_out:
        per_o = n_loc // tn
        out_spec = pl.BlockSpec((1, tm, tn), lambda i, j, k: (j // per_o, i, j % per_o))
        out_shape = jax.ShapeDtypeStruct((stacked_out, m, n_loc), out_dtype)
    else:
        out_spec, out_shape = o_spec, jax.ShapeDtypeStruct((m, n), out_dtype)
    return pl.pallas_call(
        body, name=name, grid=(m // tm, n // tn, nk), in_specs=in_specs, out_specs=out_spec, out_shape=out_shape,
        scratch_shapes=[pltpu.VMEM((tm, tn), F32)] if nk > 1 else [],
        compiler_params=_params("parallel", "parallel", "arbitrary"))(*args)


def _rowwise(name, body, ins, outs, nrows, ts, ncol=1):
    ts = min(ts, nrows)
    nrow = nrows // ts
    hb = ts // SUBLANE
    nb8 = nrows // SUBLANE
    in_specs, args = [], []
    for kind, arr, cb in ins:
        if kind == "row":
            spec = pl.BlockSpec((ts, cb), lambda j, i: (i, j))
        elif kind == "prev":
            spec = pl.BlockSpec((SUBLANE, cb), lambda j, i: (jnp.maximum(i * hb - 1, 0), j))
        elif kind == "next":
            spec = pl.BlockSpec((SUBLANE, cb), lambda j, i: (jnp.minimum((i + 1) * hb, nb8 - 1), j))
        else:
            spec = pl.BlockSpec((arr.shape[0], cb), lambda j, i: (0, j))
        in_specs.append(spec)
        args.append(arr)
    out_specs, out_shape = [], []
    for kind, rows, ctot, cb, dt in outs:
        if kind == "row":
            out_shape.append(jax.ShapeDtypeStruct((nrows, ctot), dt))
            out_specs.append(pl.BlockSpec((ts, cb), lambda j, i: (i, j)))
        else:
            out_shape.append(jax.ShapeDtypeStruct((rows, ctot), dt))
            out_specs.append(pl.BlockSpec((rows, cb), lambda j, i: (0, j)))

    def kern(*refs):
        body(pl.program_id(1), nrow, *refs)

    return pl.pallas_call(kern, name=name, grid=(ncol, nrow), in_specs=in_specs, out_specs=out_specs,
                          out_shape=out_shape, compiler_params=_params("parallel", "arbitrary"))(*args)


def _colsum(x):
    return jnp.sum(x, axis=0, keepdims=True)


def _acc(i, ref, val):
    @pl.when(i == 0)
    def _():
        ref[...] = val

    @pl.when(i > 0)
    def _():
        ref[...] += val


def _shift_down(x, halo, k):
    xx = jnp.concatenate([halo, x], axis=0)
    return pltpu.roll(xx, k, axis=0)[SUBLANE:, :]


def _shift_up(x, halo, k):
    xx = jnp.concatenate([x, halo], axis=0)
    n = xx.shape[0]
    return pltpu.roll(xx, n - k, axis=0)[: x.shape[0], :]


def _sigmoid(x):
    return 1.0 / (1.0 + jnp.exp(-x))


def _silu(x):
    return x * _sigmoid(x)


def _dsilu(x):
    s = _sigmoid(x)
    return s * (1.0 + x * (1.0 - s))


_GELU_K = 0.7978845608028654
_GELU_C = 0.044715


def _gelu(x):
    return 0.5 * x * (1.0 + jnp.tanh(_GELU_K * (x + _GELU_C * x * x * x)))


def _dgelu(x):
    t = jnp.tanh(_GELU_K * (x + _GELU_C * x * x * x))
    return 0.5 * (1.0 + t) + 0.5 * x * (1.0 - t * t) * _GELU_K * (1.0 + 3.0 * _GELU_C * x * x)


def _softplus(x):
    return jnp.maximum(x, 0.0) + jnp.log1p(jnp.exp(-jnp.abs(x)))


def _neg_expm1(x):
    poly = -x * (1.0 + x * (0.5 + x * (1.0 / 6.0 + x * (1.0 / 24.0 + x * (1.0 / 120.0)))))
    return jnp.where(x > -0.05, poly, 1.0 - jnp.exp(x))


def _rmsnorm_fwd(h, g, name):
    s, d = h.shape

    def body(i, n, h_ref, g_ref, o_ref):
        x = h_ref[...]
        r = lax.rsqrt(jnp.mean(x * x, axis=-1, keepdims=True) + NORM_EPS)
        o_ref[...] = (x * r * g_ref[...]).astype(BF16)

    return _rowwise(name, body, [("row", h, d), ("vec", g, d)], [("row", None, d, d, BF16)], s, 256)[0]


def _rmsnorm_bwd(dn, h, g, dres, name):
    s, d = h.shape

    def body(i, n, dn_ref, h_ref, g_ref, dres_ref, dh_ref, dg_ref):
        x = h_ref[...]
        dy = dn_ref[...].astype(F32)
        r = lax.rsqrt(jnp.mean(x * x, axis=-1, keepdims=True) + NORM_EPS)
        xh = x * r
        _acc(i, dg_ref, _colsum(dy * xh))
        dxh = dy * g_ref[...]
        dh_ref[...] = dres_ref[...] + r * (dxh - xh * jnp.mean(dxh * xh, axis=-1, keepdims=True))

    return _rowwise(name, body, [("row", dn, d), ("row", h, d), ("vec", g, d), ("row", dres, d)],
                    [("row", None, d, d, F32), ("acc", 1, d, d, F32)], s, 256)


def _final_loss_bwd(h, g, tgt):
    s, d = h.shape

    def body(i, n, h_ref, g_ref, t_ref, dh_ref, dg_ref, loss_ref):
        x = h_ref[...]
        gg = g_ref[...]
        r = lax.rsqrt(jnp.mean(x * x, axis=-1, keepdims=True) + NORM_EPS)
        xh = x * r
        err = xh * gg - t_ref[...]
        part = 0.5 * jnp.sum(jnp.mean(err * err, axis=-1, keepdims=True), axis=0, keepdims=True)
        _acc(i, loss_ref, jnp.broadcast_to(part, (1, LANE)))
        dy = err * (1.0 / d)
        _acc(i, dg_ref, _colsum(dy * xh))
        dxh = dy * gg
        dh_ref[...] = r * (dxh - xh * jnp.mean(dxh * xh, axis=-1, keepdims=True))

    return _rowwise("final_loss_bwd", body, [("row", h, d), ("vec", g, d), ("row", tgt, d)],
                    [("row", None, d, d, F32), ("acc", 1, d, d, F32), ("acc", 1, LANE, LANE, F32)], s, 256)


def _conv_rows(x, halo, w, b):
    y = b + w[3:4, :] * x
    for k in range(CONV_WIDTH - 1):
        y = y + w[k:k + 1, :] * _shift_down(x, halo, CONV_WIDTH - 1 - k)
    return y


def _conv_fwd(x, w, b, silu, name):
    s, c = x.shape
    cb = _tile(c, 512)

    def body(i, n, x_ref, p_ref, w_ref, b_ref, o_ref):
        halo = jnp.where(i == 0, 0.0, p_ref[...])
        y = _conv_rows(x_ref[...], halo, w_ref[...], b_ref[...])
        o_ref[...] = _silu(y) if silu else y

    return _rowwise(name, body, [("row", x, cb), ("prev", x, cb), ("vec", w, cb), ("vec", b, cb)],
                    [("row", None, c, cb, F32)], s, 512, ncol=c // cb)[0]


def _silu_conv_bwd_pre(dy, x, w, b, name):
    s, c = x.shape
    cb = _tile(c, 512)

    def body(i, n, dy_ref, x_ref, p_ref, w_ref, b_ref, o_ref):
        halo = jnp.where(i == 0, 0.0, p_ref[...])
        y = _conv_rows(x_ref[...], halo, w_ref[...], b_ref[...])
        o_ref[...] = dy_ref[...] * _dsilu(y)

    return _rowwise(name, body, [("row", dy, cb), ("row", x, cb), ("prev", x, cb), ("vec", w, cb), ("vec", b, cb)],
                    [("row", None, c, cb, F32)], s, 512, ncol=c // cb)[0]


def _conv_bwd(dy, x, w, name):
    s, c = x.shape
    cb = _tile(c, 512)

    def body(i, n, dy_ref, nx_ref, x_ref, p_ref, w_ref, dx_ref, dw_ref, db_ref):
        d = dy_ref[...]
        xx = x_ref[...]
        wv = w_ref[...]
        nxt = jnp.where(i == n - 1, 0.0, nx_ref[...])
        prv = jnp.where(i == 0, 0.0, p_ref[...])
        dx = wv[3:4, :] * d
        parts = []
        for k in range(CONV_WIDTH - 1):
            sh = CONV_WIDTH - 1 - k
            dx = dx + wv[k:k + 1, :] * _shift_up(d, nxt, sh)
            parts.append(_colsum(d * _shift_down(xx, prv, sh)))
        parts.append(_colsum(d * xx))
        dx_ref[...] = dx.astype(BF16)
        _acc(i, dw_ref, jnp.concatenate(parts, axis=0))
        _acc(i, db_ref, _colsum(d))

    return _rowwise(name, body, [("row", dy, cb), ("next", dy, cb), ("row", x, cb), ("prev", x, cb), ("vec", w, cb)],
                    [("row", None, c, cb, BF16), ("acc", CONV_WIDTH, c, cb, F32), ("acc", 1, c, cb, F32)],
                    s, 512, ncol=c // cb)


def _lru_gate_math(xr, r_pre, i_pre, lam):
    r = _sigmoid(r_pre)
    ig = _sigmoid(i_pre)
    sp = _softplus(-lam)
    log_a = -LRU_C * r * sp
    a = jnp.exp(log_a)
    mult = jnp.sqrt(_neg_expm1(2.0 * log_a))
    return r, ig, sp, a, mult


def _lru_gates_fwd(xr, wr, wi, br, bi, lam):
    s, d = xr.shape
    nh, bw, _ = wr.shape
    ts = min(512, s)

    def body(x_ref, wr_ref, wi_ref, br_ref, bi_ref, lam_ref, a_ref, b_ref):
        x = x_ref[...]
        xb = x.astype(BF16)
        r_pre = jnp.dot(xb, wr_ref[0], preferred_element_type=F32) + br_ref[...]
        i_pre = jnp.dot(xb, wi_ref[0], preferred_element_type=F32) + bi_ref[...]
        _, ig, _, a, mult = _lru_gate_math(x, r_pre, i_pre, lam_ref[...])
        a_ref[...] = a
        b_ref[...] = mult * (ig * x)

    row = pl.BlockSpec((ts, bw), lambda h, i: (i, h))
    wsp = pl.BlockSpec((1, bw, bw), lambda h, i: (h, 0, 0))
    vec = pl.BlockSpec((1, bw), lambda h, i: (0, h))
    return pl.pallas_call(
        body, name="lru_gates_fwd", grid=(nh, s // ts), in_specs=[row, wsp, wsp, vec, vec, vec], out_specs=[row, row],
        out_shape=[jax.ShapeDtypeStruct((s, d), F32)] * 2, compiler_params=_params("parallel", "arbitrary"),
    )(xr, wr, wi, br, bi, lam)


def _lru_gates_bwd(xr, g, hs, wr, wi, br, bi, lam):
    s, d = xr.shape
    nh, bw, _ = wr.shape
    ts = min(512, s)
    hb = ts // SUBLANE
    tn_dims = (((0,), (0,)), ((), ()))
    nt_dims = (((1,), (1,)), ((), ()))

    def body(x_ref, g_ref, hs_ref, hp_ref, wr_ref, wi_ref, br_ref, bi_ref, lam_ref,
             dx_ref, dwr_ref, dwi_ref, dbr_ref, dbi_ref, dlam_ref):
        i = pl.program_id(1)
        x = x_ref[...]
        xb = x.astype(BF16)
        gg = g_ref[...]
        lam_v = lam_ref[...]
        r_pre = jnp.dot(xb, wr_ref[0], preferred_element_type=F32) + br_ref[...]
        i_pre = jnp.dot(xb, wi_ref[0], preferred_element_type=F32) + bi_ref[...]
        r, ig, sp, a, mult = _lru_gate_math(x, r_pre, i_pre, lam_v)
        h_prev = _shift_down(hs_ref[...], jnp.where(i == 0, 0.0, hp_ref[...]), 1)
        da = gg * h_prev
        dmult = gg * ig * x
        dlog_a = da * a - dmult * (a * a) / mult
        d_r = dlog_a * (-LRU_C * sp)
        dr_pre = d_r * r * (1.0 - r)
        di_pre = (gg * mult * x) * ig * (1.0 - ig)
        drb = dr_pre.astype(BF16)
        dib = di_pre.astype(BF16)
        dx_ref[...] = (gg * mult * ig
                       + lax.dot_general(drb, wr_ref[0], nt_dims, preferred_element_type=F32)
                       + lax.dot_general(dib, wi_ref[0], nt_dims, preferred_element_type=F32))
        dwr = lax.dot_general(xb, drb, tn_dims, preferred_element_type=F32)[None]
        dwi = lax.dot_general(xb, dib, tn_dims, preferred_element_type=F32)[None]
        dlam = _colsum(dlog_a * (-LRU_C * r)) * (-_sigmoid(-lam_v))
        _acc(i, dwr_ref, dwr)
        _acc(i, dwi_ref, dwi)
        _acc(i, dbr_ref, _colsum(dr_pre))
        _acc(i, dbi_ref, _colsum(di_pre))
        _acc(i, dlam_ref, dlam)

    row = pl.BlockSpec((ts, bw), lambda h, i: (i, h))
    prev = pl.BlockSpec((SUBLANE, bw), lambda h, i: (jnp.maximum(i * hb - 1, 0), h))
    wsp = pl.BlockSpec((1, bw, bw), lambda h, i: (h, 0, 0))
    vec = pl.BlockSpec((1, bw), lambda h, i: (0, h))
    return pl.pallas_call(
        body, name="lru_gates_bwd", grid=(nh, s // ts),
        in_specs=[row, row, row, prev, wsp, wsp, vec, vec, vec], out_specs=[row, wsp, wsp, vec, vec, vec],
        out_shape=[jax.ShapeDtypeStruct((s, d), F32), jax.ShapeDtypeStruct((nh, bw, bw), F32),
                   jax.ShapeDtypeStruct((nh, bw, bw), F32)] + [jax.ShapeDtypeStruct((1, d), F32)] * 3,
        compiler_params=_params("parallel", "arbitrary"),
    )(xr, g, hs, hs, wr, wi, br, bi, lam)


def _scan(a, b, reverse, name):
    s, c = a.shape
    cb = _tile(c, 512)
    nt = s // SUBLANE

    def body(a_ref, b_ref, o_ref):
        row = lax.broadcasted_iota(jnp.int32, (SUBLANE, cb), 0)

        def fwd_step(t, carry):
            r0 = pl.multiple_of(t * SUBLANE, SUBLANE)
            aa = a_ref[pl.ds(r0, SUBLANE), :]
            bb = b_ref[pl.ds(r0, SUBLANE), :]
            for sh in (1, 2, 4):
                a_s = jnp.where(row >= sh, pltpu.roll(aa, sh, axis=0), 1.0)
                b_s = jnp.where(row >= sh, pltpu.roll(bb, sh, axis=0), 0.0)
                bb = aa * b_s + bb
                aa = aa * a_s
            h = bb + aa * carry
            o_ref[pl.ds(r0, SUBLANE), :] = h
            return h[SUBLANE - 1:SUBLANE, :]

        def rev_step(k, carry):
            r0 = pl.multiple_of((nt - 1 - k) * SUBLANE, SUBLANE)
            aa = a_ref[pl.ds(r0, SUBLANE), :]
            dd = b_ref[pl.ds(r0, SUBLANE), :]
            cc = aa * dd
            for sh in (1, 2, 4):
                a_s = jnp.where(row < SUBLANE - sh, pltpu.roll(aa, SUBLANE - sh, axis=0), 1.0)
                c_s = jnp.where(row < SUBLANE - sh, pltpu.roll(cc, SUBLANE - sh, axis=0), 0.0)
                cc = cc + aa * c_s
                aa = aa * a_s
            big = cc + aa * carry
            nxt = jnp.where(row < SUBLANE - 1, pltpu.roll(big, SUBLANE - 1, axis=0), carry)
            o_ref[pl.ds(r0, SUBLANE), :] = dd + nxt
            return big[0:1, :]

        lax.fori_loop(0, nt, rev_step if reverse else fwd_step, jnp.zeros((1, cb), F32))

    spec = pl.BlockSpec((s, cb), lambda j: (0, j))
    return pl.pallas_call(body, name=name, grid=(c // cb,), in_specs=[spec, spec], out_specs=spec,
                          out_shape=jax.ShapeDtypeStruct((s, c), F32), compiler_params=_params("parallel"))(a, b)


def _lru_out_fwd(hs, yg):
    s, d = hs.shape
    cb = _tile(d, 1024)

    def body(i, n, h_ref, y_ref, o_ref):
        o_ref[...] = (h_ref[...] * _gelu(y_ref[...])).astype(BF16)

    return _rowwise("lru_out_fwd", body, [("row", hs, cb), ("row", yg, cb)], [("row", None, d, cb, BF16)],
                    s, 512, ncol=d // cb)[0]


def _lru_out_bwd(dy, hs, yg):
    s, d = hs.shape
    cb = _tile(d, 1024)

    def body(i, n, dy_ref, h_ref, y_ref, dh_ref, dyg_ref):
        dyv = dy_ref[...]
        y = y_ref[...]
        dh_ref[...] = dyv * _gelu(y)
        dyg_ref[...] = (dyv * h_ref[...] * _dgelu(y)).astype(BF16)

    return _rowwise("lru_out_bwd", body, [("row", dy, cb), ("row", hs, cb), ("row", yg, cb)],
                    [("row", None, d, cb, F32), ("row", None, d, cb, BF16)], s, 512, ncol=d // cb)


def _swiglu_act(gt, up):
    s, f = gt.shape
    cb = _tile(f, 1024)

    def body(i, n, g_ref, u_ref, o_ref):
        o_ref[...] = (_silu(g_ref[...].astype(F32)) * u_ref[...].astype(F32)).astype(BF16)

    return _rowwise("swiglu_act", body, [("row", gt, cb), ("row", up, cb)], [("row", None, f, cb, BF16)],
                    s, 512, ncol=f // cb)[0]


def _swiglu_bwd(dact, gt, up):
    s, f = gt.shape
    cb = _tile(f, 1024)

    def body(i, n, d_ref, g_ref, u_ref, dg_ref, du_ref):
        d = d_ref[...]
        g = g_ref[...].astype(F32)
        dg_ref[...] = (d * u_ref[...].astype(F32) * _dsilu(g)).astype(BF16)
        du_ref[...] = (d * _silu(g)).astype(BF16)

    return _rowwise("swiglu_bwd", body, [("row", dact, cb), ("row", gt, cb), ("row", up, cb)],
                    [("row", None, f, cb, BF16), ("row", None, f, cb, BF16)], s, 512, ncol=f // cb)


def _ple_fwd(h, gp, pp):
    s, d = h.shape
    cb = _tile(d, 1024)

    def body(i, n, h_ref, g_ref, p_ref, o_ref):
        o_ref[...] = h_ref[...] + _sigmoid(g_ref[...]) * p_ref[...]

    return _rowwise("ple_fwd", body, [("row", h, cb), ("row", gp, cb), ("row", pp, cb)], [("row", None, d, cb, F32)],
                    s, 512, ncol=d // cb)[0]


def _ple_bwd(dh, gp, pp):
    s, d = dh.shape
    cb = _tile(d, 1024)

    def body(i, n, d_ref, g_ref, p_ref, dg_ref, dp_ref):
        dv = d_ref[...]
        sg = _sigmoid(g_ref[...])
        dg_ref[...] = (dv * p_ref[...] * sg * (1.0 - sg)).astype(BF16)
        dp_ref[...] = (dv * sg).astype(BF16)

    return _rowwise("ple_bwd", body, [("row", dh, cb), ("row", gp, cb), ("row", pp, cb)],
                    [("row", None, d, cb, BF16), ("row", None, d, cb, BF16)], s, 512, ncol=d // cb)


def _group_matrix(n_groups):
    r = lax.broadcasted_iota(jnp.int32, (LANE, n_groups * LANE), 0)
    c = lax.broadcasted_iota(jnp.int32, (LANE, n_groups * LANE), 1)
    return ((c % LANE < SSD_HEADS_PER_GROUP) & (r == (c // LANE) * SSD_HEADS_PER_GROUP + c % LANE)).astype(F32)


def _dt_fwd(dt_pre, bias, n_heads, n_groups):
    s = dt_pre.shape[0]
    gl = n_groups * LANE

    def body(i, n, d_ref, b_ref, o_ref):
        lane = lax.broadcasted_iota(jnp.int32, d_ref.shape, 1)
        v = jnp.where(lane < n_heads, _softplus(d_ref[...] + b_ref[...]), 0.0)
        o_ref[...] = jnp.dot(v, _group_matrix(n_groups), preferred_element_type=F32, precision=HIGHEST)

    return _rowwise("ssd_dt_fwd", body, [("row", dt_pre, LANE), ("vec", bias, LANE)], [("row", None, gl, gl, F32)],
                    s, 512)[0]


def _dt_bwd(ddt_g, dt_pre, bias, n_heads, n_groups):
    s = dt_pre.shape[0]
    gl = n_groups * LANE

    def body(i, n, g_ref, d_ref, b_ref, o_ref, db_ref):
        lane = lax.broadcasted_iota(jnp.int32, d_ref.shape, 1)
        ddt = lax.dot_general(g_ref[...], _group_matrix(n_groups), _NT, preferred_element_type=F32, precision=HIGHEST)
        v = jnp.where(lane < n_heads, ddt * _sigmoid(d_ref[...] + b_ref[...]), 0.0)
        o_ref[...] = v.astype(BF16)
        _acc(i, db_ref, _colsum(v))

    return _rowwise("ssd_dt_bwd", body, [("row", ddt_g, gl), ("row", dt_pre, LANE), ("vec", bias, LANE)],
                    [("row", None, LANE, LANE, BF16), ("acc", 1, LANE, LANE, F32)], s, 512)


def _ssd_chunk_terms(dt, alog):
    ln = dt.shape[0]
    a_neg = -jnp.exp(alog)
    row = lax.broadcasted_iota(jnp.int32, (ln, ln), 0)
    col = lax.broadcasted_iota(jnp.int32, (ln, ln), 1)
    tril = row >= col
    cs = jnp.dot(tril.astype(F32), dt * a_neg, preferred_element_type=F32, precision=HIGHEST)
    return a_neg, cs, tril


def _head_lanes(v):
    return [jnp.broadcast_to(v[:, e:e + 1], v.shape) for e in range(SSD_HEADS_PER_GROUP)]


def _ssd_head_scores(bc_cs, cst, cb_mat, tril, e):
    lm = jnp.where(tril, jnp.exp(jnp.minimum(bc_cs[e] - cst[e:e + 1, :], 0.0)), 0.0)
    return (cb_mat * lm).astype(BF16), lm


_NT = (((1,), (1,)), ((), ()))
_TN = (((0,), (0,)), ((), ()))


def _ssd_fwd(xbc, dt, alog, inner, n_groups):
    s = xbc.shape[0]
    ln = SSD_CHUNK
    nc = s // ln
    gw = SSD_HEADS_PER_GROUP * SSD_HEAD_DIM
    npair = gw // LANE
    boff = inner // LANE

    def body(xs_ref, b_ref, c_ref, dt_ref, alog_ref, y_ref, sin_ref, st_ref):
        c = pl.program_id(1)

        @pl.when(c == 0)
        def _():
            st_ref[...] = jnp.zeros_like(st_ref)

        dtv = dt_ref[...]
        _, cs, tril = _ssd_chunk_terms(dtv, alog_ref[...])
        cst = cs.T
        bc_cs, bc_dt = _head_lanes(cs), _head_lanes(dtv)
        xs = xs_ref[...]
        bg = b_ref[...].astype(BF16)
        cg = c_ref[...].astype(BF16)
        cb_mat = lax.dot_general(cg, bg, _NT, preferred_element_type=F32)
        sg = st_ref[...]
        sin_ref[0] = sg
        lo = lax.broadcasted_iota(jnp.int32, (ln, LANE), 1) < SSD_HEAD_DIM
        ys, news = [], []
        for pr in range(npair):
            cols = slice(LANE * pr, LANE * (pr + 1))
            cs_p = jnp.where(lo, bc_cs[2 * pr], bc_cs[2 * pr + 1])
            x = xs[:, cols] * jnp.where(lo, bc_dt[2 * pr], bc_dt[2 * pr + 1])
            tot_p = cs_p[ln - 1:ln, :]
            xp = x.astype(BF16)
            xd = (x * jnp.exp(tot_p - cs_p)).astype(BF16)
            zero = jnp.zeros_like(xp)
            sc0 = _ssd_head_scores(bc_cs, cst, cb_mat, tril, 2 * pr)[0]
            sc1 = _ssd_head_scores(bc_cs, cst, cb_mat, tril, 2 * pr + 1)[0]
            acc = jnp.dot(sc0, jnp.where(lo, xp, zero), preferred_element_type=F32)
            acc = acc + jnp.dot(sc1, jnp.where(lo, zero, xp), preferred_element_type=F32)
            sp = sg[:, cols]
            yoff = jnp.dot(cg, sp.astype(BF16), preferred_element_type=F32) * jnp.exp(cs_p)
            ys.append(acc + yoff)
            news.append(jnp.exp(tot_p) * sp + lax.dot_general(bg, xd, _TN, preferred_element_type=F32))
        y_ref[...] = jnp.concatenate(ys, axis=1)
        st_ref[...] = jnp.concatenate(news, axis=1)

    in_specs = [pl.BlockSpec((ln, gw), lambda g, c: (c, g)),
                pl.BlockSpec((ln, SSD_STATE), lambda g, c: (c, boff + g)),
                pl.BlockSpec((ln, SSD_STATE), lambda g, c: (c, boff + n_groups + g)),
                pl.BlockSpec((ln, LANE), lambda g, c: (c, g)),
                pl.BlockSpec((1, LANE), lambda g, c: (0, g))]
    out_specs = [pl.BlockSpec((ln, gw), lambda g, c: (c, g)),
                 pl.BlockSpec((1, SSD_STATE, gw), lambda g, c: (c, 0, g))]
    return pl.pallas_call(
        body, name="ssd_fwd", grid=(n_groups, nc), in_specs=in_specs, out_specs=out_specs,
        out_shape=[jax.ShapeDtypeStruct((s, inner), F32), jax.ShapeDtypeStruct((nc, SSD_STATE, inner), F32)],
        scratch_shapes=[pltpu.VMEM((SSD_STATE, gw), F32)],
        compiler_params=_params("parallel", "arbitrary"),
    )(xbc, xbc, xbc, dt, alog)


def _ssd_bwd(xbc, dt, alog, dy, y, sin, dskip_e, inner, n_groups):
    s = xbc.shape[0]
    ln = SSD_CHUNK
    nc = s // ln
    gw = SSD_HEADS_PER_GROUP * SSD_HEAD_DIM
    npair = gw // LANE
    boff = inner // LANE

    def body(xs_ref, b_ref, c_ref, dt_ref, alog_ref, dy_ref, y_ref, sin_ref, sout_ref, dsk_ref,
             dxs_ref, db_ref, dc_ref, ddt_ref, dalog_ref, ds_ref):
        step = pl.program_id(1)

        @pl.when(step == 0)
        def _():
            ds_ref[...] = jnp.zeros_like(ds_ref)

        dtv = dt_ref[...]
        a_neg, cs, tril = _ssd_chunk_terms(dtv, alog_ref[...])
        cst = cs.T
        bc_cs, bc_dt = _head_lanes(cs), _head_lanes(dtv)
        xs = xs_ref[...]
        bg = b_ref[...].astype(BF16)
        cg = c_ref[...].astype(BF16)
        cb_mat = lax.dot_general(cg, bg, _NT, preferred_element_type=F32)
        dyv = dy_ref[...]
        yv = y_ref[...]
        dskv = dsk_ref[...]
        s_in = sin_ref[0]
        s_out = sout_ref[0]
        d_s = ds_ref[...]
        lane = lax.broadcasted_iota(jnp.int32, (ln, LANE), 1)
        rowl = lax.broadcasted_iota(jnp.int32, (ln, LANE), 0)
        lo = lane < SSD_HEAD_DIM
        dcb = jnp.zeros((ln, ln), F32)
        dbg = jnp.zeros((ln, SSD_STATE), F32)
        dcg = jnp.zeros((ln, SSD_STATE), F32)
        dcs = jnp.zeros((ln, LANE), F32)
        ddt_x = jnp.zeros((ln, LANE), F32)
        dxs_parts, nds = [], []

        def head_sums(v, pr, into):
            s0 = jnp.sum(jnp.where(lo, v, 0.0), axis=1, keepdims=True)
            s1 = jnp.sum(jnp.where(lo, 0.0, v), axis=1, keepdims=True)
            return into + jnp.where(lane == 2 * pr, s0, 0.0) + jnp.where(lane == 2 * pr + 1, s1, 0.0)

        for pr in range(npair):
            cols = slice(LANE * pr, LANE * (pr + 1))
            cs_p = jnp.where(lo, bc_cs[2 * pr], bc_cs[2 * pr + 1])
            dt_p = jnp.where(lo, bc_dt[2 * pr], bc_dt[2 * pr + 1])
            xs_p = xs[:, cols]
            x = xs_p * dt_p
            tot_p = cs_p[ln - 1:ln, :]
            dec = jnp.exp(tot_p - cs_p)
            xp = x.astype(BF16)
            xd = (x * dec).astype(BF16)
            dy_p = dyv[:, cols]
            dyp = dy_p.astype(BF16)
            dye = (jnp.exp(cs_p) * dy_p).astype(BF16)
            zero = jnp.zeros_like(dyp)
            dxp = jnp.zeros((ln, LANE), F32)
            for e, dym in ((2 * pr, jnp.where(lo, dyp, zero)), (2 * pr + 1, jnp.where(lo, zero, dyp))):
                sc, lm = _ssd_head_scores(bc_cs, cst, cb_mat, tril, e)
                dsc = lax.dot_general(dym, xp, _NT, preferred_element_type=F32)
                dcb = dcb + dsc * lm
                dxp = dxp + lax.dot_general(sc, dym, _TN, preferred_element_type=F32)
            dsp = d_s[:, cols]
            dspb = dsp.astype(BF16)
            dxp = dxp + dec * jnp.dot(bg, dspb, preferred_element_type=F32)
            dcg = dcg + lax.dot_general(dye, s_in[:, cols].astype(BF16), _NT, preferred_element_type=F32)
            dbg = dbg + lax.dot_general(xd, dspb, _NT, preferred_element_type=F32)
            nds.append(jnp.exp(tot_p) * dsp + lax.dot_general(cg, dye, _TN, preferred_element_type=F32))
            dxs_parts.append(dxp * dt_p + dy_p * dskv[:, cols])
            dcs = head_sums(yv[:, cols] * dyp.astype(F32) - xp.astype(F32) * dxp, pr, dcs)
            tot_row = jnp.broadcast_to(_colsum(s_out[:, cols] * dsp), (ln, LANE))
            dcs = head_sums(jnp.where(rowl == ln - 1, tot_row, 0.0), pr, dcs)
            ddt_x = head_sums(dxp * xs_p, pr, ddt_x)
        ds_ref[...] = jnp.concatenate(nds, axis=1)
        dxs_ref[...] = jnp.concatenate(dxs_parts, axis=1)
        dcbb = dcb.astype(BF16)
        dc_ref[...] = dcg + jnp.dot(dcbb, bg, preferred_element_type=F32)
        db_ref[...] = dbg + lax.dot_general(dcbb, cg, _TN, preferred_element_type=F32)
        row = lax.broadcasted_iota(jnp.int32, (ln, ln), 0)
        col = lax.broadcasted_iota(jnp.int32, (ln, ln), 1)
        dadt = jnp.dot((row <= col).astype(F32), dcs, preferred_element_type=F32, precision=HIGHEST)
        ddt_ref[...] = a_neg * dadt + ddt_x
        _acc(step, dalog_ref, _colsum(dadt * dtv) * a_neg)

    def rc(step):
        return nc - 1 - step

    in_specs = [pl.BlockSpec((ln, gw), lambda g, t: (rc(t), g)),
                pl.BlockSpec((ln, SSD_STATE), lambda g, t: (rc(t), boff + g)),
                pl.BlockSpec((ln, SSD_STATE), lambda g, t: (rc(t), boff + n_groups + g)),
                pl.BlockSpec((ln, LANE), lambda g, t: (rc(t), g)),
                pl.BlockSpec((1, LANE), lambda g, t: (0, g)),
                pl.BlockSpec((ln, gw), lambda g, t: (rc(t), g)),
                pl.BlockSpec((ln, gw), lambda g, t: (rc(t), g)),
                pl.BlockSpec((1, SSD_STATE, gw), lambda g, t: (rc(t), 0, g)),
                pl.BlockSpec((1, SSD_STATE, gw), lambda g, t: (jnp.minimum(rc(t) + 1, nc - 1), 0, g)),
                pl.BlockSpec((1, gw), lambda g, t: (0, g))]
    out_specs = [pl.BlockSpec((ln, gw), lambda g, t: (rc(t), g)),
                 pl.BlockSpec((ln, SSD_STATE), lambda g, t: (rc(t), g)),
                 pl.BlockSpec((ln, SSD_STATE), lambda g, t: (rc(t), g)),
                 pl.BlockSpec((ln, LANE), lambda g, t: (rc(t), g)),
                 pl.BlockSpec((1, LANE), lambda g, t: (0, g))]
    gn = n_groups * SSD_STATE
    return pl.pallas_call(
        body, name="ssd_bwd", grid=(n_groups, nc), in_specs=in_specs, out_specs=out_specs,
        out_shape=[jax.ShapeDtypeStruct((s, inner + 2 * gn), F32), jax.ShapeDtypeStruct((s, gn), F32),
                   jax.ShapeDtypeStruct((s, gn), F32), jax.ShapeDtypeStruct((s, n_groups * LANE), F32),
                   jax.ShapeDtypeStruct((1, n_groups * LANE), F32)],
        scratch_shapes=[pltpu.VMEM((SSD_STATE, gw), F32)],
        compiler_params=_params("parallel", "arbitrary"),
    )(xbc, xbc, xbc, dt, alog, dy, y, sin, sin, dskip_e)


def _put_cols(buf, part, block, name):
    s, w = part.shape
    ts = min(512, s)

    def body(b_ref, p_ref, o_ref):
        o_ref[...] = p_ref[...]

    return pl.pallas_call(
        body, name=name, grid=(s // ts,), in_specs=[_ANY, pl.BlockSpec((ts, w), lambda i: (i, 0))],
        out_specs=pl.BlockSpec((ts, w), lambda i: (i, block)), out_shape=jax.ShapeDtypeStruct(buf.shape, buf.dtype),
        input_output_aliases={0: 0}, compiler_params=_params("parallel"))(buf, part)


def _ssd_gate_norm_fwd(ysc, xbc, z, dskip_e, norm_g, n_groups):
    s, inner = ysc.shape
    gsz = inner // n_groups

    def body(i, n, y_ref, x_ref, z_ref, d_ref, g_ref, o_ref):
        y2 = (y_ref[...] + d_ref[...] * x_ref[...]) * _silu(z_ref[...])
        gg = g_ref[...]
        outs = []
        for k in range(n_groups):
            cols = slice(k * gsz, (k + 1) * gsz)
            v = y2[:, cols]
            r = lax.rsqrt(jnp.mean(v * v, axis=-1, keepdims=True) + NORM_EPS)
            outs.append(v * r * gg[:, cols])
        o_ref[...] = jnp.concatenate(outs, axis=1).astype(BF16)

    return _rowwise("ssd_gate_norm_fwd", body,
                    [("row", ysc, inner), ("row", xbc, inner), ("row", z, inner), ("vec", dskip_e, inner),
                     ("vec", norm_g, inner)], [("row", None, inner, inner, BF16)], s, 128)[0]


def _ssd_gate_norm_bwd(dyn, ysc, xbc, z, dskip_e, norm_g, n_groups):
    s, inner = ysc.shape
    gsz = inner // n_groups

    def body(i, n, dn_ref, y_ref, x_ref, z_ref, d_ref, g_ref, dy_ref, dz_ref, dg_ref, dd_ref):
        xs = x_ref[...]
        zz = z_ref[...]
        y = y_ref[...] + d_ref[...] * xs
        sz = _silu(zz)
        y2 = y * sz
        dn = dn_ref[...]
        gg = g_ref[...]
        dy2s, dgs = [], []
        for k in range(n_groups):
            cols = slice(k * gsz, (k + 1) * gsz)
            v = y2[:, cols]
            d = dn[:, cols]
            r = lax.rsqrt(jnp.mean(v * v, axis=-1, keepdims=True) + NORM_EPS)
            vh = v * r
            dgs.append(_colsum(d * vh))
            dvh = d * gg[:, cols]
            dy2s.append(r * (dvh - vh * jnp.mean(dvh * vh, axis=-1, keepdims=True)))
        dy2 = jnp.concatenate(dy2s, axis=1)
        dy = dy2 * sz
        dy_ref[...] = dy
        dz_ref[...] = (dy2 * y * _dsilu(zz)).astype(BF16)
        _acc(i, dg_ref, jnp.concatenate(dgs, axis=1))
        _acc(i, dd_ref, _colsum(dy * xs))

    return _rowwise("ssd_gate_norm_bwd", body,
                    [("row", dyn, inner), ("row", ysc, inner), ("row", xbc, inner), ("row", z, inner),
                     ("vec", dskip_e, inner), ("vec", norm_g, inner)],
                    [("row", None, inner, inner, F32), ("row", None, inner, inner, BF16),
                     ("acc", 1, inner, inner, F32), ("acc", 1, inner, inner, F32)], s, 128)


def _adamw(w, g, m, v, name, emit_g=False):
    rows, c = w.shape
    bc1 = 1.0 - ADAM_B1 ** ADAM_STEP
    bc2 = 1.0 - ADAM_B2 ** ADAM_STEP

    def body(i, n, w_ref, g_ref, m_ref, v_ref, d_ref, mo_ref, vo_ref, *go_ref):
        gg = g_ref[...]
        if emit_g:
            go_ref[0][...] = gg
        mn = ADAM_B1 * m_ref[...] + (1.0 - ADAM_B1) * gg
        vn = ADAM_B2 * v_ref[...] + (1.0 - ADAM_B2) * (gg * gg)
        d_ref[...] = -ADAM_LR * ((mn / bc1) / (jnp.sqrt(vn / bc2) + ADAM_EPS) + ADAM_WD * w_ref[...])
        mo_ref[...] = mn
        vo_ref[...] = vn

    ts = 128 if rows % 128 == 0 else rows
    return _rowwise(name, body, [("row", w, c), ("row", g, c), ("row", m, c), ("row", v, c)],
                    [("row", None, c, c, F32)] * (4 if emit_g else 3), rows, ts)


_ANY = pl.BlockSpec(memory_space=pl.ANY)


def _place():
    x, y, c = lax.axis_index("x"), lax.axis_index("y"), lax.axis_index("c")
    chips = [(1 - x, y), (x, 1 - y), (1 - x, 1 - y)]
    return x, y, c, chips


def _rcopy(src, dst, ssem, rsem, dev):
    return pltpu.make_async_remote_copy(src_ref=src, dst_ref=dst, send_sem=ssem, recv_sem=rsem, device_id=dev,
                                        device_id_type=MESH)


def _place_shard(shards, layer, q_idx, dtype, name):
    _, r, cc = shards.shape
    tr = _tile(r, 256, 16)

    def body(q_ref, s_ref, o_ref):
        o_ref[0] = s_ref[0].astype(dtype)

    grid_spec = pltpu.PrefetchScalarGridSpec(
        num_scalar_prefetch=1, grid=(r // tr,),
        in_specs=[pl.BlockSpec((1, tr, cc), lambda i, q_ref: (layer, i, 0))],
        out_specs=pl.BlockSpec((1, tr, cc), lambda i, q_ref: (q_ref[0], i, 0)))
    return pl.pallas_call(body, name=name, grid_spec=grid_spec, out_shape=jax.ShapeDtypeStruct((N_CHIPS, r, cc), dtype),
                          compiler_params=_params("parallel"))(q_idx, shards)


_HBM = pl.BlockSpec(memory_space=pltpu.HBM)
_SEM = pl.BlockSpec(memory_space=pltpu.SEMAPHORE)
_EFFECT = pltpu.SideEffectType.DATAFLOW_SIDE_EFFECTING


def _in_hbm(arrs):
    return [pltpu.with_memory_space_constraint(a, pltpu.HBM) for a in arrs]


def _gather_start(bufs, name):
    n = len(bufs)
    half = [e.shape[1] // 2 for e in bufs]

    def body(*refs):
        ins, send, recv, token = refs[:n], refs[n], refs[n + 1], refs[2 * n + 2]
        x, y, c, chips = _place()
        q = 2 * x + y
        for e in range(n):
            blk = ins[e].at[q, pl.ds(c * half[e], half[e])]
            for j, (cx, cy) in enumerate(chips):
                _rcopy(blk, blk, send.at[3 * e + j], recv.at[3 * e + j], (cx, cy, c)).start()
        token[...] = jnp.zeros_like(token)

    out = pl.pallas_call(
        body, name=name,
        out_shape=(pltpu.SemaphoreType.DMA((3 * n,)), pltpu.SemaphoreType.DMA((3 * n,)),
                   *[pltpu.HBM(b.shape, b.dtype) for b in bufs], jax.ShapeDtypeStruct((SUBLANE, LANE), F32)),
        in_specs=[_HBM] * n, out_specs=(_SEM, _SEM, *[_HBM] * n, pl.BlockSpec(memory_space=pltpu.VMEM)),
        input_output_aliases={e: 2 + e for e in range(n)},
        compiler_params=pltpu.CompilerParams(has_side_effects=_EFFECT))(*_in_hbm(bufs))
    return out[0], out[1], list(out[2:2 + n]), out[2 + n]


def _gather_wait(send, recv, bufs, after, name):
    n = len(bufs)
    half = [e.shape[1] // 2 for e in bufs]

    def body(*refs):
        ins, send_ref, recv_ref = refs[:n], refs[n], refs[n + 1]
        x, y, c, chips = _place()
        q = 2 * x + y
        for e in range(n):
            rows = pl.ds(c * half[e], half[e])
            for j, (cx, cy) in enumerate(chips):
                cp = _rcopy(ins[e].at[q, rows], ins[e].at[2 * cx + cy, rows], send_ref.at[3 * e + j],
                            recv_ref.at[3 * e + j], (cx, cy, c))
                cp.wait_send()
                cp.wait_recv()

    return list(pl.pallas_call(
        body, name=name, out_shape=tuple(pltpu.HBM(b.shape, b.dtype) for b in bufs),
        in_specs=[_HBM] * n + [_SEM, _SEM, _ANY], out_specs=[_HBM] * n,
        input_output_aliases={e: e for e in range(n)},
        compiler_params=pltpu.CompilerParams(has_side_effects=_EFFECT))(*bufs, send, recv, after))


def _forward_sibling(bufs, name):
    n = len(bufs)
    half = [e.shape[1] // 2 for e in bufs]

    def body(*refs):
        outs = refs[n:2 * n]
        send, recv = refs[2 * n:]
        x, y, c, chips = _place()
        sib = (x, y, 1 - c)
        cps = []
        for e in range(n):
            for j, (cx, cy) in enumerate(chips):
                blk = outs[e].at[2 * cx + cy, pl.ds(c * half[e], half[e])]
                cps.append(_rcopy(blk, blk, send.at[3 * e + j], recv.at[3 * e + j], sib))
        for cp in cps:
            cp.start()
        for e in range(n):
            for j, (cx, cy) in enumerate(chips):
                blk = outs[e].at[2 * cx + cy, pl.ds((1 - c) * half[e], half[e])]
                _rcopy(blk, blk, send.at[3 * e + j], recv.at[3 * e + j], sib).wait_recv()
        for cp in cps:
            cp.wait_send()

    return list(pl.pallas_call(
        body, name=name, in_specs=[_ANY] * n, out_specs=[_ANY] * n,
        out_shape=[jax.ShapeDtypeStruct(e.shape, e.dtype) for e in bufs],
        input_output_aliases={e: e for e in range(n)},
        scratch_shapes=[pltpu.SemaphoreType.DMA((3 * n,))] * 2,
    )(*bufs))


def _scatter_start(entries, name):
    n = len(entries)
    lands = [lax.empty(e.shape, e.dtype) for e in entries]

    def body(*refs):
        ins, land, send, recv, token = refs[:n], refs[n:2 * n], refs[2 * n], refs[2 * n + 1], refs[4 * n + 2]
        x, y, c, chips = _place()
        q = 2 * x + y
        for e in range(n):
            for j, (cx, cy) in enumerate(chips):
                _rcopy(ins[e].at[2 * cx + cy], land[e].at[q], send.at[3 * e + j], recv.at[3 * e + j],
                       (cx, cy, c)).start()
        token[...] = jnp.zeros_like(token)

    out = pl.pallas_call(
        body, name=name,
        out_shape=(pltpu.SemaphoreType.DMA((3 * n,)), pltpu.SemaphoreType.DMA((3 * n,)),
                   *[pltpu.HBM(b.shape, b.dtype) for b in entries + lands],
                   jax.ShapeDtypeStruct((SUBLANE, LANE), F32)),
        in_specs=[_HBM] * (2 * n),
        out_specs=(_SEM, _SEM, *[_HBM] * (2 * n), pl.BlockSpec(memory_space=pltpu.VMEM)),
        input_output_aliases={e: 2 + e for e in range(2 * n)},
        compiler_params=pltpu.CompilerParams(has_side_effects=_EFFECT))(*_in_hbm(entries + lands))
    return out[0], out[1], list(out[2:2 + n]), list(out[2 + n:2 + 2 * n]), out[2 + 2 * n]


def _scatter_wait(send, recv, entries, lands, after, name):
    n = len(entries)

    def body(*refs):
        ins, land, send_ref, recv_ref = refs[:n], refs[n:2 * n], refs[2 * n], refs[2 * n + 1]
        x, y, c, chips = _place()
        for e in range(n):
            for j, (cx, cy) in enumerate(chips):
                k = 2 * cx + cy
                cp = _rcopy(ins[e].at[k], land[e].at[k], send_ref.at[3 * e + j], recv_ref.at[3 * e + j],
                            (cx, cy, c))
                cp.wait_send()
                cp.wait_recv()

    out = pl.pallas_call(
        body, name=name, out_shape=tuple(pltpu.HBM(b.shape, b.dtype) for b in entries + lands),
        in_specs=[_HBM] * (2 * n) + [_SEM, _SEM, _ANY], out_specs=[_HBM] * (2 * n),
        input_output_aliases={e: e for e in range(2 * n)},
        compiler_params=pltpu.CompilerParams(has_side_effects=_EFFECT))(*entries, *lands, send, recv, after)
    return list(out[:n]), list(out[n:])


def _swap_start(entries, name):
    n = len(entries)
    half = [e.shape[1] // 2 for e in entries]
    lands = [lax.empty((N_CHIPS, h, e.shape[2]), e.dtype) for e, h in zip(entries, half)]

    def body(*refs):
        ins, land, send, recv, token = refs[:n], refs[n:2 * n], refs[2 * n], refs[2 * n + 1], refs[4 * n + 2]
        x, y, c, _ = _place()
        for e in range(n):
            _rcopy(ins[e].at[:, pl.ds((1 - c) * half[e], half[e]), :], land[e], send.at[e], recv.at[e],
                   (x, y, 1 - c)).start()
        token[...] = jnp.zeros_like(token)

    out = pl.pallas_call(
        body, name=name,
        out_shape=(pltpu.SemaphoreType.DMA((n,)), pltpu.SemaphoreType.DMA((n,)),
                   *[pltpu.HBM(b.shape, b.dtype) for b in entries + lands],
                   jax.ShapeDtypeStruct((SUBLANE, LANE), F32)),
        in_specs=[_HBM] * (2 * n),
        out_specs=(_SEM, _SEM, *[_HBM] * (2 * n), pl.BlockSpec(memory_space=pltpu.VMEM)),
        input_output_aliases={e: 2 + e for e in range(2 * n)},
        compiler_params=pltpu.CompilerParams(has_side_effects=_EFFECT))(*_in_hbm(entries + lands))
    return out[0], out[1], list(out[2:2 + n]), list(out[2 + n:2 + 2 * n]), out[2 + 2 * n]


def _swap_wait(send, recv, entries, lands, after, name):
    n = len(entries)
    half = [e.shape[1] // 2 for e in entries]

    def body(*refs):
        ins, land, send_ref, recv_ref = refs[:n], refs[n:2 * n], refs[2 * n], refs[2 * n + 1]
        x, y, c, _ = _place()
        for e in range(n):
            cp = _rcopy(ins[e].at[:, pl.ds((1 - c) * half[e], half[e]), :], land[e], send_ref.at[e], recv_ref.at[e],
                        (x, y, 1 - c))
            cp.wait_send()
            cp.wait_recv()

    out = pl.pallas_call(
        body, name=name, out_shape=tuple(pltpu.HBM(b.shape, b.dtype) for b in entries + lands),
        in_specs=[_HBM] * (2 * n) + [_SEM, _SEM, _ANY], out_specs=[_HBM] * (2 * n),
        input_output_aliases={e: e for e in range(2 * n)},
        compiler_params=pltpu.CompilerParams(has_side_effects=_EFFECT))(*entries, *lands, send, recv, after)
    return list(out[:n]), list(out[n:])


def _swap_halves(entries, name):
    n = len(entries)
    half = [e.shape[1] // 2 for e in entries]

    def body(*refs):
        ins, outs = refs[:n], refs[n:2 * n]
        send, recv = refs[2 * n:]
        x, y, c, _ = _place()
        cps = [_rcopy(ins[e].at[:, pl.ds((1 - c) * half[e], half[e]), :], outs[e], send.at[e], recv.at[e],
                      (x, y, 1 - c)) for e in range(n)]
        for cp in cps:
            cp.start()
        for cp in cps:
            cp.wait()

    return pl.pallas_call(
        body, name=name, in_specs=[_ANY] * n, out_specs=[_ANY] * n,
        out_shape=[jax.ShapeDtypeStruct((N_CHIPS, h, e.shape[2]), e.dtype) for e, h in zip(entries, half)],
        scratch_shapes=[pltpu.SemaphoreType.DMA((n,))] * 2,
    )(*entries)


def _join_halves(bufs, name):
    n = len(bufs)
    pairs = [(o, layer) for o in range(n) for layer in range(bufs[o].shape[0])]
    npair = len(pairs)

    def body(*refs):
        outs = refs[n:2 * n]
        send, recv = refs[2 * n:]
        x, y, c, _ = _place()
        cps = []
        for k, (o, layer) in enumerate(pairs):
            r2 = bufs[o].shape[1] // 2
            blk = outs[o].at[layer, pl.ds(c * r2, r2)]
            cps.append(_rcopy(blk, blk, send.at[k], recv.at[k], (x, y, 1 - c)))
        for cp in cps:
            cp.start()
        for k, (o, layer) in enumerate(pairs):
            r2 = bufs[o].shape[1] // 2
            blk = outs[o].at[layer, pl.ds((1 - c) * r2, r2)]
            _rcopy(blk, blk, send.at[k], recv.at[k], (x, y, 1 - c)).wait_recv()
        for cp in cps:
            cp.wait_send()

    return pl.pallas_call(
        body, name=name, in_specs=[_ANY] * n, out_specs=[_ANY] * n,
        out_shape=[jax.ShapeDtypeStruct(b.shape, b.dtype) for b in bufs],
        input_output_aliases={e: e for e in range(n)},
        scratch_shapes=[pltpu.SemaphoreType.DMA((npair,))] * 2,
    )(*bufs)


def _gather_all(v, after, name):
    def body(v_ref, after_ref, o_ref, send, recv, loc):
        x, y, c, _ = _place()
        me = 4 * x + 2 * y + c
        mine = pltpu.make_async_copy(v_ref, o_ref.at[me], loc)
        mine.start()
        peers = []
        for k in range(1, N_DEV):
            px = 1 - x if k & 4 else x
            py = 1 - y if k & 2 else y
            pc = 1 - c if k & 1 else c
            peers.append((px, py, pc))
        cps = [_rcopy(v_ref, o_ref.at[me], send.at[k], recv.at[k], peers[k]) for k in range(N_DEV - 1)]
        for cp in cps:
            cp.start()
        for k, (px, py, pc) in enumerate(peers):
            blk = o_ref.at[4 * px + 2 * py + pc]
            _rcopy(blk, blk, send.at[k], recv.at[k], (px, py, pc)).wait_recv()
        for cp in cps:
            cp.wait_send()
        mine.wait()

    return pl.pallas_call(
        body, name=name, in_specs=[_ANY, _ANY], out_specs=_ANY,
        out_shape=jax.ShapeDtypeStruct((N_DEV,) + v.shape, v.dtype),
        scratch_shapes=[pltpu.SemaphoreType.DMA((N_DEV - 1,))] * 2 + [pltpu.SemaphoreType.DMA],
    )(v, after)


def _add_own_half(gst, rx, c_idx, name):
    _, r, cc = gst.shape
    r2 = r // 2
    tr = _tile(r2, 256, 16)
    g4 = gst.reshape(N_CHIPS, 2, r2, cc)

    def body(c_ref, g_ref, r_ref, o_ref):
        o_ref[...] = (g_ref[0].astype(F32) + r_ref[...].astype(F32)).astype(BF16)

    grid_spec = pltpu.PrefetchScalarGridSpec(
        num_scalar_prefetch=1, grid=(N_CHIPS, r2 // tr),
        in_specs=[pl.BlockSpec((1, 1, tr, cc), lambda k, i, c_ref: (k, c_ref[0], i, 0)),
                  pl.BlockSpec((1, tr, cc), lambda k, i, c_ref: (k, i, 0))],
        out_specs=pl.BlockSpec((1, tr, cc), lambda k, i, c_ref: (k, i, 0)))
    return pl.pallas_call(body, name=name, grid_spec=grid_spec, out_shape=jax.ShapeDtypeStruct((N_CHIPS, r2, cc), BF16),
                          compiler_params=_params("parallel", "parallel"))(c_idx, g4, rx)


def _sum_into(buf, rx, own, layer, qc, out_shape, name):
    _, r2, cc = rx.shape
    tr = _tile(r2, 256, 16)
    nb = r2 // tr

    def body(qc_ref, *refs):
        rx_ref, own_ref, o_ref = refs[-3:]
        q = qc_ref[0]
        acc = None
        for k in range(N_CHIPS):
            v = jnp.where(q == k, own_ref[0], rx_ref[k]).astype(F32)
            acc = v if acc is None else acc + v
        o_ref[0] = acc

    in_specs = [pl.BlockSpec((N_CHIPS, tr, cc), lambda i, qc_ref: (0, i, 0)),
                pl.BlockSpec((1, tr, cc), lambda i, qc_ref: (qc_ref[0], i, 0))]
    args = (rx, own)
    aliases = {}
    if buf is not None:
        in_specs = [_ANY] + in_specs
        args = (buf,) + args
        aliases = {1: 0}
    grid_spec = pltpu.PrefetchScalarGridSpec(
        num_scalar_prefetch=1, grid=(nb,), in_specs=in_specs,
        out_specs=pl.BlockSpec((1, tr, cc), lambda i, qc_ref: (layer, qc_ref[1] * nb + i, 0)))
    return pl.pallas_call(body, name=name, grid_spec=grid_spec, out_shape=jax.ShapeDtypeStruct(out_shape, F32),
                          input_output_aliases=aliases, compiler_params=_params("parallel"))(qc, *args)


def _sum_slots(st, name):
    k, r, cc = st.shape
    tr = _tile(r, 256, 8)

    def body(s_ref, o_ref):
        acc = s_ref[0].astype(F32)
        for j in range(1, k):
            acc = acc + s_ref[j].astype(F32)
        o_ref[...] = acc

    return pl.pallas_call(body, name=name, grid=(r // tr,), in_specs=[pl.BlockSpec((k, tr, cc), lambda i: (0, i, 0))],
                          out_specs=pl.BlockSpec((tr, cc), lambda i: (i, 0)),
                          out_shape=jax.ShapeDtypeStruct((r, cc), F32), compiler_params=_params("parallel"))(st)


def _pack(arrs, rows_mult=2 * SUBLANE):
    flat = jnp.concatenate([a.reshape(-1).astype(F32) for a in arrs])
    quantum = rows_mult * LANE
    padded = -(-flat.shape[0] // quantum) * quantum
    return jnp.pad(flat, (0, padded - flat.shape[0])).reshape(-1, LANE)


def _unpack(buf, shapes):
    flat = buf.reshape(-1)
    out, off = [], 0
    for sh in shapes:
        size = 1
        for d in sh:
            size *= d
        out.append(flat[off:off + size].reshape(sh))
        off += size
    return out


def kernel(x, p, norm_mix_g, norm_ffn_g, norm_ple_g, final_norm_g, a_w_in, a_conv_w, a_conv_b, a_w_gate_r, a_b_gate_r, a_w_gate_i, a_b_gate_i, a_lambda, a_w_out, b_w_in, b_conv_w, b_conv_b, b_dt_bias, b_a_log, b_d_skip, b_norm_g, b_w_out, ffn_w_gate, ffn_w_up, ffn_w_down, ple_w_proj, ple_w_gate, loss_target, m_norm_mix_g, m_norm_ffn_g, m_norm_ple_g, m_final_norm_g, m_a_w_in, m_a_conv_w, m_a_conv_b, m_a_w_gate_r, m_a_b_gate_r, m_a_w_gate_i, m_a_b_gate_i, m_a_lambda, m_a_w_out, m_b_w_in, m_b_conv_w, m_b_conv_b, m_b_dt_bias, m_b_a_log, m_b_d_skip, m_b_norm_g, m_b_w_out, m_ffn_w_gate, m_ffn_w_up, m_ffn_w_down, m_ple_w_proj, m_ple_w_gate, v_norm_mix_g, v_norm_ffn_g, v_norm_ple_g, v_final_norm_g, v_a_w_in, v_a_conv_w, v_a_conv_b, v_a_w_gate_r, v_a_b_gate_r, v_a_w_gate_i, v_a_b_gate_i, v_a_lambda, v_a_w_out, v_b_w_in, v_b_conv_w, v_b_conv_b, v_b_dt_bias, v_b_a_log, v_b_d_skip, v_b_norm_g, v_b_w_out, v_ffn_w_gate, v_ffn_w_up, v_ffn_w_down, v_ple_w_proj, v_ple_w_gate):
    names = ["norm_mix_g", "norm_ffn_g", "norm_ple_g", "final_norm_g", "a_w_in", "a_conv_w", "a_conv_b", "a_w_gate_r",
             "a_b_gate_r", "a_w_gate_i", "a_b_gate_i", "a_lambda", "a_w_out", "b_w_in", "b_conv_w", "b_conv_b",
             "b_dt_bias", "b_a_log", "b_d_skip", "b_norm_g", "b_w_out", "ffn_w_gate", "ffn_w_up", "ffn_w_down",
             "ple_w_proj", "ple_w_gate"]
    w_in = dict(zip(names, [norm_mix_g, norm_ffn_g, norm_ple_g, final_norm_g, a_w_in, a_conv_w, a_conv_b, a_w_gate_r,
                            a_b_gate_r, a_w_gate_i, a_b_gate_i, a_lambda, a_w_out, b_w_in, b_conv_w, b_conv_b,
                            b_dt_bias, b_a_log, b_d_skip, b_norm_g, b_w_out, ffn_w_gate, ffn_w_up, ffn_w_down,
                            ple_w_proj, ple_w_gate]))
    m_in = dict(zip(names, [m_norm_mix_g, m_norm_ffn_g, m_norm_ple_g, m_final_norm_g, m_a_w_in, m_a_conv_w,
                            m_a_conv_b, m_a_w_gate_r, m_a_b_gate_r, m_a_w_gate_i, m_a_b_gate_i, m_a_lambda,
                            m_a_w_out, m_b_w_in, m_b_conv_w, m_b_conv_b, m_b_dt_bias, m_b_a_log, m_b_d_skip,
                            m_b_norm_g, m_b_w_out, m_ffn_w_gate, m_ffn_w_up, m_ffn_w_down, m_ple_w_proj,
                            m_ple_w_gate]))
    v_in = dict(zip(names, [v_norm_mix_g, v_norm_ffn_g, v_norm_ple_g, v_final_norm_g, v_a_w_in, v_a_conv_w,
                            v_a_conv_b, v_a_w_gate_r, v_a_b_gate_r, v_a_w_gate_i, v_a_b_gate_i, v_a_lambda,
                            v_a_w_out, v_b_w_in, v_b_conv_w, v_b_conv_b, v_b_dt_bias, v_b_a_log, v_b_d_skip,
                            v_b_norm_g, v_b_w_out, v_ffn_w_gate, v_ffn_w_up, v_ffn_w_down, v_ple_w_proj,
                            v_ple_w_gate]))

    s, d = x.shape[1], x.shape[2]
    depth = norm_mix_g.shape[0]
    assert depth == 2
    q_idx = 2 * lax.axis_index("x") + lax.axis_index("y")
    c_idx = lax.axis_index("c").astype(jnp.int32).reshape(1)

    inner = b_w_out.shape[1] * N_CHIPS
    n_heads = inner // SSD_HEAD_DIM
    n_groups = n_heads // SSD_HEADS_PER_GROUP
    gn = n_groups * SSD_STATE
    xbcw = inner + 2 * gn
    assert b_conv_w.shape[2] * N_CHIPS == xbcw and n_heads <= LANE

    big = [("a_w_in", "col"), ("a_w_gate_r", "gate"), ("a_w_gate_i", "gate"), ("a_w_out", "row"),
           ("ffn_w_gate", "col"), ("ffn_w_up", "col"), ("ffn_w_down", "row"), ("ple_w_proj", "col"),
           ("ple_w_gate", "row"), ("b_w_in", "col"), ("b_w_out", "row")]
    kind_of = dict(big)

    def shard2d(name, arr):
        if kind_of[name] == "gate":
            return [arr[l].reshape(-1, arr.shape[-1]) for l in range(arr.shape[0])]
        return [arr[l] for l in range(arr.shape[0])]

    small_sharded = ["a_conv_w", "a_b_gate_r", "a_b_gate_i", "b_conv_w", "b_conv_b", "b_norm_g"]
    small_pack = _pack([w_in[nm] for nm in small_sharded], rows_mult=16)

    q_vec = q_idx.astype(jnp.int32).reshape(1)
    qc_vec = jnp.stack([q_idx, lax.axis_index("c")]).astype(jnp.int32)
    gather_groups = [
        [("a_w_in", 0), ("a_w_gate_r", 0), ("a_w_gate_i", 0), ("a_w_out", 0), ("small", 0)],
        [("ffn_w_gate", 0), ("ffn_w_up", 0)],
        [("ffn_w_down", 0), ("ple_w_proj", 0), ("ple_w_gate", 0)],
        [("b_w_in", 0), ("b_w_out", 0)],
        [("ffn_w_gate", 1), ("ffn_w_up", 1), ("ffn_w_down", 1), ("ple_w_proj", 1), ("ple_w_gate", 1)],
    ]
    gather_started = [None] * len(gather_groups)

    def gather_begin(gi):
        bufs = []
        for nm, l in gather_groups[gi]:
            if nm == "small":
                bufs.append(_place_shard(small_pack[None], 0, q_vec, F32, "place_small"))
            else:
                arr = w_in[nm]
                arr = arr.reshape(arr.shape[0], -1, arr.shape[-1]) if kind_of[nm] == "gate" else arr
                bufs.append(_place_shard(arr, l, q_vec, BF16, f"place_{nm}{l}"))
        gather_started[gi] = _gather_start(bufs, f"gather_start{gi}")
        return gather_started[gi][3][0:1, 0:1]

    wst = {}

    def gather_finish(gi, after):
        send, recv, thru, _ = gather_started[gi]
        landed = _gather_wait(send, recv, thru, after, f"gather_wait{gi}")
        for k, arr in zip(gather_groups[gi], _forward_sibling(landed, f"gather_forward{gi}")):
            wst[k] = arr

    gather_finish(0, sum(gather_begin(gi) for gi in range(len(gather_groups))))
    small_st = wst[("small", 0)]
    kept_stacked = ("a_w_in", "ffn_w_gate", "ffn_w_up", "ple_w_proj")

    def whole(nm, l):
        st = wst[(nm, l)]
        kind = kind_of[nm]
        if nm in kept_stacked:
            return st
        if kind == "row":
            return st.reshape(-1, st.shape[-1])
        if kind == "col":
            return jnp.concatenate([st[k] for k in range(N_CHIPS)], axis=1)
        heads = w_in[nm].shape[1]
        return st.reshape(N_CHIPS, heads, -1, st.shape[-1]).transpose(1, 0, 2, 3).reshape(heads, -1, st.shape[-1])

    small_parts = [_unpack(small_st[k], [w_in[nm].shape for nm in small_sharded]) for k in range(N_CHIPS)]
    small_full = {nm: jnp.concatenate([small_parts[k][i] for k in range(N_CHIPS)], axis=-1)
                  for i, nm in enumerate(small_sharded)}
    a_cw = small_full["a_conv_w"][0]
    a_br = small_full["a_b_gate_r"][0].reshape(1, -1)
    a_bi = small_full["a_b_gate_i"][0].reshape(1, -1)
    b_cw = small_full["b_conv_w"][0]
    b_cb = small_full["b_conv_b"]
    b_ng = small_full["b_norm_g"]

    def pad_lanes(v):
        return jnp.pad(v, ((0, 0), (0, LANE - v.shape[1])))

    dt_bias = pad_lanes(b_dt_bias)
    a_log = pad_lanes(b_a_log.reshape(n_groups, SSD_HEADS_PER_GROUP)).reshape(1, n_groups * LANE)
    dskip_e = jnp.repeat(b_d_skip, SSD_HEAD_DIM, axis=1)

    w_a_in = whole("a_w_in", 0)
    w_ax, w_ay = w_a_in[:N_CHIPS // 2], w_a_in[N_CHIPS // 2:]
    w_ar, w_ai, w_ao = whole("a_w_gate_r", 0), whole("a_w_gate_i", 0), whole("a_w_out", 0)
    w_fg, w_fu, w_fd, w_pp, w_pg = ([None] * depth for _ in range(5))

    def take_ffn_in(l):
        w_fg[l], w_fu[l] = whole("ffn_w_gate", l), whole("ffn_w_up", l)

    def take_ffn_out_ple(l):
        w_fd[l], w_pp[l], w_pg[l] = whole("ffn_w_down", l), whole("ple_w_proj", l), whole("ple_w_gate", l)

    grads = {}

    h0 = x[0]
    g_mix = [norm_mix_g[l:l + 1] for l in range(depth)]
    g_ffn = [norm_ffn_g[l:l + 1] for l in range(depth)]
    g_ple = [norm_ple_g[l:l + 1] for l in range(depth)]
    g_fin = final_norm_g.reshape(1, -1)

    u0 = _rmsnorm_fwd(h0, g_mix[0], "norm_mix0")
    xr_pre = _mm(u0, w_ax, stacked_b=True, name="lru_in_x")
    yg = _mm(u0, w_ay, stacked_b=True, name="lru_in_y")
    xr = _conv_fwd(xr_pre, a_cw, a_conv_b, False, "lru_conv")
    lru_a, lru_b = _lru_gates_fwd(xr, w_ar, w_ai, a_br, a_bi, a_lambda)
    hs = _scan(lru_a, lru_b, False, "lru_scan")
    y_lru = _lru_out_fwd(hs, yg)
    h_mix = [_mm(y_lru, w_ao, add=h0, name="lru_out"), None]

    def ffn_ple_fwd(h_in, l, before_down=None):
        n_f = _rmsnorm_fwd(h_in, g_ffn[l], f"norm_ffn{l}")
        gt = _mm(n_f, w_fg[l], stacked_b=True, out_dtype=BF16, name=f"ffn_gate{l}")
        up = _mm(n_f, w_fu[l], stacked_b=True, out_dtype=BF16, name=f"ffn_up{l}")
        act = _swiglu_act(gt, up)
        if before_down is not None:
            before_down(act)
        h_f = _mm(act, w_fd[l], add=h_in, name=f"ffn_down{l}")
        n_p = _rmsnorm_fwd(h_f, g_ple[l], f"norm_ple{l}")
        gp = _mm(n_p, w_pg[l], name=f"ple_gate{l}")
        pp = _mm(p[l, 0], w_pp[l], stacked_b=True, name=f"ple_proj{l}")
        h_out = _ple_fwd(h_f, gp, pp)
        return h_out, dict(h_in=h_in, n_f=n_f, gt=gt, up=up, act=act, h_f=h_f, n_p=n_p, gp=gp, pp=pp)

    gather_finish(1, h_mix[0])
    take_ffn_in(0)

    def finish_ffn_out_ple0(act):
        gather_finish(2, act)
        take_ffn_out_ple(0)

    h_l0, sv0 = ffn_ple_fwd(h_mix[0], 0, finish_ffn_out_ple0)

    gather_finish(3, h_l0)
    w_b_in = whole("b_w_in", 0)
    w_bz, w_bx = w_b_in[:, :inner], w_b_in[:, inner:inner + xbcw]
    w_bd = pad_lanes(w_b_in[:, inner + xbcw:])
    w_bo = whole("b_w_out", 0)
    u1 = _rmsnorm_fwd(h_l0, g_mix[1], "norm_mix1")
    z = _mm(u1, w_bz, name="ssd_in_z")
    xbc_pre = _mm(u1, w_bx, name="ssd_in_xbc")
    dt_pre = _mm(u1, w_bd, name="ssd_in_dt")
    xbc = _conv_fwd(xbc_pre, b_cw, b_cb, True, "ssd_conv")
    dt = _dt_fwd(dt_pre, dt_bias, n_heads, n_groups)
    ysc, s_in = _ssd_fwd(xbc, dt, a_log, inner, n_groups)
    yn = _ssd_gate_norm_fwd(ysc, xbc, z, dskip_e, b_ng, n_groups)
    h_mix[1] = _mm(yn, w_bo, add=h_l0, name="ssd_out")
    gather_finish(4, h_mix[1])
    take_ffn_in(1)
    take_ffn_out_ple(1)
    h_l1, sv1 = ffn_ple_fwd(h_mix[1], 1)

    dh, dg_fin, loss_row = _final_loss_bwd(h_l1, g_fin, loss_target[0])

    d_norm_ffn, d_norm_ple, d_norm_mix = [None] * depth, [None] * depth, [None] * depth
    for nm in ("ffn_w_gate", "ffn_w_up", "ffn_w_down", "ple_w_proj", "ple_w_gate"):
        grads[nm] = [None] * depth

    def stacked(nm, gfull):
        kind = kind_of[nm]
        if nm in kept_stacked:
            return gfull
        if kind == "row":
            return gfull.reshape(N_CHIPS, -1, gfull.shape[-1])
        if kind == "col":
            n_loc = gfull.shape[1] // N_CHIPS
            return jnp.stack([gfull[:, k * n_loc:(k + 1) * n_loc] for k in range(N_CHIPS)])
        heads, bw, _ = gfull.shape
        return gfull.reshape(heads, N_CHIPS, bw // N_CHIPS, bw).transpose(1, 0, 2, 3).reshape(N_CHIPS, -1, bw)

    reduce_started = []

    def reduce_start(keys, tag):
        gst = [stacked(nm, grads[nm][l]) for nm, l in keys]
        from_sib = _swap_halves(gst, f"reduce_swap_{tag}")
        return reduce_exchange(keys, tag, gst, from_sib)

    def reduce_exchange(keys, tag, gst, from_sib):
        chip_sum = [_add_own_half(g, r, c_idx, f"reduce_add_{nm}{l}") for g, r, (nm, l) in zip(gst, from_sib, keys)]
        send, recv, ents, lands, token = _scatter_start(chip_sum, f"reduce_scatter_start_{tag}")
        reduce_started.append((keys, tag, send, recv, ents, lands))
        return token[0:1, 0:1]

    def reduce_swap_begin(keys, tag):
        gst = [stacked(nm, grads[nm][l]) for nm, l in keys]
        send, recv, ents, lands, token = _swap_start(gst, f"reduce_swap_start_{tag}")
        return (keys, tag, send, recv, ents, lands), token[0:1, 0:1]

    def reduce_swap_end(state, after):
        keys, tag, send, recv, ents, lands = state
        gst, from_sib = _swap_wait(send, recv, ents, lands, after, f"reduce_swap_wait_{tag}")
        return reduce_exchange(keys, tag, gst, from_sib)

    def ffn_ple_keys(l):
        return [("ple_w_gate", l), ("ple_w_proj", l), ("ffn_w_down", l), ("ffn_w_gate", l), ("ffn_w_up", l)]

    def ffn_ple_bwd(dh_out, sv, l, g_ple_l, after_dact=None):
        dgp, dpp = _ple_bwd(dh_out, sv["gp"], sv["pp"])
        grads["ple_w_gate"][l] = _mm(sv["n_p"], dgp, ta=True, out_dtype=BF16, name=f"ple_gate_dw{l}")
        grads["ple_w_proj"][l] = _mm(p[l, 0], dpp, ta=True, out_dtype=BF16, stacked_out=N_CHIPS,
                                     name=f"ple_proj_dw{l}")
        dn = _mm(dgp, w_pg[l], tb=True, name=f"ple_gate_dx{l}")
        dh_f, d_norm_ple[l] = _rmsnorm_bwd(dn, sv["h_f"], g_ple_l, dh_out, f"norm_ple_bwd{l}")
        grads["ffn_w_down"][l] = _mm(sv["act"], dh_f, ta=True, out_dtype=BF16, name=f"ffn_down_dw{l}")
        dact = _mm(dh_f, w_fd[l], tb=True, name=f"ffn_down_dx{l}")
        g_ffn_l = g_ffn[l] if after_dact is None else g_ffn[l] + after_dact(dact)
        dgt, dup = _swiglu_bwd(dact, sv["gt"], sv["up"])
        grads["ffn_w_gate"][l] = _mm(sv["n_f"], dgt, ta=True, out_dtype=BF16, stacked_out=N_CHIPS,
                                     name=f"ffn_gate_dw{l}")
        grads["ffn_w_up"][l] = _mm(sv["n_f"], dup, ta=True, out_dtype=BF16, stacked_out=N_CHIPS, name=f"ffn_up_dw{l}")
        dn = _mm(dgt, w_fg[l], tb=True, stacked_b=True, name=f"ffn_gate_dx{l}")
        dn = _mm(dup, w_fu[l], tb=True, stacked_b=True, add=dn, name=f"ffn_up_dx{l}")
        dh_in, d_norm_ffn[l] = _rmsnorm_bwd(dn, sv["h_in"], g_ffn_l, dh_f, f"norm_ffn_bwd{l}")
        return dh_in

    dh = ffn_ple_bwd(dh, sv1, 1, g_ple[1])
    swapping, tok = reduce_swap_begin(ffn_ple_keys(1), "l1")

    grads["b_w_out"] = [_mm(yn, dh, ta=True, out_dtype=BF16, name="ssd_out_dw")]
    dyn = _mm(dh, w_bo, tb=True, name="ssd_out_dx")
    dy_ssd, dz, d_b_norm_g, dd_lane = _ssd_gate_norm_bwd(dyn, ysc, xbc, z, dskip_e, b_ng + tok, n_groups)
    tok = reduce_swap_end(swapping, dy_ssd)
    dxs, d_bm, d_cm, ddt, d_a_log = _ssd_bwd(xbc, dt, a_log, dy_ssd, ysc, s_in, dskip_e + tok, inner, n_groups)
    dxbc = _put_cols(dxs, d_bm, inner // gn, "ssd_put_db")
    dxbc = _put_cols(dxbc, d_cm, inner // gn + 1, "ssd_put_dc")
    dconv = _silu_conv_bwd_pre(dxbc, xbc_pre, b_cw, b_cb, "ssd_conv_bwd_pre")
    dxbc_pre, d_b_conv_w, d_b_conv_b = _conv_bwd(dconv, xbc_pre, b_cw, "ssd_conv_bwd")
    ddt_pre, d_dt_bias = _dt_bwd(ddt, dt_pre, dt_bias, n_heads, n_groups)
    d_a_log = d_a_log.reshape(n_groups, LANE)[:, :SSD_HEADS_PER_GROUP].reshape(1, n_heads)
    gw_bz = _mm(u1, dz, ta=True, out_dtype=BF16, name="ssd_in_z_dw")
    gw_bx = _mm(u1, dxbc_pre, ta=True, out_dtype=BF16, name="ssd_in_xbc_dw")
    gw_bd = _mm(u1, ddt_pre, ta=True, out_dtype=BF16, name="ssd_in_dt_dw")
    grads["b_w_in"] = [jnp.concatenate([gw_bz, gw_bx, gw_bd[:, :n_heads]], axis=1)]
    du = _mm(dz, w_bz, tb=True, name="ssd_in_z_dx")
    du = _mm(dxbc_pre, w_bx, tb=True, add=du, name="ssd_in_xbc_dx")
    du = _mm(ddt_pre, w_bd, tb=True, add=du, name="ssd_in_dt_dx")
    dh, d_norm_mix[1] = _rmsnorm_bwd(du, h_l0, g_mix[1], dh, "norm_mix_bwd1")
    swapping, tok = reduce_swap_begin([("b_w_out", 0), ("b_w_in", 0)], "ssd")

    dh = ffn_ple_bwd(dh, sv0, 0, g_ple[0] + tok, lambda dact: reduce_swap_end(swapping, dact))
    tok = reduce_start(ffn_ple_keys(0), "l0")

    grads["a_w_out"] = [_mm(y_lru, dh, ta=True, out_dtype=BF16, name="lru_out_dw")]
    dy_lru = _mm(dh, w_ao, tb=True, name="lru_out_dx")
    dhs, dyg = _lru_out_bwd(dy_lru, hs, yg)
    g_scan = _scan(lru_a, dhs, True, "lru_scan_bwd")
    dxr, d_wr, d_wi, d_br, d_bi, d_lam = _lru_gates_bwd(xr, g_scan, hs, w_ar, w_ai, a_br, a_bi, a_lambda + tok)
    dxr_pre, d_a_conv_w, d_a_conv_b = _conv_bwd(dxr, xr_pre, a_cw, "lru_conv_bwd")
    gw_ax = _mm(u0, dxr_pre, ta=True, out_dtype=BF16, stacked_out=N_CHIPS // 2, name="lru_in_x_dw")
    gw_ay = _mm(u0, dyg, ta=True, out_dtype=BF16, stacked_out=N_CHIPS // 2, name="lru_in_y_dw")
    grads["a_w_in"] = [jnp.concatenate([gw_ax, gw_ay], axis=0)]
    grads["a_w_gate_r"] = [d_wr.astype(BF16)]
    grads["a_w_gate_i"] = [d_wi.astype(BF16)]
    du = _mm(dxr_pre, w_ax, tb=True, stacked_b=True, name="lru_in_x_dx")
    du = _mm(dyg, w_ay, tb=True, stacked_b=True, add=du, name="lru_in_y_dx")
    grad_x, d_norm_mix[0] = _rmsnorm_bwd(du, h0, g_mix[0], dh, "norm_mix_bwd0")

    tok = reduce_start([("a_w_out", 0), ("a_w_in", 0), ("a_w_gate_r", 0), ("a_w_gate_i", 0)], "lru")
    grad_out, delta_out, m_out, v_out = {}, {}, {}, {}

    def reduce_finish(groups, after, tag):
        g_half = {}
        for keys, gtag, send, recv, ents, lands in groups:
            ents, lands = _scatter_wait(send, recv, ents, lands, after, f"reduce_scatter_wait_{gtag}")
            for rx, own, (nm, l) in zip(lands, ents, keys):
                sh = shard2d(nm, w_in[nm])
                g_half[nm] = _sum_into(g_half.get(nm), rx, own, l, qc_vec, (len(sh),) + sh[0].shape,
                                       f"reduce_sum_{nm}{l}")
        nms = list(g_half)
        seen = jnp.zeros((1, 1), F32)
        for nm, gfull in zip(nms, _join_halves([g_half[nm] for nm in nms], f"reduce_join_{tag}")):
            shape, cols = w_in[nm].shape, gfull.shape[-1]
            dl, mn, vn, gout = _adamw(w_in[nm].reshape(-1, cols), gfull.reshape(-1, cols),
                                      m_in[nm].reshape(-1, cols), v_in[nm].reshape(-1, cols), f"adamw_{nm}",
                                      emit_g=True)
            grad_out[nm], delta_out[nm] = gout.reshape(shape), dl.reshape(shape)
            m_out[nm], v_out[nm] = mn.reshape(shape), vn.reshape(shape)
            seen = seen + dl[0:1, 0:1]
        return seen

    by_tag = {grp[1]: grp for grp in reduce_started}
    done = reduce_finish([by_tag["ssd"]], grad_x[0:1, 0:1] + tok, "ssd")
    done = reduce_finish([by_tag["l1"], by_tag["l0"]], done, "ffn_ple")
    done = reduce_finish([by_tag["lru"]], done, "lru")

    small_full_grads = {
        "norm_mix_g": jnp.concatenate(d_norm_mix, axis=0), "norm_ffn_g": jnp.concatenate(d_norm_ffn, axis=0),
        "norm_ple_g": jnp.concatenate(d_norm_ple, axis=0), "final_norm_g": dg_fin[0],
        "a_conv_w": d_a_conv_w[None], "a_conv_b": d_a_conv_b,
        "a_b_gate_r": d_br.reshape(a_b_gate_r.shape[0], a_b_gate_r.shape[1], -1),
        "a_b_gate_i": d_bi.reshape(a_b_gate_i.shape[0], a_b_gate_i.shape[1], -1),
        "a_lambda": d_lam, "b_conv_w": d_b_conv_w[None], "b_conv_b": d_b_conv_b,
        "b_dt_bias": d_dt_bias[:, :n_heads], "b_a_log": d_a_log,
        "b_d_skip": dd_lane.reshape(1, n_heads, SSD_HEAD_DIM).sum(axis=-1), "b_norm_g": d_b_norm_g,
    }
    small_names = list(small_full_grads)
    small_shapes = [small_full_grads[nm].shape for nm in small_names]
    packed = _pack([loss_row] + [small_full_grads[nm] for nm in small_names])
    total = _sum_slots(_gather_all(packed, done, "gather_small_grads"), "sum_small_grads")
    parts = _unpack(total, [(1, LANE)] + small_shapes)
    loss = parts[0][0, 0]
    g_small = {}
    for nm, gfull in zip(small_names, parts[1:]):
        if nm in small_sharded:
            n_loc = w_in[nm].shape[-1]
            gfull = lax.dynamic_slice_in_dim(gfull, q_idx * n_loc, n_loc, axis=gfull.ndim - 1)
        g_small[nm] = gfull

    sm_shapes = [w_in[nm].shape for nm in small_names]
    dl, mn, vn = _adamw(_pack([w_in[nm] for nm in small_names]), _pack([g_small[nm] for nm in small_names]),
                        _pack([m_in[nm] for nm in small_names]), _pack([v_in[nm] for nm in small_names]),
                        "adamw_small")
    for nm, a, b_, c_ in zip(small_names, _unpack(dl, sm_shapes), _unpack(mn, sm_shapes), _unpack(vn, sm_shapes)):
        grad_out[nm] = g_small[nm].reshape(w_in[nm].shape)
        delta_out[nm], m_out[nm], v_out[nm] = a, b_, c_

    return (loss, grad_x[None], *[grad_out[nm] for nm in names], *[delta_out[nm] for nm in names],
            *[m_out[nm] for nm in names], *[v_out[nm] for nm in names])
```

```python
import functools

import jax
import jax.numpy as jnp
from jax import lax
from jax.experimental import pallas as pl
from jax.experimental.pallas import tpu as pltpu

F32 = jnp.float32
BF16 = jnp.bfloat16
MESH = pl.DeviceIdType.MESH
HIGHEST = lax.Precision.HIGHEST

NORM_EPS = 1e-6
LRU_C = 8.0
CONV_WIDTH = 4
SSD_HEAD_DIM = 64
SSD_STATE = 128
SSD_CHUNK = 128
SSD_HEADS_PER_GROUP = 8
LANE = 128
SUBLANE = 8
N_CHIPS = 4
N_DEV = 8
VMEM_LIMIT = 48 * 1024 * 1024

ADAM_LR = 0.001
ADAM_B1 = 0.9
ADAM_B2 = 0.999
ADAM_EPS = 1e-08
ADAM_WD = 0.01
ADAM_STEP = 10


def _tile(n, cap, mult=LANE):
    best = None
    for t in range(mult, min(n, cap) + 1, mult):
        if n % t == 0:
            best = t
    return best if best is not None else n


def _params(*sem):
    return pltpu.CompilerParams(dimension_semantics=sem, vmem_limit_bytes=VMEM_LIMIT)


def _mm(a, b, *, ta=False, tb=False, add=None, out_dtype=F32, stacked_b=False, stacked_out=0, name):
    if ta:
        kd, m = a.shape
    else:
        m, kd = a.shape
    n_loc = None
    if stacked_b:
        slots, kb, n_loc = b.shape
        if tb:
            n, kb = kb, slots * n_loc
        else:
            n = slots * n_loc
    elif tb:
        n, kb = b.shape
    else:
        kb, n = b.shape
    assert kd == kb, (a.shape, b.shape, ta, tb)
    tm = _tile(m, 1408)
    tn = _tile(n, 1408)
    tk = _tile(kd, 2048)
    if stacked_b and tb:
        tk = _tile(n_loc, 1408)
    elif stacked_b:
        tn = _tile(n_loc, 1408)
    if stacked_out:
        n_loc = n // stacked_out
        tn = _tile(n_loc, 1408)
    nk = kd // tk
    dims = (((0 if ta else 1,), (1 if tb else 0,)), ((), ()))

    def body(*refs):
        a_ref, b_ref = refs[:2]
        add_ref = refs[2] if add is not None else None
        o_ref = refs[3] if add is not None else refs[2]
        bv = b_ref[0] if stacked_b else b_ref[...]
        part = lax.dot_general(a_ref[...].astype(BF16), bv.astype(BF16), dims, preferred_element_type=F32)

        def finish(r):
            if add is not None:
                r = r + add_ref[...]
            if stacked_out:
                o_ref[0] = r.astype(out_dtype)
            else:
                o_ref[...] = r.astype(out_dtype)

        if nk == 1:
            finish(part)
        else:
            acc_ref = refs[-1]
            k = pl.program_id(2)

            @pl.when(k == 0)
            def _():
                acc_ref[...] = part

            @pl.when((k > 0) & (k < nk - 1))
            def _():
                acc_ref[...] += part

            @pl.when(k == nk - 1)
            def _():
                finish(acc_ref[...] + part)

    a_spec = pl.BlockSpec((tk, tm), lambda i, j, k: (k, i)) if ta else pl.BlockSpec((tm, tk), lambda i, j, k: (i, k))
    if stacked_b and tb:
        per = n_loc // tk
        b_spec = pl.BlockSpec((1, tn, tk), lambda i, j, k: (k // per, j, k % per))
    elif stacked_b:
        per = n_loc // tn
        b_spec = pl.BlockSpec((1, tk, tn), lambda i, j, k: (j // per, k, j % per))
    elif tb:
        b_spec = pl.BlockSpec((tn, tk), lambda i, j, k: (j, k))
    else:
        b_spec = pl.BlockSpec((tk, tn), lambda i, j, k: (k, j))
    o_spec = pl.BlockSpec((tm, tn), lambda i, j, k: (i, j))
    in_specs = [a_spec, b_spec] + ([o_spec] if add is not None else [])
    args = (a, b) + ((add,) if add is not None else ())
    if stacked_out:
        per_o = n_loc // tn
        out_spec = pl.BlockSpec((1, tm, tn), lambda i, j, k: (j // per_o, i, j % per_o))
        out_shape = jax.ShapeDtypeStruct((stacked_out, m, n_loc), out_dtype)
    else:
        out_spec, out_shape = o_spec, jax.ShapeDtypeStruct((m, n), out_dtype)
    return pl.pallas_call(
        body, name=name, grid=(m // tm, n // tn, nk), in_specs=in_specs, out_specs=out_spec, out_shape=out_shape,
        scratch_shapes=[pltpu.VMEM((tm, tn), F32)] if nk > 1 else [],
        compiler_params=_params("parallel", "parallel", "arbitrary"))(*args)


def _rowwise(name, body, ins, outs, nrows, ts, ncol=1):
    ts = min(ts, nrows)
    nrow = nrows // ts
    hb = ts // SUBLANE
    nb8 = nrows // SUBLANE
    in_specs, args = [], []
    for kind, arr, cb in ins:
        if kind == "row":
            spec = pl.BlockSpec((ts, cb), lambda j, i: (i, j))
        elif kind == "prev":
            spec = pl.BlockSpec((SUBLANE, cb), lambda j, i: (jnp.maximum(i * hb - 1, 0), j))
        elif kind == "next":
            spec = pl.BlockSpec((SUBLANE, cb), lambda j, i: (jnp.minimum((i + 1) * hb, nb8 - 1), j))
        else:
            spec = pl.BlockSpec((arr.shape[0], cb), lambda j, i: (0, j))
        in_specs.append(spec)
        args.append(arr)
    out_specs, out_shape = [], []
    for kind, rows, ctot, cb, dt in outs:
        if kind == "row":
            out_shape.append(jax.ShapeDtypeStruct((nrows, ctot), dt))
            out_specs.append(pl.BlockSpec((ts, cb), lambda j, i: (i, j)))
        else:
            out_shape.append(jax.ShapeDtypeStruct((rows, ctot), dt))
            out_specs.append(pl.BlockSpec((rows, cb), lambda j, i: (0, j)))

    def kern(*refs):
        body(pl.program_id(1), nrow, *refs)

    return pl.pallas_call(kern, name=name, grid=(ncol, nrow), in_specs=in_specs, out_specs=out_specs,
                          out_shape=out_shape, compiler_params=_params("parallel", "arbitrary"))(*args)


def _colsum(x):
    return jnp.sum(x, axis=0, keepdims=True)


def _acc(i, ref, val):
    @pl.when(i == 0)
    def _():
        ref[...] = val

    @pl.when(i > 0)
    def _():
        ref[...] += val


def _shift_down(x, halo, k):
    xx = jnp.concatenate([halo, x], axis=0)
    return pltpu.roll(xx, k, axis=0)[SUBLANE:, :]


def _shift_up(x, halo, k):
    xx = jnp.concatenate([x, halo], axis=0)
    n = xx.shape[0]
    return pltpu.roll(xx, n - k, axis=0)[: x.shape[0], :]


def _sigmoid(x):
    return 1.0 / (1.0 + jnp.exp(-x))


def _silu(x):
    return x * _sigmoid(x)


def _dsilu(x):
    s = _sigmoid(x)
    return s * (1.0 + x * (1.0 - s))


_GELU_K = 0.7978845608028654
_GELU_C = 0.044715


def _gelu(x):
    return 0.5 * x * (1.0 + jnp.tanh(_GELU_K * (x + _GELU_C * x * x * x)))


def _dgelu(x):
    t = jnp.tanh(_GELU_K * (x + _GELU_C * x * x * x))
    return 0.5 * (1.0 + t) + 0.5 * x * (1.0 - t * t) * _GELU_K * (1.0 + 3.0 * _GELU_C * x * x)


def _softplus(x):
    return jnp.maximum(x, 0.0) + jnp.log1p(jnp.exp(-jnp.abs(x)))


def _neg_expm1(x):
    poly = -x * (1.0 + x * (0.5 + x * (1.0 / 6.0 + x * (1.0 / 24.0 + x * (1.0 / 120.0)))))
    return jnp.where(x > -0.05, poly, 1.0 - jnp.exp(x))


def _rmsnorm_fwd(h, g, name):
    s, d = h.shape

    def body(i, n, h_ref, g_ref, o_ref):
        x = h_ref[...]
        r = lax.rsqrt(jnp.mean(x * x, axis=-1, keepdims=True) + NORM_EPS)
        o_ref[...] = (x * r * g_ref[...]).astype(BF16)

    return _rowwise(name, body, [("row", h, d), ("vec", g, d)], [("row", None, d, d, BF16)], s, 256)[0]


def _rmsnorm_bwd(dn, h, g, dres, name):
    s, d = h.shape

    def body(i, n, dn_ref, h_ref, g_ref, dres_ref, dh_ref, dg_ref):
        x = h_ref[...]
        dy = dn_ref[...].astype(F32)
        r = lax.rsqrt(jnp.mean(x * x, axis=-1, keepdims=True) + NORM_EPS)
        xh = x * r
        _acc(i, dg_ref, _colsum(dy * xh))
        dxh = dy * g_ref[...]
        dh_ref[...] = dres_ref[...] + r * (dxh - xh * jnp.mean(dxh * xh, axis=-1, keepdims=True))

    return _rowwise(name, body, [("row", dn, d), ("row", h, d), ("vec", g, d), ("row", dres, d)],
                    [("row", None, d, d, F32), ("acc", 1, d, d, F32)], s, 256)


def _final_loss_bwd(h, g, tgt):
    s, d = h.shape

    def body(i, n, h_ref, g_ref, t_ref, dh_ref, dg_ref, loss_ref):
        x = h_ref[...]
        gg = g_ref[...]
        r = lax.rsqrt(jnp.mean(x * x, axis=-1, keepdims=True) + NORM_EPS)
        xh = x * r
        err = xh * gg - t_ref[...]
        part = 0.5 * jnp.sum(jnp.mean(err * err, axis=-1, keepdims=True), axis=0, keepdims=True)
        _acc(i, loss_ref, jnp.broadcast_to(part, (1, LANE)))
        dy = err * (1.0 / d)
        _acc(i, dg_ref, _colsum(dy * xh))
        dxh = dy * gg
        dh_ref[...] = r * (dxh - xh * jnp.mean(dxh * xh, axis=-1, keepdims=True))

    return _rowwise("final_loss_bwd", body, [("row", h, d), ("vec", g, d), ("row", tgt, d)],
                    [("row", None, d, d, F32), ("acc", 1, d, d, F32), ("acc", 1, LANE, LANE, F32)], s, 256)


def _conv_rows(x, halo, w, b):
    y = b + w[3:4, :] * x
    for k in range(CONV_WIDTH - 1):
        y = y + w[k:k + 1, :] * _shift_down(x, halo, CONV_WIDTH - 1 - k)
    return y


def _conv_fwd(x, w, b, silu, name):
    s, c = x.shape
    cb = _tile(c, 512)

    def body(i, n, x_ref, p_ref, w_ref, b_ref, o_ref):
        halo = jnp.where(i == 0, 0.0, p_ref[...])
        y = _conv_rows(x_ref[...], halo, w_ref[...], b_ref[...])
        o_ref[...] = _silu(y) if silu else y

    return _rowwise(name, body, [("row", x, cb), ("prev", x, cb), ("vec", w, cb), ("vec", b, cb)],
                    [("row", None, c, cb, F32)], s, 512, ncol=c // cb)[0]


def _silu_conv_bwd_pre(dy, x, w, b, name):
    s, c = x.shape
    cb = _tile(c, 512)

    def body(i, n, dy_ref, x_ref, p_ref, w_ref, b_ref, o_ref):
        halo = jnp.where(i == 0, 0.0, p_ref[...])
        y = _conv_rows(x_ref[...], halo, w_ref[...], b_ref[...])
        o_ref[...] = dy_ref[...] * _dsilu(y)

    return _rowwise(name, body, [("row", dy, cb), ("row", x, cb), ("prev", x, cb), ("vec", w, cb), ("vec", b, cb)],
                    [("row", None, c, cb, F32)], s, 512, ncol=c // cb)[0]


def _conv_bwd(dy, x, w, name):
    s, c = x.shape
    cb = _tile(c, 512)

    def body(i, n, dy_ref, nx_ref, x_ref, p_ref, w_ref, dx_ref, dw_ref, db_ref):
        d = dy_ref[...]
        xx = x_ref[...]
        wv = w_ref[...]
        nxt = jnp.where(i == n - 1, 0.0, nx_ref[...])
        prv = jnp.where(i == 0, 0.0, p_ref[...])
        dx = wv[3:4, :] * d
        parts = []
        for k in range(CONV_WIDTH - 1):
            sh = CONV_WIDTH - 1 - k
            dx = dx + wv[k:k + 1, :] * _shift_up(d, nxt, sh)
            parts.append(_colsum(d * _shift_down(xx, prv, sh)))
        parts.append(_colsum(d * xx))
        dx_ref[...] = dx.astype(BF16)
        _acc(i, dw_ref, jnp.concatenate(parts, axis=0))
        _acc(i, db_ref, _colsum(d))

    return _rowwise(name, body, [("row", dy, cb), ("next", dy, cb), ("row", x, cb), ("prev", x, cb), ("vec", w, cb)],
                    [("row", None, c, cb, BF16), ("acc", CONV_WIDTH, c, cb, F32), ("acc", 1, c, cb, F32)],
                    s, 512, ncol=c // cb)


def _lru_gate_math(xr, r_pre, i_pre, lam):
    r = _sigmoid(r_pre)
    ig = _sigmoid(i_pre)
    sp = _softplus(-lam)
    log_a = -LRU_C * r * sp
    a = jnp.exp(log_a)
    mult = jnp.sqrt(_neg_expm1(2.0 * log_a))
    return r, ig, sp, a, mult


def _lru_gates_fwd(xr, wr, wi, br, bi, lam):
    s, d = xr.shape
    nh, bw, _ = wr.shape
    ts = min(512, s)

    def body(x_ref, wr_ref, wi_ref, br_ref, bi_ref, lam_ref, a_ref, b_ref):
        x = x_ref[...]
        xb = x.astype(BF16)
        r_pre = jnp.dot(xb, wr_ref[0], preferred_element_type=F32) + br_ref[...]
        i_pre = jnp.dot(xb, wi_ref[0], preferred_element_type=F32) + bi_ref[...]
        _, ig, _, a, mult = _lru_gate_math(x, r_pre, i_pre, lam_ref[...])
        a_ref[...] = a
        b_ref[...] = mult * (ig * x)

    row = pl.BlockSpec((ts, bw), lambda h, i: (i, h))
    wsp = pl.BlockSpec((1, bw, bw), lambda h, i: (h, 0, 0))
    vec = pl.BlockSpec((1, bw), lambda h, i: (0, h))
    return pl.pallas_call(
        body, name="lru_gates_fwd", grid=(nh, s // ts), in_specs=[row, wsp, wsp, vec, vec, vec], out_specs=[row, row],
        out_shape=[jax.ShapeDtypeStruct((s, d), F32)] * 2, compiler_params=_params("parallel", "arbitrary"),
    )(xr, wr, wi, br, bi, lam)


def _lru_gates_bwd(xr, g, hs, wr, wi, br, bi, lam):
    s, d = xr.shape
    nh, bw, _ = wr.shape
    ts = min(512, s)
    hb = ts // SUBLANE
    tn_dims = (((0,), (0,)), ((), ()))
    nt_dims = (((1,), (1,)), ((), ()))

    def body(x_ref, g_ref, hs_ref, hp_ref, wr_ref, wi_ref, br_ref, bi_ref, lam_ref,
             dx_ref, dwr_ref, dwi_ref, dbr_ref, dbi_ref, dlam_ref):
        i = pl.program_id(1)
        x = x_ref[...]
        xb = x.astype(BF16)
        gg = g_ref[...]
        lam_v = lam_ref[...]
        r_pre = jnp.dot(xb, wr_ref[0], preferred_element_type=F32) + br_ref[...]
        i_pre = jnp.dot(xb, wi_ref[0], preferred_element_type=F32) + bi_ref[...]
        r, ig, sp, a, mult = _lru_gate_math(x, r_pre, i_pre, lam_v)
        h_prev = _shift_down(hs_ref[...], jnp.where(i == 0, 0.0, hp_ref[...]), 1)
        da = gg * h_prev
        dmult = gg * ig * x
        dlog_a = da * a - dmult * (a * a) / mult
        d_r = dlog_a * (-LRU_C * sp)
        dr_pre = d_r * r * (1.0 - r)
        di_pre = (gg * mult * x) * ig * (1.0 - ig)
        drb = dr_pre.astype(BF16)
        dib = di_pre.astype(BF16)
        dx_ref[...] = (gg * mult * ig
                       + lax.dot_general(drb, wr_ref[0], nt_dims, preferred_element_type=F32)
                       + lax.dot_general(dib, wi_ref[0], nt_dims, preferred_element_type=F32))
        dwr = lax.dot_general(xb, drb, tn_dims, preferred_element_type=F32)[None]
        dwi = lax.dot_general(xb, dib, tn_dims, preferred_element_type=F32)[None]
        dlam = _colsum(dlog_a * (-LRU_C * r)) * (-_sigmoid(-lam_v))
        _acc(i, dwr_ref, dwr)
        _acc(i, dwi_ref, dwi)
        _acc(i, dbr_ref, _colsum(dr_pre))
        _acc(i, dbi_ref, _colsum(di_pre))
        _acc(i, dlam_ref, dlam)

    row = pl.BlockSpec((ts, bw), lambda h, i: (i, h))
    prev = pl.BlockSpec((SUBLANE, bw), lambda h, i: (jnp.maximum(i * hb - 1, 0), h))
    wsp = pl.BlockSpec((1, bw, bw), lambda h, i: (h, 0, 0))
    vec = pl.BlockSpec((1, bw), lambda h, i: (0, h))
    return pl.pallas_call(
        body, name="lru_gates_bwd", grid=(nh, s // ts),
        in_specs=[row, row, row, prev, wsp, wsp, vec, vec, vec], out_specs=[row, wsp, wsp, vec, vec, vec],
        out_shape=[jax.ShapeDtypeStruct((s, d), F32), jax.ShapeDtypeStruct((nh, bw, bw), F32),
                   jax.ShapeDtypeStruct((nh, bw, bw), F32)] + [jax.ShapeDtypeStruct((1, d), F32)] * 3,
        compiler_params=_params("parallel", "arbitrary"),
    )(xr, g, hs, hs, wr, wi, br, bi, lam)


def _scan(a, b, reverse, name):
    s, c = a.shape
    cb = _tile(c, 512)
    nt = s // SUBLANE

    def body(a_ref, b_ref, o_ref):
        row = lax.broadcasted_iota(jnp.int32, (SUBLANE, cb), 0)

        def fwd_step(t, carry):
            r0 = pl.multiple_of(t * SUBLANE, SUBLANE)
            aa = a_ref[pl.ds(r0, SUBLANE), :]
            bb = b_ref[pl.ds(r0, SUBLANE), :]
            for sh in (1, 2, 4):
                a_s = jnp.where(row >= sh, pltpu.roll(aa, sh, axis=0), 1.0)
                b_s = jnp.where(row >= sh, pltpu.roll(bb, sh, axis=0), 0.0)
                bb = aa * b_s + bb
                aa = aa * a_s
            h = bb + aa * carry
            o_ref[pl.ds(r0, SUBLANE), :] = h
            return h[SUBLANE - 1:SUBLANE, :]

        def rev_step(k, carry):
            r0 = pl.multiple_of((nt - 1 - k) * SUBLANE, SUBLANE)
            aa = a_ref[pl.ds(r0, SUBLANE), :]
            dd = b_ref[pl.ds(r0, SUBLANE), :]
            cc = aa * dd
            for sh in (1, 2, 4):
                a_s = jnp.where(row < SUBLANE - sh, pltpu.roll(aa, SUBLANE - sh, axis=0), 1.0)
                c_s = jnp.where(row < SUBLANE - sh, pltpu.roll(cc, SUBLANE - sh, axis=0), 0.0)
                cc = cc + aa * c_s
                aa = aa * a_s
            big = cc + aa * carry
            nxt = jnp.where(row < SUBLANE - 1, pltpu.roll(big, SUBLANE - 1, axis=0), carry)
            o_ref[pl.ds(r0, SUBLANE), :] = dd + nxt
            return big[0:1, :]

        lax.fori_loop(0, nt, rev_step if reverse else fwd_step, jnp.zeros((1, cb), F32))

    spec = pl.BlockSpec((s, cb), lambda j: (0, j))
    return pl.pallas_call(body, name=name, grid=(c // cb,), in_specs=[spec, spec], out_specs=spec,
                          out_shape=jax.ShapeDtypeStruct((s, c), F32), compiler_params=_params("parallel"))(a, b)


def _lru_out_fwd(hs, yg):
    s, d = hs.shape
    cb = _tile(d, 1024)

    def body(i, n, h_ref, y_ref, o_ref):
        o_ref[...] = (h_ref[...] * _gelu(y_ref[...])).astype(BF16)

    return _rowwise("lru_out_fwd", body, [("row", hs, cb), ("row", yg, cb)], [("row", None, d, cb, BF16)],
                    s, 512, ncol=d // cb)[0]


def _lru_out_bwd(dy, hs, yg):
    s, d = hs.shape
    cb = _tile(d, 1024)

    def body(i, n, dy_ref, h_ref, y_ref, dh_ref, dyg_ref):
        dyv = dy_ref[...]
        y = y_ref[...]
        dh_ref[...] = dyv * _gelu(y)
        dyg_ref[...] = (dyv * h_ref[...] * _dgelu(y)).astype(BF16)

    return _rowwise("lru_out_bwd", body, [("row", dy, cb), ("row", hs, cb), ("row", yg, cb)],
                    [("row", None, d, cb, F32), ("row", None, d, cb, BF16)], s, 512, ncol=d // cb)


def _swiglu_act(gt, up):
    s, f = gt.shape
    cb = _tile(f, 1024)

    def body(i, n, g_ref, u_ref, o_ref):
        o_ref[...] = (_silu(g_ref[...].astype(F32)) * u_ref[...].astype(F32)).astype(BF16)

    return _rowwise("swiglu_act", body, [("row", gt, cb), ("row", up, cb)], [("row", None, f, cb, BF16)],
                    s, 512, ncol=f // cb)[0]


def _swiglu_bwd(dact, gt, up):
    s, f = gt.shape
    cb = _tile(f, 1024)

    def body(i, n, d_ref, g_ref, u_ref, dg_ref, du_ref):
        d = d_ref[...]
        g = g_ref[...].astype(F32)
        dg_ref[...] = (d * u_ref[...].astype(F32) * _dsilu(g)).astype(BF16)
        du_ref[...] = (d * _silu(g)).astype(BF16)

    return _rowwise("swiglu_bwd", body, [("row", dact, cb), ("row", gt, cb), ("row", up, cb)],
                    [("row", None, f, cb, BF16), ("row", None, f, cb, BF16)], s, 512, ncol=f // cb)


def _ple_fwd(h, gp, pp):
    s, d = h.shape
    cb = _tile(d, 1024)

    def body(i, n, h_ref, g_ref, p_ref, o_ref):
        o_ref[...] = h_ref[...] + _sigmoid(g_ref[...]) * p_ref[...]

    return _rowwise("ple_fwd", body, [("row", h, cb), ("row", gp, cb), ("row", pp, cb)], [("row", None, d, cb, F32)],
                    s, 512, ncol=d // cb)[0]


def _ple_bwd(dh, gp, pp):
    s, d = dh.shape
    cb = _tile(d, 1024)

    def body(i, n, d_ref, g_ref, p_ref, dg_ref, dp_ref):
        dv = d_ref[...]
        sg = _sigmoid(g_ref[...])
        dg_ref[...] = (dv * p_ref[...] * sg * (1.0 - sg)).astype(BF16)
        dp_ref[...] = (dv * sg).astype(BF16)

    return _rowwise("ple_bwd", body, [("row", dh, cb), ("row", gp, cb), ("row", pp, cb)],
                    [("row", None, d, cb, BF16), ("row", None, d, cb, BF16)], s, 512, ncol=d // cb)


def _group_matrix(n_groups):
    r = lax.broadcasted_iota(jnp.int32, (LANE, n_groups * LANE), 0)
    c = lax.broadcasted_iota(jnp.int32, (LANE, n_groups * LANE), 1)
    return ((c % LANE < SSD_HEADS_PER_GROUP) & (r == (c // LANE) * SSD_HEADS_PER_GROUP + c % LANE)).astype(F32)


def _dt_fwd(dt_pre, bias, n_heads, n_groups):
    s = dt_pre.shape[0]
    gl = n_groups * LANE

    def body(i, n, d_ref, b_ref, o_ref):
        lane = lax.broadcasted_iota(jnp.int32, d_ref.shape, 1)
        v = jnp.where(lane < n_heads, _softplus(d_ref[...] + b_ref[...]), 0.0)
        o_ref[...] = jnp.dot(v, _group_matrix(n_groups), preferred_element_type=F32, precision=HIGHEST)

    return _rowwise("ssd_dt_fwd", body, [("row", dt_pre, LANE), ("vec", bias, LANE)], [("row", None, gl, gl, F32)],
                    s, 512)[0]


def _dt_bwd(ddt_g, dt_pre, bias, n_heads, n_groups):
    s = dt_pre.shape[0]
    gl = n_groups * LANE

    def body(i, n, g_ref, d_ref, b_ref, o_ref, db_ref):
        lane = lax.broadcasted_iota(jnp.int32, d_ref.shape, 1)
        ddt = lax.dot_general(g_ref[...], _group_matrix(n_groups), _NT, preferred_element_type=F32, precision=HIGHEST)
        v = jnp.where(lane < n_heads, ddt * _sigmoid(d_ref[...] + b_ref[...]), 0.0)
        o_ref[...] = v.astype(BF16)
        _acc(i, db_ref, _colsum(v))

    return _rowwise("ssd_dt_bwd", body, [("row", ddt_g, gl), ("row", dt_pre, LANE), ("vec", bias, LANE)],
                    [("row", None, LANE, LANE, BF16), ("acc", 1, LANE, LANE, F32)], s, 512)


def _ssd_chunk_terms(dt, alog):
    ln = dt.shape[0]
    a_neg = -jnp.exp(alog)
    row = lax.broadcasted_iota(jnp.int32, (ln, ln), 0)
    col = lax.broadcasted_iota(jnp.int32, (ln, ln), 1)
    tril = row >= col
    cs = jnp.dot(tril.astype(F32), dt * a_neg, preferred_element_type=F32, precision=HIGHEST)
    return a_neg, cs, tril


def _head_lanes(v):
    return [jnp.broadcast_to(v[:, e:e + 1], v.shape) for e in range(SSD_HEADS_PER_GROUP)]


def _ssd_head_scores(bc_cs, cst, cb_mat, tril, e):
    lm = jnp.where(tril, jnp.exp(jnp.minimum(bc_cs[e] - cst[e:e + 1, :], 0.0)), 0.0)
    return (cb_mat * lm).astype(BF16), lm


_NT = (((1,), (1,)), ((), ()))
_TN = (((0,), (0,)), ((), ()))


def _ssd_fwd(xbc, dt, alog, inner, n_groups):
    s = xbc.shape[0]
    ln = SSD_CHUNK
    nc = s // ln
    gw = SSD_HEADS_PER_GROUP * SSD_HEAD_DIM
    npair = gw // LANE
    boff = inner // LANE

    def body(xs_ref, b_ref, c_ref, dt_ref, alog_ref, y_ref, sin_ref, st_ref):
        c = pl.program_id(1)

        @pl.when(c == 0)
        def _():
            st_ref[...] = jnp.zeros_like(st_ref)

        dtv = dt_ref[...]
        _, cs, tril = _ssd_chunk_terms(dtv, alog_ref[...])
        cst = cs.T
        bc_cs, bc_dt = _head_lanes(cs), _head_lanes(dtv)
        xs = xs_ref[...]
        bg = b_ref[...].astype(BF16)
        cg = c_ref[...].astype(BF16)
        cb_mat = lax.dot_general(cg, bg, _NT, preferred_element_type=F32)
        sg = st_ref[...]
        sin_ref[0] = sg
        lo = lax.broadcasted_iota(jnp.int32, (ln, LANE), 1) < SSD_HEAD_DIM
        ys, news = [], []
        for pr in range(npair):
            cols = slice(LANE * pr, LANE * (pr + 1))
            cs_p = jnp.where(lo, bc_cs[2 * pr], bc_cs[2 * pr + 1])
            x = xs[:, cols] * jnp.where(lo, bc_dt[2 * pr], bc_dt[2 * pr + 1])
            tot_p = cs_p[ln - 1:ln, :]
            xp = x.astype(BF16)
            xd = (x * jnp.exp(tot_p - cs_p)).astype(BF16)
            zero = jnp.zeros_like(xp)
            sc0 = _ssd_head_scores(bc_cs, cst, cb_mat, tril, 2 * pr)[0]
            sc1 = _ssd_head_scores(bc_cs, cst, cb_mat, tril, 2 * pr + 1)[0]
            acc = jnp.dot(sc0, jnp.where(lo, xp, zero), preferred_element_type=F32)
            acc = acc + jnp.dot(sc1, jnp.where(lo, zero, xp), preferred_element_type=F32)
            sp = sg[:, cols]
            yoff = jnp.dot(cg, sp.astype(BF16), preferred_element_type=F32) * jnp.exp(cs_p)
            ys.append(acc + yoff)
            news.append(jnp.exp(tot_p) * sp + lax.dot_general(bg, xd, _TN, preferred_element_type=F32))
        y_ref[...] = jnp.concatenate(ys, axis=1)
        st_ref[...] = jnp.concatenate(news, axis=1)

    in_specs = [pl.BlockSpec((ln, gw), lambda g, c: (c, g)),
                pl.BlockSpec((ln, SSD_STATE), lambda g, c: (c, boff + g)),
                pl.BlockSpec((ln, SSD_STATE), lambda g, c: (c, boff + n_groups + g)),
                pl.BlockSpec((ln, LANE), lambda g, c: (c, g)),
                pl.BlockSpec((1, LANE), lambda g, c: (0, g))]
    out_specs = [pl.BlockSpec((ln, gw), lambda g, c: (c, g)),
                 pl.BlockSpec((1, SSD_STATE, gw), lambda g, c: (c, 0, g))]
    return pl.pallas_call(
        body, name="ssd_fwd", grid=(n_groups, nc), in_specs=in_specs, out_specs=out_specs,
        out_shape=[jax.ShapeDtypeStruct((s, inner), F32), jax.ShapeDtypeStruct((nc, SSD_STATE, inner), F32)],
        scratch_shapes=[pltpu.VMEM((SSD_STATE, gw), F32)],
        compiler_params=_params("parallel", "arbitrary"),
    )(xbc, xbc, xbc, dt, alog)


def _ssd_bwd(xbc, dt, alog, dy, y, sin, dskip_e, inner, n_groups):
    s = xbc.shape[0]
    ln = SSD_CHUNK
    nc = s // ln
    gw = SSD_HEADS_PER_GROUP * SSD_HEAD_DIM
    npair = gw // LANE
    boff = inner // LANE

    def body(xs_ref, b_ref, c_ref, dt_ref, alog_ref, dy_ref, y_ref, sin_ref, sout_ref, dsk_ref,
             dxs_ref, db_ref, dc_ref, ddt_ref, dalog_ref, ds_ref):
        step = pl.program_id(1)

        @pl.when(step == 0)
        def _():
            ds_ref[...] = jnp.zeros_like(ds_ref)

        dtv = dt_ref[...]
        a_neg, cs, tril = _ssd_chunk_terms(dtv, alog_ref[...])
        cst = cs.T
        bc_cs, bc_dt = _head_lanes(cs), _head_lanes(dtv)
        xs = xs_ref[...]
        bg = b_ref[...].astype(BF16)
        cg = c_ref[...].astype(BF16)
        cb_mat = lax.dot_general(cg, bg, _NT, preferred_element_type=F32)
        dyv = dy_ref[...]
        yv = y_ref[...]
        dskv = dsk_ref[...]
        s_in = sin_ref[0]
        s_out = sout_ref[0]
        d_s = ds_ref[...]
        lane = lax.broadcasted_iota(jnp.int32, (ln, LANE), 1)
        rowl = lax.broadcasted_iota(jnp.int32, (ln, LANE), 0)
        lo = lane < SSD_HEAD_DIM
        dcb = jnp.zeros((ln, ln), F32)
        dbg = jnp.zeros((ln, SSD_STATE), F32)
        dcg = jnp.zeros((ln, SSD_STATE), F32)
        dcs = jnp.zeros((ln, LANE), F32)
        ddt_x = jnp.zeros((ln, LANE), F32)
        dxs_parts, nds = [], []

        def head_sums(v, pr, into):
            s0 = jnp.sum(jnp.where(lo, v, 0.0), axis=1, keepdims=True)
            s1 = jnp.sum(jnp.where(lo, 0.0, v), axis=1, keepdims=True)
            return into + jnp.where(lane == 2 * pr, s0, 0.0) + jnp.where(lane == 2 * pr + 1, s1, 0.0)

        for pr in range(npair):
            cols = slice(LANE * pr, LANE * (pr + 1))
            cs_p = jnp.where(lo, bc_cs[2 * pr], bc_cs[2 * pr + 1])
            dt_p = jnp.where(lo, bc_dt[2 * pr], bc_dt[2 * pr + 1])
            xs_p = xs[:, cols]
            x = xs_p * dt_p
            tot_p = cs_p[ln - 1:ln, :]
            dec = jnp.exp(tot_p - cs_p)
            xp = x.astype(BF16)
            xd = (x * dec).astype(BF16)
            dy_p = dyv[:, cols]
            dyp = dy_p.astype(BF16)
            dye = (jnp.exp(cs_p) * dy_p).astype(BF16)
            zero = jnp.zeros_like(dyp)
            dxp = jnp.zeros((ln, LANE), F32)
            for e, dym in ((2 * pr, jnp.where(lo, dyp, zero)), (2 * pr + 1, jnp.where(lo, zero, dyp))):
                sc, lm = _ssd_head_scores(bc_cs, cst, cb_mat, tril, e)
                dsc = lax.dot_general(dym, xp, _NT, preferred_element_type=F32)
                dcb = dcb + dsc * lm
                dxp = dxp + lax.dot_general(sc, dym, _TN, preferred_element_type=F32)
            dsp = d_s[:, cols]
            dspb = dsp.astype(BF16)
            dxp = dxp + dec * jnp.dot(bg, dspb, preferred_element_type=F32)
            dcg = dcg + lax.dot_general(dye, s_in[:, cols].astype(BF16), _NT, preferred_element_type=F32)
            dbg = dbg + lax.dot_general(xd, dspb, _NT, preferred_element_type=F32)
            nds.append(jnp.exp(tot_p) * dsp + lax.dot_general(cg, dye, _TN, preferred_element_type=F32))
            dxs_parts.append(dxp * dt_p + dy_p * dskv[:, cols])
            dcs = head_sums(yv[:, cols] * dyp.astype(F32) - xp.astype(F32) * dxp, pr, dcs)
            tot_row = jnp.broadcast_to(_colsum(s_out[:, cols] * dsp), (ln, LANE))
            dcs = head_sums(jnp.where(rowl == ln - 1, tot_row, 0.0), pr, dcs)
            ddt_x = head_sums(dxp * xs_p, pr, ddt_x)
        ds_ref[...] = jnp.concatenate(nds, axis=1)
        dxs_ref[...] = jnp.concatenate(dxs_parts, axis=1)
        dcbb = dcb.astype(BF16)
        dc_ref[...] = dcg + jnp.dot(dcbb, bg, preferred_element_type=F32)
        db_ref[...] = dbg + lax.dot_general(dcbb, cg, _TN, preferred_element_type=F32)
        row = lax.broadcasted_iota(jnp.int32, (ln, ln), 0)
        col = lax.broadcasted_iota(jnp.int32, (ln, ln), 1)
        dadt = jnp.dot((row <= col).astype(F32), dcs, preferred_element_type=F32, precision=HIGHEST)
        ddt_ref[...] = a_neg * dadt + ddt_x
        _acc(step, dalog_ref, _colsum(dadt * dtv) * a_neg)

    def rc(step):
        return nc - 1 - step

    in_specs = [pl.BlockSpec((ln, gw), lambda g, t: (rc(t), g)),
                pl.BlockSpec((ln, SSD_STATE), lambda g, t: (rc(t), boff + g)),
                pl.BlockSpec((ln, SSD_STATE), lambda g, t: (rc(t), boff + n_groups + g)),
                pl.BlockSpec((ln, LANE), lambda g, t: (rc(t), g)),
                pl.BlockSpec((1, LANE), lambda g, t: (0, g)),
                pl.BlockSpec((ln, gw), lambda g, t: (rc(t), g)),
                pl.BlockSpec((ln, gw), lambda g, t: (rc(t), g)),
                pl.BlockSpec((1, SSD_STATE, gw), lambda g, t: (rc(t), 0, g)),
                pl.BlockSpec((1, SSD_STATE, gw), lambda g, t: (jnp.minimum(rc(t) + 1, nc - 1), 0, g)),
                pl.BlockSpec((1, gw), lambda g, t: (0, g))]
    out_specs = [pl.BlockSpec((ln, gw), lambda g, t: (rc(t), g)),
                 pl.BlockSpec((ln, SSD_STATE), lambda g, t: (rc(t), g)),
                 pl.BlockSpec((ln, SSD_STATE), lambda g, t: (rc(t), g)),
                 pl.BlockSpec((ln, LANE), lambda g, t: (rc(t), g)),
                 pl.BlockSpec((1, LANE), lambda g, t: (0, g))]
    gn = n_groups * SSD_STATE
    return pl.pallas_call(
        body, name="ssd_bwd", grid=(n_groups, nc), in_specs=in_specs, out_specs=out_specs,
        out_shape=[jax.ShapeDtypeStruct((s, inner + 2 * gn), F32), jax.ShapeDtypeStruct((s, gn), F32),
                   jax.ShapeDtypeStruct((s, gn), F32), jax.ShapeDtypeStruct((s, n_groups * LANE), F32),
                   jax.ShapeDtypeStruct((1, n_groups * LANE), F32)],
        scratch_shapes=[pltpu.VMEM((SSD_STATE, gw), F32)],
        compiler_params=_params("parallel", "arbitrary"),
    )(xbc, xbc, xbc, dt, alog, dy, y, sin, sin, dskip_e)


def _put_cols(buf, part, block, name):
    s, w = part.shape
    ts = min(512, s)

    def body(b_ref, p_ref, o_ref):
        o_ref[...] = p_ref[...]

    return pl.pallas_call(
        body, name=name, grid=(s // ts,), in_specs=[_ANY, pl.BlockSpec((ts, w), lambda i: (i, 0))],
        out_specs=pl.BlockSpec((ts, w), lambda i: (i, block)), out_shape=jax.ShapeDtypeStruct(buf.shape, buf.dtype),
        input_output_aliases={0: 0}, compiler_params=_params("parallel"))(buf, part)


def _ssd_gate_norm_fwd(ysc, xbc, z, dskip_e, norm_g, n_groups):
    s, inner = ysc.shape
    gsz = inner // n_groups

    def body(i, n, y_ref, x_ref, z_ref, d_ref, g_ref, o_ref):
        y2 = (y_ref[...] + d_ref[...] * x_ref[...]) * _silu(z_ref[...])
        gg = g_ref[...]
        outs = []
        for k in range(n_groups):
            cols = slice(k * gsz, (k + 1) * gsz)
            v = y2[:, cols]
            r = lax.rsqrt(jnp.mean(v * v, axis=-1, keepdims=True) + NORM_EPS)
            outs.append(v * r * gg[:, cols])
        o_ref[...] = jnp.concatenate(outs, axis=1).astype(BF16)

    return _rowwise("ssd_gate_norm_fwd", body,
                    [("row", ysc, inner), ("row", xbc, inner), ("row", z, inner), ("vec", dskip_e, inner),
                     ("vec", norm_g, inner)], [("row", None, inner, inner, BF16)], s, 128)[0]


def _ssd_gate_norm_bwd(dyn, ysc, xbc, z, dskip_e, norm_g, n_groups):
    s, inner = ysc.shape
    gsz = inner // n_groups

    def body(i, n, dn_ref, y_ref, x_ref, z_ref, d_ref, g_ref, dy_ref, dz_ref, dg_ref, dd_ref):
        xs = x_ref[...]
        zz = z_ref[...]
        y = y_ref[...] + d_ref[...] * xs
        sz = _silu(zz)
        y2 = y * sz
        dn = dn_ref[...]
        gg = g_ref[...]
        dy2s, dgs = [], []
        for k in range(n_groups):
            cols = slice(k * gsz, (k + 1) * gsz)
            v = y2[:, cols]
            d = dn[:, cols]
            r = lax.rsqrt(jnp.mean(v * v, axis=-1, keepdims=True) + NORM_EPS)
            vh = v * r
            dgs.append(_colsum(d * vh))
            dvh = d * gg[:, cols]
            dy2s.append(r * (dvh - vh * jnp.mean(dvh * vh, axis=-1, keepdims=True)))
        dy2 = jnp.concatenate(dy2s, axis=1)
        dy = dy2 * sz
        dy_ref[...] = dy
        dz_ref[...] = (dy2 * y * _dsilu(zz)).astype(BF16)
        _acc(i, dg_ref, jnp.concatenate(dgs, axis=1))
        _acc(i, dd_ref, _colsum(dy * xs))

    return _rowwise("ssd_gate_norm_bwd", body,
                    [("row", dyn, inner), ("row", ysc, inner), ("row", xbc, inner), ("row", z, inner),
                     ("vec", dskip_e, inner), ("vec", norm_g, inner)],
                    [("row", None, inner, inner, F32), ("row", None, inner, inner, BF16),
                     ("acc", 1, inner, inner, F32), ("acc", 1, inner, inner, F32)], s, 128)


def _adamw(w, g, m, v, name, emit_g=False):
    rows, c = w.shape
    bc1 = 1.0 - ADAM_B1 ** ADAM_STEP
    bc2 = 1.0 - ADAM_B2 ** ADAM_STEP

    def body(i, n, w_ref, g_ref, m_ref, v_ref, d_ref, mo_ref, vo_ref, *go_ref):
        gg = g_ref[...]
        if emit_g:
            go_ref[0][...] = gg
        mn = ADAM_B1 * m_ref[...] + (1.0 - ADAM_B1) * gg
        vn = ADAM_B2 * v_ref[...] + (1.0 - ADAM_B2) * (gg * gg)
        d_ref[...] = -ADAM_LR * ((mn / bc1) / (jnp.sqrt(vn / bc2) + ADAM_EPS) + ADAM_WD * w_ref[...])
        mo_ref[...] = mn
        vo_ref[...] = vn

    ts = 128 if rows % 128 == 0 else rows
    return _rowwise(name, body, [("row", w, c), ("row", g, c), ("row", m, c), ("row", v, c)],
                    [("row", None, c, c, F32)] * (4 if emit_g else 3), rows, ts)


_ANY = pl.BlockSpec(memory_space=pl.ANY)


def _place():
    x, y, c = lax.axis_index("x"), lax.axis_index("y"), lax.axis_index("c")
    chips = [(1 - x, y), (x, 1 - y), (1 - x, 1 - y)]
    return x, y, c, chips


def _rcopy(src, dst, ssem, rsem, dev):
    return pltpu.make_async_remote_copy(src_ref=src, dst_ref=dst, send_sem=ssem, recv_sem=rsem, device_id=dev,
                                        device_id_type=MESH)


def _place_shard(shards, layer, q_idx, dtype, name):
    _, r, cc = shards.shape
    tr = _tile(r, 256, 16)

    def body(q_ref, s_ref, o_ref):
        o_ref[0] = s_ref[0].astype(dtype)

    grid_spec = pltpu.PrefetchScalarGridSpec(
        num_scalar_prefetch=1, grid=(r // tr,),
        in_specs=[pl.BlockSpec((1, tr, cc), lambda i, q_ref: (layer, i, 0))],
        out_specs=pl.BlockSpec((1, tr, cc), lambda i, q_ref: (q_ref[0], i, 0)))
    return pl.pallas_call(body, name=name, grid_spec=grid_spec, out_shape=jax.ShapeDtypeStruct((N_CHIPS, r, cc), dtype),
                          compiler_params=_params("parallel"))(q_idx, shards)


_HBM = pl.BlockSpec(memory_space=pltpu.HBM)
_SEM = pl.BlockSpec(memory_space=pltpu.SEMAPHORE)
_EFFECT = pltpu.SideEffectType.DATAFLOW_SIDE_EFFECTING


def _in_hbm(arrs):
    return [pltpu.with_memory_space_constraint(a, pltpu.HBM) for a in arrs]


def _gather_start(bufs, name):
    n = len(bufs)
    half = [e.shape[1] // 2 for e in bufs]

    def body(*refs):
        ins, send, recv, token = refs[:n], refs[n], refs[n + 1], refs[2 * n + 2]
        x, y, c, chips = _place()
        q = 2 * x + y
        for e in range(n):
            blk = ins[e].at[q, pl.ds(c * half[e], half[e])]
            for j, (cx, cy) in enumerate(chips):
                _rcopy(blk, blk, send.at[3 * e + j], recv.at[3 * e + j], (cx, cy, c)).start()
        token[...] = jnp.zeros_like(token)

    out = pl.pallas_call(
        body, name=name,
        out_shape=(pltpu.SemaphoreType.DMA((3 * n,)), pltpu.SemaphoreType.DMA((3 * n,)),
                   *[pltpu.HBM(b.shape, b.dtype) for b in bufs], jax.ShapeDtypeStruct((SUBLANE, LANE), F32)),
        in_specs=[_HBM] * n, out_specs=(_SEM, _SEM, *[_HBM] * n, pl.BlockSpec(memory_space=pltpu.VMEM)),
        input_output_aliases={e: 2 + e for e in range(n)},
        compiler_params=pltpu.CompilerParams(has_side_effects=_EFFECT))(*_in_hbm(bufs))
    return out[0], out[1], list(out[2:2 + n]), out[2 + n]


def _gather_wait(send, recv, bufs, after, name):
    n = len(bufs)
    half = [e.shape[1] // 2 for e in bufs]

    def body(*refs):
        ins, send_ref, recv_ref = refs[:n], refs[n], refs[n + 1]
        x, y, c, chips = _place()
        q = 2 * x + y
        for e in range(n):
            rows = pl.ds(c * half[e], half[e])
            for j, (cx, cy) in enumerate(chips):
                cp = _rcopy(ins[e].at[q, rows], ins[e].at[2 * cx + cy, rows], send_ref.at[3 * e + j],
                            recv_ref.at[3 * e + j], (cx, cy, c))
                cp.wait_send()
                cp.wait_recv()

    return list(pl.pallas_call(
        body, name=name, out_shape=tuple(pltpu.HBM(b.shape, b.dtype) for b in bufs),
        in_specs=[_HBM] * n + [_SEM, _SEM, _ANY], out_specs=[_HBM] * n,
        input_output_aliases={e: e for e in range(n)},
        compiler_params=pltpu.CompilerParams(has_side_effects=_EFFECT))(*bufs, send, recv, after))


def _forward_sibling(bufs, name):
    n = len(bufs)
    half = [e.shape[1] // 2 for e in bufs]

    def body(*refs):
        outs = refs[n:2 * n]
        send, recv = refs[2 * n:]
        x, y, c, chips = _place()
        sib = (x, y, 1 - c)
        cps = []
        for e in range(n):
            for j, (cx, cy) in enumerate(chips):
                blk = outs[e].at[2 * cx + cy, pl.ds(c * half[e], half[e])]
                cps.append(_rcopy(blk, blk, send.at[3 * e + j], recv.at[3 * e + j], sib))
        for cp in cps:
            cp.start()
        for e in range(n):
            for j, (cx, cy) in enumerate(chips):
                blk = outs[e].at[2 * cx + cy, pl.ds((1 - c) * half[e], half[e])]
                _rcopy(blk, blk, send.at[3 * e + j], recv.at[3 * e + j], sib).wait_recv()
        for cp in cps:
            cp.wait_send()

    return list(pl.pallas_call(
        body, name=name, in_specs=[_ANY] * n, out_specs=[_ANY] * n,
        out_shape=[jax.ShapeDtypeStruct(e.shape, e.dtype) for e in bufs],
        input_output_aliases={e: e for e in range(n)},
        scratch_shapes=[pltpu.SemaphoreType.DMA((3 * n,))] * 2,
    )(*bufs))


def _scatter_start(entries, name):
    n = len(entries)
    lands = [lax.empty(e.shape, e.dtype) for e in entries]

    def body(*refs):
        ins, land, send, recv, token = refs[:n], refs[n:2 * n], refs[2 * n], refs[2 * n + 1], refs[4 * n + 2]
        x, y, c, chips = _place()
        q = 2 * x + y
        for e in range(n):
            for j, (cx, cy) in enumerate(chips):
                _rcopy(ins[e].at[2 * cx + cy], land[e].at[q], send.at[3 * e + j], recv.at[3 * e + j],
                       (cx, cy, c)).start()
        token[...] = jnp.zeros_like(token)

    out = pl.pallas_call(
        body, name=name,
        out_shape=(pltpu.SemaphoreType.DMA((3 * n,)), pltpu.SemaphoreType.DMA((3 * n,)),
                   *[pltpu.HBM(b.shape, b.dtype) for b in entries + lands],
                   jax.ShapeDtypeStruct((SUBLANE, LANE), F32)),
        in_specs=[_HBM] * (2 * n),
        out_specs=(_SEM, _SEM, *[_HBM] * (2 * n), pl.BlockSpec(memory_space=pltpu.VMEM)),
        input_output_aliases={e: 2 + e for e in range(2 * n)},
        compiler_params=pltpu.CompilerParams(has_side_effects=_EFFECT))(*_in_hbm(entries + lands))
    return out[0], out[1], list(out[2:2 + n]), list(out[2 + n:2 + 2 * n]), out[2 + 2 * n]


def _scatter_wait(send, recv, entries, lands, after, name):
    n = len(entries)

    def body(*refs):
        ins, land, send_ref, recv_ref = refs[:n], refs[n:2 * n], refs[2 * n], refs[2 * n + 1]
        x, y, c, chips = _place()
        for e in range(n):
            for j, (cx, cy) in enumerate(chips):
                k = 2 * cx + cy
                cp = _rcopy(ins[e].at[k], land[e].at[k], send_ref.at[3 * e + j], recv_ref.at[3 * e + j],
                            (cx, cy, c))
                cp.wait_send()
                cp.wait_recv()

    out = pl.pallas_call(
        body, name=name, out_shape=tuple(pltpu.HBM(b.shape, b.dtype) for b in entries + lands),
        in_specs=[_HBM] * (2 * n) + [_SEM, _SEM, _ANY], out_specs=[_HBM] * (2 * n),
        input_output_aliases={e: e for e in range(2 * n)},
        compiler_params=pltpu.CompilerParams(has_side_effects=_EFFECT))(*entries, *lands, send, recv, after)
    return list(out[:n]), list(out[n:])


def _swap_start(entries, name):
    n = len(entries)
    half = [e.shape[1] // 2 for e in entries]
    lands = [lax.empty((N_CHIPS, h, e.shape[2]), e.dtype) for e, h in zip(entries, half)]

    def body(*refs):
        ins, land, send, recv, token = refs[:n], refs[n:2 * n], refs[2 * n], refs[2 * n + 1], refs[4 * n + 2]
        x, y, c, _ = _place()
        for e in range(n):
            _rcopy(ins[e].at[:, pl.ds((1 - c) * half[e], half[e]), :], land[e], send.at[e], recv.at[e],
                   (x, y, 1 - c)).start()
        token[...] = jnp.zeros_like(token)

    out = pl.pallas_call(
        body, name=name,
        out_shape=(pltpu.SemaphoreType.DMA((n,)), pltpu.SemaphoreType.DMA((n,)),
                   *[pltpu.HBM(b.shape, b.dtype) for b in entries + lands],
                   jax.ShapeDtypeStruct((SUBLANE, LANE), F32)),
        in_specs=[_HBM] * (2 * n),
        out_specs=(_SEM, _SEM, *[_HBM] * (2 * n), pl.BlockSpec(memory_space=pltpu.VMEM)),
        input_output_aliases={e: 2 + e for e in range(2 * n)},
        compiler_params=pltpu.CompilerParams(has_side_effects=_EFFECT))(*_in_hbm(entries + lands))
    return out[0], out[1], list(out[2:2 + n]), list(out[2 + n:2 + 2 * n]), out[2 + 2 * n]


def _swap_wait(send, recv, entries, lands, after, name):
    n = len(entries)
    half = [e.shape[1] // 2 for e in entries]

    def body(*refs):
        ins, land, send_ref, recv_ref = refs[:n], refs[n:2 * n], refs[2 * n], refs[2 * n + 1]
        x, y, c, _ = _place()
        for e in range(n):
            cp = _rcopy(ins[e].at[:, pl.ds((1 - c) * half[e], half[e]), :], land[e], send_ref.at[e], recv_ref.at[e],
                        (x, y, 1 - c))
            cp.wait_send()
            cp.wait_recv()

    out = pl.pallas_call(
        body, name=name, out_shape=tuple(pltpu.HBM(b.shape, b.dtype) for b in entries + lands),
        in_specs=[_HBM] * (2 * n) + [_SEM, _SEM, _ANY], out_specs=[_HBM] * (2 * n),
        input_output_aliases={e: e for e in range(2 * n)},
        compiler_params=pltpu.CompilerParams(has_side_effects=_EFFECT))(*entries, *lands, send, recv, after)
    return list(out[:n]), list(out[n:])


def _swap_halves(entries, name):
    n = len(entries)
    half = [e.shape[1] // 2 for e in entries]

    def body(*refs):
        ins, outs = refs[:n], refs[n:2 * n]
        send, recv = refs[2 * n:]
        x, y, c, _ = _place()
        cps = [_rcopy(ins[e].at[:, pl.ds((1 - c) * half[e], half[e]), :], outs[e], send.at[e], recv.at[e],
                      (x, y, 1 - c)) for e in range(n)]
        for cp in cps:
            cp.start()
        for cp in cps:
            cp.wait()

    return pl.pallas_call(
        body, name=name, in_specs=[_ANY] * n, out_specs=[_ANY] * n,
        out_shape=[jax.ShapeDtypeStruct((N_CHIPS, h, e.shape[2]), e.dtype) for e, h in zip(entries, half)],
        scratch_shapes=[pltpu.SemaphoreType.DMA((n,))] * 2,
    )(*entries)


def _join_halves(bufs, name):
    n = len(bufs)
    pairs = [(o, layer) for o in range(n) for layer in range(bufs[o].shape[0])]
    npair = len(pairs)

    def body(*refs):
        outs = refs[n:2 * n]
        send, recv = refs[2 * n:]
        x, y, c, _ = _place()
        cps = []
        for k, (o, layer) in enumerate(pairs):
            r2 = bufs[o].shape[1] // 2
            blk = outs[o].at[layer, pl.ds(c * r2, r2)]
            cps.append(_rcopy(blk, blk, send.at[k], recv.at[k], (x, y, 1 - c)))
        for cp in cps:
            cp.start()
        for k, (o, layer) in enumerate(pairs):
            r2 = bufs[o].shape[1] // 2
            blk = outs[o].at[layer, pl.ds((1 - c) * r2, r2)]
            _rcopy(blk, blk, send.at[k], recv.at[k], (x, y, 1 - c)).wait_recv()
        for cp in cps:
            cp.wait_send()

    return pl.pallas_call(
        body, name=name, in_specs=[_ANY] * n, out_specs=[_ANY] * n,
        out_shape=[jax.ShapeDtypeStruct(b.shape, b.dtype) for b in bufs],
        input_output_aliases={e: e for e in range(n)},
        scratch_shapes=[pltpu.SemaphoreType.DMA((npair,))] * 2,
    )(*bufs)


def _gather_all(v, after, name):
    def body(v_ref, after_ref, o_ref, send, recv, loc):
        x, y, c, _ = _place()
        me = 4 * x + 2 * y + c
        mine = pltpu.make_async_copy(v_ref, o_ref.at[me], loc)
        mine.start()
        peers = []
        for k in range(1, N_DEV):
            px = 1 - x if k & 4 else x
            py = 1 - y if k & 2 else y
            pc = 1 - c if k & 1 else c
            peers.append((px, py, pc))
        cps = [_rcopy(v_ref, o_ref.at[me], send.at[k], recv.at[k], peers[k]) for k in range(N_DEV - 1)]
        for cp in cps:
            cp.start()
        for k, (px, py, pc) in enumerate(peers):
            blk = o_ref.at[4 * px + 2 * py + pc]
            _rcopy(blk, blk, send.at[k], recv.at[k], (px, py, pc)).wait_recv()
        for cp in cps:
            cp.wait_send()
        mine.wait()

    return pl.pallas_call(
        body, name=name, in_specs=[_ANY, _ANY], out_specs=_ANY,
        out_shape=jax.ShapeDtypeStruct((N_DEV,) + v.shape, v.dtype),
        scratch_shapes=[pltpu.SemaphoreType.DMA((N_DEV - 1,))] * 2 + [pltpu.SemaphoreType.DMA],
    )(v, after)


def _add_own_half(gst, rx, c_idx, name):
    _, r, cc = gst.shape
    r2 = r // 2
    tr = _tile(r2, 256, 16)
    g4 = gst.reshape(N_CHIPS, 2, r2, cc)

    def body(c_ref, g_ref, r_ref, o_ref):
        o_ref[...] = (g_ref[0].astype(F32) + r_ref[...].astype(F32)).astype(BF16)

    grid_spec = pltpu.PrefetchScalarGridSpec(
        num_scalar_prefetch=1, grid=(N_CHIPS, r2 // tr),
        in_specs=[pl.BlockSpec((1, 1, tr, cc), lambda k, i, c_ref: (k, c_ref[0], i, 0)),
                  pl.BlockSpec((1, tr, cc), lambda k, i, c_ref: (k, i, 0))],
        out_specs=pl.BlockSpec((1, tr, cc), lambda k, i, c_ref: (k, i, 0)))
    return pl.pallas_call(body, name=name, grid_spec=grid_spec, out_shape=jax.ShapeDtypeStruct((N_CHIPS, r2, cc), BF16),
                          compiler_params=_params("parallel", "parallel"))(c_idx, g4, rx)


def _sum_into(buf, rx, own, layer, qc, out_shape, name):
    _, r2, cc = rx.shape
    tr = _tile(r2, 256, 16)
    nb = r2 // tr

    def body(qc_ref, *refs):
        rx_ref, own_ref, o_ref = refs[-3:]
        q = qc_ref[0]
        acc = None
        for k in range(N_CHIPS):
            v = jnp.where(q == k, own_ref[0], rx_ref[k]).astype(F32)
            acc = v if acc is None else acc + v
        o_ref[0] = acc

    in_specs = [pl.BlockSpec((N_CHIPS, tr, cc), lambda i, qc_ref: (0, i, 0)),
                pl.BlockSpec((1, tr, cc), lambda i, qc_ref: (qc_ref[0], i, 0))]
    args = (rx, own)
    aliases = {}
    if buf is not None:
        in_specs = [_ANY] + in_specs
        args = (buf,) + args
        aliases = {1: 0}
    grid_spec = pltpu.PrefetchScalarGridSpec(
        num_scalar_prefetch=1, grid=(nb,), in_specs=in_specs,
        out_specs=pl.BlockSpec((1, tr, cc), lambda i, qc_ref: (layer, qc_ref[1] * nb + i, 0)))
    return pl.pallas_call(body, name=name, grid_spec=grid_spec, out_shape=jax.ShapeDtypeStruct(out_shape, F32),
                          input_output_aliases=aliases, compiler_params=_params("parallel"))(qc, *args)


def _sum_slots(st, name):
    k, r, cc = st.shape
    tr = _tile(r, 256, 8)

    def body(s_ref, o_ref):
        acc = s_ref[0].astype(F32)
        for j in range(1, k):
            acc = acc + s_ref[j].astype(F32)
        o_ref[...] = acc

    return pl.pallas_call(body, name=name, grid=(r // tr,), in_specs=[pl.BlockSpec((k, tr, cc), lambda i: (0, i, 0))],
                          out_specs=pl.BlockSpec((tr, cc), lambda i: (i, 0)),
                          out_shape=jax.ShapeDtypeStruct((r, cc), F32), compiler_params=_params("parallel"))(st)


def _pack(arrs, rows_mult=2 * SUBLANE):
    flat = jnp.concatenate([a.reshape(-1).astype(F32) for a in arrs])
    quantum = rows_mult * LANE
    padded = -(-flat.shape[0] // quantum) * quantum
    return jnp.pad(flat, (0, padded - flat.shape[0])).reshape(-1, LANE)


def _unpack(buf, shapes):
    flat = buf.reshape(-1)
    out, off = [], 0
    for sh in shapes:
        size = 1
        for d in sh:
            size *= d
        out.append(flat[off:off + size].reshape(sh))
        off += size
    return out


def kernel(x, p, norm_mix_g, norm_ffn_g, norm_ple_g, final_norm_g, a_w_in, a_conv_w, a_conv_b, a_w_gate_r, a_b_gate_r, a_w_gate_i, a_b_gate_i, a_lambda, a_w_out, b_w_in, b_conv_w, b_conv_b, b_dt_bias, b_a_log, b_d_skip, b_norm_g, b_w_out, ffn_w_gate, ffn_w_up, ffn_w_down, ple_w_proj, ple_w_gate, loss_target, m_norm_mix_g, m_norm_ffn_g, m_norm_ple_g, m_final_norm_g, m_a_w_in, m_a_conv_w, m_a_conv_b, m_a_w_gate_r, m_a_b_gate_r, m_a_w_gate_i, m_a_b_gate_i, m_a_lambda, m_a_w_out, m_b_w_in, m_b_conv_w, m_b_conv_b, m_b_dt_bias, m_b_a_log, m_b_d_skip, m_b_norm_g, m_b_w_out, m_ffn_w_gate, m_ffn_w_up, m_ffn_w_down, m_ple_w_proj, m_ple_w_gate, v_norm_mix_g, v_norm_ffn_g, v_norm_ple_g, v_final_norm_g, v_a_w_in, v_a_conv_w, v_a_conv_b, v_a_w_gate_r, v_a_b_gate_r, v_a_w_gate_i, v_a_b_gate_i, v_a_lambda, v_a_w_out, v_b_w_in, v_b_conv_w, v_b_conv_b, v_b_dt_bias, v_b_a_log, v_b_d_skip, v_b_norm_g, v_b_w_out, v_ffn_w_gate, v_ffn_w_up, v_ffn_w_down, v_ple_w_proj, v_ple_w_gate):
    names = ["norm_mix_g", "norm_ffn_g", "norm_ple_g", "final_norm_g", "a_w_in", "a_conv_w", "a_conv_b", "a_w_gate_r",
             "a_b_gate_r", "a_w_gate_i", "a_b_gate_i", "a_lambda", "a_w_out", "b_w_in", "b_conv_w", "b_conv_b",
             "b_dt_bias", "b_a_log", "b_d_skip", "b_norm_g", "b_w_out", "ffn_w_gate", "ffn_w_up", "ffn_w_down",
             "ple_w_proj", "ple_w_gate"]
    w_in = dict(zip(names, [norm_mix_g, norm_ffn_g, norm_ple_g, final_norm_g, a_w_in, a_conv_w, a_conv_b, a_w_gate_r,
                            a_b_gate_r, a_w_gate_i, a_b_gate_i, a_lambda, a_w_out, b_w_in, b_conv_w, b_conv_b,
                            b_dt_bias, b_a_log, b_d_skip, b_norm_g, b_w_out, ffn_w_gate, ffn_w_up, ffn_w_down,
                            ple_w_proj, ple_w_gate]))
    m_in = dict(zip(names, [m_norm_mix_g, m_norm_ffn_g, m_norm_ple_g, m_final_norm_g, m_a_w_in, m_a_conv_w,
                            m_a_conv_b, m_a_w_gate_r, m_a_b_gate_r, m_a_w_gate_i, m_a_b_gate_i, m_a_lambda,
                            m_a_w_out, m_b_w_in, m_b_conv_w, m_b_conv_b, m_b_dt_bias, m_b_a_log, m_b_d_skip,
                            m_b_norm_g, m_b_w_out, m_ffn_w_gate, m_ffn_w_up, m_ffn_w_down, m_ple_w_proj,
                            m_ple_w_gate]))
    v_in = dict(zip(names, [v_norm_mix_g, v_norm_ffn_g, v_norm_ple_g, v_final_norm_g, v_a_w_in, v_a_conv_w,
                            v_a_conv_b, v_a_w_gate_r, v_a_b_gate_r, v_a_w_gate_i, v_a_b_gate_i, v_a_lambda,
                            v_a_w_out, v_b_w_in, v_b_conv_w, v_b_conv_b, v_b_dt_bias, v_b_a_log, v_b_d_skip,
                            v_b_norm_g, v_b_w_out, v_ffn_w_gate, v_ffn_w_up, v_ffn_w_down, v_ple_w_proj,
                            v_ple_w_gate]))

    s, d = x.shape[1], x.shape[2]
    depth = norm_mix_g.shape[0]
    assert depth == 2
    q_idx = 2 * lax.axis_index("x") + lax.axis_index("y")
    c_idx = lax.axis_index("c").astype(jnp.int32).reshape(1)

    inner = b_w_out.shape[1] * N_CHIPS
    n_heads = inner // SSD_HEAD_DIM
    n_groups = n_heads // SSD_HEADS_PER_GROUP
    gn = n_groups * SSD_STATE
    xbcw = inner + 2 * gn
    assert b_conv_w.shape[2] * N_CHIPS == xbcw and n_heads <= LANE

    big = [("a_w_in", "col"), ("a_w_gate_r", "gate"), ("a_w_gate_i", "gate"), ("a_w_out", "row"),
           ("ffn_w_gate", "col"), ("ffn_w_up", "col"), ("ffn_w_down", "row"), ("ple_w_proj", "col"),
           ("ple_w_gate", "row"), ("b_w_in", "col"), ("b_w_out", "row")]
    kind_of = dict(big)

    def shard2d(name, arr):
        if kind_of[name] == "gate":
            return [arr[l].reshape(-1, arr.shape[-1]) for l in range(arr.shape[0])]
        return [arr[l] for l in range(arr.shape[0])]

    small_sharded = ["a_conv_w", "a_b_gate_r", "a_b_gate_i", "b_conv_w", "b_conv_b", "b_norm_g"]
    small_pack = _pack([w_in[nm] for nm in small_sharded], rows_mult=16)

    q_vec = q_idx.astype(jnp.int32).reshape(1)
    qc_vec = jnp.stack([q_idx, lax.axis_index("c")]).astype(jnp.int32)
    gather_groups = [
        [("a_w_in", 0), ("a_w_gate_r", 0), ("a_w_gate_i", 0), ("a_w_out", 0), ("small", 0)],
        [("ffn_w_gate", 0), ("ffn_w_up", 0)],
        [("ffn_w_down", 0), ("ple_w_proj", 0), ("ple_w_gate", 0)],
        [("b_w_in", 0)],
        [("b_w_out", 0)],
        [("ffn_w_gate", 1), ("ffn_w_up", 1), ("ffn_w_down", 1), ("ple_w_proj", 1), ("ple_w_gate", 1)],
    ]
    gather_started = [None] * len(gather_groups)

    def gather_begin(gi):
        bufs = []
        for nm, l in gather_groups[gi]:
            if nm == "small":
                bufs.append(_place_shard(small_pack[None], 0, q_vec, F32, "place_small"))
            else:
                arr = w_in[nm]
                arr = arr.reshape(arr.shape[0], -1, arr.shape[-1]) if kind_of[nm] == "gate" else arr
                bufs.append(_place_shard(arr, l, q_vec, BF16, f"place_{nm}{l}"))
        gather_started[gi] = _gather_start(bufs, f"gather_start{gi}")
        return gather_started[gi][3][0:1, 0:1]

    wst = {}

    def gather_finish(gi, after):
        send, recv, thru, _ = gather_started[gi]
        landed = _gather_wait(send, recv, thru, after, f"gather_wait{gi}")
        for k, arr in zip(gather_groups[gi], _forward_sibling(landed, f"gather_forward{gi}")):
            wst[k] = arr

    gather_finish(0, sum(gather_begin(gi) for gi in range(len(gather_groups))))
    small_st = wst[("small", 0)]
    kept_stacked = ("a_w_in", "ffn_w_gate", "ffn_w_up", "ple_w_proj")

    def whole(nm, l):
        st = wst[(nm, l)]
        kind = kind_of[nm]
        if nm in kept_stacked:
            return st
        if kind == "row":
            return st.reshape(-1, st.shape[-1])
        if kind == "col":
            return jnp.concatenate([st[k] for k in range(N_CHIPS)], axis=1)
        heads = w_in[nm].shape[1]
        return st.reshape(N_CHIPS, heads, -1, st.shape[-1]).transpose(1, 0, 2, 3).reshape(heads, -1, st.shape[-1])

    small_parts = [_unpack(small_st[k], [w_in[nm].shape for nm in small_sharded]) for k in range(N_CHIPS)]
    small_full = {nm: jnp.concatenate([small_parts[k][i] for k in range(N_CHIPS)], axis=-1)
                  for i, nm in enumerate(small_sharded)}
    a_cw = small_full["a_conv_w"][0]
    a_br = small_full["a_b_gate_r"][0].reshape(1, -1)
    a_bi = small_full["a_b_gate_i"][0].reshape(1, -1)
    b_cw = small_full["b_conv_w"][0]
    b_cb = small_full["b_conv_b"]
    b_ng = small_full["b_norm_g"]

    def pad_lanes(v):
        return jnp.pad(v, ((0, 0), (0, LANE - v.shape[1])))

    dt_bias = pad_lanes(b_dt_bias)
    a_log = pad_lanes(b_a_log.reshape(n_groups, SSD_HEADS_PER_GROUP)).reshape(1, n_groups * LANE)
    dskip_e = jnp.repeat(b_d_skip, SSD_HEAD_DIM, axis=1)

    w_a_in = whole("a_w_in", 0)
    w_ax, w_ay = w_a_in[:N_CHIPS // 2], w_a_in[N_CHIPS // 2:]
    w_ar, w_ai, w_ao = whole("a_w_gate_r", 0), whole("a_w_gate_i", 0), whole("a_w_out", 0)
    w_fg, w_fu, w_fd, w_pp, w_pg = ([None] * depth for _ in range(5))

    def take_ffn_in(l):
        w_fg[l], w_fu[l] = whole("ffn_w_gate", l), whole("ffn_w_up", l)

    def take_ffn_out_ple(l):
        w_fd[l], w_pp[l], w_pg[l] = whole("ffn_w_down", l), whole("ple_w_proj", l), whole("ple_w_gate", l)

    grads = {}

    h0 = x[0]
    g_mix = [norm_mix_g[l:l + 1] for l in range(depth)]
    g_ffn = [norm_ffn_g[l:l + 1] for l in range(depth)]
    g_ple = [norm_ple_g[l:l + 1] for l in range(depth)]
    g_fin = final_norm_g.reshape(1, -1)

    u0 = _rmsnorm_fwd(h0, g_mix[0], "norm_mix0")
    xr_pre = _mm(u0, w_ax, stacked_b=True, name="lru_in_x")
    yg = _mm(u0, w_ay, stacked_b=True, name="lru_in_y")
    xr = _conv_fwd(xr_pre, a_cw, a_conv_b, False, "lru_conv")
    lru_a, lru_b = _lru_gates_fwd(xr, w_ar, w_ai, a_br, a_bi, a_lambda)
    hs = _scan(lru_a, lru_b, False, "lru_scan")
    y_lru = _lru_out_fwd(hs, yg)
    h_mix = [_mm(y_lru, w_ao, add=h0, name="lru_out"), None]

    def ffn_ple_fwd(h_in, l, before_down=None):
        n_f = _rmsnorm_fwd(h_in, g_ffn[l], f"norm_ffn{l}")
        gt = _mm(n_f, w_fg[l], stacked_b=True, out_dtype=BF16, name=f"ffn_gate{l}")
        up = _mm(n_f, w_fu[l], stacked_b=True, out_dtype=BF16, name=f"ffn_up{l}")
        act = _swiglu_act(gt, up)
        if before_down is not None:
            before_down(act)
        h_f = _mm(act, w_fd[l], add=h_in, name=f"ffn_down{l}")
        n_p = _rmsnorm_fwd(h_f, g_ple[l], f"norm_ple{l}")
        gp = _mm(n_p, w_pg[l], name=f"ple_gate{l}")
        pp = _mm(p[l, 0], w_pp[l], stacked_b=True, name=f"ple_proj{l}")
        h_out = _ple_fwd(h_f, gp, pp)
        return h_out, dict(h_in=h_in, n_f=n_f, gt=gt, up=up, act=act, h_f=h_f, n_p=n_p, gp=gp, pp=pp)

    gather_finish(1, h_mix[0])
    take_ffn_in(0)

    def finish_ffn_out_ple0(act):
        gather_finish(2, act)
        take_ffn_out_ple(0)

    h_l0, sv0 = ffn_ple_fwd(h_mix[0], 0, finish_ffn_out_ple0)

    gather_finish(3, h_l0)
    w_b_in = whole("b_w_in", 0)
    w_bz, w_bx = w_b_in[:, :inner], w_b_in[:, inner:inner + xbcw]
    w_bd = pad_lanes(w_b_in[:, inner + xbcw:])
    u1 = _rmsnorm_fwd(h_l0, g_mix[1], "norm_mix1")
    z = _mm(u1, w_bz, name="ssd_in_z")
    xbc_pre = _mm(u1, w_bx, name="ssd_in_xbc")
    dt_pre = _mm(u1, w_bd, name="ssd_in_dt")
    xbc = _conv_fwd(xbc_pre, b_cw, b_cb, True, "ssd_conv")
    dt = _dt_fwd(dt_pre, dt_bias, n_heads, n_groups)
    ysc, s_in = _ssd_fwd(xbc, dt, a_log, inner, n_groups)
    yn = _ssd_gate_norm_fwd(ysc, xbc, z, dskip_e, b_ng, n_groups)
    gather_finish(4, yn)
    w_bo = whole("b_w_out", 0)
    h_mix[1] = _mm(yn, w_bo, add=h_l0, name="ssd_out")
    gather_finish(5, h_mix[1])
    take_ffn_in(1)
    take_ffn_out_ple(1)
    h_l1, sv1 = ffn_ple_fwd(h_mix[1], 1)

    dh, dg_fin, loss_row = _final_loss_bwd(h_l1, g_fin, loss_target[0])

    d_norm_ffn, d_norm_ple, d_norm_mix = [None] * depth, [None] * depth, [None] * depth
    for nm in ("ffn_w_gate", "ffn_w_up", "ffn_w_down", "ple_w_proj", "ple_w_gate"):
        grads[nm] = [None] * depth

    def stacked(nm, gfull):
        kind = kind_of[nm]
        if nm in kept_stacked:
            return gfull
        if kind == "row":
            return gfull.reshape(N_CHIPS, -1, gfull.shape[-1])
        if kind == "col":
            n_loc = gfull.shape[1] // N_CHIPS
            return jnp.stack([gfull[:, k * n_loc:(k + 1) * n_loc] for k in range(N_CHIPS)])
        heads, bw, _ = gfull.shape
        return gfull.reshape(heads, N_CHIPS, bw // N_CHIPS, bw).transpose(1, 0, 2, 3).reshape(N_CHIPS, -1, bw)

    reduce_started = []

    def reduce_start(keys, tag):
        gst = [stacked(nm, grads[nm][l]) for nm, l in keys]
        from_sib = _swap_halves(gst, f"reduce_swap_{tag}")
        return reduce_exchange(keys, tag, gst, from_sib)

    def reduce_exchange(keys, tag, gst, from_sib):
        chip_sum = [_add_own_half(g, r, c_idx, f"reduce_add_{nm}{l}") for g, r, (nm, l) in zip(gst, from_sib, keys)]
        send, recv, ents, lands, token = _scatter_start(chip_sum, f"reduce_scatter_start_{tag}")
        reduce_started.append((keys, tag, send, recv, ents, lands))
        return token[0:1, 0:1]

    def reduce_swap_begin(keys, tag):
        gst = [stacked(nm, grads[nm][l]) for nm, l in keys]
        send, recv, ents, lands, token = _swap_start(gst, f"reduce_swap_start_{tag}")
        return (keys, tag, send, recv, ents, lands), token[0:1, 0:1]

    def reduce_swap_end(state, after):
        keys, tag, send, recv, ents, lands = state
        gst, from_sib = _swap_wait(send, recv, ents, lands, after, f"reduce_swap_wait_{tag}")
        return reduce_exchange(keys, tag, gst, from_sib)

    def ffn_ple_keys(l):
        return [("ple_w_gate", l), ("ple_w_proj", l), ("ffn_w_down", l), ("ffn_w_gate", l), ("ffn_w_up", l)]

    def ffn_ple_bwd(dh_out, sv, l, g_ple_l, after_dact=None):
        dgp, dpp = _ple_bwd(dh_out, sv["gp"], sv["pp"])
        grads["ple_w_gate"][l] = _mm(sv["n_p"], dgp, ta=True, out_dtype=BF16, name=f"ple_gate_dw{l}")
        grads["ple_w_proj"][l] = _mm(p[l, 0], dpp, ta=True, out_dtype=BF16, stacked_out=N_CHIPS,
                                     name=f"ple_proj_dw{l}")
        dn = _mm(dgp, w_pg[l], tb=True, name=f"ple_gate_dx{l}")
        dh_f, d_norm_ple[l] = _rmsnorm_bwd(dn, sv["h_f"], g_ple_l, dh_out, f"norm_ple_bwd{l}")
        grads["ffn_w_down"][l] = _mm(sv["act"], dh_f, ta=True, out_dtype=BF16, name=f"ffn_down_dw{l}")
        dact = _mm(dh_f, w_fd[l], tb=True, name=f"ffn_down_dx{l}")
        g_ffn_l = g_ffn[l] if after_dact is None else g_ffn[l] + after_dact(dact)
        dgt, dup = _swiglu_bwd(dact, sv["gt"], sv["up"])
        grads["ffn_w_gate"][l] = _mm(sv["n_f"], dgt, ta=True, out_dtype=BF16, stacked_out=N_CHIPS,
                                     name=f"ffn_gate_dw{l}")
        grads["ffn_w_up"][l] = _mm(sv["n_f"], dup, ta=True, out_dtype=BF16, stacked_out=N_CHIPS, name=f"ffn_up_dw{l}")
        dn = _mm(dgt, w_fg[l], tb=True, stacked_b=True, name=f"ffn_gate_dx{l}")
        dn = _mm(dup, w_fu[l], tb=True, stacked_b=True, add=dn, name=f"ffn_up_dx{l}")
        dh_in, d_norm_ffn[l] = _rmsnorm_bwd(dn, sv["h_in"], g_ffn_l, dh_f, f"norm_ffn_bwd{l}")
        return dh_in

    dh = ffn_ple_bwd(dh, sv1, 1, g_ple[1])
    swapping, tok = reduce_swap_begin(ffn_ple_keys(1), "l1")

    grads["b_w_out"] = [_mm(yn, dh, ta=True, out_dtype=BF16, name="ssd_out_dw")]
    dyn = _mm(dh, w_bo, tb=True, name="ssd_out_dx")
    dy_ssd, dz, d_b_norm_g, dd_lane = _ssd_gate_norm_bwd(dyn, ysc, xbc, z, dskip_e, b_ng + tok, n_groups)
    tok = reduce_swap_end(swapping, dy_ssd)
    dxs, d_bm, d_cm, ddt, d_a_log = _ssd_bwd(xbc, dt, a_log, dy_ssd, ysc, s_in, dskip_e + tok, inner, n_groups)
    dxbc = _put_cols(dxs, d_bm, inner // gn, "ssd_put_db")
    dxbc = _put_cols(dxbc, d_cm, inner // gn + 1, "ssd_put_dc")
    dconv = _silu_conv_bwd_pre(dxbc, xbc_pre, b_cw, b_cb, "ssd_conv_bwd_pre")
    dxbc_pre, d_b_conv_w, d_b_conv_b = _conv_bwd(dconv, xbc_pre, b_cw, "ssd_conv_bwd")
    ddt_pre, d_dt_bias = _dt_bwd(ddt, dt_pre, dt_bias, n_heads, n_groups)
    d_a_log = d_a_log.reshape(n_groups, LANE)[:, :SSD_HEADS_PER_GROUP].reshape(1, n_heads)
    gw_bz = _mm(u1, dz, ta=True, out_dtype=BF16, name="ssd_in_z_dw")
    gw_bx = _mm(u1, dxbc_pre, ta=True, out_dtype=BF16, name="ssd_in_xbc_dw")
    gw_bd = _mm(u1, ddt_pre, ta=True, out_dtype=BF16, name="ssd_in_dt_dw")
    grads["b_w_in"] = [jnp.concatenate([gw_bz, gw_bx, gw_bd[:, :n_heads]], axis=1)]
    du = _mm(dz, w_bz, tb=True, name="ssd_in_z_dx")
    du = _mm(dxbc_pre, w_bx, tb=True, add=du, name="ssd_in_xbc_dx")
    du = _mm(ddt_pre, w_bd, tb=True, add=du, name="ssd_in_dt_dx")
    dh, d_norm_mix[1] = _rmsnorm_bwd(du, h_l0, g_mix[1], dh, "norm_mix_bwd1")
    swapping, tok = reduce_swap_begin([("b_w_out", 0), ("b_w_in", 0)], "ssd")

    dh = ffn_ple_bwd(dh, sv0, 0, g_ple[0] + tok, lambda dact: reduce_swap_end(swapping, dact))
    tok = reduce_start(ffn_ple_keys(0), "l0")

    grads["a_w_out"] = [_mm(y_lru, dh, ta=True, out_dtype=BF16, name="lru_out_dw")]
    dy_lru = _mm(dh, w_ao, tb=True, name="lru_out_dx")
    dhs, dyg = _lru_out_bwd(dy_lru, hs, yg)
    g_scan = _scan(lru_a, dhs, True, "lru_scan_bwd")
    dxr, d_wr, d_wi, d_br, d_bi, d_lam = _lru_gates_bwd(xr, g_scan, hs, w_ar, w_ai, a_br, a_bi, a_lambda + tok)
    dxr_pre, d_a_conv_w, d_a_conv_b = _conv_bwd(dxr, xr_pre, a_cw, "lru_conv_bwd")
    gw_ax = _mm(u0, dxr_pre, ta=True, out_dtype=BF16, stacked_out=N_CHIPS // 2, name="lru_in_x_dw")
    gw_ay = _mm(u0, dyg, ta=True, out_dtype=BF16, stacked_out=N_CHIPS // 2, name="lru_in_y_dw")
    grads["a_w_in"] = [jnp.concatenate([gw_ax, gw_ay], axis=0)]
    grads["a_w_gate_r"] = [d_wr.astype(BF16)]
    grads["a_w_gate_i"] = [d_wi.astype(BF16)]
    du = _mm(dxr_pre, w_ax, tb=True, stacked_b=True, name="lru_in_x_dx")
    du = _mm(dyg, w_ay, tb=True, stacked_b=True, add=du, name="lru_in_y_dx")
    grad_x, d_norm_mix[0] = _rmsnorm_bwd(du, h0, g_mix[0], dh, "norm_mix_bwd0")

    tok = reduce_start([("a_w_out", 0), ("a_w_in", 0), ("a_w_gate_r", 0), ("a_w_gate_i", 0)], "lru")
    grad_out, delta_out, m_out, v_out = {}, {}, {}, {}

    def reduce_finish(groups, after, tag):
        g_half = {}
        for keys, gtag, send, recv, ents, lands in groups:
            ents, lands = _scatter_wait(send, recv, ents, lands, after, f"reduce_scatter_wait_{gtag}")
            for rx, own, (nm, l) in zip(lands, ents, keys):
                sh = shard2d(nm, w_in[nm])
                g_half[nm] = _sum_into(g_half.get(nm), rx, own, l, qc_vec, (len(sh),) + sh[0].shape,
                                       f"reduce_sum_{nm}{l}")
        nms = list(g_half)
        seen = jnp.zeros((1, 1), F32)
        for nm, gfull in zip(nms, _join_halves([g_half[nm] for nm in nms], f"reduce_join_{tag}")):
            shape, cols = w_in[nm].shape, gfull.shape[-1]
            dl, mn, vn, gout = _adamw(w_in[nm].reshape(-1, cols), gfull.reshape(-1, cols),
                                      m_in[nm].reshape(-1, cols), v_in[nm].reshape(-1, cols), f"adamw_{nm}",
                                      emit_g=True)
            grad_out[nm], delta_out[nm] = gout.reshape(shape), dl.reshape(shape)
            m_out[nm], v_out[nm] = mn.reshape(shape), vn.reshape(shape)
            seen = seen + dl[0:1, 0:1]
        return seen

    by_tag = {grp[1]: grp for grp in reduce_started}
    done = reduce_finish([by_tag["ssd"]], grad_x[0:1, 0:1] + tok, "ssd")
    done = reduce_finish([by_tag["l1"], by_tag["l0"]], done, "ffn_ple")
    done = reduce_finish([by_tag["lru"]], done, "lru")

    small_full_grads = {
        "norm_mix_g": jnp.concatenate(d_norm_mix, axis=0), "norm_ffn_g": jnp.concatenate(d_norm_ffn, axis=0),
        "norm_ple_g": jnp.concatenate(d_norm_ple, axis=0), "final_norm_g": dg_fin[0],
        "a_conv_w": d_a_conv_w[None], "a_conv_b": d_a_conv_b,
        "a_b_gate_r": d_br.reshape(a_b_gate_r.shape[0], a_b_gate_r.shape[1], -1),
        "a_b_gate_i": d_bi.reshape(a_b_gate_i.shape[0], a_b_gate_i.shape[1], -1),
        "a_lambda": d_lam, "b_conv_w": d_b_conv_w[None], "b_conv_b": d_b_conv_b,
        "b_dt_bias": d_dt_bias[:, :n_heads], "b_a_log": d_a_log,
        "b_d_skip": dd_lane.reshape(1, n_heads, SSD_HEAD_DIM).sum(axis=-1), "b_norm_g": d_b_norm_g,
    }
    small_names = list(small_full_grads)
    small_shapes = [small_full_grads[nm].shape for nm in small_names]
    packed = _pack([loss_row] + [small_full_grads[nm] for nm in small_names])
    total = _sum_slots(_gather_all(packed, done, "gather_small_grads"), "sum_small_grads")
    parts = _unpack(total, [(1, LANE)] + small_shapes)
    loss = parts[0][0, 0]
    g_small = {}
    for nm, gfull in zip(small_names, parts[1:]):
        if nm in small_sharded:
            n_loc = w_in[nm].shape[-1]
            gfull = lax.dynamic_slice_in_dim(gfull, q_idx * n_loc, n_loc, axis=gfull.ndim - 1)
        g_small[nm] = gfull

    sm_shapes = [w_in[nm].shape for nm in small_names]
    dl, mn, vn = _adamw(_pack([w_in[nm] for nm in small_names]), _pack([g_small[nm] for nm in small_names]),
                        _pack([m_in[nm] for nm in small_names]), _pack([v_in[nm] for nm in small_names]),
                        "adamw_small")
    for nm, a, b_, c_ in zip(small_names, _unpack(dl, sm_shapes), _unpack(mn, sm_shapes), _unpack(vn, sm_shapes)):
        grad_out[nm] = g_small[nm].reshape(w_in[nm].shape)
        delta_out[nm], m_out[nm], v_out[nm] = a, b_, c_

    return (loss, grad_x[None], *[grad_out[nm] for nm in names], *[delta_out[nm] for nm in names],
            *[m_out[nm] for nm in names], *[v_out[nm] for nm in names])
```

```python
import functools

import jax
import jax.numpy as jnp
from jax import lax
from jax.experimental import pallas as pl
from jax.experimental.pallas import tpu as pltpu

F32 = jnp.float32
BF16 = jnp.bfloat16
MESH = pl.DeviceIdType.MESH
HIGHEST = lax.Precision.HIGHEST

NORM_EPS = 1e-6
LRU_C = 8.0
CONV_WIDTH = 4
SSD_HEAD_DIM = 64
SSD_STATE = 128
SSD_CHUNK = 128
SSD_HEADS_PER_GROUP = 8
LANE = 128
SUBLANE = 8
N_CHIPS = 4
N_DEV = 8
VMEM_LIMIT = 48 * 1024 * 1024

ADAM_LR = 0.001
ADAM_B1 = 0.9
ADAM_B2 = 0.999
ADAM_EPS = 1e-08
ADAM_WD = 0.01
ADAM_STEP = 10


def _tile(n, cap, mult=LANE):
    best = None
    for t in range(mult, min(n, cap) + 1, mult):
        if n % t == 0:
            best = t
    return best if best is not None else n


def _params(*sem):
    return pltpu.CompilerParams(dimension_semantics=sem, vmem_limit_bytes=VMEM_LIMIT)


def _mm(a, b, *, ta=False, tb=False, add=None, out_dtype=F32, stacked_b=False, stacked_out=0, name):
    if ta:
        kd, m = a.shape
    else:
        m, kd = a.shape
    n_loc = None
    if stacked_b:
        slots, kb, n_loc = b.shape
        if tb:
            n, kb = kb, slots * n_loc
        else:
            n = slots * n_loc
    elif tb:
        n, kb = b.shape
    else:
        kb, n = b.shape
    assert kd == kb, (a.shape, b.shape, ta, tb)
    tm = _tile(m, 1408)
    tn = _tile(n, 1408)
    tk = _tile(kd, 2048)
    if stacked_b and tb:
        tk = _tile(n_loc, 1408)
    elif stacked_b:
        tn = _tile(n_loc, 1408)
    if stacked_out:
        n_loc = n // stacked_out
        tn = _tile(n_loc, 1408)
    nk = kd // tk
    dims = (((0 if ta else 1,), (1 if tb else 0,)), ((), ()))

    def body(*refs):
        a_ref, b_ref = refs[:2]
        add_ref = refs[2] if add is not None else None
        o_ref = refs[3] if add is not None else refs[2]
        bv = b_ref[0] if stacked_b else b_ref[...]
        part = lax.dot_general(a_ref[...].astype(BF16), bv.astype(BF16), dims, preferred_element_type=F32)

        def finish(r):
            if add is not None:
                r = r + add_ref[...]
            if stacked_out:
                o_ref[0] = r.astype(out_dtype)
            else:
                o_ref[...] = r.astype(out_dtype)

        if nk == 1:
            finish(part)
        else:
            acc_ref = refs[-1]
            k = pl.program_id(2)

            @pl.when(k == 0)
            def _():
                acc_ref[...] = part

            @pl.when((k > 0) & (k < nk - 1))
            def _():
                acc_ref[...] += part

            @pl.when(k == nk - 1)
            def _():
                finish(acc_ref[...] + part)

    a_spec = pl.BlockSpec((tk, tm), lambda i, j, k: (k, i)) if ta else pl.BlockSpec((tm, tk), lambda i, j, k: (i, k))
    if stacked_b and tb:
        per = n_loc // tk
        b_spec = pl.BlockSpec((1, tn, tk), lambda i, j, k: (k // per, j, k % per))
    elif stacked_b:
        per = n_loc // tn
        b_spec = pl.BlockSpec((1, tk, tn), lambda i, j, k: (j // per, k, j % per))
    elif tb:
        b_spec = pl.BlockSpec((tn, tk), lambda i, j, k: (j, k))
    else:
        b_spec = pl.BlockSpec((tk, tn), lambda i, j, k: (k, j))
    o_spec = pl.BlockSpec((tm, tn), lambda i, j, k: (i, j))
    in_specs = [a_spec, b_spec] + ([o_spec] if add is not None else [])
    args = (a, b) + ((add,) if add is not None else ())
    if stacked_out:
        per_o = n_loc // tn
        out_spec = pl.BlockSpec((1, tm, tn), lambda i, j, k: (j // per_o, i, j % per_o))
        out_shape = jax.ShapeDtypeStruct((stacked_out, m, n_loc), out_dtype)
    else:
        out_spec, out_shape = o_spec, jax.ShapeDtypeStruct((m, n), out_dtype)
    return pl.pallas_call(
        body, name=name, grid=(m // tm, n // tn, nk), in_specs=in_specs, out_specs=out_spec, out_shape=out_shape,
        scratch_shapes=[pltpu.VMEM((tm, tn), F32)] if nk > 1 else [],
        compiler_params=_params("parallel", "parallel", "arbitrary"))(*args)


def _rowwise(name, body, ins, outs, nrows, ts, ncol=1):
    ts = min(ts, nrows)
    nrow = nrows // ts
    hb = ts // SUBLANE
    nb8 = nrows // SUBLANE
    in_specs, args = [], []
    for kind, arr, cb in ins:
        if kind == "row":
            spec = pl.BlockSpec((ts, cb), lambda j, i: (i, j))
        elif kind == "prev":
            spec = pl.BlockSpec((SUBLANE, cb), lambda j, i: (jnp.maximum(i * hb - 1, 0), j))
        elif kind == "next":
            spec = pl.BlockSpec((SUBLANE, cb), lambda j, i: (jnp.minimum((i + 1) * hb, nb8 - 1), j))
        else:
            spec = pl.BlockSpec((arr.shape[0], cb), lambda j, i: (0, j))
        in_specs.append(spec)
        args.append(arr)
    out_specs, out_shape = [], []
    for kind, rows, ctot, cb, dt in outs:
        if kind == "row":
            out_shape.append(jax.ShapeDtypeStruct((nrows, ctot), dt))
            out_specs.append(pl.BlockSpec((ts, cb), lambda j, i: (i, j)))
        else:
            out_shape.append(jax.ShapeDtypeStruct((rows, ctot), dt))
            out_specs.append(pl.BlockSpec((rows, cb), lambda j, i: (0, j)))

    def kern(*refs):
        body(pl.program_id(1), nrow, *refs)

    return pl.pallas_call(kern, name=name, grid=(ncol, nrow), in_specs=in_specs, out_specs=out_specs,
                          out_shape=out_shape, compiler_params=_params("parallel", "arbitrary"))(*args)


def _colsum(x):
    return jnp.sum(x, axis=0, keepdims=True)


def _acc(i, ref, val):
    @pl.when(i == 0)
    def _():
        ref[...] = val

    @pl.when(i > 0)
    def _():
        ref[...] += val


def _shift_down(x, halo, k):
    xx = jnp.concatenate([halo, x], axis=0)
    return pltpu.roll(xx, k, axis=0)[SUBLANE:, :]


def _shift_up(x, halo, k):
    xx = jnp.concatenate([x, halo], axis=0)
    n = xx.shape[0]
    return pltpu.roll(xx, n - k, axis=0)[: x.shape[0], :]


def _sigmoid(x):
    return 1.0 / (1.0 + jnp.exp(-x))


def _silu(x):
    return x * _sigmoid(x)


def _dsilu(x):
    s = _sigmoid(x)
    return s * (1.0 + x * (1.0 - s))


_GELU_K = 0.7978845608028654
_GELU_C = 0.044715


def _gelu(x):
    return 0.5 * x * (1.0 + jnp.tanh(_GELU_K * (x + _GELU_C * x * x * x)))


def _dgelu(x):
    t = jnp.tanh(_GELU_K * (x + _GELU_C * x * x * x))
    return 0.5 * (1.0 + t) + 0.5 * x * (1.0 - t * t) * _GELU_K * (1.0 + 3.0 * _GELU_C * x * x)


def _softplus(x):
    return jnp.maximum(x, 0.0) + jnp.log1p(jnp.exp(-jnp.abs(x)))


def _neg_expm1(x):
    poly = -x * (1.0 + x * (0.5 + x * (1.0 / 6.0 + x * (1.0 / 24.0 + x * (1.0 / 120.0)))))
    return jnp.where(x > -0.05, poly, 1.0 - jnp.exp(x))


def _rmsnorm_fwd(h, g, name):
    s, d = h.shape

    def body(i, n, h_ref, g_ref, o_ref):
        x = h_ref[...]
        r = lax.rsqrt(jnp.mean(x * x, axis=-1, keepdims=True) + NORM_EPS)
        o_ref[...] = (x * r * g_ref[...]).astype(BF16)

    return _rowwise(name, body, [("row", h, d), ("vec", g, d)], [("row", None, d, d, BF16)], s, 256)[0]


def _rmsnorm_bwd(dn, h, g, dres, name):
    s, d = h.shape

    def body(i, n, dn_ref, h_ref, g_ref, dres_ref, dh_ref, dg_ref):
        x = h_ref[...]
        dy = dn_ref[...].astype(F32)
        r = lax.rsqrt(jnp.mean(x * x, axis=-1, keepdims=True) + NORM_EPS)
        xh = x * r
        _acc(i, dg_ref, _colsum(dy * xh))
        dxh = dy * g_ref[...]
        dh_ref[...] = dres_ref[...] + r * (dxh - xh * jnp.mean(dxh * xh, axis=-1, keepdims=True))

    return _rowwise(name, body, [("row", dn, d), ("row", h, d), ("vec", g, d), ("row", dres, d)],
                    [("row", None, d, d, F32), ("acc", 1, d, d, F32)], s, 256)


def _final_loss_bwd(h, g, tgt):
    s, d = h.shape

    def body(i, n, h_ref, g_ref, t_ref, dh_ref, dg_ref, loss_ref):
        x = h_ref[...]
        gg = g_ref[...]
        r = lax.rsqrt(jnp.mean(x * x, axis=-1, keepdims=True) + NORM_EPS)
        xh = x * r
        err = xh * gg - t_ref[...]
        part = 0.5 * jnp.sum(jnp.mean(err * err, axis=-1, keepdims=True), axis=0, keepdims=True)
        _acc(i, loss_ref, jnp.broadcast_to(part, (1, LANE)))
        dy = err * (1.0 / d)
        _acc(i, dg_ref, _colsum(dy * xh))
        dxh = dy * gg
        dh_ref[...] = r * (dxh - xh * jnp.mean(dxh * xh, axis=-1, keepdims=True))

    return _rowwise("final_loss_bwd", body, [("row", h, d), ("vec", g, d), ("row", tgt, d)],
                    [("row", None, d, d, F32), ("acc", 1, d, d, F32), ("acc", 1, LANE, LANE, F32)], s, 256)


def _conv_rows(x, halo, w, b):
    y = b + w[3:4, :] * x
    for k in range(CONV_WIDTH - 1):
        y = y + w[k:k + 1, :] * _shift_down(x, halo, CONV_WIDTH - 1 - k)
    return y


def _conv_fwd(x, w, b, silu, name):
    s, c = x.shape
    cb = _tile(c, 512)

    def body(i, n, x_ref, p_ref, w_ref, b_ref, o_ref):
        halo = jnp.where(i == 0, 0.0, p_ref[...])
        y = _conv_rows(x_ref[...], halo, w_ref[...], b_ref[...])
        o_ref[...] = _silu(y) if silu else y

    return _rowwise(name, body, [("row", x, cb), ("prev", x, cb), ("vec", w, cb), ("vec", b, cb)],
                    [("row", None, c, cb, F32)], s, 512, ncol=c // cb)[0]


def _silu_conv_bwd_pre(dy, x, w, b, name):
    s, c = x.shape
    cb = _tile(c, 512)

    def body(i, n, dy_ref, x_ref, p_ref, w_ref, b_ref, o_ref):
        halo = jnp.where(i == 0, 0.0, p_ref[...])
        y = _conv_rows(x_ref[...], halo, w_ref[...], b_ref[...])
        o_ref[...] = dy_ref[...] * _dsilu(y)

    return _rowwise(name, body, [("row", dy, cb), ("row", x, cb), ("prev", x, cb), ("vec", w, cb), ("vec", b, cb)],
                    [("row", None, c, cb, F32)], s, 512, ncol=c // cb)[0]


def _conv_bwd(dy, x, w, name):
    s, c = x.shape
    cb = _tile(c, 512)

    def body(i, n, dy_ref, nx_ref, x_ref, p_ref, w_ref, dx_ref, dw_ref, db_ref):
        d = dy_ref[...]
        xx = x_ref[...]
        wv = w_ref[...]
        nxt = jnp.where(i == n - 1, 0.0, nx_ref[...])
        prv = jnp.where(i == 0, 0.0, p_ref[...])
        dx = wv[3:4, :] * d
        parts = []
        for k in range(CONV_WIDTH - 1):
            sh = CONV_WIDTH - 1 - k
            dx = dx + wv[k:k + 1, :] * _shift_up(d, nxt, sh)
            parts.append(_colsum(d * _shift_down(xx, prv, sh)))
        parts.append(_colsum(d * xx))
        dx_ref[...] = dx.astype(BF16)
        _acc(i, dw_ref, jnp.concatenate(parts, axis=0))
        _acc(i, db_ref, _colsum(d))

    return _rowwise(name, body, [("row", dy, cb), ("next", dy, cb), ("row", x, cb), ("prev", x, cb), ("vec", w, cb)],
                    [("row", None, c, cb, BF16), ("acc", CONV_WIDTH, c, cb, F32), ("acc", 1, c, cb, F32)],
                    s, 512, ncol=c // cb)


def _lru_gate_math(xr, r_pre, i_pre, lam):
    r = _sigmoid(r_pre)
    ig = _sigmoid(i_pre)
    sp = _softplus(-lam)
    log_a = -LRU_C * r * sp
    a = jnp.exp(log_a)
    mult = jnp.sqrt(_neg_expm1(2.0 * log_a))
    return r, ig, sp, a, mult


def _lru_gates_fwd(xr, wr, wi, br, bi, lam):
    s, d = xr.shape
    nh, bw, _ = wr.shape
    ts = min(512, s)

    def body(x_ref, wr_ref, wi_ref, br_ref, bi_ref, lam_ref, a_ref, b_ref):
        x = x_ref[...]
        xb = x.astype(BF16)
        r_pre = jnp.dot(xb, wr_ref[0], preferred_element_type=F32) + br_ref[...]
        i_pre = jnp.dot(xb, wi_ref[0], preferred_element_type=F32) + bi_ref[...]
        _, ig, _, a, mult = _lru_gate_math(x, r_pre, i_pre, lam_ref[...])
        a_ref[...] = a
        b_ref[...] = mult * (ig * x)

    row = pl.BlockSpec((ts, bw), lambda h, i: (i, h))
    wsp = pl.BlockSpec((1, bw, bw), lambda h, i: (h, 0, 0))
    vec = pl.BlockSpec((1, bw), lambda h, i: (0, h))
    return pl.pallas_call(
        body, name="lru_gates_fwd", grid=(nh, s // ts), in_specs=[row, wsp, wsp, vec, vec, vec], out_specs=[row, row],
        out_shape=[jax.ShapeDtypeStruct((s, d), F32)] * 2, compiler_params=_params("parallel", "arbitrary"),
    )(xr, wr, wi, br, bi, lam)


def _lru_gates_bwd(xr, g, hs, wr, wi, br, bi, lam):
    s, d = xr.shape
    nh, bw, _ = wr.shape
    ts = min(512, s)
    hb = ts // SUBLANE
    tn_dims = (((0,), (0,)), ((), ()))
    nt_dims = (((1,), (1,)), ((), ()))

    def body(x_ref, g_ref, hs_ref, hp_ref, wr_ref, wi_ref, br_ref, bi_ref, lam_ref,
             dx_ref, dwr_ref, dwi_ref, dbr_ref, dbi_ref, dlam_ref):
        i = pl.program_id(1)
        x = x_ref[...]
        xb = x.astype(BF16)
        gg = g_ref[...]
        lam_v = lam_ref[...]
        r_pre = jnp.dot(xb, wr_ref[0], preferred_element_type=F32) + br_ref[...]
        i_pre = jnp.dot(xb, wi_ref[0], preferred_element_type=F32) + bi_ref[...]
        r, ig, sp, a, mult = _lru_gate_math(x, r_pre, i_pre, lam_v)
        h_prev = _shift_down(hs_ref[...], jnp.where(i == 0, 0.0, hp_ref[...]), 1)
        da = gg * h_prev
        dmult = gg * ig * x
        dlog_a = da * a - dmult * (a * a) / mult
        d_r = dlog_a * (-LRU_C * sp)
        dr_pre = d_r * r * (1.0 - r)
        di_pre = (gg * mult * x) * ig * (1.0 - ig)
        drb = dr_pre.astype(BF16)
        dib = di_pre.astype(BF16)
        dx_ref[...] = (gg * mult * ig
                       + lax.dot_general(drb, wr_ref[0], nt_dims, preferred_element_type=F32)
                       + lax.dot_general(dib, wi_ref[0], nt_dims, preferred_element_type=F32))
        dwr = lax.dot_general(xb, drb, tn_dims, preferred_element_type=F32)[None]
        dwi = lax.dot_general(xb, dib, tn_dims, preferred_element_type=F32)[None]
        dlam = _colsum(dlog_a * (-LRU_C * r)) * (-_sigmoid(-lam_v))
        _acc(i, dwr_ref, dwr)
        _acc(i, dwi_ref, dwi)
        _acc(i, dbr_ref, _colsum(dr_pre))
        _acc(i, dbi_ref, _colsum(di_pre))
        _acc(i, dlam_ref, dlam)

    row = pl.BlockSpec((ts, bw), lambda h, i: (i, h))
    prev = pl.BlockSpec((SUBLANE, bw), lambda h, i: (jnp.maximum(i * hb - 1, 0), h))
    wsp = pl.BlockSpec((1, bw, bw), lambda h, i: (h, 0, 0))
    vec = pl.BlockSpec((1, bw), lambda h, i: (0, h))
    return pl.pallas_call(
        body, name="lru_gates_bwd", grid=(nh, s // ts),
        in_specs=[row, row, row, prev, wsp, wsp, vec, vec, vec], out_specs=[row, wsp, wsp, vec, vec, vec],
        out_shape=[jax.ShapeDtypeStruct((s, d), F32), jax.ShapeDtypeStruct((nh, bw, bw), F32),
                   jax.ShapeDtypeStruct((nh, bw, bw), F32)] + [jax.ShapeDtypeStruct((1, d), F32)] * 3,
        compiler_params=_params("parallel", "arbitrary"),
    )(xr, g, hs, hs, wr, wi, br, bi, lam)


def _scan(a, b, reverse, name):
    s, c = a.shape
    cb = _tile(c, 512)
    nt = s // SUBLANE

    def body(a_ref, b_ref, o_ref):
        row = lax.broadcasted_iota(jnp.int32, (SUBLANE, cb), 0)

        def fwd_step(t, carry):
            r0 = pl.multiple_of(t * SUBLANE, SUBLANE)
            aa = a_ref[pl.ds(r0, SUBLANE), :]
            bb = b_ref[pl.ds(r0, SUBLANE), :]
            for sh in (1, 2, 4):
                a_s = jnp.where(row >= sh, pltpu.roll(aa, sh, axis=0), 1.0)
                b_s = jnp.where(row >= sh, pltpu.roll(bb, sh, axis=0), 0.0)
                bb = aa * b_s + bb
                aa = aa * a_s
            h = bb + aa * carry
            o_ref[pl.ds(r0, SUBLANE), :] = h
            return h[SUBLANE - 1:SUBLANE, :]

        def rev_step(k, carry):
            r0 = pl.multiple_of((nt - 1 - k) * SUBLANE, SUBLANE)
            aa = a_ref[pl.ds(r0, SUBLANE), :]
            dd = b_ref[pl.ds(r0, SUBLANE), :]
            cc = aa * dd
            for sh in (1, 2, 4):
                a_s = jnp.where(row < SUBLANE - sh, pltpu.roll(aa, SUBLANE - sh, axis=0), 1.0)
                c_s = jnp.where(row < SUBLANE - sh, pltpu.roll(cc, SUBLANE - sh, axis=0), 0.0)
                cc = cc + aa * c_s
                aa = aa * a_s
            big = cc + aa * carry
            nxt = jnp.where(row < SUBLANE - 1, pltpu.roll(big, SUBLANE - 1, axis=0), carry)
            o_ref[pl.ds(r0, SUBLANE), :] = dd + nxt
            return big[0:1, :]

        lax.fori_loop(0, nt, rev_step if reverse else fwd_step, jnp.zeros((1, cb), F32))

    spec = pl.BlockSpec((s, cb), lambda j: (0, j))
    return pl.pallas_call(body, name=name, grid=(c // cb,), in_specs=[spec, spec], out_specs=spec,
                          out_shape=jax.ShapeDtypeStruct((s, c), F32), compiler_params=_params("parallel"))(a, b)


def _lru_out_fwd(hs, yg):
    s, d = hs.shape
    cb = _tile(d, 1024)

    def body(i, n, h_ref, y_ref, o_ref):
        o_ref[...] = (h_ref[...] * _gelu(y_ref[...])).astype(BF16)

    return _rowwise("lru_out_fwd", body, [("row", hs, cb), ("row", yg, cb)], [("row", None, d, cb, BF16)],
                    s, 512, ncol=d // cb)[0]


def _lru_out_bwd(dy, hs, yg):
    s, d = hs.shape
    cb = _tile(d, 1024)

    def body(i, n, dy_ref, h_ref, y_ref, dh_ref, dyg_ref):
        dyv = dy_ref[...]
        y = y_ref[...]
        dh_ref[...] = dyv * _gelu(y)
        dyg_ref[...] = (dyv * h_ref[...] * _dgelu(y)).astype(BF16)

    return _rowwise("lru_out_bwd", body, [("row", dy, cb), ("row", hs, cb), ("row", yg, cb)],
                    [("row", None, d, cb, F32), ("row", None, d, cb, BF16)], s, 512, ncol=d // cb)


def _swiglu_act(gt, up):
    s, f = gt.shape
    cb = _tile(f, 1024)

    def body(i, n, g_ref, u_ref, o_ref):
        o_ref[...] = (_silu(g_ref[...].astype(F32)) * u_ref[...].astype(F32)).astype(BF16)

    return _rowwise("swiglu_act", body, [("row", gt, cb), ("row", up, cb)], [("row", None, f, cb, BF16)],
                    s, 512, ncol=f // cb)[0]


def _swiglu_bwd(dact, gt, up):
    s, f = gt.shape
    cb = _tile(f, 1024)

    def body(i, n, d_ref, g_ref, u_ref, dg_ref, du_ref):
        d = d_ref[...]
        g = g_ref[...].astype(F32)
        dg_ref[...] = (d * u_ref[...].astype(F32) * _dsilu(g)).astype(BF16)
        du_ref[...] = (d * _silu(g)).astype(BF16)

    return _rowwise("swiglu_bwd", body, [("row", dact, cb), ("row", gt, cb), ("row", up, cb)],
                    [("row", None, f, cb, BF16), ("row", None, f, cb, BF16)], s, 512, ncol=f // cb)


def _ple_fwd(h, gp, pp):
    s, d = h.shape
    cb = _tile(d, 1024)

    def body(i, n, h_ref, g_ref, p_ref, o_ref):
        o_ref[...] = h_ref[...] + _sigmoid(g_ref[...]) * p_ref[...]

    return _rowwise("ple_fwd", body, [("row", h, cb), ("row", gp, cb), ("row", pp, cb)], [("row", None, d, cb, F32)],
                    s, 512, ncol=d // cb)[0]


def _ple_bwd(dh, gp, pp):
    s, d = dh.shape
    cb = _tile(d, 1024)

    def body(i, n, d_ref, g_ref, p_ref, dg_ref, dp_ref):
        dv = d_ref[...]
        sg = _sigmoid(g_ref[...])
        dg_ref[...] = (dv * p_ref[...] * sg * (1.0 - sg)).astype(BF16)
        dp_ref[...] = (dv * sg).astype(BF16)

    return _rowwise("ple_bwd", body, [("row", dh, cb), ("row", gp, cb), ("row", pp, cb)],
                    [("row", None, d, cb, BF16), ("row", None, d, cb, BF16)], s, 512, ncol=d // cb)


def _group_matrix(n_groups):
    r = lax.broadcasted_iota(jnp.int32, (LANE, n_groups * LANE), 0)
    c = lax.broadcasted_iota(jnp.int32, (LANE, n_groups * LANE), 1)
    return ((c % LANE < SSD_HEADS_PER_GROUP) & (r == (c // LANE) * SSD_HEADS_PER_GROUP + c % LANE)).astype(F32)


def _dt_fwd(dt_pre, bias, n_heads, n_groups):
    s = dt_pre.shape[0]
    gl = n_groups * LANE

    def body(i, n, d_ref, b_ref, o_ref):
        lane = lax.broadcasted_iota(jnp.int32, d_ref.shape, 1)
        v = jnp.where(lane < n_heads, _softplus(d_ref[...] + b_ref[...]), 0.0)
        o_ref[...] = jnp.dot(v, _group_matrix(n_groups), preferred_element_type=F32, precision=HIGHEST)

    return _rowwise("ssd_dt_fwd", body, [("row", dt_pre, LANE), ("vec", bias, LANE)], [("row", None, gl, gl, F32)],
                    s, 512)[0]


def _dt_bwd(ddt_g, dt_pre, bias, n_heads, n_groups):
    s = dt_pre.shape[0]
    gl = n_groups * LANE

    def body(i, n, g_ref, d_ref, b_ref, o_ref, db_ref):
        lane = lax.broadcasted_iota(jnp.int32, d_ref.shape, 1)
        ddt = lax.dot_general(g_ref[...], _group_matrix(n_groups), _NT, preferred_element_type=F32, precision=HIGHEST)
        v = jnp.where(lane < n_heads, ddt * _sigmoid(d_ref[...] + b_ref[...]), 0.0)
        o_ref[...] = v.astype(BF16)
        _acc(i, db_ref, _colsum(v))

    return _rowwise("ssd_dt_bwd", body, [("row", ddt_g, gl), ("row", dt_pre, LANE), ("vec", bias, LANE)],
                    [("row", None, LANE, LANE, BF16), ("acc", 1, LANE, LANE, F32)], s, 512)


def _ssd_chunk_terms(dt, alog):
    ln = dt.shape[0]
    a_neg = -jnp.exp(alog)
    row = lax.broadcasted_iota(jnp.int32, (ln, ln), 0)
    col = lax.broadcasted_iota(jnp.int32, (ln, ln), 1)
    tril = row >= col
    cs = jnp.dot(tril.astype(F32), dt * a_neg, preferred_element_type=F32, precision=HIGHEST)
    return a_neg, cs, tril


def _head_lanes(v):
    return [jnp.broadcast_to(v[:, e:e + 1], v.shape) for e in range(SSD_HEADS_PER_GROUP)]


def _ssd_head_scores(bc_cs, cst, cb_mat, tril, e):
    lm = jnp.where(tril, jnp.exp(jnp.minimum(bc_cs[e] - cst[e:e + 1, :], 0.0)), 0.0)
    return (cb_mat * lm).astype(BF16), lm


_NT = (((1,), (1,)), ((), ()))
_TN = (((0,), (0,)), ((), ()))


def _ssd_fwd(xbc, dt, alog, inner, n_groups):
    s = xbc.shape[0]
    ln = SSD_CHUNK
    nc = s // ln
    gw = SSD_HEADS_PER_GROUP * SSD_HEAD_DIM
    npair = gw // LANE
    boff = inner // LANE

    def body(xs_ref, b_ref, c_ref, dt_ref, alog_ref, y_ref, sin_ref, st_ref):
        c = pl.program_id(1)

        @pl.when(c == 0)
        def _():
            st_ref[...] = jnp.zeros_like(st_ref)

        dtv = dt_ref[...]
        _, cs, tril = _ssd_chunk_terms(dtv, alog_ref[...])
        cst = cs.T
        bc_cs, bc_dt = _head_lanes(cs), _head_lanes(dtv)
        xs = xs_ref[...]
        bg = b_ref[...].astype(BF16)
        cg = c_ref[...].astype(BF16)
        cb_mat = lax.dot_general(cg, bg, _NT, preferred_element_type=F32)
        sg = st_ref[...]
        sin_ref[0] = sg
        lo = lax.broadcasted_iota(jnp.int32, (ln, LANE), 1) < SSD_HEAD_DIM
        ys, news = [], []
        for pr in range(npair):
            cols = slice(LANE * pr, LANE * (pr + 1))
            cs_p = jnp.where(lo, bc_cs[2 * pr], bc_cs[2 * pr + 1])
            x = xs[:, cols] * jnp.where(lo, bc_dt[2 * pr], bc_dt[2 * pr + 1])
            tot_p = cs_p[ln - 1:ln, :]
            xp = x.astype(BF16)
            xd = (x * jnp.exp(tot_p - cs_p)).astype(BF16)
            zero = jnp.zeros_like(xp)
            sc0 = _ssd_head_scores(bc_cs, cst, cb_mat, tril, 2 * pr)[0]
            sc1 = _ssd_head_scores(bc_cs, cst, cb_mat, tril, 2 * pr + 1)[0]
            acc = jnp.dot(sc0, jnp.where(lo, xp, zero), preferred_element_type=F32)
            acc = acc + jnp.dot(sc1, jnp.where(lo, zero, xp), preferred_element_type=F32)
            sp = sg[:, cols]
            yoff = jnp.dot(cg, sp.astype(BF16), preferred_element_type=F32) * jnp.exp(cs_p)
            ys.append(acc + yoff)
            news.append(jnp.exp(tot_p) * sp + lax.dot_general(bg, xd, _TN, preferred_element_type=F32))
        y_ref[...] = jnp.concatenate(ys, axis=1)
        st_ref[...] = jnp.concatenate(news, axis=1)

    in_specs = [pl.BlockSpec((ln, gw), lambda g, c: (c, g)),
                pl.BlockSpec((ln, SSD_STATE), lambda g, c: (c, boff + g)),
                pl.BlockSpec((ln, SSD_STATE), lambda g, c: (c, boff + n_groups + g)),
                pl.BlockSpec((ln, LANE), lambda g, c: (c, g)),
                pl.BlockSpec((1, LANE), lambda g, c: (0, g))]
    out_specs = [pl.BlockSpec((ln, gw), lambda g, c: (c, g)),
                 pl.BlockSpec((1, SSD_STATE, gw), lambda g, c: (c, 0, g))]
    return pl.pallas_call(
        body, name="ssd_fwd", grid=(n_groups, nc), in_specs=in_specs, out_specs=out_specs,
        out_shape=[jax.ShapeDtypeStruct((s, inner), F32), jax.ShapeDtypeStruct((nc, SSD_STATE, inner), F32)],
        scratch_shapes=[pltpu.VMEM((SSD_STATE, gw), F32)],
        compiler_params=_params("parallel", "arbitrary"),
    )(xbc, xbc, xbc, dt, alog)


def _ssd_bwd(xbc, dt, alog, dy, y, sin, dskip_e, inner, n_groups):
    s = xbc.shape[0]
    ln = SSD_CHUNK
    nc = s // ln
    gw = SSD_HEADS_PER_GROUP * SSD_HEAD_DIM
    npair = gw // LANE
    boff = inner // LANE

    def body(xs_ref, b_ref, c_ref, dt_ref, alog_ref, dy_ref, y_ref, sin_ref, sout_ref, dsk_ref,
             dxs_ref, db_ref, dc_ref, ddt_ref, dalog_ref, ds_ref):
        step = pl.program_id(1)

        @pl.when(step == 0)
        def _():
            ds_ref[...] = jnp.zeros_like(ds_ref)

        dtv = dt_ref[...]
        a_neg, cs, tril = _ssd_chunk_terms(dtv, alog_ref[...])
        cst = cs.T
        bc_cs, bc_dt = _head_lanes(cs), _head_lanes(dtv)
        xs = xs_ref[...]
        bg = b_ref[...].astype(BF16)
        cg = c_ref[...].astype(BF16)
        cb_mat = lax.dot_general(cg, bg, _NT, preferred_element_type=F32)
        dyv = dy_ref[...]
        yv = y_ref[...]
        dskv = dsk_ref[...]
        s_in = sin_ref[0]
        s_out = sout_ref[0]
        d_s = ds_ref[...]
        lane = lax.broadcasted_iota(jnp.int32, (ln, LANE), 1)
        rowl = lax.broadcasted_iota(jnp.int32, (ln, LANE), 0)
        lo = lane < SSD_HEAD_DIM
        dcb = jnp.zeros((ln, ln), F32)
        dbg = jnp.zeros((ln, SSD_STATE), F32)
        dcg = jnp.zeros((ln, SSD_STATE), F32)
        dcs = jnp.zeros((ln, LANE), F32)
        ddt_x = jnp.zeros((ln, LANE), F32)
        dxs_parts, nds = [], []

        def head_sums(v, pr, into):
            s0 = jnp.sum(jnp.where(lo, v, 0.0), axis=1, keepdims=True)
            s1 = jnp.sum(jnp.where(lo, 0.0, v), axis=1, keepdims=True)
            return into + jnp.where(lane == 2 * pr, s0, 0.0) + jnp.where(lane == 2 * pr + 1, s1, 0.0)

        for pr in range(npair):
            cols = slice(LANE * pr, LANE * (pr + 1))
            cs_p = jnp.where(lo, bc_cs[2 * pr], bc_cs[2 * pr + 1])
            dt_p = jnp.where(lo, bc_dt[2 * pr], bc_dt[2 * pr + 1])
            xs_p = xs[:, cols]
            x = xs_p * dt_p
            tot_p = cs_p[ln - 1:ln, :]
            dec = jnp.exp(tot_p - cs_p)
            xp = x.astype(BF16)
            xd = (x * dec).astype(BF16)
            dy_p = dyv[:, cols]
            dyp = dy_p.astype(BF16)
            dye = (jnp.exp(cs_p) * dy_p).astype(BF16)
            zero = jnp.zeros_like(dyp)
            dxp = jnp.zeros((ln, LANE), F32)
            for e, dym in ((2 * pr, jnp.where(lo, dyp, zero)), (2 * pr + 1, jnp.where(lo, zero, dyp))):
                sc, lm = _ssd_head_scores(bc_cs, cst, cb_mat, tril, e)
                dsc = lax.dot_general(dym, xp, _NT, preferred_element_type=F32)
                dcb = dcb + dsc * lm
                dxp = dxp + lax.dot_general(sc, dym, _TN, preferred_element_type=F32)
            dsp = d_s[:, cols]
            dspb = dsp.astype(BF16)
            dxp = dxp + dec * jnp.dot(bg, dspb, preferred_element_type=F32)
            dcg = dcg + lax.dot_general(dye, s_in[:, cols].astype(BF16), _NT, preferred_element_type=F32)
            dbg = dbg + lax.dot_general(xd, dspb, _NT, preferred_element_type=F32)
            nds.append(jnp.exp(tot_p) * dsp + lax.dot_general(cg, dye, _TN, preferred_element_type=F32))
            dxs_parts.append(dxp * dt_p + dy_p * dskv[:, cols])
            dcs = head_sums(yv[:, cols] * dyp.astype(F32) - xp.astype(F32) * dxp, pr, dcs)
            tot_row = jnp.broadcast_to(_colsum(s_out[:, cols] * dsp), (ln, LANE))
            dcs = head_sums(jnp.where(rowl == ln - 1, tot_row, 0.0), pr, dcs)
            ddt_x = head_sums(dxp * xs_p, pr, ddt_x)
        ds_ref[...] = jnp.concatenate(nds, axis=1)
        dxs_ref[...] = jnp.concatenate(dxs_parts, axis=1)
        dcbb = dcb.astype(BF16)
        dc_ref[...] = dcg + jnp.dot(dcbb, bg, preferred_element_type=F32)
        db_ref[...] = dbg + lax.dot_general(dcbb, cg, _TN, preferred_element_type=F32)
        row = lax.broadcasted_iota(jnp.int32, (ln, ln), 0)
        col = lax.broadcasted_iota(jnp.int32, (ln, ln), 1)
        dadt = jnp.dot((row <= col).astype(F32), dcs, preferred_element_type=F32, precision=HIGHEST)
        ddt_ref[...] = a_neg * dadt + ddt_x
        _acc(step, dalog_ref, _colsum(dadt * dtv) * a_neg)

    def rc(step):
        return nc - 1 - step

    in_specs = [pl.BlockSpec((ln, gw), lambda g, t: (rc(t), g)),
                pl.BlockSpec((ln, SSD_STATE), lambda g, t: (rc(t), boff + g)),
                pl.BlockSpec((ln, SSD_STATE), lambda g, t: (rc(t), boff + n_groups + g)),
                pl.BlockSpec((ln, LANE), lambda g, t: (rc(t), g)),
                pl.BlockSpec((1, LANE), lambda g, t: (0, g)),
                pl.BlockSpec((ln, gw), lambda g, t: (rc(t), g)),
                pl.BlockSpec((ln, gw), lambda g, t: (rc(t), g)),
                pl.BlockSpec((1, SSD_STATE, gw), lambda g, t: (rc(t), 0, g)),
                pl.BlockSpec((1, SSD_STATE, gw), lambda g, t: (jnp.minimum(rc(t) + 1, nc - 1), 0, g)),
                pl.BlockSpec((1, gw), lambda g, t: (0, g))]
    out_specs = [pl.BlockSpec((ln, gw), lambda g, t: (rc(t), g)),
                 pl.BlockSpec((ln, SSD_STATE), lambda g, t: (rc(t), g)),
                 pl.BlockSpec((ln, SSD_STATE), lambda g, t: (rc(t), g)),
                 pl.BlockSpec((ln, LANE), lambda g, t: (rc(t), g)),
                 pl.BlockSpec((1, LANE), lambda g, t: (0, g))]
    gn = n_groups * SSD_STATE
    return pl.pallas_call(
        body, name="ssd_bwd", grid=(n_groups, nc), in_specs=in_specs, out_specs=out_specs,
        out_shape=[jax.ShapeDtypeStruct((s, inner + 2 * gn), F32), jax.ShapeDtypeStruct((s, gn), F32),
                   jax.ShapeDtypeStruct((s, gn), F32), jax.ShapeDtypeStruct((s, n_groups * LANE), F32),
                   jax.ShapeDtypeStruct((1, n_groups * LANE), F32)],
        scratch_shapes=[pltpu.VMEM((SSD_STATE, gw), F32)],
        compiler_params=_params("parallel", "arbitrary"),
    )(xbc, xbc, xbc, dt, alog, dy, y, sin, sin, dskip_e)


def _put_cols(buf, part, block, name):
    s, w = part.shape
    ts = min(512, s)

    def body(b_ref, p_ref, o_ref):
        o_ref[...] = p_ref[...]

    return pl.pallas_call(
        body, name=name, grid=(s // ts,), in_specs=[_ANY, pl.BlockSpec((ts, w), lambda i: (i, 0))],
        out_specs=pl.BlockSpec((ts, w), lambda i: (i, block)), out_shape=jax.ShapeDtypeStruct(buf.shape, buf.dtype),
        input_output_aliases={0: 0}, compiler_params=_params("parallel"))(buf, part)


def _ssd_gate_norm_fwd(ysc, xbc, z, dskip_e, norm_g, n_groups):
    s, inner = ysc.shape
    gsz = inner // n_groups

    def body(i, n, y_ref, x_ref, z_ref, d_ref, g_ref, o_ref):
        y2 = (y_ref[...] + d_ref[...] * x_ref[...]) * _silu(z_ref[...])
        gg = g_ref[...]
        outs = []
        for k in range(n_groups):
            cols = slice(k * gsz, (k + 1) * gsz)
            v = y2[:, cols]
            r = lax.rsqrt(jnp.mean(v * v, axis=-1, keepdims=True) + NORM_EPS)
            outs.append(v * r * gg[:, cols])
        o_ref[...] = jnp.concatenate(outs, axis=1).astype(BF16)

    return _rowwise("ssd_gate_norm_fwd", body,
                    [("row", ysc, inner), ("row", xbc, inner), ("row", z, inner), ("vec", dskip_e, inner),
                     ("vec", norm_g, inner)], [("row", None, inner, inner, BF16)], s, 128)[0]


def _ssd_gate_norm_bwd(dyn, ysc, xbc, z, dskip_e, norm_g, n_groups):
    s, inner = ysc.shape
    gsz = inner // n_groups

    def body(i, n, dn_ref, y_ref, x_ref, z_ref, d_ref, g_ref, dy_ref, dz_ref, dg_ref, dd_ref):
        xs = x_ref[...]
        zz = z_ref[...]
        y = y_ref[...] + d_ref[...] * xs
        sz = _silu(zz)
        y2 = y * sz
        dn = dn_ref[...]
        gg = g_ref[...]
        dy2s, dgs = [], []
        for k in range(n_groups):
            cols = slice(k * gsz, (k + 1) * gsz)
            v = y2[:, cols]
            d = dn[:, cols]
            r = lax.rsqrt(jnp.mean(v * v, axis=-1, keepdims=True) + NORM_EPS)
            vh = v * r
            dgs.append(_colsum(d * vh))
            dvh = d * gg[:, cols]
            dy2s.append(r * (dvh - vh * jnp.mean(dvh * vh, axis=-1, keepdims=True)))
        dy2 = jnp.concatenate(dy2s, axis=1)
        dy = dy2 * sz
        dy_ref[...] = dy
        dz_ref[...] = (dy2 * y * _dsilu(zz)).astype(BF16)
        _acc(i, dg_ref, jnp.concatenate(dgs, axis=1))
        _acc(i, dd_ref, _colsum(dy * xs))

    return _rowwise("ssd_gate_norm_bwd", body,
                    [("row", dyn, inner), ("row", ysc, inner), ("row", xbc, inner), ("row", z, inner),
                     ("vec", dskip_e, inner), ("vec", norm_g, inner)],
                    [("row", None, inner, inner, F32), ("row", None, inner, inner, BF16),
                     ("acc", 1, inner, inner, F32), ("acc", 1, inner, inner, F32)], s, 128)


def _adamw(w, g, m, v, name, emit_g=False):
    rows, c = w.shape
    bc1 = 1.0 - ADAM_B1 ** ADAM_STEP
    bc2 = 1.0 - ADAM_B2 ** ADAM_STEP

    def body(i, n, w_ref, g_ref, m_ref, v_ref, d_ref, mo_ref, vo_ref, *go_ref):
        gg = g_ref[...]
        if emit_g:
            go_ref[0][...] = gg
        mn = ADAM_B1 * m_ref[...] + (1.0 - ADAM_B1) * gg
        vn = ADAM_B2 * v_ref[...] + (1.0 - ADAM_B2) * (gg * gg)
        d_ref[...] = -ADAM_LR * ((mn / bc1) / (jnp.sqrt(vn / bc2) + ADAM_EPS) + ADAM_WD * w_ref[...])
        mo_ref[...] = mn
        vo_ref[...] = vn

    ts = 128 if rows % 128 == 0 else rows
    return _rowwise(name, body, [("row", w, c), ("row", g, c), ("row", m, c), ("row", v, c)],
                    [("row", None, c, c, F32)] * (4 if emit_g else 3), rows, ts)


_ANY = pl.BlockSpec(memory_space=pl.ANY)


def _place():
    x, y, c = lax.axis_index("x"), lax.axis_index("y"), lax.axis_index("c")
    chips = [(1 - x, y), (x, 1 - y), (1 - x, 1 - y)]
    return x, y, c, chips


def _rcopy(src, dst, ssem, rsem, dev):
    return pltpu.make_async_remote_copy(src_ref=src, dst_ref=dst, send_sem=ssem, recv_sem=rsem, device_id=dev,
                                        device_id_type=MESH)


def _place_shard(shards, layer, q_idx, dtype, name):
    _, r, cc = shards.shape
    tr = _tile(r, 256, 16)

    def body(q_ref, s_ref, o_ref):
        o_ref[0] = s_ref[0].astype(dtype)

    grid_spec = pltpu.PrefetchScalarGridSpec(
        num_scalar_prefetch=1, grid=(r // tr,),
        in_specs=[pl.BlockSpec((1, tr, cc), lambda i, q_ref: (layer, i, 0))],
        out_specs=pl.BlockSpec((1, tr, cc), lambda i, q_ref: (q_ref[0], i, 0)))
    return pl.pallas_call(body, name=name, grid_spec=grid_spec, out_shape=jax.ShapeDtypeStruct((N_CHIPS, r, cc), dtype),
                          compiler_params=_params("parallel"))(q_idx, shards)


_HBM = pl.BlockSpec(memory_space=pltpu.HBM)
_SEM = pl.BlockSpec(memory_space=pltpu.SEMAPHORE)
_EFFECT = pltpu.SideEffectType.DATAFLOW_SIDE_EFFECTING


def _in_hbm(arrs):
    return [pltpu.with_memory_space_constraint(a, pltpu.HBM) for a in arrs]


def _gather_start(bufs, name):
    n = len(bufs)
    half = [e.shape[1] // 2 for e in bufs]

    def body(*refs):
        ins, send, recv, token = refs[:n], refs[n], refs[n + 1], refs[2 * n + 2]
        x, y, c, chips = _place()
        q = 2 * x + y
        for e in range(n):
            blk = ins[e].at[q, pl.ds(c * half[e], half[e])]
            for j, (cx, cy) in enumerate(chips):
                _rcopy(blk, blk, send.at[3 * e + j], recv.at[3 * e + j], (cx, cy, c)).start()
        token[...] = jnp.zeros_like(token)

    out = pl.pallas_call(
        body, name=name,
        out_shape=(pltpu.SemaphoreType.DMA((3 * n,)), pltpu.SemaphoreType.DMA((3 * n,)),
                   *[pltpu.HBM(b.shape, b.dtype) for b in bufs], jax.ShapeDtypeStruct((SUBLANE, LANE), F32)),
        in_specs=[_HBM] * n, out_specs=(_SEM, _SEM, *[_HBM] * n, pl.BlockSpec(memory_space=pltpu.VMEM)),
        input_output_aliases={e: 2 + e for e in range(n)},
        compiler_params=pltpu.CompilerParams(has_side_effects=_EFFECT))(*_in_hbm(bufs))
    return out[0], out[1], list(out[2:2 + n]), out[2 + n]


def _gather_wait(send, recv, bufs, after, name):
    n = len(bufs)
    half = [e.shape[1] // 2 for e in bufs]

    def body(*refs):
        ins, send_ref, recv_ref = refs[:n], refs[n], refs[n + 1]
        x, y, c, chips = _place()
        q = 2 * x + y
        for e in range(n):
            rows = pl.ds(c * half[e], half[e])
            for j, (cx, cy) in enumerate(chips):
                cp = _rcopy(ins[e].at[q, rows], ins[e].at[2 * cx + cy, rows], send_ref.at[3 * e + j],
                            recv_ref.at[3 * e + j], (cx, cy, c))
                cp.wait_send()
                cp.wait_recv()

    return list(pl.pallas_call(
        body, name=name, out_shape=tuple(pltpu.HBM(b.shape, b.dtype) for b in bufs),
        in_specs=[_HBM] * n + [_SEM, _SEM, _ANY], out_specs=[_HBM] * n,
        input_output_aliases={e: e for e in range(n)},
        compiler_params=pltpu.CompilerParams(has_side_effects=_EFFECT))(*bufs, send, recv, after))


def _forward_sibling(bufs, name):
    n = len(bufs)
    half = [e.shape[1] // 2 for e in bufs]

    def body(*refs):
        outs = refs[n:2 * n]
        send, recv = refs[2 * n:]
        x, y, c, chips = _place()
        sib = (x, y, 1 - c)
        cps = []
        for e in range(n):
            for j, (cx, cy) in enumerate(chips):
                blk = outs[e].at[2 * cx + cy, pl.ds(c * half[e], half[e])]
                cps.append(_rcopy(blk, blk, send.at[3 * e + j], recv.at[3 * e + j], sib))
        for cp in cps:
            cp.start()
        for e in range(n):
            for j, (cx, cy) in enumerate(chips):
                blk = outs[e].at[2 * cx + cy, pl.ds((1 - c) * half[e], half[e])]
                _rcopy(blk, blk, send.at[3 * e + j], recv.at[3 * e + j], sib).wait_recv()
        for cp in cps:
            cp.wait_send()

    return list(pl.pallas_call(
        body, name=name, in_specs=[_ANY] * n, out_specs=[_ANY] * n,
        out_shape=[jax.ShapeDtypeStruct(e.shape, e.dtype) for e in bufs],
        input_output_aliases={e: e for e in range(n)},
        scratch_shapes=[pltpu.SemaphoreType.DMA((3 * n,))] * 2,
    )(*bufs))


def _scatter_start(entries, name):
    n = len(entries)
    lands = [lax.empty(e.shape, e.dtype) for e in entries]

    def body(*refs):
        ins, land, send, recv, token = refs[:n], refs[n:2 * n], refs[2 * n], refs[2 * n + 1], refs[4 * n + 2]
        x, y, c, chips = _place()
        q = 2 * x + y
        for e in range(n):
            for j, (cx, cy) in enumerate(chips):
                _rcopy(ins[e].at[2 * cx + cy], land[e].at[q], send.at[3 * e + j], recv.at[3 * e + j],
                       (cx, cy, c)).start()
        token[...] = jnp.zeros_like(token)

    out = pl.pallas_call(
        body, name=name,
        out_shape=(pltpu.SemaphoreType.DMA((3 * n,)), pltpu.SemaphoreType.DMA((3 * n,)),
                   *[pltpu.HBM(b.shape, b.dtype) for b in entries + lands],
                   jax.ShapeDtypeStruct((SUBLANE, LANE), F32)),
        in_specs=[_HBM] * (2 * n),
        out_specs=(_SEM, _SEM, *[_HBM] * (2 * n), pl.BlockSpec(memory_space=pltpu.VMEM)),
        input_output_aliases={e: 2 + e for e in range(2 * n)},
        compiler_params=pltpu.CompilerParams(has_side_effects=_EFFECT))(*_in_hbm(entries + lands))
    return out[0], out[1], list(out[2:2 + n]), list(out[2 + n:2 + 2 * n]), out[2 + 2 * n]


def _scatter_wait(send, recv, entries, lands, after, name):
    n = len(entries)

    def body(*refs):
        ins, land, send_ref, recv_ref = refs[:n], refs[n:2 * n], refs[2 * n], refs[2 * n + 1]
        x, y, c, chips = _place()
        for e in range(n):
            for j, (cx, cy) in enumerate(chips):
                k = 2 * cx + cy
                cp = _rcopy(ins[e].at[k], land[e].at[k], send_ref.at[3 * e + j], recv_ref.at[3 * e + j],
                            (cx, cy, c))
                cp.wait_send()
                cp.wait_recv()

    out = pl.pallas_call(
        body, name=name, out_shape=tuple(pltpu.HBM(b.shape, b.dtype) for b in entries + lands),
        in_specs=[_HBM] * (2 * n) + [_SEM, _SEM, _ANY], out_specs=[_HBM] * (2 * n),
        input_output_aliases={e: e for e in range(2 * n)},
        compiler_params=pltpu.CompilerParams(has_side_effects=_EFFECT))(*entries, *lands, send, recv, after)
    return list(out[:n]), list(out[n:])


def _swap_start(entries, name):
    n = len(entries)
    half = [e.shape[1] // 2 for e in entries]
    lands = [lax.empty((N_CHIPS, h, e.shape[2]), e.dtype) for e, h in zip(entries, half)]

    def body(*refs):
        ins, land, send, recv, token = refs[:n], refs[n:2 * n], refs[2 * n], refs[2 * n + 1], refs[4 * n + 2]
        x, y, c, _ = _place()
        for e in range(n):
            _rcopy(ins[e].at[:, pl.ds((1 - c) * half[e], half[e]), :], land[e], send.at[e], recv.at[e],
                   (x, y, 1 - c)).start()
        token[...] = jnp.zeros_like(token)

    out = pl.pallas_call(
        body, name=name,
        out_shape=(pltpu.SemaphoreType.DMA((n,)), pltpu.SemaphoreType.DMA((n,)),
                   *[pltpu.HBM(b.shape, b.dtype) for b in entries + lands],
                   jax.ShapeDtypeStruct((SUBLANE, LANE), F32)),
        in_specs=[_HBM] * (2 * n),
        out_specs=(_SEM, _SEM, *[_HBM] * (2 * n), pl.BlockSpec(memory_space=pltpu.VMEM)),
        input_output_aliases={e: 2 + e for e in range(2 * n)},
        compiler_params=pltpu.CompilerParams(has_side_effects=_EFFECT))(*_in_hbm(entries + lands))
    return out[0], out[1], list(out[2:2 + n]), list(out[2 + n:2 + 2 * n]), out[2 + 2 * n]


def _swap_wait(send, recv, entries, lands, after, name):
    n = len(entries)
    half = [e.shape[1] // 2 for e in entries]

    def body(*refs):
        ins, land, send_ref, recv_ref = refs[:n], refs[n:2 * n], refs[2 * n], refs[2 * n + 1]
        x, y, c, _ = _place()
        for e in range(n):
            cp = _rcopy(ins[e].at[:, pl.ds((1 - c) * half[e], half[e]), :], land[e], send_ref.at[e], recv_ref.at[e],
                        (x, y, 1 - c))
            cp.wait_send()
            cp.wait_recv()

    out = pl.pallas_call(
        body, name=name, out_shape=tuple(pltpu.HBM(b.shape, b.dtype) for b in entries + lands),
        in_specs=[_HBM] * (2 * n) + [_SEM, _SEM, _ANY], out_specs=[_HBM] * (2 * n),
        input_output_aliases={e: e for e in range(2 * n)},
        compiler_params=pltpu.CompilerParams(has_side_effects=_EFFECT))(*entries, *lands, send, recv, after)
    return list(out[:n]), list(out[n:])


def _swap_halves(entries, name):
    n = len(entries)
    half = [e.shape[1] // 2 for e in entries]

    def body(*refs):
        ins, outs = refs[:n], refs[n:2 * n]
        send, recv = refs[2 * n:]
        x, y, c, _ = _place()
        cps = [_rcopy(ins[e].at[:, pl.ds((1 - c) * half[e], half[e]), :], outs[e], send.at[e], recv.at[e],
                      (x, y, 1 - c)) for e in range(n)]
        for cp in cps:
            cp.start()
        for cp in cps:
            cp.wait()

    return pl.pallas_call(
        body, name=name, in_specs=[_ANY] * n, out_specs=[_ANY] * n,
        out_shape=[jax.ShapeDtypeStruct((N_CHIPS, h, e.shape[2]), e.dtype) for e, h in zip(entries, half)],
        scratch_shapes=[pltpu.SemaphoreType.DMA((n,))] * 2,
    )(*entries)


def _join_halves(bufs, name):
    n = len(bufs)
    pairs = [(o, layer) for o in range(n) for layer in range(bufs[o].shape[0])]
    npair = len(pairs)

    def body(*refs):
        outs = refs[n:2 * n]
        send, recv = refs[2 * n:]
        x, y, c, _ = _place()
        cps = []
        for k, (o, layer) in enumerate(pairs):
            r2 = bufs[o].shape[1] // 2
            blk = outs[o].at[layer, pl.ds(c * r2, r2)]
            cps.append(_rcopy(blk, blk, send.at[k], recv.at[k], (x, y, 1 - c)))
        for cp in cps:
            cp.start()
        for k, (o, layer) in enumerate(pairs):
            r2 = bufs[o].shape[1] // 2
            blk = outs[o].at[layer, pl.ds((1 - c) * r2, r2)]
            _rcopy(blk, blk, send.at[k], recv.at[k], (x, y, 1 - c)).wait_recv()
        for cp in cps:
            cp.wait_send()

    return pl.pallas_call(
        body, name=name, in_specs=[_ANY] * n, out_specs=[_ANY] * n,
        out_shape=[jax.ShapeDtypeStruct(b.shape, b.dtype) for b in bufs],
        input_output_aliases={e: e for e in range(n)},
        scratch_shapes=[pltpu.SemaphoreType.DMA((npair,))] * 2,
    )(*bufs)


def _gather_all(v, after, name):
    def body(v_ref, after_ref, o_ref, send, recv, loc):
        x, y, c, _ = _place()
        me = 4 * x + 2 * y + c
        mine = pltpu.make_async_copy(v_ref, o_ref.at[me], loc)
        mine.start()
        peers = []
        for k in range(1, N_DEV):
            px = 1 - x if k & 4 else x
            py = 1 - y if k & 2 else y
            pc = 1 - c if k & 1 else c
            peers.append((px, py, pc))
        cps = [_rcopy(v_ref, o_ref.at[me], send.at[k], recv.at[k], peers[k]) for k in range(N_DEV - 1)]
        for cp in cps:
            cp.start()
        for k, (px, py, pc) in enumerate(peers):
            blk = o_ref.at[4 * px + 2 * py + pc]
            _rcopy(blk, blk, send.at[k], recv.at[k], (px, py, pc)).wait_recv()
        for cp in cps:
            cp.wait_send()
        mine.wait()

    return pl.pallas_call(
        body, name=name, in_specs=[_ANY, _ANY], out_specs=_ANY,
        out_shape=jax.ShapeDtypeStruct((N_DEV,) + v.shape, v.dtype),
        scratch_shapes=[pltpu.SemaphoreType.DMA((N_DEV - 1,))] * 2 + [pltpu.SemaphoreType.DMA],
    )(v, after)


def _add_own_half(gst, rx, c_idx, name):
    _, r, cc = gst.shape
    r2 = r // 2
    tr = _tile(r2, 256, 16)
    g4 = gst.reshape(N_CHIPS, 2, r2, cc)

    def body(c_ref, g_ref, r_ref, o_ref):
        o_ref[...] = (g_ref[0].astype(F32) + r_ref[...].astype(F32)).astype(BF16)

    grid_spec = pltpu.PrefetchScalarGridSpec(
        num_scalar_prefetch=1, grid=(N_CHIPS, r2 // tr),
        in_specs=[pl.BlockSpec((1, 1, tr, cc), lambda k, i, c_ref: (k, c_ref[0], i, 0)),
                  pl.BlockSpec((1, tr, cc), lambda k, i, c_ref: (k, i, 0))],
        out_specs=pl.BlockSpec((1, tr, cc), lambda k, i, c_ref: (k, i, 0)))
    return pl.pallas_call(body, name=name, grid_spec=grid_spec, out_shape=jax.ShapeDtypeStruct((N_CHIPS, r2, cc), BF16),
                          compiler_params=_params("parallel", "parallel"))(c_idx, g4, rx)


def _sum_into(buf, rx, own, layer, qc, out_shape, name):
    _, r2, cc = rx.shape
    tr = _tile(r2, 256, 16)
    nb = r2 // tr

    def body(qc_ref, *refs):
        rx_ref, own_ref, o_ref = refs[-3:]
        q = qc_ref[0]
        acc = None
        for k in range(N_CHIPS):
            v = jnp.where(q == k, own_ref[0], rx_ref[k]).astype(F32)
            acc = v if acc is None else acc + v
        o_ref[0] = acc

    in_specs = [pl.BlockSpec((N_CHIPS, tr, cc), lambda i, qc_ref: (0, i, 0)),
                pl.BlockSpec((1, tr, cc), lambda i, qc_ref: (qc_ref[0], i, 0))]
    args = (rx, own)
    aliases = {}
    if buf is not None:
        in_specs = [_ANY] + in_specs
        args = (buf,) + args
        aliases = {1: 0}
    grid_spec = pltpu.PrefetchScalarGridSpec(
        num_scalar_prefetch=1, grid=(nb,), in_specs=in_specs,
        out_specs=pl.BlockSpec((1, tr, cc), lambda i, qc_ref: (layer, qc_ref[1] * nb + i, 0)))
    return pl.pallas_call(body, name=name, grid_spec=grid_spec, out_shape=jax.ShapeDtypeStruct(out_shape, F32),
                          input_output_aliases=aliases, compiler_params=_params("parallel"))(qc, *args)


def _sum_slots(st, name):
    k, r, cc = st.shape
    tr = _tile(r, 256, 8)

    def body(s_ref, o_ref):
        acc = s_ref[0].astype(F32)
        for j in range(1, k):
            acc = acc + s_ref[j].astype(F32)
        o_ref[...] = acc

    return pl.pallas_call(body, name=name, grid=(r // tr,), in_specs=[pl.BlockSpec((k, tr, cc), lambda i: (0, i, 0))],
                          out_specs=pl.BlockSpec((tr, cc), lambda i: (i, 0)),
                          out_shape=jax.ShapeDtypeStruct((r, cc), F32), compiler_params=_params("parallel"))(st)


def _pack(arrs, rows_mult=2 * SUBLANE):
    flat = jnp.concatenate([a.reshape(-1).astype(F32) for a in arrs])
    quantum = rows_mult * LANE
    padded = -(-flat.shape[0] // quantum) * quantum
    return jnp.pad(flat, (0, padded - flat.shape[0])).reshape(-1, LANE)


def _unpack(buf, shapes):
    flat = buf.reshape(-1)
    out, off = [], 0
    for sh in shapes:
        size = 1
        for d in sh:
            size *= d
        out.append(flat[off:off + size].reshape(sh))
        off += size
    return out


def kernel(x, p, norm_mix_g, norm_ffn_g, norm_ple_g, final_norm_g, a_w_in, a_conv_w, a_conv_b, a_w_gate_r, a_b_gate_r, a_w_gate_i, a_b_gate_i, a_lambda, a_w_out, b_w_in, b_conv_w, b_conv_b, b_dt_bias, b_a_log, b_d_skip, b_norm_g, b_w_out, ffn_w_gate, ffn_w_up, ffn_w_down, ple_w_proj, ple_w_gate, loss_target, m_norm_mix_g, m_norm_ffn_g, m_norm_ple_g, m_final_norm_g, m_a_w_in, m_a_conv_w, m_a_conv_b, m_a_w_gate_r, m_a_b_gate_r, m_a_w_gate_i, m_a_b_gate_i, m_a_lambda, m_a_w_out, m_b_w_in, m_b_conv_w, m_b_conv_b, m_b_dt_bias, m_b_a_log, m_b_d_skip, m_b_norm_g, m_b_w_out, m_ffn_w_gate, m_ffn_w_up, m_ffn_w_down, m_ple_w_proj, m_ple_w_gate, v_norm_mix_g, v_norm_ffn_g, v_norm_ple_g, v_final_norm_g, v_a_w_in, v_a_conv_w, v_a_conv_b, v_a_w_gate_r, v_a_b_gate_r, v_a_w_gate_i, v_a_b_gate_i, v_a_lambda, v_a_w_out, v_b_w_in, v_b_conv_w, v_b_conv_b, v_b_dt_bias, v_b_a_log, v_b_d_skip, v_b_norm_g, v_b_w_out, v_ffn_w_gate, v_ffn_w_up, v_ffn_w_down, v_ple_w_proj, v_ple_w_gate):
    names = ["norm_mix_g", "norm_ffn_g", "norm_ple_g", "final_norm_g", "a_w_in", "a_conv_w", "a_conv_b", "a_w_gate_r",
             "a_b_gate_r", "a_w_gate_i", "a_b_gate_i", "a_lambda", "a_w_out", "b_w_in", "b_conv_w", "b_conv_b",
             "b_dt_bias", "b_a_log", "b_d_skip", "b_norm_g", "b_w_out", "ffn_w_gate", "ffn_w_up", "ffn_w_down",
             "ple_w_proj", "ple_w_gate"]
    w_in = dict(zip(names, [norm_mix_g, norm_ffn_g, norm_ple_g, final_norm_g, a_w_in, a_conv_w, a_conv_b, a_w_gate_r,
                            a_b_gate_r, a_w_gate_i, a_b_gate_i, a_lambda, a_w_out, b_w_in, b_conv_w, b_conv_b,
                            b_dt_bias, b_a_log, b_d_skip, b_norm_g, b_w_out, ffn_w_gate, ffn_w_up, ffn_w_down,
                            ple_w_proj, ple_w_gate]))
    m_in = dict(zip(names, [m_norm_mix_g, m_norm_ffn_g, m_norm_ple_g, m_final_norm_g, m_a_w_in, m_a_conv_w,
                            m_a_conv_b, m_a_w_gate_r, m_a_b_gate_r, m_a_w_gate_i, m_a_b_gate_i, m_a_lambda,
                            m_a_w_out, m_b_w_in, m_b_conv_w, m_b_conv_b, m_b_dt_bias, m_b_a_log, m_b_d_skip,
                            m_b_norm_g, m_b_w_out, m_ffn_w_gate, m_ffn_w_up, m_ffn_w_down, m_ple_w_proj,
                            m_ple_w_gate]))
    v_in = dict(zip(names, [v_norm_mix_g, v_norm_ffn_g, v_norm_ple_g, v_final_norm_g, v_a_w_in, v_a_conv_w,
                            v_a_conv_b, v_a_w_gate_r, v_a_b_gate_r, v_a_w_gate_i, v_a_b_gate_i, v_a_lambda,
                            v_a_w_out, v_b_w_in, v_b_conv_w, v_b_conv_b, v_b_dt_bias, v_b_a_log, v_b_d_skip,
                            v_b_norm_g, v_b_w_out, v_ffn_w_gate, v_ffn_w_up, v_ffn_w_down, v_ple_w_proj,
                            v_ple_w_gate]))

    s, d = x.shape[1], x.shape[2]
    depth = norm_mix_g.shape[0]
    assert depth == 2
    q_idx = 2 * lax.axis_index("x") + lax.axis_index("y")
    c_idx = lax.axis_index("c").astype(jnp.int32).reshape(1)

    inner = b_w_out.shape[1] * N_CHIPS
    n_heads = inner // SSD_HEAD_DIM
    n_groups = n_heads // SSD_HEADS_PER_GROUP
    gn = n_groups * SSD_STATE
    xbcw = inner + 2 * gn
    assert b_conv_w.shape[2] * N_CHIPS == xbcw and n_heads <= LANE

    big = [("a_w_in", "col"), ("a_w_gate_r", "gate"), ("a_w_gate_i", "gate"), ("a_w_out", "row"),
           ("ffn_w_gate", "col"), ("ffn_w_up", "col"), ("ffn_w_down", "row"), ("ple_w_proj", "col"),
           ("ple_w_gate", "row"), ("b_w_in", "col"), ("b_w_out", "row")]
    kind_of = dict(big)

    def shard2d(name, arr):
        if kind_of[name] == "gate":
            return [arr[l].reshape(-1, arr.shape[-1]) for l in range(arr.shape[0])]
        return [arr[l] for l in range(arr.shape[0])]

    small_sharded = ["a_conv_w", "a_b_gate_r", "a_b_gate_i", "b_conv_w", "b_conv_b", "b_norm_g"]
    small_pack = _pack([w_in[nm] for nm in small_sharded], rows_mult=16)

    q_vec = q_idx.astype(jnp.int32).reshape(1)
    qc_vec = jnp.stack([q_idx, lax.axis_index("c")]).astype(jnp.int32)
    gather_groups = [
        [("a_w_in", 0), ("small", 0)],
        [("a_w_gate_r", 0), ("a_w_gate_i", 0), ("a_w_out", 0)],
        [("ffn_w_gate", 0), ("ffn_w_up", 0)],
        [("ffn_w_down", 0), ("ple_w_proj", 0), ("ple_w_gate", 0)],
        [("b_w_in", 0)],
        [("b_w_out", 0)],
        [("ffn_w_gate", 1), ("ffn_w_up", 1), ("ffn_w_down", 1), ("ple_w_proj", 1), ("ple_w_gate", 1)],
    ]
    gather_started = [None] * len(gather_groups)

    def gather_begin(gi):
        bufs = []
        for nm, l in gather_groups[gi]:
            if nm == "small":
                bufs.append(_place_shard(small_pack[None], 0, q_vec, F32, "place_small"))
            else:
                arr = w_in[nm]
                arr = arr.reshape(arr.shape[0], -1, arr.shape[-1]) if kind_of[nm] == "gate" else arr
                bufs.append(_place_shard(arr, l, q_vec, BF16, f"place_{nm}{l}"))
        gather_started[gi] = _gather_start(bufs, f"gather_start{gi}")
        return gather_started[gi][3][0:1, 0:1]

    wst = {}

    def gather_finish(gi, after):
        send, recv, thru, _ = gather_started[gi]
        landed = _gather_wait(send, recv, thru, after, f"gather_wait{gi}")
        for k, arr in zip(gather_groups[gi], _forward_sibling(landed, f"gather_forward{gi}")):
            wst[k] = arr

    gather_finish(0, sum(gather_begin(gi) for gi in range(len(gather_groups))))
    small_st = wst[("small", 0)]
    kept_stacked = ("a_w_in", "ffn_w_gate", "ffn_w_up", "ple_w_proj")

    def whole(nm, l):
        st = wst[(nm, l)]
        kind = kind_of[nm]
        if nm in kept_stacked:
            return st
        if kind == "row":
            return st.reshape(-1, st.shape[-1])
        if kind == "col":
            return jnp.concatenate([st[k] for k in range(N_CHIPS)], axis=1)
        heads = w_in[nm].shape[1]
        return st.reshape(N_CHIPS, heads, -1, st.shape[-1]).transpose(1, 0, 2, 3).reshape(heads, -1, st.shape[-1])

    small_parts = [_unpack(small_st[k], [w_in[nm].shape for nm in small_sharded]) for k in range(N_CHIPS)]
    small_full = {nm: jnp.concatenate([small_parts[k][i] for k in range(N_CHIPS)], axis=-1)
                  for i, nm in enumerate(small_sharded)}
    a_cw = small_full["a_conv_w"][0]
    a_br = small_full["a_b_gate_r"][0].reshape(1, -1)
    a_bi = small_full["a_b_gate_i"][0].reshape(1, -1)
    b_cw = small_full["b_conv_w"][0]
    b_cb = small_full["b_conv_b"]
    b_ng = small_full["b_norm_g"]

    def pad_lanes(v):
        return jnp.pad(v, ((0, 0), (0, LANE - v.shape[1])))

    dt_bias = pad_lanes(b_dt_bias)
    a_log = pad_lanes(b_a_log.reshape(n_groups, SSD_HEADS_PER_GROUP)).reshape(1, n_groups * LANE)
    dskip_e = jnp.repeat(b_d_skip, SSD_HEAD_DIM, axis=1)

    w_a_in = whole("a_w_in", 0)
    w_ax, w_ay = w_a_in[:N_CHIPS // 2], w_a_in[N_CHIPS // 2:]
    w_fg, w_fu, w_fd, w_pp, w_pg = ([None] * depth for _ in range(5))

    def take_ffn_in(l):
        w_fg[l], w_fu[l] = whole("ffn_w_gate", l), whole("ffn_w_up", l)

    def take_ffn_out_ple(l):
        w_fd[l], w_pp[l], w_pg[l] = whole("ffn_w_down", l), whole("ple_w_proj", l), whole("ple_w_gate", l)

    grads = {}

    h0 = x[0]
    g_mix = [norm_mix_g[l:l + 1] for l in range(depth)]
    g_ffn = [norm_ffn_g[l:l + 1] for l in range(depth)]
    g_ple = [norm_ple_g[l:l + 1] for l in range(depth)]
    g_fin = final_norm_g.reshape(1, -1)

    u0 = _rmsnorm_fwd(h0, g_mix[0], "norm_mix0")
    xr_pre = _mm(u0, w_ax, stacked_b=True, name="lru_in_x")
    yg = _mm(u0, w_ay, stacked_b=True, name="lru_in_y")
    xr = _conv_fwd(xr_pre, a_cw, a_conv_b, False, "lru_conv")
    gather_finish(1, xr)
    w_ar, w_ai, w_ao = whole("a_w_gate_r", 0), whole("a_w_gate_i", 0), whole("a_w_out", 0)
    lru_a, lru_b = _lru_gates_fwd(xr, w_ar, w_ai, a_br, a_bi, a_lambda)
    hs = _scan(lru_a, lru_b, False, "lru_scan")
    y_lru = _lru_out_fwd(hs, yg)
    h_mix = [_mm(y_lru, w_ao, add=h0, name="lru_out"), None]

    def ffn_ple_fwd(h_in, l, before_down=None):
        n_f = _rmsnorm_fwd(h_in, g_ffn[l], f"norm_ffn{l}")
        gt = _mm(n_f, w_fg[l], stacked_b=True, out_dtype=BF16, name=f"ffn_gate{l}")
        up = _mm(n_f, w_fu[l], stacked_b=True, out_dtype=BF16, name=f"ffn_up{l}")
        act = _swiglu_act(gt, up)
        if before_down is not None:
            before_down(act)
        h_f = _mm(act, w_fd[l], add=h_in, name=f"ffn_down{l}")
        n_p = _rmsnorm_fwd(h_f, g_ple[l], f"norm_ple{l}")
        gp = _mm(n_p, w_pg[l], name=f"ple_gate{l}")
        pp = _mm(p[l, 0], w_pp[l], stacked_b=True, name=f"ple_proj{l}")
        h_out = _ple_fwd(h_f, gp, pp)
        return h_out, dict(h_in=h_in, n_f=n_f, gt=gt, up=up, act=act, h_f=h_f, n_p=n_p, gp=gp, pp=pp)

    gather_finish(2, h_mix[0])
    take_ffn_in(0)

    def finish_ffn_out_ple0(act):
        gather_finish(3, act)
        take_ffn_out_ple(0)

    h_l0, sv0 = ffn_ple_fwd(h_mix[0], 0, finish_ffn_out_ple0)

    gather_finish(4, h_l0)
    w_b_in = whole("b_w_in", 0)
    w_bz, w_bx = w_b_in[:, :inner], w_b_in[:, inner:inner + xbcw]
    w_bd = pad_lanes(w_b_in[:, inner + xbcw:])
    u1 = _rmsnorm_fwd(h_l0, g_mix[1], "norm_mix1")
    z = _mm(u1, w_bz, name="ssd_in_z")
    xbc_pre = _mm(u1, w_bx, name="ssd_in_xbc")
    dt_pre = _mm(u1, w_bd, name="ssd_in_dt")
    xbc = _conv_fwd(xbc_pre, b_cw, b_cb, True, "ssd_conv")
    dt = _dt_fwd(dt_pre, dt_bias, n_heads, n_groups)
    ysc, s_in = _ssd_fwd(xbc, dt, a_log, inner, n_groups)
    yn = _ssd_gate_norm_fwd(ysc, xbc, z, dskip_e, b_ng, n_groups)
    gather_finish(5, yn)
    w_bo = whole("b_w_out", 0)
    h_mix[1] = _mm(yn, w_bo, add=h_l0, name="ssd_out")
    gather_finish(6, h_mix[1])
    take_ffn_in(1)
    take_ffn_out_ple(1)
    h_l1, sv1 = ffn_ple_fwd(h_mix[1], 1)

    dh, dg_fin, loss_row = _final_loss_bwd(h_l1, g_fin, loss_target[0])

    d_norm_ffn, d_norm_ple, d_norm_mix = [None] * depth, [None] * depth, [None] * depth
    for nm in ("ffn_w_gate", "ffn_w_up", "ffn_w_down", "ple_w_proj", "ple_w_gate"):
        grads[nm] = [None] * depth

    def stacked(nm, gfull):
        kind = kind_of[nm]
        if nm in kept_stacked:
            return gfull
        if kind == "row":
            return gfull.reshape(N_CHIPS, -1, gfull.shape[-1])
        if kind == "col":
            n_loc = gfull.shape[1] // N_CHIPS
            return jnp.stack([gfull[:, k * n_loc:(k + 1) * n_loc] for k in range(N_CHIPS)])
        heads, bw, _ = gfull.shape
        return gfull.reshape(heads, N_CHIPS, bw // N_CHIPS, bw).transpose(1, 0, 2, 3).reshape(N_CHIPS, -1, bw)

    reduce_started = []

    def reduce_start(keys, tag):
        gst = [stacked(nm, grads[nm][l]) for nm, l in keys]
        from_sib = _swap_halves(gst, f"reduce_swap_{tag}")
        return reduce_exchange(keys, tag, gst, from_sib)

    def reduce_exchange(keys, tag, gst, from_sib):
        chip_sum = [_add_own_half(g, r, c_idx, f"reduce_add_{nm}{l}") for g, r, (nm, l) in zip(gst, from_sib, keys)]
        send, recv, ents, lands, token = _scatter_start(chip_sum, f"reduce_scatter_start_{tag}")
        reduce_started.append((keys, tag, send, recv, ents, lands))
        return token[0:1, 0:1]

    def reduce_swap_begin(keys, tag):
        gst = [stacked(nm, grads[nm][l]) for nm, l in keys]
        send, recv, ents, lands, token = _swap_start(gst, f"reduce_swap_start_{tag}")
        return (keys, tag, send, recv, ents, lands), token[0:1, 0:1]

    def reduce_swap_end(state, after):
        keys, tag, send, recv, ents, lands = state
        gst, from_sib = _swap_wait(send, recv, ents, lands, after, f"reduce_swap_wait_{tag}")
        return reduce_exchange(keys, tag, gst, from_sib)

    def ffn_ple_keys(l):
        return [("ple_w_gate", l), ("ple_w_proj", l), ("ffn_w_down", l), ("ffn_w_gate", l), ("ffn_w_up", l)]

    def ffn_ple_bwd(dh_out, sv, l, g_ple_l, after_dact=None):
        dgp, dpp = _ple_bwd(dh_out, sv["gp"], sv["pp"])
        grads["ple_w_gate"][l] = _mm(sv["n_p"], dgp, ta=True, out_dtype=BF16, name=f"ple_gate_dw{l}")
        grads["ple_w_proj"][l] = _mm(p[l, 0], dpp, ta=True, out_dtype=BF16, stacked_out=N_CHIPS,
                                     name=f"ple_proj_dw{l}")
        dn = _mm(dgp, w_pg[l], tb=True, name=f"ple_gate_dx{l}")
        dh_f, d_norm_ple[l] = _rmsnorm_bwd(dn, sv["h_f"], g_ple_l, dh_out, f"norm_ple_bwd{l}")
        grads["ffn_w_down"][l] = _mm(sv["act"], dh_f, ta=True, out_dtype=BF16, name=f"ffn_down_dw{l}")
        dact = _mm(dh_f, w_fd[l], tb=True, name=f"ffn_down_dx{l}")
        g_ffn_l = g_ffn[l] if after_dact is None else g_ffn[l] + after_dact(dact)
        dgt, dup = _swiglu_bwd(dact, sv["gt"], sv["up"])
        grads["ffn_w_gate"][l] = _mm(sv["n_f"], dgt, ta=True, out_dtype=BF16, stacked_out=N_CHIPS,
                                     name=f"ffn_gate_dw{l}")
        grads["ffn_w_up"][l] = _mm(sv["n_f"], dup, ta=True, out_dtype=BF16, stacked_out=N_CHIPS, name=f"ffn_up_dw{l}")
        dn = _mm(dgt, w_fg[l], tb=True, stacked_b=True, name=f"ffn_gate_dx{l}")
        dn = _mm(dup, w_fu[l], tb=True, stacked_b=True, add=dn, name=f"ffn_up_dx{l}")
        dh_in, d_norm_ffn[l] = _rmsnorm_bwd(dn, sv["h_in"], g_ffn_l, dh_f, f"norm_ffn_bwd{l}")
        return dh_in

    dh = ffn_ple_bwd(dh, sv1, 1, g_ple[1])
    swapping, tok = reduce_swap_begin(ffn_ple_keys(1), "l1")

    grads["b_w_out"] = [_mm(yn, dh, ta=True, out_dtype=BF16, name="ssd_out_dw")]
    dyn = _mm(dh, w_bo, tb=True, name="ssd_out_dx")
    dy_ssd, dz, d_b_norm_g, dd_lane = _ssd_gate_norm_bwd(dyn, ysc, xbc, z, dskip_e, b_ng + tok, n_groups)
    tok = reduce_swap_end(swapping, dy_ssd)
    dxs, d_bm, d_cm, ddt, d_a_log = _ssd_bwd(xbc, dt, a_log, dy_ssd, ysc, s_in, dskip_e + tok, inner, n_groups)
    dxbc = _put_cols(dxs, d_bm, inner // gn, "ssd_put_db")
    dxbc = _put_cols(dxbc, d_cm, inner // gn + 1, "ssd_put_dc")
    dconv = _silu_conv_bwd_pre(dxbc, xbc_pre, b_cw, b_cb, "ssd_conv_bwd_pre")
    dxbc_pre, d_b_conv_w, d_b_conv_b = _conv_bwd(dconv, xbc_pre, b_cw, "ssd_conv_bwd")
    ddt_pre, d_dt_bias = _dt_bwd(ddt, dt_pre, dt_bias, n_heads, n_groups)
    d_a_log = d_a_log.reshape(n_groups, LANE)[:, :SSD_HEADS_PER_GROUP].reshape(1, n_heads)
    gw_bz = _mm(u1, dz, ta=True, out_dtype=BF16, name="ssd_in_z_dw")
    gw_bx = _mm(u1, dxbc_pre, ta=True, out_dtype=BF16, name="ssd_in_xbc_dw")
    gw_bd = _mm(u1, ddt_pre, ta=True, out_dtype=BF16, name="ssd_in_dt_dw")
    grads["b_w_in"] = [jnp.concatenate([gw_bz, gw_bx, gw_bd[:, :n_heads]], axis=1)]
    du = _mm(dz, w_bz, tb=True, name="ssd_in_z_dx")
    du = _mm(dxbc_pre, w_bx, tb=True, add=du, name="ssd_in_xbc_dx")
    du = _mm(ddt_pre, w_bd, tb=True, add=du, name="ssd_in_dt_dx")
    dh, d_norm_mix[1] = _rmsnorm_bwd(du, h_l0, g_mix[1], dh, "norm_mix_bwd1")
    swapping, tok = reduce_swap_begin([("b_w_out", 0), ("b_w_in", 0)], "ssd")

    dh = ffn_ple_bwd(dh, sv0, 0, g_ple[0] + tok, lambda dact: reduce_swap_end(swapping, dact))
    tok = reduce_start(ffn_ple_keys(0), "l0")

    grads["a_w_out"] = [_mm(y_lru, dh, ta=True, out_dtype=BF16, name="lru_out_dw")]
    dy_lru = _mm(dh, w_ao, tb=True, name="lru_out_dx")
    dhs, dyg = _lru_out_bwd(dy_lru, hs, yg)
    g_scan = _scan(lru_a, dhs, True, "lru_scan_bwd")
    dxr, d_wr, d_wi, d_br, d_bi, d_lam = _lru_gates_bwd(xr, g_scan, hs, w_ar, w_ai, a_br, a_bi, a_lambda + tok)
    dxr_pre, d_a_conv_w, d_a_conv_b = _conv_bwd(dxr, xr_pre, a_cw, "lru_conv_bwd")
    gw_ax = _mm(u0, dxr_pre, ta=True, out_dtype=BF16, stacked_out=N_CHIPS // 2, name="lru_in_x_dw")
    gw_ay = _mm(u0, dyg, ta=True, out_dtype=BF16, stacked_out=N_CHIPS // 2, name="lru_in_y_dw")
    grads["a_w_in"] = [jnp.concatenate([gw_ax, gw_ay], axis=0)]
    grads["a_w_gate_r"] = [d_wr.astype(BF16)]
    grads["a_w_gate_i"] = [d_wi.astype(BF16)]
    du = _mm(dxr_pre, w_ax, tb=True, stacked_b=True, name="lru_in_x_dx")
    du = _mm(dyg, w_ay, tb=True, stacked_b=True, add=du, name="lru_in_y_dx")
    grad_x, d_norm_mix[0] = _rmsnorm_bwd(du, h0, g_mix[0], dh, "norm_mix_bwd0")

    tok = reduce_start([("a_w_out", 0), ("a_w_in", 0), ("a_w_gate_r", 0), ("a_w_gate_i", 0)], "lru")
    grad_out, delta_out, m_out, v_out = {}, {}, {}, {}

    def reduce_finish(groups, after, tag):
        g_half = {}
        for keys, gtag, send, recv, ents, lands in groups:
            ents, lands = _scatter_wait(send, recv, ents, lands, after, f"reduce_scatter_wait_{gtag}")
            for rx, own, (nm, l) in zip(lands, ents, keys):
                sh = shard2d(nm, w_in[nm])
                g_half[nm] = _sum_into(g_half.get(nm), rx, own, l, qc_vec, (len(sh),) + sh[0].shape,
                                       f"reduce_sum_{nm}{l}")
        nms = list(g_half)
        seen = jnp.zeros((1, 1), F32)
        for nm, gfull in zip(nms, _join_halves([g_half[nm] for nm in nms], f"reduce_join_{tag}")):
            shape, cols = w_in[nm].shape, gfull.shape[-1]
            dl, mn, vn, gout = _adamw(w_in[nm].reshape(-1, cols), gfull.reshape(-1, cols),
                                      m_in[nm].reshape(-1, cols), v_in[nm].reshape(-1, cols), f"adamw_{nm}",
                                      emit_g=True)
            grad_out[nm], delta_out[nm] = gout.reshape(shape), dl.reshape(shape)
            m_out[nm], v_out[nm] = mn.reshape(shape), vn.reshape(shape)
            seen = seen + dl[0:1, 0:1]
        return seen

    by_tag = {grp[1]: grp for grp in reduce_started}
    done = reduce_finish([by_tag["ssd"]], grad_x[0:1, 0:1] + tok, "ssd")
    done = reduce_finish([by_tag["l1"], by_tag["l0"]], done, "ffn_ple")
    done = reduce_finish([by_tag["lru"]], done, "lru")

    small_full_grads = {
        "norm_mix_g": jnp.concatenate(d_norm_mix, axis=0), "norm_ffn_g": jnp.concatenate(d_norm_ffn, axis=0),
        "norm_ple_g": jnp.concatenate(d_norm_ple, axis=0), "final_norm_g": dg_fin[0],
        "a_conv_w": d_a_conv_w[None], "a_conv_b": d_a_conv_b,
        "a_b_gate_r": d_br.reshape(a_b_gate_r.shape[0], a_b_gate_r.shape[1], -1),
        "a_b_gate_i": d_bi.reshape(a_b_gate_i.shape[0], a_b_gate_i.shape[1], -1),
        "a_lambda": d_lam, "b_conv_w": d_b_conv_w[None], "b_conv_b": d_b_conv_b,
        "b_dt_bias": d_dt_bias[:, :n_heads], "b_a_log": d_a_log,
        "b_d_skip": dd_lane.reshape(1, n_heads, SSD_HEAD_DIM).sum(axis=-1), "b_norm_g": d_b_norm_g,
    }
    small_names = list(small_full_grads)
    small_shapes = [small_full_grads[nm].shape for nm in small_names]
    packed = _pack([loss_row] + [small_full_grads[nm] for nm in small_names])
    total = _sum_slots(_gather_all(packed, done, "gather_small_grads"), "sum_small_grads")
    parts = _unpack(total, [(1, LANE)] + small_shapes)
    loss = parts[0][0, 0]
    g_small = {}
    for nm, gfull in zip(small_names, parts[1:]):
        if nm in small_sharded:
            n_loc = w_in[nm].shape[-1]
            gfull = lax.dynamic_slice_in_dim(gfull, q_idx * n_loc, n_loc, axis=gfull.ndim - 1)
        g_small[nm] = gfull

    sm_shapes = [w_in[nm].shape for nm in small_names]
    dl, mn, vn = _adamw(_pack([w_in[nm] for nm in small_names]), _pack([g_small[nm] for nm in small_names]),
                        _pack([m_in[nm] for nm in small_names]), _pack([v_in[nm] for nm in small_names]),
                        "adamw_small")
    for nm, a, b_, c_ in zip(small_names, _unpack(dl, sm_shapes), _unpack(mn, sm_shapes), _unpack(vn, sm_shapes)):
        grad_out[nm] = g_small[nm].reshape(w_in[nm].shape)
        delta_out[nm], m_out[nm], v_out[nm] = a, b_, c_

    return (loss, grad_x[None], *[grad_out[nm] for nm in names], *[delta_out[nm] for nm in names],
            *[m_out[nm] for nm in names], *[v_out[nm] for nm in names])
```
